```python
import jax, jax.numpy as jnp
from jax import lax
import numpy as np

D_MODEL = 2048
BATCH = 8
SEQ = 4096
DEPTH = 1

POOL_WIDTH = D_MODEL
POOL_WINDOWS = (2, 4, 8, 16)
N_POOL_GROUPS = len(POOL_WINDOWS)
POOL_GROUP_WIDTH = POOL_WIDTH // N_POOL_GROUPS
LRU_WIDTH = D_MODEL
LRU_BLOCK = 256
N_LRU_HEADS = LRU_WIDTH // LRU_BLOCK
CONV_WIDTH = 4
LRU_C = 8.0
N_DIRS = 2
N_BRANCHES = 2
D_FF = 4 * D_MODEL
IN_WIDTH = POOL_WIDTH + 2 * LRU_WIDTH + N_BRANCHES * D_MODEL
DN_ALPHA = (2.0 * DEPTH) ** 0.25
DN_BETA = (8.0 * DEPTH) ** -0.25
LN_EPS = 1e-5

kernel_name = "hybrid_pool_rglru_encoder_block"


def layer_norm(x, g, b):
    xf = x.astype(jnp.float32)
    mu = jnp.mean(xf, axis=-1, keepdims=True)
    xc = xf - mu
    var = jnp.mean(xc * xc, axis=-1, keepdims=True)
    y = xc * lax.rsqrt(var + LN_EPS) * g.astype(jnp.float32) + b.astype(jnp.float32)
    return y.astype(x.dtype)


def multiscale_pool(u, pool_w, pool_scale):
    B, S, P = u.shape
    uf = u.astype(jnp.float32)
    csum = jnp.pad(jnp.cumsum(uf, axis=1), ((0, 0), (1, 0), (0, 0)))
    t = jnp.arange(S)
    outs = []
    for g, w in enumerate(POOL_WINDOWS):
        lo = jnp.clip(t - w // 2, 0, S)
        hi = jnp.clip(t + w // 2, 0, S)
        sl = slice(g * POOL_GROUP_WIDTH, (g + 1) * POOL_GROUP_WIDTH)
        c = csum[:, :, sl]
        mean = (c[:, hi] - c[:, lo]) / (hi - lo).astype(jnp.float32)[None, :, None]
        outs.append(mean - uf[:, :, sl])
    d = jnp.stack(outs, axis=2)
    y = jnp.einsum('bsgi,gio->bsgo', d, pool_w.astype(jnp.float32)).reshape(B, S, P)
    return (y * pool_scale.astype(jnp.float32)).astype(u.dtype)


def centred_depthwise_conv(u, w, b):
    S = u.shape[1]
    left = CONV_WIDTH // 2
    right = CONV_WIDTH - 1 - left
    up = jnp.pad(u, ((0, 0), (left, right), (0, 0)))
    y = b
    for k in range(CONV_WIDTH):
        y = y + up[:, k:k + S, :] * w[k]
    return y


def _lin_combine(p, q):
    a1, b1 = p
    a2, b2 = q
    return a1 * a2, a2 * b1 + b2


def rg_lru(xc, wa, ba, wx, bx, lam, reverse):
    B, S, R = xc.shape
    xf = xc.astype(jnp.float32)
    xh = xf.reshape(B, S, N_LRU_HEADS, LRU_BLOCK)
    r = jax.nn.sigmoid(jnp.einsum('bshi,hio->bsho', xh, wa.astype(jnp.float32)).reshape(B, S, R) + ba.astype(jnp.float32))
    i = jax.nn.sigmoid(jnp.einsum('bshi,hio->bsho', xh, wx.astype(jnp.float32)).reshape(B, S, R) + bx.astype(jnp.float32))
    log_a = -LRU_C * jax.nn.softplus(-lam.astype(jnp.float32)) * r
    a = jnp.exp(log_a)
    inp = jnp.sqrt(-jnp.expm1(2.0 * log_a)) * (i * xf)
    _, h = lax.associative_scan(_lin_combine, (a, inp), axis=1, reverse=reverse)
    return h


def hybrid_mixer(x, w_in, pool_w, pool_scale, conv_w, conv_b, lru_wa, lru_ba, lru_wx, lru_bx,
                 lru_lambda, w_pool_up, w_lru_up, w_out, b_out):
    B, S, D = x.shape
    z = jnp.einsum('bsd,de->bse', x, w_in)
    o1 = POOL_WIDTH
    o2 = o1 + LRU_WIDTH
    o3 = o2 + LRU_WIDTH
    u_pool, u_lru, u_gate, g_logits = z[..., :o1], z[..., o1:o2], z[..., o2:o3], z[..., o3:]
    y_pool = multiscale_pool(u_pool, pool_w, pool_scale)
    xc = centred_depthwise_conv(u_lru, conv_w, conv_b)
    h = (rg_lru(xc, lru_wa[0], lru_ba[0], lru_wx[0], lru_bx[0], lru_lambda[0], False)
         + rg_lru(xc, lru_wa[1], lru_ba[1], lru_wx[1], lru_bx[1], lru_lambda[1], True))
    y_lru = h.astype(x.dtype) * jax.nn.gelu(u_gate)
    g = jax.nn.sigmoid(g_logits.astype(jnp.float32)).astype(x.dtype).reshape(B, S, N_BRANCHES, D)
    m = (g[:, :, 0] * jnp.einsum('bsp,pd->bsd', y_pool, w_pool_up)
         + g[:, :, 1] * jnp.einsum('bsr,rd->bsd', y_lru, w_lru_up))
    return jnp.einsum('bsd,de->bse', m, w_out) + b_out


def sq_relu_mlp(x, w1, b1, w2, b2):
    hdn = jnp.square(jax.nn.relu(jnp.einsum('bsd,df->bsf', x, w1) + b1))
    return jnp.einsum('bsf,fd->bsd', hdn, w2) + b2


def _fwd_setup_inputs(seed: int = 0) -> dict:
    key = jax.random.key(seed)
    ks = jax.random.split(key, 24)
    f32 = jnp.float32
    L, D, P, R = DEPTH, D_MODEL, POOL_WIDTH, LRU_WIDTH
    nrm = lambda k, shape, s: jax.random.normal(k, shape, f32) * s
    a_base = jax.random.uniform(ks[10], (L, N_DIRS, R), f32, minval=0.9, maxval=0.999)
    s = a_base ** (1.0 / LRU_C)
    lam = jnp.log(s) - jnp.log1p(-s)
    return {
        "x": nrm(ks[0], (BATCH, SEQ, D), 1.0),
        "w_in": nrm(ks[1], (L, D, IN_WIDTH), D ** -0.5),
        "pool_w": nrm(ks[2], (L, N_POOL_GROUPS, POOL_GROUP_WIDTH, POOL_GROUP_WIDTH), POOL_GROUP_WIDTH ** -0.5),
        "pool_scale": 1.0 + nrm(ks[3], (L, P), 0.1),
        "conv_w": nrm(ks[4], (L, CONV_WIDTH, R), CONV_WIDTH ** -0.5),
        "conv_b": nrm(ks[5], (L, R), 0.01),
        "lru_wa": nrm(ks[6], (L, N_DIRS, N_LRU_HEADS, LRU_BLOCK, LRU_BLOCK), LRU_BLOCK ** -0.5),
        "lru_ba": nrm(ks[7], (L, N_DIRS, R), 0.01),
        "lru_wx": nrm(ks[8], (L, N_DIRS, N_LRU_HEADS, LRU_BLOCK, LRU_BLOCK), LRU_BLOCK ** -0.5),
        "lru_bx": nrm(ks[9], (L, N_DIRS, R), 0.01),
        "lru_lambda": lam,
        "w_pool_up": nrm(ks[11], (L, P, D), DN_BETA * P ** -0.5),
        "w_lru_up": nrm(ks[12], (L, R, D), DN_BETA * R ** -0.5),
        "w_out": nrm(ks[13], (L, D, D), DN_BETA * D ** -0.5),
        "b_out": nrm(ks[14], (L, D), 0.01),
        "ln1_g": 1.0 + nrm(ks[15], (L, D), 0.1),
        "ln1_b": nrm(ks[16], (L, D), 0.01),
        "w_ff1": nrm(ks[17], (L, D, D_FF), D ** -0.5),
        "b_ff1": nrm(ks[18], (L, D_FF), 0.01),
        "w_ff2": nrm(ks[19], (L, D_FF, D), DN_BETA * D_FF ** -0.5),
        "b_ff2": nrm(ks[20], (L, D), 0.01),
        "ln2_g": 1.0 + nrm(ks[21], (L, D), 0.1),
        "ln2_b": nrm(ks[22], (L, D), 0.01),
    }


def _fwd_reference(x, w_in, pool_w, pool_scale, conv_w, conv_b, lru_wa, lru_ba, lru_wx, lru_bx,
              lru_lambda, w_pool_up, w_lru_up, w_out, b_out, ln1_g, ln1_b,
              w_ff1, b_ff1, w_ff2, b_ff2, ln2_g, ln2_b):
    for l in range(DEPTH):
        mix = hybrid_mixer(x, w_in[l], pool_w[l], pool_scale[l], conv_w[l], conv_b[l],
                           lru_wa[l], lru_ba[l], lru_wx[l], lru_bx[l], lru_lambda[l],
                           w_pool_up[l], w_lru_up[l], w_out[l], b_out[l])
        x = layer_norm(DN_ALPHA * x + mix, ln1_g[l], ln1_b[l])
        ff = sq_relu_mlp(x, w_ff1[l], b_ff1[l], w_ff2[l], b_ff2[l])
        x = layer_norm(DN_ALPHA * x + ff, ln2_g[l], ln2_b[l])
    return x


import jax as _jax
import jax.numpy as _jnp

TWIN_FORMAT = 'train_step'
FWD_PARAMS = ['x', 'w_in', 'pool_w', 'pool_scale', 'conv_w', 'conv_b', 'lru_wa', 'lru_ba', 'lru_wx', 'lru_bx', 'lru_lambda', 'w_pool_up', 'w_lru_up', 'w_out', 'b_out', 'ln1_g', 'ln1_b', 'w_ff1', 'b_ff1', 'w_ff2', 'b_ff2', 'ln2_g', 'ln2_b']
TWIN_WEIGHTS = ['w_in', 'pool_w', 'pool_scale', 'conv_w', 'conv_b', 'lru_wa', 'lru_ba', 'lru_wx', 'lru_bx', 'lru_lambda', 'w_pool_up', 'w_lru_up', 'w_out', 'b_out', 'ln1_g', 'ln1_b', 'w_ff1', 'b_ff1', 'w_ff2', 'b_ff2', 'ln2_g', 'ln2_b']
TWIN_DIFF_INPUT = 'x'
TWIN_INPUTS = ['x', 'w_in', 'pool_w', 'pool_scale', 'conv_w', 'conv_b', 'lru_wa', 'lru_ba', 'lru_wx', 'lru_bx', 'lru_lambda', 'w_pool_up', 'w_lru_up', 'w_out', 'b_out', 'ln1_g', 'ln1_b', 'w_ff1', 'b_ff1', 'w_ff2', 'b_ff2', 'ln2_g', 'ln2_b', 'loss_target', 'm_w_in', 'm_pool_w', 'm_pool_scale', 'm_conv_w', 'm_conv_b', 'm_lru_wa', 'm_lru_ba', 'm_lru_wx', 'm_lru_bx', 'm_lru_lambda', 'm_w_pool_up', 'm_w_lru_up', 'm_w_out', 'm_b_out', 'm_ln1_g', 'm_ln1_b', 'm_w_ff1', 'm_b_ff1', 'm_w_ff2', 'm_b_ff2', 'm_ln2_g', 'm_ln2_b', 'v_w_in', 'v_pool_w', 'v_pool_scale', 'v_conv_w', 'v_conv_b', 'v_lru_wa', 'v_lru_ba', 'v_lru_wx', 'v_lru_bx', 'v_lru_lambda', 'v_w_pool_up', 'v_w_lru_up', 'v_w_out', 'v_b_out', 'v_ln1_g', 'v_ln1_b', 'v_w_ff1', 'v_b_ff1', 'v_w_ff2', 'v_b_ff2', 'v_ln2_g', 'v_ln2_b']
TWIN_OUTPUTS = ['loss', 'grad_x', 'grad_w_in', 'grad_pool_w', 'grad_pool_scale', 'grad_conv_w', 'grad_conv_b', 'grad_lru_wa', 'grad_lru_ba', 'grad_lru_wx', 'grad_lru_bx', 'grad_lru_lambda', 'grad_w_pool_up', 'grad_w_lru_up', 'grad_w_out', 'grad_b_out', 'grad_ln1_g', 'grad_ln1_b', 'grad_w_ff1', 'grad_b_ff1', 'grad_w_ff2', 'grad_b_ff2', 'grad_ln2_g', 'grad_ln2_b', 'delta_w_in', 'delta_pool_w', 'delta_pool_scale', 'delta_conv_w', 'delta_conv_b', 'delta_lru_wa', 'delta_lru_ba', 'delta_lru_wx', 'delta_lru_bx', 'delta_lru_lambda', 'delta_w_pool_up', 'delta_w_lru_up', 'delta_w_out', 'delta_b_out', 'delta_ln1_g', 'delta_ln1_b', 'delta_w_ff1', 'delta_b_ff1', 'delta_w_ff2', 'delta_b_ff2', 'delta_ln2_g', 'delta_ln2_b', 'new_m_w_in', 'new_m_pool_w', 'new_m_pool_scale', 'new_m_conv_w', 'new_m_conv_b', 'new_m_lru_wa', 'new_m_lru_ba', 'new_m_lru_wx', 'new_m_lru_bx', 'new_m_lru_lambda', 'new_m_w_pool_up', 'new_m_w_lru_up', 'new_m_w_out', 'new_m_b_out', 'new_m_ln1_g', 'new_m_ln1_b', 'new_m_w_ff1', 'new_m_b_ff1', 'new_m_w_ff2', 'new_m_b_ff2', 'new_m_ln2_g', 'new_m_ln2_b', 'new_v_w_in', 'new_v_pool_w', 'new_v_pool_scale', 'new_v_conv_w', 'new_v_conv_b', 'new_v_lru_wa', 'new_v_lru_ba', 'new_v_lru_wx', 'new_v_lru_bx', 'new_v_lru_lambda', 'new_v_w_pool_up', 'new_v_w_lru_up', 'new_v_w_out', 'new_v_b_out', 'new_v_ln1_g', 'new_v_ln1_b', 'new_v_w_ff1', 'new_v_b_ff1', 'new_v_w_ff2', 'new_v_b_ff2', 'new_v_ln2_g', 'new_v_ln2_b']
TWIN_LEAF_KINDS = {'loss': 'loss', 'grad_x': 'grad_x', 'grad_w_in': 'grad_w', 'grad_pool_w': 'grad_w', 'grad_pool_scale': 'grad_w', 'grad_conv_w': 'grad_w', 'grad_conv_b': 'grad_w', 'grad_lru_wa': 'grad_w', 'grad_lru_ba': 'grad_w', 'grad_lru_wx': 'grad_w', 'grad_lru_bx': 'grad_w', 'grad_lru_lambda': 'grad_w', 'grad_w_pool_up': 'grad_w', 'grad_w_lru_up': 'grad_w', 'grad_w_out': 'grad_w', 'grad_b_out': 'grad_w', 'grad_ln1_g': 'grad_w', 'grad_ln1_b': 'grad_w', 'grad_w_ff1': 'grad_w', 'grad_b_ff1': 'grad_w', 'grad_w_ff2': 'grad_w', 'grad_b_ff2': 'grad_w', 'grad_ln2_g': 'grad_w', 'grad_ln2_b': 'grad_w', 'delta_w_in': 'delta_w', 'delta_pool_w': 'delta_w', 'delta_pool_scale': 'delta_w', 'delta_conv_w': 'delta_w', 'delta_conv_b': 'delta_w', 'delta_lru_wa': 'delta_w', 'delta_lru_ba': 'delta_w', 'delta_lru_wx': 'delta_w', 'delta_lru_bx': 'delta_w', 'delta_lru_lambda': 'delta_w', 'delta_w_pool_up': 'delta_w', 'delta_w_lru_up': 'delta_w', 'delta_w_out': 'delta_w', 'delta_b_out': 'delta_w', 'delta_ln1_g': 'delta_w', 'delta_ln1_b': 'delta_w', 'delta_w_ff1': 'delta_w', 'delta_b_ff1': 'delta_w', 'delta_w_ff2': 'delta_w', 'delta_b_ff2': 'delta_w', 'delta_ln2_g': 'delta_w', 'delta_ln2_b': 'delta_w', 'new_m_w_in': 'new_m', 'new_m_pool_w': 'new_m', 'new_m_pool_scale': 'new_m', 'new_m_conv_w': 'new_m', 'new_m_conv_b': 'new_m', 'new_m_lru_wa': 'new_m', 'new_m_lru_ba': 'new_m', 'new_m_lru_wx': 'new_m', 'new_m_lru_bx': 'new_m', 'new_m_lru_lambda': 'new_m', 'new_m_w_pool_up': 'new_m', 'new_m_w_lru_up': 'new_m', 'new_m_w_out': 'new_m', 'new_m_b_out': 'new_m', 'new_m_ln1_g': 'new_m', 'new_m_ln1_b': 'new_m', 'new_m_w_ff1': 'new_m', 'new_m_b_ff1': 'new_m', 'new_m_w_ff2': 'new_m', 'new_m_b_ff2': 'new_m', 'new_m_ln2_g': 'new_m', 'new_m_ln2_b': 'new_m', 'new_v_w_in': 'new_v', 'new_v_pool_w': 'new_v', 'new_v_pool_scale': 'new_v', 'new_v_conv_w': 'new_v', 'new_v_conv_b': 'new_v', 'new_v_lru_wa': 'new_v', 'new_v_lru_ba': 'new_v', 'new_v_lru_wx': 'new_v', 'new_v_lru_bx': 'new_v', 'new_v_lru_lambda': 'new_v', 'new_v_w_pool_up': 'new_v', 'new_v_w_lru_up': 'new_v', 'new_v_w_out': 'new_v', 'new_v_b_out': 'new_v', 'new_v_ln1_g': 'new_v', 'new_v_ln1_b': 'new_v', 'new_v_w_ff1': 'new_v', 'new_v_b_ff1': 'new_v', 'new_v_w_ff2': 'new_v', 'new_v_b_ff2': 'new_v', 'new_v_ln2_g': 'new_v', 'new_v_ln2_b': 'new_v'}


def _forward(args):
    return _fwd_reference(*[args[k] for k in FWD_PARAMS])


def _output_shape():
    def fwd():
        inp = _fwd_setup_inputs(0)
        return _fwd_reference(*[inp[k] for k in FWD_PARAMS])
    out = _jax.eval_shape(fwd)
    return out.shape, out.dtype

N_MICROBATCH = 1
ADAM_LR = 0.001
ADAM_B1 = 0.9
ADAM_B2 = 0.999
ADAM_EPS = 1e-08
ADAM_WD = 0.01
ADAM_STEP = 10
PER_EXAMPLE_BATCH_AXIS = {'x': 0, 'loss_target': 0}
SHARED_INPUTS = []
_WEIGHT_DTYPES = {'w_in': _jnp.float32, 'pool_w': _jnp.float32, 'pool_scale': _jnp.float32, 'conv_w': _jnp.float32, 'conv_b': _jnp.float32, 'lru_wa': _jnp.float32, 'lru_ba': _jnp.float32, 'lru_wx': _jnp.float32, 'lru_bx': _jnp.float32, 'lru_lambda': _jnp.float32, 'w_pool_up': _jnp.float32, 'w_lru_up': _jnp.float32, 'w_out': _jnp.float32, 'b_out': _jnp.float32, 'ln1_g': _jnp.float32, 'ln1_b': _jnp.float32, 'w_ff1': _jnp.float32, 'b_ff1': _jnp.float32, 'w_ff2': _jnp.float32, 'b_ff2': _jnp.float32, 'ln2_g': _jnp.float32, 'ln2_b': _jnp.float32}
MOMENT_SCALE = {'w_in': 8.894716e-03, 'pool_w': 1.411993e-02, 'pool_scale': 1.390051e-02, 'conv_w': 1.028834e-02, 'conv_b': 2.836534e-01, 'lru_wa': 2.387600e-03, 'lru_ba': 1.865113e-03, 'lru_wx': 4.355637e-03, 'lru_bx': 2.421241e-03, 'lru_lambda': 3.416552e-03, 'w_pool_up': 2.372298e-02, 'w_lru_up': 1.754624e-02, 'w_out': 2.949899e-02, 'b_out': 4.952807e-01, 'ln1_g': 2.349113e+00, 'ln1_b': 6.018041e-01, 'w_ff1': 2.867714e-02, 'b_ff1': 1.081179e-01, 'w_ff2': 2.222967e-01, 'b_ff2': 4.755494e-01, 'ln2_g': 1.630271e+01, 'ln2_b': 3.382898e+00}


def _to_microbatches(a, axis):
    t = _jnp.moveaxis(a, axis, 0)
    t = t.reshape((N_MICROBATCH, t.shape[0] // N_MICROBATCH) + t.shape[1:])
    return _jnp.moveaxis(t, 1, axis + 1)


def setup_inputs(seed: int = 0) -> dict:
    inp = _fwd_setup_inputs(seed)
    key = _jax.random.fold_in(_jax.random.key(seed), 7919)
    shape, _ = _output_shape()
    out = dict(inp)
    out["loss_target"] = _jax.random.normal(_jax.random.fold_in(key, 0), shape, _jnp.float32)
    for i, name in enumerate(TWIN_WEIGHTS):
        w = inp[name].astype(_jnp.float32)
        if MOMENT_SCALE is None:
            s = _jnp.sqrt(_jnp.mean(_jnp.square(w)) + 1e-30)
        else:
            s = MOMENT_SCALE[name]
        km, kv = _jax.random.split(_jax.random.fold_in(key, i + 1))
        out[name] = w
        out["m_" + name] = s * _jax.random.normal(km, w.shape, _jnp.float32)
        out["v_" + name] = (s * s) * _jax.random.uniform(kv, w.shape, _jnp.float32, 0.5, 1.5)
    if N_MICROBATCH > 1:
        for name, axis in PER_EXAMPLE_BATCH_AXIS.items():
            out[name] = _to_microbatches(out[name], axis)
    return {'x': out['x'], 'w_in': out['w_in'], 'pool_w': out['pool_w'], 'pool_scale': out['pool_scale'], 'conv_w': out['conv_w'], 'conv_b': out['conv_b'], 'lru_wa': out['lru_wa'], 'lru_ba': out['lru_ba'], 'lru_wx': out['lru_wx'], 'lru_bx': out['lru_bx'], 'lru_lambda': out['lru_lambda'], 'w_pool_up': out['w_pool_up'], 'w_lru_up': out['w_lru_up'], 'w_out': out['w_out'], 'b_out': out['b_out'], 'ln1_g': out['ln1_g'], 'ln1_b': out['ln1_b'], 'w_ff1': out['w_ff1'], 'b_ff1': out['b_ff1'], 'w_ff2': out['w_ff2'], 'b_ff2': out['b_ff2'], 'ln2_g': out['ln2_g'], 'ln2_b': out['ln2_b'], 'loss_target': out['loss_target'], 'm_w_in': out['m_w_in'], 'm_pool_w': out['m_pool_w'], 'm_pool_scale': out['m_pool_scale'], 'm_conv_w': out['m_conv_w'], 'm_conv_b': out['m_conv_b'], 'm_lru_wa': out['m_lru_wa'], 'm_lru_ba': out['m_lru_ba'], 'm_lru_wx': out['m_lru_wx'], 'm_lru_bx': out['m_lru_bx'], 'm_lru_lambda': out['m_lru_lambda'], 'm_w_pool_up': out['m_w_pool_up'], 'm_w_lru_up': out['m_w_lru_up'], 'm_w_out': out['m_w_out'], 'm_b_out': out['m_b_out'], 'm_ln1_g': out['m_ln1_g'], 'm_ln1_b': out['m_ln1_b'], 'm_w_ff1': out['m_w_ff1'], 'm_b_ff1': out['m_b_ff1'], 'm_w_ff2': out['m_w_ff2'], 'm_b_ff2': out['m_b_ff2'], 'm_ln2_g': out['m_ln2_g'], 'm_ln2_b': out['m_ln2_b'], 'v_w_in': out['v_w_in'], 'v_pool_w': out['v_pool_w'], 'v_pool_scale': out['v_pool_scale'], 'v_conv_w': out['v_conv_w'], 'v_conv_b': out['v_conv_b'], 'v_lru_wa': out['v_lru_wa'], 'v_lru_ba': out['v_lru_ba'], 'v_lru_wx': out['v_lru_wx'], 'v_lru_bx': out['v_lru_bx'], 'v_lru_lambda': out['v_lru_lambda'], 'v_w_pool_up': out['v_w_pool_up'], 'v_w_lru_up': out['v_w_lru_up'], 'v_w_out': out['v_w_out'], 'v_b_out': out['v_b_out'], 'v_ln1_g': out['v_ln1_g'], 'v_ln1_b': out['v_ln1_b'], 'v_w_ff1': out['v_w_ff1'], 'v_b_ff1': out['v_b_ff1'], 'v_w_ff2': out['v_w_ff2'], 'v_b_ff2': out['v_b_ff2'], 'v_ln2_g': out['v_ln2_g'], 'v_ln2_b': out['v_ln2_b']}


def _loss(weights, diff, rest, loss_target):
    with _jax.named_scope("forward"):
        args = {**rest, TWIN_DIFF_INPUT: diff, **{k: w.astype(_WEIGHT_DTYPES[k]) for k, w in weights.items()}}
        y = _forward(args)
    with _jax.named_scope("loss_head"):
        err = _jnp.square(y.astype(_jnp.float32) - loss_target)
        return 0.5 * _jnp.sum(_jnp.mean(err, axis=-1)) if err.ndim else 0.5 * err


def _adamw(w, g, m, v):
    m = ADAM_B1 * m + (1.0 - ADAM_B1) * g
    v = ADAM_B2 * v + (1.0 - ADAM_B2) * _jnp.square(g)
    m_hat = m / (1.0 - ADAM_B1 ** ADAM_STEP)
    v_hat = v / (1.0 - ADAM_B2 ** ADAM_STEP)
    delta = -ADAM_LR * (m_hat / (_jnp.sqrt(v_hat) + ADAM_EPS) + ADAM_WD * w)
    return delta, m, v


def reference(x, w_in, pool_w, pool_scale, conv_w, conv_b, lru_wa, lru_ba, lru_wx, lru_bx, lru_lambda, w_pool_up, w_lru_up, w_out, b_out, ln1_g, ln1_b, w_ff1, b_ff1, w_ff2, b_ff2, ln2_g, ln2_b, loss_target, m_w_in, m_pool_w, m_pool_scale, m_conv_w, m_conv_b, m_lru_wa, m_lru_ba, m_lru_wx, m_lru_bx, m_lru_lambda, m_w_pool_up, m_w_lru_up, m_w_out, m_b_out, m_ln1_g, m_ln1_b, m_w_ff1, m_b_ff1, m_w_ff2, m_b_ff2, m_ln2_g, m_ln2_b, v_w_in, v_pool_w, v_pool_scale, v_conv_w, v_conv_b, v_lru_wa, v_lru_ba, v_lru_wx, v_lru_bx, v_lru_lambda, v_w_pool_up, v_w_lru_up, v_w_out, v_b_out, v_ln1_g, v_ln1_b, v_w_ff1, v_b_ff1, v_w_ff2, v_b_ff2, v_ln2_g, v_ln2_b):
    given = dict(x=x, w_in=w_in, pool_w=pool_w, pool_scale=pool_scale, conv_w=conv_w, conv_b=conv_b, lru_wa=lru_wa, lru_ba=lru_ba, lru_wx=lru_wx, lru_bx=lru_bx, lru_lambda=lru_lambda, w_pool_up=w_pool_up, w_lru_up=w_lru_up, w_out=w_out, b_out=b_out, ln1_g=ln1_g, ln1_b=ln1_b, w_ff1=w_ff1, b_ff1=b_ff1, w_ff2=w_ff2, b_ff2=b_ff2, ln2_g=ln2_g, ln2_b=ln2_b, loss_target=loss_target, m_w_in=m_w_in, m_pool_w=m_pool_w, m_pool_scale=m_pool_scale, m_conv_w=m_conv_w, m_conv_b=m_conv_b, m_lru_wa=m_lru_wa, m_lru_ba=m_lru_ba, m_lru_wx=m_lru_wx, m_lru_bx=m_lru_bx, m_lru_lambda=m_lru_lambda, m_w_pool_up=m_w_pool_up, m_w_lru_up=m_w_lru_up, m_w_out=m_w_out, m_b_out=m_b_out, m_ln1_g=m_ln1_g, m_ln1_b=m_ln1_b, m_w_ff1=m_w_ff1, m_b_ff1=m_b_ff1, m_w_ff2=m_w_ff2, m_b_ff2=m_b_ff2, m_ln2_g=m_ln2_g, m_ln2_b=m_ln2_b, v_w_in=v_w_in, v_pool_w=v_pool_w, v_pool_scale=v_pool_scale, v_conv_w=v_conv_w, v_conv_b=v_conv_b, v_lru_wa=v_lru_wa, v_lru_ba=v_lru_ba, v_lru_wx=v_lru_wx, v_lru_bx=v_lru_bx, v_lru_lambda=v_lru_lambda, v_w_pool_up=v_w_pool_up, v_w_lru_up=v_w_lru_up, v_w_out=v_w_out, v_b_out=v_b_out, v_ln1_g=v_ln1_g, v_ln1_b=v_ln1_b, v_w_ff1=v_w_ff1, v_b_ff1=v_b_ff1, v_w_ff2=v_w_ff2, v_b_ff2=v_b_ff2, v_ln2_g=v_ln2_g, v_ln2_b=v_ln2_b)
    weights = {n: given[n] for n in TWIN_WEIGHTS}
    shared = {n: given[n] for n in SHARED_INPUTS}
    per_example = {n: given[n] for n in ['x']}
    grad_fn = _jax.value_and_grad(_loss, argnums=(0, 1))

    def one_microbatch(ex, loss_target):
        ex = dict(ex)
        diff = ex.pop(TWIN_DIFF_INPUT)
        return grad_fn(weights, diff, {**shared, **ex}, loss_target)

    if N_MICROBATCH == 1:
        loss, (grad_w, grad_x) = one_microbatch(per_example, given["loss_target"])
    else:
        def body(carry, xs):
            loss_sum, grad_sum = carry
            l_k, (gw_k, gx_k) = one_microbatch(xs[0], xs[1])
            with _jax.named_scope("update"):
                return (loss_sum + l_k, _jax.tree.map(_jnp.add, grad_sum, gw_k)), gx_k

        init = (_jnp.zeros((), _jnp.float32), _jax.tree.map(_jnp.zeros_like, weights))
        (loss, grad_w), grad_x = _jax.lax.scan(body, init, (per_example, given["loss_target"]))
    with _jax.named_scope("update"):
        delta_w, new_m, new_v = {}, {}, {}
        for n in TWIN_WEIGHTS:
            delta_w[n], new_m[n], new_v[n] = _adamw(weights[n], grad_w[n], given["m_" + n], given["v_" + n])
    return (loss, grad_x, *[grad_w[n] for n in TWIN_WEIGHTS], *[delta_w[n] for n in TWIN_WEIGHTS],
            *[new_m[n] for n in TWIN_WEIGHTS], *[new_v[n] for n in TWIN_WEIGHTS])
```

```python
import functools
import math

import jax
import jax.numpy as jnp
from jax import lax
from jax.experimental import pallas as pl
from jax.experimental.pallas import tpu as pltpu

F32 = jnp.float32
BF16 = jnp.bfloat16
MESH = pl.DeviceIdType.MESH
ANY = pl.BlockSpec(memory_space=pl.ANY)

N_CHIP = 4
N_DEV = 8
VMEM_LIMIT_BYTES = 56 * 1024 * 1024
SUBLANES = 8
PAD = 8

POOL_WINDOWS = (2, 4, 8, 16)
LRU_C = 8.0
DN_ALPHA = 2.0 ** 0.25
LN_EPS = 1e-5
ADAM_LR, ADAM_B1, ADAM_B2, ADAM_EPS, ADAM_WD, ADAM_STEP = 0.001, 0.9, 0.999, 1e-08, 0.01, 10

WEIGHTS = ("w_in", "pool_w", "pool_scale", "conv_w", "conv_b", "lru_wa", "lru_ba", "lru_wx", "lru_bx", "lru_lambda",
           "w_pool_up", "w_lru_up", "w_out", "b_out", "ln1_g", "ln1_b", "w_ff1", "b_ff1", "w_ff2", "b_ff2", "ln2_g", "ln2_b")


def _cparams(sem=None):
    return pltpu.CompilerParams(dimension_semantics=sem, vmem_limit_bytes=VMEM_LIMIT_BYTES)


def _tile(dim, pref, unit=128):
    if dim <= pref:
        return dim
    t = (pref // unit) * unit
    while t > unit and dim % t:
        t -= unit
    assert dim % t == 0, (dim, pref)
    return t


def _mesh_pos():
    x, y, c = lax.axis_index("x"), lax.axis_index("y"), lax.axis_index("c")
    return x, y, c


def _other_chips(x, y):
    return [(1 - x, y), (x, 1 - y), (1 - x, 1 - y)]


def _all_gather_small(v):
    m_per, n = v.shape

    def body(x_ref, out_ref, send_sems, recv_sems, local_sem):
        x, y, c = _mesh_pos()
        me, sibling = (x, y, c), (x, y, 1 - c)
        chips = _other_chips(x, y)

        def rows(px, py, pc):
            return out_ref.at[4 * px + 2 * py + pc]

        def copy(k, block, to, src=None):
            return pltpu.make_async_remote_copy(
                src_ref=rows(*block) if src is None else src, dst_ref=rows(*block),
                send_sem=send_sems.at[k], recv_sem=recv_sems.at[k], device_id=to, device_id_type=MESH)

        mine = pltpu.make_async_copy(x_ref, rows(*me), local_sem)
        mine.start()
        first = [copy(0, me, sibling, src=x_ref)]
        first += [copy(1 + j, me, (*chip, c), src=x_ref) for j, chip in enumerate(chips)]
        for cp in first:
            cp.start()
        passed = [copy(4 + j, (*chip, c), sibling) for j, chip in enumerate(chips)]
        for j, chip in enumerate(chips):
            copy(1 + j, (*chip, c), me).wait_recv()
            passed[j].start()
        copy(0, sibling, me).wait_recv()
        for j, chip in enumerate(chips):
            copy(4 + j, (*chip, 1 - c), me).wait_recv()
        for cp in first + passed:
            cp.wait_send()
        mine.wait()

    return pl.pallas_call(
        body, name="all_gather_small",
        out_shape=jax.ShapeDtypeStruct((N_DEV, m_per, n), v.dtype),
        in_specs=[pl.BlockSpec(memory_space=pltpu.VMEM)],
        out_specs=pl.BlockSpec(memory_space=pltpu.VMEM),
        scratch_shapes=[pltpu.SemaphoreType.DMA((7,)), pltpu.SemaphoreType.DMA((7,)), pltpu.SemaphoreType.DMA],
    )(v)


def _chip_all_gather(ts):
    n = len(ts)

    def body(*refs):
        ins, outs = refs[:n], refs[n:2 * n]
        send_sems, recv_sems, local_sems = refs[2 * n:]
        x, y, c = _mesh_pos()
        me, sibling = (x, y, c), (x, y, 1 - c)
        k_me = 2 * x + y
        chips = _other_chips(x, y)

        def half(t, which):
            rh = ts[t].shape[0] // 2
            return pl.ds(which * rh, rh)

        def copy(t, s, kk, which, to, src=None):
            dst = outs[t].at[kk, half(t, which)]
            return pltpu.make_async_remote_copy(
                src_ref=dst if src is None else src, dst_ref=dst,
                send_sem=send_sems.at[t, s], recv_sem=recv_sems.at[t, s], device_id=to, device_id_type=MESH)

        local = [pltpu.make_async_copy(ins[t], outs[t].at[k_me], local_sems.at[t]) for t in range(n)]
        for cp in local:
            cp.start()
        sent = []
        for t in range(n):
            for j, chip in enumerate(chips):
                cp = copy(t, j, k_me, c, (*chip, c), src=ins[t].at[half(t, c)])
                cp.start()
                sent.append(cp)
        for t in range(n):
            for j, chip in enumerate(chips):
                kk = 2 * chip[0] + chip[1]
                copy(t, j, kk, c, me).wait_recv()
                cp = copy(t, 3 + j, kk, c, sibling)
                cp.start()
                sent.append(cp)
        for t in range(n):
            for j, chip in enumerate(chips):
                copy(t, 3 + j, 2 * chip[0] + chip[1], 1 - c, me).wait_recv()
        for cp in sent:
            cp.wait_send()
        for cp in local:
            cp.wait()

    return pl.pallas_call(
        body, name="chip_all_gather",
        out_shape=[jax.ShapeDtypeStruct((N_CHIP,) + t.shape, t.dtype) for t in ts],
        in_specs=[ANY] * n, out_specs=[ANY] * n,
        scratch_shapes=[pltpu.SemaphoreType.DMA((n, 6)), pltpu.SemaphoreType.DMA((n, 6)), pltpu.SemaphoreType.DMA((n,))],
    )(*ts)


def _sibling_swap_halves(gs):
    n = len(gs)

    def body(*refs):
        ins, outs = refs[:n], refs[n:2 * n]
        send_sems, recv_sems = refs[2 * n:]
        x, y, c = _mesh_pos()
        cps = []
        for t in range(n):
            rh = gs[t].shape[1] // 2
            cp = pltpu.make_async_remote_copy(
                src_ref=ins[t].at[:, pl.ds((1 - c) * rh, rh)], dst_ref=outs[t],
                send_sem=send_sems.at[t], recv_sem=recv_sems.at[t], device_id=(x, y, 1 - c), device_id_type=MESH)
            cp.start()
            cps.append(cp)
        for cp in cps:
            cp.wait()

    return pl.pallas_call(
        body, name="sibling_swap_halves",
        out_shape=[jax.ShapeDtypeStruct((g.shape[0], g.shape[1] // 2, g.shape[2]), g.dtype) for g in gs],
        in_specs=[ANY] * n, out_specs=[ANY] * n,
        scratch_shapes=[pltpu.SemaphoreType.DMA((n,)), pltpu.SemaphoreType.DMA((n,))],
    )(*gs)


def _chip_scatter(ps):
    n = len(ps)

    def body(*refs):
        ins, outs = refs[:n], refs[n:2 * n]
        send_sems, recv_sems = refs[2 * n:]
        x, y, c = _mesh_pos()
        cps = []
        for t in range(n):
            for j, chip in enumerate(_other_chips(x, y)):
                cp = pltpu.make_async_remote_copy(
                    src_ref=ins[t].at[2 * chip[0] + chip[1]], dst_ref=outs[t].at[j],
                    send_sem=send_sems.at[t, j], recv_sem=recv_sems.at[t, j], device_id=(*chip, c), device_id_type=MESH)
                cp.start()
                cps.append(cp)
        for cp in cps:
            cp.wait()

    return pl.pallas_call(
        body, name="chip_scatter",
        out_shape=[jax.ShapeDtypeStruct((3,) + p.shape[1:], p.dtype) for p in ps],
        in_specs=[ANY] * n, out_specs=[ANY] * n,
        scratch_shapes=[pltpu.SemaphoreType.DMA((n, 3)), pltpu.SemaphoreType.DMA((n, 3))],
    )(*ps)


def _sibling_join_halves(fs):
    n = len(fs)

    def body(*refs):
        ins, outs = refs[:n], refs[n:2 * n]
        send_sems, recv_sems, local_sems = refs[2 * n:]
        x, y, c = _mesh_pos()
        cps, loc = [], []
        for t in range(n):
            rh = fs[t].shape[0]
            mine = outs[t].at[pl.ds(c * rh, rh)]
            lc = pltpu.make_async_copy(ins[t], mine, local_sems.at[t])
            lc.start()
            loc.append(lc)
            cp = pltpu.make_async_remote_copy(
                src_ref=ins[t], dst_ref=mine, send_sem=send_sems.at[t], recv_sem=recv_sems.at[t],
                device_id=(x, y, 1 - c), device_id_type=MESH)
            cp.start()
            cps.append(cp)
        for t in range(n):
            rh = fs[t].shape[0]
            theirs = outs[t].at[pl.ds((1 - c) * rh, rh)]
            pltpu.make_async_remote_copy(
                src_ref=theirs, dst_ref=theirs, send_sem=send_sems.at[t], recv_sem=recv_sems.at[t],
                device_id=(x, y, c), device_id_type=MESH).wait_recv()
        for cp in cps:
            cp.wait_send()
        for lc in loc:
            lc.wait()

    return pl.pallas_call(
        body, name="sibling_join_halves",
        out_shape=[jax.ShapeDtypeStruct((2 * f.shape[0], f.shape[1]), f.dtype) for f in fs],
        in_specs=[ANY] * n, out_specs=[ANY] * n,
        scratch_shapes=[pltpu.SemaphoreType.DMA((n,)), pltpu.SemaphoreType.DMA((n,)), pltpu.SemaphoreType.DMA((n,))],
    )(*fs)


_DIMS = {"nn": (((1,), (0,)), ((), ())), "nt": (((1,), (1,)), ((), ())), "tn": (((0,), (0,)), ((), ()))}


def _accum(ref, val, first):
    @pl.when(first)
    def _():
        ref[...] = val

    @pl.when(jnp.logical_not(first))
    def _():
        ref[...] += val


def _matmul(name, grid, pairs, extras, outs, acc_shape, epilogue):
    n_p, n_e, n_o = len(pairs), len(extras), len(outs)
    n_k = grid[-1]
    dims = [_DIMS[p[4]] for p in pairs]

    def body(*refs):
        ab = refs[:2 * n_p]
        ex = refs[2 * n_p:2 * n_p + n_e]
        out = refs[2 * n_p + n_e:2 * n_p + n_e + n_o]
        accs = refs[2 * n_p + n_e + n_o:]
        ids = [pl.program_id(ax) for ax in range(len(grid))]
        k = ids[-1]

        @pl.when(k == 0)
        def _():
            for acc in accs:
                acc[...] = jnp.zeros_like(acc)

        for p in range(n_p):
            a = ab[2 * p][...].astype(BF16)
            b = ab[2 * p + 1][...].astype(BF16)
            accs[p][...] += lax.dot_general(a, b, dims[p], preferred_element_type=F32)

        @pl.when(k == n_k - 1)
        def _():
            epilogue([acc[...] for acc in accs], ex, out, ids)

    in_specs = []
    operands = []
    for a, a_spec, b, b_spec, _ in pairs:
        in_specs += [a_spec, b_spec]
        operands += [a, b]
    for e, e_spec in extras:
        in_specs.append(e_spec)
        operands.append(e)
    return pl.pallas_call(
        body, name=name, grid=grid, in_specs=in_specs,
        out_specs=[o[1] for o in outs], out_shape=[o[0] for o in outs],
        scratch_shapes=[pltpu.VMEM(acc_shape, F32) for _ in pairs],
        compiler_params=_cparams(("arbitrary",) * len(grid)),
    )(*operands)


def _sds(shape, dtype):
    return jax.ShapeDtypeStruct(shape, dtype)


def _row(n):
    return pl.BlockSpec((1, n), lambda *_: (0, 0))


def _layer_norm(r):
    mu = jnp.mean(r, axis=-1, keepdims=True)
    xc = r - mu
    var = jnp.mean(xc * xc, axis=-1, keepdims=True)
    rstd = lax.rsqrt(var + LN_EPS)
    return xc * rstd, rstd


def _layer_norm_bwd(dxhat, xhat, rstd):
    m1 = jnp.mean(dxhat, axis=-1, keepdims=True)
    m2 = jnp.mean(dxhat * xhat, axis=-1, keepdims=True)
    return rstd * (dxhat - m1 - xhat * m2)


def _colsum(v):
    return jnp.sum(v, axis=0, keepdims=True)


def _fwd_in(x_bf, wg_in):
    s, d = x_bf.shape
    inc = wg_in.shape[2]
    tm, tn, tk = _tile(s, 1024), _tile(inc, 1280), _tile(d, 2048)
    nb = inc // tn

    def epi(accs, ex, out, ids):
        out[0][...] = accs[0].astype(BF16)

    return _matmul(
        "fwd_in", (s // tm, N_CHIP * nb, d // tk),
        [(x_bf, pl.BlockSpec((tm, tk), lambda i, j, k: (i, k)),
          wg_in, pl.BlockSpec((None, tk, tn), lambda i, j, k: (j // nb, k, j % nb)), "nn")],
        [], [(_sds((s, N_CHIP * inc), BF16), pl.BlockSpec((tm, tn), lambda i, j, k: (i, j)))],
        (tm, tn), epi)[0]


def _fwd_merge(y_pool, y_lru, w_pu, w_lu, z):
    s, d = y_pool.shape
    tm, tn, tk = _tile(s, 1024), _tile(d, 1024), _tile(d, 1024)
    ga0, gb0 = 3 * d // tn, 4 * d // tn

    def epi(accs, ex, out, ids):
        sa = jax.nn.sigmoid(ex[0][...].astype(F32))
        sb = jax.nn.sigmoid(ex[1][...].astype(F32))
        out[0][...] = (sa * accs[0] + sb * accs[1]).astype(BF16)
        out[1][...] = accs[0].astype(BF16)
        out[2][...] = accs[1].astype(BF16)

    a_spec = pl.BlockSpec((tm, tk), lambda i, j, k: (i, k))
    b_spec = pl.BlockSpec((tk, tn), lambda i, j, k: (k, j))
    o_spec = pl.BlockSpec((tm, tn), lambda i, j, k: (i, j))
    return _matmul(
        "fwd_merge", (s // tm, d // tn, d // tk),
        [(y_pool, a_spec, w_pu, b_spec, "nn"), (y_lru, a_spec, w_lu, b_spec, "nn")],
        [(z, pl.BlockSpec((tm, tn), lambda i, j, k: (i, ga0 + j))), (z, pl.BlockSpec((tm, tn), lambda i, j, k: (i, gb0 + j)))],
        [(_sds((s, d), BF16), o_spec)] * 3, (tm, tn), epi)


def _fwd_out_ln1(m, w_out, x, b_out, g1, b1):
    s, d = x.shape
    tm, tk = _tile(s, 256), _tile(d, 2048)

    def epi(accs, ex, out, ids):
        r = DN_ALPHA * ex[0][...] + accs[0] + ex[1][...]
        xhat, rstd = _layer_norm(r)
        out[0][...] = xhat
        out[1][...] = (xhat * ex[2][...] + ex[3][...]).astype(BF16)
        out[2][...] = rstd

    full = pl.BlockSpec((tm, d), lambda i, j, k: (i, 0))
    return _matmul(
        "fwd_out_ln1", (s // tm, 1, d // tk),
        [(m, pl.BlockSpec((tm, tk), lambda i, j, k: (i, k)), w_out, pl.BlockSpec((tk, d), lambda i, j, k: (k, 0)), "nn")],
        [(x, full), (b_out, _row(d)), (g1, _row(d)), (b1, _row(d))],
        [(_sds((s, d), F32), full), (_sds((s, d), BF16), full), (_sds((s, 1), F32), pl.BlockSpec((tm, 1), lambda i, j, k: (i, 0)))],
        (tm, d), epi)


def _fwd_ff1(x1_bf, wg_ff1, b_ff1):
    s, d = x1_bf.shape
    fc = wg_ff1.shape[2]
    tm, tn, tk = _tile(s, 1024), _tile(fc, 1024), _tile(d, 2048)
    nb = fc // tn

    def epi(accs, ex, out, ids):
        p = jnp.maximum(accs[0] + ex[0][...], 0.0)
        out[0][...] = (p * p).astype(BF16)

    return _matmul(
        "fwd_ff1", (s // tm, N_CHIP * nb, d // tk),
        [(x1_bf, pl.BlockSpec((tm, tk), lambda i, j, k: (i, k)),
          wg_ff1, pl.BlockSpec((None, tk, tn), lambda i, j, k: (j // nb, k, j % nb)), "nn")],
        [(b_ff1, pl.BlockSpec((1, tn), lambda i, j, k: (0, j)))],
        [(_sds((s, N_CHIP * fc), BF16), pl.BlockSpec((tm, tn), lambda i, j, k: (i, j)))],
        (tm, tn), epi)[0]


def _fwd_ff2_ln2_loss(hdn, w_ff2, xhat1, g1, b1, b_ff2, g2, b2, target):
    s, f = hdn.shape
    d = xhat1.shape[1]
    tm, tk = _tile(s, 256), _tile(f, 2048)

    def epi(accs, ex, out, ids):
        first = ids[0] == 0
        x1 = ex[0][...] * ex[1][...] + ex[2][...]
        r = DN_ALPHA * x1 + accs[0] + ex[3][...]
        xhat, rstd = _layer_norm(r)
        g2v = ex[4][...]
        err = xhat * g2v + ex[5][...] - ex[6][...]
        part = 0.5 * jnp.sum(jnp.mean(err * err, axis=-1, keepdims=True), axis=0, keepdims=True)
        dy = err / d
        dr2 = _layer_norm_bwd(dy * g2v, xhat, rstd)
        out[0][...] = dr2
        out[1][...] = dr2.astype(BF16)
        _accum(out[2], _colsum(dy * xhat), first)
        _accum(out[3], _colsum(dy), first)
        _accum(out[4], _colsum(dr2), first)
        _accum(out[5], jnp.broadcast_to(part, (1, 128)), first)

    full = pl.BlockSpec((tm, d), lambda i, j, k: (i, 0))
    return _matmul(
        "fwd_ff2_ln2_loss", (s // tm, 1, f // tk),
        [(hdn, pl.BlockSpec((tm, tk), lambda i, j, k: (i, k)), w_ff2, pl.BlockSpec((tk, d), lambda i, j, k: (k, 0)), "nn")],
        [(xhat1, full), (g1, _row(d)), (b1, _row(d)), (b_ff2, _row(d)), (g2, _row(d)), (b2, _row(d)), (target, full)],
        [(_sds((s, d), F32), full), (_sds((s, d), BF16), full), (_sds((1, d), F32), _row(d)), (_sds((1, d), F32), _row(d)),
         (_sds((1, d), F32), _row(d)), (_sds((1, 128), F32), _row(128))],
        (tm, d), epi)


def _bwd_ff2_in(dr2_bf, w_ff2, hdn):
    s, d = dr2_bf.shape
    f = hdn.shape[1]
    tm, tn, tk = _tile(s, 1024), _tile(f, 1024), _tile(d, 2048)

    def epi(accs, ex, out, ids):
        dpre = accs[0] * (2.0 * jnp.sqrt(ex[0][...].astype(F32)))
        out[0][...] = dpre.astype(BF16)
        _accum(out[1], _colsum(dpre), ids[1] == 0)

    return _matmul(
        "bwd_ff2_in", (f // tn, s // tm, d // tk),
        [(dr2_bf, pl.BlockSpec((tm, tk), lambda j, i, k: (i, k)), w_ff2, pl.BlockSpec((tn, tk), lambda j, i, k: (j, k)), "nt")],
        [(hdn, pl.BlockSpec((tm, tn), lambda j, i, k: (i, j)))],
        [(_sds((s, f), BF16), pl.BlockSpec((tm, tn), lambda j, i, k: (i, j))), (_sds((1, f), F32), pl.BlockSpec((1, tn), lambda j, i, k: (0, j)))],
        (tm, tn), epi)


def _bwd_ff1_in_ln1(dpre, wg_ff1, dr2, xhat1, rstd1, g1):
    s, f = dpre.shape
    d = xhat1.shape[1]
    fc = wg_ff1.shape[2]
    tm, tk = _tile(s, 256), _tile(fc, 2048)
    nb = fc // tk

    def epi(accs, ex, out, ids):
        first = ids[0] == 0
        xhat = ex[1][...]
        dx1 = accs[0] + DN_ALPHA * ex[0][...]
        dr1 = _layer_norm_bwd(dx1 * ex[3][...], xhat, ex[2][...])
        out[0][...] = dr1
        out[1][...] = dr1.astype(BF16)
        _accum(out[2], _colsum(dx1 * xhat), first)
        _accum(out[3], _colsum(dx1), first)
        _accum(out[4], _colsum(dr1), first)

    full = pl.BlockSpec((tm, d), lambda i, j, k: (i, 0))
    return _matmul(
        "bwd_ff1_in_ln1", (s // tm, 1, f // tk),
        [(dpre, pl.BlockSpec((tm, tk), lambda i, j, k: (i, k)),
          wg_ff1, pl.BlockSpec((None, d, tk), lambda i, j, k: (k // nb, 0, k % nb)), "nt")],
        [(dr2, full), (xhat1, full), (rstd1, pl.BlockSpec((tm, 1), lambda i, j, k: (i, 0))), (g1, _row(d))],
        [(_sds((s, d), F32), full), (_sds((s, d), BF16), full), (_sds((1, d), F32), _row(d)), (_sds((1, d), F32), _row(d)),
         (_sds((1, d), F32), _row(d))],
        (tm, d), epi)


def _bwd_out_in(dr1_bf, w_out, z, pa, pb):
    s, d = dr1_bf.shape
    tm, tn, tk = _tile(s, 1024), _tile(d, 1024), _tile(d, 2048)
    ga0, gb0 = 3 * d // tn, 4 * d // tn

    def epi(accs, ex, out, ids):
        dm = accs[0]
        sa = jax.nn.sigmoid(ex[0][...].astype(F32))
        sb = jax.nn.sigmoid(ex[1][...].astype(F32))
        out[0][...] = (dm * sa).astype(BF16)
        out[1][...] = (dm * sb).astype(BF16)
        out[2][...] = (dm * ex[2][...].astype(F32) * sa * (1.0 - sa)).astype(BF16)
        out[3][...] = (dm * ex[3][...].astype(F32) * sb * (1.0 - sb)).astype(BF16)

    o_spec = pl.BlockSpec((tm, tn), lambda i, j, k: (i, j))
    return _matmul(
        "bwd_out_in", (s // tm, d // tn, d // tk),
        [(dr1_bf, pl.BlockSpec((tm, tk), lambda i, j, k: (i, k)), w_out, pl.BlockSpec((tn, tk), lambda i, j, k: (j, k)), "nt")],
        [(z, pl.BlockSpec((tm, tn), lambda i, j, k: (i, ga0 + j))), (z, pl.BlockSpec((tm, tn), lambda i, j, k: (i, gb0 + j))),
         (pa, o_spec), (pb, o_spec)],
        [(_sds((s, d), BF16), o_spec)] * 4, (tm, tn), epi)


def _bwd_up_in(name, dp, w_up):
    s, d = dp.shape
    n = w_up.shape[0]
    tm, tn, tk = _tile(s, 1024), _tile(n, 1024), _tile(d, 2048)

    def epi(accs, ex, out, ids):
        out[0][...] = accs[0].astype(BF16)

    return _matmul(
        name, (s // tm, n // tn, d // tk),
        [(dp, pl.BlockSpec((tm, tk), lambda i, j, k: (i, k)), w_up, pl.BlockSpec((tn, tk), lambda i, j, k: (j, k)), "nt")],
        [], [(_sds((s, n), BF16), pl.BlockSpec((tm, tn), lambda i, j, k: (i, j)))], (tm, tn), epi)[0]


def _bwd_in(dz, wg_in, dr1):
    s, d = dr1.shape
    inc = wg_in.shape[2]
    tm, tn, tk = _tile(s, 1024), _tile(d, 1024), _tile(inc, 1280)
    nb = inc // tk

    def epi(accs, ex, out, ids):
        out[0][...] = accs[0] + DN_ALPHA * ex[0][...]

    o_spec = pl.BlockSpec((tm, tn), lambda i, j, k: (i, j))
    return _matmul(
        "bwd_in", (s // tm, d // tn, N_CHIP * nb),
        [(dz, pl.BlockSpec((tm, tk), lambda i, j, k: (i, k)),
          wg_in, pl.BlockSpec((None, tn, tk), lambda i, j, k: (k // nb, j, k % nb)), "nt")],
        [(dr1, o_spec)], [(_sds((s, d), F32), o_spec)], (tm, tn), epi)[0]


def _wgrad(name, a, b, col_sharded):
    s, ka = a.shape
    n = b.shape[1]
    tm, tk = _tile(ka, 1024), _tile(s, 1024)
    tn = _tile(n // N_CHIP, 1280) if col_sharded else _tile(n, 1024)

    def epi(accs, ex, out, ids):
        out[0][...] = accs[0].astype(BF16)

    if col_sharded:
        nb = (n // N_CHIP) // tn
        o = (_sds((N_CHIP, ka, n // N_CHIP), BF16), pl.BlockSpec((None, tm, tn), lambda i, j, k: (j // nb, i, j % nb)))
    else:
        o = (_sds((ka, n), BF16), pl.BlockSpec((tm, tn), lambda i, j, k: (i, j)))
    res = _matmul(
        name, (ka // tm, n // tn, s // tk),
        [(a, pl.BlockSpec((tk, tm), lambda i, j, k: (k, i)), b, pl.BlockSpec((tk, tn), lambda i, j, k: (k, j)), "tn")],
        [], [o], (tm, tn), epi)[0]
    return res if col_sharded else res.reshape(N_CHIP, ka // N_CHIP, n)


def _chunk(s):
    return _tile(s, 512, SUBLANES)


def _zero_pads(ref, s):
    zeros = jnp.zeros((PAD, ref.shape[1]), F32)
    ref[pl.ds(0, PAD), :] = zeros
    ref[pl.ds(PAD + s, PAD), :] = zeros


def _window(ref, t0, t):
    return ref[pl.ds(t0, t + 2 * PAD), :]


def _shift(sup, off, t):
    return sup[PAD + off:PAD + off + t, :]


def _pool_count(t0, t, s, w):
    pos = t0 + lax.broadcasted_iota(jnp.int32, (t, 1), 0)
    return (jnp.minimum(pos + w // 2, s) - jnp.maximum(pos - w // 2, 0)).astype(F32)


def _pool_fwd(z, pool_w, pool_scale):
    s = z.shape[0]
    n_g, pg = pool_w.shape[0], pool_w.shape[1]
    assert n_g == len(POOL_WINDOWS) and max(POOL_WINDOWS) // 2 <= PAD
    t = _chunk(s)

    def body(u_ref, w_ref, sc_ref, d_ref, y_ref, pad_ref):
        g = pl.program_id(0)
        _zero_pads(pad_ref, s)
        pad_ref[pl.ds(PAD, s), :] = u_ref[...].astype(F32)
        for gi, w in enumerate(POOL_WINDOWS):
            @pl.when(g == gi)
            def _():
                def step(ch, carry):
                    t0 = pl.multiple_of(ch * t, t)
                    sup = _window(pad_ref, t0, t)
                    acc = _shift(sup, -(w // 2), t)
                    for o in range(-(w // 2) + 1, w // 2):
                        acc = acc + _shift(sup, o, t)
                    dd = (acc / _pool_count(t0, t, s, w) - _shift(sup, 0, t)).astype(BF16)
                    d_ref[pl.ds(t0, t), :] = dd
                    y = jnp.dot(dd, w_ref[...], preferred_element_type=F32) * sc_ref[...]
                    y_ref[pl.ds(t0, t), :] = y.astype(BF16)
                    return carry

                lax.fori_loop(0, s // t, step, 0)

    blk = pl.BlockSpec((s, pg), lambda g: (0, g))
    return pl.pallas_call(
        body, name="pool_fwd", grid=(n_g,),
        in_specs=[blk, pl.BlockSpec((None, pg, pg), lambda g: (g, 0, 0)), pl.BlockSpec((1, pg), lambda g: (0, g))],
        out_specs=[blk, blk], out_shape=[_sds((s, n_g * pg), BF16)] * 2,
        scratch_shapes=[pltpu.VMEM((s + 2 * PAD, pg), F32)],
        compiler_params=_cparams(("arbitrary",)),
    )(z, pool_w, pool_scale)


def _pool_bwd(dsv, dy, pool_w, pool_scale):
    s = dsv.shape[0]
    n_g, pg = pool_w.shape[0], pool_w.shape[1]
    t = _chunk(s)

    def body(d_ref, dy_ref, w_ref, sc_ref, du_ref, dw_ref, dsc_ref, epad_ref, dwacc_ref):
        g = pl.program_id(0)
        _zero_pads(epad_ref, s)
        dwacc_ref[...] = jnp.zeros_like(dwacc_ref)
        for gi, w in enumerate(POOL_WINDOWS):
            @pl.when(g == gi)
            def _():
                def first(ch, dsc):
                    t0 = pl.multiple_of(ch * t, t)
                    dd = d_ref[pl.ds(t0, t), :]
                    dyc = dy_ref[pl.ds(t0, t), :].astype(F32)
                    wv = w_ref[...]
                    ypre = jnp.dot(dd, wv, preferred_element_type=F32)
                    dq = (dyc * sc_ref[...]).astype(BF16)
                    dwacc_ref[...] += lax.dot_general(dd, dq, _DIMS["tn"], preferred_element_type=F32)
                    ddv = lax.dot_general(dq, wv, _DIMS["nt"], preferred_element_type=F32)
                    epad_ref[pl.ds(pl.multiple_of(PAD + t0, SUBLANES), t), :] = ddv / _pool_count(t0, t, s, w)
                    return dsc + _colsum(dyc * ypre)

                dsc_ref[...] = lax.fori_loop(0, s // t, first, jnp.zeros((1, pg), F32))

                def second(ch, carry):
                    t0 = pl.multiple_of(ch * t, t)
                    sup = _window(epad_ref, t0, t)
                    acc = _shift(sup, -(w // 2) + 1, t)
                    for o in range(-(w // 2) + 2, w // 2 + 1):
                        acc = acc + _shift(sup, o, t)
                    du_ref[pl.ds(t0, t), :] = (acc - _shift(sup, 0, t) * _pool_count(t0, t, s, w)).astype(BF16)
                    return carry

                lax.fori_loop(0, s // t, second, 0)

        dw_ref[...] = dwacc_ref[...].astype(BF16)

    blk = pl.BlockSpec((s, pg), lambda g: (0, g))
    w_spec = pl.BlockSpec((None, pg, pg), lambda g: (g, 0, 0))
    sc_spec = pl.BlockSpec((1, pg), lambda g: (0, g))
    return pl.pallas_call(
        body, name="pool_bwd", grid=(n_g,),
        in_specs=[blk, blk, w_spec, sc_spec], out_specs=[blk, w_spec, sc_spec],
        out_shape=[_sds((s, n_g * pg), BF16), _sds((n_g, pg, pg), BF16), _sds((1, n_g * pg), F32)],
        scratch_shapes=[pltpu.VMEM((s + 2 * PAD, pg), F32), pltpu.VMEM((pg, pg), F32)],
        compiler_params=_cparams(("arbitrary",)),
    )(dsv, dy, pool_w, pool_scale)


def _softplus(x):
    e = jnp.exp(-jnp.abs(x))
    log1p_e = jnp.where(e < 1e-2, e * (1.0 - e * (0.5 - e / 3.0)), jnp.log(1.0 + e))
    return jnp.maximum(x, 0.0) + log1p_e


def _neg_expm1(x):
    series = 1.0 + x / 7.0
    for n in (6.0, 5.0, 4.0, 3.0, 2.0):
        series = 1.0 + (x / n) * series
    return jnp.where(x > -0.25, -x * series, 1.0 - jnp.exp(x))


_GELU_C = math.sqrt(2.0 / math.pi)


def _gelu(x):
    th = jnp.tanh(_GELU_C * (x + 0.044715 * x * x * x))
    return 0.5 * x * (1.0 + th), th


def _gelu_grad(x, th):
    return 0.5 * (1.0 + th) + 0.5 * x * (1.0 - th * th) * _GELU_C * (1.0 + 3.0 * 0.044715 * x * x)


def _scan_chunk(a_ref, b_ref, o_ref, o_off, carry, t, reverse):
    n = a_ref.shape[1]
    row = lax.broadcasted_iota(jnp.int32, (SUBLANES, n), 0)
    n_groups = t // SUBLANES

    def step(gi, carry):
        g = n_groups - 1 - gi if reverse else gi
        r0 = pl.multiple_of(g * SUBLANES, SUBLANES)
        a = a_ref[pl.ds(r0, SUBLANES), :]
        b = b_ref[pl.ds(r0, SUBLANES), :]
        for k in (1, 2, 4):
            keep = row < SUBLANES - k if reverse else row >= k
            sh = SUBLANES - k if reverse else k
            ar = jnp.where(keep, pltpu.roll(a, sh, 0), 1.0)
            br = jnp.where(keep, pltpu.roll(b, sh, 0), 0.0)
            b = a * br + b
            a = a * ar
        h = a * carry + b
        o_ref[pl.ds(pl.multiple_of(o_off + r0, SUBLANES), SUBLANES), :] = h
        edge = h[0:1, :] if reverse else h[SUBLANES - 1:SUBLANES, :]
        return jnp.broadcast_to(edge, h.shape)

    return lax.fori_loop(0, n_groups, step, carry)


def _lru_params(pk_ref):
    rows = pk_ref[...]
    get = lambda i: rows[i:i + 1, :]
    cw = [get(k) for k in range(4)]
    lam = (get(9), get(10))
    big_l = tuple(-LRU_C * _softplus(-v) for v in lam)
    return cw, get(4), (get(5), get(6)), (get(7), get(8)), lam, big_l


def _conv(sup, cw, cb, t):
    xc = cb + cw[0] * _shift(sup, -2, t)
    for k in range(1, 4):
        xc = xc + cw[k] * _shift(sup, k - 2, t)
    return xc


def _gates(xcb, w_ref, d, bk, ba, bx, big_l):
    pre = jnp.dot(xcb, w_ref[:, pl.ds(d * 2 * bk, 2 * bk)], preferred_element_type=F32)
    r = jax.nn.sigmoid(pre[:, :bk] + ba[d])
    i = jax.nn.sigmoid(pre[:, bk:] + bx[d])
    la = big_l[d] * r
    return r, i, jnp.exp(la), jnp.sqrt(_neg_expm1(2.0 * la))


def _lru_specs(s, d, bk):
    u_spec = pl.BlockSpec((s, bk), lambda h: (0, d // bk + h))
    ug_spec = pl.BlockSpec((s, bk), lambda h: (0, 2 * d // bk + h))
    w_spec = pl.BlockSpec((None, bk, 4 * bk), lambda h: (h, 0, 0))
    pk_spec = pl.BlockSpec((None, 16, bk), lambda h: (h, 0, 0))
    blk = pl.BlockSpec((s, bk), lambda h: (0, h))
    return u_spec, ug_spec, w_spec, pk_spec, blk


def _lru_fwd(z, gatew, pk):
    s = z.shape[0]
    n_h, bk = gatew.shape[0], gatew.shape[1]
    d = n_h * bk
    t = _chunk(s)
    n_ch = s // t

    def body(u_ref, ug_ref, w_ref, pk_ref, y_ref, upad, h0buf, abuf, bbuf):
        _zero_pads(upad, s)
        upad[pl.ds(PAD, s), :] = u_ref[...].astype(F32)
        cw, cb, ba, bx, _, big_l = _lru_params(pk_ref)
        zero = jnp.zeros((SUBLANES, bk), F32)

        def fill(t0, dr):
            xc = _conv(_window(upad, t0, t), cw, cb, t)
            _, i, a, sq = _gates(xc.astype(BF16), w_ref, dr, bk, ba, bx, big_l)
            abuf[...] = a
            bbuf[...] = sq * i * xc

        def up(ch, carry):
            t0 = pl.multiple_of(ch * t, t)
            fill(t0, 0)
            return _scan_chunk(abuf, bbuf, h0buf, t0, carry, t, False)

        lax.fori_loop(0, n_ch, up, zero)

        def down(ci, carry):
            t0 = pl.multiple_of((n_ch - 1 - ci) * t, t)
            fill(t0, 1)
            carry = _scan_chunk(abuf, bbuf, bbuf, 0, carry, t, True)
            gl, _ = _gelu(ug_ref[pl.ds(t0, t), :].astype(F32))
            y_ref[pl.ds(t0, t), :] = ((h0buf[pl.ds(t0, t), :] + bbuf[...]) * gl).astype(BF16)
            return carry

        lax.fori_loop(0, n_ch, down, zero)

    u_spec, ug_spec, w_spec, pk_spec, blk = _lru_specs(s, d, bk)
    return pl.pallas_call(
        body, name="lru_fwd", grid=(n_h,),
        in_specs=[u_spec, ug_spec, w_spec, pk_spec], out_specs=blk, out_shape=_sds((s, d), BF16),
        scratch_shapes=[pltpu.VMEM((s + 2 * PAD, bk), F32), pltpu.VMEM((s, bk), F32), pltpu.VMEM((t, bk), F32), pltpu.VMEM((t, bk), F32)],
        compiler_params=_cparams(("arbitrary",)),
    )(z, z, gatew, pk)


def _lru_grads(lam_, hnb, a, sq, r, i, xc, xcb, w_ref, dwacc, d, big_l, acc):
    bk = xc.shape[1]
    dba, dbx, dl = acc
    dla = lam_ * hnb * a - lam_ * i * xc * (a * a) / sq
    dpr = dla * big_l * r * (1.0 - r)
    dpi = lam_ * sq * xc * i * (1.0 - i)
    dprb, dpib = dpr.astype(BF16), dpi.astype(BF16)
    c0 = d * 2 * bk
    dxc = (lam_ * sq * i
           + lax.dot_general(dprb, w_ref[:, pl.ds(c0, bk)], _DIMS["nt"], preferred_element_type=F32)
           + lax.dot_general(dpib, w_ref[:, pl.ds(c0 + bk, bk)], _DIMS["nt"], preferred_element_type=F32))
    dwacc[:, pl.ds(c0, bk)] += lax.dot_general(xcb, dprb, _DIMS["tn"], preferred_element_type=F32)
    dwacc[:, pl.ds(c0 + bk, bk)] += lax.dot_general(xcb, dpib, _DIMS["tn"], preferred_element_type=F32)
    return dxc, (dba + _colsum(dpr), dbx + _colsum(dpi), dl + _colsum(dla * r))


def _lru_bwd(z, dy, gatew, pk):
    s = z.shape[0]
    n_h, bk = gatew.shape[0], gatew.shape[1]
    d = n_h * bk
    t = _chunk(s)
    n_ch = s // t

    def body(u_ref, ug_ref, dy_ref, w_ref, pk_ref, du_ref, dug_ref, dw_ref, dpk_ref,
             upad, h0pad, h1pad, dxpad, abuf, bbuf, lbuf, dwacc, edge):
        for ref in (upad, h0pad, h1pad, dxpad):
            _zero_pads(ref, s)
        upad[pl.ds(PAD, s), :] = u_ref[...].astype(F32)
        dwacc[...] = jnp.zeros_like(dwacc)
        cw, cb, ba, bx, lam, big_l = _lru_params(pk_ref)
        zero = jnp.zeros((SUBLANES, bk), F32)
        zrow = jnp.zeros((1, bk), F32)
        rowi = lax.broadcasted_iota(jnp.int32, (t, bk), 0)

        def at(t0):
            return pl.ds(pl.multiple_of(PAD + t0, SUBLANES), t)

        def conv_in(t0):
            xc = _conv(_window(upad, t0, t), cw, cb, t)
            return xc, xc.astype(BF16)

        def dh_of(t0):
            ug = ug_ref[pl.ds(t0, t), :].astype(F32)
            gl, th = _gelu(ug)
            dyv = dy_ref[pl.ds(t0, t), :].astype(F32)
            return dyv * gl, dyv * _gelu_grad(ug, th)

        def sweep1(ch, carry):
            t0 = pl.multiple_of(ch * t, t)
            xc, xcb = conv_in(t0)
            _, i, a, sq = _gates(xcb, w_ref, 0, bk, ba, bx, big_l)
            abuf[...] = a
            bbuf[...] = sq * i * xc
            return _scan_chunk(abuf, bbuf, h0pad, PAD + t0, carry, t, False)

        lax.fori_loop(0, n_ch, sweep1, zero)

        edge[...] = zero

        def sweep2(ci, st):
            carry_h, carry_l, acc = st
            t0 = pl.multiple_of((n_ch - 1 - ci) * t, t)
            xc, xcb = conv_in(t0)
            _, i1, a1, sq1 = _gates(xcb, w_ref, 1, bk, ba, bx, big_l)
            abuf[...] = a1
            bbuf[...] = sq1 * i1 * xc
            carry_h = _scan_chunk(abuf, bbuf, h1pad, PAD + t0, carry_h, t, True)
            dh, dgl = dh_of(t0)
            dug_ref[pl.ds(t0, t), :] = (dgl * (h0pad[at(t0), :] + h1pad[at(t0), :])).astype(BF16)
            r0, i0, a0, sq0 = _gates(xcb, w_ref, 0, bk, ba, bx, big_l)
            abuf[...] = jnp.where(rowi == t - 1, edge[0:1, :], pltpu.roll(a0, t - 1, 0))
            bbuf[...] = dh
            carry_l = _scan_chunk(abuf, bbuf, lbuf, 0, carry_l, t, True)
            edge[...] = jnp.broadcast_to(a0[0:1, :], (SUBLANES, bk))
            hprev = _shift(_window(h0pad, t0, t), -1, t)
            dxc, acc = _lru_grads(lbuf[...], hprev, a0, sq0, r0, i0, xc, xcb, w_ref, dwacc, 0, big_l[0], acc)
            dxpad[at(t0), :] = dxc
            return carry_h, carry_l, acc

        _, _, acc0 = lax.fori_loop(0, n_ch, sweep2, (zero, zero, (zrow, zrow, zrow)))

        edge[...] = zero

        def sweep3(ch, st):
            carry_l, acc = st
            t0 = pl.multiple_of(ch * t, t)
            xc, xcb = conv_in(t0)
            r1, i1, a1, sq1 = _gates(xcb, w_ref, 1, bk, ba, bx, big_l)
            dh, _ = dh_of(t0)
            abuf[...] = jnp.where(rowi == 0, edge[0:1, :], pltpu.roll(a1, 1, 0))
            bbuf[...] = dh
            carry_l = _scan_chunk(abuf, bbuf, lbuf, 0, carry_l, t, False)
            edge[...] = jnp.broadcast_to(a1[t - 1:t, :], (SUBLANES, bk))
            hnext = _shift(_window(h1pad, t0, t), 1, t)
            dxc, acc = _lru_grads(lbuf[...], hnext, a1, sq1, r1, i1, xc, xcb, w_ref, dwacc, 1, big_l[1], acc)
            dxpad[at(t0), :] += dxc
            return carry_l, acc

        _, acc1 = lax.fori_loop(0, n_ch, sweep3, (zero, (zrow, zrow, zrow)))

        def sweep4(ch, st):
            t0 = pl.multiple_of(ch * t, t)
            sdx = _window(dxpad, t0, t)
            su = _window(upad, t0, t)
            dxc = _shift(sdx, 0, t)
            du = cw[0] * _shift(sdx, 2, t) + cw[1] * _shift(sdx, 1, t) + cw[2] * dxc + cw[3] * _shift(sdx, -1, t)
            du_ref[pl.ds(t0, t), :] = du.astype(BF16)
            return tuple(st[k] + _colsum(dxc * _shift(su, k - 2, t)) for k in range(4)) + (st[4] + _colsum(dxc),)

        conv_g = lax.fori_loop(0, n_ch, sweep4, (zrow,) * 5)

        dpk_ref[...] = jnp.zeros_like(dpk_ref)
        rows = list(conv_g) + [acc0[0], acc1[0], acc0[1], acc1[1],
                               acc0[2] * LRU_C * jax.nn.sigmoid(-lam[0]), acc1[2] * LRU_C * jax.nn.sigmoid(-lam[1])]
        for k, v in enumerate(rows):
            dpk_ref[pl.ds(k, 1), :] = v
        dw_ref[...] = dwacc[...].astype(BF16)

    u_spec, ug_spec, w_spec, pk_spec, blk = _lru_specs(s, d, bk)
    padded = pltpu.VMEM((s + 2 * PAD, bk), F32)
    chunk = pltpu.VMEM((t, bk), F32)
    return pl.pallas_call(
        body, name="lru_bwd", grid=(n_h,),
        in_specs=[u_spec, ug_spec, blk, w_spec, pk_spec], out_specs=[blk, blk, w_spec, pk_spec],
        out_shape=[_sds((s, d), BF16), _sds((s, d), BF16), _sds((n_h, bk, 4 * bk), BF16), _sds((n_h, 16, bk), F32)],
        scratch_shapes=[padded, padded, padded, padded, chunk, chunk, chunk, pltpu.VMEM((bk, 4 * bk), F32),
                        pltpu.VMEM((SUBLANES, bk), F32)],
        compiler_params=_cparams(("arbitrary",)),
    )(z, z, dy, gatew, pk)


def _scalar(v):
    return jnp.reshape(v, (1,)).astype(jnp.int32)


def _add_sibling(g, r, c):
    _, rows, cols = g.shape
    rh = rows // 2
    tr = _tile(rh, 512, 16)
    nr = rh // tr

    def body(c_ref, g_ref, r_ref, o_ref):
        o_ref[...] = (g_ref[...].astype(F32) + r_ref[...].astype(F32)).astype(BF16)

    spec = pl.BlockSpec((None, tr, cols), lambda k, i, c_ref: (k, i, 0))
    return pl.pallas_call(
        body, name="add_sibling", out_shape=_sds((N_CHIP, rh, cols), BF16),
        grid_spec=pltpu.PrefetchScalarGridSpec(
            num_scalar_prefetch=1, grid=(N_CHIP, nr),
            in_specs=[pl.BlockSpec((None, tr, cols), lambda k, i, c_ref: (k, c_ref[0] * nr + i, 0)), spec], out_specs=spec),
        compiler_params=_cparams(("arbitrary", "arbitrary")),
    )(_scalar(c), g, r)


def _sum_chips(p, rcv, k_me):
    _, rh, cols = p.shape
    tr = _tile(rh, 512, 16)

    def body(k_ref, p_ref, r_ref, o_ref):
        acc = p_ref[...].astype(F32)
        for j in range(3):
            acc = acc + r_ref[j].astype(F32)
        o_ref[...] = acc

    return pl.pallas_call(
        body, name="sum_chips", out_shape=_sds((rh, cols), F32),
        grid_spec=pltpu.PrefetchScalarGridSpec(
            num_scalar_prefetch=1, grid=(rh // tr,),
            in_specs=[pl.BlockSpec((None, tr, cols), lambda i, k_ref: (k_ref[0], i, 0)),
                      pl.BlockSpec((3, tr, cols), lambda i, k_ref: (0, i, 0))],
            out_specs=pl.BlockSpec((tr, cols), lambda i, k_ref: (i, 0))),
        compiler_params=_cparams(("arbitrary",)),
    )(_scalar(k_me), p, rcv)


def _sum_devices(g):
    def body(g_ref, o_ref):
        acc = g_ref[0]
        for dev in range(1, N_DEV):
            acc = acc + g_ref[dev]
        o_ref[...] = acc

    return pl.pallas_call(body, name="sum_devices", out_shape=_sds(g.shape[1:], F32))(g)


def _adamw(w, g, m, v):
    rows, cols = w.shape
    tr = _tile(rows, 256, SUBLANES)

    def body(w_ref, g_ref, m_ref, v_ref, d_ref, nm_ref, nv_ref):
        gv = g_ref[...]
        nm = ADAM_B1 * m_ref[...] + (1.0 - ADAM_B1) * gv
        nv = ADAM_B2 * v_ref[...] + (1.0 - ADAM_B2) * (gv * gv)
        m_hat = nm / (1.0 - ADAM_B1 ** ADAM_STEP)
        v_hat = nv / (1.0 - ADAM_B2 ** ADAM_STEP)
        d_ref[...] = -ADAM_LR * (m_hat / (jnp.sqrt(v_hat) + ADAM_EPS) + ADAM_WD * w_ref[...])
        nm_ref[...] = nm
        nv_ref[...] = nv

    spec = pl.BlockSpec((tr, cols), lambda i: (i, 0))
    return pl.pallas_call(
        body, name="adamw", grid=(rows // tr,), in_specs=[spec] * 4, out_specs=[spec] * 3,
        out_shape=[_sds((rows, cols), F32)] * 3, compiler_params=_cparams(("arbitrary",)),
    )(w, g, m, v)


def _pack(vs, unit):
    flat = jnp.concatenate([v.reshape(-1).astype(F32) for v in vs])
    pad = (-flat.shape[0]) % unit
    if pad:
        flat = jnp.concatenate([flat, jnp.zeros((pad,), F32)])
    return flat.reshape(-1, 128)


def _unpack(p, like):
    flat = p.reshape(-1)
    out, off = [], 0
    for v in like:
        n = math.prod(v.shape)
        out.append(flat[off:off + n].reshape(v.shape))
        off += n
    return out


def kernel(x, w_in, pool_w, pool_scale, conv_w, conv_b, lru_wa, lru_ba, lru_wx, lru_bx, lru_lambda, w_pool_up, w_lru_up, w_out, b_out, ln1_g, ln1_b, w_ff1, b_ff1, w_ff2, b_ff2, ln2_g, ln2_b, loss_target, m_w_in, m_pool_w, m_pool_scale, m_conv_w, m_conv_b, m_lru_wa, m_lru_ba, m_lru_wx, m_lru_bx, m_lru_lambda, m_w_pool_up, m_w_lru_up, m_w_out, m_b_out, m_ln1_g, m_ln1_b, m_w_ff1, m_b_ff1, m_w_ff2, m_b_ff2, m_ln2_g, m_ln2_b, v_w_in, v_pool_w, v_pool_scale, v_conv_w, v_conv_b, v_lru_wa, v_lru_ba, v_lru_wx, v_lru_bx, v_lru_lambda, v_w_pool_up, v_w_lru_up, v_w_out, v_b_out, v_ln1_g, v_ln1_b, v_w_ff1, v_b_ff1, v_w_ff2, v_b_ff2, v_ln2_g, v_ln2_b):
    given = dict(locals())
    wt = {n: given[n] for n in WEIGHTS}
    mom = {n: given["m_" + n] for n in WEIGHTS}
    vel = {n: given["v_" + n] for n in WEIGHTS}

    ix, iy, ic = _mesh_pos()
    k_me = 2 * ix + iy
    s, d = x.shape[1], x.shape[2]
    ds = d // N_CHIP
    n_g, pgs, pg = pool_w.shape[1], pool_w.shape[2], pool_w.shape[3]
    n_h, bks, bk = lru_wa.shape[2], lru_wa.shape[3], lru_wa.shape[4]
    f = b_ff1.shape[1]
    x2 = x[0]
    x_bf = x2.astype(BF16)
    vec = lambda a: a.reshape(1, -1)

    sharded_vecs = [conv_w[0], lru_ba[0], lru_bx[0], lru_lambda[0]]
    rows_sv = jnp.concatenate(sharded_vecs + [jnp.zeros((6, ds), F32)], axis=0)
    sv = _all_gather_small(rows_sv)
    sv = sv.reshape(N_CHIP, 2, 16, ds)[:, 0].transpose(1, 0, 2).reshape(16, d)
    conv_w_f, ba_f, bx_f, lam_f = sv[0:4], sv[4:6], sv[6:8], sv[8:10]
    pk = jnp.concatenate([conv_w_f, conv_b, ba_f, bx_f, lam_f, jnp.zeros((5, d), F32)], axis=0)
    pk = pk.reshape(16, n_h, bk).transpose(1, 0, 2)

    def gate_stack(wa, wx):
        return jnp.stack([wa[0], wx[0]], axis=1)

    mats = {
        "w_in": w_in[0], "w_pool_up": w_pool_up[0], "w_lru_up": w_lru_up[0], "w_out": w_out[0],
        "w_ff1": w_ff1[0], "w_ff2": w_ff2[0],
        "pool_w": pool_w[0].reshape(n_g * pgs, pg),
        "gate_w": gate_stack(lru_wa, lru_wx).reshape(4 * n_h * bks, bk),
    }
    names = list(mats)
    gathered = dict(zip(names, _chip_all_gather([mats[n].astype(BF16) for n in names])))
    wg_in, wg_ff1 = gathered["w_in"], gathered["w_ff1"]
    wf_pu, wf_lu, wf_out = (gathered[n].reshape(d, d) for n in ("w_pool_up", "w_lru_up", "w_out"))
    wf_ff2 = gathered["w_ff2"].reshape(f, d)
    wf_pool = gathered["pool_w"].reshape(N_CHIP, n_g, pgs, pg).transpose(1, 0, 2, 3).reshape(n_g, pg, pg)
    wf_gate = (gathered["gate_w"].reshape(N_CHIP, 2, 2, n_h, bks, bk).transpose(3, 0, 4, 1, 2, 5)
               .reshape(n_h, bk, 4 * bk))

    z = _fwd_in(x_bf, wg_in)
    d_pool, y_pool = _pool_fwd(z, wf_pool, pool_scale)
    y_lru = _lru_fwd(z, wf_gate, pk)
    m_mix, p_a, p_b = _fwd_merge(y_pool, y_lru, wf_pu, wf_lu, z)
    xhat1, x1_bf, rstd1 = _fwd_out_ln1(m_mix, wf_out, x2, b_out, ln1_g, ln1_b)
    hdn = _fwd_ff1(x1_bf, wg_ff1, b_ff1)
    dr2, dr2_bf, g_ln2_g, g_ln2_b, g_b_ff2, loss_part = _fwd_ff2_ln2_loss(
        hdn, wf_ff2, xhat1, ln1_g, ln1_b, b_ff2, ln2_g, ln2_b, loss_target[0])

    dpre, g_b_ff1 = _bwd_ff2_in(dr2_bf, wf_ff2, hdn)
    gw = {"w_ff2": _wgrad("wgrad_ff2", hdn, dr2_bf, False)}
    dr1, dr1_bf, g_ln1_g, g_ln1_b, g_b_out = _bwd_ff1_in_ln1(dpre, wg_ff1, dr2, xhat1, rstd1, ln1_g)
    gw["w_ff1"] = _wgrad("wgrad_ff1", x1_bf, dpre, True)
    dp_a, dp_b, dg_a, dg_b = _bwd_out_in(dr1_bf, wf_out, z, p_a, p_b)
    gw["w_out"] = _wgrad("wgrad_out", m_mix, dr1_bf, False)
    dy_pool = _bwd_up_in("bwd_pool_up_in", dp_a, wf_pu)
    dy_lru = _bwd_up_in("bwd_lru_up_in", dp_b, wf_lu)
    gw["w_pool_up"] = _wgrad("wgrad_pool_up", y_pool, dp_a, False)
    gw["w_lru_up"] = _wgrad("wgrad_lru_up", y_lru, dp_b, False)
    du_pool, g_pool_w, g_pool_scale = _pool_bwd(d_pool, dy_pool, wf_pool, pool_scale)
    du_lru, du_gate, g_gate_w, g_pk = _lru_bwd(z, dy_lru, wf_gate, pk)
    dz = jnp.concatenate([du_pool, du_lru, du_gate, dg_a, dg_b], axis=1)
    grad_x = _bwd_in(dz, wg_in, dr1)
    gw["w_in"] = _wgrad("wgrad_in", x_bf, dz, True)
    gw["pool_w"] = (g_pool_w.reshape(n_g, N_CHIP, pgs, pg).transpose(1, 0, 2, 3).reshape(N_CHIP, n_g * pgs, pg))
    gw["gate_w"] = (g_gate_w.reshape(n_h, N_CHIP, bks, 2, 2, bk).transpose(1, 3, 4, 0, 2, 5)
                    .reshape(N_CHIP, 4 * n_h * bks, bk))

    parts = [gw[n] for n in names]
    from_sibling = _sibling_swap_halves(parts)
    chip_sums = [_add_sibling(g, r, ic) for g, r in zip(parts, from_sibling)]
    from_chips = _chip_scatter(chip_sums)
    halves = [_sum_chips(p, r, k_me) for p, r in zip(chip_sums, from_chips)]
    g_mat = dict(zip(names, _sibling_join_halves(halves)))

    def stacked(tree):
        return gate_stack(tree["lru_wa"], tree["lru_wx"]).reshape(4 * n_h * bks, bk)

    res = {}
    for n in names:
        if n == "gate_w":
            upd = _adamw(stacked(wt), g_mat[n], stacked(mom), stacked(vel))
            outs = [o.reshape(2, 2, n_h, bks, bk) for o in (g_mat[n],) + tuple(upd)]
            res["lru_wa"] = [o[:, 0][None] for o in outs]
            res["lru_wx"] = [o[:, 1][None] for o in outs]
        else:
            shp = wt[n].shape
            upd = _adamw(wt[n].reshape(mats[n].shape), g_mat[n], mom[n].reshape(mats[n].shape), vel[n].reshape(mats[n].shape))
            res[n] = [o.reshape(shp) for o in (g_mat[n],) + tuple(upd)]

    g_pk = g_pk.transpose(1, 0, 2).reshape(16, d)
    vec_full = {
        "pool_scale": g_pool_scale, "conv_w": g_pk[0:4], "conv_b": g_pk[4:5],
        "lru_ba": g_pk[5:7], "lru_bx": g_pk[7:9], "lru_lambda": g_pk[9:11],
        "b_out": g_b_out, "ln1_g": g_ln1_g, "ln1_b": g_ln1_b, "b_ff1": g_b_ff1, "b_ff2": g_b_ff2,
        "ln2_g": g_ln2_g, "ln2_b": g_ln2_b,
    }
    vnames = list(vec_full)
    vg = _sum_devices(_all_gather_small(_pack([vec_full[n] for n in vnames], 1024)))
    vg = dict(zip(vnames, _unpack(vg, [vec_full[n] for n in vnames])))
    for n in ("conv_w", "lru_ba", "lru_bx", "lru_lambda"):
        vg[n] = lax.dynamic_slice_in_dim(vg[n], k_me * ds, ds, axis=1)
    vg = {n: vg[n].reshape(wt[n].shape) for n in vnames}
    upd = _adamw(_pack([wt[n] for n in vnames], 1024), _pack([vg[n] for n in vnames], 1024),
                 _pack([mom[n] for n in vnames], 1024), _pack([vel[n] for n in vnames], 1024))
    upd = [_unpack(u, [wt[n] for n in vnames]) for u in upd]
    for i, n in enumerate(vnames):
        res[n] = [vg[n], upd[0][i], upd[1][i], upd[2][i]]

    loss = lax.psum(loss_part[0, 0], ("x", "y", "c"))
    return (loss, grad_x[None], *[res[n][0] for n in WEIGHTS], *[res[n][1] for n in WEIGHTS],
            *[res[n][2] for n in WEIGHTS], *[res[n][3] for n in WEIGHTS])
```

```python
import functools
import math

import jax
import jax.numpy as jnp
from jax import lax
from jax.experimental import pallas as pl
from jax.experimental.pallas import tpu as pltpu

F32 = jnp.float32
BF16 = jnp.bfloat16
MESH = pl.DeviceIdType.MESH
ANY = pl.BlockSpec(memory_space=pl.ANY)

N_CHIP = 4
N_DEV = 8
VMEM_LIMIT_BYTES = 56 * 1024 * 1024
SUBLANES = 8
PAD = 8

POOL_WINDOWS = (2, 4, 8, 16)
LRU_C = 8.0
DN_ALPHA = 2.0 ** 0.25
LN_EPS = 1e-5
ADAM_LR, ADAM_B1, ADAM_B2, ADAM_EPS, ADAM_WD, ADAM_STEP = 0.001, 0.9, 0.999, 1e-08, 0.01, 10

WEIGHTS = ("w_in", "pool_w", "pool_scale", "conv_w", "conv_b", "lru_wa", "lru_ba", "lru_wx", "lru_bx", "lru_lambda",
           "w_pool_up", "w_lru_up", "w_out", "b_out", "ln1_g", "ln1_b", "w_ff1", "b_ff1", "w_ff2", "b_ff2", "ln2_g", "ln2_b")


def _cparams(sem=None):
    return pltpu.CompilerParams(dimension_semantics=sem, vmem_limit_bytes=VMEM_LIMIT_BYTES)


def _tile(dim, pref, unit=128):
    if dim <= pref:
        return dim
    t = (pref // unit) * unit
    while t > unit and dim % t:
        t -= unit
    assert dim % t == 0, (dim, pref)
    return t


def _mesh_pos():
    x, y, c = lax.axis_index("x"), lax.axis_index("y"), lax.axis_index("c")
    return x, y, c


def _other_chips(x, y):
    return [(1 - x, y), (x, 1 - y), (1 - x, 1 - y)]


def _all_gather_small(v):
    m_per, n = v.shape

    def body(x_ref, out_ref, send_sems, recv_sems, local_sem):
        x, y, c = _mesh_pos()
        me, sibling = (x, y, c), (x, y, 1 - c)
        chips = _other_chips(x, y)

        def rows(px, py, pc):
            return out_ref.at[4 * px + 2 * py + pc]

        def copy(k, block, to, src=None):
            return pltpu.make_async_remote_copy(
                src_ref=rows(*block) if src is None else src, dst_ref=rows(*block),
                send_sem=send_sems.at[k], recv_sem=recv_sems.at[k], device_id=to, device_id_type=MESH)

        mine = pltpu.make_async_copy(x_ref, rows(*me), local_sem)
        mine.start()
        first = [copy(0, me, sibling, src=x_ref)]
        first += [copy(1 + j, me, (*chip, c), src=x_ref) for j, chip in enumerate(chips)]
        for cp in first:
            cp.start()
        passed = [copy(4 + j, (*chip, c), sibling) for j, chip in enumerate(chips)]
        for j, chip in enumerate(chips):
            copy(1 + j, (*chip, c), me).wait_recv()
            passed[j].start()
        copy(0, sibling, me).wait_recv()
        for j, chip in enumerate(chips):
            copy(4 + j, (*chip, 1 - c), me).wait_recv()
        for cp in first + passed:
            cp.wait_send()
        mine.wait()

    return pl.pallas_call(
        body, name="all_gather_small",
        out_shape=jax.ShapeDtypeStruct((N_DEV, m_per, n), v.dtype),
        in_specs=[pl.BlockSpec(memory_space=pltpu.VMEM)],
        out_specs=pl.BlockSpec(memory_space=pltpu.VMEM),
        scratch_shapes=[pltpu.SemaphoreType.DMA((7,)), pltpu.SemaphoreType.DMA((7,)), pltpu.SemaphoreType.DMA],
    )(v)


def _chip_all_gather(ts):
    n = len(ts)

    def body(*refs):
        outs = refs[n:2 * n]
        send_sems, recv_sems = refs[2 * n:]
        x, y, c = _mesh_pos()
        me, sibling = (x, y, c), (x, y, 1 - c)
        k_me = 2 * x + y
        chips = _other_chips(x, y)

        def half(t, which):
            rh = ts[t].shape[1] // 2
            return pl.ds(which * rh, rh)

        def copy(t, s, kk, which, to, src=None):
            dst = outs[t].at[kk, half(t, which)]
            return pltpu.make_async_remote_copy(
                src_ref=dst if src is None else src, dst_ref=dst,
                send_sem=send_sems.at[t, s], recv_sem=recv_sems.at[t, s], device_id=to, device_id_type=MESH)

        sent = []
        for t in range(n):
            for j, chip in enumerate(chips):
                cp = copy(t, j, k_me, c, (*chip, c))
                cp.start()
                sent.append(cp)
        for t in range(n):
            for j, chip in enumerate(chips):
                kk = 2 * chip[0] + chip[1]
                copy(t, j, kk, c, me).wait_recv()
                cp = copy(t, 3 + j, kk, c, sibling)
                cp.start()
                sent.append(cp)
        for t in range(n):
            for j, chip in enumerate(chips):
                copy(t, 3 + j, 2 * chip[0] + chip[1], 1 - c, me).wait_recv()
        for cp in sent:
            cp.wait_send()

    return pl.pallas_call(
        body, name="chip_all_gather",
        out_shape=[jax.ShapeDtypeStruct(t.shape, t.dtype) for t in ts],
        in_specs=[ANY] * n, out_specs=[ANY] * n, input_output_aliases={t: t for t in range(n)},
        scratch_shapes=[pltpu.SemaphoreType.DMA((n, 6)), pltpu.SemaphoreType.DMA((n, 6))],
    )(*ts)


def _cast_place(w, k_me):
    rows, cols = w.shape
    tr = _tile(rows, 512, 16)

    def body(k_ref, w_ref, o_ref):
        o_ref[...] = w_ref[...].astype(BF16)

    return pl.pallas_call(
        body, name="cast_place", out_shape=_sds((N_CHIP, rows, cols), BF16),
        grid_spec=pltpu.PrefetchScalarGridSpec(
            num_scalar_prefetch=1, grid=(rows // tr,),
            in_specs=[pl.BlockSpec((tr, cols), lambda i, k_ref: (i, 0))],
            out_specs=pl.BlockSpec((None, tr, cols), lambda i, k_ref: (k_ref[0], i, 0))),
        compiler_params=_cparams(("arbitrary",)),
    )(_scalar(k_me), w)


def _sibling_swap_halves(gs):
    n = len(gs)

    def body(*refs):
        ins, outs = refs[:n], refs[n:2 * n]
        send_sems, recv_sems = refs[2 * n:]
        x, y, c = _mesh_pos()
        cps = []
        for t in range(n):
            rh = gs[t].shape[1] // 2
            cp = pltpu.make_async_remote_copy(
                src_ref=ins[t].at[:, pl.ds((1 - c) * rh, rh)], dst_ref=outs[t],
                send_sem=send_sems.at[t], recv_sem=recv_sems.at[t], device_id=(x, y, 1 - c), device_id_type=MESH)
            cp.start()
            cps.append(cp)
        for cp in cps:
            cp.wait()

    return pl.pallas_call(
        body, name="sibling_swap_halves",
        out_shape=[jax.ShapeDtypeStruct((g.shape[0], g.shape[1] // 2, g.shape[2]), g.dtype) for g in gs],
        in_specs=[ANY] * n, out_specs=[ANY] * n,
        scratch_shapes=[pltpu.SemaphoreType.DMA((n,)), pltpu.SemaphoreType.DMA((n,))],
    )(*gs)


def _chip_scatter(ps):
    n = len(ps)

    def body(*refs):
        ins, outs = refs[:n], refs[n:2 * n]
        send_sems, recv_sems = refs[2 * n:]
        x, y, c = _mesh_pos()
        cps = []
        for t in range(n):
            for j, chip in enumerate(_other_chips(x, y)):
                cp = pltpu.make_async_remote_copy(
                    src_ref=ins[t].at[2 * chip[0] + chip[1]], dst_ref=outs[t].at[j],
                    send_sem=send_sems.at[t, j], recv_sem=recv_sems.at[t, j], device_id=(*chip, c), device_id_type=MESH)
                cp.start()
                cps.append(cp)
        for cp in cps:
            cp.wait()

    return pl.pallas_call(
        body, name="chip_scatter",
        out_shape=[jax.ShapeDtypeStruct((3,) + p.shape[1:], p.dtype) for p in ps],
        in_specs=[ANY] * n, out_specs=[ANY] * n,
        scratch_shapes=[pltpu.SemaphoreType.DMA((n, 3)), pltpu.SemaphoreType.DMA((n, 3))],
    )(*ps)


def _sibling_join_halves(fs):
    n = len(fs)

    def body(*refs):
        outs = refs[n:2 * n]
        send_sems, recv_sems = refs[2 * n:]
        x, y, c = _mesh_pos()
        cps = []
        for t in range(n):
            rh = fs[t].shape[0] // 2
            mine = outs[t].at[pl.ds(c * rh, rh)]
            cp = pltpu.make_async_remote_copy(
                src_ref=mine, dst_ref=mine, send_sem=send_sems.at[t], recv_sem=recv_sems.at[t],
                device_id=(x, y, 1 - c), device_id_type=MESH)
            cp.start()
            cps.append(cp)
        for t in range(n):
            rh = fs[t].shape[0] // 2
            theirs = outs[t].at[pl.ds((1 - c) * rh, rh)]
            pltpu.make_async_remote_copy(
                src_ref=theirs, dst_ref=theirs, send_sem=send_sems.at[t], recv_sem=recv_sems.at[t],
                device_id=(x, y, c), device_id_type=MESH).wait_recv()
        for cp in cps:
            cp.wait_send()

    return pl.pallas_call(
        body, name="sibling_join_halves",
        out_shape=[jax.ShapeDtypeStruct(f.shape, f.dtype) for f in fs],
        in_specs=[ANY] * n, out_specs=[ANY] * n, input_output_aliases={t: t for t in range(n)},
        scratch_shapes=[pltpu.SemaphoreType.DMA((n,)), pltpu.SemaphoreType.DMA((n,))],
    )(*fs)


_DIMS = {"nn": (((1,), (0,)), ((), ())), "nt": (((1,), (1,)), ((), ())), "tn": (((0,), (0,)), ((), ()))}


def _accum(ref, val, first):
    @pl.when(first)
    def _():
        ref[...] = val

    @pl.when(jnp.logical_not(first))
    def _():
        ref[...] += val


def _matmul(name, grid, pairs, extras, outs, acc_shape, epilogue):
    n_p, n_e, n_o = len(pairs), len(extras), len(outs)
    n_k = grid[-1]
    dims = [_DIMS[p[4]] for p in pairs]

    def body(*refs):
        ab = refs[:2 * n_p]
        ex = refs[2 * n_p:2 * n_p + n_e]
        out = refs[2 * n_p + n_e:2 * n_p + n_e + n_o]
        accs = refs[2 * n_p + n_e + n_o:]
        ids = [pl.program_id(ax) for ax in range(len(grid))]
        k = ids[-1]

        @pl.when(k == 0)
        def _():
            for acc in accs:
                acc[...] = jnp.zeros_like(acc)

        for p in range(n_p):
            a = ab[2 * p][...].astype(BF16)
            b = ab[2 * p + 1][...].astype(BF16)
            accs[p][...] += lax.dot_general(a, b, dims[p], preferred_element_type=F32)

        @pl.when(k == n_k - 1)
        def _():
            epilogue([acc[...] for acc in accs], ex, out, ids)

    in_specs = []
    operands = []
    for a, a_spec, b, b_spec, _ in pairs:
        in_specs += [a_spec, b_spec]
        operands += [a, b]
    for e, e_spec in extras:
        in_specs.append(e_spec)
        operands.append(e)
    return pl.pallas_call(
        body, name=name, grid=grid, in_specs=in_specs,
        out_specs=[o[1] for o in outs], out_shape=[o[0] for o in outs],
        scratch_shapes=[pltpu.VMEM(acc_shape, F32) for _ in pairs],
        compiler_params=_cparams(("arbitrary",) * len(grid)),
    )(*operands)


def _sds(shape, dtype):
    return jax.ShapeDtypeStruct(shape, dtype)


def _row(n):
    return pl.BlockSpec((1, n), lambda *_: (0, 0))


def _layer_norm(r):
    mu = jnp.mean(r, axis=-1, keepdims=True)
    xc = r - mu
    var = jnp.mean(xc * xc, axis=-1, keepdims=True)
    rstd = lax.rsqrt(var + LN_EPS)
    return xc * rstd, rstd


def _layer_norm_bwd(dxhat, xhat, rstd):
    m1 = jnp.mean(dxhat, axis=-1, keepdims=True)
    m2 = jnp.mean(dxhat * xhat, axis=-1, keepdims=True)
    return rstd * (dxhat - m1 - xhat * m2)


def _colsum(v):
    return jnp.sum(v, axis=0, keepdims=True)


def _fwd_in(x_bf, wg_in):
    s, d = x_bf.shape
    inc = wg_in.shape[2]
    tm, tn, tk = _tile(s, 1024), _tile(inc, 1280), _tile(d, 2048)
    nb = inc // tn

    def epi(accs, ex, out, ids):
        out[0][...] = accs[0].astype(BF16)

    return _matmul(
        "fwd_in", (s // tm, N_CHIP * nb, d // tk),
        [(x_bf, pl.BlockSpec((tm, tk), lambda i, j, k: (i, k)),
          wg_in, pl.BlockSpec((None, tk, tn), lambda i, j, k: (j // nb, k, j % nb)), "nn")],
        [], [(_sds((s, N_CHIP * inc), BF16), pl.BlockSpec((tm, tn), lambda i, j, k: (i, j)))],
        (tm, tn), epi)[0]


def _fwd_merge(y_pool, y_lru, w_pu, w_lu, z):
    s, d = y_pool.shape
    tm, tn, tk = _tile(s, 1024), _tile(d, 1024), _tile(d, 1024)
    ga0, gb0 = 3 * d // tn, 4 * d // tn

    def epi(accs, ex, out, ids):
        sa = _sigmoid(ex[0][...].astype(F32))
        sb = _sigmoid(ex[1][...].astype(F32))
        out[0][...] = (sa * accs[0] + sb * accs[1]).astype(BF16)
        out[1][...] = accs[0].astype(BF16)
        out[2][...] = accs[1].astype(BF16)

    a_spec = pl.BlockSpec((tm, tk), lambda i, j, k: (i, k))
    b_spec = pl.BlockSpec((tk, tn), lambda i, j, k: (k, j))
    o_spec = pl.BlockSpec((tm, tn), lambda i, j, k: (i, j))
    return _matmul(
        "fwd_merge", (s // tm, d // tn, d // tk),
        [(y_pool, a_spec, w_pu, b_spec, "nn"), (y_lru, a_spec, w_lu, b_spec, "nn")],
        [(z, pl.BlockSpec((tm, tn), lambda i, j, k: (i, ga0 + j))), (z, pl.BlockSpec((tm, tn), lambda i, j, k: (i, gb0 + j)))],
        [(_sds((s, d), BF16), o_spec)] * 3, (tm, tn), epi)


def _fwd_out_ln1(m, w_out, x, b_out, g1, b1):
    s, d = x.shape
    tm, tk = _tile(s, 256), _tile(d, 2048)

    def epi(accs, ex, out, ids):
        r = DN_ALPHA * ex[0][...] + accs[0] + ex[1][...]
        xhat, rstd = _layer_norm(r)
        out[0][...] = xhat
        out[1][...] = (xhat * ex[2][...] + ex[3][...]).astype(BF16)
        out[2][...] = rstd

    full = pl.BlockSpec((tm, d), lambda i, j, k: (i, 0))
    return _matmul(
        "fwd_out_ln1", (s // tm, 1, d // tk),
        [(m, pl.BlockSpec((tm, tk), lambda i, j, k: (i, k)), w_out, pl.BlockSpec((tk, d), lambda i, j, k: (k, 0)), "nn")],
        [(x, full), (b_out, _row(d)), (g1, _row(d)), (b1, _row(d))],
        [(_sds((s, d), F32), full), (_sds((s, d), BF16), full), (_sds((s, 1), F32), pl.BlockSpec((tm, 1), lambda i, j, k: (i, 0)))],
        (tm, d), epi)


def _fwd_ff1(x1_bf, wg_ff1, b_ff1):
    s, d = x1_bf.shape
    fc = wg_ff1.shape[2]
    tm, tn, tk = _tile(s, 1024), _tile(fc, 1024), _tile(d, 2048)
    nb = fc // tn

    def epi(accs, ex, out, ids):
        p = jnp.maximum(accs[0] + ex[0][...], 0.0)
        out[0][...] = (p * p).astype(BF16)

    return _matmul(
        "fwd_ff1", (s // tm, N_CHIP * nb, d // tk),
        [(x1_bf, pl.BlockSpec((tm, tk), lambda i, j, k: (i, k)),
          wg_ff1, pl.BlockSpec((None, tk, tn), lambda i, j, k: (j // nb, k, j % nb)), "nn")],
        [(b_ff1, pl.BlockSpec((1, tn), lambda i, j, k: (0, j)))],
        [(_sds((s, N_CHIP * fc), BF16), pl.BlockSpec((tm, tn), lambda i, j, k: (i, j)))],
        (tm, tn), epi)[0]


def _fwd_ff2_ln2_loss(hdn, w_ff2, xhat1, g1, b1, b_ff2, g2, b2, target):
    s, f = hdn.shape
    d = xhat1.shape[1]
    tm, tk = _tile(s, 256), _tile(f, 2048)

    def epi(accs, ex, out, ids):
        first = ids[0] == 0
        x1 = ex[0][...] * ex[1][...] + ex[2][...]
        r = DN_ALPHA * x1 + accs[0] + ex[3][...]
        xhat, rstd = _layer_norm(r)
        g2v = ex[4][...]
        err = xhat * g2v + ex[5][...] - ex[6][...]
        part = 0.5 * jnp.sum(jnp.mean(err * err, axis=-1, keepdims=True), axis=0, keepdims=True)
        dy = err * (1.0 / d)
        dr2 = _layer_norm_bwd(dy * g2v, xhat, rstd)
        out[0][...] = dr2
        out[1][...] = dr2.astype(BF16)
        _accum(out[2], _colsum(dy * xhat), first)
        _accum(out[3], _colsum(dy), first)
        _accum(out[4], _colsum(dr2), first)
        _accum(out[5], jnp.broadcast_to(part, (1, 128)), first)

    full = pl.BlockSpec((tm, d), lambda i, j, k: (i, 0))
    return _matmul(
        "fwd_ff2_ln2_loss", (s // tm, 1, f // tk),
        [(hdn, pl.BlockSpec((tm, tk), lambda i, j, k: (i, k)), w_ff2, pl.BlockSpec((tk, d), lambda i, j, k: (k, 0)), "nn")],
        [(xhat1, full), (g1, _row(d)), (b1, _row(d)), (b_ff2, _row(d)), (g2, _row(d)), (b2, _row(d)), (target, full)],
        [(_sds((s, d), F32), full), (_sds((s, d), BF16), full), (_sds((1, d), F32), _row(d)), (_sds((1, d), F32), _row(d)),
         (_sds((1, d), F32), _row(d)), (_sds((1, 128), F32), _row(128))],
        (tm, d), epi)


def _bwd_ff2_in(dr2_bf, w_ff2, hdn):
    s, d = dr2_bf.shape
    f = hdn.shape[1]
    tm, tn, tk = _tile(s, 1024), _tile(f, 1024), _tile(d, 2048)

    def epi(accs, ex, out, ids):
        dpre = accs[0] * (2.0 * jnp.sqrt(ex[0][...].astype(F32)))
        out[0][...] = dpre.astype(BF16)
        _accum(out[1], _colsum(dpre), ids[1] == 0)

    return _matmul(
        "bwd_ff2_in", (f // tn, s // tm, d // tk),
        [(dr2_bf, pl.BlockSpec((tm, tk), lambda j, i, k: (i, k)), w_ff2, pl.BlockSpec((tn, tk), lambda j, i, k: (j, k)), "nt")],
        [(hdn, pl.BlockSpec((tm, tn), lambda j, i, k: (i, j)))],
        [(_sds((s, f), BF16), pl.BlockSpec((tm, tn), lambda j, i, k: (i, j))), (_sds((1, f), F32), pl.BlockSpec((1, tn), lambda j, i, k: (0, j)))],
        (tm, tn), epi)


def _bwd_ff1_in_ln1(dpre, wg_ff1, dr2, xhat1, rstd1, g1):
    s, f = dpre.shape
    d = xhat1.shape[1]
    fc = wg_ff1.shape[2]
    tm, tk = _tile(s, 256), _tile(fc, 2048)
    nb = fc // tk

    def epi(accs, ex, out, ids):
        first = ids[0] == 0
        xhat = ex[1][...]
        dx1 = accs[0] + DN_ALPHA * ex[0][...]
        dr1 = _layer_norm_bwd(dx1 * ex[3][...], xhat, ex[2][...])
        out[0][...] = dr1
        out[1][...] = dr1.astype(BF16)
        _accum(out[2], _colsum(dx1 * xhat), first)
        _accum(out[3], _colsum(dx1), first)
        _accum(out[4], _colsum(dr1), first)

    full = pl.BlockSpec((tm, d), lambda i, j, k: (i, 0))
    return _matmul(
        "bwd_ff1_in_ln1", (s // tm, 1, f // tk),
        [(dpre, pl.BlockSpec((tm, tk), lambda i, j, k: (i, k)),
          wg_ff1, pl.BlockSpec((None, d, tk), lambda i, j, k: (k // nb, 0, k % nb)), "nt")],
        [(dr2, full), (xhat1, full), (rstd1, pl.BlockSpec((tm, 1), lambda i, j, k: (i, 0))), (g1, _row(d))],
        [(_sds((s, d), F32), full), (_sds((s, d), BF16), full), (_sds((1, d), F32), _row(d)), (_sds((1, d), F32), _row(d)),
         (_sds((1, d), F32), _row(d))],
        (tm, d), epi)


def _bwd_out_in(dr1_bf, w_out, z, pa, pb):
    s, d = dr1_bf.shape
    tm, tn, tk = _tile(s, 1024), _tile(d, 1024), _tile(d, 2048)
    ga0, gb0 = 3 * d // tn, 4 * d // tn

    def epi(accs, ex, out, ids):
        dm = accs[0]
        sa = _sigmoid(ex[0][...].astype(F32))
        sb = _sigmoid(ex[1][...].astype(F32))
        out[0][...] = (dm * sa).astype(BF16)
        out[1][...] = (dm * sb).astype(BF16)
        out[2][...] = (dm * ex[2][...].astype(F32) * sa * (1.0 - sa)).astype(BF16)
        out[3][...] = (dm * ex[3][...].astype(F32) * sb * (1.0 - sb)).astype(BF16)

    o_spec = pl.BlockSpec((tm, tn), lambda i, j, k: (i, j))
    return _matmul(
        "bwd_out_in", (s // tm, d // tn, d // tk),
        [(dr1_bf, pl.BlockSpec((tm, tk), lambda i, j, k: (i, k)), w_out, pl.BlockSpec((tn, tk), lambda i, j, k: (j, k)), "nt")],
        [(z, pl.BlockSpec((tm, tn), lambda i, j, k: (i, ga0 + j))), (z, pl.BlockSpec((tm, tn), lambda i, j, k: (i, gb0 + j))),
         (pa, o_spec), (pb, o_spec)],
        [(_sds((s, d), BF16), o_spec)] * 4, (tm, tn), epi)


def _bwd_up_in(name, dp, w_up):
    s, d = dp.shape
    n = w_up.shape[0]
    tm, tn, tk = _tile(s, 1024), _tile(n, 1024), _tile(d, 2048)

    def epi(accs, ex, out, ids):
        out[0][...] = accs[0].astype(BF16)

    return _matmul(
        name, (s // tm, n // tn, d // tk),
        [(dp, pl.BlockSpec((tm, tk), lambda i, j, k: (i, k)), w_up, pl.BlockSpec((tn, tk), lambda i, j, k: (j, k)), "nt")],
        [], [(_sds((s, n), BF16), pl.BlockSpec((tm, tn), lambda i, j, k: (i, j)))], (tm, tn), epi)[0]


def _bwd_in(dz, wg_in, dr1):
    s, d = dr1.shape
    inc = wg_in.shape[2]
    tm, tn, tk = _tile(s, 1024), _tile(d, 1024), _tile(inc, 1280)
    nb = inc // tk

    def epi(accs, ex, out, ids):
        out[0][...] = accs[0] + DN_ALPHA * ex[0][...]

    o_spec = pl.BlockSpec((tm, tn), lambda i, j, k: (i, j))
    return _matmul(
        "bwd_in", (s // tm, d // tn, N_CHIP * nb),
        [(dz, pl.BlockSpec((tm, tk), lambda i, j, k: (i, k)),
          wg_in, pl.BlockSpec((None, tn, tk), lambda i, j, k: (k // nb, j, k % nb)), "nt")],
        [(dr1, o_spec)], [(_sds((s, d), F32), o_spec)], (tm, tn), epi)[0]


def _wgrad(name, a, b, col_sharded):
    s, ka = a.shape
    n = b.shape[1]
    tm, tk = _tile(ka, 1024), _tile(s, 1024)
    tn = _tile(n // N_CHIP, 1280) if col_sharded else _tile(n, 1024)

    def epi(accs, ex, out, ids):
        out[0][...] = accs[0].astype(BF16)

    if col_sharded:
        nb = (n // N_CHIP) // tn
        o = (_sds((N_CHIP, ka, n // N_CHIP), BF16), pl.BlockSpec((None, tm, tn), lambda i, j, k: (j // nb, i, j % nb)))
    else:
        o = (_sds((ka, n), BF16), pl.BlockSpec((tm, tn), lambda i, j, k: (i, j)))
    res = _matmul(
        name, (ka // tm, n // tn, s // tk),
        [(a, pl.BlockSpec((tk, tm), lambda i, j, k: (k, i)), b, pl.BlockSpec((tk, tn), lambda i, j, k: (k, j)), "tn")],
        [], [o], (tm, tn), epi)[0]
    return res if col_sharded else res.reshape(N_CHIP, ka // N_CHIP, n)


def _chunk(s):
    return _tile(s, 512, SUBLANES)


def _zero_pads(ref, s):
    zeros = jnp.zeros((PAD, ref.shape[1]), F32)
    ref[pl.ds(0, PAD), :] = zeros
    ref[pl.ds(PAD + s, PAD), :] = zeros


def _window(ref, t0, t):
    return ref[pl.ds(t0, t + 2 * PAD), :]


def _shift(sup, off, t):
    return sup[PAD + off:PAD + off + t, :]


def _pool_count(t0, t, s, w):
    pos = t0 + lax.broadcasted_iota(jnp.int32, (t, 1), 0)
    return (jnp.minimum(pos + w // 2, s) - jnp.maximum(pos - w // 2, 0)).astype(F32)


def _pool_fwd(z, pool_w, pool_scale):
    s = z.shape[0]
    n_g, pg = pool_w.shape[0], pool_w.shape[1]
    assert n_g == len(POOL_WINDOWS) and max(POOL_WINDOWS) // 2 <= PAD
    t = _chunk(s)

    def body(u_ref, w_ref, sc_ref, d_ref, y_ref, pad_ref):
        g = pl.program_id(0)
        _zero_pads(pad_ref, s)
        pad_ref[pl.ds(PAD, s), :] = u_ref[...].astype(F32)
        for gi, w in enumerate(POOL_WINDOWS):
            @pl.when(g == gi)
            def _():
                def step(ch, carry):
                    t0 = pl.multiple_of(ch * t, t)
                    sup = _window(pad_ref, t0, t)
                    acc = _shift(sup, -(w // 2), t)
                    for o in range(-(w // 2) + 1, w // 2):
                        acc = acc + _shift(sup, o, t)
                    dd = (acc * (1.0 / _pool_count(t0, t, s, w)) - _shift(sup, 0, t)).astype(BF16)
                    d_ref[pl.ds(t0, t), :] = dd
                    y = jnp.dot(dd, w_ref[...], preferred_element_type=F32) * sc_ref[...]
                    y_ref[pl.ds(t0, t), :] = y.astype(BF16)
                    return carry

                lax.fori_loop(0, s // t, step, 0)

    blk = pl.BlockSpec((s, pg), lambda g: (0, g))
    return pl.pallas_call(
        body, name="pool_fwd", grid=(n_g,),
        in_specs=[blk, pl.BlockSpec((None, pg, pg), lambda g: (g, 0, 0)), pl.BlockSpec((1, pg), lambda g: (0, g))],
        out_specs=[blk, blk], out_shape=[_sds((s, n_g * pg), BF16)] * 2,
        scratch_shapes=[pltpu.VMEM((s + 2 * PAD, pg), F32)],
        compiler_params=_cparams(("arbitrary",)),
    )(z, pool_w, pool_scale)


def _pool_bwd(dsv, dy, pool_w, pool_scale):
    s = dsv.shape[0]
    n_g, pg = pool_w.shape[0], pool_w.shape[1]
    t = _chunk(s)

    def body(d_ref, dy_ref, w_ref, sc_ref, du_ref, dw_ref, dsc_ref, epad_ref, dwacc_ref):
        g = pl.program_id(0)
        _zero_pads(epad_ref, s)
        dwacc_ref[...] = jnp.zeros_like(dwacc_ref)
        for gi, w in enumerate(POOL_WINDOWS):
            @pl.when(g == gi)
            def _():
                def first(ch, dsc):
                    t0 = pl.multiple_of(ch * t, t)
                    dd = d_ref[pl.ds(t0, t), :]
                    dyc = dy_ref[pl.ds(t0, t), :].astype(F32)
                    wv = w_ref[...]
                    ypre = jnp.dot(dd, wv, preferred_element_type=F32)
                    dq = (dyc * sc_ref[...]).astype(BF16)
                    dwacc_ref[...] += lax.dot_general(dd, dq, _DIMS["tn"], preferred_element_type=F32)
                    ddv = lax.dot_general(dq, wv, _DIMS["nt"], preferred_element_type=F32)
                    epad_ref[pl.ds(pl.multiple_of(PAD + t0, SUBLANES), t), :] = ddv * (1.0 / _pool_count(t0, t, s, w))
                    return dsc + _colsum(dyc * ypre)

                dsc_ref[...] = lax.fori_loop(0, s // t, first, jnp.zeros((1, pg), F32))

                def second(ch, carry):
                    t0 = pl.multiple_of(ch * t, t)
                    sup = _window(epad_ref, t0, t)
                    acc = _shift(sup, -(w // 2) + 1, t)
                    for o in range(-(w // 2) + 2, w // 2 + 1):
                        acc = acc + _shift(sup, o, t)
                    du_ref[pl.ds(t0, t), :] = (acc - _shift(sup, 0, t) * _pool_count(t0, t, s, w)).astype(BF16)
                    return carry

                lax.fori_loop(0, s // t, second, 0)

        dw_ref[...] = dwacc_ref[...].astype(BF16)

    blk = pl.BlockSpec((s, pg), lambda g: (0, g))
    w_spec = pl.BlockSpec((None, pg, pg), lambda g: (g, 0, 0))
    sc_spec = pl.BlockSpec((1, pg), lambda g: (0, g))
    return pl.pallas_call(
        body, name="pool_bwd", grid=(n_g,),
        in_specs=[blk, blk, w_spec, sc_spec], out_specs=[blk, w_spec, sc_spec],
        out_shape=[_sds((s, n_g * pg), BF16), _sds((n_g, pg, pg), BF16), _sds((1, n_g * pg), F32)],
        scratch_shapes=[pltpu.VMEM((s + 2 * PAD, pg), F32), pltpu.VMEM((pg, pg), F32)],
        compiler_params=_cparams(("arbitrary",)),
    )(dsv, dy, pool_w, pool_scale)


def _sigmoid(x):
    return 0.5 * jnp.tanh(0.5 * x) + 0.5


def _softplus(x):
    e = jnp.exp(-jnp.abs(x))
    log1p_e = jnp.where(e < 1e-2, e * (1.0 - e * (0.5 - e * (1.0 / 3.0))), jnp.log(1.0 + e))
    return jnp.maximum(x, 0.0) + log1p_e


def _neg_expm1(x):
    series = 1.0 + x * (1.0 / 7.0)
    for n in (6.0, 5.0, 4.0, 3.0, 2.0):
        series = 1.0 + (x * (1.0 / n)) * series
    return jnp.where(x > -0.25, -x * series, 1.0 - jnp.exp(x))


_GELU_C = math.sqrt(2.0 / math.pi)


def _gelu(x):
    th = jnp.tanh(_GELU_C * (x + 0.044715 * x * x * x))
    return 0.5 * x * (1.0 + th), th


def _gelu_grad(x, th):
    return 0.5 * (1.0 + th) + 0.5 * x * (1.0 - th * th) * _GELU_C * (1.0 + 3.0 * 0.044715 * x * x)


def _scan_chunk(a_ref, b_ref, o_ref, o_off, carry, t, reverse):
    n = a_ref.shape[1]
    row = lax.broadcasted_iota(jnp.int32, (SUBLANES, n), 0)
    n_groups = t // SUBLANES

    def step(gi, carry):
        g = n_groups - 1 - gi if reverse else gi
        r0 = pl.multiple_of(g * SUBLANES, SUBLANES)
        a = a_ref[pl.ds(r0, SUBLANES), :]
        b = b_ref[pl.ds(r0, SUBLANES), :]
        for k in (1, 2, 4):
            keep = row < SUBLANES - k if reverse else row >= k
            sh = SUBLANES - k if reverse else k
            ar = jnp.where(keep, pltpu.roll(a, sh, 0), 1.0)
            br = jnp.where(keep, pltpu.roll(b, sh, 0), 0.0)
            b = a * br + b
            a = a * ar
        h = a * carry + b
        o_ref[pl.ds(pl.multiple_of(o_off + r0, SUBLANES), SUBLANES), :] = h
        edge = h[0:1, :] if reverse else h[SUBLANES - 1:SUBLANES, :]
        return jnp.broadcast_to(edge, h.shape)

    return lax.fori_loop(0, n_groups, step, carry)


def _lru_params(pk_ref):
    rows = pk_ref[...]
    get = lambda i: rows[i:i + 1, :]
    cw = [get(k) for k in range(4)]
    lam = (get(9), get(10))
    big_l = tuple(-LRU_C * _softplus(-v) for v in lam)
    return cw, get(4), (get(5), get(6)), (get(7), get(8)), lam, big_l


def _conv(sup, cw, cb, t):
    xc = cb + cw[0] * _shift(sup, -2, t)
    for k in range(1, 4):
        xc = xc + cw[k] * _shift(sup, k - 2, t)
    return xc


def _gates(xcb, w_ref, d, bk, ba, bx, big_l):
    pre = jnp.dot(xcb, w_ref[:, pl.ds(d * 2 * bk, 2 * bk)], preferred_element_type=F32)
    r = _sigmoid(pre[:, :bk] + ba[d])
    i = _sigmoid(pre[:, bk:] + bx[d])
    la = big_l[d] * r
    var = _neg_expm1(2.0 * la)
    rs = lax.rsqrt(jnp.maximum(var, 1e-30))
    return r, i, jnp.exp(la), var * rs, rs


def _lru_specs(s, d, bk):
    u_spec = pl.BlockSpec((s, bk), lambda h: (0, d // bk + h))
    ug_spec = pl.BlockSpec((s, bk), lambda h: (0, 2 * d // bk + h))
    w_spec = pl.BlockSpec((None, bk, 4 * bk), lambda h: (h, 0, 0))
    pk_spec = pl.BlockSpec((None, 16, bk), lambda h: (h, 0, 0))
    blk = pl.BlockSpec((s, bk), lambda h: (0, h))
    return u_spec, ug_spec, w_spec, pk_spec, blk


def _lru_fwd(z, gatew, pk):
    s = z.shape[0]
    n_h, bk = gatew.shape[0], gatew.shape[1]
    d = n_h * bk
    t = _chunk(s)
    n_ch = s // t

    def body(u_ref, ug_ref, w_ref, pk_ref, y_ref, upad, h0buf, abuf, bbuf):
        _zero_pads(upad, s)
        upad[pl.ds(PAD, s), :] = u_ref[...].astype(F32)
        cw, cb, ba, bx, _, big_l = _lru_params(pk_ref)
        zero = jnp.zeros((SUBLANES, bk), F32)

        def fill(t0, dr):
            xc = _conv(_window(upad, t0, t), cw, cb, t)
            _, i, a, sq, _ = _gates(xc.astype(BF16), w_ref, dr, bk, ba, bx, big_l)
            abuf[...] = a
            bbuf[...] = sq * i * xc

        def up(ch, carry):
            t0 = pl.multiple_of(ch * t, t)
            fill(t0, 0)
            return _scan_chunk(abuf, bbuf, h0buf, t0, carry, t, False)

        lax.fori_loop(0, n_ch, up, zero)

        def down(ci, carry):
            t0 = pl.multiple_of((n_ch - 1 - ci) * t, t)
            fill(t0, 1)
            carry = _scan_chunk(abuf, bbuf, bbuf, 0, carry, t, True)
            gl, _ = _gelu(ug_ref[pl.ds(t0, t), :].astype(F32))
            y_ref[pl.ds(t0, t), :] = ((h0buf[pl.ds(t0, t), :] + bbuf[...]) * gl).astype(BF16)
            return carry

        lax.fori_loop(0, n_ch, down, zero)

    u_spec, ug_spec, w_spec, pk_spec, blk = _lru_specs(s, d, bk)
    return pl.pallas_call(
        body, name="lru_fwd", grid=(n_h,),
        in_specs=[u_spec, ug_spec, w_spec, pk_spec], out_specs=blk, out_shape=_sds((s, d), BF16),
        scratch_shapes=[pltpu.VMEM((s + 2 * PAD, bk), F32), pltpu.VMEM((s, bk), F32), pltpu.VMEM((t, bk), F32), pltpu.VMEM((t, bk), F32)],
        compiler_params=_cparams(("arbitrary",)),
    )(z, z, gatew, pk)


def _lru_grads(lam_, hnb, a, sq, rs, r, i, xc, xcb, w_ref, dwacc, d, big_l, acc):
    bk = xc.shape[1]
    dba, dbx, dl = acc
    q = lam_ * i * xc
    dla = lam_ * hnb * a - q * (a * a) * rs
    dpr = dla * big_l * r * (1.0 - r)
    dpi = q * sq * (1.0 - i)
    dprb, dpib = dpr.astype(BF16), dpi.astype(BF16)
    c0 = d * 2 * bk
    dxc = (lam_ * sq * i
           + lax.dot_general(dprb, w_ref[:, pl.ds(c0, bk)], _DIMS["nt"], preferred_element_type=F32)
           + lax.dot_general(dpib, w_ref[:, pl.ds(c0 + bk, bk)], _DIMS["nt"], preferred_element_type=F32))
    dwacc[:, pl.ds(c0, bk)] += lax.dot_general(xcb, dprb, _DIMS["tn"], preferred_element_type=F32)
    dwacc[:, pl.ds(c0 + bk, bk)] += lax.dot_general(xcb, dpib, _DIMS["tn"], preferred_element_type=F32)
    return dxc, (dba + _colsum(dpr), dbx + _colsum(dpi), dl + _colsum(dla * r))


def _lru_bwd(z, dy, gatew, pk):
    s = z.shape[0]
    n_h, bk = gatew.shape[0], gatew.shape[1]
    d = n_h * bk
    t = _chunk(s)
    n_ch = s // t

    def body(u_ref, ug_ref, dy_ref, w_ref, pk_ref, du_ref, dug_ref, dw_ref, dpk_ref,
             upad, h0pad, h1pad, dxpad, abuf, bbuf, lbuf, dwacc, edge):
        for ref in (upad, h0pad, h1pad, dxpad):
            _zero_pads(ref, s)
        upad[pl.ds(PAD, s), :] = u_ref[...].astype(F32)
        dwacc[...] = jnp.zeros_like(dwacc)
        cw, cb, ba, bx, lam, big_l = _lru_params(pk_ref)
        zero = jnp.zeros((SUBLANES, bk), F32)
        zrow = jnp.zeros((1, bk), F32)
        rowi = lax.broadcasted_iota(jnp.int32, (t, bk), 0)

        def at(t0):
            return pl.ds(pl.multiple_of(PAD + t0, SUBLANES), t)

        def conv_in(t0):
            xc = _conv(_window(upad, t0, t), cw, cb, t)
            return xc, xc.astype(BF16)

        def dh_of(t0):
            ug = ug_ref[pl.ds(t0, t), :].astype(F32)
            gl, th = _gelu(ug)
            dyv = dy_ref[pl.ds(t0, t), :].astype(F32)
            return dyv * gl, dyv * _gelu_grad(ug, th)

        def sweep1(ch, carry):
            t0 = pl.multiple_of(ch * t, t)
            xc, xcb = conv_in(t0)
            _, i, a, sq, _ = _gates(xcb, w_ref, 0, bk, ba, bx, big_l)
            abuf[...] = a
            bbuf[...] = sq * i * xc
            return _scan_chunk(abuf, bbuf, h0pad, PAD + t0, carry, t, False)

        lax.fori_loop(0, n_ch, sweep1, zero)

        edge[...] = zero

        def sweep2(ci, st):
            carry_h, carry_l, acc = st
            t0 = pl.multiple_of((n_ch - 1 - ci) * t, t)
            xc, xcb = conv_in(t0)
            _, i1, a1, sq1, _ = _gates(xcb, w_ref, 1, bk, ba, bx, big_l)
            abuf[...] = a1
            bbuf[...] = sq1 * i1 * xc
            carry_h = _scan_chunk(abuf, bbuf, h1pad, PAD + t0, carry_h, t, True)
            dh, dgl = dh_of(t0)
            dug_ref[pl.ds(t0, t), :] = (dgl * (h0pad[at(t0), :] + h1pad[at(t0), :])).astype(BF16)
            r0, i0, a0, sq0, rs0 = _gates(xcb, w_ref, 0, bk, ba, bx, big_l)
            abuf[...] = jnp.where(rowi == t - 1, edge[0:1, :], pltpu.roll(a0, t - 1, 0))
            bbuf[...] = dh
            carry_l = _scan_chunk(abuf, bbuf, lbuf, 0, carry_l, t, True)
            edge[...] = jnp.broadcast_to(a0[0:1, :], (SUBLANES, bk))
            hprev = _shift(_window(h0pad, t0, t), -1, t)
            dxc, acc = _lru_grads(lbuf[...], hprev, a0, sq0, rs0, r0, i0, xc, xcb, w_ref, dwacc, 0, big_l[0], acc)
            dxpad[at(t0), :] = dxc
            return carry_h, carry_l, acc

        _, _, acc0 = lax.fori_loop(0, n_ch, sweep2, (zero, zero, (zrow, zrow, zrow)))

        edge[...] = zero

        def sweep3(ch, st):
            carry_l, acc = st
            t0 = pl.multiple_of(ch * t, t)
            xc, xcb = conv_in(t0)
            r1, i1, a1, sq1, rs1 = _gates(xcb, w_ref, 1, bk, ba, bx, big_l)
            dh, _ = dh_of(t0)
            abuf[...] = jnp.where(rowi == 0, edge[0:1, :], pltpu.roll(a1, 1, 0))
            bbuf[...] = dh
            carry_l = _scan_chunk(abuf, bbuf, lbuf, 0, carry_l, t, False)
            edge[...] = jnp.broadcast_to(a1[t - 1:t, :], (SUBLANES, bk))
            hnext = _shift(_window(h1pad, t0, t), 1, t)
            dxc, acc = _lru_grads(lbuf[...], hnext, a1, sq1, rs1, r1, i1, xc, xcb, w_ref, dwacc, 1, big_l[1], acc)
            dxpad[at(t0), :] += dxc
            return carry_l, acc

        _, acc1 = lax.fori_loop(0, n_ch, sweep3, (zero, (zrow, zrow, zrow)))

        def sweep4(ch, st):
            t0 = pl.multiple_of(ch * t, t)
            sdx = _window(dxpad, t0, t)
            su = _window(upad, t0, t)
            dxc = _shift(sdx, 0, t)
            du = cw[0] * _shift(sdx, 2, t) + cw[1] * _shift(sdx, 1, t) + cw[2] * dxc + cw[3] * _shift(sdx, -1, t)
            du_ref[pl.ds(t0, t), :] = du.astype(BF16)
            return tuple(st[k] + _colsum(dxc * _shift(su, k - 2, t)) for k in range(4)) + (st[4] + _colsum(dxc),)

        conv_g = lax.fori_loop(0, n_ch, sweep4, (zrow,) * 5)

        dpk_ref[...] = jnp.zeros_like(dpk_ref)
        rows = list(conv_g) + [acc0[0], acc1[0], acc0[1], acc1[1],
                               acc0[2] * LRU_C * _sigmoid(-lam[0]), acc1[2] * LRU_C * _sigmoid(-lam[1])]
        for k, v in enumerate(rows):
            dpk_ref[pl.ds(k, 1), :] = v
        dw_ref[...] = dwacc[...].astype(BF16)

    u_spec, ug_spec, w_spec, pk_spec, blk = _lru_specs(s, d, bk)
    padded = pltpu.VMEM((s + 2 * PAD, bk), F32)
    chunk = pltpu.VMEM((t, bk), F32)
    return pl.pallas_call(
        body, name="lru_bwd", grid=(n_h,),
        in_specs=[u_spec, ug_spec, blk, w_spec, pk_spec], out_specs=[blk, blk, w_spec, pk_spec],
        out_shape=[_sds((s, d), BF16), _sds((s, d), BF16), _sds((n_h, bk, 4 * bk), BF16), _sds((n_h, 16, bk), F32)],
        scratch_shapes=[padded, padded, padded, padded, chunk, chunk, chunk, pltpu.VMEM((bk, 4 * bk), F32),
                        pltpu.VMEM((SUBLANES, bk), F32)],
        compiler_params=_cparams(("arbitrary",)),
    )(z, z, dy, gatew, pk)


def _scalar(v):
    return jnp.reshape(v, (1,)).astype(jnp.int32)


def _add_sibling(g, r, c):
    _, rows, cols = g.shape
    rh = rows // 2
    tr = _tile(rh, 512, 16)
    nr = rh // tr

    def body(c_ref, g_ref, r_ref, o_ref):
        o_ref[...] = (g_ref[...].astype(F32) + r_ref[...].astype(F32)).astype(BF16)

    spec = pl.BlockSpec((None, tr, cols), lambda k, i, c_ref: (k, i, 0))
    return pl.pallas_call(
        body, name="add_sibling", out_shape=_sds((N_CHIP, rh, cols), BF16),
        grid_spec=pltpu.PrefetchScalarGridSpec(
            num_scalar_prefetch=1, grid=(N_CHIP, nr),
            in_specs=[pl.BlockSpec((None, tr, cols), lambda k, i, c_ref: (k, c_ref[0] * nr + i, 0)), spec], out_specs=spec),
        compiler_params=_cparams(("arbitrary", "arbitrary")),
    )(_scalar(c), g, r)


def _sum_chips(p, rcv, k_me, c):
    _, rh, cols = p.shape
    tr = _tile(rh, 512, 16)
    nr = rh // tr

    def body(kc_ref, p_ref, r_ref, o_ref):
        acc = p_ref[...].astype(F32)
        for j in range(3):
            acc = acc + r_ref[j].astype(F32)
        o_ref[...] = acc

    return pl.pallas_call(
        body, name="sum_chips", out_shape=_sds((2 * rh, cols), F32),
        grid_spec=pltpu.PrefetchScalarGridSpec(
            num_scalar_prefetch=1, grid=(nr,),
            in_specs=[pl.BlockSpec((None, tr, cols), lambda i, kc_ref: (kc_ref[0], i, 0)),
                      pl.BlockSpec((3, tr, cols), lambda i, kc_ref: (0, i, 0))],
            out_specs=pl.BlockSpec((tr, cols), lambda i, kc_ref: (kc_ref[1] * nr + i, 0))),
        compiler_params=_cparams(("arbitrary",)),
    )(jnp.stack([k_me, c]).astype(jnp.int32), p, rcv)


def _sum_devices(g):
    def body(g_ref, o_ref):
        acc = g_ref[0]
        for dev in range(1, N_DEV):
            acc = acc + g_ref[dev]
        o_ref[...] = acc

    return pl.pallas_call(body, name="sum_devices", out_shape=_sds(g.shape[1:], F32))(g)


def _adamw(w, g, m, v):
    rows, cols = w.shape
    tr = _tile(rows, 256, SUBLANES)

    def body(w_ref, g_ref, m_ref, v_ref, d_ref, nm_ref, nv_ref):
        gv = g_ref[...]
        nm = ADAM_B1 * m_ref[...] + (1.0 - ADAM_B1) * gv
        nv = ADAM_B2 * v_ref[...] + (1.0 - ADAM_B2) * (gv * gv)
        m_hat = nm / (1.0 - ADAM_B1 ** ADAM_STEP)
        v_hat = nv / (1.0 - ADAM_B2 ** ADAM_STEP)
        d_ref[...] = -ADAM_LR * (m_hat / (jnp.sqrt(v_hat) + ADAM_EPS) + ADAM_WD * w_ref[...])
        nm_ref[...] = nm
        nv_ref[...] = nv

    spec = pl.BlockSpec((tr, cols), lambda i: (i, 0))
    return pl.pallas_call(
        body, name="adamw", grid=(rows // tr,), in_specs=[spec] * 4, out_specs=[spec] * 3,
        out_shape=[_sds((rows, cols), F32)] * 3, compiler_params=_cparams(("arbitrary",)),
    )(w, g, m, v)


def _pack(vs, unit):
    flat = jnp.concatenate([v.reshape(-1).astype(F32) for v in vs])
    pad = (-flat.shape[0]) % unit
    if pad:
        flat = jnp.concatenate([flat, jnp.zeros((pad,), F32)])
    return flat.reshape(-1, 128)


def _unpack(p, like):
    flat = p.reshape(-1)
    out, off = [], 0
    for v in like:
        n = math.prod(v.shape)
        out.append(flat[off:off + n].reshape(v.shape))
        off += n
    return out


def kernel(x, w_in, pool_w, pool_scale, conv_w, conv_b, lru_wa, lru_ba, lru_wx, lru_bx, lru_lambda, w_pool_up, w_lru_up, w_out, b_out, ln1_g, ln1_b, w_ff1, b_ff1, w_ff2, b_ff2, ln2_g, ln2_b, loss_target, m_w_in, m_pool_w, m_pool_scale, m_conv_w, m_conv_b, m_lru_wa, m_lru_ba, m_lru_wx, m_lru_bx, m_lru_lambda, m_w_pool_up, m_w_lru_up, m_w_out, m_b_out, m_ln1_g, m_ln1_b, m_w_ff1, m_b_ff1, m_w_ff2, m_b_ff2, m_ln2_g, m_ln2_b, v_w_in, v_pool_w, v_pool_scale, v_conv_w, v_conv_b, v_lru_wa, v_lru_ba, v_lru_wx, v_lru_bx, v_lru_lambda, v_w_pool_up, v_w_lru_up, v_w_out, v_b_out, v_ln1_g, v_ln1_b, v_w_ff1, v_b_ff1, v_w_ff2, v_b_ff2, v_ln2_g, v_ln2_b):
    given = dict(locals())
    wt = {n: given[n] for n in WEIGHTS}
    mom = {n: given["m_" + n] for n in WEIGHTS}
    vel = {n: given["v_" + n] for n in WEIGHTS}

    ix, iy, ic = _mesh_pos()
    k_me = 2 * ix + iy
    s, d = x.shape[1], x.shape[2]
    ds = d // N_CHIP
    n_g, pgs, pg = pool_w.shape[1], pool_w.shape[2], pool_w.shape[3]
    n_h, bks, bk = lru_wa.shape[2], lru_wa.shape[3], lru_wa.shape[4]
    f = b_ff1.shape[1]
    x2 = x[0]
    x_bf = x2.astype(BF16)
    vec = lambda a: a.reshape(1, -1)

    sharded_vecs = [conv_w[0], lru_ba[0], lru_bx[0], lru_lambda[0]]
    rows_sv = jnp.concatenate(sharded_vecs + [jnp.zeros((6, ds), F32)], axis=0)
    sv = _all_gather_small(rows_sv)
    sv = sv.reshape(N_CHIP, 2, 16, ds)[:, 0].transpose(1, 0, 2).reshape(16, d)
    conv_w_f, ba_f, bx_f, lam_f = sv[0:4], sv[4:6], sv[6:8], sv[8:10]
    pk = jnp.concatenate([conv_w_f, conv_b, ba_f, bx_f, lam_f, jnp.zeros((5, d), F32)], axis=0)
    pk = pk.reshape(16, n_h, bk).transpose(1, 0, 2)

    def gate_stack(wa, wx):
        return jnp.stack([wa[0], wx[0]], axis=1)

    mats = {
        "w_in": w_in[0], "w_pool_up": w_pool_up[0], "w_lru_up": w_lru_up[0], "w_out": w_out[0],
        "w_ff1": w_ff1[0], "w_ff2": w_ff2[0],
        "pool_w": pool_w[0].reshape(n_g * pgs, pg),
        "gate_w": gate_stack(lru_wa, lru_wx).reshape(4 * n_h * bks, bk),
    }
    names = list(mats)
    gathered = dict(zip(names, _chip_all_gather([_cast_place(mats[n], k_me) for n in names])))
    wg_in, wg_ff1 = gathered["w_in"], gathered["w_ff1"]
    wf_pu, wf_lu, wf_out = (gathered[n].reshape(d, d) for n in ("w_pool_up", "w_lru_up", "w_out"))
    wf_ff2 = gathered["w_ff2"].reshape(f, d)
    wf_pool = gathered["pool_w"].reshape(N_CHIP, n_g, pgs, pg).transpose(1, 0, 2, 3).reshape(n_g, pg, pg)
    wf_gate = (gathered["gate_w"].reshape(N_CHIP, 2, 2, n_h, bks, bk).transpose(3, 0, 4, 1, 2, 5)
               .reshape(n_h, bk, 4 * bk))

    z = _fwd_in(x_bf, wg_in)
    d_pool, y_pool = _pool_fwd(z, wf_pool, pool_scale)
    y_lru = _lru_fwd(z, wf_gate, pk)
    m_mix, p_a, p_b = _fwd_merge(y_pool, y_lru, wf_pu, wf_lu, z)
    xhat1, x1_bf, rstd1 = _fwd_out_ln1(m_mix, wf_out, x2, b_out, ln1_g, ln1_b)
    hdn = _fwd_ff1(x1_bf, wg_ff1, b_ff1)
    dr2, dr2_bf, g_ln2_g, g_ln2_b, g_b_ff2, loss_part = _fwd_ff2_ln2_loss(
        hdn, wf_ff2, xhat1, ln1_g, ln1_b, b_ff2, ln2_g, ln2_b, loss_target[0])

    dpre, g_b_ff1 = _bwd_ff2_in(dr2_bf, wf_ff2, hdn)
    gw = {"w_ff2": _wgrad("wgrad_ff2", hdn, dr2_bf, False)}
    dr1, dr1_bf, g_ln1_g, g_ln1_b, g_b_out = _bwd_ff1_in_ln1(dpre, wg_ff1, dr2, xhat1, rstd1, ln1_g)
    gw["w_ff1"] = _wgrad("wgrad_ff1", x1_bf, dpre, True)
    dp_a, dp_b, dg_a, dg_b = _bwd_out_in(dr1_bf, wf_out, z, p_a, p_b)
    gw["w_out"] = _wgrad("wgrad_out", m_mix, dr1_bf, False)
    dy_pool = _bwd_up_in("bwd_pool_up_in", dp_a, wf_pu)
    dy_lru = _bwd_up_in("bwd_lru_up_in", dp_b, wf_lu)
    gw["w_pool_up"] = _wgrad("wgrad_pool_up", y_pool, dp_a, False)
    gw["w_lru_up"] = _wgrad("wgrad_lru_up", y_lru, dp_b, False)
    du_pool, g_pool_w, g_pool_scale = _pool_bwd(d_pool, dy_pool, wf_pool, pool_scale)
    du_lru, du_gate, g_gate_w, g_pk = _lru_bwd(z, dy_lru, wf_gate, pk)
    dz = jnp.concatenate([du_pool, du_lru, du_gate, dg_a, dg_b], axis=1)
    grad_x = _bwd_in(dz, wg_in, dr1)
    gw["w_in"] = _wgrad("wgrad_in", x_bf, dz, True)
    gw["pool_w"] = (g_pool_w.reshape(n_g, N_CHIP, pgs, pg).transpose(1, 0, 2, 3).reshape(N_CHIP, n_g * pgs, pg))
    gw["gate_w"] = (g_gate_w.reshape(n_h, N_CHIP, bks, 2, 2, bk).transpose(1, 3, 4, 0, 2, 5)
                    .reshape(N_CHIP, 4 * n_h * bks, bk))

    parts = [gw[n] for n in names]
    from_sibling = _sibling_swap_halves(parts)
    chip_sums = [_add_sibling(g, r, ic) for g, r in zip(parts, from_sibling)]
    from_chips = _chip_scatter(chip_sums)
    halves = [_sum_chips(p, r, k_me, ic) for p, r in zip(chip_sums, from_chips)]
    g_mat = dict(zip(names, _sibling_join_halves(halves)))

    def stacked(tree):
        return gate_stack(tree["lru_wa"], tree["lru_wx"]).reshape(4 * n_h * bks, bk)

    res = {}
    for n in names:
        if n == "gate_w":
            upd = _adamw(stacked(wt), g_mat[n], stacked(mom), stacked(vel))
            outs = [o.reshape(2, 2, n_h, bks, bk) for o in (g_mat[n],) + tuple(upd)]
            res["lru_wa"] = [o[:, 0][None] for o in outs]
            res["lru_wx"] = [o[:, 1][None] for o in outs]
        else:
            shp = wt[n].shape
            upd = _adamw(wt[n].reshape(mats[n].shape), g_mat[n], mom[n].reshape(mats[n].shape), vel[n].reshape(mats[n].shape))
            res[n] = [o.reshape(shp) for o in (g_mat[n],) + tuple(upd)]

    g_pk = g_pk.transpose(1, 0, 2).reshape(16, d)
    vec_full = {
        "pool_scale": g_pool_scale, "conv_w": g_pk[0:4], "conv_b": g_pk[4:5],
        "lru_ba": g_pk[5:7], "lru_bx": g_pk[7:9], "lru_lambda": g_pk[9:11],
        "b_out": g_b_out, "ln1_g": g_ln1_g, "ln1_b": g_ln1_b, "b_ff1": g_b_ff1, "b_ff2": g_b_ff2,
        "ln2_g": g_ln2_g, "ln2_b": g_ln2_b,
    }
    vnames = list(vec_full)
    vg = _sum_devices(_all_gather_small(_pack([vec_full[n] for n in vnames], 1024)))
    vg = dict(zip(vnames, _unpack(vg, [vec_full[n] for n in vnames])))
    for n in ("conv_w", "lru_ba", "lru_bx", "lru_lambda"):
        vg[n] = lax.dynamic_slice_in_dim(vg[n], k_me * ds, ds, axis=1)
    vg = {n: vg[n].reshape(wt[n].shape) for n in vnames}
    upd = _adamw(_pack([wt[n] for n in vnames], 1024), _pack([vg[n] for n in vnames], 1024),
                 _pack([mom[n] for n in vnames], 1024), _pack([vel[n] for n in vnames], 1024))
    upd = [_unpack(u, [wt[n] for n in vnames]) for u in upd]
    for i, n in enumerate(vnames):
        res[n] = [vg[n], upd[0][i], upd[1][i], upd[2][i]]

    loss = lax.psum(loss_part[0, 0], ("x", "y", "c"))
    return (loss, grad_x[None], *[res[n][0] for n in WEIGHTS], *[res[n][1] for n in WEIGHTS],
            *[res[n][2] for n in WEIGHTS], *[res[n][3] for n in WEIGHTS])
```

```python
import functools
import math

import jax
import jax.numpy as jnp
from jax import lax
from jax.experimental import pallas as pl
from jax.experimental.pallas import tpu as pltpu

F32 = jnp.float32
BF16 = jnp.bfloat16
MESH = pl.DeviceIdType.MESH
ANY = pl.BlockSpec(memory_space=pl.ANY)

N_CHIP = 4
N_DEV = 8
VMEM_LIMIT_BYTES = 56 * 1024 * 1024
SUBLANES = 8
PAD = 8
SCAN_UNROLL = 8

POOL_WINDOWS = (2, 4, 8, 16)
LRU_C = 8.0
DN_ALPHA = 2.0 ** 0.25
LN_EPS = 1e-5
ADAM_LR, ADAM_B1, ADAM_B2, ADAM_EPS, ADAM_WD, ADAM_STEP = 0.001, 0.9, 0.999, 1e-08, 0.01, 10

WEIGHTS = ("w_in", "pool_w", "pool_scale", "conv_w", "conv_b", "lru_wa", "lru_ba", "lru_wx", "lru_bx", "lru_lambda",
           "w_pool_up", "w_lru_up", "w_out", "b_out", "ln1_g", "ln1_b", "w_ff1", "b_ff1", "w_ff2", "b_ff2", "ln2_g", "ln2_b")


def _cparams(sem=None):
    return pltpu.CompilerParams(dimension_semantics=sem, vmem_limit_bytes=VMEM_LIMIT_BYTES)


def _tile(dim, pref, unit=128):
    if dim <= pref:
        return dim
    t = (pref // unit) * unit
    while t > unit and dim % t:
        t -= unit
    assert dim % t == 0, (dim, pref)
    return t


def _mesh_pos():
    x, y, c = lax.axis_index("x"), lax.axis_index("y"), lax.axis_index("c")
    return x, y, c


def _other_chips(x, y):
    return [(1 - x, y), (x, 1 - y), (1 - x, 1 - y)]


def _all_gather_small(v):
    m_per, n = v.shape

    def body(x_ref, out_ref, send_sems, recv_sems, local_sem):
        x, y, c = _mesh_pos()
        me, sibling = (x, y, c), (x, y, 1 - c)
        chips = _other_chips(x, y)

        def rows(px, py, pc):
            return out_ref.at[4 * px + 2 * py + pc]

        def copy(k, block, to, src=None):
            return pltpu.make_async_remote_copy(
                src_ref=rows(*block) if src is None else src, dst_ref=rows(*block),
                send_sem=send_sems.at[k], recv_sem=recv_sems.at[k], device_id=to, device_id_type=MESH)

        mine = pltpu.make_async_copy(x_ref, rows(*me), local_sem)
        mine.start()
        first = [copy(0, me, sibling, src=x_ref)]
        first += [copy(1 + j, me, (*chip, c), src=x_ref) for j, chip in enumerate(chips)]
        for cp in first:
            cp.start()
        passed = [copy(4 + j, (*chip, c), sibling) for j, chip in enumerate(chips)]
        for j, chip in enumerate(chips):
            copy(1 + j, (*chip, c), me).wait_recv()
            passed[j].start()
        copy(0, sibling, me).wait_recv()
        for j, chip in enumerate(chips):
            copy(4 + j, (*chip, 1 - c), me).wait_recv()
        for cp in first + passed:
            cp.wait_send()
        mine.wait()

    return pl.pallas_call(
        body, name="all_gather_small",
        out_shape=jax.ShapeDtypeStruct((N_DEV, m_per, n), v.dtype),
        in_specs=[pl.BlockSpec(memory_space=pltpu.VMEM)],
        out_specs=pl.BlockSpec(memory_space=pltpu.VMEM),
        scratch_shapes=[pltpu.SemaphoreType.DMA((7,)), pltpu.SemaphoreType.DMA((7,)), pltpu.SemaphoreType.DMA],
    )(v)


def _chip_all_gather(ts):
    n = len(ts)

    def body(*refs):
        outs = refs[n:2 * n]
        send_sems, recv_sems = refs[2 * n:]
        x, y, c = _mesh_pos()
        me, sibling = (x, y, c), (x, y, 1 - c)
        k_me = 2 * x + y
        chips = _other_chips(x, y)

        def half(t, which):
            rh = ts[t].shape[1] // 2
            return pl.ds(which * rh, rh)

        def copy(t, s, kk, which, to, src=None):
            dst = outs[t].at[kk, half(t, which)]
            return pltpu.make_async_remote_copy(
                src_ref=dst if src is None else src, dst_ref=dst,
                send_sem=send_sems.at[t, s], recv_sem=recv_sems.at[t, s], device_id=to, device_id_type=MESH)

        sent = []
        for t in range(n):
            for j, chip in enumerate(chips):
                cp = copy(t, j, k_me, c, (*chip, c))
                cp.start()
                sent.append(cp)
        for t in range(n):
            for j, chip in enumerate(chips):
                kk = 2 * chip[0] + chip[1]
                copy(t, j, kk, c, me).wait_recv()
                cp = copy(t, 3 + j, kk, c, sibling)
                cp.start()
                sent.append(cp)
        for t in range(n):
            for j, chip in enumerate(chips):
                copy(t, 3 + j, 2 * chip[0] + chip[1], 1 - c, me).wait_recv()
        for cp in sent:
            cp.wait_send()

    return pl.pallas_call(
        body, name="chip_all_gather",
        out_shape=[jax.ShapeDtypeStruct(t.shape, t.dtype) for t in ts],
        in_specs=[ANY] * n, out_specs=[ANY] * n, input_output_aliases={t: t for t in range(n)},
        scratch_shapes=[pltpu.SemaphoreType.DMA((n, 6)), pltpu.SemaphoreType.DMA((n, 6))],
    )(*ts)


def _cast_place(w, k_me):
    rows, cols = w.shape
    tr = _tile(rows, 512, 16)

    def body(k_ref, w_ref, o_ref):
        o_ref[...] = w_ref[...].astype(BF16)

    return pl.pallas_call(
        body, name="cast_place", out_shape=_sds((N_CHIP, rows, cols), BF16),
        grid_spec=pltpu.PrefetchScalarGridSpec(
            num_scalar_prefetch=1, grid=(rows // tr,),
            in_specs=[pl.BlockSpec((tr, cols), lambda i, k_ref: (i, 0))],
            out_specs=pl.BlockSpec((None, tr, cols), lambda i, k_ref: (k_ref[0], i, 0))),
        compiler_params=_cparams(("arbitrary",)),
    )(_scalar(k_me), w)


def _sibling_swap_halves(gs):
    n = len(gs)

    def body(*refs):
        ins, outs = refs[:n], refs[n:2 * n]
        send_sems, recv_sems = refs[2 * n:]
        x, y, c = _mesh_pos()
        cps = []
        for t in range(n):
            rh = gs[t].shape[1] // 2
            cp = pltpu.make_async_remote_copy(
                src_ref=ins[t].at[:, pl.ds((1 - c) * rh, rh)], dst_ref=outs[t],
                send_sem=send_sems.at[t], recv_sem=recv_sems.at[t], device_id=(x, y, 1 - c), device_id_type=MESH)
            cp.start()
            cps.append(cp)
        for cp in cps:
            cp.wait()

    return pl.pallas_call(
        body, name="sibling_swap_halves",
        out_shape=[jax.ShapeDtypeStruct((g.shape[0], g.shape[1] // 2, g.shape[2]), g.dtype) for g in gs],
        in_specs=[ANY] * n, out_specs=[ANY] * n,
        scratch_shapes=[pltpu.SemaphoreType.DMA((n,)), pltpu.SemaphoreType.DMA((n,))],
    )(*gs)


def _chip_scatter(ps):
    n = len(ps)

    def body(*refs):
        ins, outs = refs[:n], refs[n:2 * n]
        send_sems, recv_sems = refs[2 * n:]
        x, y, c = _mesh_pos()
        cps = []
        for t in range(n):
            for j, chip in enumerate(_other_chips(x, y)):
                cp = pltpu.make_async_remote_copy(
                    src_ref=ins[t].at[2 * chip[0] + chip[1]], dst_ref=outs[t].at[j],
                    send_sem=send_sems.at[t, j], recv_sem=recv_sems.at[t, j], device_id=(*chip, c), device_id_type=MESH)
                cp.start()
                cps.append(cp)
        for cp in cps:
            cp.wait()

    return pl.pallas_call(
        body, name="chip_scatter",
        out_shape=[jax.ShapeDtypeStruct((3,) + p.shape[1:], p.dtype) for p in ps],
        in_specs=[ANY] * n, out_specs=[ANY] * n,
        scratch_shapes=[pltpu.SemaphoreType.DMA((n, 3)), pltpu.SemaphoreType.DMA((n, 3))],
    )(*ps)


def _sibling_join_halves(fs):
    n = len(fs)

    def body(*refs):
        outs = refs[n:2 * n]
        send_sems, recv_sems = refs[2 * n:]
        x, y, c = _mesh_pos()
        cps = []
        for t in range(n):
            rh = fs[t].shape[0] // 2
            mine = outs[t].at[pl.ds(c * rh, rh)]
            cp = pltpu.make_async_remote_copy(
                src_ref=mine, dst_ref=mine, send_sem=send_sems.at[t], recv_sem=recv_sems.at[t],
                device_id=(x, y, 1 - c), device_id_type=MESH)
            cp.start()
            cps.append(cp)
        for t in range(n):
            rh = fs[t].shape[0] // 2
            theirs = outs[t].at[pl.ds((1 - c) * rh, rh)]
            pltpu.make_async_remote_copy(
                src_ref=theirs, dst_ref=theirs, send_sem=send_sems.at[t], recv_sem=recv_sems.at[t],
                device_id=(x, y, c), device_id_type=MESH).wait_recv()
        for cp in cps:
            cp.wait_send()

    return pl.pallas_call(
        body, name="sibling_join_halves",
        out_shape=[jax.ShapeDtypeStruct(f.shape, f.dtype) for f in fs],
        in_specs=[ANY] * n, out_specs=[ANY] * n, input_output_aliases={t: t for t in range(n)},
        scratch_shapes=[pltpu.SemaphoreType.DMA((n,)), pltpu.SemaphoreType.DMA((n,))],
    )(*fs)


_DIMS = {"nn": (((1,), (0,)), ((), ())), "nt": (((1,), (1,)), ((), ())), "tn": (((0,), (0,)), ((), ()))}


def _accum(ref, val, first):
    @pl.when(first)
    def _():
        ref[...] = val

    @pl.when(jnp.logical_not(first))
    def _():
        ref[...] += val


def _matmul(name, grid, pairs, extras, outs, acc_shape, epilogue):
    n_p, n_e, n_o = len(pairs), len(extras), len(outs)
    n_k = grid[-1]
    dims = [_DIMS[p[4]] for p in pairs]

    def body(*refs):
        ab = refs[:2 * n_p]
        ex = refs[2 * n_p:2 * n_p + n_e]
        out = refs[2 * n_p + n_e:2 * n_p + n_e + n_o]
        accs = refs[2 * n_p + n_e + n_o:]
        ids = [pl.program_id(ax) for ax in range(len(grid))]
        k = ids[-1]

        @pl.when(k == 0)
        def _():
            for acc in accs:
                acc[...] = jnp.zeros_like(acc)

        for p in range(n_p):
            a = ab[2 * p][...].astype(BF16)
            b = ab[2 * p + 1][...].astype(BF16)
            accs[p][...] += lax.dot_general(a, b, dims[p], preferred_element_type=F32)

        @pl.when(k == n_k - 1)
        def _():
            epilogue([acc[...] for acc in accs], ex, out, ids)

    in_specs = []
    operands = []
    for a, a_spec, b, b_spec, _ in pairs:
        in_specs += [a_spec, b_spec]
        operands += [a, b]
    for e, e_spec in extras:
        in_specs.append(e_spec)
        operands.append(e)
    return pl.pallas_call(
        body, name=name, grid=grid, in_specs=in_specs,
        out_specs=[o[1] for o in outs], out_shape=[o[0] for o in outs],
        scratch_shapes=[pltpu.VMEM(acc_shape, F32) for _ in pairs],
        compiler_params=_cparams(("arbitrary",) * len(grid)),
    )(*operands)


def _sds(shape, dtype):
    return jax.ShapeDtypeStruct(shape, dtype)


def _row(n):
    return pl.BlockSpec((1, n), lambda *_: (0, 0))


def _layer_norm(r):
    mu = jnp.mean(r, axis=-1, keepdims=True)
    xc = r - mu
    var = jnp.mean(xc * xc, axis=-1, keepdims=True)
    rstd = lax.rsqrt(var + LN_EPS)
    return xc * rstd, rstd


def _layer_norm_bwd(dxhat, xhat, rstd):
    m1 = jnp.mean(dxhat, axis=-1, keepdims=True)
    m2 = jnp.mean(dxhat * xhat, axis=-1, keepdims=True)
    return rstd * (dxhat - m1 - xhat * m2)


def _colsum(v):
    return jnp.sum(v, axis=0, keepdims=True)


def _fwd_in(x_bf, wg_in):
    s, d = x_bf.shape
    inc = wg_in.shape[2]
    tm, tn, tk = _tile(s, 1024), _tile(inc, 1280), _tile(d, 2048)
    nb = inc // tn

    def epi(accs, ex, out, ids):
        out[0][...] = accs[0].astype(BF16)

    return _matmul(
        "fwd_in", (s // tm, N_CHIP * nb, d // tk),
        [(x_bf, pl.BlockSpec((tm, tk), lambda i, j, k: (i, k)),
          wg_in, pl.BlockSpec((None, tk, tn), lambda i, j, k: (j // nb, k, j % nb)), "nn")],
        [], [(_sds((s, N_CHIP * inc), BF16), pl.BlockSpec((tm, tn), lambda i, j, k: (i, j)))],
        (tm, tn), epi)[0]


def _fwd_merge(y_pool, y_lru, w_pu, w_lu, z):
    s, d = y_pool.shape
    tm, tn, tk = _tile(s, 1024), _tile(d, 1024), _tile(d, 1024)
    ga0, gb0 = 3 * d // tn, 4 * d // tn

    def epi(accs, ex, out, ids):
        sa = _sigmoid(ex[0][...].astype(F32))
        sb = _sigmoid(ex[1][...].astype(F32))
        out[0][...] = (sa * accs[0] + sb * accs[1]).astype(BF16)
        out[1][...] = accs[0].astype(BF16)
        out[2][...] = accs[1].astype(BF16)

    a_spec = pl.BlockSpec((tm, tk), lambda i, j, k: (i, k))
    b_spec = pl.BlockSpec((tk, tn), lambda i, j, k: (k, j))
    o_spec = pl.BlockSpec((tm, tn), lambda i, j, k: (i, j))
    return _matmul(
        "fwd_merge", (s // tm, d // tn, d // tk),
        [(y_pool, a_spec, w_pu, b_spec, "nn"), (y_lru, a_spec, w_lu, b_spec, "nn")],
        [(z, pl.BlockSpec((tm, tn), lambda i, j, k: (i, ga0 + j))), (z, pl.BlockSpec((tm, tn), lambda i, j, k: (i, gb0 + j)))],
        [(_sds((s, d), BF16), o_spec)] * 3, (tm, tn), epi)


def _fwd_out_ln1(m, w_out, x, b_out, g1, b1):
    s, d = x.shape
    tm, tk = _tile(s, 256), _tile(d, 2048)

    def epi(accs, ex, out, ids):
        r = DN_ALPHA * ex[0][...] + accs[0] + ex[1][...]
        xhat, rstd = _layer_norm(r)
        out[0][...] = xhat
        out[1][...] = (xhat * ex[2][...] + ex[3][...]).astype(BF16)
        out[2][...] = rstd

    full = pl.BlockSpec((tm, d), lambda i, j, k: (i, 0))
    return _matmul(
        "fwd_out_ln1", (s // tm, 1, d // tk),
        [(m, pl.BlockSpec((tm, tk), lambda i, j, k: (i, k)), w_out, pl.BlockSpec((tk, d), lambda i, j, k: (k, 0)), "nn")],
        [(x, full), (b_out, _row(d)), (g1, _row(d)), (b1, _row(d))],
        [(_sds((s, d), F32), full), (_sds((s, d), BF16), full), (_sds((s, 1), F32), pl.BlockSpec((tm, 1), lambda i, j, k: (i, 0)))],
        (tm, d), epi)


def _fwd_ff1(x1_bf, wg_ff1, b_ff1):
    s, d = x1_bf.shape
    fc = wg_ff1.shape[2]
    tm, tn, tk = _tile(s, 1024), _tile(fc, 1024), _tile(d, 2048)
    nb = fc // tn

    def epi(accs, ex, out, ids):
        p = jnp.maximum(accs[0] + ex[0][...], 0.0)
        out[0][...] = (p * p).astype(BF16)

    return _matmul(
        "fwd_ff1", (s // tm, N_CHIP * nb, d // tk),
        [(x1_bf, pl.BlockSpec((tm, tk), lambda i, j, k: (i, k)),
          wg_ff1, pl.BlockSpec((None, tk, tn), lambda i, j, k: (j // nb, k, j % nb)), "nn")],
        [(b_ff1, pl.BlockSpec((1, tn), lambda i, j, k: (0, j)))],
        [(_sds((s, N_CHIP * fc), BF16), pl.BlockSpec((tm, tn), lambda i, j, k: (i, j)))],
        (tm, tn), epi)[0]


def _fwd_ff2_ln2_loss(hdn, w_ff2, xhat1, g1, b1, b_ff2, g2, b2, target):
    s, f = hdn.shape
    d = xhat1.shape[1]
    tm, tk = _tile(s, 256), _tile(f, 2048)

    def epi(accs, ex, out, ids):
        first = ids[0] == 0
        x1 = ex[0][...] * ex[1][...] + ex[2][...]
        r = DN_ALPHA * x1 + accs[0] + ex[3][...]
        xhat, rstd = _layer_norm(r)
        g2v = ex[4][...]
        err = xhat * g2v + ex[5][...] - ex[6][...]
        part = 0.5 * jnp.sum(jnp.mean(err * err, axis=-1, keepdims=True), axis=0, keepdims=True)
        dy = err * (1.0 / d)
        dr2 = _layer_norm_bwd(dy * g2v, xhat, rstd)
        out[0][...] = dr2
        out[1][...] = dr2.astype(BF16)
        _accum(out[2], _colsum(dy * xhat), first)
        _accum(out[3], _colsum(dy), first)
        _accum(out[4], _colsum(dr2), first)
        _accum(out[5], jnp.broadcast_to(part, (1, 128)), first)

    full = pl.BlockSpec((tm, d), lambda i, j, k: (i, 0))
    return _matmul(
        "fwd_ff2_ln2_loss", (s // tm, 1, f // tk),
        [(hdn, pl.BlockSpec((tm, tk), lambda i, j, k: (i, k)), w_ff2, pl.BlockSpec((tk, d), lambda i, j, k: (k, 0)), "nn")],
        [(xhat1, full), (g1, _row(d)), (b1, _row(d)), (b_ff2, _row(d)), (g2, _row(d)), (b2, _row(d)), (target, full)],
        [(_sds((s, d), F32), full), (_sds((s, d), BF16), full), (_sds((1, d), F32), _row(d)), (_sds((1, d), F32), _row(d)),
         (_sds((1, d), F32), _row(d)), (_sds((1, 128), F32), _row(128))],
        (tm, d), epi)


def _bwd_ff2_in(dr2_bf, w_ff2, hdn):
    s, d = dr2_bf.shape
    f = hdn.shape[1]
    tm, tn, tk = _tile(s, 1024), _tile(f, 1024), _tile(d, 2048)

    def epi(accs, ex, out, ids):
        dpre = accs[0] * (2.0 * jnp.sqrt(ex[0][...].astype(F32)))
        out[0][...] = dpre.astype(BF16)
        _accum(out[1], _colsum(dpre), ids[1] == 0)

    return _matmul(
        "bwd_ff2_in", (f // tn, s // tm, d // tk),
        [(dr2_bf, pl.BlockSpec((tm, tk), lambda j, i, k: (i, k)), w_ff2, pl.BlockSpec((tn, tk), lambda j, i, k: (j, k)), "nt")],
        [(hdn, pl.BlockSpec((tm, tn), lambda j, i, k: (i, j)))],
        [(_sds((s, f), BF16), pl.BlockSpec((tm, tn), lambda j, i, k: (i, j))), (_sds((1, f), F32), pl.BlockSpec((1, tn), lambda j, i, k: (0, j)))],
        (tm, tn), epi)


def _bwd_ff1_in_ln1(dpre, wg_ff1, dr2, xhat1, rstd1, g1):
    s, f = dpre.shape
    d = xhat1.shape[1]
    fc = wg_ff1.shape[2]
    tm, tk = _tile(s, 256), _tile(fc, 2048)
    nb = fc // tk

    def epi(accs, ex, out, ids):
        first = ids[0] == 0
        xhat = ex[1][...]
        dx1 = accs[0] + DN_ALPHA * ex[0][...]
        dr1 = _layer_norm_bwd(dx1 * ex[3][...], xhat, ex[2][...])
        out[0][...] = dr1
        out[1][...] = dr1.astype(BF16)
        _accum(out[2], _colsum(dx1 * xhat), first)
        _accum(out[3], _colsum(dx1), first)
        _accum(out[4], _colsum(dr1), first)

    full = pl.BlockSpec((tm, d), lambda i, j, k: (i, 0))
    return _matmul(
        "bwd_ff1_in_ln1", (s // tm, 1, f // tk),
        [(dpre, pl.BlockSpec((tm, tk), lambda i, j, k: (i, k)),
          wg_ff1, pl.BlockSpec((None, d, tk), lambda i, j, k: (k // nb, 0, k % nb)), "nt")],
        [(dr2, full), (xhat1, full), (rstd1, pl.BlockSpec((tm, 1), lambda i, j, k: (i, 0))), (g1, _row(d))],
        [(_sds((s, d), F32), full), (_sds((s, d), BF16), full), (_sds((1, d), F32), _row(d)), (_sds((1, d), F32), _row(d)),
         (_sds((1, d), F32), _row(d))],
        (tm, d), epi)


def _bwd_out_in(dr1_bf, w_out, z, pa, pb):
    s, d = dr1_bf.shape
    tm, tn, tk = _tile(s, 1024), _tile(d, 1024), _tile(d, 2048)
    ga0, gb0 = 3 * d // tn, 4 * d // tn

    def epi(accs, ex, out, ids):
        dm = accs[0]
        sa = _sigmoid(ex[0][...].astype(F32))
        sb = _sigmoid(ex[1][...].astype(F32))
        out[0][...] = (dm * sa).astype(BF16)
        out[1][...] = (dm * sb).astype(BF16)
        out[2][...] = (dm * ex[2][...].astype(F32) * sa * (1.0 - sa)).astype(BF16)
        out[3][...] = (dm * ex[3][...].astype(F32) * sb * (1.0 - sb)).astype(BF16)

    o_spec = pl.BlockSpec((tm, tn), lambda i, j, k: (i, j))
    return _matmul(
        "bwd_out_in", (s // tm, d // tn, d // tk),
        [(dr1_bf, pl.BlockSpec((tm, tk), lambda i, j, k: (i, k)), w_out, pl.BlockSpec((tn, tk), lambda i, j, k: (j, k)), "nt")],
        [(z, pl.BlockSpec((tm, tn), lambda i, j, k: (i, ga0 + j))), (z, pl.BlockSpec((tm, tn), lambda i, j, k: (i, gb0 + j))),
         (pa, o_spec), (pb, o_spec)],
        [(_sds((s, d), BF16), o_spec)] * 4, (tm, tn), epi)


def _bwd_up_in(name, dp, w_up):
    s, d = dp.shape
    n = w_up.shape[0]
    tm, tn, tk = _tile(s, 1024), _tile(n, 1024), _tile(d, 2048)

    def epi(accs, ex, out, ids):
        out[0][...] = accs[0].astype(BF16)

    return _matmul(
        name, (s // tm, n // tn, d // tk),
        [(dp, pl.BlockSpec((tm, tk), lambda i, j, k: (i, k)), w_up, pl.BlockSpec((tn, tk), lambda i, j, k: (j, k)), "nt")],
        [], [(_sds((s, n), BF16), pl.BlockSpec((tm, tn), lambda i, j, k: (i, j)))], (tm, tn), epi)[0]


def _bwd_in(dz, wg_in, dr1):
    s, d = dr1.shape
    inc = wg_in.shape[2]
    tm, tn, tk = _tile(s, 1024), _tile(d, 1024), _tile(inc, 1280)
    nb = inc // tk

    def epi(accs, ex, out, ids):
        out[0][...] = accs[0] + DN_ALPHA * ex[0][...]

    o_spec = pl.BlockSpec((tm, tn), lambda i, j, k: (i, j))
    return _matmul(
        "bwd_in", (s // tm, d // tn, N_CHIP * nb),
        [(dz, pl.BlockSpec((tm, tk), lambda i, j, k: (i, k)),
          wg_in, pl.BlockSpec((None, tn, tk), lambda i, j, k: (k // nb, j, k % nb)), "nt")],
        [(dr1, o_spec)], [(_sds((s, d), F32), o_spec)], (tm, tn), epi)[0]


def _wgrad(name, a, b, col_sharded):
    s, ka = a.shape
    n = b.shape[1]
    tm, tk = _tile(ka, 1024), _tile(s, 1024)
    tn = _tile(n // N_CHIP, 1280) if col_sharded else _tile(n, 1024)

    def epi(accs, ex, out, ids):
        out[0][...] = accs[0].astype(BF16)

    if col_sharded:
        nb = (n // N_CHIP) // tn
        o = (_sds((N_CHIP, ka, n // N_CHIP), BF16), pl.BlockSpec((None, tm, tn), lambda i, j, k: (j // nb, i, j % nb)))
    else:
        o = (_sds((ka, n), BF16), pl.BlockSpec((tm, tn), lambda i, j, k: (i, j)))
    res = _matmul(
        name, (ka // tm, n // tn, s // tk),
        [(a, pl.BlockSpec((tk, tm), lambda i, j, k: (k, i)), b, pl.BlockSpec((tk, tn), lambda i, j, k: (k, j)), "tn")],
        [], [o], (tm, tn), epi)[0]
    return res if col_sharded else res.reshape(N_CHIP, ka // N_CHIP, n)


def _chunk(s):
    return _tile(s, 512, SUBLANES)


def _zero_pads(ref, s):
    zeros = jnp.zeros((PAD, ref.shape[1]), F32)
    ref[pl.ds(0, PAD), :] = zeros
    ref[pl.ds(PAD + s, PAD), :] = zeros


def _window(ref, t0, t):
    return ref[pl.ds(t0, t + 2 * PAD), :]


def _shift(sup, off, t):
    return sup[PAD + off:PAD + off + t, :]


def _pool_count(t0, t, s, w):
    pos = t0 + lax.broadcasted_iota(jnp.int32, (t, 1), 0)
    return (jnp.minimum(pos + w // 2, s) - jnp.maximum(pos - w // 2, 0)).astype(F32)


def _pool_fwd(z, pool_w, pool_scale):
    s = z.shape[0]
    n_g, pg = pool_w.shape[0], pool_w.shape[1]
    assert n_g == len(POOL_WINDOWS) and max(POOL_WINDOWS) // 2 <= PAD
    t = _chunk(s)

    def body(u_ref, w_ref, sc_ref, d_ref, y_ref, pad_ref):
        g = pl.program_id(0)
        _zero_pads(pad_ref, s)
        pad_ref[pl.ds(PAD, s), :] = u_ref[...].astype(F32)
        for gi, w in enumerate(POOL_WINDOWS):
            @pl.when(g == gi)
            def _():
                def step(ch, carry):
                    t0 = pl.multiple_of(ch * t, t)
                    sup = _window(pad_ref, t0, t)
                    acc = _shift(sup, -(w // 2), t)
                    for o in range(-(w // 2) + 1, w // 2):
                        acc = acc + _shift(sup, o, t)
                    dd = (acc * (1.0 / _pool_count(t0, t, s, w)) - _shift(sup, 0, t)).astype(BF16)
                    d_ref[pl.ds(t0, t), :] = dd
                    y = jnp.dot(dd, w_ref[...], preferred_element_type=F32) * sc_ref[...]
                    y_ref[pl.ds(t0, t), :] = y.astype(BF16)
                    return carry

                lax.fori_loop(0, s // t, step, 0)

    blk = pl.BlockSpec((s, pg), lambda g: (0, g))
    return pl.pallas_call(
        body, name="pool_fwd", grid=(n_g,),
        in_specs=[blk, pl.BlockSpec((None, pg, pg), lambda g: (g, 0, 0)), pl.BlockSpec((1, pg), lambda g: (0, g))],
        out_specs=[blk, blk], out_shape=[_sds((s, n_g * pg), BF16)] * 2,
        scratch_shapes=[pltpu.VMEM((s + 2 * PAD, pg), F32)],
        compiler_params=_cparams(("arbitrary",)),
    )(z, pool_w, pool_scale)


def _pool_bwd(dsv, dy, pool_w, pool_scale):
    s = dsv.shape[0]
    n_g, pg = pool_w.shape[0], pool_w.shape[1]
    t = _chunk(s)

    def body(d_ref, dy_ref, w_ref, sc_ref, du_ref, dw_ref, dsc_ref, epad_ref, dwacc_ref):
        g = pl.program_id(0)
        _zero_pads(epad_ref, s)
        dwacc_ref[...] = jnp.zeros_like(dwacc_ref)
        for gi, w in enumerate(POOL_WINDOWS):
            @pl.when(g == gi)
            def _():
                def first(ch, dsc):
                    t0 = pl.multiple_of(ch * t, t)
                    dd = d_ref[pl.ds(t0, t), :]
                    dyc = dy_ref[pl.ds(t0, t), :].astype(F32)
                    wv = w_ref[...]
                    ypre = jnp.dot(dd, wv, preferred_element_type=F32)
                    dq = (dyc * sc_ref[...]).astype(BF16)
                    dwacc_ref[...] += lax.dot_general(dd, dq, _DIMS["tn"], preferred_element_type=F32)
                    ddv = lax.dot_general(dq, wv, _DIMS["nt"], preferred_element_type=F32)
                    epad_ref[pl.ds(pl.multiple_of(PAD + t0, SUBLANES), t), :] = ddv * (1.0 / _pool_count(t0, t, s, w))
                    return dsc + _colsum(dyc * ypre)

                dsc_ref[...] = lax.fori_loop(0, s // t, first, jnp.zeros((1, pg), F32))

                def second(ch, carry):
                    t0 = pl.multiple_of(ch * t, t)
                    sup = _window(epad_ref, t0, t)
                    acc = _shift(sup, -(w // 2) + 1, t)
                    for o in range(-(w // 2) + 2, w // 2 + 1):
                        acc = acc + _shift(sup, o, t)
                    du_ref[pl.ds(t0, t), :] = (acc - _shift(sup, 0, t) * _pool_count(t0, t, s, w)).astype(BF16)
                    return carry

                lax.fori_loop(0, s // t, second, 0)

        dw_ref[...] = dwacc_ref[...].astype(BF16)

    blk = pl.BlockSpec((s, pg), lambda g: (0, g))
    w_spec = pl.BlockSpec((None, pg, pg), lambda g: (g, 0, 0))
    sc_spec = pl.BlockSpec((1, pg), lambda g: (0, g))
    return pl.pallas_call(
        body, name="pool_bwd", grid=(n_g,),
        in_specs=[blk, blk, w_spec, sc_spec], out_specs=[blk, w_spec, sc_spec],
        out_shape=[_sds((s, n_g * pg), BF16), _sds((n_g, pg, pg), BF16), _sds((1, n_g * pg), F32)],
        scratch_shapes=[pltpu.VMEM((s + 2 * PAD, pg), F32), pltpu.VMEM((pg, pg), F32)],
        compiler_params=_cparams(("arbitrary",)),
    )(dsv, dy, pool_w, pool_scale)


def _sigmoid(x):
    return 0.5 * jnp.tanh(0.5 * x) + 0.5


def _softplus(x):
    e = jnp.exp(-jnp.abs(x))
    log1p_e = jnp.where(e < 1e-2, e * (1.0 - e * (0.5 - e * (1.0 / 3.0))), jnp.log(1.0 + e))
    return jnp.maximum(x, 0.0) + log1p_e


_GELU_C = math.sqrt(2.0 / math.pi)


def _gelu(x):
    th = jnp.tanh(_GELU_C * (x + 0.044715 * x * x * x))
    return 0.5 * x * (1.0 + th), th


def _gelu_grad(x, th):
    return 0.5 * (1.0 + th) + 0.5 * x * (1.0 - th * th) * _GELU_C * (1.0 + 3.0 * 0.044715 * x * x)


def _scan_chunk(a_ref, b_ref, o_ref, o_off, carry, t, reverse):
    n = a_ref.shape[1]
    row = lax.broadcasted_iota(jnp.int32, (SUBLANES, n), 0)
    n_groups = t // SUBLANES
    unroll = math.gcd(n_groups, SCAN_UNROLL)
    last = 0 if reverse else SUBLANES - 1

    def step(si, carry):
        for u in range(unroll):
            gi = si * unroll + u
            g = n_groups - 1 - gi if reverse else gi
            r0 = pl.multiple_of(g * SUBLANES, SUBLANES)
            a = a_ref[pl.ds(r0, SUBLANES), :]
            b = b_ref[pl.ds(r0, SUBLANES), :]
            for k in (1, 2, 4):
                keep = row < SUBLANES - k if reverse else row >= k
                sh = SUBLANES - k if reverse else k
                ar = jnp.where(keep, pltpu.roll(a, sh, 0), 1.0)
                br = jnp.where(keep, pltpu.roll(b, sh, 0), 0.0)
                b = a * br + b
                a = a * ar
            o_ref[pl.ds(pl.multiple_of(o_off + r0, SUBLANES), SUBLANES), :] = a * carry + b
            carry = (jnp.broadcast_to(a[last:last + 1, :], a.shape) * carry
                     + jnp.broadcast_to(b[last:last + 1, :], b.shape))
        return carry

    return lax.fori_loop(0, n_groups // unroll, step, carry)


def _lru_params(pk_ref):
    rows = pk_ref[...]
    get = lambda i: rows[i:i + 1, :]
    cw = [get(k) for k in range(4)]
    lam = (get(9), get(10))
    big_l = tuple(-LRU_C * _softplus(-v) for v in lam)
    return cw, get(4), (get(5), get(6)), (get(7), get(8)), lam, big_l


def _conv(sup, cw, cb, t):
    xc = cb + cw[0] * _shift(sup, -2, t)
    for k in range(1, 4):
        xc = xc + cw[k] * _shift(sup, k - 2, t)
    return xc


def _gates(xcb, w_ref, d, bk, ba, bx, big_l):
    pre = jnp.dot(xcb, w_ref[:, pl.ds(d * 2 * bk, 2 * bk)], preferred_element_type=F32)
    r = _sigmoid(pre[:, :bk] + ba[d])
    i = _sigmoid(pre[:, bk:] + bx[d])
    la = big_l[d] * r
    a = jnp.exp(la)
    var = jnp.tanh(-la) * (1.0 + a * a)
    rs = lax.rsqrt(jnp.maximum(var, 1e-30))
    return r, i, a, var * rs, rs


def _lru_specs(s, d, bk):
    u_spec = pl.BlockSpec((s, bk), lambda h: (0, d // bk + h))
    ug_spec = pl.BlockSpec((s, bk), lambda h: (0, 2 * d // bk + h))
    w_spec = pl.BlockSpec((None, bk, 4 * bk), lambda h: (h, 0, 0))
    pk_spec = pl.BlockSpec((None, 16, bk), lambda h: (h, 0, 0))
    blk = pl.BlockSpec((s, bk), lambda h: (0, h))
    return u_spec, ug_spec, w_spec, pk_spec, blk


def _lru_fwd(z, gatew, pk):
    s = z.shape[0]
    n_h, bk = gatew.shape[0], gatew.shape[1]
    d = n_h * bk
    t = _chunk(s)
    n_ch = s // t

    def body(u_ref, ug_ref, w_ref, pk_ref, y_ref, upad, h0buf, abuf, bbuf, xcbuf, h1buf):
        _zero_pads(upad, s)
        upad[pl.ds(PAD, s), :] = u_ref[...].astype(F32)
        cw, cb, ba, bx, _, big_l = _lru_params(pk_ref)
        zero = jnp.zeros((SUBLANES, bk), F32)

        def fill(xc, dr):
            _, i, a, sq, _ = _gates(xc.astype(BF16), w_ref, dr, bk, ba, bx, big_l)
            abuf[...] = a
            bbuf[...] = sq * i * xc

        def up(ch, carry):
            t0 = pl.multiple_of(ch * t, t)
            xc = _conv(_window(upad, t0, t), cw, cb, t)
            xcbuf[pl.ds(t0, t), :] = xc
            fill(xc, 0)
            return _scan_chunk(abuf, bbuf, h0buf, t0, carry, t, False)

        lax.fori_loop(0, n_ch, up, zero)

        def down(ci, carry):
            t0 = pl.multiple_of((n_ch - 1 - ci) * t, t)
            fill(xcbuf[pl.ds(t0, t), :], 1)
            carry = _scan_chunk(abuf, bbuf, h1buf, 0, carry, t, True)
            gl, _ = _gelu(ug_ref[pl.ds(t0, t), :].astype(F32))
            y_ref[pl.ds(t0, t), :] = ((h0buf[pl.ds(t0, t), :] + h1buf[...]) * gl).astype(BF16)
            return carry

        lax.fori_loop(0, n_ch, down, zero)

    u_spec, ug_spec, w_spec, pk_spec, blk = _lru_specs(s, d, bk)
    return pl.pallas_call(
        body, name="lru_fwd", grid=(n_h,),
        in_specs=[u_spec, ug_spec, w_spec, pk_spec], out_specs=blk, out_shape=_sds((s, d), BF16),
        scratch_shapes=[pltpu.VMEM((s + 2 * PAD, bk), F32), pltpu.VMEM((s, bk), F32), pltpu.VMEM((t, bk), F32), pltpu.VMEM((t, bk), F32),
                        pltpu.VMEM((s, bk), F32), pltpu.VMEM((t, bk), F32)],
        compiler_params=_cparams(("arbitrary",)),
    )(z, z, gatew, pk)


def _lru_grads(lam_, hnb, a, sq, rs, r, i, xc, xcb, w_ref, dwacc, d, big_l, acc):
    bk = xc.shape[1]
    dba, dbx, dl = acc
    q = lam_ * i * xc
    dla = lam_ * hnb * a - q * (a * a) * rs
    dpr = dla * big_l * r * (1.0 - r)
    dpi = q * sq * (1.0 - i)
    dprb, dpib = dpr.astype(BF16), dpi.astype(BF16)
    c0 = d * 2 * bk
    dxc = (lam_ * sq * i
           + lax.dot_general(dprb, w_ref[:, pl.ds(c0, bk)], _DIMS["nt"], preferred_element_type=F32)
           + lax.dot_general(dpib, w_ref[:, pl.ds(c0 + bk, bk)], _DIMS["nt"], preferred_element_type=F32))
    dwacc[:, pl.ds(c0, bk)] += lax.dot_general(xcb, dprb, _DIMS["tn"], preferred_element_type=F32)
    dwacc[:, pl.ds(c0 + bk, bk)] += lax.dot_general(xcb, dpib, _DIMS["tn"], preferred_element_type=F32)
    return dxc, (dba + _colsum(dpr), dbx + _colsum(dpi), dl + _colsum(dla * r))


def _lru_bwd(z, dy, gatew, pk):
    s = z.shape[0]
    n_h, bk = gatew.shape[0], gatew.shape[1]
    d = n_h * bk
    t = _chunk(s)
    n_ch = s // t

    def body(u_ref, ug_ref, dy_ref, w_ref, pk_ref, du_ref, dug_ref, dw_ref, dpk_ref,
             upad, h0pad, h1pad, dxpad, abuf, bbuf, lbuf, dwacc, edge, xcbuf):
        for ref in (upad, h0pad, h1pad, dxpad):
            _zero_pads(ref, s)
        upad[pl.ds(PAD, s), :] = u_ref[...].astype(F32)
        dwacc[...] = jnp.zeros_like(dwacc)
        cw, cb, ba, bx, lam, big_l = _lru_params(pk_ref)
        zero = jnp.zeros((SUBLANES, bk), F32)
        zrow = jnp.zeros((1, bk), F32)
        rowi = lax.broadcasted_iota(jnp.int32, (t, bk), 0)

        def at(t0):
            return pl.ds(pl.multiple_of(PAD + t0, SUBLANES), t)

        def conv_in(t0):
            xc = xcbuf[pl.ds(t0, t), :]
            return xc, xc.astype(BF16)

        def dh_of(t0):
            ug = ug_ref[pl.ds(t0, t), :].astype(F32)
            gl, th = _gelu(ug)
            dyv = dy_ref[pl.ds(t0, t), :].astype(F32)
            return dyv * gl, dyv * _gelu_grad(ug, th)

        def sweep1(ch, carry):
            t0 = pl.multiple_of(ch * t, t)
            xc = _conv(_window(upad, t0, t), cw, cb, t)
            xcbuf[pl.ds(t0, t), :] = xc
            _, i, a, sq, _ = _gates(xc.astype(BF16), w_ref, 0, bk, ba, bx, big_l)
            abuf[...] = a
            bbuf[...] = sq * i * xc
            return _scan_chunk(abuf, bbuf, h0pad, PAD + t0, carry, t, False)

        lax.fori_loop(0, n_ch, sweep1, zero)

        edge[...] = zero

        def sweep2(ci, st):
            carry_h, carry_l, acc = st
            t0 = pl.multiple_of((n_ch - 1 - ci) * t, t)
            xc, xcb = conv_in(t0)
            _, i1, a1, sq1, _ = _gates(xcb, w_ref, 1, bk, ba, bx, big_l)
            abuf[...] = a1
            bbuf[...] = sq1 * i1 * xc
            carry_h = _scan_chunk(abuf, bbuf, h1pad, PAD + t0, carry_h, t, True)
            dh, dgl = dh_of(t0)
            dug_ref[pl.ds(t0, t), :] = (dgl * (h0pad[at(t0), :] + h1pad[at(t0), :])).astype(BF16)
            r0, i0, a0, sq0, rs0 = _gates(xcb, w_ref, 0, bk, ba, bx, big_l)
            abuf[...] = jnp.where(rowi == t - 1, edge[0:1, :], pltpu.roll(a0, t - 1, 0))
            bbuf[...] = dh
            carry_l = _scan_chunk(abuf, bbuf, lbuf, 0, carry_l, t, True)
            edge[...] = jnp.broadcast_to(a0[0:1, :], (SUBLANES, bk))
            hprev = _shift(_window(h0pad, t0, t), -1, t)
            dxc, acc = _lru_grads(lbuf[...], hprev, a0, sq0, rs0, r0, i0, xc, xcb, w_ref, dwacc, 0, big_l[0], acc)
            dxpad[at(t0), :] = dxc
            return carry_h, carry_l, acc

        _, _, acc0 = lax.fori_loop(0, n_ch, sweep2, (zero, zero, (zrow, zrow, zrow)))

        edge[...] = zero

        def sweep3(ch, st):
            carry_l, acc = st
            t0 = pl.multiple_of(ch * t, t)
            xc, xcb = conv_in(t0)
            r1, i1, a1, sq1, rs1 = _gates(xcb, w_ref, 1, bk, ba, bx, big_l)
            dh, _ = dh_of(t0)
            abuf[...] = jnp.where(rowi == 0, edge[0:1, :], pltpu.roll(a1, 1, 0))
            bbuf[...] = dh
            carry_l = _scan_chunk(abuf, bbuf, lbuf, 0, carry_l, t, False)
            edge[...] = jnp.broadcast_to(a1[t - 1:t, :], (SUBLANES, bk))
            hnext = _shift(_window(h1pad, t0, t), 1, t)
            dxc, acc = _lru_grads(lbuf[...], hnext, a1, sq1, rs1, r1, i1, xc, xcb, w_ref, dwacc, 1, big_l[1], acc)
            dxpad[at(t0), :] += dxc
            return carry_l, acc

        _, acc1 = lax.fori_loop(0, n_ch, sweep3, (zero, (zrow, zrow, zrow)))

        def sweep4(ch, st):
            t0 = pl.multiple_of(ch * t, t)
            sdx = _window(dxpad, t0, t)
            su = _window(upad, t0, t)
            dxc = _shift(sdx, 0, t)
            du = cw[0] * _shift(sdx, 2, t) + cw[1] * _shift(sdx, 1, t) + cw[2] * dxc + cw[3] * _shift(sdx, -1, t)
            du_ref[pl.ds(t0, t), :] = du.astype(BF16)
            return tuple(st[k] + _colsum(dxc * _shift(su, k - 2, t)) for k in range(4)) + (st[4] + _colsum(dxc),)

        conv_g = lax.fori_loop(0, n_ch, sweep4, (zrow,) * 5)

        dpk_ref[...] = jnp.zeros_like(dpk_ref)
        rows = list(conv_g) + [acc0[0], acc1[0], acc0[1], acc1[1],
                               acc0[2] * LRU_C * _sigmoid(-lam[0]), acc1[2] * LRU_C * _sigmoid(-lam[1])]
        for k, v in enumerate(rows):
            dpk_ref[pl.ds(k, 1), :] = v
        dw_ref[...] = dwacc[...].astype(BF16)

    u_spec, ug_spec, w_spec, pk_spec, blk = _lru_specs(s, d, bk)
    padded = pltpu.VMEM((s + 2 * PAD, bk), F32)
    chunk = pltpu.VMEM((t, bk), F32)
    return pl.pallas_call(
        body, name="lru_bwd", grid=(n_h,),
        in_specs=[u_spec, ug_spec, blk, w_spec, pk_spec], out_specs=[blk, blk, w_spec, pk_spec],
        out_shape=[_sds((s, d), BF16), _sds((s, d), BF16), _sds((n_h, bk, 4 * bk), BF16), _sds((n_h, 16, bk), F32)],
        scratch_shapes=[padded, padded, padded, padded, chunk, chunk, chunk, pltpu.VMEM((bk, 4 * bk), F32),
                        pltpu.VMEM((SUBLANES, bk), F32), pltpu.VMEM((s, bk), F32)],
        compiler_params=_cparams(("arbitrary",)),
    )(z, z, dy, gatew, pk)


def _scalar(v):
    return jnp.reshape(v, (1,)).astype(jnp.int32)


def _add_sibling(g, r, c):
    _, rows, cols = g.shape
    rh = rows // 2
    tr = _tile(rh, 512, 16)
    nr = rh // tr

    def body(c_ref, g_ref, r_ref, o_ref):
        o_ref[...] = (g_ref[...].astype(F32) + r_ref[...].astype(F32)).astype(BF16)

    spec = pl.BlockSpec((None, tr, cols), lambda k, i, c_ref: (k, i, 0))
    return pl.pallas_call(
        body, name="add_sibling", out_shape=_sds((N_CHIP, rh, cols), BF16),
        grid_spec=pltpu.PrefetchScalarGridSpec(
            num_scalar_prefetch=1, grid=(N_CHIP, nr),
            in_specs=[pl.BlockSpec((None, tr, cols), lambda k, i, c_ref: (k, c_ref[0] * nr + i, 0)), spec], out_specs=spec),
        compiler_params=_cparams(("arbitrary", "arbitrary")),
    )(_scalar(c), g, r)


def _sum_chips(p, rcv, k_me, c):
    _, rh, cols = p.shape
    tr = _tile(rh, 512, 16)
    nr = rh // tr

    def body(kc_ref, p_ref, r_ref, o_ref):
        acc = p_ref[...].astype(F32)
        for j in range(3):
            acc = acc + r_ref[j].astype(F32)
        o_ref[...] = acc

    return pl.pallas_call(
        body, name="sum_chips", out_shape=_sds((2 * rh, cols), F32),
        grid_spec=pltpu.PrefetchScalarGridSpec(
            num_scalar_prefetch=1, grid=(nr,),
            in_specs=[pl.BlockSpec((None, tr, cols), lambda i, kc_ref: (kc_ref[0], i, 0)),
                      pl.BlockSpec((3, tr, cols), lambda i, kc_ref: (0, i, 0))],
            out_specs=pl.BlockSpec((tr, cols), lambda i, kc_ref: (kc_ref[1] * nr + i, 0))),
        compiler_params=_cparams(("arbitrary",)),
    )(jnp.stack([k_me, c]).astype(jnp.int32), p, rcv)


def _sum_devices(g):
    def body(g_ref, o_ref):
        acc = g_ref[0]
        for dev in range(1, N_DEV):
            acc = acc + g_ref[dev]
        o_ref[...] = acc

    return pl.pallas_call(body, name="sum_devices", out_shape=_sds(g.shape[1:], F32))(g)


def _adamw(w, g, m, v):
    rows, cols = w.shape
    tr = _tile(rows, 256, SUBLANES)

    def body(w_ref, g_ref, m_ref, v_ref, d_ref, nm_ref, nv_ref):
        gv = g_ref[...]
        nm = ADAM_B1 * m_ref[...] + (1.0 - ADAM_B1) * gv
        nv = ADAM_B2 * v_ref[...] + (1.0 - ADAM_B2) * (gv * gv)
        m_hat = nm / (1.0 - ADAM_B1 ** ADAM_STEP)
        v_hat = nv / (1.0 - ADAM_B2 ** ADAM_STEP)
        d_ref[...] = -ADAM_LR * (m_hat / (jnp.sqrt(v_hat) + ADAM_EPS) + ADAM_WD * w_ref[...])
        nm_ref[...] = nm
        nv_ref[...] = nv

    spec = pl.BlockSpec((tr, cols), lambda i: (i, 0))
    return pl.pallas_call(
        body, name="adamw", grid=(rows // tr,), in_specs=[spec] * 4, out_specs=[spec] * 3,
        out_shape=[_sds((rows, cols), F32)] * 3, compiler_params=_cparams(("arbitrary",)),
    )(w, g, m, v)


def _pack(vs, unit):
    flat = jnp.concatenate([v.reshape(-1).astype(F32) for v in vs])
    pad = (-flat.shape[0]) % unit
    if pad:
        flat = jnp.concatenate([flat, jnp.zeros((pad,), F32)])
    return flat.reshape(-1, 128)


def _unpack(p, like):
    flat = p.reshape(-1)
    out, off = [], 0
    for v in like:
        n = math.prod(v.shape)
        out.append(flat[off:off + n].reshape(v.shape))
        off += n
    return out


def kernel(x, w_in, pool_w, pool_scale, conv_w, conv_b, lru_wa, lru_ba, lru_wx, lru_bx, lru_lambda, w_pool_up, w_lru_up, w_out, b_out, ln1_g, ln1_b, w_ff1, b_ff1, w_ff2, b_ff2, ln2_g, ln2_b, loss_target, m_w_in, m_pool_w, m_pool_scale, m_conv_w, m_conv_b, m_lru_wa, m_lru_ba, m_lru_wx, m_lru_bx, m_lru_lambda, m_w_pool_up, m_w_lru_up, m_w_out, m_b_out, m_ln1_g, m_ln1_b, m_w_ff1, m_b_ff1, m_w_ff2, m_b_ff2, m_ln2_g, m_ln2_b, v_w_in, v_pool_w, v_pool_scale, v_conv_w, v_conv_b, v_lru_wa, v_lru_ba, v_lru_wx, v_lru_bx, v_lru_lambda, v_w_pool_up, v_w_lru_up, v_w_out, v_b_out, v_ln1_g, v_ln1_b, v_w_ff1, v_b_ff1, v_w_ff2, v_b_ff2, v_ln2_g, v_ln2_b):
    given = dict(locals())
    wt = {n: given[n] for n in WEIGHTS}
    mom = {n: given["m_" + n] for n in WEIGHTS}
    vel = {n: given["v_" + n] for n in WEIGHTS}

    ix, iy, ic = _mesh_pos()
    k_me = 2 * ix + iy
    s, d = x.shape[1], x.shape[2]
    ds = d // N_CHIP
    n_g, pgs, pg = pool_w.shape[1], pool_w.shape[2], pool_w.shape[3]
    n_h, bks, bk = lru_wa.shape[2], lru_wa.shape[3], lru_wa.shape[4]
    f = b_ff1.shape[1]
    x2 = x[0]
    x_bf = x2.astype(BF16)
    vec = lambda a: a.reshape(1, -1)

    sharded_vecs = [conv_w[0], lru_ba[0], lru_bx[0], lru_lambda[0]]
    rows_sv = jnp.concatenate(sharded_vecs + [jnp.zeros((6, ds), F32)], axis=0)
    sv = _all_gather_small(rows_sv)
    sv = sv.reshape(N_CHIP, 2, 16, ds)[:, 0].transpose(1, 0, 2).reshape(16, d)
    conv_w_f, ba_f, bx_f, lam_f = sv[0:4], sv[4:6], sv[6:8], sv[8:10]
    pk = jnp.concatenate([conv_w_f, conv_b, ba_f, bx_f, lam_f, jnp.zeros((5, d), F32)], axis=0)
    pk = pk.reshape(16, n_h, bk).transpose(1, 0, 2)

    def gate_stack(wa, wx):
        return jnp.stack([wa[0], wx[0]], axis=1)

    mats = {
        "w_in": w_in[0], "w_pool_up": w_pool_up[0], "w_lru_up": w_lru_up[0], "w_out": w_out[0],
        "w_ff1": w_ff1[0], "w_ff2": w_ff2[0],
        "pool_w": pool_w[0].reshape(n_g * pgs, pg),
        "gate_w": gate_stack(lru_wa, lru_wx).reshape(4 * n_h * bks, bk),
    }
    names = list(mats)
    gathered = dict(zip(names, _chip_all_gather([_cast_place(mats[n], k_me) for n in names])))
    wg_in, wg_ff1 = gathered["w_in"], gathered["w_ff1"]
    wf_pu, wf_lu, wf_out = (gathered[n].reshape(d, d) for n in ("w_pool_up", "w_lru_up", "w_out"))
    wf_ff2 = gathered["w_ff2"].reshape(f, d)
    wf_pool = gathered["pool_w"].reshape(N_CHIP, n_g, pgs, pg).transpose(1, 0, 2, 3).reshape(n_g, pg, pg)
    wf_gate = (gathered["gate_w"].reshape(N_CHIP, 2, 2, n_h, bks, bk).transpose(3, 0, 4, 1, 2, 5)
               .reshape(n_h, bk, 4 * bk))

    z = _fwd_in(x_bf, wg_in)
    d_pool, y_pool = _pool_fwd(z, wf_pool, pool_scale)
    y_lru = _lru_fwd(z, wf_gate, pk)
    m_mix, p_a, p_b = _fwd_merge(y_pool, y_lru, wf_pu, wf_lu, z)
    xhat1, x1_bf, rstd1 = _fwd_out_ln1(m_mix, wf_out, x2, b_out, ln1_g, ln1_b)
    hdn = _fwd_ff1(x1_bf, wg_ff1, b_ff1)
    dr2, dr2_bf, g_ln2_g, g_ln2_b, g_b_ff2, loss_part = _fwd_ff2_ln2_loss(
        hdn, wf_ff2, xhat1, ln1_g, ln1_b, b_ff2, ln2_g, ln2_b, loss_target[0])

    dpre, g_b_ff1 = _bwd_ff2_in(dr2_bf, wf_ff2, hdn)
    gw = {"w_ff2": _wgrad("wgrad_ff2", hdn, dr2_bf, False)}
    dr1, dr1_bf, g_ln1_g, g_ln1_b, g_b_out = _bwd_ff1_in_ln1(dpre, wg_ff1, dr2, xhat1, rstd1, ln1_g)
    gw["w_ff1"] = _wgrad("wgrad_ff1", x1_bf, dpre, True)
    dp_a, dp_b, dg_a, dg_b = _bwd_out_in(dr1_bf, wf_out, z, p_a, p_b)
    gw["w_out"] = _wgrad("wgrad_out", m_mix, dr1_bf, False)
    dy_pool = _bwd_up_in("bwd_pool_up_in", dp_a, wf_pu)
    dy_lru = _bwd_up_in("bwd_lru_up_in", dp_b, wf_lu)
    gw["w_pool_up"] = _wgrad("wgrad_pool_up", y_pool, dp_a, False)
    gw["w_lru_up"] = _wgrad("wgrad_lru_up", y_lru, dp_b, False)
    du_pool, g_pool_w, g_pool_scale = _pool_bwd(d_pool, dy_pool, wf_pool, pool_scale)
    du_lru, du_gate, g_gate_w, g_pk = _lru_bwd(z, dy_lru, wf_gate, pk)
    dz = jnp.concatenate([du_pool, du_lru, du_gate, dg_a, dg_b], axis=1)
    grad_x = _bwd_in(dz, wg_in, dr1)
    gw["w_in"] = _wgrad("wgrad_in", x_bf, dz, True)
    gw["pool_w"] = (g_pool_w.reshape(n_g, N_CHIP, pgs, pg).transpose(1, 0, 2, 3).reshape(N_CHIP, n_g * pgs, pg))
    gw["gate_w"] = (g_gate_w.reshape(n_h, N_CHIP, bks, 2, 2, bk).transpose(1, 3, 4, 0, 2, 5)
                    .reshape(N_CHIP, 4 * n_h * bks, bk))

    parts = [gw[n] for n in names]
    from_sibling = _sibling_swap_halves(parts)
    chip_sums = [_add_sibling(g, r, ic) for g, r in zip(parts, from_sibling)]
    from_chips = _chip_scatter(chip_sums)
    halves = [_sum_chips(p, r, k_me, ic) for p, r in zip(chip_sums, from_chips)]
    g_mat = dict(zip(names, _sibling_join_halves(halves)))

    def stacked(tree):
        return gate_stack(tree["lru_wa"], tree["lru_wx"]).reshape(4 * n_h * bks, bk)

    res = {}
    for n in names:
        if n == "gate_w":
            upd = _adamw(stacked(wt), g_mat[n], stacked(mom), stacked(vel))
            outs = [o.reshape(2, 2, n_h, bks, bk) for o in (g_mat[n],) + tuple(upd)]
            res["lru_wa"] = [o[:, 0][None] for o in outs]
            res["lru_wx"] = [o[:, 1][None] for o in outs]
        else:
            shp = wt[n].shape
            upd = _adamw(wt[n].reshape(mats[n].shape), g_mat[n], mom[n].reshape(mats[n].shape), vel[n].reshape(mats[n].shape))
            res[n] = [o.reshape(shp) for o in (g_mat[n],) + tuple(upd)]

    g_pk = g_pk.transpose(1, 0, 2).reshape(16, d)
    vec_full = {
        "pool_scale": g_pool_scale, "conv_w": g_pk[0:4], "conv_b": g_pk[4:5],
        "lru_ba": g_pk[5:7], "lru_bx": g_pk[7:9], "lru_lambda": g_pk[9:11],
        "b_out": g_b_out, "ln1_g": g_ln1_g, "ln1_b": g_ln1_b, "b_ff1": g_b_ff1, "b_ff2": g_b_ff2,
        "ln2_g": g_ln2_g, "ln2_b": g_ln2_b,
    }
    vnames = list(vec_full)
    vg = _sum_devices(_all_gather_small(_pack([vec_full[n] for n in vnames], 1024)))
    vg = dict(zip(vnames, _unpack(vg, [vec_full[n] for n in vnames])))
    for n in ("conv_w", "lru_ba", "lru_bx", "lru_lambda"):
        vg[n] = lax.dynamic_slice_in_dim(vg[n], k_me * ds, ds, axis=1)
    vg = {n: vg[n].reshape(wt[n].shape) for n in vnames}
    upd = _adamw(_pack([wt[n] for n in vnames], 1024), _pack([vg[n] for n in vnames], 1024),
                 _pack([mom[n] for n in vnames], 1024), _pack([vel[n] for n in vnames], 1024))
    upd = [_unpack(u, [wt[n] for n in vnames]) for u in upd]
    for i, n in enumerate(vnames):
        res[n] = [vg[n], upd[0][i], upd[1][i], upd[2][i]]

    loss = lax.psum(loss_part[0, 0], ("x", "y", "c"))
    return (loss, grad_x[None], *[res[n][0] for n in WEIGHTS], *[res[n][1] for n in WEIGHTS],
            *[res[n][2] for n in WEIGHTS], *[res[n][3] for n in WEIGHTS])
```

```python
import functools
import math

import jax
import jax.numpy as jnp
from jax import lax
from jax.experimental import pallas as pl
from jax.experimental.pallas import tpu as pltpu

F32 = jnp.float32
BF16 = jnp.bfloat16
MESH = pl.DeviceIdType.MESH
ANY = pl.BlockSpec(memory_space=pl.ANY)

N_CHIP = 4
N_DEV = 8
VMEM_LIMIT_BYTES = 56 * 1024 * 1024
SUBLANES = 8
PAD = 8
SCAN_UNROLL = 8

POOL_WINDOWS = (2, 4, 8, 16)
LRU_C = 8.0
DN_ALPHA = 2.0 ** 0.25
LN_EPS = 1e-5
ADAM_LR, ADAM_B1, ADAM_B2, ADAM_EPS, ADAM_WD, ADAM_STEP = 0.001, 0.9, 0.999, 1e-08, 0.01, 10

WEIGHTS = ("w_in", "pool_w", "pool_scale", "conv_w", "conv_b", "lru_wa", "lru_ba", "lru_wx", "lru_bx", "lru_lambda",
           "w_pool_up", "w_lru_up", "w_out", "b_out", "ln1_g", "ln1_b", "w_ff1", "b_ff1", "w_ff2", "b_ff2", "ln2_g", "ln2_b")


def _cparams(sem=None):
    return pltpu.CompilerParams(dimension_semantics=sem, vmem_limit_bytes=VMEM_LIMIT_BYTES)


def _tile(dim, pref, unit=128):
    if dim <= pref:
        return dim
    t = (pref // unit) * unit
    while t > unit and dim % t:
        t -= unit
    assert dim % t == 0, (dim, pref)
    return t


def _mesh_pos():
    x, y, c = lax.axis_index("x"), lax.axis_index("y"), lax.axis_index("c")
    return x, y, c


def _other_chips(x, y):
    return [(1 - x, y), (x, 1 - y), (1 - x, 1 - y)]


def _all_gather_small(v):
    m_per, n = v.shape

    def body(x_ref, out_ref, send_sems, recv_sems, local_sem):
        x, y, c = _mesh_pos()
        me, sibling = (x, y, c), (x, y, 1 - c)
        chips = _other_chips(x, y)

        def rows(px, py, pc):
            return out_ref.at[4 * px + 2 * py + pc]

        def copy(k, block, to, src=None):
            return pltpu.make_async_remote_copy(
                src_ref=rows(*block) if src is None else src, dst_ref=rows(*block),
                send_sem=send_sems.at[k], recv_sem=recv_sems.at[k], device_id=to, device_id_type=MESH)

        mine = pltpu.make_async_copy(x_ref, rows(*me), local_sem)
        mine.start()
        first = [copy(0, me, sibling, src=x_ref)]
        first += [copy(1 + j, me, (*chip, c), src=x_ref) for j, chip in enumerate(chips)]
        for cp in first:
            cp.start()
        passed = [copy(4 + j, (*chip, c), sibling) for j, chip in enumerate(chips)]
        for j, chip in enumerate(chips):
            copy(1 + j, (*chip, c), me).wait_recv()
            passed[j].start()
        copy(0, sibling, me).wait_recv()
        for j, chip in enumerate(chips):
            copy(4 + j, (*chip, 1 - c), me).wait_recv()
        for cp in first + passed:
            cp.wait_send()
        mine.wait()

    return pl.pallas_call(
        body, name="all_gather_small",
        out_shape=jax.ShapeDtypeStruct((N_DEV, m_per, n), v.dtype),
        in_specs=[pl.BlockSpec(memory_space=pltpu.VMEM)],
        out_specs=pl.BlockSpec(memory_space=pltpu.VMEM),
        scratch_shapes=[pltpu.SemaphoreType.DMA((7,)), pltpu.SemaphoreType.DMA((7,)), pltpu.SemaphoreType.DMA],
    )(v)


class _Stage:
    def __init__(self, srcs, bufs, news, n_sems, copies):
        self.srcs, self.bufs, self.news, self.n_sems, self.copies = list(srcs), list(bufs), list(news), n_sems, copies


def _remote(src, dst, send_sems, recv_sems, s, to):
    return pltpu.make_async_remote_copy(src_ref=src, dst_ref=dst, send_sem=send_sems.at[s], recv_sem=recv_sems.at[s],
                                        device_id=to, device_id_type=MESH)


def _stage_operands(stages, n_in, n_out):
    ins, outs, aliases, scratch = [], [], {}, []
    for st in stages:
        for i in range(len(st.bufs)):
            aliases[n_in + len(ins) + len(st.srcs) + i] = n_out + len(outs) + i
        ins += st.srcs + st.bufs
        outs += [jax.ShapeDtypeStruct(b.shape, b.dtype) for b in st.bufs] + st.news
        scratch += [pltpu.SemaphoreType.DMA((st.n_sems,)), pltpu.SemaphoreType.DMA((st.n_sems,))]
    return ins, outs, aliases, scratch


def _stage_refs(stages, in_refs, out_refs, sem_refs):
    parts, i, o = [], 0, 0
    for n, st in enumerate(stages):
        src = in_refs[i:i + len(st.srcs)]
        i += len(st.srcs) + len(st.bufs)
        buf = out_refs[o:o + len(st.bufs)]
        new = out_refs[o + len(st.bufs):o + len(st.bufs) + len(st.news)]
        o += len(st.bufs) + len(st.news)
        parts.append((src, buf, new, sem_refs[2 * n], sem_refs[2 * n + 1]))
    return parts


def _stage_results(stages, res):
    out, o = [], 0
    for st in stages:
        n = len(st.bufs) + len(st.news)
        out.append(list(res[o:o + n]))
        o += n
    return out


def _stages_start(stages, parts):
    for st, part in zip(stages, parts):
        for cp in st.copies(*part)[0]:
            cp.start()


def _stages_wait(stages, parts):
    for st, part in zip(stages, parts):
        started, landing = st.copies(*part)
        for cp in landing:
            cp.wait_recv()
        for cp in started:
            cp.wait_send()


def _run_stages(name, stages):
    ins, outs, aliases, scratch = _stage_operands(stages, 0, 0)

    def body(*refs):
        parts = _stage_refs(stages, refs[:len(ins)], refs[len(ins):len(ins) + len(outs)], refs[len(ins) + len(outs):])
        for st, part in zip(stages, parts):
            _stages_start([st], [part])
            _stages_wait([st], [part])

    res = pl.pallas_call(
        body, name=name, out_shape=outs, in_specs=[ANY] * len(ins), out_specs=[ANY] * len(outs),
        input_output_aliases=aliases, scratch_shapes=scratch)(*ins)
    return _stage_results(stages, res)


def _gather_ici(ts):
    def copies(src, buf, new, send_sems, recv_sems):
        x, y, c = _mesh_pos()
        started, landing = [], []
        for t in range(len(ts)):
            rh = ts[t].shape[1] // 2
            rows = pl.ds(c * rh, rh)
            for j, chip in enumerate(_other_chips(x, y)):
                mine = buf[t].at[2 * x + y, rows]
                theirs = buf[t].at[2 * chip[0] + chip[1], rows]
                started.append(_remote(mine, mine, send_sems, recv_sems, 3 * t + j, (*chip, c)))
                landing.append(_remote(theirs, theirs, send_sems, recv_sems, 3 * t + j, (x, y, c)))
        return started, landing

    return _Stage([], ts, [], 3 * len(ts), copies)


def _gather_d2d(ts):
    def copies(src, buf, new, send_sems, recv_sems):
        x, y, c = _mesh_pos()
        started, landing = [], []
        for t in range(len(ts)):
            rh = ts[t].shape[1] // 2
            for j, chip in enumerate(_other_chips(x, y)):
                got = buf[t].at[2 * chip[0] + chip[1], pl.ds(c * rh, rh)]
                other = buf[t].at[2 * chip[0] + chip[1], pl.ds((1 - c) * rh, rh)]
                started.append(_remote(got, got, send_sems, recv_sems, 3 * t + j, (x, y, 1 - c)))
                landing.append(_remote(other, other, send_sems, recv_sems, 3 * t + j, (x, y, c)))
        return started, landing

    return _Stage([], ts, [], 3 * len(ts), copies)


def _swap_halves(gs):
    def copies(src, buf, new, send_sems, recv_sems):
        x, y, c = _mesh_pos()
        started, landing = [], []
        for t in range(len(gs)):
            rh = gs[t].shape[1] // 2
            started.append(_remote(src[t].at[:, pl.ds((1 - c) * rh, rh)], new[t], send_sems, recv_sems, t, (x, y, 1 - c)))
            landing.append(_remote(new[t], new[t], send_sems, recv_sems, t, (x, y, c)))
        return started, landing

    news = [jax.ShapeDtypeStruct((g.shape[0], g.shape[1] // 2, g.shape[2]), g.dtype) for g in gs]
    return _Stage(gs, [], news, len(gs), copies)


def _scatter_chips(ps):
    def copies(src, buf, new, send_sems, recv_sems):
        x, y, c = _mesh_pos()
        started, landing = [], []
        for t in range(len(ps)):
            for j, chip in enumerate(_other_chips(x, y)):
                started.append(_remote(src[t].at[2 * chip[0] + chip[1]], new[t].at[j], send_sems, recv_sems, 3 * t + j, (*chip, c)))
                landing.append(_remote(new[t].at[j], new[t].at[j], send_sems, recv_sems, 3 * t + j, (x, y, c)))
        return started, landing

    return _Stage(ps, [], [jax.ShapeDtypeStruct((3,) + p.shape[1:], p.dtype) for p in ps], 3 * len(ps), copies)


def _join_halves(fs):
    def copies(src, buf, new, send_sems, recv_sems):
        x, y, c = _mesh_pos()
        started, landing = [], []
        for t in range(len(fs)):
            rh = fs[t].shape[0] // 2
            mine = buf[t].at[pl.ds(c * rh, rh)]
            theirs = buf[t].at[pl.ds((1 - c) * rh, rh)]
            started.append(_remote(mine, mine, send_sems, recv_sems, t, (x, y, 1 - c)))
            landing.append(_remote(theirs, theirs, send_sems, recv_sems, t, (x, y, c)))
        return started, landing

    return _Stage([], fs, [], len(fs), copies)


def _cast_place(w, k_me):
    rows, cols = w.shape
    tr = _tile(rows, 512, 16)

    def body(k_ref, w_ref, o_ref):
        o_ref[...] = w_ref[...].astype(BF16)

    return pl.pallas_call(
        body, name="cast_place", out_shape=_sds((N_CHIP, rows, cols), BF16),
        grid_spec=pltpu.PrefetchScalarGridSpec(
            num_scalar_prefetch=1, grid=(rows // tr,),
            in_specs=[pl.BlockSpec((tr, cols), lambda i, k_ref: (i, 0))],
            out_specs=pl.BlockSpec((None, tr, cols), lambda i, k_ref: (k_ref[0], i, 0))),
        compiler_params=_cparams(("arbitrary",)),
    )(_scalar(k_me), w)


_DIMS = {"nn": (((1,), (0,)), ((), ())), "nt": (((1,), (1,)), ((), ())), "tn": (((0,), (0,)), ((), ()))}


def _accum(ref, val, first):
    @pl.when(first)
    def _():
        ref[...] = val

    @pl.when(jnp.logical_not(first))
    def _():
        ref[...] += val


def _grid_edges(grid):
    ids = [pl.program_id(ax) for ax in range(len(grid))]
    first = functools.reduce(jnp.logical_and, [i == 0 for i in ids])
    last = functools.reduce(jnp.logical_and, [i == n - 1 for i, n in zip(ids, grid)])
    return first, last


def _host_call(name, grid, body, operands, in_specs, out_shape, out_specs, scratch, stages):
    s_ins, s_outs, aliases, s_scratch = _stage_operands(stages, len(operands), len(out_shape))
    n_in, n_out, n_scr = len(operands), len(out_shape), len(scratch)

    def full_body(*refs):
        in_refs = refs[:n_in]
        s_in_refs = refs[n_in:n_in + len(s_ins)]
        o0 = n_in + len(s_ins)
        out_refs = refs[o0:o0 + n_out]
        s_out_refs = refs[o0 + n_out:o0 + n_out + len(s_outs)]
        c0 = o0 + n_out + len(s_outs)
        scr_refs = refs[c0:c0 + n_scr]
        if stages:
            parts = _stage_refs(stages, s_in_refs, s_out_refs, refs[c0 + n_scr:])
            first, last = _grid_edges(grid)
            pl.when(first)(lambda: _stages_start(stages, parts))
        body(in_refs, out_refs, scr_refs)
        if stages:
            pl.when(last)(lambda: _stages_wait(stages, parts))

    res = pl.pallas_call(
        full_body, name=name, grid=grid, in_specs=list(in_specs) + [ANY] * len(s_ins),
        out_specs=list(out_specs) + [ANY] * len(s_outs), out_shape=list(out_shape) + s_outs,
        input_output_aliases=aliases, scratch_shapes=list(scratch) + s_scratch,
        compiler_params=_cparams(("arbitrary",) * len(grid)),
    )(*operands, *s_ins)
    return list(res[:n_out]), _stage_results(stages, res[n_out:])


def _matmul(name, grid, pairs, extras, outs, acc_shape, epilogue, stages=()):
    n_p = len(pairs)
    n_k = grid[-1]
    dims = [_DIMS[p[4]] for p in pairs]

    def body(in_refs, out, accs):
        ab, ex = in_refs[:2 * n_p], in_refs[2 * n_p:]
        ids = [pl.program_id(ax) for ax in range(len(grid))]
        k = ids[-1]

        @pl.when(k == 0)
        def _():
            for acc in accs:
                acc[...] = jnp.zeros_like(acc)

        for p in range(n_p):
            a = ab[2 * p][...].astype(BF16)
            b = ab[2 * p + 1][...].astype(BF16)
            accs[p][...] += lax.dot_general(a, b, dims[p], preferred_element_type=F32)

        @pl.when(k == n_k - 1)
        def _():
            epilogue([acc[...] for acc in accs], ex, out, ids)

    in_specs = []
    operands = []
    for a, a_spec, b, b_spec, _ in pairs:
        in_specs += [a_spec, b_spec]
        operands += [a, b]
    for e, e_spec in extras:
        in_specs.append(e_spec)
        operands.append(e)
    res, stage_res = _host_call(name, grid, body, operands, in_specs, [o[0] for o in outs], [o[1] for o in outs],
                                [pltpu.VMEM(acc_shape, F32) for _ in pairs], list(stages))
    return (res, stage_res) if stages else res


def _out(res, stages, single=False):
    outs = res[0] if stages else res
    outs = outs[0] if single else outs
    return (outs, res[1]) if stages else outs


def _sds(shape, dtype):
    return jax.ShapeDtypeStruct(shape, dtype)


def _row(n):
    return pl.BlockSpec((1, n), lambda *_: (0, 0))


def _layer_norm(r):
    mu = jnp.mean(r, axis=-1, keepdims=True)
    xc = r - mu
    var = jnp.mean(xc * xc, axis=-1, keepdims=True)
    rstd = lax.rsqrt(var + LN_EPS)
    return xc * rstd, rstd


def _layer_norm_bwd(dxhat, xhat, rstd):
    m1 = jnp.mean(dxhat, axis=-1, keepdims=True)
    m2 = jnp.mean(dxhat * xhat, axis=-1, keepdims=True)
    return rstd * (dxhat - m1 - xhat * m2)


def _colsum(v):
    return jnp.sum(v, axis=0, keepdims=True)


def _fwd_in(x_bf, wg_in, stages=()):
    s, d = x_bf.shape
    inc = wg_in.shape[2]
    tm, tn, tk = _tile(s, 1024), _tile(inc, 1280), _tile(d, 2048)
    nb = inc // tn

    def epi(accs, ex, out, ids):
        out[0][...] = accs[0].astype(BF16)

    return _out(_matmul(
        "fwd_in", (s // tm, N_CHIP * nb, d // tk),
        [(x_bf, pl.BlockSpec((tm, tk), lambda i, j, k: (i, k)),
          wg_in, pl.BlockSpec((None, tk, tn), lambda i, j, k: (j // nb, k, j % nb)), "nn")],
        [], [(_sds((s, N_CHIP * inc), BF16), pl.BlockSpec((tm, tn), lambda i, j, k: (i, j)))],
        (tm, tn), epi, stages), stages, True)


def _fwd_merge(y_pool, y_lru, w_pu, w_lu, z, stages=()):
    s, d = y_pool.shape
    tm, tn, tk = _tile(s, 1024), _tile(d, 1024), _tile(d, 1024)
    ga0, gb0 = 3 * d // tn, 4 * d // tn

    def epi(accs, ex, out, ids):
        sa = _sigmoid(ex[0][...].astype(F32))
        sb = _sigmoid(ex[1][...].astype(F32))
        out[0][...] = (sa * accs[0] + sb * accs[1]).astype(BF16)
        out[1][...] = accs[0].astype(BF16)
        out[2][...] = accs[1].astype(BF16)

    a_spec = pl.BlockSpec((tm, tk), lambda i, j, k: (i, k))
    b_spec = pl.BlockSpec((tk, tn), lambda i, j, k: (k, j))
    o_spec = pl.BlockSpec((tm, tn), lambda i, j, k: (i, j))
    return _out(_matmul(
        "fwd_merge", (s // tm, d // tn, d // tk),
        [(y_pool, a_spec, w_pu, b_spec, "nn"), (y_lru, a_spec, w_lu, b_spec, "nn")],
        [(z, pl.BlockSpec((tm, tn), lambda i, j, k: (i, ga0 + j))), (z, pl.BlockSpec((tm, tn), lambda i, j, k: (i, gb0 + j)))],
        [(_sds((s, d), BF16), o_spec)] * 3, (tm, tn), epi, stages), stages)


def _fwd_out_ln1(m, w_out, x, b_out, g1, b1, stages=()):
    s, d = x.shape
    tm, tk = _tile(s, 256), _tile(d, 2048)

    def epi(accs, ex, out, ids):
        r = DN_ALPHA * ex[0][...] + accs[0] + ex[1][...]
        xhat, rstd = _layer_norm(r)
        out[0][...] = xhat
        out[1][...] = (xhat * ex[2][...] + ex[3][...]).astype(BF16)
        out[2][...] = rstd

    full = pl.BlockSpec((tm, d), lambda i, j, k: (i, 0))
    return _out(_matmul(
        "fwd_out_ln1", (s // tm, 1, d // tk),
        [(m, pl.BlockSpec((tm, tk), lambda i, j, k: (i, k)), w_out, pl.BlockSpec((tk, d), lambda i, j, k: (k, 0)), "nn")],
        [(x, full), (b_out, _row(d)), (g1, _row(d)), (b1, _row(d))],
        [(_sds((s, d), F32), full), (_sds((s, d), BF16), full), (_sds((s, 1), F32), pl.BlockSpec((tm, 1), lambda i, j, k: (i, 0)))],
        (tm, d), epi, stages), stages)


def _fwd_ff1(x1_bf, wg_ff1, b_ff1, stages=()):
    s, d = x1_bf.shape
    fc = wg_ff1.shape[2]
    tm, tn, tk = _tile(s, 1024), _tile(fc, 1024), _tile(d, 2048)
    nb = fc // tn

    def epi(accs, ex, out, ids):
        p = jnp.maximum(accs[0] + ex[0][...], 0.0)
        out[0][...] = (p * p).astype(BF16)

    return _out(_matmul(
        "fwd_ff1", (s // tm, N_CHIP * nb, d // tk),
        [(x1_bf, pl.BlockSpec((tm, tk), lambda i, j, k: (i, k)),
          wg_ff1, pl.BlockSpec((None, tk, tn), lambda i, j, k: (j // nb, k, j % nb)), "nn")],
        [(b_ff1, pl.BlockSpec((1, tn), lambda i, j, k: (0, j)))],
        [(_sds((s, N_CHIP * fc), BF16), pl.BlockSpec((tm, tn), lambda i, j, k: (i, j)))],
        (tm, tn), epi, stages), stages, True)


def _fwd_ff2_ln2_loss(hdn, w_ff2, xhat1, g1, b1, b_ff2, g2, b2, target, stages=()):
    s, f = hdn.shape
    d = xhat1.shape[1]
    tm, tk = _tile(s, 256), _tile(f, 2048)

    def epi(accs, ex, out, ids):
        first = ids[0] == 0
        x1 = ex[0][...] * ex[1][...] + ex[2][...]
        r = DN_ALPHA * x1 + accs[0] + ex[3][...]
        xhat, rstd = _layer_norm(r)
        g2v = ex[4][...]
        err = xhat * g2v + ex[5][...] - ex[6][...]
        part = 0.5 * jnp.sum(jnp.mean(err * err, axis=-1, keepdims=True), axis=0, keepdims=True)
        dy = err * (1.0 / d)
        dr2 = _layer_norm_bwd(dy * g2v, xhat, rstd)
        out[0][...] = dr2
        out[1][...] = dr2.astype(BF16)
        _accum(out[2], _colsum(dy * xhat), first)
        _accum(out[3], _colsum(dy), first)
        _accum(out[4], _colsum(dr2), first)
        _accum(out[5], jnp.broadcast_to(part, (1, 128)), first)

    full = pl.BlockSpec((tm, d), lambda i, j, k: (i, 0))
    return _out(_matmul(
        "fwd_ff2_ln2_loss", (s // tm, 1, f // tk),
        [(hdn, pl.BlockSpec((tm, tk), lambda i, j, k: (i, k)), w_ff2, pl.BlockSpec((tk, d), lambda i, j, k: (k, 0)), "nn")],
        [(xhat1, full), (g1, _row(d)), (b1, _row(d)), (b_ff2, _row(d)), (g2, _row(d)), (b2, _row(d)), (target, full)],
        [(_sds((s, d), F32), full), (_sds((s, d), BF16), full), (_sds((1, d), F32), _row(d)), (_sds((1, d), F32), _row(d)),
         (_sds((1, d), F32), _row(d)), (_sds((1, 128), F32), _row(128))],
        (tm, d), epi, stages), stages)


def _bwd_ff2_in(dr2_bf, w_ff2, hdn, stages=()):
    s, d = dr2_bf.shape
    f = hdn.shape[1]
    tm, tn, tk = _tile(s, 1024), _tile(f, 1024), _tile(d, 2048)

    def epi(accs, ex, out, ids):
        dpre = accs[0] * (2.0 * jnp.sqrt(ex[0][...].astype(F32)))
        out[0][...] = dpre.astype(BF16)
        _accum(out[1], _colsum(dpre), ids[1] == 0)

    return _out(_matmul(
        "bwd_ff2_in", (f // tn, s // tm, d // tk),
        [(dr2_bf, pl.BlockSpec((tm, tk), lambda j, i, k: (i, k)), w_ff2, pl.BlockSpec((tn, tk), lambda j, i, k: (j, k)), "nt")],
        [(hdn, pl.BlockSpec((tm, tn), lambda j, i, k: (i, j)))],
        [(_sds((s, f), BF16), pl.BlockSpec((tm, tn), lambda j, i, k: (i, j))), (_sds((1, f), F32), pl.BlockSpec((1, tn), lambda j, i, k: (0, j)))],
        (tm, tn), epi, stages), stages)


def _bwd_ff1_in_ln1(dpre, wg_ff1, dr2, xhat1, rstd1, g1, stages=()):
    s, f = dpre.shape
    d = xhat1.shape[1]
    fc = wg_ff1.shape[2]
    tm, tk = _tile(s, 256), _tile(fc, 2048)
    nb = fc // tk

    def epi(accs, ex, out, ids):
        first = ids[0] == 0
        xhat = ex[1][...]
        dx1 = accs[0] + DN_ALPHA * ex[0][...]
        dr1 = _layer_norm_bwd(dx1 * ex[3][...], xhat, ex[2][...])
        out[0][...] = dr1
        out[1][...] = dr1.astype(BF16)
        _accum(out[2], _colsum(dx1 * xhat), first)
        _accum(out[3], _colsum(dx1), first)
        _accum(out[4], _colsum(dr1), first)

    full = pl.BlockSpec((tm, d), lambda i, j, k: (i, 0))
    return _out(_matmul(
        "bwd_ff1_in_ln1", (s // tm, 1, f // tk),
        [(dpre, pl.BlockSpec((tm, tk), lambda i, j, k: (i, k)),
          wg_ff1, pl.BlockSpec((None, d, tk), lambda i, j, k: (k // nb, 0, k % nb)), "nt")],
        [(dr2, full), (xhat1, full), (rstd1, pl.BlockSpec((tm, 1), lambda i, j, k: (i, 0))), (g1, _row(d))],
        [(_sds((s, d), F32), full), (_sds((s, d), BF16), full), (_sds((1, d), F32), _row(d)), (_sds((1, d), F32), _row(d)),
         (_sds((1, d), F32), _row(d))],
        (tm, d), epi, stages), stages)


def _bwd_out_in(dr1_bf, w_out, z, pa, pb, stages=()):
    s, d = dr1_bf.shape
    tm, tn, tk = _tile(s, 1024), _tile(d, 1024), _tile(d, 2048)
    ga0, gb0 = 3 * d // tn, 4 * d // tn

    def epi(accs, ex, out, ids):
        dm = accs[0]
        sa = _sigmoid(ex[0][...].astype(F32))
        sb = _sigmoid(ex[1][...].astype(F32))
        out[0][...] = (dm * sa).astype(BF16)
        out[1][...] = (dm * sb).astype(BF16)
        out[2][...] = (dm * ex[2][...].astype(F32) * sa * (1.0 - sa)).astype(BF16)
        out[3][...] = (dm * ex[3][...].astype(F32) * sb * (1.0 - sb)).astype(BF16)

    o_spec = pl.BlockSpec((tm, tn), lambda i, j, k: (i, j))
    return _out(_matmul(
        "bwd_out_in", (s // tm, d // tn, d // tk),
        [(dr1_bf, pl.BlockSpec((tm, tk), lambda i, j, k: (i, k)), w_out, pl.BlockSpec((tn, tk), lambda i, j, k: (j, k)), "nt")],
        [(z, pl.BlockSpec((tm, tn), lambda i, j, k: (i, ga0 + j))), (z, pl.BlockSpec((tm, tn), lambda i, j, k: (i, gb0 + j))),
         (pa, o_spec), (pb, o_spec)],
        [(_sds((s, d), BF16), o_spec)] * 4, (tm, tn), epi, stages), stages)


def _bwd_up_in(name, dp, w_up, stages=()):
    s, d = dp.shape
    n = w_up.shape[0]
    tm, tn, tk = _tile(s, 1024), _tile(n, 1024), _tile(d, 2048)

    def epi(accs, ex, out, ids):
        out[0][...] = accs[0].astype(BF16)

    return _out(_matmul(
        name, (s // tm, n // tn, d // tk),
        [(dp, pl.BlockSpec((tm, tk), lambda i, j, k: (i, k)), w_up, pl.BlockSpec((tn, tk), lambda i, j, k: (j, k)), "nt")],
        [], [(_sds((s, n), BF16), pl.BlockSpec((tm, tn), lambda i, j, k: (i, j)))], (tm, tn), epi, stages), stages, True)


def _bwd_in(dz, wg_in, dr1, stages=()):
    s, d = dr1.shape
    inc = wg_in.shape[2]
    tm, tn, tk = _tile(s, 1024), _tile(d, 1024), _tile(inc, 1280)
    nb = inc // tk

    def epi(accs, ex, out, ids):
        out[0][...] = accs[0] + DN_ALPHA * ex[0][...]

    o_spec = pl.BlockSpec((tm, tn), lambda i, j, k: (i, j))
    return _out(_matmul(
        "bwd_in", (s // tm, d // tn, N_CHIP * nb),
        [(dz, pl.BlockSpec((tm, tk), lambda i, j, k: (i, k)),
          wg_in, pl.BlockSpec((None, tn, tk), lambda i, j, k: (k // nb, j, k % nb)), "nt")],
        [(dr1, o_spec)], [(_sds((s, d), F32), o_spec)], (tm, tn), epi, stages), stages, True)


def _wgrad(name, a, b, col_sharded, stages=()):
    s, ka = a.shape
    n = b.shape[1]
    tm, tk = _tile(ka, 1024), _tile(s, 1024)
    tn = _tile(n // N_CHIP, 1280) if col_sharded else _tile(n, 1024)

    def epi(accs, ex, out, ids):
        out[0][...] = accs[0].astype(BF16)

    if col_sharded:
        nb = (n // N_CHIP) // tn
        o = (_sds((N_CHIP, ka, n // N_CHIP), BF16), pl.BlockSpec((None, tm, tn), lambda i, j, k: (j // nb, i, j % nb)))
    else:
        o = (_sds((ka, n), BF16), pl.BlockSpec((tm, tn), lambda i, j, k: (i, j)))
    res = _out(_matmul(
        name, (ka // tm, n // tn, s // tk),
        [(a, pl.BlockSpec((tk, tm), lambda i, j, k: (k, i)), b, pl.BlockSpec((tk, tn), lambda i, j, k: (k, j)), "tn")],
        [], [o], (tm, tn), epi, stages), stages, True)
    res, stage_res = res if stages else (res, None)
    res = res if col_sharded else res.reshape(N_CHIP, ka // N_CHIP, n)
    return (res, stage_res) if stages else res


def _chunk(s):
    return _tile(s, 512, SUBLANES)


def _zero_pads(ref, s):
    zeros = jnp.zeros((PAD, ref.shape[1]), F32)
    ref[pl.ds(0, PAD), :] = zeros
    ref[pl.ds(PAD + s, PAD), :] = zeros


def _window(ref, t0, t):
    return ref[pl.ds(t0, t + 2 * PAD), :]


def _shift(sup, off, t):
    return sup[PAD + off:PAD + off + t, :]


def _pool_count(t0, t, s, w):
    pos = t0 + lax.broadcasted_iota(jnp.int32, (t, 1), 0)
    return (jnp.minimum(pos + w // 2, s) - jnp.maximum(pos - w // 2, 0)).astype(F32)


def _pool_fwd(z, pool_w, pool_scale, stages=()):
    s = z.shape[0]
    n_g, pg = pool_w.shape[0], pool_w.shape[1]
    assert n_g == len(POOL_WINDOWS) and max(POOL_WINDOWS) // 2 <= PAD
    t = _chunk(s)

    def body(u_ref, w_ref, sc_ref, d_ref, y_ref, pad_ref):
        g = pl.program_id(0)
        _zero_pads(pad_ref, s)
        pad_ref[pl.ds(PAD, s), :] = u_ref[...].astype(F32)
        for gi, w in enumerate(POOL_WINDOWS):
            @pl.when(g == gi)
            def _():
                def step(ch, carry):
                    t0 = pl.multiple_of(ch * t, t)
                    sup = _window(pad_ref, t0, t)
                    acc = _shift(sup, -(w // 2), t)
                    for o in range(-(w // 2) + 1, w // 2):
                        acc = acc + _shift(sup, o, t)
                    dd = (acc * (1.0 / _pool_count(t0, t, s, w)) - _shift(sup, 0, t)).astype(BF16)
                    d_ref[pl.ds(t0, t), :] = dd
                    y = jnp.dot(dd, w_ref[...], preferred_element_type=F32) * sc_ref[...]
                    y_ref[pl.ds(t0, t), :] = y.astype(BF16)
                    return carry

                lax.fori_loop(0, s // t, step, 0)

    blk = pl.BlockSpec((s, pg), lambda g: (0, g))
    res = _host_call(
        "pool_fwd", (n_g,), lambda ins, outs, scr: body(*ins, *outs, *scr), [z, pool_w, pool_scale],
        [blk, pl.BlockSpec((None, pg, pg), lambda g: (g, 0, 0)), pl.BlockSpec((1, pg), lambda g: (0, g))],
        [_sds((s, n_g * pg), BF16)] * 2, [blk, blk], [pltpu.VMEM((s + 2 * PAD, pg), F32)], list(stages))
    return res if stages else res[0]


def _pool_bwd(dsv, dy, pool_w, pool_scale, stages=()):
    s = dsv.shape[0]
    n_g, pg = pool_w.shape[0], pool_w.shape[1]
    t = _chunk(s)

    def body(d_ref, dy_ref, w_ref, sc_ref, du_ref, dw_ref, dsc_ref, epad_ref, dwacc_ref):
        g = pl.program_id(0)
        _zero_pads(epad_ref, s)
        dwacc_ref[...] = jnp.zeros_like(dwacc_ref)
        for gi, w in enumerate(POOL_WINDOWS):
            @pl.when(g == gi)
            def _():
                def first(ch, dsc):
                    t0 = pl.multiple_of(ch * t, t)
                    dd = d_ref[pl.ds(t0, t), :]
                    dyc = dy_ref[pl.ds(t0, t), :].astype(F32)
                    wv = w_ref[...]
                    ypre = jnp.dot(dd, wv, preferred_element_type=F32)
                    dq = (dyc * sc_ref[...]).astype(BF16)
                    dwacc_ref[...] += lax.dot_general(dd, dq, _DIMS["tn"], preferred_element_type=F32)
                    ddv = lax.dot_general(dq, wv, _DIMS["nt"], preferred_element_type=F32)
                    epad_ref[pl.ds(pl.multiple_of(PAD + t0, SUBLANES), t), :] = ddv * (1.0 / _pool_count(t0, t, s, w))
                    return dsc + _colsum(dyc * ypre)

                dsc_ref[...] = lax.fori_loop(0, s // t, first, jnp.zeros((1, pg), F32))

                def second(ch, carry):
                    t0 = pl.multiple_of(ch * t, t)
                    sup = _window(epad_ref, t0, t)
                    acc = _shift(sup, -(w // 2) + 1, t)
                    for o in range(-(w // 2) + 2, w // 2 + 1):
                        acc = acc + _shift(sup, o, t)
                    du_ref[pl.ds(t0, t), :] = (acc - _shift(sup, 0, t) * _pool_count(t0, t, s, w)).astype(BF16)
                    return carry

                lax.fori_loop(0, s // t, second, 0)

        dw_ref[...] = dwacc_ref[...].astype(BF16)

    blk = pl.BlockSpec((s, pg), lambda g: (0, g))
    w_spec = pl.BlockSpec((None, pg, pg), lambda g: (g, 0, 0))
    sc_spec = pl.BlockSpec((1, pg), lambda g: (0, g))
    res = _host_call(
        "pool_bwd", (n_g,), lambda ins, outs, scr: body(*ins, *outs, *scr), [dsv, dy, pool_w, pool_scale],
        [blk, blk, w_spec, sc_spec], [_sds((s, n_g * pg), BF16), _sds((n_g, pg, pg), BF16), _sds((1, n_g * pg), F32)],
        [blk, w_spec, sc_spec], [pltpu.VMEM((s + 2 * PAD, pg), F32), pltpu.VMEM((pg, pg), F32)], list(stages))
    return res if stages else res[0]


def _sigmoid(x):
    return 0.5 * jnp.tanh(0.5 * x) + 0.5


def _softplus(x):
    e = jnp.exp(-jnp.abs(x))
    log1p_e = jnp.where(e < 1e-2, e * (1.0 - e * (0.5 - e * (1.0 / 3.0))), jnp.log(1.0 + e))
    return jnp.maximum(x, 0.0) + log1p_e


_GELU_C = math.sqrt(2.0 / math.pi)


def _gelu(x):
    th = jnp.tanh(_GELU_C * (x + 0.044715 * x * x * x))
    return 0.5 * x * (1.0 + th), th


def _gelu_grad(x, th):
    return 0.5 * (1.0 + th) + 0.5 * x * (1.0 - th * th) * _GELU_C * (1.0 + 3.0 * 0.044715 * x * x)


def _scan_chunk(a_ref, b_ref, o_ref, o_off, carry, t, reverse):
    n = a_ref.shape[1]
    row = lax.broadcasted_iota(jnp.int32, (SUBLANES, n), 0)
    n_groups = t // SUBLANES
    unroll = math.gcd(n_groups, SCAN_UNROLL)
    last = 0 if reverse else SUBLANES - 1

    def step(si, carry):
        for u in range(unroll):
            gi = si * unroll + u
            g = n_groups - 1 - gi if reverse else gi
            r0 = pl.multiple_of(g * SUBLANES, SUBLANES)
            a = a_ref[pl.ds(r0, SUBLANES), :]
            b = b_ref[pl.ds(r0, SUBLANES), :]
            for k in (1, 2, 4):
                keep = row < SUBLANES - k if reverse else row >= k
                sh = SUBLANES - k if reverse else k
                ar = jnp.where(keep, pltpu.roll(a, sh, 0), 1.0)
                br = jnp.where(keep, pltpu.roll(b, sh, 0), 0.0)
                b = a * br + b
                a = a * ar
            o_ref[pl.ds(pl.multiple_of(o_off + r0, SUBLANES), SUBLANES), :] = a * carry + b
            carry = (jnp.broadcast_to(a[last:last + 1, :], a.shape) * carry
                     + jnp.broadcast_to(b[last:last + 1, :], b.shape))
        return carry

    return lax.fori_loop(0, n_groups // unroll, step, carry)


def _lru_params(pk_ref):
    rows = pk_ref[...]
    get = lambda i: rows[i:i + 1, :]
    cw = [get(k) for k in range(4)]
    lam = (get(9), get(10))
    big_l = tuple(-LRU_C * _softplus(-v) for v in lam)
    return cw, get(4), (get(5), get(6)), (get(7), get(8)), lam, big_l


def _conv(sup, cw, cb, t):
    xc = cb + cw[0] * _shift(sup, -2, t)
    for k in range(1, 4):
        xc = xc + cw[k] * _shift(sup, k - 2, t)
    return xc


def _gates(xcb, w_ref, d, bk, ba, bx, big_l):
    pre = jnp.dot(xcb, w_ref[:, pl.ds(d * 2 * bk, 2 * bk)], preferred_element_type=F32)
    r = _sigmoid(pre[:, :bk] + ba[d])
    i = _sigmoid(pre[:, bk:] + bx[d])
    la = big_l[d] * r
    a = jnp.exp(la)
    var = jnp.tanh(-la) * (1.0 + a * a)
    rs = lax.rsqrt(jnp.maximum(var, 1e-30))
    return r, i, a, var * rs, rs


def _lru_specs(s, d, bk):
    u_spec = pl.BlockSpec((s, bk), lambda h: (0, d // bk + h))
    ug_spec = pl.BlockSpec((s, bk), lambda h: (0, 2 * d // bk + h))
    w_spec = pl.BlockSpec((None, bk, 4 * bk), lambda h: (h, 0, 0))
    pk_spec = pl.BlockSpec((None, 16, bk), lambda h: (h, 0, 0))
    blk = pl.BlockSpec((s, bk), lambda h: (0, h))
    return u_spec, ug_spec, w_spec, pk_spec, blk


def _lru_fwd(z, gatew, pk, stages=()):
    s = z.shape[0]
    n_h, bk = gatew.shape[0], gatew.shape[1]
    d = n_h * bk
    t = _chunk(s)
    n_ch = s // t

    def body(u_ref, ug_ref, w_ref, pk_ref, y_ref, upad, h0buf, abuf, bbuf, xcbuf, h1buf):
        _zero_pads(upad, s)
        upad[pl.ds(PAD, s), :] = u_ref[...].astype(F32)
        cw, cb, ba, bx, _, big_l = _lru_params(pk_ref)
        zero = jnp.zeros((SUBLANES, bk), F32)

        def fill(xc, dr):
            _, i, a, sq, _ = _gates(xc.astype(BF16), w_ref, dr, bk, ba, bx, big_l)
            abuf[...] = a
            bbuf[...] = sq * i * xc

        def up(ch, carry):
            t0 = pl.multiple_of(ch * t, t)
            xc = _conv(_window(upad, t0, t), cw, cb, t)
            xcbuf[pl.ds(t0, t), :] = xc
            fill(xc, 0)
            return _scan_chunk(abuf, bbuf, h0buf, t0, carry, t, False)

        lax.fori_loop(0, n_ch, up, zero)

        def down(ci, carry):
            t0 = pl.multiple_of((n_ch - 1 - ci) * t, t)
            fill(xcbuf[pl.ds(t0, t), :], 1)
            carry = _scan_chunk(abuf, bbuf, h1buf, 0, carry, t, True)
            gl, _ = _gelu(ug_ref[pl.ds(t0, t), :].astype(F32))
            y_ref[pl.ds(t0, t), :] = ((h0buf[pl.ds(t0, t), :] + h1buf[...]) * gl).astype(BF16)
            return carry

        lax.fori_loop(0, n_ch, down, zero)

    u_spec, ug_spec, w_spec, pk_spec, blk = _lru_specs(s, d, bk)
    res = _host_call(
        "lru_fwd", (n_h,), lambda ins, outs, scr: body(*ins, *outs, *scr), [z, z, gatew, pk],
        [u_spec, ug_spec, w_spec, pk_spec], [_sds((s, d), BF16)], [blk],
        [pltpu.VMEM((s + 2 * PAD, bk), F32), pltpu.VMEM((s, bk), F32), pltpu.VMEM((t, bk), F32), pltpu.VMEM((t, bk), F32),
         pltpu.VMEM((s, bk), F32), pltpu.VMEM((t, bk), F32)], list(stages))
    return (res[0][0], res[1]) if stages else res[0][0]


def _lru_grads(lam_, hnb, a, sq, rs, r, i, xc, xcb, w_ref, dwacc, d, big_l, acc):
    bk = xc.shape[1]
    dba, dbx, dl = acc
    q = lam_ * i * xc
    dla = lam_ * hnb * a - q * (a * a) * rs
    dpr = dla * big_l * r * (1.0 - r)
    dpi = q * sq * (1.0 - i)
    dprb, dpib = dpr.astype(BF16), dpi.astype(BF16)
    c0 = d * 2 * bk
    dxc = (lam_ * sq * i
           + lax.dot_general(dprb, w_ref[:, pl.ds(c0, bk)], _DIMS["nt"], preferred_element_type=F32)
           + lax.dot_general(dpib, w_ref[:, pl.ds(c0 + bk, bk)], _DIMS["nt"], preferred_element_type=F32))
    dwacc[:, pl.ds(c0, bk)] += lax.dot_general(xcb, dprb, _DIMS["tn"], preferred_element_type=F32)
    dwacc[:, pl.ds(c0 + bk, bk)] += lax.dot_general(xcb, dpib, _DIMS["tn"], preferred_element_type=F32)
    return dxc, (dba + _colsum(dpr), dbx + _colsum(dpi), dl + _colsum(dla * r))


def _lru_bwd(z, dy, gatew, pk, stages=()):
    s = z.shape[0]
    n_h, bk = gatew.shape[0], gatew.shape[1]
    d = n_h * bk
    t = _chunk(s)
    n_ch = s // t

    def body(u_ref, ug_ref, dy_ref, w_ref, pk_ref, du_ref, dug_ref, dw_ref, dpk_ref,
             upad, h0pad, h1pad, dxpad, abuf, bbuf, lbuf, dwacc, edge, xcbuf):
        for ref in (upad, h0pad, h1pad, dxpad):
            _zero_pads(ref, s)
        upad[pl.ds(PAD, s), :] = u_ref[...].astype(F32)
        dwacc[...] = jnp.zeros_like(dwacc)
        cw, cb, ba, bx, lam, big_l = _lru_params(pk_ref)
        zero = jnp.zeros((SUBLANES, bk), F32)
        zrow = jnp.zeros((1, bk), F32)
        rowi = lax.broadcasted_iota(jnp.int32, (t, bk), 0)

        def at(t0):
            return pl.ds(pl.multiple_of(PAD + t0, SUBLANES), t)

        def conv_in(t0):
            xc = xcbuf[pl.ds(t0, t), :]
            return xc, xc.astype(BF16)

        def dh_of(t0):
            ug = ug_ref[pl.ds(t0, t), :].astype(F32)
            gl, th = _gelu(ug)
            dyv = dy_ref[pl.ds(t0, t), :].astype(F32)
            return dyv * gl, dyv * _gelu_grad(ug, th)

        def sweep1(ch, carry):
            t0 = pl.multiple_of(ch * t, t)
            xc = _conv(_window(upad, t0, t), cw, cb, t)
            xcbuf[pl.ds(t0, t), :] = xc
            _, i, a, sq, _ = _gates(xc.astype(BF16), w_ref, 0, bk, ba, bx, big_l)
            abuf[...] = a
            bbuf[...] = sq * i * xc
            return _scan_chunk(abuf, bbuf, h0pad, PAD + t0, carry, t, False)

        lax.fori_loop(0, n_ch, sweep1, zero)

        edge[...] = zero

        def sweep2(ci, st):
            carry_h, carry_l, acc = st
            t0 = pl.multiple_of((n_ch - 1 - ci) * t, t)
            xc, xcb = conv_in(t0)
            _, i1, a1, sq1, _ = _gates(xcb, w_ref, 1, bk, ba, bx, big_l)
            abuf[...] = a1
            bbuf[...] = sq1 * i1 * xc
            carry_h = _scan_chunk(abuf, bbuf, h1pad, PAD + t0, carry_h, t, True)
            dh, dgl = dh_of(t0)
            dug_ref[pl.ds(t0, t), :] = (dgl * (h0pad[at(t0), :] + h1pad[at(t0), :])).astype(BF16)
            r0, i0, a0, sq0, rs0 = _gates(xcb, w_ref, 0, bk, ba, bx, big_l)
            abuf[...] = jnp.where(rowi == t - 1, edge[0:1, :], pltpu.roll(a0, t - 1, 0))
            bbuf[...] = dh
            carry_l = _scan_chunk(abuf, bbuf, lbuf, 0, carry_l, t, True)
            edge[...] = jnp.broadcast_to(a0[0:1, :], (SUBLANES, bk))
            hprev = _shift(_window(h0pad, t0, t), -1, t)
            dxc, acc = _lru_grads(lbuf[...], hprev, a0, sq0, rs0, r0, i0, xc, xcb, w_ref, dwacc, 0, big_l[0], acc)
            dxpad[at(t0), :] = dxc
            return carry_h, carry_l, acc

        _, _, acc0 = lax.fori_loop(0, n_ch, sweep2, (zero, zero, (zrow, zrow, zrow)))

        edge[...] = zero

        def sweep3(ch, st):
            carry_l, acc = st
            t0 = pl.multiple_of(ch * t, t)
            xc, xcb = conv_in(t0)
            r1, i1, a1, sq1, rs1 = _gates(xcb, w_ref, 1, bk, ba, bx, big_l)
            dh, _ = dh_of(t0)
            abuf[...] = jnp.where(rowi == 0, edge[0:1, :], pltpu.roll(a1, 1, 0))
            bbuf[...] = dh
            carry_l = _scan_chunk(abuf, bbuf, lbuf, 0, carry_l, t, False)
            edge[...] = jnp.broadcast_to(a1[t - 1:t, :], (SUBLANES, bk))
            hnext = _shift(_window(h1pad, t0, t), 1, t)
            dxc, acc = _lru_grads(lbuf[...], hnext, a1, sq1, rs1, r1, i1, xc, xcb, w_ref, dwacc, 1, big_l[1], acc)
            dxpad[at(t0), :] += dxc
            return carry_l, acc

        _, acc1 = lax.fori_loop(0, n_ch, sweep3, (zero, (zrow, zrow, zrow)))

        def sweep4(ch, st):
            t0 = pl.multiple_of(ch * t, t)
            sdx = _window(dxpad, t0, t)
            su = _window(upad, t0, t)
            dxc = _shift(sdx, 0, t)
            du = cw[0] * _shift(sdx, 2, t) + cw[1] * _shift(sdx, 1, t) + cw[2] * dxc + cw[3] * _shift(sdx, -1, t)
            du_ref[pl.ds(t0, t), :] = du.astype(BF16)
            return tuple(st[k] + _colsum(dxc * _shift(su, k - 2, t)) for k in range(4)) + (st[4] + _colsum(dxc),)

        conv_g = lax.fori_loop(0, n_ch, sweep4, (zrow,) * 5)

        dpk_ref[...] = jnp.zeros_like(dpk_ref)
        rows = list(conv_g) + [acc0[0], acc1[0], acc0[1], acc1[1],
                               acc0[2] * LRU_C * _sigmoid(-lam[0]), acc1[2] * LRU_C * _sigmoid(-lam[1])]
        for k, v in enumerate(rows):
            dpk_ref[pl.ds(k, 1), :] = v
        dw_ref[...] = dwacc[...].astype(BF16)

    u_spec, ug_spec, w_spec, pk_spec, blk = _lru_specs(s, d, bk)
    padded = pltpu.VMEM((s + 2 * PAD, bk), F32)
    chunk = pltpu.VMEM((t, bk), F32)
    res = _host_call(
        "lru_bwd", (n_h,), lambda ins, outs, scr: body(*ins, *outs, *scr), [z, z, dy, gatew, pk],
        [u_spec, ug_spec, blk, w_spec, pk_spec],
        [_sds((s, d), BF16), _sds((s, d), BF16), _sds((n_h, bk, 4 * bk), BF16), _sds((n_h, 16, bk), F32)],
        [blk, blk, w_spec, pk_spec],
        [padded, padded, padded, padded, chunk, chunk, chunk, pltpu.VMEM((bk, 4 * bk), F32),
         pltpu.VMEM((SUBLANES, bk), F32), pltpu.VMEM((s, bk), F32)], list(stages))
    return res if stages else res[0]


def _scalar(v):
    return jnp.reshape(v, (1,)).astype(jnp.int32)


def _add_sibling(g, r, c):
    _, rows, cols = g.shape
    rh = rows // 2
    tr = _tile(rh, 512, 16)
    nr = rh // tr

    def body(c_ref, g_ref, r_ref, o_ref):
        o_ref[...] = (g_ref[...].astype(F32) + r_ref[...].astype(F32)).astype(BF16)

    spec = pl.BlockSpec((None, tr, cols), lambda k, i, c_ref: (k, i, 0))
    return pl.pallas_call(
        body, name="add_sibling", out_shape=_sds((N_CHIP, rh, cols), BF16),
        grid_spec=pltpu.PrefetchScalarGridSpec(
            num_scalar_prefetch=1, grid=(N_CHIP, nr),
            in_specs=[pl.BlockSpec((None, tr, cols), lambda k, i, c_ref: (k, c_ref[0] * nr + i, 0)), spec], out_specs=spec),
        compiler_params=_cparams(("arbitrary", "arbitrary")),
    )(_scalar(c), g, r)


def _sum_chips(p, rcv, k_me, c):
    _, rh, cols = p.shape
    tr = _tile(rh, 512, 16)
    nr = rh // tr

    def body(kc_ref, p_ref, r_ref, o_ref):
        acc = p_ref[...].astype(F32)
        for j in range(3):
            acc = acc + r_ref[j].astype(F32)
        o_ref[...] = acc

    return pl.pallas_call(
        body, name="sum_chips", out_shape=_sds((2 * rh, cols), F32),
        grid_spec=pltpu.PrefetchScalarGridSpec(
            num_scalar_prefetch=1, grid=(nr,),
            in_specs=[pl.BlockSpec((None, tr, cols), lambda i, kc_ref: (kc_ref[0], i, 0)),
                      pl.BlockSpec((3, tr, cols), lambda i, kc_ref: (0, i, 0))],
            out_specs=pl.BlockSpec((tr, cols), lambda i, kc_ref: (kc_ref[1] * nr + i, 0))),
        compiler_params=_cparams(("arbitrary",)),
    )(jnp.stack([k_me, c]).astype(jnp.int32), p, rcv)


def _sum_devices(g):
    def body(g_ref, o_ref):
        acc = g_ref[0]
        for dev in range(1, N_DEV):
            acc = acc + g_ref[dev]
        o_ref[...] = acc

    return pl.pallas_call(body, name="sum_devices", out_shape=_sds(g.shape[1:], F32))(g)


def _adamw(w, g, m, v):
    rows, cols = w.shape
    tr = _tile(rows, 256, SUBLANES)

    def body(w_ref, g_ref, m_ref, v_ref, d_ref, nm_ref, nv_ref):
        gv = g_ref[...]
        nm = ADAM_B1 * m_ref[...] + (1.0 - ADAM_B1) * gv
        nv = ADAM_B2 * v_ref[...] + (1.0 - ADAM_B2) * (gv * gv)
        m_hat = nm / (1.0 - ADAM_B1 ** ADAM_STEP)
        v_hat = nv / (1.0 - ADAM_B2 ** ADAM_STEP)
        d_ref[...] = -ADAM_LR * (m_hat / (jnp.sqrt(v_hat) + ADAM_EPS) + ADAM_WD * w_ref[...])
        nm_ref[...] = nm
        nv_ref[...] = nv

    spec = pl.BlockSpec((tr, cols), lambda i: (i, 0))
    return pl.pallas_call(
        body, name="adamw", grid=(rows // tr,), in_specs=[spec] * 4, out_specs=[spec] * 3,
        out_shape=[_sds((rows, cols), F32)] * 3, compiler_params=_cparams(("arbitrary",)),
    )(w, g, m, v)


def _pack(vs, unit):
    flat = jnp.concatenate([v.reshape(-1).astype(F32) for v in vs])
    pad = (-flat.shape[0]) % unit
    if pad:
        flat = jnp.concatenate([flat, jnp.zeros((pad,), F32)])
    return flat.reshape(-1, 128)


def _unpack(p, like):
    flat = p.reshape(-1)
    out, off = [], 0
    for v in like:
        n = math.prod(v.shape)
        out.append(flat[off:off + n].reshape(v.shape))
        off += n
    return out


def kernel(x, w_in, pool_w, pool_scale, conv_w, conv_b, lru_wa, lru_ba, lru_wx, lru_bx, lru_lambda, w_pool_up, w_lru_up, w_out, b_out, ln1_g, ln1_b, w_ff1, b_ff1, w_ff2, b_ff2, ln2_g, ln2_b, loss_target, m_w_in, m_pool_w, m_pool_scale, m_conv_w, m_conv_b, m_lru_wa, m_lru_ba, m_lru_wx, m_lru_bx, m_lru_lambda, m_w_pool_up, m_w_lru_up, m_w_out, m_b_out, m_ln1_g, m_ln1_b, m_w_ff1, m_b_ff1, m_w_ff2, m_b_ff2, m_ln2_g, m_ln2_b, v_w_in, v_pool_w, v_pool_scale, v_conv_w, v_conv_b, v_lru_wa, v_lru_ba, v_lru_wx, v_lru_bx, v_lru_lambda, v_w_pool_up, v_w_lru_up, v_w_out, v_b_out, v_ln1_g, v_ln1_b, v_w_ff1, v_b_ff1, v_w_ff2, v_b_ff2, v_ln2_g, v_ln2_b):
    given = dict(locals())
    wt = {n: given[n] for n in WEIGHTS}
    mom = {n: given["m_" + n] for n in WEIGHTS}
    vel = {n: given["v_" + n] for n in WEIGHTS}

    ix, iy, ic = _mesh_pos()
    k_me = 2 * ix + iy
    s, d = x.shape[1], x.shape[2]
    ds = d // N_CHIP
    n_g, pgs, pg = pool_w.shape[1], pool_w.shape[2], pool_w.shape[3]
    n_h, bks, bk = lru_wa.shape[2], lru_wa.shape[3], lru_wa.shape[4]
    f = b_ff1.shape[1]
    x2 = x[0]
    x_bf = x2.astype(BF16)
    vec = lambda a: a.reshape(1, -1)

    sharded_vecs = [conv_w[0], lru_ba[0], lru_bx[0], lru_lambda[0]]
    rows_sv = jnp.concatenate(sharded_vecs + [jnp.zeros((6, ds), F32)], axis=0)
    sv = _all_gather_small(rows_sv)
    sv = sv.reshape(N_CHIP, 2, 16, ds)[:, 0].transpose(1, 0, 2).reshape(16, d)
    conv_w_f, ba_f, bx_f, lam_f = sv[0:4], sv[4:6], sv[6:8], sv[8:10]
    pk = jnp.concatenate([conv_w_f, conv_b, ba_f, bx_f, lam_f, jnp.zeros((5, d), F32)], axis=0)
    pk = pk.reshape(16, n_h, bk).transpose(1, 0, 2)

    def gate_stack(wa, wx):
        return jnp.stack([wa[0], wx[0]], axis=1)

    mats = {
        "w_in": w_in[0], "w_pool_up": w_pool_up[0], "w_lru_up": w_lru_up[0], "w_out": w_out[0],
        "w_ff1": w_ff1[0], "w_ff2": w_ff2[0],
        "pool_w": pool_w[0].reshape(n_g * pgs, pg),
        "gate_w": gate_stack(lru_wa, lru_wx).reshape(4 * n_h * bks, bk),
    }
    names = list(mats)
    placed = {n: _cast_place(mats[n], k_me) for n in names}

    def add_sibling(gs, swapped):
        return [_add_sibling(g, r, ic) for g, r in zip(gs, swapped)]

    def sum_chips(ps, received):
        return [_sum_chips(p, r, k_me, ic) for p, r in zip(ps, received)]

    first = ["w_in", "pool_w", "gate_w"]
    (bufs,) = _run_stages("gather_first_ici", [_gather_ici([placed[n] for n in first])])
    (bufs,) = _run_stages("gather_first_d2d", [_gather_d2d(bufs)])
    wg_in = bufs[0]
    wf_pool = bufs[1].reshape(N_CHIP, n_g, pgs, pg).transpose(1, 0, 2, 3).reshape(n_g, pg, pg)
    wf_gate = bufs[2].reshape(N_CHIP, 2, 2, n_h, bks, bk).transpose(3, 0, 4, 1, 2, 5).reshape(n_h, bk, 4 * bk)

    z, (bufs,) = _fwd_in(x_bf, wg_in, stages=[_gather_ici([placed[n] for n in ("w_pool_up", "w_lru_up", "w_out")])])
    (d_pool, y_pool), (bufs,) = _pool_fwd(z, wf_pool, pool_scale, stages=[_gather_d2d(bufs)])
    wf_pu, wf_lu, wf_out = (b.reshape(d, d) for b in bufs)
    y_lru, (bufs,) = _lru_fwd(z, wf_gate, pk, stages=[_gather_ici([placed["w_ff1"]])])
    (m_mix, p_a, p_b), (bufs,) = _fwd_merge(y_pool, y_lru, wf_pu, wf_lu, z, stages=[_gather_d2d(bufs)])
    wg_ff1 = bufs[0]
    xhat1, x1_bf, rstd1 = _fwd_out_ln1(m_mix, wf_out, x2, b_out, ln1_g, ln1_b)
    hdn, (bufs,) = _fwd_ff1(x1_bf, wg_ff1, b_ff1, stages=[_gather_ici([placed["w_ff2"]])])
    (bufs,) = _run_stages("gather_ff2_d2d", [_gather_d2d(bufs)])
    wf_ff2 = bufs[0].reshape(f, d)
    dr2, dr2_bf, g_ln2_g, g_ln2_b, g_b_ff2, loss_part = _fwd_ff2_ln2_loss(
        hdn, wf_ff2, xhat1, ln1_g, ln1_b, b_ff2, ln2_g, ln2_b, loss_target[0])

    dpre, g_b_ff1 = _bwd_ff2_in(dr2_bf, wf_ff2, hdn)
    g_ff = [_wgrad("wgrad_ff1", x1_bf, dpre, True), _wgrad("wgrad_ff2", hdn, dr2_bf, False)]
    (dr1, dr1_bf, g_ln1_g, g_ln1_b, g_b_out), (swapped,) = _bwd_ff1_in_ln1(
        dpre, wg_ff1, dr2, xhat1, rstd1, ln1_g, stages=[_swap_halves(g_ff)])
    sums_ff = add_sibling(g_ff, swapped)
    dp_a, dp_b, dg_a, dg_b = _bwd_out_in(dr1_bf, wf_out, z, p_a, p_b)
    dy_pool = _bwd_up_in("bwd_pool_up_in", dp_a, wf_pu)
    dy_lru = _bwd_up_in("bwd_lru_up_in", dp_b, wf_lu)
    g_mix = [_wgrad("wgrad_pool_up", y_pool, dp_a, False), _wgrad("wgrad_lru_up", y_lru, dp_b, False),
             _wgrad("wgrad_out", m_mix, dr1_bf, False)]
    (du_pool, g_pool_w, g_pool_scale), (swapped,) = _pool_bwd(
        d_pool, dy_pool, wf_pool, pool_scale, stages=[_swap_halves(g_mix)])
    sums_mix = add_sibling(g_mix, swapped)
    (du_lru, du_gate, g_gate_w, g_pk), (recv_ff, recv_mix) = _lru_bwd(
        z, dy_lru, wf_gate, pk, stages=[_scatter_chips(sums_ff), _scatter_chips(sums_mix)])
    halves = sum_chips(sums_ff + sums_mix, recv_ff + recv_mix)
    g_small = [g_pool_w.reshape(n_g, N_CHIP, pgs, pg).transpose(1, 0, 2, 3).reshape(N_CHIP, n_g * pgs, pg),
               g_gate_w.reshape(n_h, N_CHIP, bks, 2, 2, bk).transpose(1, 3, 4, 0, 2, 5).reshape(N_CHIP, 4 * n_h * bks, bk)]
    dz = jnp.concatenate([du_pool, du_lru, du_gate, dg_a, dg_b], axis=1)
    grad_x, (joined, swapped) = _bwd_in(dz, wg_in, dr1, stages=[_join_halves(halves), _swap_halves(g_small)])
    g_mat = dict(zip(["w_ff1", "w_ff2", "w_pool_up", "w_lru_up", "w_out"], joined))
    sums_small = add_sibling(g_small, swapped)
    g_in, (recv_small,) = _wgrad("wgrad_in", x_bf, dz, True, stages=[_scatter_chips(sums_small)])
    halves = sum_chips(sums_small, recv_small)
    (swapped,) = _run_stages("swap_in", [_swap_halves([g_in])])
    sums_in = add_sibling([g_in], swapped)
    (recv_in,) = _run_stages("scatter_in", [_scatter_chips(sums_in)])
    halves += sum_chips(sums_in, recv_in)
    (joined,) = _run_stages("join_last", [_join_halves(halves)])
    g_mat.update(zip(["pool_w", "gate_w", "w_in"], joined))

    def stacked(tree):
        return gate_stack(tree["lru_wa"], tree["lru_wx"]).reshape(4 * n_h * bks, bk)

    res = {}
    for n in names:
        if n == "gate_w":
            upd = _adamw(stacked(wt), g_mat[n], stacked(mom), stacked(vel))
            outs = [o.reshape(2, 2, n_h, bks, bk) for o in (g_mat[n],) + tuple(upd)]
            res["lru_wa"] = [o[:, 0][None] for o in outs]
            res["lru_wx"] = [o[:, 1][None] for o in outs]
        else:
            shp = wt[n].shape
            upd = _adamw(wt[n].reshape(mats[n].shape), g_mat[n], mom[n].reshape(mats[n].shape), vel[n].reshape(mats[n].shape))
            res[n] = [o.reshape(shp) for o in (g_mat[n],) + tuple(upd)]

    g_pk = g_pk.transpose(1, 0, 2).reshape(16, d)
    vec_full = {
        "pool_scale": g_pool_scale, "conv_w": g_pk[0:4], "conv_b": g_pk[4:5],
        "lru_ba": g_pk[5:7], "lru_bx": g_pk[7:9], "lru_lambda": g_pk[9:11],
        "b_out": g_b_out, "ln1_g": g_ln1_g, "ln1_b": g_ln1_b, "b_ff1": g_b_ff1, "b_ff2": g_b_ff2,
        "ln2_g": g_ln2_g, "ln2_b": g_ln2_b,
    }
    vnames = list(vec_full)
    vg = _sum_devices(_all_gather_small(_pack([vec_full[n] for n in vnames], 1024)))
    vg = dict(zip(vnames, _unpack(vg, [vec_full[n] for n in vnames])))
    for n in ("conv_w", "lru_ba", "lru_bx", "lru_lambda"):
        vg[n] = lax.dynamic_slice_in_dim(vg[n], k_me * ds, ds, axis=1)
    vg = {n: vg[n].reshape(wt[n].shape) for n in vnames}
    upd = _adamw(_pack([wt[n] for n in vnames], 1024), _pack([vg[n] for n in vnames], 1024),
                 _pack([mom[n] for n in vnames], 1024), _pack([vel[n] for n in vnames], 1024))
    upd = [_unpack(u, [wt[n] for n in vnames]) for u in upd]
    for i, n in enumerate(vnames):
        res[n] = [vg[n], upd[0][i], upd[1][i], upd[2][i]]

    loss = lax.psum(loss_part[0, 0], ("x", "y", "c"))
    return (loss, grad_x[None], *[res[n][0] for n in WEIGHTS], *[res[n][1] for n in WEIGHTS],
            *[res[n][2] for n in WEIGHTS], *[res[n][3] for n in WEIGHTS])
```

```python
import functools
import math

import jax
import jax.numpy as jnp
from jax import lax
from jax.experimental import pallas as pl
from jax.experimental.pallas import tpu as pltpu

F32 = jnp.float32
BF16 = jnp.bfloat16
MESH = pl.DeviceIdType.MESH
ANY = pl.BlockSpec(memory_space=pl.ANY)

N_CHIP = 4
N_DEV = 8
VMEM_LIMIT_BYTES = 56 * 1024 * 1024
SUBLANES = 8
PAD = 8
SCAN_UNROLL = 8

POOL_WINDOWS = (2, 4, 8, 16)
LRU_C = 8.0
DN_ALPHA = 2.0 ** 0.25
LN_EPS = 1e-5
ADAM_LR, ADAM_B1, ADAM_B2, ADAM_EPS, ADAM_WD, ADAM_STEP = 0.001, 0.9, 0.999, 1e-08, 0.01, 10

WEIGHTS = ("w_in", "pool_w", "pool_scale", "conv_w", "conv_b", "lru_wa", "lru_ba", "lru_wx", "lru_bx", "lru_lambda",
           "w_pool_up", "w_lru_up", "w_out", "b_out", "ln1_g", "ln1_b", "w_ff1", "b_ff1", "w_ff2", "b_ff2", "ln2_g", "ln2_b")


def _cparams(sem=None):
    return pltpu.CompilerParams(dimension_semantics=sem, vmem_limit_bytes=VMEM_LIMIT_BYTES)


def _tile(dim, pref, unit=128):
    if dim <= pref:
        return dim
    t = (pref // unit) * unit
    while t > unit and dim % t:
        t -= unit
    assert dim % t == 0, (dim, pref)
    return t


def _mesh_pos():
    x, y, c = lax.axis_index("x"), lax.axis_index("y"), lax.axis_index("c")
    return x, y, c


def _other_chips(x, y):
    return [(1 - x, y), (x, 1 - y), (1 - x, 1 - y)]


def _all_gather_small(v):
    m_per, n = v.shape

    def body(x_ref, out_ref, send_sems, recv_sems, local_sem):
        x, y, c = _mesh_pos()
        me, sibling = (x, y, c), (x, y, 1 - c)
        chips = _other_chips(x, y)

        def rows(px, py, pc):
            return out_ref.at[4 * px + 2 * py + pc]

        def copy(k, block, to, src=None):
            return pltpu.make_async_remote_copy(
                src_ref=rows(*block) if src is None else src, dst_ref=rows(*block),
                send_sem=send_sems.at[k], recv_sem=recv_sems.at[k], device_id=to, device_id_type=MESH)

        mine = pltpu.make_async_copy(x_ref, rows(*me), local_sem)
        mine.start()
        first = [copy(0, me, sibling, src=x_ref)]
        first += [copy(1 + j, me, (*chip, c), src=x_ref) for j, chip in enumerate(chips)]
        for cp in first:
            cp.start()
        passed = [copy(4 + j, (*chip, c), sibling) for j, chip in enumerate(chips)]
        for j, chip in enumerate(chips):
            copy(1 + j, (*chip, c), me).wait_recv()
            passed[j].start()
        copy(0, sibling, me).wait_recv()
        for j, chip in enumerate(chips):
            copy(4 + j, (*chip, 1 - c), me).wait_recv()
        for cp in first + passed:
            cp.wait_send()
        mine.wait()

    return pl.pallas_call(
        body, name="all_gather_small",
        out_shape=jax.ShapeDtypeStruct((N_DEV, m_per, n), v.dtype),
        in_specs=[pl.BlockSpec(memory_space=pltpu.VMEM)],
        out_specs=pl.BlockSpec(memory_space=pltpu.VMEM),
        scratch_shapes=[pltpu.SemaphoreType.DMA((7,)), pltpu.SemaphoreType.DMA((7,)), pltpu.SemaphoreType.DMA],
    )(v)


class _Stage:
    def __init__(self, srcs, bufs, news, n_sems, copies):
        self.srcs, self.bufs, self.news, self.n_sems, self.copies = list(srcs), list(bufs), list(news), n_sems, copies


def _remote(src, dst, send_sems, recv_sems, s, to):
    return pltpu.make_async_remote_copy(src_ref=src, dst_ref=dst, send_sem=send_sems.at[s], recv_sem=recv_sems.at[s],
                                        device_id=to, device_id_type=MESH)


def _stage_operands(stages, n_in, n_out):
    ins, outs, aliases, scratch = [], [], {}, []
    for st in stages:
        for i in range(len(st.bufs)):
            aliases[n_in + len(ins) + len(st.srcs) + i] = n_out + len(outs) + i
        ins += st.srcs + st.bufs
        outs += [jax.ShapeDtypeStruct(b.shape, b.dtype) for b in st.bufs] + st.news
        scratch += [pltpu.SemaphoreType.DMA((st.n_sems,)), pltpu.SemaphoreType.DMA((st.n_sems,))]
    return ins, outs, aliases, scratch


def _stage_refs(stages, in_refs, out_refs, sem_refs):
    parts, i, o = [], 0, 0
    for n, st in enumerate(stages):
        src = in_refs[i:i + len(st.srcs)]
        i += len(st.srcs) + len(st.bufs)
        buf = out_refs[o:o + len(st.bufs)]
        new = out_refs[o + len(st.bufs):o + len(st.bufs) + len(st.news)]
        o += len(st.bufs) + len(st.news)
        parts.append((src, buf, new, sem_refs[2 * n], sem_refs[2 * n + 1]))
    return parts


def _stage_results(stages, res):
    out, o = [], 0
    for st in stages:
        n = len(st.bufs) + len(st.news)
        out.append(list(res[o:o + n]))
        o += n
    return out


def _stages_start(stages, parts):
    for st, part in zip(stages, parts):
        for cp in st.copies(*part)[0]:
            cp.start()


def _stages_wait(stages, parts):
    for st, part in zip(stages, parts):
        started, landing = st.copies(*part)
        for cp in landing:
            cp.wait_recv()
        for cp in started:
            cp.wait_send()


def _run_stages(name, stages):
    ins, outs, aliases, scratch = _stage_operands(stages, 0, 0)

    def body(*refs):
        parts = _stage_refs(stages, refs[:len(ins)], refs[len(ins):len(ins) + len(outs)], refs[len(ins) + len(outs):])
        for st, part in zip(stages, parts):
            _stages_start([st], [part])
            _stages_wait([st], [part])

    res = pl.pallas_call(
        body, name=name, out_shape=outs, in_specs=[ANY] * len(ins), out_specs=[ANY] * len(outs),
        input_output_aliases=aliases, scratch_shapes=scratch)(*ins)
    return _stage_results(stages, res)


def _gather_ici(ts):
    def copies(src, buf, new, send_sems, recv_sems):
        x, y, c = _mesh_pos()
        started, landing = [], []
        for t in range(len(ts)):
            rh = ts[t].shape[1] // 2
            rows = pl.ds(c * rh, rh)
            for j, chip in enumerate(_other_chips(x, y)):
                mine = buf[t].at[2 * x + y, rows]
                theirs = buf[t].at[2 * chip[0] + chip[1], rows]
                started.append(_remote(mine, mine, send_sems, recv_sems, 3 * t + j, (*chip, c)))
                landing.append(_remote(theirs, theirs, send_sems, recv_sems, 3 * t + j, (x, y, c)))
        return started, landing

    return _Stage([], ts, [], 3 * len(ts), copies)


def _gather_d2d(ts):
    def copies(src, buf, new, send_sems, recv_sems):
        x, y, c = _mesh_pos()
        started, landing = [], []
        for t in range(len(ts)):
            rh = ts[t].shape[1] // 2
            for j, chip in enumerate(_other_chips(x, y)):
                got = buf[t].at[2 * chip[0] + chip[1], pl.ds(c * rh, rh)]
                other = buf[t].at[2 * chip[0] + chip[1], pl.ds((1 - c) * rh, rh)]
                started.append(_remote(got, got, send_sems, recv_sems, 3 * t + j, (x, y, 1 - c)))
                landing.append(_remote(other, other, send_sems, recv_sems, 3 * t + j, (x, y, c)))
        return started, landing

    return _Stage([], ts, [], 3 * len(ts), copies)


def _swap_halves(gs):
    def copies(src, buf, new, send_sems, recv_sems):
        x, y, c = _mesh_pos()
        started, landing = [], []
        for t in range(len(gs)):
            rh = gs[t].shape[1] // 2
            started.append(_remote(src[t].at[:, pl.ds((1 - c) * rh, rh)], new[t], send_sems, recv_sems, t, (x, y, 1 - c)))
            landing.append(_remote(new[t], new[t], send_sems, recv_sems, t, (x, y, c)))
        return started, landing

    news = [jax.ShapeDtypeStruct((g.shape[0], g.shape[1] // 2, g.shape[2]), g.dtype) for g in gs]
    return _Stage(gs, [], news, len(gs), copies)


def _scatter_chips(ps):
    def copies(src, buf, new, send_sems, recv_sems):
        x, y, c = _mesh_pos()
        started, landing = [], []
        for t in range(len(ps)):
            for j, chip in enumerate(_other_chips(x, y)):
                started.append(_remote(src[t].at[2 * chip[0] + chip[1]], new[t].at[j], send_sems, recv_sems, 3 * t + j, (*chip, c)))
                landing.append(_remote(new[t].at[j], new[t].at[j], send_sems, recv_sems, 3 * t + j, (x, y, c)))
        return started, landing

    return _Stage(ps, [], [jax.ShapeDtypeStruct((3,) + p.shape[1:], p.dtype) for p in ps], 3 * len(ps), copies)


def _join_halves(fs):
    def copies(src, buf, new, send_sems, recv_sems):
        x, y, c = _mesh_pos()
        started, landing = [], []
        for t in range(len(fs)):
            rh = fs[t].shape[0] // 2
            mine = buf[t].at[pl.ds(c * rh, rh)]
            theirs = buf[t].at[pl.ds((1 - c) * rh, rh)]
            started.append(_remote(mine, mine, send_sems, recv_sems, t, (x, y, 1 - c)))
            landing.append(_remote(theirs, theirs, send_sems, recv_sems, t, (x, y, c)))
        return started, landing

    return _Stage([], fs, [], len(fs), copies)


def _cast_place(w, k_me):
    rows, cols = w.shape
    tr = _tile(rows, 512, 16)

    def body(k_ref, w_ref, o_ref):
        o_ref[...] = w_ref[...].astype(BF16)

    return pl.pallas_call(
        body, name="cast_place", out_shape=_sds((N_CHIP, rows, cols), BF16),
        grid_spec=pltpu.PrefetchScalarGridSpec(
            num_scalar_prefetch=1, grid=(rows // tr,),
            in_specs=[pl.BlockSpec((tr, cols), lambda i, k_ref: (i, 0))],
            out_specs=pl.BlockSpec((None, tr, cols), lambda i, k_ref: (k_ref[0], i, 0))),
        compiler_params=_cparams(("arbitrary",)),
    )(_scalar(k_me), w)


_DIMS = {"nn": (((1,), (0,)), ((), ())), "nt": (((1,), (1,)), ((), ())), "tn": (((0,), (0,)), ((), ()))}


def _accum(ref, val, first):
    @pl.when(first)
    def _():
        ref[...] = val

    @pl.when(jnp.logical_not(first))
    def _():
        ref[...] += val


def _grid_edges(grid):
    ids = [pl.program_id(ax) for ax in range(len(grid))]
    first = functools.reduce(jnp.logical_and, [i == 0 for i in ids])
    last = functools.reduce(jnp.logical_and, [i == n - 1 for i, n in zip(ids, grid)])
    return first, last


def _host_call(name, grid, body, operands, in_specs, out_shape, out_specs, scratch, stages):
    s_ins, s_outs, aliases, s_scratch = _stage_operands(stages, len(operands), len(out_shape))
    n_in, n_out, n_scr = len(operands), len(out_shape), len(scratch)

    def full_body(*refs):
        in_refs = refs[:n_in]
        s_in_refs = refs[n_in:n_in + len(s_ins)]
        o0 = n_in + len(s_ins)
        out_refs = refs[o0:o0 + n_out]
        s_out_refs = refs[o0 + n_out:o0 + n_out + len(s_outs)]
        c0 = o0 + n_out + len(s_outs)
        scr_refs = refs[c0:c0 + n_scr]
        if stages:
            parts = _stage_refs(stages, s_in_refs, s_out_refs, refs[c0 + n_scr:])
            first, last = _grid_edges(grid)
            pl.when(first)(lambda: _stages_start(stages, parts))
        body(in_refs, out_refs, scr_refs)
        if stages:
            pl.when(last)(lambda: _stages_wait(stages, parts))

    res = pl.pallas_call(
        full_body, name=name, grid=grid, in_specs=list(in_specs) + [ANY] * len(s_ins),
        out_specs=list(out_specs) + [ANY] * len(s_outs), out_shape=list(out_shape) + s_outs,
        input_output_aliases=aliases, scratch_shapes=list(scratch) + s_scratch,
        compiler_params=_cparams(("arbitrary",) * len(grid)),
    )(*operands, *s_ins)
    return list(res[:n_out]), _stage_results(stages, res[n_out:])


def _matmul(name, grid, pairs, extras, outs, acc_shape, epilogue, stages=()):
    n_p = len(pairs)
    n_k = grid[-1]
    dims = [_DIMS[p[4]] for p in pairs]

    def body(in_refs, out, accs):
        ab, ex = in_refs[:2 * n_p], in_refs[2 * n_p:]
        ids = [pl.program_id(ax) for ax in range(len(grid))]
        k = ids[-1]

        @pl.when(k == 0)
        def _():
            for acc in accs:
                acc[...] = jnp.zeros_like(acc)

        for p in range(n_p):
            a = ab[2 * p][...].astype(BF16)
            b = ab[2 * p + 1][...].astype(BF16)
            accs[p][...] += lax.dot_general(a, b, dims[p], preferred_element_type=F32)

        @pl.when(k == n_k - 1)
        def _():
            epilogue([acc[...] for acc in accs], ex, out, ids)

    in_specs = []
    operands = []
    for a, a_spec, b, b_spec, _ in pairs:
        in_specs += [a_spec, b_spec]
        operands += [a, b]
    for e, e_spec in extras:
        in_specs.append(e_spec)
        operands.append(e)
    res, stage_res = _host_call(name, grid, body, operands, in_specs, [o[0] for o in outs], [o[1] for o in outs],
                                [pltpu.VMEM(acc_shape, F32) for _ in pairs], list(stages))
    return (res, stage_res) if stages else res


def _out(res, stages, single=False):
    outs = res[0] if stages else res
    outs = outs[0] if single else outs
    return (outs, res[1]) if stages else outs


def _sds(shape, dtype):
    return jax.ShapeDtypeStruct(shape, dtype)


def _row(n):
    return pl.BlockSpec((1, n), lambda *_: (0, 0))


def _layer_norm(r):
    mu = jnp.mean(r, axis=-1, keepdims=True)
    xc = r - mu
    var = jnp.mean(xc * xc, axis=-1, keepdims=True)
    rstd = lax.rsqrt(var + LN_EPS)
    return xc * rstd, rstd


def _layer_norm_bwd(dxhat, xhat, rstd):
    m1 = jnp.mean(dxhat, axis=-1, keepdims=True)
    m2 = jnp.mean(dxhat * xhat, axis=-1, keepdims=True)
    return rstd * (dxhat - m1 - xhat * m2)


def _colsum(v):
    return jnp.sum(v, axis=0, keepdims=True)


def _fwd_in(x_bf, wg_in, stages=()):
    s, d = x_bf.shape
    inc = wg_in.shape[2]
    tm, tn, tk = _tile(s, 1024), _tile(inc, 1280), _tile(d, 2048)
    nb = inc // tn

    def epi(accs, ex, out, ids):
        out[0][...] = accs[0].astype(BF16)

    return _out(_matmul(
        "fwd_in", (s // tm, N_CHIP * nb, d // tk),
        [(x_bf, pl.BlockSpec((tm, tk), lambda i, j, k: (i, k)),
          wg_in, pl.BlockSpec((None, tk, tn), lambda i, j, k: (j // nb, k, j % nb)), "nn")],
        [], [(_sds((s, N_CHIP * inc), BF16), pl.BlockSpec((tm, tn), lambda i, j, k: (i, j)))],
        (tm, tn), epi, stages), stages, True)


def _fwd_merge(y_pool, y_lru, w_pu, w_lu, z, stages=()):
    s, d = y_pool.shape
    tm, tn, tk = _tile(s, 1024), _tile(d, 1024), _tile(d, 1024)
    ga0, gb0 = 3 * d // tn, 4 * d // tn

    def epi(accs, ex, out, ids):
        sa = _sigmoid(ex[0][...].astype(F32))
        sb = _sigmoid(ex[1][...].astype(F32))
        out[0][...] = (sa * accs[0] + sb * accs[1]).astype(BF16)
        out[1][...] = accs[0].astype(BF16)
        out[2][...] = accs[1].astype(BF16)

    a_spec = pl.BlockSpec((tm, tk), lambda i, j, k: (i, k))
    b_spec = pl.BlockSpec((tk, tn), lambda i, j, k: (k, j))
    o_spec = pl.BlockSpec((tm, tn), lambda i, j, k: (i, j))
    return _out(_matmul(
        "fwd_merge", (s // tm, d // tn, d // tk),
        [(y_pool, a_spec, w_pu, b_spec, "nn"), (y_lru, a_spec, w_lu, b_spec, "nn")],
        [(z, pl.BlockSpec((tm, tn), lambda i, j, k: (i, ga0 + j))), (z, pl.BlockSpec((tm, tn), lambda i, j, k: (i, gb0 + j)))],
        [(_sds((s, d), BF16), o_spec)] * 3, (tm, tn), epi, stages), stages)


def _fwd_out_ln1(m, w_out, x, b_out, g1, b1, stages=()):
    s, d = x.shape
    tm, tk = _tile(s, 512), _tile(d, 2048)

    def epi(accs, ex, out, ids):
        r = DN_ALPHA * ex[0][...] + accs[0] + ex[1][...]
        xhat, rstd = _layer_norm(r)
        out[0][...] = xhat
        out[1][...] = (xhat * ex[2][...] + ex[3][...]).astype(BF16)
        out[2][...] = rstd

    full = pl.BlockSpec((tm, d), lambda i, j, k: (i, 0))
    return _out(_matmul(
        "fwd_out_ln1", (s // tm, 1, d // tk),
        [(m, pl.BlockSpec((tm, tk), lambda i, j, k: (i, k)), w_out, pl.BlockSpec((tk, d), lambda i, j, k: (k, 0)), "nn")],
        [(x, full), (b_out, _row(d)), (g1, _row(d)), (b1, _row(d))],
        [(_sds((s, d), F32), full), (_sds((s, d), BF16), full), (_sds((s, 1), F32), pl.BlockSpec((tm, 1), lambda i, j, k: (i, 0)))],
        (tm, d), epi, stages), stages)


def _fwd_ff1(x1_bf, wg_ff1, b_ff1, stages=()):
    s, d = x1_bf.shape
    fc = wg_ff1.shape[2]
    tm, tn, tk = _tile(s, 1024), _tile(fc, 1024), _tile(d, 2048)
    nb = fc // tn

    def epi(accs, ex, out, ids):
        p = jnp.maximum(accs[0] + ex[0][...], 0.0)
        out[0][...] = (p * p).astype(BF16)

    return _out(_matmul(
        "fwd_ff1", (s // tm, N_CHIP * nb, d // tk),
        [(x1_bf, pl.BlockSpec((tm, tk), lambda i, j, k: (i, k)),
          wg_ff1, pl.BlockSpec((None, tk, tn), lambda i, j, k: (j // nb, k, j % nb)), "nn")],
        [(b_ff1, pl.BlockSpec((1, tn), lambda i, j, k: (0, j)))],
        [(_sds((s, N_CHIP * fc), BF16), pl.BlockSpec((tm, tn), lambda i, j, k: (i, j)))],
        (tm, tn), epi, stages), stages, True)


def _fwd_ff2_ln2_loss(hdn, w_ff2, xhat1, g1, b1, b_ff2, g2, b2, target, stages=()):
    s, f = hdn.shape
    d = xhat1.shape[1]
    tm, tk = _tile(s, 512), _tile(f, 1024)

    def epi(accs, ex, out, ids):
        first = ids[0] == 0
        x1 = ex[0][...] * ex[1][...] + ex[2][...]
        r = DN_ALPHA * x1 + accs[0] + ex[3][...]
        xhat, rstd = _layer_norm(r)
        g2v = ex[4][...]
        err = xhat * g2v + ex[5][...] - ex[6][...]
        part = 0.5 * jnp.sum(jnp.mean(err * err, axis=-1, keepdims=True), axis=0, keepdims=True)
        dy = err * (1.0 / d)
        dr2 = _layer_norm_bwd(dy * g2v, xhat, rstd)
        out[0][...] = dr2
        out[1][...] = dr2.astype(BF16)
        _accum(out[2], _colsum(dy * xhat), first)
        _accum(out[3], _colsum(dy), first)
        _accum(out[4], _colsum(dr2), first)
        _accum(out[5], jnp.broadcast_to(part, (1, 128)), first)

    full = pl.BlockSpec((tm, d), lambda i, j, k: (i, 0))
    return _out(_matmul(
        "fwd_ff2_ln2_loss", (s // tm, 1, f // tk),
        [(hdn, pl.BlockSpec((tm, tk), lambda i, j, k: (i, k)), w_ff2, pl.BlockSpec((tk, d), lambda i, j, k: (k, 0)), "nn")],
        [(xhat1, full), (g1, _row(d)), (b1, _row(d)), (b_ff2, _row(d)), (g2, _row(d)), (b2, _row(d)), (target, full)],
        [(_sds((s, d), F32), full), (_sds((s, d), BF16), full), (_sds((1, d), F32), _row(d)), (_sds((1, d), F32), _row(d)),
         (_sds((1, d), F32), _row(d)), (_sds((1, 128), F32), _row(128))],
        (tm, d), epi, stages), stages)


def _bwd_ff2_in(dr2_bf, w_ff2, hdn, stages=()):
    s, d = dr2_bf.shape
    f = hdn.shape[1]
    tm, tn, tk = _tile(s, 1024), _tile(f, 1024), _tile(d, 2048)

    def epi(accs, ex, out, ids):
        dpre = accs[0] * (2.0 * jnp.sqrt(ex[0][...].astype(F32)))
        out[0][...] = dpre.astype(BF16)
        _accum(out[1], _colsum(dpre), ids[1] == 0)

    return _out(_matmul(
        "bwd_ff2_in", (f // tn, s // tm, d // tk),
        [(dr2_bf, pl.BlockSpec((tm, tk), lambda j, i, k: (i, k)), w_ff2, pl.BlockSpec((tn, tk), lambda j, i, k: (j, k)), "nt")],
        [(hdn, pl.BlockSpec((tm, tn), lambda j, i, k: (i, j)))],
        [(_sds((s, f), BF16), pl.BlockSpec((tm, tn), lambda j, i, k: (i, j))), (_sds((1, f), F32), pl.BlockSpec((1, tn), lambda j, i, k: (0, j)))],
        (tm, tn), epi, stages), stages)


def _bwd_ff1_in_ln1(dpre, wg_ff1, dr2, xhat1, rstd1, g1, stages=()):
    s, f = dpre.shape
    d = xhat1.shape[1]
    fc = wg_ff1.shape[2]
    tm, tk = _tile(s, 512), _tile(fc, 1024)
    nb = fc // tk

    def epi(accs, ex, out, ids):
        first = ids[0] == 0
        xhat = ex[1][...]
        dx1 = accs[0] + DN_ALPHA * ex[0][...]
        dr1 = _layer_norm_bwd(dx1 * ex[3][...], xhat, ex[2][...])
        out[0][...] = dr1
        out[1][...] = dr1.astype(BF16)
        _accum(out[2], _colsum(dx1 * xhat), first)
        _accum(out[3], _colsum(dx1), first)
        _accum(out[4], _colsum(dr1), first)

    full = pl.BlockSpec((tm, d), lambda i, j, k: (i, 0))
    return _out(_matmul(
        "bwd_ff1_in_ln1", (s // tm, 1, f // tk),
        [(dpre, pl.BlockSpec((tm, tk), lambda i, j, k: (i, k)),
          wg_ff1, pl.BlockSpec((None, d, tk), lambda i, j, k: (k // nb, 0, k % nb)), "nt")],
        [(dr2, full), (xhat1, full), (rstd1, pl.BlockSpec((tm, 1), lambda i, j, k: (i, 0))), (g1, _row(d))],
        [(_sds((s, d), F32), full), (_sds((s, d), BF16), full), (_sds((1, d), F32), _row(d)), (_sds((1, d), F32), _row(d)),
         (_sds((1, d), F32), _row(d))],
        (tm, d), epi, stages), stages)


def _bwd_out_in(dr1_bf, w_out, z, pa, pb, stages=()):
    s, d = dr1_bf.shape
    tm, tn, tk = _tile(s, 1024), _tile(d, 1024), _tile(d, 2048)
    ga0, gb0 = 3 * d // tn, 4 * d // tn

    def epi(accs, ex, out, ids):
        dm = accs[0]
        sa = _sigmoid(ex[0][...].astype(F32))
        sb = _sigmoid(ex[1][...].astype(F32))
        out[0][...] = (dm * sa).astype(BF16)
        out[1][...] = (dm * sb).astype(BF16)
        out[2][...] = (dm * ex[2][...].astype(F32) * sa * (1.0 - sa)).astype(BF16)
        out[3][...] = (dm * ex[3][...].astype(F32) * sb * (1.0 - sb)).astype(BF16)

    o_spec = pl.BlockSpec((tm, tn), lambda i, j, k: (i, j))
    return _out(_matmul(
        "bwd_out_in", (s // tm, d // tn, d // tk),
        [(dr1_bf, pl.BlockSpec((tm, tk), lambda i, j, k: (i, k)), w_out, pl.BlockSpec((tn, tk), lambda i, j, k: (j, k)), "nt")],
        [(z, pl.BlockSpec((tm, tn), lambda i, j, k: (i, ga0 + j))), (z, pl.BlockSpec((tm, tn), lambda i, j, k: (i, gb0 + j))),
         (pa, o_spec), (pb, o_spec)],
        [(_sds((s, d), BF16), o_spec)] * 4, (tm, tn), epi, stages), stages)


def _bwd_up_in(name, dp, w_up, stages=()):
    s, d = dp.shape
    n = w_up.shape[0]
    tm, tn, tk = _tile(s, 1024), _tile(n, 1024), _tile(d, 2048)

    def epi(accs, ex, out, ids):
        out[0][...] = accs[0].astype(BF16)

    return _out(_matmul(
        name, (s // tm, n // tn, d // tk),
        [(dp, pl.BlockSpec((tm, tk), lambda i, j, k: (i, k)), w_up, pl.BlockSpec((tn, tk), lambda i, j, k: (j, k)), "nt")],
        [], [(_sds((s, n), BF16), pl.BlockSpec((tm, tn), lambda i, j, k: (i, j)))], (tm, tn), epi, stages), stages, True)


def _bwd_in(dz, wg_in, dr1, stages=()):
    s, d = dr1.shape
    inc = wg_in.shape[2]
    tm, tn, tk = _tile(s, 1024), _tile(d, 1024), _tile(inc, 1280)
    nb = inc // tk

    def epi(accs, ex, out, ids):
        out[0][...] = accs[0] + DN_ALPHA * ex[0][...]

    o_spec = pl.BlockSpec((tm, tn), lambda i, j, k: (i, j))
    return _out(_matmul(
        "bwd_in", (s // tm, d // tn, N_CHIP * nb),
        [(dz, pl.BlockSpec((tm, tk), lambda i, j, k: (i, k)),
          wg_in, pl.BlockSpec((None, tn, tk), lambda i, j, k: (k // nb, j, k % nb)), "nt")],
        [(dr1, o_spec)], [(_sds((s, d), F32), o_spec)], (tm, tn), epi, stages), stages, True)


def _wgrad(name, a, b, col_sharded, stages=()):
    s, ka = a.shape
    n = b.shape[1]
    tm, tk = _tile(ka, 1024), _tile(s, 1024)
    tn = _tile(n // N_CHIP, 1280) if col_sharded else _tile(n, 1024)

    def epi(accs, ex, out, ids):
        out[0][...] = accs[0].astype(BF16)

    if col_sharded:
        nb = (n // N_CHIP) // tn
        o = (_sds((N_CHIP, ka, n // N_CHIP), BF16), pl.BlockSpec((None, tm, tn), lambda i, j, k: (j // nb, i, j % nb)))
    else:
        o = (_sds((ka, n), BF16), pl.BlockSpec((tm, tn), lambda i, j, k: (i, j)))
    res = _out(_matmul(
        name, (ka // tm, n // tn, s // tk),
        [(a, pl.BlockSpec((tk, tm), lambda i, j, k: (k, i)), b, pl.BlockSpec((tk, tn), lambda i, j, k: (k, j)), "tn")],
        [], [o], (tm, tn), epi, stages), stages, True)
    res, stage_res = res if stages else (res, None)
    res = res if col_sharded else res.reshape(N_CHIP, ka // N_CHIP, n)
    return (res, stage_res) if stages else res


def _chunk(s):
    return _tile(s, 512, SUBLANES)


def _zero_pads(ref, s):
    zeros = jnp.zeros((PAD, ref.shape[1]), F32)
    ref[pl.ds(0, PAD), :] = zeros
    ref[pl.ds(PAD + s, PAD), :] = zeros


def _window(ref, t0, t):
    return ref[pl.ds(t0, t + 2 * PAD), :]


def _shift(sup, off, t):
    return sup[PAD + off:PAD + off + t, :]


def _pool_count(t0, t, s, w):
    pos = t0 + lax.broadcasted_iota(jnp.int32, (t, 1), 0)
    return (jnp.minimum(pos + w // 2, s) - jnp.maximum(pos - w // 2, 0)).astype(F32)


def _pool_fwd(z, pool_w, pool_scale, stages=()):
    s = z.shape[0]
    n_g, pg = pool_w.shape[0], pool_w.shape[1]
    assert n_g == len(POOL_WINDOWS) and max(POOL_WINDOWS) // 2 <= PAD
    t = _chunk(s)

    def body(u_ref, w_ref, sc_ref, d_ref, y_ref, pad_ref):
        g = pl.program_id(0)
        _zero_pads(pad_ref, s)
        pad_ref[pl.ds(PAD, s), :] = u_ref[...].astype(F32)
        for gi, w in enumerate(POOL_WINDOWS):
            @pl.when(g == gi)
            def _():
                def step(ch, carry):
                    t0 = pl.multiple_of(ch * t, t)
                    sup = _window(pad_ref, t0, t)
                    acc = _shift(sup, -(w // 2), t)
                    for o in range(-(w // 2) + 1, w // 2):
                        acc = acc + _shift(sup, o, t)
                    dd = (acc * (1.0 / _pool_count(t0, t, s, w)) - _shift(sup, 0, t)).astype(BF16)
                    d_ref[pl.ds(t0, t), :] = dd
                    y = jnp.dot(dd, w_ref[...], preferred_element_type=F32) * sc_ref[...]
                    y_ref[pl.ds(t0, t), :] = y.astype(BF16)
                    return carry

                lax.fori_loop(0, s // t, step, 0)

    blk = pl.BlockSpec((s, pg), lambda g: (0, g))
    res = _host_call(
        "pool_fwd", (n_g,), lambda ins, outs, scr: body(*ins, *outs, *scr), [z, pool_w, pool_scale],
        [blk, pl.BlockSpec((None, pg, pg), lambda g: (g, 0, 0)), pl.BlockSpec((1, pg), lambda g: (0, g))],
        [_sds((s, n_g * pg), BF16)] * 2, [blk, blk], [pltpu.VMEM((s + 2 * PAD, pg), F32)], list(stages))
    return res if stages else res[0]


def _pool_bwd(dsv, dy, pool_w, pool_scale, stages=()):
    s = dsv.shape[0]
    n_g, pg = pool_w.shape[0], pool_w.shape[1]
    t = _chunk(s)

    def body(d_ref, dy_ref, w_ref, sc_ref, du_ref, dw_ref, dsc_ref, epad_ref, dwacc_ref):
        g = pl.program_id(0)
        _zero_pads(epad_ref, s)
        dwacc_ref[...] = jnp.zeros_like(dwacc_ref)
        for gi, w in enumerate(POOL_WINDOWS):
            @pl.when(g == gi)
            def _():
                def first(ch, dsc):
                    t0 = pl.multiple_of(ch * t, t)
                    dd = d_ref[pl.ds(t0, t), :]
                    dyc = dy_ref[pl.ds(t0, t), :].astype(F32)
                    wv = w_ref[...]
                    ypre = jnp.dot(dd, wv, preferred_element_type=F32)
                    dq = (dyc * sc_ref[...]).astype(BF16)
                    dwacc_ref[...] += lax.dot_general(dd, dq, _DIMS["tn"], preferred_element_type=F32)
                    ddv = lax.dot_general(dq, wv, _DIMS["nt"], preferred_element_type=F32)
                    epad_ref[pl.ds(pl.multiple_of(PAD + t0, SUBLANES), t), :] = ddv * (1.0 / _pool_count(t0, t, s, w))
                    return dsc + _colsum(dyc * ypre)

                dsc_ref[...] = lax.fori_loop(0, s // t, first, jnp.zeros((1, pg), F32))

                def second(ch, carry):
                    t0 = pl.multiple_of(ch * t, t)
                    sup = _window(epad_ref, t0, t)
                    acc = _shift(sup, -(w // 2) + 1, t)
                    for o in range(-(w // 2) + 2, w // 2 + 1):
                        acc = acc + _shift(sup, o, t)
                    du_ref[pl.ds(t0, t), :] = (acc - _shift(sup, 0, t) * _pool_count(t0, t, s, w)).astype(BF16)
                    return carry

                lax.fori_loop(0, s // t, second, 0)

        dw_ref[...] = dwacc_ref[...].astype(BF16)

    blk = pl.BlockSpec((s, pg), lambda g: (0, g))
    w_spec = pl.BlockSpec((None, pg, pg), lambda g: (g, 0, 0))
    sc_spec = pl.BlockSpec((1, pg), lambda g: (0, g))
    res = _host_call(
        "pool_bwd", (n_g,), lambda ins, outs, scr: body(*ins, *outs, *scr), [dsv, dy, pool_w, pool_scale],
        [blk, blk, w_spec, sc_spec], [_sds((s, n_g * pg), BF16), _sds((n_g, pg, pg), BF16), _sds((1, n_g * pg), F32)],
        [blk, w_spec, sc_spec], [pltpu.VMEM((s + 2 * PAD, pg), F32), pltpu.VMEM((pg, pg), F32)], list(stages))
    return res if stages else res[0]


def _sigmoid(x):
    return 0.5 * jnp.tanh(0.5 * x) + 0.5


def _softplus(x):
    e = jnp.exp(-jnp.abs(x))
    log1p_e = jnp.where(e < 1e-2, e * (1.0 - e * (0.5 - e * (1.0 / 3.0))), jnp.log(1.0 + e))
    return jnp.maximum(x, 0.0) + log1p_e


_GELU_C = math.sqrt(2.0 / math.pi)


def _gelu(x):
    th = jnp.tanh(_GELU_C * (x + 0.044715 * x * x * x))
    return 0.5 * x * (1.0 + th), th


def _gelu_grad(x, th):
    return 0.5 * (1.0 + th) + 0.5 * x * (1.0 - th * th) * _GELU_C * (1.0 + 3.0 * 0.044715 * x * x)


def _scan_chunk(a_ref, b_ref, o_ref, o_off, carry, t, reverse):
    n = a_ref.shape[1]
    row = lax.broadcasted_iota(jnp.int32, (SUBLANES, n), 0)
    n_groups = t // SUBLANES
    unroll = math.gcd(n_groups, SCAN_UNROLL)
    last = 0 if reverse else SUBLANES - 1

    def step(si, carry):
        for u in range(unroll):
            gi = si * unroll + u
            g = n_groups - 1 - gi if reverse else gi
            r0 = pl.multiple_of(g * SUBLANES, SUBLANES)
            a = a_ref[pl.ds(r0, SUBLANES), :]
            b = b_ref[pl.ds(r0, SUBLANES), :]
            for k in (1, 2, 4):
                keep = row < SUBLANES - k if reverse else row >= k
                sh = SUBLANES - k if reverse else k
                ar = jnp.where(keep, pltpu.roll(a, sh, 0), 1.0)
                br = jnp.where(keep, pltpu.roll(b, sh, 0), 0.0)
                b = a * br + b
                a = a * ar
            o_ref[pl.ds(pl.multiple_of(o_off + r0, SUBLANES), SUBLANES), :] = a * carry + b
            carry = (jnp.broadcast_to(a[last:last + 1, :], a.shape) * carry
                     + jnp.broadcast_to(b[last:last + 1, :], b.shape))
        return carry

    return lax.fori_loop(0, n_groups // unroll, step, carry)


def _lru_params(pk_ref):
    rows = pk_ref[...]
    get = lambda i: rows[i:i + 1, :]
    cw = [get(k) for k in range(4)]
    lam = (get(9), get(10))
    big_l = tuple(-LRU_C * _softplus(-v) for v in lam)
    return cw, get(4), (get(5), get(6)), (get(7), get(8)), lam, big_l


def _conv(sup, cw, cb, t):
    xc = cb + cw[0] * _shift(sup, -2, t)
    for k in range(1, 4):
        xc = xc + cw[k] * _shift(sup, k - 2, t)
    return xc


def _gates(xcb, w_ref, d, bk, ba, bx, big_l):
    pre = jnp.dot(xcb, w_ref[:, pl.ds(d * 2 * bk, 2 * bk)], preferred_element_type=F32)
    r = _sigmoid(pre[:, :bk] + ba[d])
    i = _sigmoid(pre[:, bk:] + bx[d])
    la = big_l[d] * r
    a = jnp.exp(la)
    var = jnp.tanh(-la) * (1.0 + a * a)
    rs = lax.rsqrt(jnp.maximum(var, 1e-30))
    return r, i, a, var * rs, rs


def _lru_specs(s, d, bk):
    u_spec = pl.BlockSpec((s, bk), lambda h: (0, d // bk + h))
    ug_spec = pl.BlockSpec((s, bk), lambda h: (0, 2 * d // bk + h))
    w_spec = pl.BlockSpec((None, bk, 4 * bk), lambda h: (h, 0, 0))
    pk_spec = pl.BlockSpec((None, 16, bk), lambda h: (h, 0, 0))
    blk = pl.BlockSpec((s, bk), lambda h: (0, h))
    return u_spec, ug_spec, w_spec, pk_spec, blk


def _lru_fwd(z, gatew, pk, stages=()):
    s = z.shape[0]
    n_h, bk = gatew.shape[0], gatew.shape[1]
    d = n_h * bk
    t = _chunk(s)
    n_ch = s // t

    def body(u_ref, ug_ref, w_ref, pk_ref, y_ref, upad, h0buf, abuf, bbuf, xcbuf, h1buf):
        _zero_pads(upad, s)
        upad[pl.ds(PAD, s), :] = u_ref[...].astype(F32)
        cw, cb, ba, bx, _, big_l = _lru_params(pk_ref)
        zero = jnp.zeros((SUBLANES, bk), F32)

        def fill(xc, dr):
            _, i, a, sq, _ = _gates(xc.astype(BF16), w_ref, dr, bk, ba, bx, big_l)
            abuf[...] = a
            bbuf[...] = sq * i * xc

        def up(ch, carry):
            t0 = pl.multiple_of(ch * t, t)
            xc = _conv(_window(upad, t0, t), cw, cb, t)
            xcbuf[pl.ds(t0, t), :] = xc
            fill(xc, 0)
            return _scan_chunk(abuf, bbuf, h0buf, t0, carry, t, False)

        lax.fori_loop(0, n_ch, up, zero)

        def down(ci, carry):
            t0 = pl.multiple_of((n_ch - 1 - ci) * t, t)
            fill(xcbuf[pl.ds(t0, t), :], 1)
            carry = _scan_chunk(abuf, bbuf, h1buf, 0, carry, t, True)
            gl, _ = _gelu(ug_ref[pl.ds(t0, t), :].astype(F32))
            y_ref[pl.ds(t0, t), :] = ((h0buf[pl.ds(t0, t), :] + h1buf[...]) * gl).astype(BF16)
            return carry

        lax.fori_loop(0, n_ch, down, zero)

    u_spec, ug_spec, w_spec, pk_spec, blk = _lru_specs(s, d, bk)
    res = _host_call(
        "lru_fwd", (n_h,), lambda ins, outs, scr: body(*ins, *outs, *scr), [z, z, gatew, pk],
        [u_spec, ug_spec, w_spec, pk_spec], [_sds((s, d), BF16)], [blk],
        [pltpu.VMEM((s + 2 * PAD, bk), F32), pltpu.VMEM((s, bk), F32), pltpu.VMEM((t, bk), F32), pltpu.VMEM((t, bk), F32),
         pltpu.VMEM((s, bk), F32), pltpu.VMEM((t, bk), F32)], list(stages))
    return (res[0][0], res[1]) if stages else res[0][0]


def _lru_grads(lam_, hnb, a, sq, rs, r, i, xc, xcb, w_ref, dwacc, d, big_l, acc):
    bk = xc.shape[1]
    dba, dbx, dl = acc
    q = lam_ * i * xc
    dla = lam_ * hnb * a - q * (a * a) * rs
    dpr = dla * big_l * r * (1.0 - r)
    dpi = q * sq * (1.0 - i)
    dprb, dpib = dpr.astype(BF16), dpi.astype(BF16)
    c0 = d * 2 * bk
    dxc = (lam_ * sq * i
           + lax.dot_general(dprb, w_ref[:, pl.ds(c0, bk)], _DIMS["nt"], preferred_element_type=F32)
           + lax.dot_general(dpib, w_ref[:, pl.ds(c0 + bk, bk)], _DIMS["nt"], preferred_element_type=F32))
    dwacc[:, pl.ds(c0, bk)] += lax.dot_general(xcb, dprb, _DIMS["tn"], preferred_element_type=F32)
    dwacc[:, pl.ds(c0 + bk, bk)] += lax.dot_general(xcb, dpib, _DIMS["tn"], preferred_element_type=F32)
    return dxc, (dba + _colsum(dpr), dbx + _colsum(dpi), dl + _colsum(dla * r))


def _lru_bwd(z, dy, gatew, pk, stages=()):
    s = z.shape[0]
    n_h, bk = gatew.shape[0], gatew.shape[1]
    d = n_h * bk
    t = _chunk(s)
    n_ch = s // t

    def body(u_ref, ug_ref, dy_ref, w_ref, pk_ref, du_ref, dug_ref, dw_ref, dpk_ref,
             upad, h0pad, h1pad, dxpad, abuf, bbuf, lbuf, dwacc, edge, xcbuf):
        for ref in (upad, h0pad, h1pad, dxpad):
            _zero_pads(ref, s)
        upad[pl.ds(PAD, s), :] = u_ref[...].astype(F32)
        dwacc[...] = jnp.zeros_like(dwacc)
        cw, cb, ba, bx, lam, big_l = _lru_params(pk_ref)
        zero = jnp.zeros((SUBLANES, bk), F32)
        zrow = jnp.zeros((1, bk), F32)
        rowi = lax.broadcasted_iota(jnp.int32, (t, bk), 0)

        def at(t0):
            return pl.ds(pl.multiple_of(PAD + t0, SUBLANES), t)

        def conv_in(t0):
            xc = xcbuf[pl.ds(t0, t), :]
            return xc, xc.astype(BF16)

        def dh_of(t0):
            ug = ug_ref[pl.ds(t0, t), :].astype(F32)
            gl, th = _gelu(ug)
            dyv = dy_ref[pl.ds(t0, t), :].astype(F32)
            return dyv * gl, dyv * _gelu_grad(ug, th)

        def sweep1(ch, carry):
            t0 = pl.multiple_of(ch * t, t)
            xc = _conv(_window(upad, t0, t), cw, cb, t)
            xcbuf[pl.ds(t0, t), :] = xc
            _, i, a, sq, _ = _gates(xc.astype(BF16), w_ref, 0, bk, ba, bx, big_l)
            abuf[...] = a
            bbuf[...] = sq * i * xc
            return _scan_chunk(abuf, bbuf, h0pad, PAD + t0, carry, t, False)

        lax.fori_loop(0, n_ch, sweep1, zero)

        edge[...] = zero

        def sweep2(ci, st):
            carry_h, carry_l, acc = st
            t0 = pl.multiple_of((n_ch - 1 - ci) * t, t)
            xc, xcb = conv_in(t0)
            _, i1, a1, sq1, _ = _gates(xcb, w_ref, 1, bk, ba, bx, big_l)
            abuf[...] = a1
            bbuf[...] = sq1 * i1 * xc
            carry_h = _scan_chunk(abuf, bbuf, h1pad, PAD + t0, carry_h, t, True)
            dh, dgl = dh_of(t0)
            dug_ref[pl.ds(t0, t), :] = (dgl * (h0pad[at(t0), :] + h1pad[at(t0), :])).astype(BF16)
            r0, i0, a0, sq0, rs0 = _gates(xcb, w_ref, 0, bk, ba, bx, big_l)
            abuf[...] = jnp.where(rowi == t - 1, edge[0:1, :], pltpu.roll(a0, t - 1, 0))
            bbuf[...] = dh
            carry_l = _scan_chunk(abuf, bbuf, lbuf, 0, carry_l, t, True)
            edge[...] = jnp.broadcast_to(a0[0:1, :], (SUBLANES, bk))
            hprev = _shift(_window(h0pad, t0, t), -1, t)
            dxc, acc = _lru_grads(lbuf[...], hprev, a0, sq0, rs0, r0, i0, xc, xcb, w_ref, dwacc, 0, big_l[0], acc)
            dxpad[at(t0), :] = dxc
            return carry_h, carry_l, acc

        _, _, acc0 = lax.fori_loop(0, n_ch, sweep2, (zero, zero, (zrow, zrow, zrow)))

        edge[...] = zero

        def sweep3(ch, st):
            carry_l, acc = st
            t0 = pl.multiple_of(ch * t, t)
            xc, xcb = conv_in(t0)
            r1, i1, a1, sq1, rs1 = _gates(xcb, w_ref, 1, bk, ba, bx, big_l)
            dh, _ = dh_of(t0)
            abuf[...] = jnp.where(rowi == 0, edge[0:1, :], pltpu.roll(a1, 1, 0))
            bbuf[...] = dh
            carry_l = _scan_chunk(abuf, bbuf, lbuf, 0, carry_l, t, False)
            edge[...] = jnp.broadcast_to(a1[t - 1:t, :], (SUBLANES, bk))
            hnext = _shift(_window(h1pad, t0, t), 1, t)
            dxc, acc = _lru_grads(lbuf[...], hnext, a1, sq1, rs1, r1, i1, xc, xcb, w_ref, dwacc, 1, big_l[1], acc)
            dxpad[at(t0), :] += dxc
            return carry_l, acc

        _, acc1 = lax.fori_loop(0, n_ch, sweep3, (zero, (zrow, zrow, zrow)))

        def sweep4(ch, st):
            t0 = pl.multiple_of(ch * t, t)
            sdx = _window(dxpad, t0, t)
            su = _window(upad, t0, t)
            dxc = _shift(sdx, 0, t)
            du = cw[0] * _shift(sdx, 2, t) + cw[1] * _shift(sdx, 1, t) + cw[2] * dxc + cw[3] * _shift(sdx, -1, t)
            du_ref[pl.ds(t0, t), :] = du.astype(BF16)
            return tuple(st[k] + _colsum(dxc * _shift(su, k - 2, t)) for k in range(4)) + (st[4] + _colsum(dxc),)

        conv_g = lax.fori_loop(0, n_ch, sweep4, (zrow,) * 5)

        dpk_ref[...] = jnp.zeros_like(dpk_ref)
        rows = list(conv_g) + [acc0[0], acc1[0], acc0[1], acc1[1],
                               acc0[2] * LRU_C * _sigmoid(-lam[0]), acc1[2] * LRU_C * _sigmoid(-lam[1])]
        for k, v in enumerate(rows):
            dpk_ref[pl.ds(k, 1), :] = v
        dw_ref[...] = dwacc[...].astype(BF16)

    u_spec, ug_spec, w_spec, pk_spec, blk = _lru_specs(s, d, bk)
    padded = pltpu.VMEM((s + 2 * PAD, bk), F32)
    chunk = pltpu.VMEM((t, bk), F32)
    res = _host_call(
        "lru_bwd", (n_h,), lambda ins, outs, scr: body(*ins, *outs, *scr), [z, z, dy, gatew, pk],
        [u_spec, ug_spec, blk, w_spec, pk_spec],
        [_sds((s, d), BF16), _sds((s, d), BF16), _sds((n_h, bk, 4 * bk), BF16), _sds((n_h, 16, bk), F32)],
        [blk, blk, w_spec, pk_spec],
        [padded, padded, padded, padded, chunk, chunk, chunk, pltpu.VMEM((bk, 4 * bk), F32),
         pltpu.VMEM((SUBLANES, bk), F32), pltpu.VMEM((s, bk), F32)], list(stages))
    return res if stages else res[0]


def _scalar(v):
    return jnp.reshape(v, (1,)).astype(jnp.int32)


def _add_sibling(g, r, c):
    _, rows, cols = g.shape
    rh = rows // 2
    tr = _tile(rh, 512, 16)
    nr = rh // tr

    def body(c_ref, g_ref, r_ref, o_ref):
        o_ref[...] = (g_ref[...].astype(F32) + r_ref[...].astype(F32)).astype(BF16)

    spec = pl.BlockSpec((None, tr, cols), lambda k, i, c_ref: (k, i, 0))
    return pl.pallas_call(
        body, name="add_sibling", out_shape=_sds((N_CHIP, rh, cols), BF16),
        grid_spec=pltpu.PrefetchScalarGridSpec(
            num_scalar_prefetch=1, grid=(N_CHIP, nr),
            in_specs=[pl.BlockSpec((None, tr, cols), lambda k, i, c_ref: (k, c_ref[0] * nr + i, 0)), spec], out_specs=spec),
        compiler_params=_cparams(("arbitrary", "arbitrary")),
    )(_scalar(c), g, r)


def _sum_chips(p, rcv, k_me, c):
    _, rh, cols = p.shape
    tr = _tile(rh, 512, 16)
    nr = rh // tr

    def body(kc_ref, p_ref, r_ref, o_ref):
        acc = p_ref[...].astype(F32)
        for j in range(3):
            acc = acc + r_ref[j].astype(F32)
        o_ref[...] = acc

    return pl.pallas_call(
        body, name="sum_chips", out_shape=_sds((2 * rh, cols), F32),
        grid_spec=pltpu.PrefetchScalarGridSpec(
            num_scalar_prefetch=1, grid=(nr,),
            in_specs=[pl.BlockSpec((None, tr, cols), lambda i, kc_ref: (kc_ref[0], i, 0)),
                      pl.BlockSpec((3, tr, cols), lambda i, kc_ref: (0, i, 0))],
            out_specs=pl.BlockSpec((tr, cols), lambda i, kc_ref: (kc_ref[1] * nr + i, 0))),
        compiler_params=_cparams(("arbitrary",)),
    )(jnp.stack([k_me, c]).astype(jnp.int32), p, rcv)


def _sum_devices(g):
    def body(g_ref, o_ref):
        acc = g_ref[0]
        for dev in range(1, N_DEV):
            acc = acc + g_ref[dev]
        o_ref[...] = acc

    return pl.pallas_call(body, name="sum_devices", out_shape=_sds(g.shape[1:], F32))(g)


def _adamw(w, g, m, v, stages=()):
    rows, cols = w.shape
    tr = _tile(rows, 256, SUBLANES)

    def body(ins, outs, scr):
        w_ref, g_ref, m_ref, v_ref = ins
        d_ref, nm_ref, nv_ref = outs
        gv = g_ref[...]
        nm = ADAM_B1 * m_ref[...] + (1.0 - ADAM_B1) * gv
        nv = ADAM_B2 * v_ref[...] + (1.0 - ADAM_B2) * (gv * gv)
        m_hat = nm / (1.0 - ADAM_B1 ** ADAM_STEP)
        v_hat = nv / (1.0 - ADAM_B2 ** ADAM_STEP)
        d_ref[...] = -ADAM_LR * (m_hat / (jnp.sqrt(v_hat) + ADAM_EPS) + ADAM_WD * w_ref[...])
        nm_ref[...] = nm
        nv_ref[...] = nv

    spec = pl.BlockSpec((tr, cols), lambda i: (i, 0))
    res = _host_call("adamw", (rows // tr,), body, [w, g, m, v], [spec] * 4, [_sds((rows, cols), F32)] * 3, [spec] * 3, [],
                     list(stages))
    return res if stages else res[0]


def _pack(vs, unit):
    flat = jnp.concatenate([v.reshape(-1).astype(F32) for v in vs])
    pad = (-flat.shape[0]) % unit
    if pad:
        flat = jnp.concatenate([flat, jnp.zeros((pad,), F32)])
    return flat.reshape(-1, 128)


def _unpack(p, like):
    flat = p.reshape(-1)
    out, off = [], 0
    for v in like:
        n = math.prod(v.shape)
        out.append(flat[off:off + n].reshape(v.shape))
        off += n
    return out


def kernel(x, w_in, pool_w, pool_scale, conv_w, conv_b, lru_wa, lru_ba, lru_wx, lru_bx, lru_lambda, w_pool_up, w_lru_up, w_out, b_out, ln1_g, ln1_b, w_ff1, b_ff1, w_ff2, b_ff2, ln2_g, ln2_b, loss_target, m_w_in, m_pool_w, m_pool_scale, m_conv_w, m_conv_b, m_lru_wa, m_lru_ba, m_lru_wx, m_lru_bx, m_lru_lambda, m_w_pool_up, m_w_lru_up, m_w_out, m_b_out, m_ln1_g, m_ln1_b, m_w_ff1, m_b_ff1, m_w_ff2, m_b_ff2, m_ln2_g, m_ln2_b, v_w_in, v_pool_w, v_pool_scale, v_conv_w, v_conv_b, v_lru_wa, v_lru_ba, v_lru_wx, v_lru_bx, v_lru_lambda, v_w_pool_up, v_w_lru_up, v_w_out, v_b_out, v_ln1_g, v_ln1_b, v_w_ff1, v_b_ff1, v_w_ff2, v_b_ff2, v_ln2_g, v_ln2_b):
    given = dict(locals())
    wt = {n: given[n] for n in WEIGHTS}
    mom = {n: given["m_" + n] for n in WEIGHTS}
    vel = {n: given["v_" + n] for n in WEIGHTS}

    ix, iy, ic = _mesh_pos()
    k_me = 2 * ix + iy
    s, d = x.shape[1], x.shape[2]
    ds = d // N_CHIP
    n_g, pgs, pg = pool_w.shape[1], pool_w.shape[2], pool_w.shape[3]
    n_h, bks, bk = lru_wa.shape[2], lru_wa.shape[3], lru_wa.shape[4]
    f = b_ff1.shape[1]
    x2 = x[0]
    x_bf = x2.astype(BF16)
    vec = lambda a: a.reshape(1, -1)

    sharded_vecs = [conv_w[0], lru_ba[0], lru_bx[0], lru_lambda[0]]
    rows_sv = jnp.concatenate(sharded_vecs + [jnp.zeros((6, ds), F32)], axis=0)
    sv = _all_gather_small(rows_sv)
    sv = sv.reshape(N_CHIP, 2, 16, ds)[:, 0].transpose(1, 0, 2).reshape(16, d)
    conv_w_f, ba_f, bx_f, lam_f = sv[0:4], sv[4:6], sv[6:8], sv[8:10]
    pk = jnp.concatenate([conv_w_f, conv_b, ba_f, bx_f, lam_f, jnp.zeros((5, d), F32)], axis=0)
    pk = pk.reshape(16, n_h, bk).transpose(1, 0, 2)

    def gate_stack(wa, wx):
        return jnp.stack([wa[0], wx[0]], axis=1)

    mats = {
        "w_in": w_in[0], "w_pool_up": w_pool_up[0], "w_lru_up": w_lru_up[0], "w_out": w_out[0],
        "w_ff1": w_ff1[0], "w_ff2": w_ff2[0],
        "pool_w": pool_w[0].reshape(n_g * pgs, pg),
        "gate_w": gate_stack(lru_wa, lru_wx).reshape(4 * n_h * bks, bk),
    }
    names = list(mats)
    placed = {n: _cast_place(mats[n], k_me) for n in names}

    def add_sibling(gs, swapped):
        return [_add_sibling(g, r, ic) for g, r in zip(gs, swapped)]

    def sum_chips(ps, received):
        return [_sum_chips(p, r, k_me, ic) for p, r in zip(ps, received)]

    first = ["w_in", "pool_w", "gate_w"]
    (bufs,) = _run_stages("gather_first_ici", [_gather_ici([placed[n] for n in first])])
    (bufs,) = _run_stages("gather_first_d2d", [_gather_d2d(bufs)])
    wg_in = bufs[0]
    wf_pool = bufs[1].reshape(N_CHIP, n_g, pgs, pg).transpose(1, 0, 2, 3).reshape(n_g, pg, pg)
    wf_gate = bufs[2].reshape(N_CHIP, 2, 2, n_h, bks, bk).transpose(3, 0, 4, 1, 2, 5).reshape(n_h, bk, 4 * bk)

    z, (bufs,) = _fwd_in(x_bf, wg_in, stages=[_gather_ici([placed[n] for n in ("w_pool_up", "w_lru_up", "w_out")])])
    (d_pool, y_pool), (bufs,) = _pool_fwd(z, wf_pool, pool_scale, stages=[_gather_d2d(bufs)])
    wf_pu, wf_lu, wf_out = (b.reshape(d, d) for b in bufs)
    y_lru, (bufs,) = _lru_fwd(z, wf_gate, pk, stages=[_gather_ici([placed["w_ff1"]])])
    (m_mix, p_a, p_b), (bufs,) = _fwd_merge(y_pool, y_lru, wf_pu, wf_lu, z, stages=[_gather_d2d(bufs)])
    wg_ff1 = bufs[0]
    xhat1, x1_bf, rstd1 = _fwd_out_ln1(m_mix, wf_out, x2, b_out, ln1_g, ln1_b)
    hdn, (bufs,) = _fwd_ff1(x1_bf, wg_ff1, b_ff1, stages=[_gather_ici([placed["w_ff2"]])])
    (bufs,) = _run_stages("gather_ff2_d2d", [_gather_d2d(bufs)])
    wf_ff2 = bufs[0].reshape(f, d)
    dr2, dr2_bf, g_ln2_g, g_ln2_b, g_b_ff2, loss_part = _fwd_ff2_ln2_loss(
        hdn, wf_ff2, xhat1, ln1_g, ln1_b, b_ff2, ln2_g, ln2_b, loss_target[0])

    dpre, g_b_ff1 = _bwd_ff2_in(dr2_bf, wf_ff2, hdn)
    g_ff = [_wgrad("wgrad_ff1", x1_bf, dpre, True), _wgrad("wgrad_ff2", hdn, dr2_bf, False)]
    (dr1, dr1_bf, g_ln1_g, g_ln1_b, g_b_out), (swapped,) = _bwd_ff1_in_ln1(
        dpre, wg_ff1, dr2, xhat1, rstd1, ln1_g, stages=[_swap_halves(g_ff)])
    sums_ff = add_sibling(g_ff, swapped)
    dp_a, dp_b, dg_a, dg_b = _bwd_out_in(dr1_bf, wf_out, z, p_a, p_b)
    dy_pool = _bwd_up_in("bwd_pool_up_in", dp_a, wf_pu)
    dy_lru = _bwd_up_in("bwd_lru_up_in", dp_b, wf_lu)
    g_mix = [_wgrad("wgrad_pool_up", y_pool, dp_a, False), _wgrad("wgrad_lru_up", y_lru, dp_b, False),
             _wgrad("wgrad_out", m_mix, dr1_bf, False)]
    (du_pool, g_pool_w, g_pool_scale), (swapped,) = _pool_bwd(
        d_pool, dy_pool, wf_pool, pool_scale, stages=[_swap_halves(g_mix)])
    sums_mix = add_sibling(g_mix, swapped)
    (du_lru, du_gate, g_gate_w, g_pk), (recv_ff, recv_mix) = _lru_bwd(
        z, dy_lru, wf_gate, pk, stages=[_scatter_chips(sums_ff), _scatter_chips(sums_mix)])
    halves = sum_chips(sums_ff + sums_mix, recv_ff + recv_mix)
    g_small = [g_pool_w.reshape(n_g, N_CHIP, pgs, pg).transpose(1, 0, 2, 3).reshape(N_CHIP, n_g * pgs, pg),
               g_gate_w.reshape(n_h, N_CHIP, bks, 2, 2, bk).transpose(1, 3, 4, 0, 2, 5).reshape(N_CHIP, 4 * n_h * bks, bk)]
    dz = jnp.concatenate([du_pool, du_lru, du_gate, dg_a, dg_b], axis=1)
    g_in, (joined, swapped) = _wgrad("wgrad_in", x_bf, dz, True, stages=[_join_halves(halves), _swap_halves(g_small)])
    g_mat = dict(zip(["w_ff1", "w_ff2", "w_pool_up", "w_lru_up", "w_out"], joined))
    sums_small = add_sibling(g_small, swapped)

    def stacked(tree):
        return gate_stack(tree["lru_wa"], tree["lru_wx"]).reshape(4 * n_h * bks, bk)

    res = {}

    def update(n, stages=()):
        if n == "gate_w":
            upd = _adamw(stacked(wt), g_mat[n], stacked(mom), stacked(vel), stages)
            upd, stage_res = upd if stages else (upd, None)
            outs = [o.reshape(2, 2, n_h, bks, bk) for o in [g_mat[n]] + list(upd)]
            res["lru_wa"] = [o[:, 0][None] for o in outs]
            res["lru_wx"] = [o[:, 1][None] for o in outs]
        else:
            shp2 = mats[n].shape
            upd = _adamw(wt[n].reshape(shp2), g_mat[n], mom[n].reshape(shp2), vel[n].reshape(shp2), stages)
            upd, stage_res = upd if stages else (upd, None)
            res[n] = [o.reshape(wt[n].shape) for o in [g_mat[n]] + list(upd)]
        return stage_res

    (swapped,) = update("w_ff1", [_swap_halves([g_in])])
    sums_in = add_sibling([g_in], swapped)
    grad_x, (recv_small, recv_in) = _bwd_in(dz, wg_in, dr1, stages=[_scatter_chips(sums_small), _scatter_chips(sums_in)])
    halves = sum_chips(sums_small + sums_in, recv_small + recv_in)
    (joined,) = update("w_ff2", [_join_halves(halves)])
    g_mat.update(zip(["pool_w", "gate_w", "w_in"], joined))
    for n in ("w_pool_up", "w_lru_up", "w_out", "pool_w", "gate_w", "w_in"):
        update(n)

    g_pk = g_pk.transpose(1, 0, 2).reshape(16, d)
    vec_full = {
        "pool_scale": g_pool_scale, "conv_w": g_pk[0:4], "conv_b": g_pk[4:5],
        "lru_ba": g_pk[5:7], "lru_bx": g_pk[7:9], "lru_lambda": g_pk[9:11],
        "b_out": g_b_out, "ln1_g": g_ln1_g, "ln1_b": g_ln1_b, "b_ff1": g_b_ff1, "b_ff2": g_b_ff2,
        "ln2_g": g_ln2_g, "ln2_b": g_ln2_b,
    }
    vnames = list(vec_full)
    vg = _sum_devices(_all_gather_small(_pack([vec_full[n] for n in vnames], 1024)))
    vg = dict(zip(vnames, _unpack(vg, [vec_full[n] for n in vnames])))
    for n in ("conv_w", "lru_ba", "lru_bx", "lru_lambda"):
        vg[n] = lax.dynamic_slice_in_dim(vg[n], k_me * ds, ds, axis=1)
    vg = {n: vg[n].reshape(wt[n].shape) for n in vnames}
    upd = _adamw(_pack([wt[n] for n in vnames], 1024), _pack([vg[n] for n in vnames], 1024),
                 _pack([mom[n] for n in vnames], 1024), _pack([vel[n] for n in vnames], 1024))
    upd = [_unpack(u, [wt[n] for n in vnames]) for u in upd]
    for i, n in enumerate(vnames):
        res[n] = [vg[n], upd[0][i], upd[1][i], upd[2][i]]

    loss = lax.psum(loss_part[0, 0], ("x", "y", "c"))
    return (loss, grad_x[None], *[res[n][0] for n in WEIGHTS], *[res[n][1] for n in WEIGHTS],
            *[res[n][2] for n in WEIGHTS], *[res[n][3] for n in WEIGHTS])
```

```python
import functools
import math

import jax
import jax.numpy as jnp
from jax import lax
from jax.experimental import pallas as pl
from jax.experimental.pallas import tpu as pltpu

F32 = jnp.float32
BF16 = jnp.bfloat16
MESH = pl.DeviceIdType.MESH
ANY = pl.BlockSpec(memory_space=pl.ANY)

N_CHIP = 4
N_DEV = 8
VMEM_LIMIT_BYTES = 56 * 1024 * 1024
SUBLANES = 8
PAD = 8
SCAN_UNROLL = 8

POOL_WINDOWS = (2, 4, 8, 16)
LRU_C = 8.0
DN_ALPHA = 2.0 ** 0.25
LN_EPS = 1e-5
ADAM_LR, ADAM_B1, ADAM_B2, ADAM_EPS, ADAM_WD, ADAM_STEP = 0.001, 0.9, 0.999, 1e-08, 0.01, 10

WEIGHTS = ("w_in", "pool_w", "pool_scale", "conv_w", "conv_b", "lru_wa", "lru_ba", "lru_wx", "lru_bx", "lru_lambda",
           "w_pool_up", "w_lru_up", "w_out", "b_out", "ln1_g", "ln1_b", "w_ff1", "b_ff1", "w_ff2", "b_ff2", "ln2_g", "ln2_b")


def _cparams(sem=None):
    return pltpu.CompilerParams(dimension_semantics=sem, vmem_limit_bytes=VMEM_LIMIT_BYTES)


def _tile(dim, pref, unit=128):
    if dim <= pref:
        return dim
    t = (pref // unit) * unit
    while t > unit and dim % t:
        t -= unit
    assert dim % t == 0, (dim, pref)
    return t


def _mesh_pos():
    x, y, c = lax.axis_index("x"), lax.axis_index("y"), lax.axis_index("c")
    return x, y, c


def _other_chips(x, y):
    return [(1 - x, y), (x, 1 - y), (1 - x, 1 - y)]


def _all_gather_small(v):
    m_per, n = v.shape

    def body(x_ref, out_ref, send_sems, recv_sems, local_sem):
        x, y, c = _mesh_pos()
        me, sibling = (x, y, c), (x, y, 1 - c)
        chips = _other_chips(x, y)

        def rows(px, py, pc):
            return out_ref.at[4 * px + 2 * py + pc]

        def copy(k, block, to, src=None):
            return pltpu.make_async_remote_copy(
                src_ref=rows(*block) if src is None else src, dst_ref=rows(*block),
                send_sem=send_sems.at[k], recv_sem=recv_sems.at[k], device_id=to, device_id_type=MESH)

        mine = pltpu.make_async_copy(x_ref, rows(*me), local_sem)
        mine.start()
        first = [copy(0, me, sibling, src=x_ref)]
        first += [copy(1 + j, me, (*chip, c), src=x_ref) for j, chip in enumerate(chips)]
        for cp in first:
            cp.start()
        passed = [copy(4 + j, (*chip, c), sibling) for j, chip in enumerate(chips)]
        for j, chip in enumerate(chips):
            copy(1 + j, (*chip, c), me).wait_recv()
            passed[j].start()
        copy(0, sibling, me).wait_recv()
        for j, chip in enumerate(chips):
            copy(4 + j, (*chip, 1 - c), me).wait_recv()
        for cp in first + passed:
            cp.wait_send()
        mine.wait()

    return pl.pallas_call(
        body, name="all_gather_small",
        out_shape=jax.ShapeDtypeStruct((N_DEV, m_per, n), v.dtype),
        in_specs=[pl.BlockSpec(memory_space=pltpu.VMEM)],
        out_specs=pl.BlockSpec(memory_space=pltpu.VMEM),
        scratch_shapes=[pltpu.SemaphoreType.DMA((7,)), pltpu.SemaphoreType.DMA((7,)), pltpu.SemaphoreType.DMA],
    )(v)


class _Stage:
    def __init__(self, srcs, bufs, news, n_sems, copies):
        self.srcs, self.bufs, self.news, self.n_sems, self.copies = list(srcs), list(bufs), list(news), n_sems, copies


def _remote(src, dst, send_sems, recv_sems, s, to):
    return pltpu.make_async_remote_copy(src_ref=src, dst_ref=dst, send_sem=send_sems.at[s], recv_sem=recv_sems.at[s],
                                        device_id=to, device_id_type=MESH)


def _stage_operands(stages, n_in, n_out):
    ins, outs, aliases, scratch = [], [], {}, []
    for st in stages:
        for i in range(len(st.bufs)):
            aliases[n_in + len(ins) + len(st.srcs) + i] = n_out + len(outs) + i
        ins += st.srcs + st.bufs
        outs += [jax.ShapeDtypeStruct(b.shape, b.dtype) for b in st.bufs] + st.news
        scratch += [pltpu.SemaphoreType.DMA((st.n_sems,)), pltpu.SemaphoreType.DMA((st.n_sems,))]
    return ins, outs, aliases, scratch


def _stage_refs(stages, in_refs, out_refs, sem_refs):
    parts, i, o = [], 0, 0
    for n, st in enumerate(stages):
        src = in_refs[i:i + len(st.srcs)]
        i += len(st.srcs) + len(st.bufs)
        buf = out_refs[o:o + len(st.bufs)]
        new = out_refs[o + len(st.bufs):o + len(st.bufs) + len(st.news)]
        o += len(st.bufs) + len(st.news)
        parts.append((src, buf, new, sem_refs[2 * n], sem_refs[2 * n + 1]))
    return parts


def _stage_results(stages, res):
    out, o = [], 0
    for st in stages:
        n = len(st.bufs) + len(st.news)
        out.append(list(res[o:o + n]))
        o += n
    return out


def _stages_start(stages, parts):
    for st, part in zip(stages, parts):
        for cp in st.copies(*part)[0]:
            cp.start()


def _stages_wait(stages, parts):
    for st, part in zip(stages, parts):
        started, landing = st.copies(*part)
        for cp in landing:
            cp.wait_recv()
        for cp in started:
            cp.wait_send()


def _run_stages(name, stages):
    ins, outs, aliases, scratch = _stage_operands(stages, 0, 0)

    def body(*refs):
        parts = _stage_refs(stages, refs[:len(ins)], refs[len(ins):len(ins) + len(outs)], refs[len(ins) + len(outs):])
        for st, part in zip(stages, parts):
            _stages_start([st], [part])
            _stages_wait([st], [part])

    res = pl.pallas_call(
        body, name=name, out_shape=outs, in_specs=[ANY] * len(ins), out_specs=[ANY] * len(outs),
        input_output_aliases=aliases, scratch_shapes=scratch)(*ins)
    return _stage_results(stages, res)


def _gather_ici(ts):
    def copies(src, buf, new, send_sems, recv_sems):
        x, y, c = _mesh_pos()
        started, landing = [], []
        for t in range(len(ts)):
            rh = ts[t].shape[1] // 2
            rows = pl.ds(c * rh, rh)
            for j, chip in enumerate(_other_chips(x, y)):
                mine = buf[t].at[2 * x + y, rows]
                theirs = buf[t].at[2 * chip[0] + chip[1], rows]
                started.append(_remote(mine, mine, send_sems, recv_sems, 3 * t + j, (*chip, c)))
                landing.append(_remote(theirs, theirs, send_sems, recv_sems, 3 * t + j, (x, y, c)))
        return started, landing

    return _Stage([], ts, [], 3 * len(ts), copies)


def _gather_d2d(ts):
    def copies(src, buf, new, send_sems, recv_sems):
        x, y, c = _mesh_pos()
        started, landing = [], []
        for t in range(len(ts)):
            rh = ts[t].shape[1] // 2
            for j, chip in enumerate(_other_chips(x, y)):
                got = buf[t].at[2 * chip[0] + chip[1], pl.ds(c * rh, rh)]
                other = buf[t].at[2 * chip[0] + chip[1], pl.ds((1 - c) * rh, rh)]
                started.append(_remote(got, got, send_sems, recv_sems, 3 * t + j, (x, y, 1 - c)))
                landing.append(_remote(other, other, send_sems, recv_sems, 3 * t + j, (x, y, c)))
        return started, landing

    return _Stage([], ts, [], 3 * len(ts), copies)


def _swap_halves(gs):
    def copies(src, buf, new, send_sems, recv_sems):
        x, y, c = _mesh_pos()
        started, landing = [], []
        for t in range(len(gs)):
            rh = gs[t].shape[1] // 2
            started.append(_remote(src[t].at[:, pl.ds((1 - c) * rh, rh)], new[t], send_sems, recv_sems, t, (x, y, 1 - c)))
            landing.append(_remote(new[t], new[t], send_sems, recv_sems, t, (x, y, c)))
        return started, landing

    news = [jax.ShapeDtypeStruct((g.shape[0], g.shape[1] // 2, g.shape[2]), g.dtype) for g in gs]
    return _Stage(gs, [], news, len(gs), copies)


def _scatter_chips(ps):
    def copies(src, buf, new, send_sems, recv_sems):
        x, y, c = _mesh_pos()
        started, landing = [], []
        for t in range(len(ps)):
            for j, chip in enumerate(_other_chips(x, y)):
                started.append(_remote(src[t].at[2 * chip[0] + chip[1]], new[t].at[j], send_sems, recv_sems, 3 * t + j, (*chip, c)))
                landing.append(_remote(new[t].at[j], new[t].at[j], send_sems, recv_sems, 3 * t + j, (x, y, c)))
        return started, landing

    return _Stage(ps, [], [jax.ShapeDtypeStruct((3,) + p.shape[1:], p.dtype) for p in ps], 3 * len(ps), copies)


def _join_halves(fs):
    def copies(src, buf, new, send_sems, recv_sems):
        x, y, c = _mesh_pos()
        started, landing = [], []
        for t in range(len(fs)):
            rh = fs[t].shape[0] // 2
            mine = buf[t].at[pl.ds(c * rh, rh)]
            theirs = buf[t].at[pl.ds((1 - c) * rh, rh)]
            started.append(_remote(mine, mine, send_sems, recv_sems, t, (x, y, 1 - c)))
            landing.append(_remote(theirs, theirs, send_sems, recv_sems, t, (x, y, c)))
        return started, landing

    return _Stage([], fs, [], len(fs), copies)


def _cast_place(w, k_me):
    rows, cols = w.shape
    tr = _tile(rows, 512, 16)

    def body(k_ref, w_ref, o_ref):
        o_ref[...] = w_ref[...].astype(BF16)

    return pl.pallas_call(
        body, name="cast_place", out_shape=_sds((N_CHIP, rows, cols), BF16),
        grid_spec=pltpu.PrefetchScalarGridSpec(
            num_scalar_prefetch=1, grid=(rows // tr,),
            in_specs=[pl.BlockSpec((tr, cols), lambda i, k_ref: (i, 0))],
            out_specs=pl.BlockSpec((None, tr, cols), lambda i, k_ref: (k_ref[0], i, 0))),
        compiler_params=_cparams(("arbitrary",)),
    )(_scalar(k_me), w)


_DIMS = {"nn": (((1,), (0,)), ((), ())), "nt": (((1,), (1,)), ((), ())), "tn": (((0,), (0,)), ((), ()))}


def _accum(ref, val, first):
    @pl.when(first)
    def _():
        ref[...] = val

    @pl.when(jnp.logical_not(first))
    def _():
        ref[...] += val


def _grid_edges(grid):
    ids = [pl.program_id(ax) for ax in range(len(grid))]
    first = functools.reduce(jnp.logical_and, [i == 0 for i in ids])
    last = functools.reduce(jnp.logical_and, [i == n - 1 for i, n in zip(ids, grid)])
    return first, last


def _host_call(name, grid, body, operands, in_specs, out_shape, out_specs, scratch, stages):
    s_ins, s_outs, aliases, s_scratch = _stage_operands(stages, len(operands), len(out_shape))
    n_in, n_out, n_scr = len(operands), len(out_shape), len(scratch)

    def full_body(*refs):
        in_refs = refs[:n_in]
        s_in_refs = refs[n_in:n_in + len(s_ins)]
        o0 = n_in + len(s_ins)
        out_refs = refs[o0:o0 + n_out]
        s_out_refs = refs[o0 + n_out:o0 + n_out + len(s_outs)]
        c0 = o0 + n_out + len(s_outs)
        scr_refs = refs[c0:c0 + n_scr]
        if stages:
            parts = _stage_refs(stages, s_in_refs, s_out_refs, refs[c0 + n_scr:])
            first, last = _grid_edges(grid)
            pl.when(first)(lambda: _stages_start(stages, parts))
        body(in_refs, out_refs, scr_refs)
        if stages:
            pl.when(last)(lambda: _stages_wait(stages, parts))

    res = pl.pallas_call(
        full_body, name=name, grid=grid, in_specs=list(in_specs) + [ANY] * len(s_ins),
        out_specs=list(out_specs) + [ANY] * len(s_outs), out_shape=list(out_shape) + s_outs,
        input_output_aliases=aliases, scratch_shapes=list(scratch) + s_scratch,
        compiler_params=_cparams(("arbitrary",) * len(grid)),
    )(*operands, *s_ins)
    return list(res[:n_out]), _stage_results(stages, res[n_out:])


def _matmul(name, grid, pairs, extras, outs, acc_shape, epilogue, stages=()):
    n_p = len(pairs)
    n_k = grid[-1]
    dims = [_DIMS[p[4]] for p in pairs]

    def body(in_refs, out, accs):
        ab, ex = in_refs[:2 * n_p], in_refs[2 * n_p:]
        ids = [pl.program_id(ax) for ax in range(len(grid))]
        k = ids[-1]

        @pl.when(k == 0)
        def _():
            for acc in accs:
                acc[...] = jnp.zeros_like(acc)

        for p in range(n_p):
            a = ab[2 * p][...].astype(BF16)
            b = ab[2 * p + 1][...].astype(BF16)
            accs[p][...] += lax.dot_general(a, b, dims[p], preferred_element_type=F32)

        @pl.when(k == n_k - 1)
        def _():
            epilogue([acc[...] for acc in accs], ex, out, ids)

    in_specs = []
    operands = []
    for a, a_spec, b, b_spec, _ in pairs:
        in_specs += [a_spec, b_spec]
        operands += [a, b]
    for e, e_spec in extras:
        in_specs.append(e_spec)
        operands.append(e)
    res, stage_res = _host_call(name, grid, body, operands, in_specs, [o[0] for o in outs], [o[1] for o in outs],
                                [pltpu.VMEM(acc_shape, F32) for _ in pairs], list(stages))
    return (res, stage_res) if stages else res


def _out(res, stages, single=False):
    outs = res[0] if stages else res
    outs = outs[0] if single else outs
    return (outs, res[1]) if stages else outs


def _sds(shape, dtype):
    return jax.ShapeDtypeStruct(shape, dtype)


def _row(n):
    return pl.BlockSpec((1, n), lambda *_: (0, 0))


def _layer_norm(r):
    mu = jnp.mean(r, axis=-1, keepdims=True)
    xc = r - mu
    var = jnp.mean(xc * xc, axis=-1, keepdims=True)
    rstd = lax.rsqrt(var + LN_EPS)
    return xc * rstd, rstd


def _layer_norm_bwd(dxhat, xhat, rstd):
    m1 = jnp.mean(dxhat, axis=-1, keepdims=True)
    m2 = jnp.mean(dxhat * xhat, axis=-1, keepdims=True)
    return rstd * (dxhat - m1 - xhat * m2)


def _colsum(v):
    return jnp.sum(v, axis=0, keepdims=True)


def _fwd_in(x_bf, wg_in, stages=()):
    s, d = x_bf.shape
    inc = wg_in.shape[2]
    tm, tn, tk = _tile(s, 1024), _tile(inc, 1280), _tile(d, 2048)
    nb = inc // tn

    def epi(accs, ex, out, ids):
        out[0][...] = accs[0].astype(BF16)

    return _out(_matmul(
        "fwd_in", (s // tm, N_CHIP * nb, d // tk),
        [(x_bf, pl.BlockSpec((tm, tk), lambda i, j, k: (i, k)),
          wg_in, pl.BlockSpec((None, tk, tn), lambda i, j, k: (j // nb, k, j % nb)), "nn")],
        [], [(_sds((s, N_CHIP * inc), BF16), pl.BlockSpec((tm, tn), lambda i, j, k: (i, j)))],
        (tm, tn), epi, stages), stages, True)


def _fwd_merge(y_pool, y_lru, w_pu, w_lu, z, stages=()):
    s, d = y_pool.shape
    tm, tn, tk = _tile(s, 1024), _tile(d, 1024), _tile(d, 1024)
    ga0, gb0 = 3 * d // tn, 4 * d // tn

    def epi(accs, ex, out, ids):
        sa = _sigmoid(ex[0][...].astype(F32))
        sb = _sigmoid(ex[1][...].astype(F32))
        out[0][...] = (sa * accs[0] + sb * accs[1]).astype(BF16)
        out[1][...] = accs[0].astype(BF16)
        out[2][...] = accs[1].astype(BF16)

    a_spec = pl.BlockSpec((tm, tk), lambda i, j, k: (i, k))
    b_spec = pl.BlockSpec((tk, tn), lambda i, j, k: (k, j))
    o_spec = pl.BlockSpec((tm, tn), lambda i, j, k: (i, j))
    return _out(_matmul(
        "fwd_merge", (s // tm, d // tn, d // tk),
        [(y_pool, a_spec, w_pu, b_spec, "nn"), (y_lru, a_spec, w_lu, b_spec, "nn")],
        [(z, pl.BlockSpec((tm, tn), lambda i, j, k: (i, ga0 + j))), (z, pl.BlockSpec((tm, tn), lambda i, j, k: (i, gb0 + j)))],
        [(_sds((s, d), BF16), o_spec)] * 3, (tm, tn), epi, stages), stages)


def _fwd_out_ln1(m, w_out, x, b_out, g1, b1, stages=()):
    s, d = x.shape
    tm, tk = _tile(s, 512), _tile(d, 2048)

    def epi(accs, ex, out, ids):
        r = DN_ALPHA * ex[0][...] + accs[0] + ex[1][...]
        xhat, rstd = _layer_norm(r)
        out[0][...] = xhat
        out[1][...] = (xhat * ex[2][...] + ex[3][...]).astype(BF16)
        out[2][...] = rstd

    full = pl.BlockSpec((tm, d), lambda i, j, k: (i, 0))
    return _out(_matmul(
        "fwd_out_ln1", (s // tm, 1, d // tk),
        [(m, pl.BlockSpec((tm, tk), lambda i, j, k: (i, k)), w_out, pl.BlockSpec((tk, d), lambda i, j, k: (k, 0)), "nn")],
        [(x, full), (b_out, _row(d)), (g1, _row(d)), (b1, _row(d))],
        [(_sds((s, d), F32), full), (_sds((s, d), BF16), full), (_sds((s, 1), F32), pl.BlockSpec((tm, 1), lambda i, j, k: (i, 0)))],
        (tm, d), epi, stages), stages)


def _fwd_ff1(x1_bf, wg_ff1, b_ff1, stages=()):
    s, d = x1_bf.shape
    fc = wg_ff1.shape[2]
    tm, tn, tk = _tile(s, 1024), _tile(fc, 1024), _tile(d, 2048)
    nb = fc // tn

    def epi(accs, ex, out, ids):
        p = jnp.maximum(accs[0] + ex[0][...], 0.0)
        out[0][...] = (p * p).astype(BF16)

    return _out(_matmul(
        "fwd_ff1", (s // tm, N_CHIP * nb, d // tk),
        [(x1_bf, pl.BlockSpec((tm, tk), lambda i, j, k: (i, k)),
          wg_ff1, pl.BlockSpec((None, tk, tn), lambda i, j, k: (j // nb, k, j % nb)), "nn")],
        [(b_ff1, pl.BlockSpec((1, tn), lambda i, j, k: (0, j)))],
        [(_sds((s, N_CHIP * fc), BF16), pl.BlockSpec((tm, tn), lambda i, j, k: (i, j)))],
        (tm, tn), epi, stages), stages, True)


def _fwd_ff2_ln2_loss(hdn, w_ff2, xhat1, g1, b1, b_ff2, g2, b2, target, stages=()):
    s, f = hdn.shape
    d = xhat1.shape[1]
    tm, tk = _tile(s, 512), _tile(f, 1024)

    def epi(accs, ex, out, ids):
        first = ids[0] == 0
        x1 = ex[0][...] * ex[1][...] + ex[2][...]
        r = DN_ALPHA * x1 + accs[0] + ex[3][...]
        xhat, rstd = _layer_norm(r)
        g2v = ex[4][...]
        err = xhat * g2v + ex[5][...] - ex[6][...]
        part = 0.5 * jnp.sum(jnp.mean(err * err, axis=-1, keepdims=True), axis=0, keepdims=True)
        dy = err * (1.0 / d)
        dr2 = _layer_norm_bwd(dy * g2v, xhat, rstd)
        out[0][...] = dr2
        out[1][...] = dr2.astype(BF16)
        _accum(out[2], _colsum(dy * xhat), first)
        _accum(out[3], _colsum(dy), first)
        _accum(out[4], _colsum(dr2), first)
        _accum(out[5], jnp.broadcast_to(part, (1, 128)), first)

    full = pl.BlockSpec((tm, d), lambda i, j, k: (i, 0))
    return _out(_matmul(
        "fwd_ff2_ln2_loss", (s // tm, 1, f // tk),
        [(hdn, pl.BlockSpec((tm, tk), lambda i, j, k: (i, k)), w_ff2, pl.BlockSpec((tk, d), lambda i, j, k: (k, 0)), "nn")],
        [(xhat1, full), (g1, _row(d)), (b1, _row(d)), (b_ff2, _row(d)), (g2, _row(d)), (b2, _row(d)), (target, full)],
        [(_sds((s, d), F32), full), (_sds((s, d), BF16), full), (_sds((1, d), F32), _row(d)), (_sds((1, d), F32), _row(d)),
         (_sds((1, d), F32), _row(d)), (_sds((1, 128), F32), _row(128))],
        (tm, d), epi, stages), stages)


def _bwd_ff2_in(dr2_bf, w_ff2, hdn, stages=()):
    s, d = dr2_bf.shape
    f = hdn.shape[1]
    tm, tn, tk = _tile(s, 1024), _tile(f, 1024), _tile(d, 2048)

    def epi(accs, ex, out, ids):
        dpre = accs[0] * (2.0 * jnp.sqrt(ex[0][...].astype(F32)))
        out[0][...] = dpre.astype(BF16)
        _accum(out[1], _colsum(dpre), ids[1] == 0)

    return _out(_matmul(
        "bwd_ff2_in", (f // tn, s // tm, d // tk),
        [(dr2_bf, pl.BlockSpec((tm, tk), lambda j, i, k: (i, k)), w_ff2, pl.BlockSpec((tn, tk), lambda j, i, k: (j, k)), "nt")],
        [(hdn, pl.BlockSpec((tm, tn), lambda j, i, k: (i, j)))],
        [(_sds((s, f), BF16), pl.BlockSpec((tm, tn), lambda j, i, k: (i, j))), (_sds((1, f), F32), pl.BlockSpec((1, tn), lambda j, i, k: (0, j)))],
        (tm, tn), epi, stages), stages)


def _bwd_ff1_in_ln1(dpre, wg_ff1, dr2, xhat1, rstd1, g1, stages=()):
    s, f = dpre.shape
    d = xhat1.shape[1]
    fc = wg_ff1.shape[2]
    tm, tk = _tile(s, 512), _tile(fc, 1024)
    nb = fc // tk

    def epi(accs, ex, out, ids):
        first = ids[0] == 0
        xhat = ex[1][...]
        dx1 = accs[0] + DN_ALPHA * ex[0][...]
        dr1 = _layer_norm_bwd(dx1 * ex[3][...], xhat, ex[2][...])
        out[0][...] = dr1
        out[1][...] = dr1.astype(BF16)
        _accum(out[2], _colsum(dx1 * xhat), first)
        _accum(out[3], _colsum(dx1), first)
        _accum(out[4], _colsum(dr1), first)

    full = pl.BlockSpec((tm, d), lambda i, j, k: (i, 0))
    return _out(_matmul(
        "bwd_ff1_in_ln1", (s // tm, 1, f // tk),
        [(dpre, pl.BlockSpec((tm, tk), lambda i, j, k: (i, k)),
          wg_ff1, pl.BlockSpec((None, d, tk), lambda i, j, k: (k // nb, 0, k % nb)), "nt")],
        [(dr2, full), (xhat1, full), (rstd1, pl.BlockSpec((tm, 1), lambda i, j, k: (i, 0))), (g1, _row(d))],
        [(_sds((s, d), F32), full), (_sds((s, d), BF16), full), (_sds((1, d), F32), _row(d)), (_sds((1, d), F32), _row(d)),
         (_sds((1, d), F32), _row(d))],
        (tm, d), epi, stages), stages)


def _bwd_out_in(dr1_bf, w_out, z, pa, pb, stages=()):
    s, d = dr1_bf.shape
    tm, tn, tk = _tile(s, 1024), _tile(d, 1024), _tile(d, 2048)
    ga0, gb0 = 3 * d // tn, 4 * d // tn

    def epi(accs, ex, out, ids):
        dm = accs[0]
        sa = _sigmoid(ex[0][...].astype(F32))
        sb = _sigmoid(ex[1][...].astype(F32))
        out[0][...] = (dm * sa).astype(BF16)
        out[1][...] = (dm * sb).astype(BF16)
        out[2][...] = (dm * ex[2][...].astype(F32) * sa * (1.0 - sa)).astype(BF16)
        out[3][...] = (dm * ex[3][...].astype(F32) * sb * (1.0 - sb)).astype(BF16)

    o_spec = pl.BlockSpec((tm, tn), lambda i, j, k: (i, j))
    return _out(_matmul(
        "bwd_out_in", (s // tm, d // tn, d // tk),
        [(dr1_bf, pl.BlockSpec((tm, tk), lambda i, j, k: (i, k)), w_out, pl.BlockSpec((tn, tk), lambda i, j, k: (j, k)), "nt")],
        [(z, pl.BlockSpec((tm, tn), lambda i, j, k: (i, ga0 + j))), (z, pl.BlockSpec((tm, tn), lambda i, j, k: (i, gb0 + j))),
         (pa, o_spec), (pb, o_spec)],
        [(_sds((s, d), BF16), o_spec)] * 4, (tm, tn), epi, stages), stages)


def _bwd_up_in(name, dp, w_up, stages=()):
    s, d = dp.shape
    n = w_up.shape[0]
    tm, tn, tk = _tile(s, 1024), _tile(n, 1024), _tile(d, 2048)

    def epi(accs, ex, out, ids):
        out[0][...] = accs[0].astype(BF16)

    return _out(_matmul(
        name, (s // tm, n // tn, d // tk),
        [(dp, pl.BlockSpec((tm, tk), lambda i, j, k: (i, k)), w_up, pl.BlockSpec((tn, tk), lambda i, j, k: (j, k)), "nt")],
        [], [(_sds((s, n), BF16), pl.BlockSpec((tm, tn), lambda i, j, k: (i, j)))], (tm, tn), epi, stages), stages, True)


def _bwd_in(dz, wg_in, dr1, stages=()):
    s, d = dr1.shape
    inc = wg_in.shape[2]
    tm, tn, tk = _tile(s, 1024), _tile(d, 1024), _tile(inc, 1280)
    nb = inc // tk

    def epi(accs, ex, out, ids):
        out[0][...] = accs[0] + DN_ALPHA * ex[0][...]

    o_spec = pl.BlockSpec((tm, tn), lambda i, j, k: (i, j))
    return _out(_matmul(
        "bwd_in", (s // tm, d // tn, N_CHIP * nb),
        [(dz, pl.BlockSpec((tm, tk), lambda i, j, k: (i, k)),
          wg_in, pl.BlockSpec((None, tn, tk), lambda i, j, k: (k // nb, j, k % nb)), "nt")],
        [(dr1, o_spec)], [(_sds((s, d), F32), o_spec)], (tm, tn), epi, stages), stages, True)


def _wgrad(name, a, b, col_sharded, stages=()):
    s, ka = a.shape
    n = b.shape[1]
    tm, tk = _tile(ka, 1024), _tile(s, 1024)
    tn = _tile(n // N_CHIP, 1280) if col_sharded else _tile(n, 1024)

    def epi(accs, ex, out, ids):
        out[0][...] = accs[0].astype(BF16)

    if col_sharded:
        nb = (n // N_CHIP) // tn
        o = (_sds((N_CHIP, ka, n // N_CHIP), BF16), pl.BlockSpec((None, tm, tn), lambda i, j, k: (j // nb, i, j % nb)))
    else:
        o = (_sds((ka, n), BF16), pl.BlockSpec((tm, tn), lambda i, j, k: (i, j)))
    res = _out(_matmul(
        name, (ka // tm, n // tn, s // tk),
        [(a, pl.BlockSpec((tk, tm), lambda i, j, k: (k, i)), b, pl.BlockSpec((tk, tn), lambda i, j, k: (k, j)), "tn")],
        [], [o], (tm, tn), epi, stages), stages, True)
    res, stage_res = res if stages else (res, None)
    res = res if col_sharded else res.reshape(N_CHIP, ka // N_CHIP, n)
    return (res, stage_res) if stages else res


def _chunk(s):
    return _tile(s, 512, SUBLANES)


def _zero_pads(ref, s):
    zeros = jnp.zeros((PAD, ref.shape[1]), F32)
    ref[pl.ds(0, PAD), :] = zeros
    ref[pl.ds(PAD + s, PAD), :] = zeros


def _window(ref, t0, t):
    return ref[pl.ds(t0, t + 2 * PAD), :]


def _shift(sup, off, t):
    return sup[PAD + off:PAD + off + t, :]


def _pool_count(t0, t, s, w):
    pos = t0 + lax.broadcasted_iota(jnp.int32, (t, 1), 0)
    return (jnp.minimum(pos + w // 2, s) - jnp.maximum(pos - w // 2, 0)).astype(F32)


def _pool_fwd(z, pool_w, pool_scale, stages=()):
    s = z.shape[0]
    n_g, pg = pool_w.shape[0], pool_w.shape[1]
    assert n_g == len(POOL_WINDOWS) and max(POOL_WINDOWS) // 2 <= PAD
    t = _chunk(s)

    def body(u_ref, w_ref, sc_ref, d_ref, y_ref, pad_ref):
        g = pl.program_id(0)
        _zero_pads(pad_ref, s)
        pad_ref[pl.ds(PAD, s), :] = u_ref[...].astype(F32)
        for gi, w in enumerate(POOL_WINDOWS):
            @pl.when(g == gi)
            def _():
                def step(ch, carry):
                    t0 = pl.multiple_of(ch * t, t)
                    sup = _window(pad_ref, t0, t)
                    acc = _shift(sup, -(w // 2), t)
                    for o in range(-(w // 2) + 1, w // 2):
                        acc = acc + _shift(sup, o, t)
                    dd = (acc * (1.0 / _pool_count(t0, t, s, w)) - _shift(sup, 0, t)).astype(BF16)
                    d_ref[pl.ds(t0, t), :] = dd
                    y = jnp.dot(dd, w_ref[...], preferred_element_type=F32) * sc_ref[...]
                    y_ref[pl.ds(t0, t), :] = y.astype(BF16)
                    return carry

                lax.fori_loop(0, s // t, step, 0)

    blk = pl.BlockSpec((s, pg), lambda g: (0, g))
    res = _host_call(
        "pool_fwd", (n_g,), lambda ins, outs, scr: body(*ins, *outs, *scr), [z, pool_w, pool_scale],
        [blk, pl.BlockSpec((None, pg, pg), lambda g: (g, 0, 0)), pl.BlockSpec((1, pg), lambda g: (0, g))],
        [_sds((s, n_g * pg), BF16)] * 2, [blk, blk], [pltpu.VMEM((s + 2 * PAD, pg), F32)], list(stages))
    return res if stages else res[0]


def _pool_bwd(dsv, dy, pool_w, pool_scale, stages=()):
    s = dsv.shape[0]
    n_g, pg = pool_w.shape[0], pool_w.shape[1]
    t = _chunk(s)

    def body(d_ref, dy_ref, w_ref, sc_ref, du_ref, dw_ref, dsc_ref, epad_ref, dwacc_ref):
        g = pl.program_id(0)
        _zero_pads(epad_ref, s)
        dwacc_ref[...] = jnp.zeros_like(dwacc_ref)
        for gi, w in enumerate(POOL_WINDOWS):
            @pl.when(g == gi)
            def _():
                def first(ch, dsc):
                    t0 = pl.multiple_of(ch * t, t)
                    dd = d_ref[pl.ds(t0, t), :]
                    dyc = dy_ref[pl.ds(t0, t), :].astype(F32)
                    wv = w_ref[...]
                    ypre = jnp.dot(dd, wv, preferred_element_type=F32)
                    dq = (dyc * sc_ref[...]).astype(BF16)
                    dwacc_ref[...] += lax.dot_general(dd, dq, _DIMS["tn"], preferred_element_type=F32)
                    ddv = lax.dot_general(dq, wv, _DIMS["nt"], preferred_element_type=F32)
                    epad_ref[pl.ds(pl.multiple_of(PAD + t0, SUBLANES), t), :] = ddv * (1.0 / _pool_count(t0, t, s, w))
                    return dsc + _colsum(dyc * ypre)

                dsc_ref[...] = lax.fori_loop(0, s // t, first, jnp.zeros((1, pg), F32))

                def second(ch, carry):
                    t0 = pl.multiple_of(ch * t, t)
                    sup = _window(epad_ref, t0, t)
                    acc = _shift(sup, -(w // 2) + 1, t)
                    for o in range(-(w // 2) + 2, w // 2 + 1):
                        acc = acc + _shift(sup, o, t)
                    du_ref[pl.ds(t0, t), :] = (acc - _shift(sup, 0, t) * _pool_count(t0, t, s, w)).astype(BF16)
                    return carry

                lax.fori_loop(0, s // t, second, 0)

        dw_ref[...] = dwacc_ref[...].astype(BF16)

    blk = pl.BlockSpec((s, pg), lambda g: (0, g))
    w_spec = pl.BlockSpec((None, pg, pg), lambda g: (g, 0, 0))
    sc_spec = pl.BlockSpec((1, pg), lambda g: (0, g))
    res = _host_call(
        "pool_bwd", (n_g,), lambda ins, outs, scr: body(*ins, *outs, *scr), [dsv, dy, pool_w, pool_scale],
        [blk, blk, w_spec, sc_spec], [_sds((s, n_g * pg), BF16), _sds((n_g, pg, pg), BF16), _sds((1, n_g * pg), F32)],
        [blk, w_spec, sc_spec], [pltpu.VMEM((s + 2 * PAD, pg), F32), pltpu.VMEM((pg, pg), F32)], list(stages))
    return res if stages else res[0]


def _sigmoid(x):
    return 0.5 * jnp.tanh(0.5 * x) + 0.5


def _softplus(x):
    e = jnp.exp(-jnp.abs(x))
    log1p_e = jnp.where(e < 1e-2, e * (1.0 - e * (0.5 - e * (1.0 / 3.0))), jnp.log(1.0 + e))
    return jnp.maximum(x, 0.0) + log1p_e


_GELU_C = math.sqrt(2.0 / math.pi)


def _gelu(x):
    th = jnp.tanh(_GELU_C * (x + 0.044715 * x * x * x))
    return 0.5 * x * (1.0 + th), th


def _gelu_grad(x, th):
    return 0.5 * (1.0 + th) + 0.5 * x * (1.0 - th * th) * _GELU_C * (1.0 + 3.0 * 0.044715 * x * x)


def _scan_chunk(a_ref, b_ref, o_ref, o_off, carry, t, reverse):
    n = a_ref.shape[1]
    row = lax.broadcasted_iota(jnp.int32, (SUBLANES, n), 0)
    n_groups = t // SUBLANES
    unroll = math.gcd(n_groups, SCAN_UNROLL)
    last = 0 if reverse else SUBLANES - 1

    def step(si, carry):
        for u in range(unroll):
            gi = si * unroll + u
            g = n_groups - 1 - gi if reverse else gi
            r0 = pl.multiple_of(g * SUBLANES, SUBLANES)
            a = a_ref[pl.ds(r0, SUBLANES), :]
            b = b_ref[pl.ds(r0, SUBLANES), :]
            for k in (1, 2, 4):
                keep = row < SUBLANES - k if reverse else row >= k
                sh = SUBLANES - k if reverse else k
                ar = jnp.where(keep, pltpu.roll(a, sh, 0), 1.0)
                br = jnp.where(keep, pltpu.roll(b, sh, 0), 0.0)
                b = a * br + b
                a = a * ar
            o_ref[pl.ds(pl.multiple_of(o_off + r0, SUBLANES), SUBLANES), :] = a * carry + b
            carry = (jnp.broadcast_to(a[last:last + 1, :], a.shape) * carry
                     + jnp.broadcast_to(b[last:last + 1, :], b.shape))
        return carry

    return lax.fori_loop(0, n_groups // unroll, step, carry)


def _lru_params(pk_ref):
    rows = pk_ref[...]
    get = lambda i: rows[i:i + 1, :]
    cw = [get(k) for k in range(4)]
    lam = (get(9), get(10))
    big_l = tuple(-LRU_C * _softplus(-v) for v in lam)
    return cw, get(4), (get(5), get(6)), (get(7), get(8)), lam, big_l


def _conv(sup, cw, cb, t):
    xc = cb + cw[0] * _shift(sup, -2, t)
    for k in range(1, 4):
        xc = xc + cw[k] * _shift(sup, k - 2, t)
    return xc


def _gates(xcb, w_ref, d, bk, ba, bx, big_l):
    pre = jnp.dot(xcb, w_ref[:, pl.ds(d * 2 * bk, 2 * bk)], preferred_element_type=F32)
    r = _sigmoid(pre[:, :bk] + ba[d])
    i = _sigmoid(pre[:, bk:] + bx[d])
    la = big_l[d] * r
    a = jnp.exp(la)
    var = jnp.tanh(-la) * (1.0 + a * a)
    rs = lax.rsqrt(jnp.maximum(var, 1e-30))
    return r, i, a, var * rs, rs


def _lru_specs(s, d, bk):
    u_spec = pl.BlockSpec((s, bk), lambda h: (0, d // bk + h))
    ug_spec = pl.BlockSpec((s, bk), lambda h: (0, 2 * d // bk + h))
    w_spec = pl.BlockSpec((None, bk, 4 * bk), lambda h: (h, 0, 0))
    pk_spec = pl.BlockSpec((None, 16, bk), lambda h: (h, 0, 0))
    blk = pl.BlockSpec((s, bk), lambda h: (0, h))
    return u_spec, ug_spec, w_spec, pk_spec, blk


def _lru_fwd(z, gatew, pk, stages=()):
    s = z.shape[0]
    n_h, bk = gatew.shape[0], gatew.shape[1]
    d = n_h * bk
    t = _chunk(s)
    n_ch = s // t

    def body(u_ref, ug_ref, w_ref, pk_ref, y_ref, upad, h0buf, abuf, bbuf, xcbuf, h1buf):
        _zero_pads(upad, s)
        upad[pl.ds(PAD, s), :] = u_ref[...].astype(F32)
        cw, cb, ba, bx, _, big_l = _lru_params(pk_ref)
        zero = jnp.zeros((SUBLANES, bk), F32)

        def fill(xc, dr):
            _, i, a, sq, _ = _gates(xc.astype(BF16), w_ref, dr, bk, ba, bx, big_l)
            abuf[...] = a
            bbuf[...] = sq * i * xc

        def up(ch, carry):
            t0 = pl.multiple_of(ch * t, t)
            xc = _conv(_window(upad, t0, t), cw, cb, t)
            xcbuf[pl.ds(t0, t), :] = xc
            fill(xc, 0)
            return _scan_chunk(abuf, bbuf, h0buf, t0, carry, t, False)

        lax.fori_loop(0, n_ch, up, zero)

        def down(ci, carry):
            t0 = pl.multiple_of((n_ch - 1 - ci) * t, t)
            fill(xcbuf[pl.ds(t0, t), :], 1)
            carry = _scan_chunk(abuf, bbuf, h1buf, 0, carry, t, True)
            gl, _ = _gelu(ug_ref[pl.ds(t0, t), :].astype(F32))
            y_ref[pl.ds(t0, t), :] = ((h0buf[pl.ds(t0, t), :] + h1buf[...]) * gl).astype(BF16)
            return carry

        lax.fori_loop(0, n_ch, down, zero)

    u_spec, ug_spec, w_spec, pk_spec, blk = _lru_specs(s, d, bk)
    res = _host_call(
        "lru_fwd", (n_h,), lambda ins, outs, scr: body(*ins, *outs, *scr), [z, z, gatew, pk],
        [u_spec, ug_spec, w_spec, pk_spec], [_sds((s, d), BF16)], [blk],
        [pltpu.VMEM((s + 2 * PAD, bk), F32), pltpu.VMEM((s, bk), F32), pltpu.VMEM((t, bk), F32), pltpu.VMEM((t, bk), F32),
         pltpu.VMEM((s, bk), F32), pltpu.VMEM((t, bk), F32)], list(stages))
    return (res[0][0], res[1]) if stages else res[0][0]


def _lru_grads(lam_, hnb, a, sq, rs, r, i, xc, xcb, w_ref, dwacc, d, big_l, acc):
    bk = xc.shape[1]
    dba, dbx, dl = acc
    q = lam_ * i * xc
    dla = lam_ * hnb * a - q * (a * a) * rs
    dpr = dla * big_l * r * (1.0 - r)
    dpi = q * sq * (1.0 - i)
    dprb, dpib = dpr.astype(BF16), dpi.astype(BF16)
    c0 = d * 2 * bk
    dxc = (lam_ * sq * i
           + lax.dot_general(dprb, w_ref[:, pl.ds(c0, bk)], _DIMS["nt"], preferred_element_type=F32)
           + lax.dot_general(dpib, w_ref[:, pl.ds(c0 + bk, bk)], _DIMS["nt"], preferred_element_type=F32))
    dwacc[:, pl.ds(c0, bk)] += lax.dot_general(xcb, dprb, _DIMS["tn"], preferred_element_type=F32)
    dwacc[:, pl.ds(c0 + bk, bk)] += lax.dot_general(xcb, dpib, _DIMS["tn"], preferred_element_type=F32)
    return dxc, (dba + _colsum(dpr), dbx + _colsum(dpi), dl + _colsum(dla * r))


def _lru_bwd(z, dy, gatew, pk, stages=()):
    s = z.shape[0]
    n_h, bk = gatew.shape[0], gatew.shape[1]
    d = n_h * bk
    t = _chunk(s)
    n_ch = s // t

    def body(u_ref, ug_ref, dy_ref, w_ref, pk_ref, du_ref, dug_ref, dw_ref, dpk_ref,
             upad, h0pad, h1pad, dxpad, abuf, bbuf, lbuf, dwacc, edge, xcbuf):
        for ref in (upad, h0pad, h1pad, dxpad):
            _zero_pads(ref, s)
        upad[pl.ds(PAD, s), :] = u_ref[...].astype(F32)
        dwacc[...] = jnp.zeros_like(dwacc)
        cw, cb, ba, bx, lam, big_l = _lru_params(pk_ref)
        zero = jnp.zeros((SUBLANES, bk), F32)
        zrow = jnp.zeros((1, bk), F32)
        rowi = lax.broadcasted_iota(jnp.int32, (t, bk), 0)

        def at(t0):
            return pl.ds(pl.multiple_of(PAD + t0, SUBLANES), t)

        def conv_in(t0):
            xc = xcbuf[pl.ds(t0, t), :]
            return xc, xc.astype(BF16)

        def dh_of(t0):
            ug = ug_ref[pl.ds(t0, t), :].astype(F32)
            gl, th = _gelu(ug)
            dyv = dy_ref[pl.ds(t0, t), :].astype(F32)
            return dyv * gl, dyv * _gelu_grad(ug, th)

        def sweep1(ch, carry):
            t0 = pl.multiple_of(ch * t, t)
            xc = _conv(_window(upad, t0, t), cw, cb, t)
            xcbuf[pl.ds(t0, t), :] = xc
            _, i, a, sq, _ = _gates(xc.astype(BF16), w_ref, 0, bk, ba, bx, big_l)
            abuf[...] = a
            bbuf[...] = sq * i * xc
            return _scan_chunk(abuf, bbuf, h0pad, PAD + t0, carry, t, False)

        lax.fori_loop(0, n_ch, sweep1, zero)

        edge[...] = zero

        def sweep2(ci, st):
            carry_h, carry_l, acc = st
            t0 = pl.multiple_of((n_ch - 1 - ci) * t, t)
            xc, xcb = conv_in(t0)
            _, i1, a1, sq1, _ = _gates(xcb, w_ref, 1, bk, ba, bx, big_l)
            abuf[...] = a1
            bbuf[...] = sq1 * i1 * xc
            carry_h = _scan_chunk(abuf, bbuf, h1pad, PAD + t0, carry_h, t, True)
            dh, dgl = dh_of(t0)
            dug_ref[pl.ds(t0, t), :] = (dgl * (h0pad[at(t0), :] + h1pad[at(t0), :])).astype(BF16)
            r0, i0, a0, sq0, rs0 = _gates(xcb, w_ref, 0, bk, ba, bx, big_l)
            abuf[...] = jnp.where(rowi == t - 1, edge[0:1, :], pltpu.roll(a0, t - 1, 0))
            bbuf[...] = dh
            carry_l = _scan_chunk(abuf, bbuf, lbuf, 0, carry_l, t, True)
            edge[...] = jnp.broadcast_to(a0[0:1, :], (SUBLANES, bk))
            hprev = _shift(_window(h0pad, t0, t), -1, t)
            dxc, acc = _lru_grads(lbuf[...], hprev, a0, sq0, rs0, r0, i0, xc, xcb, w_ref, dwacc, 0, big_l[0], acc)
            dxpad[at(t0), :] = dxc
            return carry_h, carry_l, acc

        _, _, acc0 = lax.fori_loop(0, n_ch, sweep2, (zero, zero, (zrow, zrow, zrow)))

        edge[...] = zero

        def sweep3(ch, st):
            carry_l, acc = st
            t0 = pl.multiple_of(ch * t, t)
            xc, xcb = conv_in(t0)
            r1, i1, a1, sq1, rs1 = _gates(xcb, w_ref, 1, bk, ba, bx, big_l)
            dh, _ = dh_of(t0)
            abuf[...] = jnp.where(rowi == 0, edge[0:1, :], pltpu.roll(a1, 1, 0))
            bbuf[...] = dh
            carry_l = _scan_chunk(abuf, bbuf, lbuf, 0, carry_l, t, False)
            edge[...] = jnp.broadcast_to(a1[t - 1:t, :], (SUBLANES, bk))
            hnext = _shift(_window(h1pad, t0, t), 1, t)
            dxc, acc = _lru_grads(lbuf[...], hnext, a1, sq1, rs1, r1, i1, xc, xcb, w_ref, dwacc, 1, big_l[1], acc)
            dxpad[at(t0), :] += dxc
            return carry_l, acc

        _, acc1 = lax.fori_loop(0, n_ch, sweep3, (zero, (zrow, zrow, zrow)))

        def sweep4(ch, st):
            t0 = pl.multiple_of(ch * t, t)
            sdx = _window(dxpad, t0, t)
            su = _window(upad, t0, t)
            dxc = _shift(sdx, 0, t)
            du = cw[0] * _shift(sdx, 2, t) + cw[1] * _shift(sdx, 1, t) + cw[2] * dxc + cw[3] * _shift(sdx, -1, t)
            du_ref[pl.ds(t0, t), :] = du.astype(BF16)
            return tuple(st[k] + _colsum(dxc * _shift(su, k - 2, t)) for k in range(4)) + (st[4] + _colsum(dxc),)

        conv_g = lax.fori_loop(0, n_ch, sweep4, (zrow,) * 5)

        dpk_ref[...] = jnp.zeros_like(dpk_ref)
        rows = list(conv_g) + [acc0[0], acc1[0], acc0[1], acc1[1],
                               acc0[2] * LRU_C * _sigmoid(-lam[0]), acc1[2] * LRU_C * _sigmoid(-lam[1])]
        for k, v in enumerate(rows):
            dpk_ref[pl.ds(k, 1), :] = v
        dw_ref[...] = dwacc[...].astype(BF16)

    u_spec, ug_spec, w_spec, pk_spec, blk = _lru_specs(s, d, bk)
    padded = pltpu.VMEM((s + 2 * PAD, bk), F32)
    chunk = pltpu.VMEM((t, bk), F32)
    res = _host_call(
        "lru_bwd", (n_h,), lambda ins, outs, scr: body(*ins, *outs, *scr), [z, z, dy, gatew, pk],
        [u_spec, ug_spec, blk, w_spec, pk_spec],
        [_sds((s, d), BF16), _sds((s, d), BF16), _sds((n_h, bk, 4 * bk), BF16), _sds((n_h, 16, bk), F32)],
        [blk, blk, w_spec, pk_spec],
        [padded, padded, padded, padded, chunk, chunk, chunk, pltpu.VMEM((bk, 4 * bk), F32),
         pltpu.VMEM((SUBLANES, bk), F32), pltpu.VMEM((s, bk), F32)], list(stages))
    return res if stages else res[0]


def _scalar(v):
    return jnp.reshape(v, (1,)).astype(jnp.int32)


def _add_sibling(g, r, c):
    _, rows, cols = g.shape
    rh = rows // 2
    tr = _tile(rh, 512, 16)
    nr = rh // tr

    def body(c_ref, g_ref, r_ref, o_ref):
        o_ref[...] = (g_ref[...].astype(F32) + r_ref[...].astype(F32)).astype(BF16)

    spec = pl.BlockSpec((None, tr, cols), lambda k, i, c_ref: (k, i, 0))
    return pl.pallas_call(
        body, name="add_sibling", out_shape=_sds((N_CHIP, rh, cols), BF16),
        grid_spec=pltpu.PrefetchScalarGridSpec(
            num_scalar_prefetch=1, grid=(N_CHIP, nr),
            in_specs=[pl.BlockSpec((None, tr, cols), lambda k, i, c_ref: (k, c_ref[0] * nr + i, 0)), spec], out_specs=spec),
        compiler_params=_cparams(("arbitrary", "arbitrary")),
    )(_scalar(c), g, r)


def _sum_chips(p, rcv, k_me, c):
    _, rh, cols = p.shape
    tr = _tile(rh, 512, 16)
    nr = rh // tr

    def body(kc_ref, p_ref, r_ref, o_ref):
        acc = p_ref[...].astype(F32)
        for j in range(3):
            acc = acc + r_ref[j].astype(F32)
        o_ref[...] = acc

    return pl.pallas_call(
        body, name="sum_chips", out_shape=_sds((2 * rh, cols), F32),
        grid_spec=pltpu.PrefetchScalarGridSpec(
            num_scalar_prefetch=1, grid=(nr,),
            in_specs=[pl.BlockSpec((None, tr, cols), lambda i, kc_ref: (kc_ref[0], i, 0)),
                      pl.BlockSpec((3, tr, cols), lambda i, kc_ref: (0, i, 0))],
            out_specs=pl.BlockSpec((tr, cols), lambda i, kc_ref: (kc_ref[1] * nr + i, 0))),
        compiler_params=_cparams(("arbitrary",)),
    )(jnp.stack([k_me, c]).astype(jnp.int32), p, rcv)


def _sum_devices(g):
    def body(g_ref, o_ref):
        acc = g_ref[0]
        for dev in range(1, N_DEV):
            acc = acc + g_ref[dev]
        o_ref[...] = acc

    return pl.pallas_call(body, name="sum_devices", out_shape=_sds(g.shape[1:], F32))(g)


def _adamw(w, g, m, v):
    rows, cols = w.shape
    tr = _tile(rows, 256, SUBLANES)

    def body(ins, outs, scr):
        w_ref, g_ref, m_ref, v_ref = ins
        go_ref, d_ref, nm_ref, nv_ref = outs
        gv = g_ref[...]
        go_ref[...] = gv
        nm = ADAM_B1 * m_ref[...] + (1.0 - ADAM_B1) * gv
        nv = ADAM_B2 * v_ref[...] + (1.0 - ADAM_B2) * (gv * gv)
        m_hat = nm / (1.0 - ADAM_B1 ** ADAM_STEP)
        v_hat = nv / (1.0 - ADAM_B2 ** ADAM_STEP)
        d_ref[...] = -ADAM_LR * (m_hat / (jnp.sqrt(v_hat) + ADAM_EPS) + ADAM_WD * w_ref[...])
        nm_ref[...] = nm
        nv_ref[...] = nv

    spec = pl.BlockSpec((tr, cols), lambda i: (i, 0))
    return _host_call("adamw", (rows // tr,), body, [w, g, m, v], [spec] * 4, [_sds((rows, cols), F32)] * 4, [spec] * 4, [], [])[0]


def _pack(vs, unit):
    flat = jnp.concatenate([v.reshape(-1).astype(F32) for v in vs])
    pad = (-flat.shape[0]) % unit
    if pad:
        flat = jnp.concatenate([flat, jnp.zeros((pad,), F32)])
    return flat.reshape(-1, 128)


def _unpack(p, like):
    flat = p.reshape(-1)
    out, off = [], 0
    for v in like:
        n = math.prod(v.shape)
        out.append(flat[off:off + n].reshape(v.shape))
        off += n
    return out


def kernel(x, w_in, pool_w, pool_scale, conv_w, conv_b, lru_wa, lru_ba, lru_wx, lru_bx, lru_lambda, w_pool_up, w_lru_up, w_out, b_out, ln1_g, ln1_b, w_ff1, b_ff1, w_ff2, b_ff2, ln2_g, ln2_b, loss_target, m_w_in, m_pool_w, m_pool_scale, m_conv_w, m_conv_b, m_lru_wa, m_lru_ba, m_lru_wx, m_lru_bx, m_lru_lambda, m_w_pool_up, m_w_lru_up, m_w_out, m_b_out, m_ln1_g, m_ln1_b, m_w_ff1, m_b_ff1, m_w_ff2, m_b_ff2, m_ln2_g, m_ln2_b, v_w_in, v_pool_w, v_pool_scale, v_conv_w, v_conv_b, v_lru_wa, v_lru_ba, v_lru_wx, v_lru_bx, v_lru_lambda, v_w_pool_up, v_w_lru_up, v_w_out, v_b_out, v_ln1_g, v_ln1_b, v_w_ff1, v_b_ff1, v_w_ff2, v_b_ff2, v_ln2_g, v_ln2_b):
    given = dict(locals())
    wt = {n: given[n] for n in WEIGHTS}
    mom = {n: given["m_" + n] for n in WEIGHTS}
    vel = {n: given["v_" + n] for n in WEIGHTS}

    ix, iy, ic = _mesh_pos()
    k_me = 2 * ix + iy
    s, d = x.shape[1], x.shape[2]
    ds = d // N_CHIP
    n_g, pgs, pg = pool_w.shape[1], pool_w.shape[2], pool_w.shape[3]
    n_h, bks, bk = lru_wa.shape[2], lru_wa.shape[3], lru_wa.shape[4]
    f = b_ff1.shape[1]
    x2 = x[0]
    x_bf = x2.astype(BF16)
    vec = lambda a: a.reshape(1, -1)

    sharded_vecs = [conv_w[0], lru_ba[0], lru_bx[0], lru_lambda[0]]
    rows_sv = jnp.concatenate(sharded_vecs + [jnp.zeros((6, ds), F32)], axis=0)
    sv = _all_gather_small(rows_sv)
    sv = sv.reshape(N_CHIP, 2, 16, ds)[:, 0].transpose(1, 0, 2).reshape(16, d)
    conv_w_f, ba_f, bx_f, lam_f = sv[0:4], sv[4:6], sv[6:8], sv[8:10]
    pk = jnp.concatenate([conv_w_f, conv_b, ba_f, bx_f, lam_f, jnp.zeros((5, d), F32)], axis=0)
    pk = pk.reshape(16, n_h, bk).transpose(1, 0, 2)

    def gate_stack(wa, wx):
        return jnp.stack([wa[0], wx[0]], axis=1)

    mats = {
        "w_in": w_in[0], "w_pool_up": w_pool_up[0], "w_lru_up": w_lru_up[0], "w_out": w_out[0],
        "w_ff1": w_ff1[0], "w_ff2": w_ff2[0],
        "pool_w": pool_w[0].reshape(n_g * pgs, pg),
        "gate_w": gate_stack(lru_wa, lru_wx).reshape(4 * n_h * bks, bk),
    }
    names = list(mats)
    placed = {n: _cast_place(mats[n], k_me) for n in names}

    def add_sibling(gs, swapped):
        return [_add_sibling(g, r, ic) for g, r in zip(gs, swapped)]

    def sum_chips(ps, received):
        return [_sum_chips(p, r, k_me, ic) for p, r in zip(ps, received)]

    first = ["w_in", "pool_w", "gate_w"]
    (bufs,) = _run_stages("gather_first_ici", [_gather_ici([placed[n] for n in first])])
    (bufs,) = _run_stages("gather_first_d2d", [_gather_d2d(bufs)])
    wg_in = bufs[0]
    wf_pool = bufs[1].reshape(N_CHIP, n_g, pgs, pg).transpose(1, 0, 2, 3).reshape(n_g, pg, pg)
    wf_gate = bufs[2].reshape(N_CHIP, 2, 2, n_h, bks, bk).transpose(3, 0, 4, 1, 2, 5).reshape(n_h, bk, 4 * bk)

    z, (bufs,) = _fwd_in(x_bf, wg_in, stages=[_gather_ici([placed[n] for n in ("w_pool_up", "w_lru_up", "w_out")])])
    (d_pool, y_pool), (bufs,) = _pool_fwd(z, wf_pool, pool_scale, stages=[_gather_d2d(bufs)])
    wf_pu, wf_lu, wf_out = (b.reshape(d, d) for b in bufs)
    y_lru, (bufs,) = _lru_fwd(z, wf_gate, pk, stages=[_gather_ici([placed["w_ff1"]])])
    (m_mix, p_a, p_b), (bufs,) = _fwd_merge(y_pool, y_lru, wf_pu, wf_lu, z, stages=[_gather_d2d(bufs)])
    wg_ff1 = bufs[0]
    xhat1, x1_bf, rstd1 = _fwd_out_ln1(m_mix, wf_out, x2, b_out, ln1_g, ln1_b)
    hdn, (bufs,) = _fwd_ff1(x1_bf, wg_ff1, b_ff1, stages=[_gather_ici([placed["w_ff2"]])])
    (bufs,) = _run_stages("gather_ff2_d2d", [_gather_d2d(bufs)])
    wf_ff2 = bufs[0].reshape(f, d)
    dr2, dr2_bf, g_ln2_g, g_ln2_b, g_b_ff2, loss_part = _fwd_ff2_ln2_loss(
        hdn, wf_ff2, xhat1, ln1_g, ln1_b, b_ff2, ln2_g, ln2_b, loss_target[0])

    dpre, g_b_ff1 = _bwd_ff2_in(dr2_bf, wf_ff2, hdn)
    g_ff = [_wgrad("wgrad_ff1", x1_bf, dpre, True), _wgrad("wgrad_ff2", hdn, dr2_bf, False)]
    (dr1, dr1_bf, g_ln1_g, g_ln1_b, g_b_out), (swapped,) = _bwd_ff1_in_ln1(
        dpre, wg_ff1, dr2, xhat1, rstd1, ln1_g, stages=[_swap_halves(g_ff)])
    sums_ff = add_sibling(g_ff, swapped)
    dp_a, dp_b, dg_a, dg_b = _bwd_out_in(dr1_bf, wf_out, z, p_a, p_b)
    dy_pool = _bwd_up_in("bwd_pool_up_in", dp_a, wf_pu)
    dy_lru = _bwd_up_in("bwd_lru_up_in", dp_b, wf_lu)
    g_mix = [_wgrad("wgrad_pool_up", y_pool, dp_a, False), _wgrad("wgrad_lru_up", y_lru, dp_b, False),
             _wgrad("wgrad_out", m_mix, dr1_bf, False)]
    (du_pool, g_pool_w, g_pool_scale), (swapped,) = _pool_bwd(
        d_pool, dy_pool, wf_pool, pool_scale, stages=[_swap_halves(g_mix)])
    sums_mix = add_sibling(g_mix, swapped)
    (du_lru, du_gate, g_gate_w, g_pk), (recv_ff, recv_mix) = _lru_bwd(
        z, dy_lru, wf_gate, pk, stages=[_scatter_chips(sums_ff), _scatter_chips(sums_mix)])
    halves = sum_chips(sums_ff + sums_mix, recv_ff + recv_mix)
    g_small = [g_pool_w.reshape(n_g, N_CHIP, pgs, pg).transpose(1, 0, 2, 3).reshape(N_CHIP, n_g * pgs, pg),
               g_gate_w.reshape(n_h, N_CHIP, bks, 2, 2, bk).transpose(1, 3, 4, 0, 2, 5).reshape(N_CHIP, 4 * n_h * bks, bk)]
    dz = jnp.concatenate([du_pool, du_lru, du_gate, dg_a, dg_b], axis=1)
    g_in, (joined, swapped) = _wgrad("wgrad_in", x_bf, dz, True, stages=[_join_halves(halves), _swap_halves(g_small)])
    g_mat = dict(zip(["w_ff1", "w_ff2", "w_pool_up", "w_lru_up", "w_out"], joined))
    sums_small = add_sibling(g_small, swapped)

    def stacked(tree):
        return gate_stack(tree["lru_wa"], tree["lru_wx"]).reshape(4 * n_h * bks, bk)

    res = {}

    def update(n):
        if n == "gate_w":
            outs = [o.reshape(2, 2, n_h, bks, bk) for o in _adamw(stacked(wt), g_mat[n], stacked(mom), stacked(vel))]
            res["lru_wa"] = [o[:, 0][None] for o in outs]
            res["lru_wx"] = [o[:, 1][None] for o in outs]
        else:
            shp2 = mats[n].shape
            outs = _adamw(wt[n].reshape(shp2), g_mat[n], mom[n].reshape(shp2), vel[n].reshape(shp2))
            res[n] = [o.reshape(wt[n].shape) for o in outs]

    (swapped,) = _run_stages("swap_in", [_swap_halves([g_in])])
    sums_in = add_sibling([g_in], swapped)
    grad_x, (recv_small, recv_in) = _bwd_in(dz, wg_in, dr1, stages=[_scatter_chips(sums_small), _scatter_chips(sums_in)])
    halves = sum_chips(sums_small + sums_in, recv_small + recv_in)
    (joined,) = _run_stages("join_last", [_join_halves(halves)])
    g_mat.update(zip(["pool_w", "gate_w", "w_in"], joined))
    for n in names:
        update(n)

    g_pk = g_pk.transpose(1, 0, 2).reshape(16, d)
    vec_full = {
        "pool_scale": g_pool_scale, "conv_w": g_pk[0:4], "conv_b": g_pk[4:5],
        "lru_ba": g_pk[5:7], "lru_bx": g_pk[7:9], "lru_lambda": g_pk[9:11],
        "b_out": g_b_out, "ln1_g": g_ln1_g, "ln1_b": g_ln1_b, "b_ff1": g_b_ff1, "b_ff2": g_b_ff2,
        "ln2_g": g_ln2_g, "ln2_b": g_ln2_b,
    }
    vnames = list(vec_full)
    vg = _sum_devices(_all_gather_small(_pack([vec_full[n] for n in vnames], 1024)))
    vg = dict(zip(vnames, _unpack(vg, [vec_full[n] for n in vnames])))
    for n in ("conv_w", "lru_ba", "lru_bx", "lru_lambda"):
        vg[n] = lax.dynamic_slice_in_dim(vg[n], k_me * ds, ds, axis=1)
    vg = {n: vg[n].reshape(wt[n].shape) for n in vnames}
    upd = _adamw(_pack([wt[n] for n in vnames], 1024), _pack([vg[n] for n in vnames], 1024),
                 _pack([mom[n] for n in vnames], 1024), _pack([vel[n] for n in vnames], 1024))
    upd = [_unpack(u, [wt[n] for n in vnames]) for u in upd]
    for i, n in enumerate(vnames):
        res[n] = [vg[n], upd[1][i], upd[2][i], upd[3][i]]

    loss = lax.psum(loss_part[0, 0], ("x", "y", "c"))
    return (loss, grad_x[None], *[res[n][0] for n in WEIGHTS], *[res[n][1] for n in WEIGHTS],
            *[res[n][2] for n in WEIGHTS], *[res[n][3] for n in WEIGHTS])
```

```python
import functools
import math

import jax
import jax.numpy as jnp
from jax import lax
from jax.experimental import pallas as pl
from jax.experimental.pallas import tpu as pltpu

F32 = jnp.float32
BF16 = jnp.bfloat16
MESH = pl.DeviceIdType.MESH
ANY = pl.BlockSpec(memory_space=pl.ANY)

N_CHIP = 4
N_DEV = 8
VMEM_LIMIT_BYTES = 56 * 1024 * 1024
SUBLANES = 8
PAD = 8
SCAN_UNROLL = 8

POOL_WINDOWS = (2, 4, 8, 16)
LRU_C = 8.0
DN_ALPHA = 2.0 ** 0.25
LN_EPS = 1e-5
ADAM_LR, ADAM_B1, ADAM_B2, ADAM_EPS, ADAM_WD, ADAM_STEP = 0.001, 0.9, 0.999, 1e-08, 0.01, 10

WEIGHTS = ("w_in", "pool_w", "pool_scale", "conv_w", "conv_b", "lru_wa", "lru_ba", "lru_wx", "lru_bx", "lru_lambda",
           "w_pool_up", "w_lru_up", "w_out", "b_out", "ln1_g", "ln1_b", "w_ff1", "b_ff1", "w_ff2", "b_ff2", "ln2_g", "ln2_b")


def _cparams(sem=None):
    return pltpu.CompilerParams(dimension_semantics=sem, vmem_limit_bytes=VMEM_LIMIT_BYTES)


def _tile(dim, pref, unit=128):
    if dim <= pref:
        return dim
    t = (pref // unit) * unit
    while t > unit and dim % t:
        t -= unit
    assert dim % t == 0, (dim, pref)
    return t


def _mesh_pos():
    x, y, c = lax.axis_index("x"), lax.axis_index("y"), lax.axis_index("c")
    return x, y, c


def _other_chips(x, y):
    return [(1 - x, y), (x, 1 - y), (1 - x, 1 - y)]


def _all_gather_small(v):
    m_per, n = v.shape

    def body(x_ref, out_ref, send_sems, recv_sems, local_sem):
        x, y, c = _mesh_pos()
        me, sibling = (x, y, c), (x, y, 1 - c)
        chips = _other_chips(x, y)

        def rows(px, py, pc):
            return out_ref.at[4 * px + 2 * py + pc]

        def copy(k, block, to, src=None):
            return pltpu.make_async_remote_copy(
                src_ref=rows(*block) if src is None else src, dst_ref=rows(*block),
                send_sem=send_sems.at[k], recv_sem=recv_sems.at[k], device_id=to, device_id_type=MESH)

        mine = pltpu.make_async_copy(x_ref, rows(*me), local_sem)
        mine.start()
        first = [copy(0, me, sibling, src=x_ref)]
        first += [copy(1 + j, me, (*chip, c), src=x_ref) for j, chip in enumerate(chips)]
        for cp in first:
            cp.start()
        passed = [copy(4 + j, (*chip, c), sibling) for j, chip in enumerate(chips)]
        for j, chip in enumerate(chips):
            copy(1 + j, (*chip, c), me).wait_recv()
            passed[j].start()
        copy(0, sibling, me).wait_recv()
        for j, chip in enumerate(chips):
            copy(4 + j, (*chip, 1 - c), me).wait_recv()
        for cp in first + passed:
            cp.wait_send()
        mine.wait()

    return pl.pallas_call(
        body, name="all_gather_small",
        out_shape=jax.ShapeDtypeStruct((N_DEV, m_per, n), v.dtype),
        in_specs=[pl.BlockSpec(memory_space=pltpu.VMEM)],
        out_specs=pl.BlockSpec(memory_space=pltpu.VMEM),
        scratch_shapes=[pltpu.SemaphoreType.DMA((7,)), pltpu.SemaphoreType.DMA((7,)), pltpu.SemaphoreType.DMA],
    )(v)


class _Stage:
    def __init__(self, srcs, bufs, news, n_sems, copies):
        self.srcs, self.bufs, self.news, self.n_sems, self.copies = list(srcs), list(bufs), list(news), n_sems, copies
        self.phases = [(copies, 0)]


class _SemsFrom:
    def __init__(self, ref, offset):
        self.ref, self.offset, self.at = ref, offset, self

    def __getitem__(self, s):
        return self.ref.at[self.offset + s]


def _chain(stages):
    chained = _Stage([], stages[0].bufs, [], sum(st.n_sems for st in stages), None)
    chained.phases, first = [], 0
    for st in stages:
        chained.phases.append((st.copies, first))
        first += st.n_sems
    return chained


def _remote(src, dst, send_sems, recv_sems, s, to):
    return pltpu.make_async_remote_copy(src_ref=src, dst_ref=dst, send_sem=send_sems.at[s], recv_sem=recv_sems.at[s],
                                        device_id=to, device_id_type=MESH)


def _stage_operands(stages, n_in, n_out):
    ins, outs, aliases, scratch = [], [], {}, []
    for st in stages:
        for i in range(len(st.bufs)):
            aliases[n_in + len(ins) + len(st.srcs) + i] = n_out + len(outs) + i
        ins += st.srcs + st.bufs
        outs += [jax.ShapeDtypeStruct(b.shape, b.dtype) for b in st.bufs] + st.news
        scratch += [pltpu.SemaphoreType.DMA((st.n_sems,)), pltpu.SemaphoreType.DMA((st.n_sems,))]
    return ins, outs, aliases, scratch


def _stage_refs(stages, in_refs, out_refs, sem_refs):
    parts, i, o = [], 0, 0
    for n, st in enumerate(stages):
        src = in_refs[i:i + len(st.srcs)]
        i += len(st.srcs) + len(st.bufs)
        buf = out_refs[o:o + len(st.bufs)]
        new = out_refs[o + len(st.bufs):o + len(st.bufs) + len(st.news)]
        o += len(st.bufs) + len(st.news)
        parts.append((src, buf, new, sem_refs[2 * n], sem_refs[2 * n + 1]))
    return parts


def _stage_results(stages, res):
    out, o = [], 0
    for st in stages:
        n = len(st.bufs) + len(st.news)
        out.append(list(res[o:o + n]))
        o += n
    return out


def _stages_start(stages, parts):
    for st, part in zip(stages, parts):
        for cp in st.copies(*part)[0]:
            cp.start()


def _stages_wait(stages, parts):
    for st, part in zip(stages, parts):
        started, landing = st.copies(*part)
        for cp in landing:
            cp.wait_recv()
        for cp in started:
            cp.wait_send()


def _run_stages(name, stages):
    ins, outs, aliases, scratch = _stage_operands(stages, 0, 0)

    def body(*refs):
        parts = _stage_refs(stages, refs[:len(ins)], refs[len(ins):len(ins) + len(outs)], refs[len(ins) + len(outs):])
        for st, (src, buf, new, send_sems, recv_sems) in zip(stages, parts):
            for copies, first in st.phases:
                started, landing = copies(src, buf, new, _SemsFrom(send_sems, first), _SemsFrom(recv_sems, first))
                for cp in started:
                    cp.start()
                for cp in landing:
                    cp.wait_recv()
                for cp in started:
                    cp.wait_send()

    res = pl.pallas_call(
        body, name=name, out_shape=outs, in_specs=[ANY] * len(ins), out_specs=[ANY] * len(outs),
        input_output_aliases=aliases, scratch_shapes=scratch)(*ins)
    return _stage_results(stages, res)


def _gather_direct(ts):
    def copies(src, buf, new, send_sems, recv_sems):
        x, y, c = _mesh_pos()
        started, landing = [], []
        for t in range(len(ts)):
            rh = ts[t].shape[1] // 2
            rows = pl.ds(c * rh, rh)
            mine = buf[t].at[2 * x + y, rows]
            for j, chip in enumerate(_other_chips(x, y)[:2]):
                theirs = buf[t].at[2 * chip[0] + chip[1], rows]
                started.append(_remote(mine, mine, send_sems, recv_sems, 2 * t + j, (*chip, c)))
                landing.append(_remote(theirs, theirs, send_sems, recv_sems, 2 * t + j, (x, y, c)))
        return started, landing

    return _Stage([], ts, [], 2 * len(ts), copies)


def _gather_relay(ts):
    def copies(src, buf, new, send_sems, recv_sems):
        x, y, c = _mesh_pos()
        (x_nb, y_nb, diag) = _other_chips(x, y)
        block = lambda chip: 2 * chip[0] + chip[1]
        started, landing = [], []
        for t in range(len(ts)):
            rq = ts[t].shape[1] // 4
            q0, q1 = pl.ds(2 * c * rq, rq), pl.ds((2 * c + 1) * rq, rq)
            from_y, from_x = buf[t].at[block(y_nb), q0], buf[t].at[block(x_nb), q1]
            started.append(_remote(from_y, from_y, send_sems, recv_sems, 2 * t, (*x_nb, c)))
            started.append(_remote(from_x, from_x, send_sems, recv_sems, 2 * t + 1, (*y_nb, c)))
            for j, q in enumerate((q0, q1)):
                lands = buf[t].at[block(diag), q]
                landing.append(_remote(lands, lands, send_sems, recv_sems, 2 * t + j, (x, y, c)))
        return started, landing

    return _Stage([], ts, [], 2 * len(ts), copies)


def _gather_d2d(ts):
    def copies(src, buf, new, send_sems, recv_sems):
        x, y, c = _mesh_pos()
        started, landing = [], []
        for t in range(len(ts)):
            rh = ts[t].shape[1] // 2
            for j, chip in enumerate(_other_chips(x, y)):
                got = buf[t].at[2 * chip[0] + chip[1], pl.ds(c * rh, rh)]
                other = buf[t].at[2 * chip[0] + chip[1], pl.ds((1 - c) * rh, rh)]
                started.append(_remote(got, got, send_sems, recv_sems, 3 * t + j, (x, y, 1 - c)))
                landing.append(_remote(other, other, send_sems, recv_sems, 3 * t + j, (x, y, c)))
        return started, landing

    return _Stage([], ts, [], 3 * len(ts), copies)


def _swap_halves(gs):
    def copies(src, buf, new, send_sems, recv_sems):
        x, y, c = _mesh_pos()
        started, landing = [], []
        for t in range(len(gs)):
            rh = gs[t].shape[1] // 2
            started.append(_remote(src[t].at[:, pl.ds((1 - c) * rh, rh)], new[t], send_sems, recv_sems, t, (x, y, 1 - c)))
            landing.append(_remote(new[t], new[t], send_sems, recv_sems, t, (x, y, c)))
        return started, landing

    news = [jax.ShapeDtypeStruct((g.shape[0], g.shape[1] // 2, g.shape[2]), g.dtype) for g in gs]
    return _Stage(gs, [], news, len(gs), copies)


def _scatter_chips(ps):
    def copies(src, buf, new, send_sems, recv_sems):
        x, y, c = _mesh_pos()
        started, landing = [], []
        for t in range(len(ps)):
            for j, chip in enumerate(_other_chips(x, y)):
                started.append(_remote(src[t].at[2 * chip[0] + chip[1]], new[t].at[j], send_sems, recv_sems, 3 * t + j, (*chip, c)))
                landing.append(_remote(new[t].at[j], new[t].at[j], send_sems, recv_sems, 3 * t + j, (x, y, c)))
        return started, landing

    return _Stage(ps, [], [jax.ShapeDtypeStruct((3,) + p.shape[1:], p.dtype) for p in ps], 3 * len(ps), copies)


def _join_halves(fs):
    def copies(src, buf, new, send_sems, recv_sems):
        x, y, c = _mesh_pos()
        started, landing = [], []
        for t in range(len(fs)):
            rh = fs[t].shape[0] // 2
            mine = buf[t].at[pl.ds(c * rh, rh)]
            theirs = buf[t].at[pl.ds((1 - c) * rh, rh)]
            started.append(_remote(mine, mine, send_sems, recv_sems, t, (x, y, 1 - c)))
            landing.append(_remote(theirs, theirs, send_sems, recv_sems, t, (x, y, c)))
        return started, landing

    return _Stage([], fs, [], len(fs), copies)


def _cast_place(w, k_me):
    rows, cols = w.shape
    tr = _tile(rows, 512, 16)

    def body(k_ref, w_ref, o_ref):
        o_ref[...] = w_ref[...].astype(BF16)

    return pl.pallas_call(
        body, name="cast_place", out_shape=_sds((N_CHIP, rows, cols), BF16),
        grid_spec=pltpu.PrefetchScalarGridSpec(
            num_scalar_prefetch=1, grid=(rows // tr,),
            in_specs=[pl.BlockSpec((tr, cols), lambda i, k_ref: (i, 0))],
            out_specs=pl.BlockSpec((None, tr, cols), lambda i, k_ref: (k_ref[0], i, 0))),
        compiler_params=_cparams(("arbitrary",)),
    )(_scalar(k_me), w)


_DIMS = {"nn": (((1,), (0,)), ((), ())), "nt": (((1,), (1,)), ((), ())), "tn": (((0,), (0,)), ((), ()))}


def _accum(ref, val, first):
    @pl.when(first)
    def _():
        ref[...] = val

    @pl.when(jnp.logical_not(first))
    def _():
        ref[...] += val


def _grid_edges(grid):
    ids = [pl.program_id(ax) for ax in range(len(grid))]
    first = functools.reduce(jnp.logical_and, [i == 0 for i in ids])
    last = functools.reduce(jnp.logical_and, [i == n - 1 for i, n in zip(ids, grid)])
    return first, last


def _host_call(name, grid, body, operands, in_specs, out_shape, out_specs, scratch, stages):
    s_ins, s_outs, aliases, s_scratch = _stage_operands(stages, len(operands), len(out_shape))
    n_in, n_out, n_scr = len(operands), len(out_shape), len(scratch)

    def full_body(*refs):
        in_refs = refs[:n_in]
        s_in_refs = refs[n_in:n_in + len(s_ins)]
        o0 = n_in + len(s_ins)
        out_refs = refs[o0:o0 + n_out]
        s_out_refs = refs[o0 + n_out:o0 + n_out + len(s_outs)]
        c0 = o0 + n_out + len(s_outs)
        scr_refs = refs[c0:c0 + n_scr]
        if stages:
            parts = _stage_refs(stages, s_in_refs, s_out_refs, refs[c0 + n_scr:])
            first, last = _grid_edges(grid)
            pl.when(first)(lambda: _stages_start(stages, parts))
        body(in_refs, out_refs, scr_refs)
        if stages:
            pl.when(last)(lambda: _stages_wait(stages, parts))

    res = pl.pallas_call(
        full_body, name=name, grid=grid, in_specs=list(in_specs) + [ANY] * len(s_ins),
        out_specs=list(out_specs) + [ANY] * len(s_outs), out_shape=list(out_shape) + s_outs,
        input_output_aliases=aliases, scratch_shapes=list(scratch) + s_scratch,
        compiler_params=_cparams(("arbitrary",) * len(grid)),
    )(*operands, *s_ins)
    return list(res[:n_out]), _stage_results(stages, res[n_out:])


def _matmul(name, grid, pairs, extras, outs, acc_shape, epilogue, stages=()):
    n_p = len(pairs)
    n_k = grid[-1]
    dims = [_DIMS[p[4]] for p in pairs]

    def body(in_refs, out, accs):
        ab, ex = in_refs[:2 * n_p], in_refs[2 * n_p:]
        ids = [pl.program_id(ax) for ax in range(len(grid))]
        k = ids[-1]

        @pl.when(k == 0)
        def _():
            for acc in accs:
                acc[...] = jnp.zeros_like(acc)

        for p in range(n_p):
            a = ab[2 * p][...].astype(BF16)
            b = ab[2 * p + 1][...].astype(BF16)
            accs[p][...] += lax.dot_general(a, b, dims[p], preferred_element_type=F32)

        @pl.when(k == n_k - 1)
        def _():
            epilogue([acc[...] for acc in accs], ex, out, ids)

    in_specs = []
    operands = []
    for a, a_spec, b, b_spec, _ in pairs:
        in_specs += [a_spec, b_spec]
        operands += [a, b]
    for e, e_spec in extras:
        in_specs.append(e_spec)
        operands.append(e)
    res, stage_res = _host_call(name, grid, body, operands, in_specs, [o[0] for o in outs], [o[1] for o in outs],
                                [pltpu.VMEM(acc_shape, F32) for _ in pairs], list(stages))
    return (res, stage_res) if stages else res


def _out(res, stages, single=False):
    outs = res[0] if stages else res
    outs = outs[0] if single else outs
    return (outs, res[1]) if stages else outs


def _sds(shape, dtype):
    return jax.ShapeDtypeStruct(shape, dtype)


def _row(n):
    return pl.BlockSpec((1, n), lambda *_: (0, 0))


def _layer_norm(r):
    mu = jnp.mean(r, axis=-1, keepdims=True)
    xc = r - mu
    var = jnp.mean(xc * xc, axis=-1, keepdims=True)
    rstd = lax.rsqrt(var + LN_EPS)
    return xc * rstd, rstd


def _layer_norm_bwd(dxhat, xhat, rstd):
    m1 = jnp.mean(dxhat, axis=-1, keepdims=True)
    m2 = jnp.mean(dxhat * xhat, axis=-1, keepdims=True)
    return rstd * (dxhat - m1 - xhat * m2)


def _colsum(v):
    return jnp.sum(v, axis=0, keepdims=True)


def _fwd_in(x_bf, wg_in, stages=()):
    s, d = x_bf.shape
    inc = wg_in.shape[2]
    tm, tn, tk = _tile(s, 1024), _tile(inc, 1280), _tile(d, 2048)
    nb = inc // tn

    def epi(accs, ex, out, ids):
        out[0][...] = accs[0].astype(BF16)

    return _out(_matmul(
        "fwd_in", (s // tm, N_CHIP * nb, d // tk),
        [(x_bf, pl.BlockSpec((tm, tk), lambda i, j, k: (i, k)),
          wg_in, pl.BlockSpec((None, tk, tn), lambda i, j, k: (j // nb, k, j % nb)), "nn")],
        [], [(_sds((s, N_CHIP * inc), BF16), pl.BlockSpec((tm, tn), lambda i, j, k: (i, j)))],
        (tm, tn), epi, stages), stages, True)


def _fwd_merge(y_pool, y_lru, w_pu, w_lu, z, stages=()):
    s, d = y_pool.shape
    tm, tn, tk = _tile(s, 1024), _tile(d, 1024), _tile(d, 1024)
    ga0, gb0 = 3 * d // tn, 4 * d // tn

    def epi(accs, ex, out, ids):
        sa = _sigmoid(ex[0][...].astype(F32))
        sb = _sigmoid(ex[1][...].astype(F32))
        out[0][...] = (sa * accs[0] + sb * accs[1]).astype(BF16)
        out[1][...] = accs[0].astype(BF16)
        out[2][...] = accs[1].astype(BF16)

    a_spec = pl.BlockSpec((tm, tk), lambda i, j, k: (i, k))
    b_spec = pl.BlockSpec((tk, tn), lambda i, j, k: (k, j))
    o_spec = pl.BlockSpec((tm, tn), lambda i, j, k: (i, j))
    return _out(_matmul(
        "fwd_merge", (s // tm, d // tn, d // tk),
        [(y_pool, a_spec, w_pu, b_spec, "nn"), (y_lru, a_spec, w_lu, b_spec, "nn")],
        [(z, pl.BlockSpec((tm, tn), lambda i, j, k: (i, ga0 + j))), (z, pl.BlockSpec((tm, tn), lambda i, j, k: (i, gb0 + j)))],
        [(_sds((s, d), BF16), o_spec)] * 3, (tm, tn), epi, stages), stages)


def _fwd_out_ln1(m, w_out, x, b_out, g1, b1, stages=()):
    s, d = x.shape
    tm, tk = _tile(s, 512), _tile(d, 2048)

    def epi(accs, ex, out, ids):
        r = DN_ALPHA * ex[0][...] + accs[0] + ex[1][...]
        xhat, rstd = _layer_norm(r)
        out[0][...] = xhat
        out[1][...] = (xhat * ex[2][...] + ex[3][...]).astype(BF16)
        out[2][...] = rstd

    full = pl.BlockSpec((tm, d), lambda i, j, k: (i, 0))
    return _out(_matmul(
        "fwd_out_ln1", (s // tm, 1, d // tk),
        [(m, pl.BlockSpec((tm, tk), lambda i, j, k: (i, k)), w_out, pl.BlockSpec((tk, d), lambda i, j, k: (k, 0)), "nn")],
        [(x, full), (b_out, _row(d)), (g1, _row(d)), (b1, _row(d))],
        [(_sds((s, d), F32), full), (_sds((s, d), BF16), full), (_sds((s, 1), F32), pl.BlockSpec((tm, 1), lambda i, j, k: (i, 0)))],
        (tm, d), epi, stages), stages)


def _fwd_ff1(x1_bf, wg_ff1, b_ff1, stages=()):
    s, d = x1_bf.shape
    fc = wg_ff1.shape[2]
    tm, tn, tk = _tile(s, 1024), _tile(fc, 1024), _tile(d, 2048)
    nb = fc // tn

    def epi(accs, ex, out, ids):
        p = jnp.maximum(accs[0] + ex[0][...], 0.0)
        out[0][...] = (p * p).astype(BF16)

    return _out(_matmul(
        "fwd_ff1", (s // tm, N_CHIP * nb, d // tk),
        [(x1_bf, pl.BlockSpec((tm, tk), lambda i, j, k: (i, k)),
          wg_ff1, pl.BlockSpec((None, tk, tn), lambda i, j, k: (j // nb, k, j % nb)), "nn")],
        [(b_ff1, pl.BlockSpec((1, tn), lambda i, j, k: (0, j)))],
        [(_sds((s, N_CHIP * fc), BF16), pl.BlockSpec((tm, tn), lambda i, j, k: (i, j)))],
        (tm, tn), epi, stages), stages, True)


def _fwd_ff2_ln2_loss(hdn, w_ff2, xhat1, g1, b1, b_ff2, g2, b2, target, stages=()):
    s, f = hdn.shape
    d = xhat1.shape[1]
    tm, tk = _tile(s, 512), _tile(f, 1024)

    def epi(accs, ex, out, ids):
        first = ids[0] == 0
        x1 = ex[0][...] * ex[1][...] + ex[2][...]
        r = DN_ALPHA * x1 + accs[0] + ex[3][...]
        xhat, rstd = _layer_norm(r)
        g2v = ex[4][...]
        err = xhat * g2v + ex[5][...] - ex[6][...]
        part = 0.5 * jnp.sum(jnp.mean(err * err, axis=-1, keepdims=True), axis=0, keepdims=True)
        dy = err * (1.0 / d)
        dr2 = _layer_norm_bwd(dy * g2v, xhat, rstd)
        out[0][...] = dr2
        out[1][...] = dr2.astype(BF16)
        _accum(out[2], _colsum(dy * xhat), first)
        _accum(out[3], _colsum(dy), first)
        _accum(out[4], _colsum(dr2), first)
        _accum(out[5], jnp.broadcast_to(part, (1, 128)), first)

    full = pl.BlockSpec((tm, d), lambda i, j, k: (i, 0))
    return _out(_matmul(
        "fwd_ff2_ln2_loss", (s // tm, 1, f // tk),
        [(hdn, pl.BlockSpec((tm, tk), lambda i, j, k: (i, k)), w_ff2, pl.BlockSpec((tk, d), lambda i, j, k: (k, 0)), "nn")],
        [(xhat1, full), (g1, _row(d)), (b1, _row(d)), (b_ff2, _row(d)), (g2, _row(d)), (b2, _row(d)), (target, full)],
        [(_sds((s, d), F32), full), (_sds((s, d), BF16), full), (_sds((1, d), F32), _row(d)), (_sds((1, d), F32), _row(d)),
         (_sds((1, d), F32), _row(d)), (_sds((1, 128), F32), _row(128))],
        (tm, d), epi, stages), stages)


def _bwd_ff2_in(dr2_bf, w_ff2, hdn, stages=()):
    s, d = dr2_bf.shape
    f = hdn.shape[1]
    tm, tn, tk = _tile(s, 1024), _tile(f, 1024), _tile(d, 2048)

    def epi(accs, ex, out, ids):
        dpre = accs[0] * (2.0 * jnp.sqrt(ex[0][...].astype(F32)))
        out[0][...] = dpre.astype(BF16)
        _accum(out[1], _colsum(dpre), ids[1] == 0)

    return _out(_matmul(
        "bwd_ff2_in", (f // tn, s // tm, d // tk),
        [(dr2_bf, pl.BlockSpec((tm, tk), lambda j, i, k: (i, k)), w_ff2, pl.BlockSpec((tn, tk), lambda j, i, k: (j, k)), "nt")],
        [(hdn, pl.BlockSpec((tm, tn), lambda j, i, k: (i, j)))],
        [(_sds((s, f), BF16), pl.BlockSpec((tm, tn), lambda j, i, k: (i, j))), (_sds((1, f), F32), pl.BlockSpec((1, tn), lambda j, i, k: (0, j)))],
        (tm, tn), epi, stages), stages)


def _bwd_ff1_in_ln1(dpre, wg_ff1, dr2, xhat1, rstd1, g1, stages=()):
    s, f = dpre.shape
    d = xhat1.shape[1]
    fc = wg_ff1.shape[2]
    tm, tk = _tile(s, 512), _tile(fc, 1024)
    nb = fc // tk

    def epi(accs, ex, out, ids):
        first = ids[0] == 0
        xhat = ex[1][...]
        dx1 = accs[0] + DN_ALPHA * ex[0][...]
        dr1 = _layer_norm_bwd(dx1 * ex[3][...], xhat, ex[2][...])
        out[0][...] = dr1
        out[1][...] = dr1.astype(BF16)
        _accum(out[2], _colsum(dx1 * xhat), first)
        _accum(out[3], _colsum(dx1), first)
        _accum(out[4], _colsum(dr1), first)

    full = pl.BlockSpec((tm, d), lambda i, j, k: (i, 0))
    return _out(_matmul(
        "bwd_ff1_in_ln1", (s // tm, 1, f // tk),
        [(dpre, pl.BlockSpec((tm, tk), lambda i, j, k: (i, k)),
          wg_ff1, pl.BlockSpec((None, d, tk), lambda i, j, k: (k // nb, 0, k % nb)), "nt")],
        [(dr2, full), (xhat1, full), (rstd1, pl.BlockSpec((tm, 1), lambda i, j, k: (i, 0))), (g1, _row(d))],
        [(_sds((s, d), F32), full), (_sds((s, d), BF16), full), (_sds((1, d), F32), _row(d)), (_sds((1, d), F32), _row(d)),
         (_sds((1, d), F32), _row(d))],
        (tm, d), epi, stages), stages)


def _bwd_out_in(dr1_bf, w_out, z, pa, pb, stages=()):
    s, d = dr1_bf.shape
    tm, tn, tk = _tile(s, 1024), _tile(d, 1024), _tile(d, 2048)
    ga0, gb0 = 3 * d // tn, 4 * d // tn

    def epi(accs, ex, out, ids):
        dm = accs[0]
        sa = _sigmoid(ex[0][...].astype(F32))
        sb = _sigmoid(ex[1][...].astype(F32))
        out[0][...] = (dm * sa).astype(BF16)
        out[1][...] = (dm * sb).astype(BF16)
        out[2][...] = (dm * ex[2][...].astype(F32) * sa * (1.0 - sa)).astype(BF16)
        out[3][...] = (dm * ex[3][...].astype(F32) * sb * (1.0 - sb)).astype(BF16)

    o_spec = pl.BlockSpec((tm, tn), lambda i, j, k: (i, j))
    return _out(_matmul(
        "bwd_out_in", (s // tm, d // tn, d // tk),
        [(dr1_bf, pl.BlockSpec((tm, tk), lambda i, j, k: (i, k)), w_out, pl.BlockSpec((tn, tk), lambda i, j, k: (j, k)), "nt")],
        [(z, pl.BlockSpec((tm, tn), lambda i, j, k: (i, ga0 + j))), (z, pl.BlockSpec((tm, tn), lambda i, j, k: (i, gb0 + j))),
         (pa, o_spec), (pb, o_spec)],
        [(_sds((s, d), BF16), o_spec)] * 4, (tm, tn), epi, stages), stages)


def _bwd_up_in(name, dp, w_up, stages=()):
    s, d = dp.shape
    n = w_up.shape[0]
    tm, tn, tk = _tile(s, 1024), _tile(n, 1024), _tile(d, 2048)

    def epi(accs, ex, out, ids):
        out[0][...] = accs[0].astype(BF16)

    return _out(_matmul(
        name, (s // tm, n // tn, d // tk),
        [(dp, pl.BlockSpec((tm, tk), lambda i, j, k: (i, k)), w_up, pl.BlockSpec((tn, tk), lambda i, j, k: (j, k)), "nt")],
        [], [(_sds((s, n), BF16), pl.BlockSpec((tm, tn), lambda i, j, k: (i, j)))], (tm, tn), epi, stages), stages, True)


def _bwd_in(dz, wg_in, dr1, stages=()):
    s, d = dr1.shape
    inc = wg_in.shape[2]
    tm, tn, tk = _tile(s, 1024), _tile(d, 1024), _tile(inc, 1280)
    nb = inc // tk

    def epi(accs, ex, out, ids):
        out[0][...] = accs[0] + DN_ALPHA * ex[0][...]

    o_spec = pl.BlockSpec((tm, tn), lambda i, j, k: (i, j))
    return _out(_matmul(
        "bwd_in", (s // tm, d // tn, N_CHIP * nb),
        [(dz, pl.BlockSpec((tm, tk), lambda i, j, k: (i, k)),
          wg_in, pl.BlockSpec((None, tn, tk), lambda i, j, k: (k // nb, j, k % nb)), "nt")],
        [(dr1, o_spec)], [(_sds((s, d), F32), o_spec)], (tm, tn), epi, stages), stages, True)


def _wgrad(name, a, b, col_sharded, stages=()):
    s, ka = a.shape
    n = b.shape[1]
    tm, tk = _tile(ka, 1024), _tile(s, 1024)
    tn = _tile(n // N_CHIP, 1280) if col_sharded else _tile(n, 1024)

    def epi(accs, ex, out, ids):
        out[0][...] = accs[0].astype(BF16)

    if col_sharded:
        nb = (n // N_CHIP) // tn
        o = (_sds((N_CHIP, ka, n // N_CHIP), BF16), pl.BlockSpec((None, tm, tn), lambda i, j, k: (j // nb, i, j % nb)))
    else:
        o = (_sds((ka, n), BF16), pl.BlockSpec((tm, tn), lambda i, j, k: (i, j)))
    res = _out(_matmul(
        name, (ka // tm, n // tn, s // tk),
        [(a, pl.BlockSpec((tk, tm), lambda i, j, k: (k, i)), b, pl.BlockSpec((tk, tn), lambda i, j, k: (k, j)), "tn")],
        [], [o], (tm, tn), epi, stages), stages, True)
    res, stage_res = res if stages else (res, None)
    res = res if col_sharded else res.reshape(N_CHIP, ka // N_CHIP, n)
    return (res, stage_res) if stages else res


def _chunk(s):
    return _tile(s, 512, SUBLANES)


def _zero_pads(ref, s):
    zeros = jnp.zeros((PAD, ref.shape[1]), F32)
    ref[pl.ds(0, PAD), :] = zeros
    ref[pl.ds(PAD + s, PAD), :] = zeros


def _window(ref, t0, t):
    return ref[pl.ds(t0, t + 2 * PAD), :]


def _shift(sup, off, t):
    return sup[PAD + off:PAD + off + t, :]


def _pool_count(t0, t, s, w):
    pos = t0 + lax.broadcasted_iota(jnp.int32, (t, 1), 0)
    return (jnp.minimum(pos + w // 2, s) - jnp.maximum(pos - w // 2, 0)).astype(F32)


def _pool_fwd(z, pool_w, pool_scale, stages=()):
    s = z.shape[0]
    n_g, pg = pool_w.shape[0], pool_w.shape[1]
    assert n_g == len(POOL_WINDOWS) and max(POOL_WINDOWS) // 2 <= PAD
    t = _chunk(s)

    def body(u_ref, w_ref, sc_ref, d_ref, y_ref, pad_ref):
        g = pl.program_id(0)
        _zero_pads(pad_ref, s)
        pad_ref[pl.ds(PAD, s), :] = u_ref[...].astype(F32)
        for gi, w in enumerate(POOL_WINDOWS):
            @pl.when(g == gi)
            def _():
                def step(ch, carry):
                    t0 = pl.multiple_of(ch * t, t)
                    sup = _window(pad_ref, t0, t)
                    acc = _shift(sup, -(w // 2), t)
                    for o in range(-(w // 2) + 1, w // 2):
                        acc = acc + _shift(sup, o, t)
                    dd = (acc * (1.0 / _pool_count(t0, t, s, w)) - _shift(sup, 0, t)).astype(BF16)
                    d_ref[pl.ds(t0, t), :] = dd
                    y = jnp.dot(dd, w_ref[...], preferred_element_type=F32) * sc_ref[...]
                    y_ref[pl.ds(t0, t), :] = y.astype(BF16)
                    return carry

                lax.fori_loop(0, s // t, step, 0)

    blk = pl.BlockSpec((s, pg), lambda g: (0, g))
    res = _host_call(
        "pool_fwd", (n_g,), lambda ins, outs, scr: body(*ins, *outs, *scr), [z, pool_w, pool_scale],
        [blk, pl.BlockSpec((None, pg, pg), lambda g: (g, 0, 0)), pl.BlockSpec((1, pg), lambda g: (0, g))],
        [_sds((s, n_g * pg), BF16)] * 2, [blk, blk], [pltpu.VMEM((s + 2 * PAD, pg), F32)], list(stages))
    return res if stages else res[0]


def _pool_bwd(dsv, dy, pool_w, pool_scale, stages=()):
    s = dsv.shape[0]
    n_g, pg = pool_w.shape[0], pool_w.shape[1]
    t = _chunk(s)

    def body(d_ref, dy_ref, w_ref, sc_ref, du_ref, dw_ref, dsc_ref, epad_ref, dwacc_ref):
        g = pl.program_id(0)
        _zero_pads(epad_ref, s)
        dwacc_ref[...] = jnp.zeros_like(dwacc_ref)
        for gi, w in enumerate(POOL_WINDOWS):
            @pl.when(g == gi)
            def _():
                def first(ch, dsc):
                    t0 = pl.multiple_of(ch * t, t)
                    dd = d_ref[pl.ds(t0, t), :]
                    dyc = dy_ref[pl.ds(t0, t), :].astype(F32)
                    wv = w_ref[...]
                    ypre = jnp.dot(dd, wv, preferred_element_type=F32)
                    dq = (dyc * sc_ref[...]).astype(BF16)
                    dwacc_ref[...] += lax.dot_general(dd, dq, _DIMS["tn"], preferred_element_type=F32)
                    ddv = lax.dot_general(dq, wv, _DIMS["nt"], preferred_element_type=F32)
                    epad_ref[pl.ds(pl.multiple_of(PAD + t0, SUBLANES), t), :] = ddv * (1.0 / _pool_count(t0, t, s, w))
                    return dsc + _colsum(dyc * ypre)

                dsc_ref[...] = lax.fori_loop(0, s // t, first, jnp.zeros((1, pg), F32))

                def second(ch, carry):
                    t0 = pl.multiple_of(ch * t, t)
                    sup = _window(epad_ref, t0, t)
                    acc = _shift(sup, -(w // 2) + 1, t)
                    for o in range(-(w // 2) + 2, w // 2 + 1):
                        acc = acc + _shift(sup, o, t)
                    du_ref[pl.ds(t0, t), :] = (acc - _shift(sup, 0, t) * _pool_count(t0, t, s, w)).astype(BF16)
                    return carry

                lax.fori_loop(0, s // t, second, 0)

        dw_ref[...] = dwacc_ref[...].astype(BF16)

    blk = pl.BlockSpec((s, pg), lambda g: (0, g))
    w_spec = pl.BlockSpec((None, pg, pg), lambda g: (g, 0, 0))
    sc_spec = pl.BlockSpec((1, pg), lambda g: (0, g))
    res = _host_call(
        "pool_bwd", (n_g,), lambda ins, outs, scr: body(*ins, *outs, *scr), [dsv, dy, pool_w, pool_scale],
        [blk, blk, w_spec, sc_spec], [_sds((s, n_g * pg), BF16), _sds((n_g, pg, pg), BF16), _sds((1, n_g * pg), F32)],
        [blk, w_spec, sc_spec], [pltpu.VMEM((s + 2 * PAD, pg), F32), pltpu.VMEM((pg, pg), F32)], list(stages))
    return res if stages else res[0]


def _sigmoid(x):
    return 0.5 * jnp.tanh(0.5 * x) + 0.5


def _softplus(x):
    e = jnp.exp(-jnp.abs(x))
    log1p_e = jnp.where(e < 1e-2, e * (1.0 - e * (0.5 - e * (1.0 / 3.0))), jnp.log(1.0 + e))
    return jnp.maximum(x, 0.0) + log1p_e


_GELU_C = math.sqrt(2.0 / math.pi)


def _gelu(x):
    th = jnp.tanh(_GELU_C * (x + 0.044715 * x * x * x))
    return 0.5 * x * (1.0 + th), th


def _gelu_grad(x, th):
    return 0.5 * (1.0 + th) + 0.5 * x * (1.0 - th * th) * _GELU_C * (1.0 + 3.0 * 0.044715 * x * x)


def _scan_chunk(a_ref, b_ref, o_ref, o_off, carry, t, reverse):
    n = a_ref.shape[1]
    row = lax.broadcasted_iota(jnp.int32, (SUBLANES, n), 0)
    n_groups = t // SUBLANES
    unroll = math.gcd(n_groups, SCAN_UNROLL)
    last = 0 if reverse else SUBLANES - 1

    def step(si, carry):
        for u in range(unroll):
            gi = si * unroll + u
            g = n_groups - 1 - gi if reverse else gi
            r0 = pl.multiple_of(g * SUBLANES, SUBLANES)
            a = a_ref[pl.ds(r0, SUBLANES), :]
            b = b_ref[pl.ds(r0, SUBLANES), :]
            for k in (1, 2, 4):
                keep = row < SUBLANES - k if reverse else row >= k
                sh = SUBLANES - k if reverse else k
                ar = jnp.where(keep, pltpu.roll(a, sh, 0), 1.0)
                br = jnp.where(keep, pltpu.roll(b, sh, 0), 0.0)
                b = a * br + b
                a = a * ar
            o_ref[pl.ds(pl.multiple_of(o_off + r0, SUBLANES), SUBLANES), :] = a * carry + b
            carry = (jnp.broadcast_to(a[last:last + 1, :], a.shape) * carry
                     + jnp.broadcast_to(b[last:last + 1, :], b.shape))
        return carry

    return lax.fori_loop(0, n_groups // unroll, step, carry)


def _lru_params(pk_ref):
    rows = pk_ref[...]
    get = lambda i: rows[i:i + 1, :]
    cw = [get(k) for k in range(4)]
    lam = (get(9), get(10))
    big_l = tuple(-LRU_C * _softplus(-v) for v in lam)
    return cw, get(4), (get(5), get(6)), (get(7), get(8)), lam, big_l


def _conv(sup, cw, cb, t):
    xc = cb + cw[0] * _shift(sup, -2, t)
    for k in range(1, 4):
        xc = xc + cw[k] * _shift(sup, k - 2, t)
    return xc


def _gates(xcb, w_ref, d, bk, ba, bx, big_l):
    pre = jnp.dot(xcb, w_ref[:, pl.ds(d * 2 * bk, 2 * bk)], preferred_element_type=F32)
    r = _sigmoid(pre[:, :bk] + ba[d])
    i = _sigmoid(pre[:, bk:] + bx[d])
    la = big_l[d] * r
    a = jnp.exp(la)
    var = jnp.tanh(-la) * (1.0 + a * a)
    rs = lax.rsqrt(jnp.maximum(var, 1e-30))
    return r, i, a, var * rs, rs


def _lru_specs(s, d, bk):
    u_spec = pl.BlockSpec((s, bk), lambda h: (0, d // bk + h))
    ug_spec = pl.BlockSpec((s, bk), lambda h: (0, 2 * d // bk + h))
    w_spec = pl.BlockSpec((None, bk, 4 * bk), lambda h: (h, 0, 0))
    pk_spec = pl.BlockSpec((None, 16, bk), lambda h: (h, 0, 0))
    blk = pl.BlockSpec((s, bk), lambda h: (0, h))
    return u_spec, ug_spec, w_spec, pk_spec, blk


def _lru_fwd(z, gatew, pk, stages=()):
    s = z.shape[0]
    n_h, bk = gatew.shape[0], gatew.shape[1]
    d = n_h * bk
    t = _chunk(s)
    n_ch = s // t

    def body(u_ref, ug_ref, w_ref, pk_ref, y_ref, upad, h0buf, abuf, bbuf, xcbuf, h1buf):
        _zero_pads(upad, s)
        upad[pl.ds(PAD, s), :] = u_ref[...].astype(F32)
        cw, cb, ba, bx, _, big_l = _lru_params(pk_ref)
        zero = jnp.zeros((SUBLANES, bk), F32)

        def fill(xc, dr):
            _, i, a, sq, _ = _gates(xc.astype(BF16), w_ref, dr, bk, ba, bx, big_l)
            abuf[...] = a
            bbuf[...] = sq * i * xc

        def up(ch, carry):
            t0 = pl.multiple_of(ch * t, t)
            xc = _conv(_window(upad, t0, t), cw, cb, t)
            xcbuf[pl.ds(t0, t), :] = xc
            fill(xc, 0)
            return _scan_chunk(abuf, bbuf, h0buf, t0, carry, t, False)

        lax.fori_loop(0, n_ch, up, zero)

        def down(ci, carry):
            t0 = pl.multiple_of((n_ch - 1 - ci) * t, t)
            fill(xcbuf[pl.ds(t0, t), :], 1)
            carry = _scan_chunk(abuf, bbuf, h1buf, 0, carry, t, True)
            gl, _ = _gelu(ug_ref[pl.ds(t0, t), :].astype(F32))
            y_ref[pl.ds(t0, t), :] = ((h0buf[pl.ds(t0, t), :] + h1buf[...]) * gl).astype(BF16)
            return carry

        lax.fori_loop(0, n_ch, down, zero)

    u_spec, ug_spec, w_spec, pk_spec, blk = _lru_specs(s, d, bk)
    res = _host_call(
        "lru_fwd", (n_h,), lambda ins, outs, scr: body(*ins, *outs, *scr), [z, z, gatew, pk],
        [u_spec, ug_spec, w_spec, pk_spec], [_sds((s, d), BF16)], [blk],
        [pltpu.VMEM((s + 2 * PAD, bk), F32), pltpu.VMEM((s, bk), F32), pltpu.VMEM((t, bk), F32), pltpu.VMEM((t, bk), F32),
         pltpu.VMEM((s, bk), F32), pltpu.VMEM((t, bk), F32)], list(stages))
    return (res[0][0], res[1]) if stages else res[0][0]


def _lru_grads(lam_, hnb, a, sq, rs, r, i, xc, xcb, w_ref, dwacc, d, big_l, acc):
    bk = xc.shape[1]
    dba, dbx, dl = acc
    q = lam_ * i * xc
    dla = lam_ * hnb * a - q * (a * a) * rs
    dpr = dla * big_l * r * (1.0 - r)
    dpi = q * sq * (1.0 - i)
    dprb, dpib = dpr.astype(BF16), dpi.astype(BF16)
    c0 = d * 2 * bk
    dxc = (lam_ * sq * i
           + lax.dot_general(dprb, w_ref[:, pl.ds(c0, bk)], _DIMS["nt"], preferred_element_type=F32)
           + lax.dot_general(dpib, w_ref[:, pl.ds(c0 + bk, bk)], _DIMS["nt"], preferred_element_type=F32))
    dwacc[:, pl.ds(c0, bk)] += lax.dot_general(xcb, dprb, _DIMS["tn"], preferred_element_type=F32)
    dwacc[:, pl.ds(c0 + bk, bk)] += lax.dot_general(xcb, dpib, _DIMS["tn"], preferred_element_type=F32)
    return dxc, (dba + _colsum(dpr), dbx + _colsum(dpi), dl + _colsum(dla * r))


def _lru_bwd(z, dy, gatew, pk, stages=()):
    s = z.shape[0]
    n_h, bk = gatew.shape[0], gatew.shape[1]
    d = n_h * bk
    t = _chunk(s)
    n_ch = s // t

    def body(u_ref, ug_ref, dy_ref, w_ref, pk_ref, du_ref, dug_ref, dw_ref, dpk_ref,
             upad, h0pad, h1pad, dxpad, abuf, bbuf, lbuf, dwacc, edge, xcbuf):
        for ref in (upad, h0pad, h1pad, dxpad):
            _zero_pads(ref, s)
        upad[pl.ds(PAD, s), :] = u_ref[...].astype(F32)
        dwacc[...] = jnp.zeros_like(dwacc)
        cw, cb, ba, bx, lam, big_l = _lru_params(pk_ref)
        zero = jnp.zeros((SUBLANES, bk), F32)
        zrow = jnp.zeros((1, bk), F32)
        rowi = lax.broadcasted_iota(jnp.int32, (t, bk), 0)

        def at(t0):
            return pl.ds(pl.multiple_of(PAD + t0, SUBLANES), t)

        def conv_in(t0):
            xc = xcbuf[pl.ds(t0, t), :]
            return xc, xc.astype(BF16)

        def dh_of(t0):
            ug = ug_ref[pl.ds(t0, t), :].astype(F32)
            gl, th = _gelu(ug)
            dyv = dy_ref[pl.ds(t0, t), :].astype(F32)
            return dyv * gl, dyv * _gelu_grad(ug, th)

        def sweep1(ch, carry):
            t0 = pl.multiple_of(ch * t, t)
            xc = _conv(_window(upad, t0, t), cw, cb, t)
            xcbuf[pl.ds(t0, t), :] = xc
            _, i, a, sq, _ = _gates(xc.astype(BF16), w_ref, 0, bk, ba, bx, big_l)
            abuf[...] = a
            bbuf[...] = sq * i * xc
            return _scan_chunk(abuf, bbuf, h0pad, PAD + t0, carry, t, False)

        lax.fori_loop(0, n_ch, sweep1, zero)

        edge[...] = zero

        def sweep2(ci, st):
            carry_h, carry_l, acc = st
            t0 = pl.multiple_of((n_ch - 1 - ci) * t, t)
            xc, xcb = conv_in(t0)
            _, i1, a1, sq1, _ = _gates(xcb, w_ref, 1, bk, ba, bx, big_l)
            abuf[...] = a1
            bbuf[...] = sq1 * i1 * xc
            carry_h = _scan_chunk(abuf, bbuf, h1pad, PAD + t0, carry_h, t, True)
            dh, dgl = dh_of(t0)
            dug_ref[pl.ds(t0, t), :] = (dgl * (h0pad[at(t0), :] + h1pad[at(t0), :])).astype(BF16)
            r0, i0, a0, sq0, rs0 = _gates(xcb, w_ref, 0, bk, ba, bx, big_l)
            abuf[...] = jnp.where(rowi == t - 1, edge[0:1, :], pltpu.roll(a0, t - 1, 0))
            bbuf[...] = dh
            carry_l = _scan_chunk(abuf, bbuf, lbuf, 0, carry_l, t, True)
            edge[...] = jnp.broadcast_to(a0[0:1, :], (SUBLANES, bk))
            hprev = _shift(_window(h0pad, t0, t), -1, t)
            dxc, acc = _lru_grads(lbuf[...], hprev, a0, sq0, rs0, r0, i0, xc, xcb, w_ref, dwacc, 0, big_l[0], acc)
            dxpad[at(t0), :] = dxc
            return carry_h, carry_l, acc

        _, _, acc0 = lax.fori_loop(0, n_ch, sweep2, (zero, zero, (zrow, zrow, zrow)))

        edge[...] = zero

        def sweep3(ch, st):
            carry_l, acc = st
            t0 = pl.multiple_of(ch * t, t)
            xc, xcb = conv_in(t0)
            r1, i1, a1, sq1, rs1 = _gates(xcb, w_ref, 1, bk, ba, bx, big_l)
            dh, _ = dh_of(t0)
            abuf[...] = jnp.where(rowi == 0, edge[0:1, :], pltpu.roll(a1, 1, 0))
            bbuf[...] = dh
            carry_l = _scan_chunk(abuf, bbuf, lbuf, 0, carry_l, t, False)
            edge[...] = jnp.broadcast_to(a1[t - 1:t, :], (SUBLANES, bk))
            hnext = _shift(_window(h1pad, t0, t), 1, t)
            dxc, acc = _lru_grads(lbuf[...], hnext, a1, sq1, rs1, r1, i1, xc, xcb, w_ref, dwacc, 1, big_l[1], acc)
            dxpad[at(t0), :] += dxc
            return carry_l, acc

        _, acc1 = lax.fori_loop(0, n_ch, sweep3, (zero, (zrow, zrow, zrow)))

        def sweep4(ch, st):
            t0 = pl.multiple_of(ch * t, t)
            sdx = _window(dxpad, t0, t)
            su = _window(upad, t0, t)
            dxc = _shift(sdx, 0, t)
            du = cw[0] * _shift(sdx, 2, t) + cw[1] * _shift(sdx, 1, t) + cw[2] * dxc + cw[3] * _shift(sdx, -1, t)
            du_ref[pl.ds(t0, t), :] = du.astype(BF16)
            return tuple(st[k] + _colsum(dxc * _shift(su, k - 2, t)) for k in range(4)) + (st[4] + _colsum(dxc),)

        conv_g = lax.fori_loop(0, n_ch, sweep4, (zrow,) * 5)

        dpk_ref[...] = jnp.zeros_like(dpk_ref)
        rows = list(conv_g) + [acc0[0], acc1[0], acc0[1], acc1[1],
                               acc0[2] * LRU_C * _sigmoid(-lam[0]), acc1[2] * LRU_C * _sigmoid(-lam[1])]
        for k, v in enumerate(rows):
            dpk_ref[pl.ds(k, 1), :] = v
        dw_ref[...] = dwacc[...].astype(BF16)

    u_spec, ug_spec, w_spec, pk_spec, blk = _lru_specs(s, d, bk)
    padded = pltpu.VMEM((s + 2 * PAD, bk), F32)
    chunk = pltpu.VMEM((t, bk), F32)
    res = _host_call(
        "lru_bwd", (n_h,), lambda ins, outs, scr: body(*ins, *outs, *scr), [z, z, dy, gatew, pk],
        [u_spec, ug_spec, blk, w_spec, pk_spec],
        [_sds((s, d), BF16), _sds((s, d), BF16), _sds((n_h, bk, 4 * bk), BF16), _sds((n_h, 16, bk), F32)],
        [blk, blk, w_spec, pk_spec],
        [padded, padded, padded, padded, chunk, chunk, chunk, pltpu.VMEM((bk, 4 * bk), F32),
         pltpu.VMEM((SUBLANES, bk), F32), pltpu.VMEM((s, bk), F32)], list(stages))
    return res if stages else res[0]


def _scalar(v):
    return jnp.reshape(v, (1,)).astype(jnp.int32)


def _add_sibling(g, r, c):
    _, rows, cols = g.shape
    rh = rows // 2
    tr = _tile(rh, 512, 16)
    nr = rh // tr

    def body(c_ref, g_ref, r_ref, o_ref):
        o_ref[...] = (g_ref[...].astype(F32) + r_ref[...].astype(F32)).astype(BF16)

    spec = pl.BlockSpec((None, tr, cols), lambda k, i, c_ref: (k, i, 0))
    return pl.pallas_call(
        body, name="add_sibling", out_shape=_sds((N_CHIP, rh, cols), BF16),
        grid_spec=pltpu.PrefetchScalarGridSpec(
            num_scalar_prefetch=1, grid=(N_CHIP, nr),
            in_specs=[pl.BlockSpec((None, tr, cols), lambda k, i, c_ref: (k, c_ref[0] * nr + i, 0)), spec], out_specs=spec),
        compiler_params=_cparams(("arbitrary", "arbitrary")),
    )(_scalar(c), g, r)


def _sum_chips(p, rcv, k_me, c):
    _, rh, cols = p.shape
    tr = _tile(rh, 512, 16)
    nr = rh // tr

    def body(kc_ref, p_ref, r_ref, o_ref):
        acc = p_ref[...].astype(F32)
        for j in range(3):
            acc = acc + r_ref[j].astype(F32)
        o_ref[...] = acc

    return pl.pallas_call(
        body, name="sum_chips", out_shape=_sds((2 * rh, cols), F32),
        grid_spec=pltpu.PrefetchScalarGridSpec(
            num_scalar_prefetch=1, grid=(nr,),
            in_specs=[pl.BlockSpec((None, tr, cols), lambda i, kc_ref: (kc_ref[0], i, 0)),
                      pl.BlockSpec((3, tr, cols), lambda i, kc_ref: (0, i, 0))],
            out_specs=pl.BlockSpec((tr, cols), lambda i, kc_ref: (kc_ref[1] * nr + i, 0))),
        compiler_params=_cparams(("arbitrary",)),
    )(jnp.stack([k_me, c]).astype(jnp.int32), p, rcv)


def _sum_devices(g):
    def body(g_ref, o_ref):
        acc = g_ref[0]
        for dev in range(1, N_DEV):
            acc = acc + g_ref[dev]
        o_ref[...] = acc

    return pl.pallas_call(body, name="sum_devices", out_shape=_sds(g.shape[1:], F32))(g)


def _adamw(w, g, m, v):
    rows, cols = w.shape
    tr = _tile(rows, 256, SUBLANES)

    def body(ins, outs, scr):
        w_ref, g_ref, m_ref, v_ref = ins
        go_ref, d_ref, nm_ref, nv_ref = outs
        gv = g_ref[...]
        go_ref[...] = gv
        nm = ADAM_B1 * m_ref[...] + (1.0 - ADAM_B1) * gv
        nv = ADAM_B2 * v_ref[...] + (1.0 - ADAM_B2) * (gv * gv)
        m_hat = nm / (1.0 - ADAM_B1 ** ADAM_STEP)
        v_hat = nv / (1.0 - ADAM_B2 ** ADAM_STEP)
        d_ref[...] = -ADAM_LR * (m_hat / (jnp.sqrt(v_hat) + ADAM_EPS) + ADAM_WD * w_ref[...])
        nm_ref[...] = nm
        nv_ref[...] = nv

    spec = pl.BlockSpec((tr, cols), lambda i: (i, 0))
    return _host_call("adamw", (rows // tr,), body, [w, g, m, v], [spec] * 4, [_sds((rows, cols), F32)] * 4, [spec] * 4, [], [])[0]


def _pack(vs, unit):
    flat = jnp.concatenate([v.reshape(-1).astype(F32) for v in vs])
    pad = (-flat.shape[0]) % unit
    if pad:
        flat = jnp.concatenate([flat, jnp.zeros((pad,), F32)])
    return flat.reshape(-1, 128)


def _unpack(p, like):
    flat = p.reshape(-1)
    out, off = [], 0
    for v in like:
        n = math.prod(v.shape)
        out.append(flat[off:off + n].reshape(v.shape))
        off += n
    return out


def kernel(x, w_in, pool_w, pool_scale, conv_w, conv_b, lru_wa, lru_ba, lru_wx, lru_bx, lru_lambda, w_pool_up, w_lru_up, w_out, b_out, ln1_g, ln1_b, w_ff1, b_ff1, w_ff2, b_ff2, ln2_g, ln2_b, loss_target, m_w_in, m_pool_w, m_pool_scale, m_conv_w, m_conv_b, m_lru_wa, m_lru_ba, m_lru_wx, m_lru_bx, m_lru_lambda, m_w_pool_up, m_w_lru_up, m_w_out, m_b_out, m_ln1_g, m_ln1_b, m_w_ff1, m_b_ff1, m_w_ff2, m_b_ff2, m_ln2_g, m_ln2_b, v_w_in, v_pool_w, v_pool_scale, v_conv_w, v_conv_b, v_lru_wa, v_lru_ba, v_lru_wx, v_lru_bx, v_lru_lambda, v_w_pool_up, v_w_lru_up, v_w_out, v_b_out, v_ln1_g, v_ln1_b, v_w_ff1, v_b_ff1, v_w_ff2, v_b_ff2, v_ln2_g, v_ln2_b):
    given = dict(locals())
    wt = {n: given[n] for n in WEIGHTS}
    mom = {n: given["m_" + n] for n in WEIGHTS}
    vel = {n: given["v_" + n] for n in WEIGHTS}

    ix, iy, ic = _mesh_pos()
    k_me = 2 * ix + iy
    s, d = x.shape[1], x.shape[2]
    ds = d // N_CHIP
    n_g, pgs, pg = pool_w.shape[1], pool_w.shape[2], pool_w.shape[3]
    n_h, bks, bk = lru_wa.shape[2], lru_wa.shape[3], lru_wa.shape[4]
    f = b_ff1.shape[1]
    x2 = x[0]
    x_bf = x2.astype(BF16)
    vec = lambda a: a.reshape(1, -1)

    sharded_vecs = [conv_w[0], lru_ba[0], lru_bx[0], lru_lambda[0]]
    rows_sv = jnp.concatenate(sharded_vecs + [jnp.zeros((6, ds), F32)], axis=0)
    sv = _all_gather_small(rows_sv)
    sv = sv.reshape(N_CHIP, 2, 16, ds)[:, 0].transpose(1, 0, 2).reshape(16, d)
    conv_w_f, ba_f, bx_f, lam_f = sv[0:4], sv[4:6], sv[6:8], sv[8:10]
    pk = jnp.concatenate([conv_w_f, conv_b, ba_f, bx_f, lam_f, jnp.zeros((5, d), F32)], axis=0)
    pk = pk.reshape(16, n_h, bk).transpose(1, 0, 2)

    def gate_stack(wa, wx):
        return jnp.stack([wa[0], wx[0]], axis=1)

    mats = {
        "w_in": w_in[0], "w_pool_up": w_pool_up[0], "w_lru_up": w_lru_up[0], "w_out": w_out[0],
        "w_ff1": w_ff1[0], "w_ff2": w_ff2[0],
        "pool_w": pool_w[0].reshape(n_g * pgs, pg),
        "gate_w": gate_stack(lru_wa, lru_wx).reshape(4 * n_h * bks, bk),
    }
    names = list(mats)
    placed = {n: _cast_place(mats[n], k_me) for n in names}

    def add_sibling(gs, swapped):
        return [_add_sibling(g, r, ic) for g, r in zip(gs, swapped)]

    def sum_chips(ps, received):
        return [_sum_chips(p, r, k_me, ic) for p, r in zip(ps, received)]

    first = [placed[n] for n in ("w_in", "pool_w", "gate_w")]
    (bufs,) = _run_stages("gather_first", [_chain([_gather_direct(first), _gather_relay(first), _gather_d2d(first)])])
    wg_in = bufs[0]
    wf_pool = bufs[1].reshape(N_CHIP, n_g, pgs, pg).transpose(1, 0, 2, 3).reshape(n_g, pg, pg)
    wf_gate = bufs[2].reshape(N_CHIP, 2, 2, n_h, bks, bk).transpose(3, 0, 4, 1, 2, 5).reshape(n_h, bk, 4 * bk)

    z, (wb_mix, wb_ff1) = _fwd_in(x_bf, wg_in, stages=[
        _gather_direct([placed[n] for n in ("w_pool_up", "w_lru_up", "w_out")]), _gather_direct([placed["w_ff1"]])])
    (d_pool, y_pool), (wb_mix,) = _pool_fwd(z, wf_pool, pool_scale, stages=[_gather_relay(wb_mix)])
    y_lru, (wb_ff1, wb_mix, wb_ff2) = _lru_fwd(z, wf_gate, pk, stages=[
        _gather_relay(wb_ff1), _gather_d2d(wb_mix), _gather_direct([placed["w_ff2"]])])
    wf_pu, wf_lu, wf_out = (b.reshape(d, d) for b in wb_mix)
    (m_mix, p_a, p_b), (wb_ff1, wb_ff2) = _fwd_merge(y_pool, y_lru, wf_pu, wf_lu, z, stages=[
        _gather_d2d(wb_ff1), _gather_relay(wb_ff2)])
    wg_ff1 = wb_ff1[0]
    (xhat1, x1_bf, rstd1), (wb_ff2,) = _fwd_out_ln1(m_mix, wf_out, x2, b_out, ln1_g, ln1_b, stages=[_gather_d2d(wb_ff2)])
    hdn = _fwd_ff1(x1_bf, wg_ff1, b_ff1)
    wf_ff2 = wb_ff2[0].reshape(f, d)
    dr2, dr2_bf, g_ln2_g, g_ln2_b, g_b_ff2, loss_part = _fwd_ff2_ln2_loss(
        hdn, wf_ff2, xhat1, ln1_g, ln1_b, b_ff2, ln2_g, ln2_b, loss_target[0])

    dpre, g_b_ff1 = _bwd_ff2_in(dr2_bf, wf_ff2, hdn)
    g_ff = [_wgrad("wgrad_ff1", x1_bf, dpre, True), _wgrad("wgrad_ff2", hdn, dr2_bf, False)]
    (dr1, dr1_bf, g_ln1_g, g_ln1_b, g_b_out), (swapped,) = _bwd_ff1_in_ln1(
        dpre, wg_ff1, dr2, xhat1, rstd1, ln1_g, stages=[_swap_halves(g_ff)])
    sums_ff = add_sibling(g_ff, swapped)
    dp_a, dp_b, dg_a, dg_b = _bwd_out_in(dr1_bf, wf_out, z, p_a, p_b)
    dy_pool = _bwd_up_in("bwd_pool_up_in", dp_a, wf_pu)
    dy_lru = _bwd_up_in("bwd_lru_up_in", dp_b, wf_lu)
    g_mix = [_wgrad("wgrad_pool_up", y_pool, dp_a, False), _wgrad("wgrad_lru_up", y_lru, dp_b, False),
             _wgrad("wgrad_out", m_mix, dr1_bf, False)]
    (du_pool, g_pool_w, g_pool_scale), (swapped,) = _pool_bwd(
        d_pool, dy_pool, wf_pool, pool_scale, stages=[_swap_halves(g_mix)])
    sums_mix = add_sibling(g_mix, swapped)
    (du_lru, du_gate, g_gate_w, g_pk), (recv_ff, recv_mix) = _lru_bwd(
        z, dy_lru, wf_gate, pk, stages=[_scatter_chips(sums_ff), _scatter_chips(sums_mix)])
    halves = sum_chips(sums_ff + sums_mix, recv_ff + recv_mix)
    g_small = [g_pool_w.reshape(n_g, N_CHIP, pgs, pg).transpose(1, 0, 2, 3).reshape(N_CHIP, n_g * pgs, pg),
               g_gate_w.reshape(n_h, N_CHIP, bks, 2, 2, bk).transpose(1, 3, 4, 0, 2, 5).reshape(N_CHIP, 4 * n_h * bks, bk)]
    dz = jnp.concatenate([du_pool, du_lru, du_gate, dg_a, dg_b], axis=1)
    g_in, (joined, swapped) = _wgrad("wgrad_in", x_bf, dz, True, stages=[_join_halves(halves), _swap_halves(g_small)])
    g_mat = dict(zip(["w_ff1", "w_ff2", "w_pool_up", "w_lru_up", "w_out"], joined))
    sums_small = add_sibling(g_small, swapped)

    def stacked(tree):
        return gate_stack(tree["lru_wa"], tree["lru_wx"]).reshape(4 * n_h * bks, bk)

    res = {}

    def update(n):
        if n == "gate_w":
            outs = [o.reshape(2, 2, n_h, bks, bk) for o in _adamw(stacked(wt), g_mat[n], stacked(mom), stacked(vel))]
            res["lru_wa"] = [o[:, 0][None] for o in outs]
            res["lru_wx"] = [o[:, 1][None] for o in outs]
        else:
            shp2 = mats[n].shape
            outs = _adamw(wt[n].reshape(shp2), g_mat[n], mom[n].reshape(shp2), vel[n].reshape(shp2))
            res[n] = [o.reshape(wt[n].shape) for o in outs]

    (swapped,) = _run_stages("swap_in", [_swap_halves([g_in])])
    sums_in = add_sibling([g_in], swapped)
    grad_x, (recv_small, recv_in) = _bwd_in(dz, wg_in, dr1, stages=[_scatter_chips(sums_small), _scatter_chips(sums_in)])
    halves = sum_chips(sums_small + sums_in, recv_small + recv_in)
    (joined,) = _run_stages("join_last", [_join_halves(halves)])
    g_mat.update(zip(["pool_w", "gate_w", "w_in"], joined))
    for n in names:
        update(n)

    g_pk = g_pk.transpose(1, 0, 2).reshape(16, d)
    vec_full = {
        "pool_scale": g_pool_scale, "conv_w": g_pk[0:4], "conv_b": g_pk[4:5],
        "lru_ba": g_pk[5:7], "lru_bx": g_pk[7:9], "lru_lambda": g_pk[9:11],
        "b_out": g_b_out, "ln1_g": g_ln1_g, "ln1_b": g_ln1_b, "b_ff1": g_b_ff1, "b_ff2": g_b_ff2,
        "ln2_g": g_ln2_g, "ln2_b": g_ln2_b,
    }
    vnames = list(vec_full)
    vg = _sum_devices(_all_gather_small(_pack([vec_full[n] for n in vnames], 1024)))
    vg = dict(zip(vnames, _unpack(vg, [vec_full[n] for n in vnames])))
    for n in ("conv_w", "lru_ba", "lru_bx", "lru_lambda"):
        vg[n] = lax.dynamic_slice_in_dim(vg[n], k_me * ds, ds, axis=1)
    vg = {n: vg[n].reshape(wt[n].shape) for n in vnames}
    upd = _adamw(_pack([wt[n] for n in vnames], 1024), _pack([vg[n] for n in vnames], 1024),
                 _pack([mom[n] for n in vnames], 1024), _pack([vel[n] for n in vnames], 1024))
    upd = [_unpack(u, [wt[n] for n in vnames]) for u in upd]
    for i, n in enumerate(vnames):
        res[n] = [vg[n], upd[1][i], upd[2][i], upd[3][i]]

    loss = lax.psum(loss_part[0, 0], ("x", "y", "c"))
    return (loss, grad_x[None], *[res[n][0] for n in WEIGHTS], *[res[n][1] for n in WEIGHTS],
            *[res[n][2] for n in WEIGHTS], *[res[n][3] for n in WEIGHTS])
```

```python
import functools
import math

import jax
import jax.numpy as jnp
from jax import lax
from jax.experimental import pallas as pl
from jax.experimental.pallas import tpu as pltpu

F32 = jnp.float32
BF16 = jnp.bfloat16
MESH = pl.DeviceIdType.MESH
ANY = pl.BlockSpec(memory_space=pl.ANY)

N_CHIP = 4
N_DEV = 8
VMEM_LIMIT_BYTES = 56 * 1024 * 1024
SUBLANES = 8
PAD = 8
SCAN_UNROLL = 8

POOL_WINDOWS = (2, 4, 8, 16)
LRU_C = 8.0
DN_ALPHA = 2.0 ** 0.25
LN_EPS = 1e-5
ADAM_LR, ADAM_B1, ADAM_B2, ADAM_EPS, ADAM_WD, ADAM_STEP = 0.001, 0.9, 0.999, 1e-08, 0.01, 10

WEIGHTS = ("w_in", "pool_w", "pool_scale", "conv_w", "conv_b", "lru_wa", "lru_ba", "lru_wx", "lru_bx", "lru_lambda",
           "w_pool_up", "w_lru_up", "w_out", "b_out", "ln1_g", "ln1_b", "w_ff1", "b_ff1", "w_ff2", "b_ff2", "ln2_g", "ln2_b")


def _cparams(sem=None):
    return pltpu.CompilerParams(dimension_semantics=sem, vmem_limit_bytes=VMEM_LIMIT_BYTES)


def _tile(dim, pref, unit=128):
    if dim <= pref:
        return dim
    t = (pref // unit) * unit
    while t > unit and dim % t:
        t -= unit
    assert dim % t == 0, (dim, pref)
    return t


def _mesh_pos():
    x, y, c = lax.axis_index("x"), lax.axis_index("y"), lax.axis_index("c")
    return x, y, c


def _other_chips(x, y):
    return [(1 - x, y), (x, 1 - y), (1 - x, 1 - y)]


def _all_gather_small(v):
    m_per, n = v.shape

    def body(x_ref, out_ref, send_sems, recv_sems, local_sem):
        x, y, c = _mesh_pos()
        me, sibling = (x, y, c), (x, y, 1 - c)
        chips = _other_chips(x, y)

        def rows(px, py, pc):
            return out_ref.at[4 * px + 2 * py + pc]

        def copy(k, block, to, src=None):
            return pltpu.make_async_remote_copy(
                src_ref=rows(*block) if src is None else src, dst_ref=rows(*block),
                send_sem=send_sems.at[k], recv_sem=recv_sems.at[k], device_id=to, device_id_type=MESH)

        mine = pltpu.make_async_copy(x_ref, rows(*me), local_sem)
        mine.start()
        first = [copy(0, me, sibling, src=x_ref)]
        first += [copy(1 + j, me, (*chip, c), src=x_ref) for j, chip in enumerate(chips)]
        for cp in first:
            cp.start()
        passed = [copy(4 + j, (*chip, c), sibling) for j, chip in enumerate(chips)]
        for j, chip in enumerate(chips):
            copy(1 + j, (*chip, c), me).wait_recv()
            passed[j].start()
        copy(0, sibling, me).wait_recv()
        for j, chip in enumerate(chips):
            copy(4 + j, (*chip, 1 - c), me).wait_recv()
        for cp in first + passed:
            cp.wait_send()
        mine.wait()

    return pl.pallas_call(
        body, name="all_gather_small",
        out_shape=jax.ShapeDtypeStruct((N_DEV, m_per, n), v.dtype),
        in_specs=[pl.BlockSpec(memory_space=pltpu.VMEM)],
        out_specs=pl.BlockSpec(memory_space=pltpu.VMEM),
        scratch_shapes=[pltpu.SemaphoreType.DMA((7,)), pltpu.SemaphoreType.DMA((7,)), pltpu.SemaphoreType.DMA],
    )(v)


class _Stage:
    def __init__(self, srcs, bufs, news, n_sems, copies):
        self.srcs, self.bufs, self.news, self.n_sems, self.copies = list(srcs), list(bufs), list(news), n_sems, copies
        self.phases = [(copies, 0)]


class _SemsFrom:
    def __init__(self, ref, offset):
        self.ref, self.offset, self.at = ref, offset, self

    def __getitem__(self, s):
        return self.ref.at[self.offset + s]


def _chain(stages):
    chained = _Stage([], stages[0].bufs, [], sum(st.n_sems for st in stages), None)
    chained.phases, first = [], 0
    for st in stages:
        chained.phases.append((st.copies, first))
        first += st.n_sems
    return chained


def _remote(src, dst, send_sems, recv_sems, s, to):
    return pltpu.make_async_remote_copy(src_ref=src, dst_ref=dst, send_sem=send_sems.at[s], recv_sem=recv_sems.at[s],
                                        device_id=to, device_id_type=MESH)


def _stage_operands(stages, n_in, n_out):
    ins, outs, aliases, scratch = [], [], {}, []
    for st in stages:
        for i in range(len(st.bufs)):
            aliases[n_in + len(ins) + len(st.srcs) + i] = n_out + len(outs) + i
        ins += st.srcs + st.bufs
        outs += [jax.ShapeDtypeStruct(b.shape, b.dtype) for b in st.bufs] + st.news
        scratch += [pltpu.SemaphoreType.DMA((st.n_sems,)), pltpu.SemaphoreType.DMA((st.n_sems,))]
    return ins, outs, aliases, scratch


def _stage_refs(stages, in_refs, out_refs, sem_refs):
    parts, i, o = [], 0, 0
    for n, st in enumerate(stages):
        src = in_refs[i:i + len(st.srcs)]
        i += len(st.srcs) + len(st.bufs)
        buf = out_refs[o:o + len(st.bufs)]
        new = out_refs[o + len(st.bufs):o + len(st.bufs) + len(st.news)]
        o += len(st.bufs) + len(st.news)
        parts.append((src, buf, new, sem_refs[2 * n], sem_refs[2 * n + 1]))
    return parts


def _stage_results(stages, res):
    out, o = [], 0
    for st in stages:
        n = len(st.bufs) + len(st.news)
        out.append(list(res[o:o + n]))
        o += n
    return out


def _stages_start(stages, parts):
    for st, part in zip(stages, parts):
        for cp in st.copies(*part)[0]:
            cp.start()


def _stages_wait(stages, parts):
    for st, part in zip(stages, parts):
        started, landing = st.copies(*part)
        for cp in landing:
            cp.wait_recv()
        for cp in started:
            cp.wait_send()


def _run_stages(name, stages):
    ins, outs, aliases, scratch = _stage_operands(stages, 0, 0)

    def body(*refs):
        parts = _stage_refs(stages, refs[:len(ins)], refs[len(ins):len(ins) + len(outs)], refs[len(ins) + len(outs):])
        for st, (src, buf, new, send_sems, recv_sems) in zip(stages, parts):
            for copies, first in st.phases:
                started, landing = copies(src, buf, new, _SemsFrom(send_sems, first), _SemsFrom(recv_sems, first))
                for cp in started:
                    cp.start()
                for cp in landing:
                    cp.wait_recv()
                for cp in started:
                    cp.wait_send()

    res = pl.pallas_call(
        body, name=name, out_shape=outs, in_specs=[ANY] * len(ins), out_specs=[ANY] * len(outs),
        input_output_aliases=aliases, scratch_shapes=scratch)(*ins)
    return _stage_results(stages, res)


def _gather_direct(ts):
    def copies(src, buf, new, send_sems, recv_sems):
        x, y, c = _mesh_pos()
        started, landing = [], []
        for t in range(len(ts)):
            rh = ts[t].shape[1] // 2
            rows = pl.ds(c * rh, rh)
            mine = buf[t].at[2 * x + y, rows]
            for j, chip in enumerate(_other_chips(x, y)[:2]):
                theirs = buf[t].at[2 * chip[0] + chip[1], rows]
                started.append(_remote(mine, mine, send_sems, recv_sems, 2 * t + j, (*chip, c)))
                landing.append(_remote(theirs, theirs, send_sems, recv_sems, 2 * t + j, (x, y, c)))
        return started, landing

    return _Stage([], ts, [], 2 * len(ts), copies)


def _gather_relay(ts):
    def copies(src, buf, new, send_sems, recv_sems):
        x, y, c = _mesh_pos()
        (x_nb, y_nb, diag) = _other_chips(x, y)
        block = lambda chip: 2 * chip[0] + chip[1]
        started, landing = [], []
        for t in range(len(ts)):
            rq = ts[t].shape[1] // 4
            q0, q1 = pl.ds(2 * c * rq, rq), pl.ds((2 * c + 1) * rq, rq)
            from_y, from_x = buf[t].at[block(y_nb), q0], buf[t].at[block(x_nb), q1]
            started.append(_remote(from_y, from_y, send_sems, recv_sems, 2 * t, (*x_nb, c)))
            started.append(_remote(from_x, from_x, send_sems, recv_sems, 2 * t + 1, (*y_nb, c)))
            for j, q in enumerate((q0, q1)):
                lands = buf[t].at[block(diag), q]
                landing.append(_remote(lands, lands, send_sems, recv_sems, 2 * t + j, (x, y, c)))
        return started, landing

    return _Stage([], ts, [], 2 * len(ts), copies)


def _gather_d2d(ts):
    def copies(src, buf, new, send_sems, recv_sems):
        x, y, c = _mesh_pos()
        started, landing = [], []
        for t in range(len(ts)):
            rh = ts[t].shape[1] // 2
            for j, chip in enumerate(_other_chips(x, y)):
                got = buf[t].at[2 * chip[0] + chip[1], pl.ds(c * rh, rh)]
                other = buf[t].at[2 * chip[0] + chip[1], pl.ds((1 - c) * rh, rh)]
                started.append(_remote(got, got, send_sems, recv_sems, 3 * t + j, (x, y, 1 - c)))
                landing.append(_remote(other, other, send_sems, recv_sems, 3 * t + j, (x, y, c)))
        return started, landing

    return _Stage([], ts, [], 3 * len(ts), copies)


def _swap_halves(gs):
    def copies(src, buf, new, send_sems, recv_sems):
        x, y, c = _mesh_pos()
        started, landing = [], []
        for t in range(len(gs)):
            rh = gs[t].shape[1] // 2
            started.append(_remote(src[t].at[:, pl.ds((1 - c) * rh, rh)], new[t], send_sems, recv_sems, t, (x, y, 1 - c)))
            landing.append(_remote(new[t], new[t], send_sems, recv_sems, t, (x, y, c)))
        return started, landing

    news = [jax.ShapeDtypeStruct((g.shape[0], g.shape[1] // 2, g.shape[2]), g.dtype) for g in gs]
    return _Stage(gs, [], news, len(gs), copies)


def _scatter_chips(ps):
    def copies(src, buf, new, send_sems, recv_sems):
        x, y, c = _mesh_pos()
        started, landing = [], []
        for t in range(len(ps)):
            for j, chip in enumerate(_other_chips(x, y)):
                started.append(_remote(src[t].at[2 * chip[0] + chip[1]], new[t].at[j], send_sems, recv_sems, 3 * t + j, (*chip, c)))
                landing.append(_remote(new[t].at[j], new[t].at[j], send_sems, recv_sems, 3 * t + j, (x, y, c)))
        return started, landing

    return _Stage(ps, [], [jax.ShapeDtypeStruct((3,) + p.shape[1:], p.dtype) for p in ps], 3 * len(ps), copies)


def _join_halves(fs):
    def copies(src, buf, new, send_sems, recv_sems):
        x, y, c = _mesh_pos()
        started, landing = [], []
        for t in range(len(fs)):
            rh = fs[t].shape[0] // 2
            mine = buf[t].at[pl.ds(c * rh, rh)]
            theirs = buf[t].at[pl.ds((1 - c) * rh, rh)]
            started.append(_remote(mine, mine, send_sems, recv_sems, t, (x, y, 1 - c)))
            landing.append(_remote(theirs, theirs, send_sems, recv_sems, t, (x, y, c)))
        return started, landing

    return _Stage([], fs, [], len(fs), copies)


def _cast_place(w, k_me):
    rows, cols = w.shape
    tr = _tile(rows, 512, 16)

    def body(k_ref, w_ref, o_ref):
        o_ref[...] = w_ref[...].astype(BF16)

    return pl.pallas_call(
        body, name="cast_place", out_shape=_sds((N_CHIP, rows, cols), BF16),
        grid_spec=pltpu.PrefetchScalarGridSpec(
            num_scalar_prefetch=1, grid=(rows // tr,),
            in_specs=[pl.BlockSpec((tr, cols), lambda i, k_ref: (i, 0))],
            out_specs=pl.BlockSpec((None, tr, cols), lambda i, k_ref: (k_ref[0], i, 0))),
        compiler_params=_cparams(("arbitrary",)),
    )(_scalar(k_me), w)


_DIMS = {"nn": (((1,), (0,)), ((), ())), "nt": (((1,), (1,)), ((), ())), "tn": (((0,), (0,)), ((), ()))}


def _accum(ref, val, first):
    @pl.when(first)
    def _():
        ref[...] = val

    @pl.when(jnp.logical_not(first))
    def _():
        ref[...] += val


def _grid_edges(grid):
    ids = [pl.program_id(ax) for ax in range(len(grid))]
    first = functools.reduce(jnp.logical_and, [i == 0 for i in ids])
    last = functools.reduce(jnp.logical_and, [i == n - 1 for i, n in zip(ids, grid)])
    return first, last


def _host_call(name, grid, body, operands, in_specs, out_shape, out_specs, scratch, stages):
    s_ins, s_outs, aliases, s_scratch = _stage_operands(stages, len(operands), len(out_shape))
    n_in, n_out, n_scr = len(operands), len(out_shape), len(scratch)

    def full_body(*refs):
        in_refs = refs[:n_in]
        s_in_refs = refs[n_in:n_in + len(s_ins)]
        o0 = n_in + len(s_ins)
        out_refs = refs[o0:o0 + n_out]
        s_out_refs = refs[o0 + n_out:o0 + n_out + len(s_outs)]
        c0 = o0 + n_out + len(s_outs)
        scr_refs = refs[c0:c0 + n_scr]
        if stages:
            parts = _stage_refs(stages, s_in_refs, s_out_refs, refs[c0 + n_scr:])
            first, last = _grid_edges(grid)
            pl.when(first)(lambda: _stages_start(stages, parts))
        body(in_refs, out_refs, scr_refs)
        if stages:
            pl.when(last)(lambda: _stages_wait(stages, parts))

    res = pl.pallas_call(
        full_body, name=name, grid=grid, in_specs=list(in_specs) + [ANY] * len(s_ins),
        out_specs=list(out_specs) + [ANY] * len(s_outs), out_shape=list(out_shape) + s_outs,
        input_output_aliases=aliases, scratch_shapes=list(scratch) + s_scratch,
        compiler_params=_cparams(("arbitrary",) * len(grid)),
    )(*operands, *s_ins)
    return list(res[:n_out]), _stage_results(stages, res[n_out:])


def _matmul(name, grid, pairs, extras, outs, acc_shape, epilogue, stages=()):
    n_p = len(pairs)
    n_k = grid[-1]
    dims = [_DIMS[p[4]] for p in pairs]

    def body(in_refs, out, accs):
        ab, ex = in_refs[:2 * n_p], in_refs[2 * n_p:]
        ids = [pl.program_id(ax) for ax in range(len(grid))]
        k = ids[-1]

        @pl.when(k == 0)
        def _():
            for acc in accs:
                acc[...] = jnp.zeros_like(acc)

        for p in range(n_p):
            a = ab[2 * p][...].astype(BF16)
            b = ab[2 * p + 1][...].astype(BF16)
            accs[p][...] += lax.dot_general(a, b, dims[p], preferred_element_type=F32)

        @pl.when(k == n_k - 1)
        def _():
            epilogue([acc[...] for acc in accs], ex, out, ids)

    in_specs = []
    operands = []
    for a, a_spec, b, b_spec, _ in pairs:
        in_specs += [a_spec, b_spec]
        operands += [a, b]
    for e, e_spec in extras:
        in_specs.append(e_spec)
        operands.append(e)
    res, stage_res = _host_call(name, grid, body, operands, in_specs, [o[0] for o in outs], [o[1] for o in outs],
                                [pltpu.VMEM(acc_shape, F32) for _ in pairs], list(stages))
    return (res, stage_res) if stages else res


def _out(res, stages, single=False):
    outs = res[0] if stages else res
    outs = outs[0] if single else outs
    return (outs, res[1]) if stages else outs


def _sds(shape, dtype):
    return jax.ShapeDtypeStruct(shape, dtype)


def _row(n):
    return pl.BlockSpec((1, n), lambda *_: (0, 0))


def _layer_norm(r):
    mu = jnp.mean(r, axis=-1, keepdims=True)
    xc = r - mu
    var = jnp.mean(xc * xc, axis=-1, keepdims=True)
    rstd = lax.rsqrt(var + LN_EPS)
    return xc * rstd, rstd


def _layer_norm_bwd(dxhat, xhat, rstd):
    m1 = jnp.mean(dxhat, axis=-1, keepdims=True)
    m2 = jnp.mean(dxhat * xhat, axis=-1, keepdims=True)
    return rstd * (dxhat - m1 - xhat * m2)


def _colsum(v):
    return jnp.sum(v, axis=0, keepdims=True)


def _fwd_in(x_bf, wg_in, stages=()):
    s, d = x_bf.shape
    inc = wg_in.shape[2]
    tm, tn, tk = _tile(s, 1024), _tile(inc, 1280), _tile(d, 2048)
    nb = inc // tn

    def epi(accs, ex, out, ids):
        out[0][...] = accs[0].astype(BF16)

    return _out(_matmul(
        "fwd_in", (s // tm, N_CHIP * nb, d // tk),
        [(x_bf, pl.BlockSpec((tm, tk), lambda i, j, k: (i, k)),
          wg_in, pl.BlockSpec((None, tk, tn), lambda i, j, k: (j // nb, k, j % nb)), "nn")],
        [], [(_sds((s, N_CHIP * inc), BF16), pl.BlockSpec((tm, tn), lambda i, j, k: (i, j)))],
        (tm, tn), epi, stages), stages, True)


def _fwd_merge(y_pool, y_lru, w_pu, w_lu, z, stages=()):
    s, d = y_pool.shape
    tm, tn, tk = _tile(s, 1024), _tile(d, 1024), _tile(d, 1024)
    ga0, gb0 = 3 * d // tn, 4 * d // tn

    def epi(accs, ex, out, ids):
        sa = _sigmoid(ex[0][...].astype(F32))
        sb = _sigmoid(ex[1][...].astype(F32))
        out[0][...] = (sa * accs[0] + sb * accs[1]).astype(BF16)
        out[1][...] = accs[0].astype(BF16)
        out[2][...] = accs[1].astype(BF16)

    a_spec = pl.BlockSpec((tm, tk), lambda i, j, k: (i, k))
    b_spec = pl.BlockSpec((tk, tn), lambda i, j, k: (k, j))
    o_spec = pl.BlockSpec((tm, tn), lambda i, j, k: (i, j))
    return _out(_matmul(
        "fwd_merge", (s // tm, d // tn, d // tk),
        [(y_pool, a_spec, w_pu, b_spec, "nn"), (y_lru, a_spec, w_lu, b_spec, "nn")],
        [(z, pl.BlockSpec((tm, tn), lambda i, j, k: (i, ga0 + j))), (z, pl.BlockSpec((tm, tn), lambda i, j, k: (i, gb0 + j)))],
        [(_sds((s, d), BF16), o_spec)] * 3, (tm, tn), epi, stages), stages)


def _fwd_out_ln1(m, w_out, x, b_out, g1, b1, stages=()):
    s, d = x.shape
    tm, tk = _tile(s, 512), _tile(d, 2048)

    def epi(accs, ex, out, ids):
        r = DN_ALPHA * ex[0][...] + accs[0] + ex[1][...]
        xhat, rstd = _layer_norm(r)
        out[0][...] = xhat
        out[1][...] = (xhat * ex[2][...] + ex[3][...]).astype(BF16)
        out[2][...] = rstd

    full = pl.BlockSpec((tm, d), lambda i, j, k: (i, 0))
    return _out(_matmul(
        "fwd_out_ln1", (s // tm, 1, d // tk),
        [(m, pl.BlockSpec((tm, tk), lambda i, j, k: (i, k)), w_out, pl.BlockSpec((tk, d), lambda i, j, k: (k, 0)), "nn")],
        [(x, full), (b_out, _row(d)), (g1, _row(d)), (b1, _row(d))],
        [(_sds((s, d), F32), full), (_sds((s, d), BF16), full), (_sds((s, 1), F32), pl.BlockSpec((tm, 1), lambda i, j, k: (i, 0)))],
        (tm, d), epi, stages), stages)


def _fwd_ff1(x1_bf, wg_ff1, b_ff1, stages=()):
    s, d = x1_bf.shape
    fc = wg_ff1.shape[2]
    tm, tn, tk = _tile(s, 1024), _tile(fc, 1024), _tile(d, 2048)
    nb = fc // tn

    def epi(accs, ex, out, ids):
        p = jnp.maximum(accs[0] + ex[0][...], 0.0)
        out[0][...] = (p * p).astype(BF16)

    return _out(_matmul(
        "fwd_ff1", (s // tm, N_CHIP * nb, d // tk),
        [(x1_bf, pl.BlockSpec((tm, tk), lambda i, j, k: (i, k)),
          wg_ff1, pl.BlockSpec((None, tk, tn), lambda i, j, k: (j // nb, k, j % nb)), "nn")],
        [(b_ff1, pl.BlockSpec((1, tn), lambda i, j, k: (0, j)))],
        [(_sds((s, N_CHIP * fc), BF16), pl.BlockSpec((tm, tn), lambda i, j, k: (i, j)))],
        (tm, tn), epi, stages), stages, True)


def _fwd_ff2_ln2_loss(hdn, w_ff2, xhat1, g1, b1, b_ff2, g2, b2, target, stages=()):
    s, f = hdn.shape
    d = xhat1.shape[1]
    tm, tk = _tile(s, 512), _tile(f, 1024)

    def epi(accs, ex, out, ids):
        first = ids[0] == 0
        x1 = ex[0][...] * ex[1][...] + ex[2][...]
        r = DN_ALPHA * x1 + accs[0] + ex[3][...]
        xhat, rstd = _layer_norm(r)
        g2v = ex[4][...]
        err = xhat * g2v + ex[5][...] - ex[6][...]
        part = 0.5 * jnp.sum(jnp.mean(err * err, axis=-1, keepdims=True), axis=0, keepdims=True)
        dy = err * (1.0 / d)
        dr2 = _layer_norm_bwd(dy * g2v, xhat, rstd)
        out[0][...] = dr2
        out[1][...] = dr2.astype(BF16)
        _accum(out[2], _colsum(dy * xhat), first)
        _accum(out[3], _colsum(dy), first)
        _accum(out[4], _colsum(dr2), first)
        _accum(out[5], jnp.broadcast_to(part, (1, 128)), first)

    full = pl.BlockSpec((tm, d), lambda i, j, k: (i, 0))
    return _out(_matmul(
        "fwd_ff2_ln2_loss", (s // tm, 1, f // tk),
        [(hdn, pl.BlockSpec((tm, tk), lambda i, j, k: (i, k)), w_ff2, pl.BlockSpec((tk, d), lambda i, j, k: (k, 0)), "nn")],
        [(xhat1, full), (g1, _row(d)), (b1, _row(d)), (b_ff2, _row(d)), (g2, _row(d)), (b2, _row(d)), (target, full)],
        [(_sds((s, d), F32), full), (_sds((s, d), BF16), full), (_sds((1, d), F32), _row(d)), (_sds((1, d), F32), _row(d)),
         (_sds((1, d), F32), _row(d)), (_sds((1, 128), F32), _row(128))],
        (tm, d), epi, stages), stages)


def _bwd_ff2_in(dr2_bf, w_ff2, hdn, stages=()):
    s, d = dr2_bf.shape
    f = hdn.shape[1]
    tm, tn, tk = _tile(s, 1024), _tile(f, 1024), _tile(d, 2048)

    def epi(accs, ex, out, ids):
        dpre = accs[0] * (2.0 * jnp.sqrt(ex[0][...].astype(F32)))
        out[0][...] = dpre.astype(BF16)
        _accum(out[1], _colsum(dpre), ids[1] == 0)

    return _out(_matmul(
        "bwd_ff2_in", (f // tn, s // tm, d // tk),
        [(dr2_bf, pl.BlockSpec((tm, tk), lambda j, i, k: (i, k)), w_ff2, pl.BlockSpec((tn, tk), lambda j, i, k: (j, k)), "nt")],
        [(hdn, pl.BlockSpec((tm, tn), lambda j, i, k: (i, j)))],
        [(_sds((s, f), BF16), pl.BlockSpec((tm, tn), lambda j, i, k: (i, j))), (_sds((1, f), F32), pl.BlockSpec((1, tn), lambda j, i, k: (0, j)))],
        (tm, tn), epi, stages), stages)


def _bwd_ff1_in_ln1(dpre, wg_ff1, dr2, xhat1, rstd1, g1, stages=()):
    s, f = dpre.shape
    d = xhat1.shape[1]
    fc = wg_ff1.shape[2]
    tm, tk = _tile(s, 512), _tile(fc, 1024)
    nb = fc // tk

    def epi(accs, ex, out, ids):
        first = ids[0] == 0
        xhat = ex[1][...]
        dx1 = accs[0] + DN_ALPHA * ex[0][...]
        dr1 = _layer_norm_bwd(dx1 * ex[3][...], xhat, ex[2][...])
        out[0][...] = dr1
        out[1][...] = dr1.astype(BF16)
        _accum(out[2], _colsum(dx1 * xhat), first)
        _accum(out[3], _colsum(dx1), first)
        _accum(out[4], _colsum(dr1), first)

    full = pl.BlockSpec((tm, d), lambda i, j, k: (i, 0))
    return _out(_matmul(
        "bwd_ff1_in_ln1", (s // tm, 1, f // tk),
        [(dpre, pl.BlockSpec((tm, tk), lambda i, j, k: (i, k)),
          wg_ff1, pl.BlockSpec((None, d, tk), lambda i, j, k: (k // nb, 0, k % nb)), "nt")],
        [(dr2, full), (xhat1, full), (rstd1, pl.BlockSpec((tm, 1), lambda i, j, k: (i, 0))), (g1, _row(d))],
        [(_sds((s, d), F32), full), (_sds((s, d), BF16), full), (_sds((1, d), F32), _row(d)), (_sds((1, d), F32), _row(d)),
         (_sds((1, d), F32), _row(d))],
        (tm, d), epi, stages), stages)


def _bwd_out_in(dr1_bf, w_out, z, pa, pb, stages=()):
    s, d = dr1_bf.shape
    tm, tn, tk = _tile(s, 1024), _tile(d, 1024), _tile(d, 2048)
    ga0, gb0 = 3 * d // tn, 4 * d // tn

    def epi(accs, ex, out, ids):
        dm = accs[0]
        sa = _sigmoid(ex[0][...].astype(F32))
        sb = _sigmoid(ex[1][...].astype(F32))
        out[0][...] = (dm * sa).astype(BF16)
        out[1][...] = (dm * sb).astype(BF16)
        out[2][...] = (dm * ex[2][...].astype(F32) * sa * (1.0 - sa)).astype(BF16)
        out[3][...] = (dm * ex[3][...].astype(F32) * sb * (1.0 - sb)).astype(BF16)

    o_spec = pl.BlockSpec((tm, tn), lambda i, j, k: (i, j))
    return _out(_matmul(
        "bwd_out_in", (s // tm, d // tn, d // tk),
        [(dr1_bf, pl.BlockSpec((tm, tk), lambda i, j, k: (i, k)), w_out, pl.BlockSpec((tn, tk), lambda i, j, k: (j, k)), "nt")],
        [(z, pl.BlockSpec((tm, tn), lambda i, j, k: (i, ga0 + j))), (z, pl.BlockSpec((tm, tn), lambda i, j, k: (i, gb0 + j))),
         (pa, o_spec), (pb, o_spec)],
        [(_sds((s, d), BF16), o_spec)] * 4, (tm, tn), epi, stages), stages)


def _bwd_up_in(name, dp, w_up, stages=()):
    s, d = dp.shape
    n = w_up.shape[0]
    tm, tn, tk = _tile(s, 1024), _tile(n, 1024), _tile(d, 2048)

    def epi(accs, ex, out, ids):
        out[0][...] = accs[0].astype(BF16)

    return _out(_matmul(
        name, (s // tm, n // tn, d // tk),
        [(dp, pl.BlockSpec((tm, tk), lambda i, j, k: (i, k)), w_up, pl.BlockSpec((tn, tk), lambda i, j, k: (j, k)), "nt")],
        [], [(_sds((s, n), BF16), pl.BlockSpec((tm, tn), lambda i, j, k: (i, j)))], (tm, tn), epi, stages), stages, True)


def _bwd_in(dz, wg_in, dr1, stages=()):
    s, d = dr1.shape
    inc = wg_in.shape[2]
    tm, tn, tk = _tile(s, 1024), _tile(d, 1024), _tile(inc, 2560)
    nb = inc // tk

    def epi(accs, ex, out, ids):
        out[0][...] = accs[0] + DN_ALPHA * ex[0][...]

    o_spec = pl.BlockSpec((tm, tn), lambda i, j, k: (i, j))
    return _out(_matmul(
        "bwd_in", (s // tm, d // tn, N_CHIP * nb),
        [(dz, pl.BlockSpec((tm, tk), lambda i, j, k: (i, k)),
          wg_in, pl.BlockSpec((None, tn, tk), lambda i, j, k: (k // nb, j, k % nb)), "nt")],
        [(dr1, o_spec)], [(_sds((s, d), F32), o_spec)], (tm, tn), epi, stages), stages, True)


def _wgrad(name, a, b, col_sharded, stages=()):
    s, ka = a.shape
    n = b.shape[1]
    tm, tk = _tile(ka, 1024), _tile(s, 2048)
    tn = _tile(n // N_CHIP, 1280) if col_sharded else _tile(n, 1024)

    def epi(accs, ex, out, ids):
        out[0][...] = accs[0].astype(BF16)

    if col_sharded:
        nb = (n // N_CHIP) // tn
        o = (_sds((N_CHIP, ka, n // N_CHIP), BF16), pl.BlockSpec((None, tm, tn), lambda i, j, k: (j // nb, i, j % nb)))
    else:
        o = (_sds((ka, n), BF16), pl.BlockSpec((tm, tn), lambda i, j, k: (i, j)))
    res = _out(_matmul(
        name, (ka // tm, n // tn, s // tk),
        [(a, pl.BlockSpec((tk, tm), lambda i, j, k: (k, i)), b, pl.BlockSpec((tk, tn), lambda i, j, k: (k, j)), "tn")],
        [], [o], (tm, tn), epi, stages), stages, True)
    res, stage_res = res if stages else (res, None)
    res = res if col_sharded else res.reshape(N_CHIP, ka // N_CHIP, n)
    return (res, stage_res) if stages else res


def _chunk(s):
    return _tile(s, 512, SUBLANES)


def _zero_pads(ref, s):
    zeros = jnp.zeros((PAD, ref.shape[1]), F32)
    ref[pl.ds(0, PAD), :] = zeros
    ref[pl.ds(PAD + s, PAD), :] = zeros


def _window(ref, t0, t):
    return ref[pl.ds(t0, t + 2 * PAD), :]


def _shift(sup, off, t):
    return sup[PAD + off:PAD + off + t, :]


def _pool_count(t0, t, s, w):
    pos = t0 + lax.broadcasted_iota(jnp.int32, (t, 1), 0)
    return (jnp.minimum(pos + w // 2, s) - jnp.maximum(pos - w // 2, 0)).astype(F32)


def _pool_fwd(z, pool_w, pool_scale, stages=()):
    s = z.shape[0]
    n_g, pg = pool_w.shape[0], pool_w.shape[1]
    assert n_g == len(POOL_WINDOWS) and max(POOL_WINDOWS) // 2 <= PAD
    t = _chunk(s)

    def body(u_ref, w_ref, sc_ref, d_ref, y_ref, pad_ref):
        g = pl.program_id(0)
        _zero_pads(pad_ref, s)
        pad_ref[pl.ds(PAD, s), :] = u_ref[...].astype(F32)
        for gi, w in enumerate(POOL_WINDOWS):
            @pl.when(g == gi)
            def _():
                def step(ch, carry):
                    t0 = pl.multiple_of(ch * t, t)
                    sup = _window(pad_ref, t0, t)
                    acc = _shift(sup, -(w // 2), t)
                    for o in range(-(w // 2) + 1, w // 2):
                        acc = acc + _shift(sup, o, t)
                    dd = (acc * (1.0 / _pool_count(t0, t, s, w)) - _shift(sup, 0, t)).astype(BF16)
                    d_ref[pl.ds(t0, t), :] = dd
                    y = jnp.dot(dd, w_ref[...], preferred_element_type=F32) * sc_ref[...]
                    y_ref[pl.ds(t0, t), :] = y.astype(BF16)
                    return carry

                lax.fori_loop(0, s // t, step, 0)

    blk = pl.BlockSpec((s, pg), lambda g: (0, g))
    res = _host_call(
        "pool_fwd", (n_g,), lambda ins, outs, scr: body(*ins, *outs, *scr), [z, pool_w, pool_scale],
        [blk, pl.BlockSpec((None, pg, pg), lambda g: (g, 0, 0)), pl.BlockSpec((1, pg), lambda g: (0, g))],
        [_sds((s, n_g * pg), BF16)] * 2, [blk, blk], [pltpu.VMEM((s + 2 * PAD, pg), F32)], list(stages))
    return res if stages else res[0]


def _pool_bwd(dsv, dy, pool_w, pool_scale, stages=()):
    s = dsv.shape[0]
    n_g, pg = pool_w.shape[0], pool_w.shape[1]
    t = _chunk(s)

    def body(d_ref, dy_ref, w_ref, sc_ref, du_ref, dw_ref, dsc_ref, epad_ref, dwacc_ref):
        g = pl.program_id(0)
        _zero_pads(epad_ref, s)
        dwacc_ref[...] = jnp.zeros_like(dwacc_ref)
        for gi, w in enumerate(POOL_WINDOWS):
            @pl.when(g == gi)
            def _():
                def first(ch, dsc):
                    t0 = pl.multiple_of(ch * t, t)
                    dd = d_ref[pl.ds(t0, t), :]
                    dyc = dy_ref[pl.ds(t0, t), :].astype(F32)
                    wv = w_ref[...]
                    ypre = jnp.dot(dd, wv, preferred_element_type=F32)
                    dq = (dyc * sc_ref[...]).astype(BF16)
                    dwacc_ref[...] += lax.dot_general(dd, dq, _DIMS["tn"], preferred_element_type=F32)
                    ddv = lax.dot_general(dq, wv, _DIMS["nt"], preferred_element_type=F32)
                    epad_ref[pl.ds(pl.multiple_of(PAD + t0, SUBLANES), t), :] = ddv * (1.0 / _pool_count(t0, t, s, w))
                    return dsc + _colsum(dyc * ypre)

                dsc_ref[...] = lax.fori_loop(0, s // t, first, jnp.zeros((1, pg), F32))

                def second(ch, carry):
                    t0 = pl.multiple_of(ch * t, t)
                    sup = _window(epad_ref, t0, t)
                    acc = _shift(sup, -(w // 2) + 1, t)
                    for o in range(-(w // 2) + 2, w // 2 + 1):
                        acc = acc + _shift(sup, o, t)
                    du_ref[pl.ds(t0, t), :] = (acc - _shift(sup, 0, t) * _pool_count(t0, t, s, w)).astype(BF16)
                    return carry

                lax.fori_loop(0, s // t, second, 0)

        dw_ref[...] = dwacc_ref[...].astype(BF16)

    blk = pl.BlockSpec((s, pg), lambda g: (0, g))
    w_spec = pl.BlockSpec((None, pg, pg), lambda g: (g, 0, 0))
    sc_spec = pl.BlockSpec((1, pg), lambda g: (0, g))
    res = _host_call(
        "pool_bwd", (n_g,), lambda ins, outs, scr: body(*ins, *outs, *scr), [dsv, dy, pool_w, pool_scale],
        [blk, blk, w_spec, sc_spec], [_sds((s, n_g * pg), BF16), _sds((n_g, pg, pg), BF16), _sds((1, n_g * pg), F32)],
        [blk, w_spec, sc_spec], [pltpu.VMEM((s + 2 * PAD, pg), F32), pltpu.VMEM((pg, pg), F32)], list(stages))
    return res if stages else res[0]


def _sigmoid(x):
    return 0.5 * jnp.tanh(0.5 * x) + 0.5


def _softplus(x):
    e = jnp.exp(-jnp.abs(x))
    log1p_e = jnp.where(e < 1e-2, e * (1.0 - e * (0.5 - e * (1.0 / 3.0))), jnp.log(1.0 + e))
    return jnp.maximum(x, 0.0) + log1p_e


_GELU_C = math.sqrt(2.0 / math.pi)


def _gelu(x):
    th = jnp.tanh(_GELU_C * (x + 0.044715 * x * x * x))
    return 0.5 * x * (1.0 + th), th


def _gelu_grad(x, th):
    return 0.5 * (1.0 + th) + 0.5 * x * (1.0 - th * th) * _GELU_C * (1.0 + 3.0 * 0.044715 * x * x)


def _scan_chunk(a_ref, b_ref, o_ref, o_off, carry, t, reverse):
    n = a_ref.shape[1]
    row = lax.broadcasted_iota(jnp.int32, (SUBLANES, n), 0)
    n_groups = t // SUBLANES
    unroll = math.gcd(n_groups, SCAN_UNROLL)
    last = 0 if reverse else SUBLANES - 1

    def step(si, carry):
        for u in range(unroll):
            gi = si * unroll + u
            g = n_groups - 1 - gi if reverse else gi
            r0 = pl.multiple_of(g * SUBLANES, SUBLANES)
            a = a_ref[pl.ds(r0, SUBLANES), :]
            b = b_ref[pl.ds(r0, SUBLANES), :]
            for k in (1, 2, 4):
                keep = row < SUBLANES - k if reverse else row >= k
                sh = SUBLANES - k if reverse else k
                ar = jnp.where(keep, pltpu.roll(a, sh, 0), 1.0)
                br = jnp.where(keep, pltpu.roll(b, sh, 0), 0.0)
                b = a * br + b
                a = a * ar
            o_ref[pl.ds(pl.multiple_of(o_off + r0, SUBLANES), SUBLANES), :] = a * carry + b
            carry = (jnp.broadcast_to(a[last:last + 1, :], a.shape) * carry
                     + jnp.broadcast_to(b[last:last + 1, :], b.shape))
        return carry

    return lax.fori_loop(0, n_groups // unroll, step, carry)


def _lru_params(pk_ref):
    rows = pk_ref[...]
    get = lambda i: rows[i:i + 1, :]
    cw = [get(k) for k in range(4)]
    lam = (get(9), get(10))
    big_l = tuple(-LRU_C * _softplus(-v) for v in lam)
    return cw, get(4), (get(5), get(6)), (get(7), get(8)), lam, big_l


def _conv(sup, cw, cb, t):
    xc = cb + cw[0] * _shift(sup, -2, t)
    for k in range(1, 4):
        xc = xc + cw[k] * _shift(sup, k - 2, t)
    return xc


def _gates(xcb, w_ref, d, bk, ba, bx, big_l):
    pre = jnp.dot(xcb, w_ref[:, pl.ds(d * 2 * bk, 2 * bk)], preferred_element_type=F32)
    r = _sigmoid(pre[:, :bk] + ba[d])
    i = _sigmoid(pre[:, bk:] + bx[d])
    la = big_l[d] * r
    a = jnp.exp(la)
    var = jnp.tanh(-la) * (1.0 + a * a)
    rs = lax.rsqrt(jnp.maximum(var, 1e-30))
    return r, i, a, var * rs, rs


def _lru_specs(s, d, bk):
    u_spec = pl.BlockSpec((s, bk), lambda h: (0, d // bk + h))
    ug_spec = pl.BlockSpec((s, bk), lambda h: (0, 2 * d // bk + h))
    w_spec = pl.BlockSpec((None, bk, 4 * bk), lambda h: (h, 0, 0))
    pk_spec = pl.BlockSpec((None, 16, bk), lambda h: (h, 0, 0))
    blk = pl.BlockSpec((s, bk), lambda h: (0, h))
    return u_spec, ug_spec, w_spec, pk_spec, blk


def _lru_fwd(z, gatew, pk, stages=()):
    s = z.shape[0]
    n_h, bk = gatew.shape[0], gatew.shape[1]
    d = n_h * bk
    t = _chunk(s)
    n_ch = s // t

    def body(u_ref, ug_ref, w_ref, pk_ref, y_ref, upad, h0buf, abuf, bbuf, xcbuf, h1buf):
        _zero_pads(upad, s)
        upad[pl.ds(PAD, s), :] = u_ref[...].astype(F32)
        cw, cb, ba, bx, _, big_l = _lru_params(pk_ref)
        zero = jnp.zeros((SUBLANES, bk), F32)

        def fill(xc, dr):
            _, i, a, sq, _ = _gates(xc.astype(BF16), w_ref, dr, bk, ba, bx, big_l)
            abuf[...] = a
            bbuf[...] = sq * i * xc

        def up(ch, carry):
            t0 = pl.multiple_of(ch * t, t)
            xc = _conv(_window(upad, t0, t), cw, cb, t)
            xcbuf[pl.ds(t0, t), :] = xc
            fill(xc, 0)
            return _scan_chunk(abuf, bbuf, h0buf, t0, carry, t, False)

        lax.fori_loop(0, n_ch, up, zero)

        def down(ci, carry):
            t0 = pl.multiple_of((n_ch - 1 - ci) * t, t)
            fill(xcbuf[pl.ds(t0, t), :], 1)
            carry = _scan_chunk(abuf, bbuf, h1buf, 0, carry, t, True)
            gl, _ = _gelu(ug_ref[pl.ds(t0, t), :].astype(F32))
            y_ref[pl.ds(t0, t), :] = ((h0buf[pl.ds(t0, t), :] + h1buf[...]) * gl).astype(BF16)
            return carry

        lax.fori_loop(0, n_ch, down, zero)

    u_spec, ug_spec, w_spec, pk_spec, blk = _lru_specs(s, d, bk)
    res = _host_call(
        "lru_fwd", (n_h,), lambda ins, outs, scr: body(*ins, *outs, *scr), [z, z, gatew, pk],
        [u_spec, ug_spec, w_spec, pk_spec], [_sds((s, d), BF16)], [blk],
        [pltpu.VMEM((s + 2 * PAD, bk), F32), pltpu.VMEM((s, bk), F32), pltpu.VMEM((t, bk), F32), pltpu.VMEM((t, bk), F32),
         pltpu.VMEM((s, bk), F32), pltpu.VMEM((t, bk), F32)], list(stages))
    return (res[0][0], res[1]) if stages else res[0][0]


def _lru_grads(lam_, hnb, a, sq, rs, r, i, xc, xcb, w_ref, dwacc, d, big_l, acc):
    bk = xc.shape[1]
    dba, dbx, dl = acc
    q = lam_ * i * xc
    dla = lam_ * hnb * a - q * (a * a) * rs
    dpr = dla * big_l * r * (1.0 - r)
    dpi = q * sq * (1.0 - i)
    dprb, dpib = dpr.astype(BF16), dpi.astype(BF16)
    c0 = d * 2 * bk
    dxc = (lam_ * sq * i
           + lax.dot_general(dprb, w_ref[:, pl.ds(c0, bk)], _DIMS["nt"], preferred_element_type=F32)
           + lax.dot_general(dpib, w_ref[:, pl.ds(c0 + bk, bk)], _DIMS["nt"], preferred_element_type=F32))
    dwacc[:, pl.ds(c0, bk)] += lax.dot_general(xcb, dprb, _DIMS["tn"], preferred_element_type=F32)
    dwacc[:, pl.ds(c0 + bk, bk)] += lax.dot_general(xcb, dpib, _DIMS["tn"], preferred_element_type=F32)
    return dxc, (dba + _colsum(dpr), dbx + _colsum(dpi), dl + _colsum(dla * r))


def _lru_bwd(z, dy, gatew, pk, stages=()):
    s = z.shape[0]
    n_h, bk = gatew.shape[0], gatew.shape[1]
    d = n_h * bk
    t = _chunk(s)
    n_ch = s // t

    def body(u_ref, ug_ref, dy_ref, w_ref, pk_ref, du_ref, dug_ref, dw_ref, dpk_ref,
             upad, h0pad, h1pad, dxpad, abuf, bbuf, lbuf, dwacc, edge, xcbuf):
        for ref in (upad, h0pad, h1pad, dxpad):
            _zero_pads(ref, s)
        upad[pl.ds(PAD, s), :] = u_ref[...].astype(F32)
        dwacc[...] = jnp.zeros_like(dwacc)
        cw, cb, ba, bx, lam, big_l = _lru_params(pk_ref)
        zero = jnp.zeros((SUBLANES, bk), F32)
        zrow = jnp.zeros((1, bk), F32)
        rowi = lax.broadcasted_iota(jnp.int32, (t, bk), 0)

        def at(t0):
            return pl.ds(pl.multiple_of(PAD + t0, SUBLANES), t)

        def conv_in(t0):
            xc = xcbuf[pl.ds(t0, t), :]
            return xc, xc.astype(BF16)

        def dh_of(t0):
            ug = ug_ref[pl.ds(t0, t), :].astype(F32)
            gl, th = _gelu(ug)
            dyv = dy_ref[pl.ds(t0, t), :].astype(F32)
            return dyv * gl, dyv * _gelu_grad(ug, th)

        def sweep1(ch, carry):
            t0 = pl.multiple_of(ch * t, t)
            xc = _conv(_window(upad, t0, t), cw, cb, t)
            xcbuf[pl.ds(t0, t), :] = xc
            _, i, a, sq, _ = _gates(xc.astype(BF16), w_ref, 0, bk, ba, bx, big_l)
            abuf[...] = a
            bbuf[...] = sq * i * xc
            return _scan_chunk(abuf, bbuf, h0pad, PAD + t0, carry, t, False)

        lax.fori_loop(0, n_ch, sweep1, zero)

        edge[...] = zero

        def sweep2(ci, st):
            carry_h, carry_l, acc = st
            t0 = pl.multiple_of((n_ch - 1 - ci) * t, t)
            xc, xcb = conv_in(t0)
            _, i1, a1, sq1, _ = _gates(xcb, w_ref, 1, bk, ba, bx, big_l)
            abuf[...] = a1
            bbuf[...] = sq1 * i1 * xc
            carry_h = _scan_chunk(abuf, bbuf, h1pad, PAD + t0, carry_h, t, True)
            dh, dgl = dh_of(t0)
            dug_ref[pl.ds(t0, t), :] = (dgl * (h0pad[at(t0), :] + h1pad[at(t0), :])).astype(BF16)
            r0, i0, a0, sq0, rs0 = _gates(xcb, w_ref, 0, bk, ba, bx, big_l)
            abuf[...] = jnp.where(rowi == t - 1, edge[0:1, :], pltpu.roll(a0, t - 1, 0))
            bbuf[...] = dh
            carry_l = _scan_chunk(abuf, bbuf, lbuf, 0, carry_l, t, True)
            edge[...] = jnp.broadcast_to(a0[0:1, :], (SUBLANES, bk))
            hprev = _shift(_window(h0pad, t0, t), -1, t)
            dxc, acc = _lru_grads(lbuf[...], hprev, a0, sq0, rs0, r0, i0, xc, xcb, w_ref, dwacc, 0, big_l[0], acc)
            dxpad[at(t0), :] = dxc
            return carry_h, carry_l, acc

        _, _, acc0 = lax.fori_loop(0, n_ch, sweep2, (zero, zero, (zrow, zrow, zrow)))

        edge[...] = zero

        def sweep3(ch, st):
            carry_l, acc = st
            t0 = pl.multiple_of(ch * t, t)
            xc, xcb = conv_in(t0)
            r1, i1, a1, sq1, rs1 = _gates(xcb, w_ref, 1, bk, ba, bx, big_l)
            dh, _ = dh_of(t0)
            abuf[...] = jnp.where(rowi == 0, edge[0:1, :], pltpu.roll(a1, 1, 0))
            bbuf[...] = dh
            carry_l = _scan_chunk(abuf, bbuf, lbuf, 0, carry_l, t, False)
            edge[...] = jnp.broadcast_to(a1[t - 1:t, :], (SUBLANES, bk))
            hnext = _shift(_window(h1pad, t0, t), 1, t)
            dxc, acc = _lru_grads(lbuf[...], hnext, a1, sq1, rs1, r1, i1, xc, xcb, w_ref, dwacc, 1, big_l[1], acc)
            dxpad[at(t0), :] += dxc
            return carry_l, acc

        _, acc1 = lax.fori_loop(0, n_ch, sweep3, (zero, (zrow, zrow, zrow)))

        def sweep4(ch, st):
            t0 = pl.multiple_of(ch * t, t)
            sdx = _window(dxpad, t0, t)
            su = _window(upad, t0, t)
            dxc = _shift(sdx, 0, t)
            du = cw[0] * _shift(sdx, 2, t) + cw[1] * _shift(sdx, 1, t) + cw[2] * dxc + cw[3] * _shift(sdx, -1, t)
            du_ref[pl.ds(t0, t), :] = du.astype(BF16)
            return tuple(st[k] + _colsum(dxc * _shift(su, k - 2, t)) for k in range(4)) + (st[4] + _colsum(dxc),)

        conv_g = lax.fori_loop(0, n_ch, sweep4, (zrow,) * 5)

        dpk_ref[...] = jnp.zeros_like(dpk_ref)
        rows = list(conv_g) + [acc0[0], acc1[0], acc0[1], acc1[1],
                               acc0[2] * LRU_C * _sigmoid(-lam[0]), acc1[2] * LRU_C * _sigmoid(-lam[1])]
        for k, v in enumerate(rows):
            dpk_ref[pl.ds(k, 1), :] = v
        dw_ref[...] = dwacc[...].astype(BF16)

    u_spec, ug_spec, w_spec, pk_spec, blk = _lru_specs(s, d, bk)
    padded = pltpu.VMEM((s + 2 * PAD, bk), F32)
    chunk = pltpu.VMEM((t, bk), F32)
    res = _host_call(
        "lru_bwd", (n_h,), lambda ins, outs, scr: body(*ins, *outs, *scr), [z, z, dy, gatew, pk],
        [u_spec, ug_spec, blk, w_spec, pk_spec],
        [_sds((s, d), BF16), _sds((s, d), BF16), _sds((n_h, bk, 4 * bk), BF16), _sds((n_h, 16, bk), F32)],
        [blk, blk, w_spec, pk_spec],
        [padded, padded, padded, padded, chunk, chunk, chunk, pltpu.VMEM((bk, 4 * bk), F32),
         pltpu.VMEM((SUBLANES, bk), F32), pltpu.VMEM((s, bk), F32)], list(stages))
    return res if stages else res[0]


def _scalar(v):
    return jnp.reshape(v, (1,)).astype(jnp.int32)


def _add_sibling(g, r, c):
    _, rows, cols = g.shape
    rh = rows // 2
    tr = _tile(rh, 512, 16)
    nr = rh // tr

    def body(c_ref, g_ref, r_ref, o_ref):
        o_ref[...] = (g_ref[...].astype(F32) + r_ref[...].astype(F32)).astype(BF16)

    spec = pl.BlockSpec((None, tr, cols), lambda k, i, c_ref: (k, i, 0))
    return pl.pallas_call(
        body, name="add_sibling", out_shape=_sds((N_CHIP, rh, cols), BF16),
        grid_spec=pltpu.PrefetchScalarGridSpec(
            num_scalar_prefetch=1, grid=(N_CHIP, nr),
            in_specs=[pl.BlockSpec((None, tr, cols), lambda k, i, c_ref: (k, c_ref[0] * nr + i, 0)), spec], out_specs=spec),
        compiler_params=_cparams(("arbitrary", "arbitrary")),
    )(_scalar(c), g, r)


def _sum_chips(p, rcv, k_me, c):
    _, rh, cols = p.shape
    tr = _tile(rh, 512, 16)
    nr = rh // tr

    def body(kc_ref, p_ref, r_ref, o_ref):
        acc = p_ref[...].astype(F32)
        for j in range(3):
            acc = acc + r_ref[j].astype(F32)
        o_ref[...] = acc

    return pl.pallas_call(
        body, name="sum_chips", out_shape=_sds((2 * rh, cols), F32),
        grid_spec=pltpu.PrefetchScalarGridSpec(
            num_scalar_prefetch=1, grid=(nr,),
            in_specs=[pl.BlockSpec((None, tr, cols), lambda i, kc_ref: (kc_ref[0], i, 0)),
                      pl.BlockSpec((3, tr, cols), lambda i, kc_ref: (0, i, 0))],
            out_specs=pl.BlockSpec((tr, cols), lambda i, kc_ref: (kc_ref[1] * nr + i, 0))),
        compiler_params=_cparams(("arbitrary",)),
    )(jnp.stack([k_me, c]).astype(jnp.int32), p, rcv)


def _sum_devices(g):
    def body(g_ref, o_ref):
        acc = g_ref[0]
        for dev in range(1, N_DEV):
            acc = acc + g_ref[dev]
        o_ref[...] = acc

    return pl.pallas_call(body, name="sum_devices", out_shape=_sds(g.shape[1:], F32))(g)


def _adamw(w, g, m, v):
    rows, cols = w.shape
    tr = _tile(rows, 256, SUBLANES)

    def body(ins, outs, scr):
        w_ref, g_ref, m_ref, v_ref = ins
        go_ref, d_ref, nm_ref, nv_ref = outs
        gv = g_ref[...]
        go_ref[...] = gv
        nm = ADAM_B1 * m_ref[...] + (1.0 - ADAM_B1) * gv
        nv = ADAM_B2 * v_ref[...] + (1.0 - ADAM_B2) * (gv * gv)
        m_hat = nm / (1.0 - ADAM_B1 ** ADAM_STEP)
        v_hat = nv / (1.0 - ADAM_B2 ** ADAM_STEP)
        d_ref[...] = -ADAM_LR * (m_hat / (jnp.sqrt(v_hat) + ADAM_EPS) + ADAM_WD * w_ref[...])
        nm_ref[...] = nm
        nv_ref[...] = nv

    spec = pl.BlockSpec((tr, cols), lambda i: (i, 0))
    return _host_call("adamw", (rows // tr,), body, [w, g, m, v], [spec] * 4, [_sds((rows, cols), F32)] * 4, [spec] * 4, [], [])[0]


def _pack(vs, unit):
    flat = jnp.concatenate([v.reshape(-1).astype(F32) for v in vs])
    pad = (-flat.shape[0]) % unit
    if pad:
        flat = jnp.concatenate([flat, jnp.zeros((pad,), F32)])
    return flat.reshape(-1, 128)


def _unpack(p, like):
    flat = p.reshape(-1)
    out, off = [], 0
    for v in like:
        n = math.prod(v.shape)
        out.append(flat[off:off + n].reshape(v.shape))
        off += n
    return out


def kernel(x, w_in, pool_w, pool_scale, conv_w, conv_b, lru_wa, lru_ba, lru_wx, lru_bx, lru_lambda, w_pool_up, w_lru_up, w_out, b_out, ln1_g, ln1_b, w_ff1, b_ff1, w_ff2, b_ff2, ln2_g, ln2_b, loss_target, m_w_in, m_pool_w, m_pool_scale, m_conv_w, m_conv_b, m_lru_wa, m_lru_ba, m_lru_wx, m_lru_bx, m_lru_lambda, m_w_pool_up, m_w_lru_up, m_w_out, m_b_out, m_ln1_g, m_ln1_b, m_w_ff1, m_b_ff1, m_w_ff2, m_b_ff2, m_ln2_g, m_ln2_b, v_w_in, v_pool_w, v_pool_scale, v_conv_w, v_conv_b, v_lru_wa, v_lru_ba, v_lru_wx, v_lru_bx, v_lru_lambda, v_w_pool_up, v_w_lru_up, v_w_out, v_b_out, v_ln1_g, v_ln1_b, v_w_ff1, v_b_ff1, v_w_ff2, v_b_ff2, v_ln2_g, v_ln2_b):
    given = dict(locals())
    wt = {n: given[n] for n in WEIGHTS}
    mom = {n: given["m_" + n] for n in WEIGHTS}
    vel = {n: given["v_" + n] for n in WEIGHTS}

    ix, iy, ic = _mesh_pos()
    k_me = 2 * ix + iy
    s, d = x.shape[1], x.shape[2]
    ds = d // N_CHIP
    n_g, pgs, pg = pool_w.shape[1], pool_w.shape[2], pool_w.shape[3]
    n_h, bks, bk = lru_wa.shape[2], lru_wa.shape[3], lru_wa.shape[4]
    f = b_ff1.shape[1]
    x2 = x[0]
    x_bf = x2.astype(BF16)
    vec = lambda a: a.reshape(1, -1)

    sharded_vecs = [conv_w[0], lru_ba[0], lru_bx[0], lru_lambda[0]]
    rows_sv = jnp.concatenate(sharded_vecs + [jnp.zeros((6, ds), F32)], axis=0)
    sv = _all_gather_small(rows_sv)
    sv = sv.reshape(N_CHIP, 2, 16, ds)[:, 0].transpose(1, 0, 2).reshape(16, d)
    conv_w_f, ba_f, bx_f, lam_f = sv[0:4], sv[4:6], sv[6:8], sv[8:10]
    pk = jnp.concatenate([conv_w_f, conv_b, ba_f, bx_f, lam_f, jnp.zeros((5, d), F32)], axis=0)
    pk = pk.reshape(16, n_h, bk).transpose(1, 0, 2)

    def gate_stack(wa, wx):
        return jnp.stack([wa[0], wx[0]], axis=1)

    mats = {
        "w_in": w_in[0], "w_pool_up": w_pool_up[0], "w_lru_up": w_lru_up[0], "w_out": w_out[0],
        "w_ff1": w_ff1[0], "w_ff2": w_ff2[0],
        "pool_w": pool_w[0].reshape(n_g * pgs, pg),
        "gate_w": gate_stack(lru_wa, lru_wx).reshape(4 * n_h * bks, bk),
    }
    names = list(mats)
    placed = {n: _cast_place(mats[n], k_me) for n in names}

    def add_sibling(gs, swapped):
        return [_add_sibling(g, r, ic) for g, r in zip(gs, swapped)]

    def sum_chips(ps, received):
        return [_sum_chips(p, r, k_me, ic) for p, r in zip(ps, received)]

    first = [placed[n] for n in ("w_in", "pool_w", "gate_w")]
    (bufs,) = _run_stages("gather_first", [_chain([_gather_direct(first), _gather_relay(first), _gather_d2d(first)])])
    wg_in = bufs[0]
    wf_pool = bufs[1].reshape(N_CHIP, n_g, pgs, pg).transpose(1, 0, 2, 3).reshape(n_g, pg, pg)
    wf_gate = bufs[2].reshape(N_CHIP, 2, 2, n_h, bks, bk).transpose(3, 0, 4, 1, 2, 5).reshape(n_h, bk, 4 * bk)

    z, (wb_mix, wb_ff1) = _fwd_in(x_bf, wg_in, stages=[
        _gather_direct([placed[n] for n in ("w_pool_up", "w_lru_up", "w_out")]), _gather_direct([placed["w_ff1"]])])
    (d_pool, y_pool), (wb_mix,) = _pool_fwd(z, wf_pool, pool_scale, stages=[_gather_relay(wb_mix)])
    y_lru, (wb_ff1, wb_mix, wb_ff2) = _lru_fwd(z, wf_gate, pk, stages=[
        _gather_relay(wb_ff1), _gather_d2d(wb_mix), _gather_direct([placed["w_ff2"]])])
    wf_pu, wf_lu, wf_out = (b.reshape(d, d) for b in wb_mix)
    (m_mix, p_a, p_b), (wb_ff1, wb_ff2) = _fwd_merge(y_pool, y_lru, wf_pu, wf_lu, z, stages=[
        _gather_d2d(wb_ff1), _gather_relay(wb_ff2)])
    wg_ff1 = wb_ff1[0]
    (xhat1, x1_bf, rstd1), (wb_ff2,) = _fwd_out_ln1(m_mix, wf_out, x2, b_out, ln1_g, ln1_b, stages=[_gather_d2d(wb_ff2)])
    hdn = _fwd_ff1(x1_bf, wg_ff1, b_ff1)
    wf_ff2 = wb_ff2[0].reshape(f, d)
    dr2, dr2_bf, g_ln2_g, g_ln2_b, g_b_ff2, loss_part = _fwd_ff2_ln2_loss(
        hdn, wf_ff2, xhat1, ln1_g, ln1_b, b_ff2, ln2_g, ln2_b, loss_target[0])

    dpre, g_b_ff1 = _bwd_ff2_in(dr2_bf, wf_ff2, hdn)
    g_ff = [_wgrad("wgrad_ff1", x1_bf, dpre, True), _wgrad("wgrad_ff2", hdn, dr2_bf, False)]
    (dr1, dr1_bf, g_ln1_g, g_ln1_b, g_b_out), (swapped,) = _bwd_ff1_in_ln1(
        dpre, wg_ff1, dr2, xhat1, rstd1, ln1_g, stages=[_swap_halves(g_ff)])
    sums_ff = add_sibling(g_ff, swapped)
    dp_a, dp_b, dg_a, dg_b = _bwd_out_in(dr1_bf, wf_out, z, p_a, p_b)
    dy_pool = _bwd_up_in("bwd_pool_up_in", dp_a, wf_pu)
    dy_lru = _bwd_up_in("bwd_lru_up_in", dp_b, wf_lu)
    g_mix = [_wgrad("wgrad_pool_up", y_pool, dp_a, False), _wgrad("wgrad_lru_up", y_lru, dp_b, False),
             _wgrad("wgrad_out", m_mix, dr1_bf, False)]
    (du_pool, g_pool_w, g_pool_scale), (swapped,) = _pool_bwd(
        d_pool, dy_pool, wf_pool, pool_scale, stages=[_swap_halves(g_mix)])
    sums_mix = add_sibling(g_mix, swapped)
    (du_lru, du_gate, g_gate_w, g_pk), (recv_ff, recv_mix) = _lru_bwd(
        z, dy_lru, wf_gate, pk, stages=[_scatter_chips(sums_ff), _scatter_chips(sums_mix)])
    halves = sum_chips(sums_ff + sums_mix, recv_ff + recv_mix)
    g_small = [g_pool_w.reshape(n_g, N_CHIP, pgs, pg).transpose(1, 0, 2, 3).reshape(N_CHIP, n_g * pgs, pg),
               g_gate_w.reshape(n_h, N_CHIP, bks, 2, 2, bk).transpose(1, 3, 4, 0, 2, 5).reshape(N_CHIP, 4 * n_h * bks, bk)]
    dz = jnp.concatenate([du_pool, du_lru, du_gate, dg_a, dg_b], axis=1)
    g_in, (joined, swapped) = _wgrad("wgrad_in", x_bf, dz, True, stages=[_join_halves(halves), _swap_halves(g_small)])
    g_mat = dict(zip(["w_ff1", "w_ff2", "w_pool_up", "w_lru_up", "w_out"], joined))
    sums_small = add_sibling(g_small, swapped)

    def stacked(tree):
        return gate_stack(tree["lru_wa"], tree["lru_wx"]).reshape(4 * n_h * bks, bk)

    res = {}

    def update(n):
        if n == "gate_w":
            outs = [o.reshape(2, 2, n_h, bks, bk) for o in _adamw(stacked(wt), g_mat[n], stacked(mom), stacked(vel))]
            res["lru_wa"] = [o[:, 0][None] for o in outs]
            res["lru_wx"] = [o[:, 1][None] for o in outs]
        else:
            shp2 = mats[n].shape
            outs = _adamw(wt[n].reshape(shp2), g_mat[n], mom[n].reshape(shp2), vel[n].reshape(shp2))
            res[n] = [o.reshape(wt[n].shape) for o in outs]

    (swapped,) = _run_stages("swap_in", [_swap_halves([g_in])])
    sums_in = add_sibling([g_in], swapped)
    grad_x, (recv_small, recv_in) = _bwd_in(dz, wg_in, dr1, stages=[_scatter_chips(sums_small), _scatter_chips(sums_in)])
    halves = sum_chips(sums_small + sums_in, recv_small + recv_in)
    (joined,) = _run_stages("join_last", [_join_halves(halves)])
    g_mat.update(zip(["pool_w", "gate_w", "w_in"], joined))
    for n in names:
        update(n)

    g_pk = g_pk.transpose(1, 0, 2).reshape(16, d)
    vec_full = {
        "pool_scale": g_pool_scale, "conv_w": g_pk[0:4], "conv_b": g_pk[4:5],
        "lru_ba": g_pk[5:7], "lru_bx": g_pk[7:9], "lru_lambda": g_pk[9:11],
        "b_out": g_b_out, "ln1_g": g_ln1_g, "ln1_b": g_ln1_b, "b_ff1": g_b_ff1, "b_ff2": g_b_ff2,
        "ln2_g": g_ln2_g, "ln2_b": g_ln2_b,
    }
    vnames = list(vec_full)
    vg = _sum_devices(_all_gather_small(_pack([vec_full[n] for n in vnames], 1024)))
    vg = dict(zip(vnames, _unpack(vg, [vec_full[n] for n in vnames])))
    for n in ("conv_w", "lru_ba", "lru_bx", "lru_lambda"):
        vg[n] = lax.dynamic_slice_in_dim(vg[n], k_me * ds, ds, axis=1)
    vg = {n: vg[n].reshape(wt[n].shape) for n in vnames}
    upd = _adamw(_pack([wt[n] for n in vnames], 1024), _pack([vg[n] for n in vnames], 1024),
                 _pack([mom[n] for n in vnames], 1024), _pack([vel[n] for n in vnames], 1024))
    upd = [_unpack(u, [wt[n] for n in vnames]) for u in upd]
    for i, n in enumerate(vnames):
        res[n] = [vg[n], upd[1][i], upd[2][i], upd[3][i]]

    loss = lax.psum(loss_part[0, 0], ("x", "y", "c"))
    return (loss, grad_x[None], *[res[n][0] for n in WEIGHTS], *[res[n][1] for n in WEIGHTS],
            *[res[n][2] for n in WEIGHTS], *[res[n][3] for n in WEIGHTS])
```

```python
import functools
import math

import jax
import jax.numpy as jnp
from jax import lax
from jax.experimental import pallas as pl
from jax.experimental.pallas import tpu as pltpu

F32 = jnp.float32
BF16 = jnp.bfloat16
MESH = pl.DeviceIdType.MESH
ANY = pl.BlockSpec(memory_space=pl.ANY)

N_CHIP = 4
N_DEV = 8
VMEM_LIMIT_BYTES = 56 * 1024 * 1024
SUBLANES = 8
PAD = 8
SCAN_UNROLL = 8

POOL_WINDOWS = (2, 4, 8, 16)
LRU_C = 8.0
DN_ALPHA = 2.0 ** 0.25
LN_EPS = 1e-5
ADAM_LR, ADAM_B1, ADAM_B2, ADAM_EPS, ADAM_WD, ADAM_STEP = 0.001, 0.9, 0.999, 1e-08, 0.01, 10

WEIGHTS = ("w_in", "pool_w", "pool_scale", "conv_w", "conv_b", "lru_wa", "lru_ba", "lru_wx", "lru_bx", "lru_lambda",
           "w_pool_up", "w_lru_up", "w_out", "b_out", "ln1_g", "ln1_b", "w_ff1", "b_ff1", "w_ff2", "b_ff2", "ln2_g", "ln2_b")


def _cparams(sem=None):
    return pltpu.CompilerParams(dimension_semantics=sem, vmem_limit_bytes=VMEM_LIMIT_BYTES)


def _tile(dim, pref, unit=128):
    if dim <= pref:
        return dim
    t = (pref // unit) * unit
    while t > unit and dim % t:
        t -= unit
    assert dim % t == 0, (dim, pref)
    return t


def _mesh_pos():
    x, y, c = lax.axis_index("x"), lax.axis_index("y"), lax.axis_index("c")
    return x, y, c


def _other_chips(x, y):
    return [(1 - x, y), (x, 1 - y), (1 - x, 1 - y)]


def _all_gather_small(v):
    m_per, n = v.shape

    def body(x_ref, out_ref, send_sems, recv_sems, local_sem):
        x, y, c = _mesh_pos()
        me, sibling = (x, y, c), (x, y, 1 - c)
        chips = _other_chips(x, y)

        def rows(px, py, pc):
            return out_ref.at[4 * px + 2 * py + pc]

        def copy(k, block, to, src=None):
            return pltpu.make_async_remote_copy(
                src_ref=rows(*block) if src is None else src, dst_ref=rows(*block),
                send_sem=send_sems.at[k], recv_sem=recv_sems.at[k], device_id=to, device_id_type=MESH)

        mine = pltpu.make_async_copy(x_ref, rows(*me), local_sem)
        mine.start()
        first = [copy(0, me, sibling, src=x_ref)]
        first += [copy(1 + j, me, (*chip, c), src=x_ref) for j, chip in enumerate(chips)]
        for cp in first:
            cp.start()
        passed = [copy(4 + j, (*chip, c), sibling) for j, chip in enumerate(chips)]
        for j, chip in enumerate(chips):
            copy(1 + j, (*chip, c), me).wait_recv()
            passed[j].start()
        copy(0, sibling, me).wait_recv()
        for j, chip in enumerate(chips):
            copy(4 + j, (*chip, 1 - c), me).wait_recv()
        for cp in first + passed:
            cp.wait_send()
        mine.wait()

    return pl.pallas_call(
        body, name="all_gather_small",
        out_shape=jax.ShapeDtypeStruct((N_DEV, m_per, n), v.dtype),
        in_specs=[pl.BlockSpec(memory_space=pltpu.VMEM)],
        out_specs=pl.BlockSpec(memory_space=pltpu.VMEM),
        scratch_shapes=[pltpu.SemaphoreType.DMA((7,)), pltpu.SemaphoreType.DMA((7,)), pltpu.SemaphoreType.DMA],
    )(v)


class _Stage:
    def __init__(self, srcs, bufs, news, n_sems, copies):
        self.srcs, self.bufs, self.news, self.n_sems, self.copies = list(srcs), list(bufs), list(news), n_sems, copies
        self.phases = [(copies, 0)]


class _SemsFrom:
    def __init__(self, ref, offset):
        self.ref, self.offset, self.at = ref, offset, self

    def __getitem__(self, s):
        return self.ref.at[self.offset + s]


def _chain(stages):
    chained = _Stage([], stages[0].bufs, [], sum(st.n_sems for st in stages), None)
    chained.phases, first = [], 0
    for st in stages:
        chained.phases.append((st.copies, first))
        first += st.n_sems
    return chained


def _remote(src, dst, send_sems, recv_sems, s, to):
    return pltpu.make_async_remote_copy(src_ref=src, dst_ref=dst, send_sem=send_sems.at[s], recv_sem=recv_sems.at[s],
                                        device_id=to, device_id_type=MESH)


def _stage_operands(stages, n_in, n_out):
    ins, outs, aliases, scratch = [], [], {}, []
    for st in stages:
        for i in range(len(st.bufs)):
            aliases[n_in + len(ins) + len(st.srcs) + i] = n_out + len(outs) + i
        ins += st.srcs + st.bufs
        outs += [jax.ShapeDtypeStruct(b.shape, b.dtype) for b in st.bufs] + st.news
        scratch += [pltpu.SemaphoreType.DMA((st.n_sems,)), pltpu.SemaphoreType.DMA((st.n_sems,))]
    return ins, outs, aliases, scratch


def _stage_refs(stages, in_refs, out_refs, sem_refs):
    parts, i, o = [], 0, 0
    for n, st in enumerate(stages):
        src = in_refs[i:i + len(st.srcs)]
        i += len(st.srcs) + len(st.bufs)
        buf = out_refs[o:o + len(st.bufs)]
        new = out_refs[o + len(st.bufs):o + len(st.bufs) + len(st.news)]
        o += len(st.bufs) + len(st.news)
        parts.append((src, buf, new, sem_refs[2 * n], sem_refs[2 * n + 1]))
    return parts


def _stage_results(stages, res):
    out, o = [], 0
    for st in stages:
        n = len(st.bufs) + len(st.news)
        out.append(list(res[o:o + n]))
        o += n
    return out


def _stages_start(stages, parts):
    for st, part in zip(stages, parts):
        for cp in st.copies(*part)[0]:
            cp.start()


def _stages_wait(stages, parts):
    for st, part in zip(stages, parts):
        started, landing = st.copies(*part)
        for cp in landing:
            cp.wait_recv()
        for cp in started:
            cp.wait_send()


def _run_stages(name, stages):
    ins, outs, aliases, scratch = _stage_operands(stages, 0, 0)

    def body(*refs):
        parts = _stage_refs(stages, refs[:len(ins)], refs[len(ins):len(ins) + len(outs)], refs[len(ins) + len(outs):])
        for st, (src, buf, new, send_sems, recv_sems) in zip(stages, parts):
            for copies, first in st.phases:
                started, landing = copies(src, buf, new, _SemsFrom(send_sems, first), _SemsFrom(recv_sems, first))
                for cp in started:
                    cp.start()
                for cp in landing:
                    cp.wait_recv()
                for cp in started:
                    cp.wait_send()

    res = pl.pallas_call(
        body, name=name, out_shape=outs, in_specs=[ANY] * len(ins), out_specs=[ANY] * len(outs),
        input_output_aliases=aliases, scratch_shapes=scratch)(*ins)
    return _stage_results(stages, res)


def _gather_direct(ts):
    def copies(src, buf, new, send_sems, recv_sems):
        x, y, c = _mesh_pos()
        started, landing = [], []
        for t in range(len(ts)):
            rh = ts[t].shape[1] // 2
            rows = pl.ds(c * rh, rh)
            mine = buf[t].at[2 * x + y, rows]
            for j, chip in enumerate(_other_chips(x, y)[:2]):
                theirs = buf[t].at[2 * chip[0] + chip[1], rows]
                started.append(_remote(mine, mine, send_sems, recv_sems, 2 * t + j, (*chip, c)))
                landing.append(_remote(theirs, theirs, send_sems, recv_sems, 2 * t + j, (x, y, c)))
        return started, landing

    return _Stage([], ts, [], 2 * len(ts), copies)


def _gather_relay(ts):
    def copies(src, buf, new, send_sems, recv_sems):
        x, y, c = _mesh_pos()
        (x_nb, y_nb, diag) = _other_chips(x, y)
        block = lambda chip: 2 * chip[0] + chip[1]
        started, landing = [], []
        for t in range(len(ts)):
            rq = ts[t].shape[1] // 4
            q0, q1 = pl.ds(2 * c * rq, rq), pl.ds((2 * c + 1) * rq, rq)
            from_y, from_x = buf[t].at[block(y_nb), q0], buf[t].at[block(x_nb), q1]
            started.append(_remote(from_y, from_y, send_sems, recv_sems, 2 * t, (*x_nb, c)))
            started.append(_remote(from_x, from_x, send_sems, recv_sems, 2 * t + 1, (*y_nb, c)))
            for j, q in enumerate((q0, q1)):
                lands = buf[t].at[block(diag), q]
                landing.append(_remote(lands, lands, send_sems, recv_sems, 2 * t + j, (x, y, c)))
        return started, landing

    return _Stage([], ts, [], 2 * len(ts), copies)


def _gather_d2d(ts):
    def copies(src, buf, new, send_sems, recv_sems):
        x, y, c = _mesh_pos()
        started, landing = [], []
        for t in range(len(ts)):
            rh = ts[t].shape[1] // 2
            for j, chip in enumerate(_other_chips(x, y)):
                got = buf[t].at[2 * chip[0] + chip[1], pl.ds(c * rh, rh)]
                other = buf[t].at[2 * chip[0] + chip[1], pl.ds((1 - c) * rh, rh)]
                started.append(_remote(got, got, send_sems, recv_sems, 3 * t + j, (x, y, 1 - c)))
                landing.append(_remote(other, other, send_sems, recv_sems, 3 * t + j, (x, y, c)))
        return started, landing

    return _Stage([], ts, [], 3 * len(ts), copies)


def _swap_halves(gs):
    def copies(src, buf, new, send_sems, recv_sems):
        x, y, c = _mesh_pos()
        started, landing = [], []
        for t in range(len(gs)):
            rh = gs[t].shape[1] // 2
            started.append(_remote(src[t].at[:, pl.ds((1 - c) * rh, rh)], new[t], send_sems, recv_sems, t, (x, y, 1 - c)))
            landing.append(_remote(new[t], new[t], send_sems, recv_sems, t, (x, y, c)))
        return started, landing

    news = [jax.ShapeDtypeStruct((g.shape[0], g.shape[1] // 2, g.shape[2]), g.dtype) for g in gs]
    return _Stage(gs, [], news, len(gs), copies)


def _scatter_chips(ps):
    def copies(src, buf, new, send_sems, recv_sems):
        x, y, c = _mesh_pos()
        started, landing = [], []
        for t in range(len(ps)):
            for j, chip in enumerate(_other_chips(x, y)):
                started.append(_remote(src[t].at[2 * chip[0] + chip[1]], new[t].at[j], send_sems, recv_sems, 3 * t + j, (*chip, c)))
                landing.append(_remote(new[t].at[j], new[t].at[j], send_sems, recv_sems, 3 * t + j, (x, y, c)))
        return started, landing

    return _Stage(ps, [], [jax.ShapeDtypeStruct((3,) + p.shape[1:], p.dtype) for p in ps], 3 * len(ps), copies)


def _join_halves(fs):
    def copies(src, buf, new, send_sems, recv_sems):
        x, y, c = _mesh_pos()
        started, landing = [], []
        for t in range(len(fs)):
            rh = fs[t].shape[0] // 2
            mine = buf[t].at[pl.ds(c * rh, rh)]
            theirs = buf[t].at[pl.ds((1 - c) * rh, rh)]
            started.append(_remote(mine, mine, send_sems, recv_sems, t, (x, y, 1 - c)))
            landing.append(_remote(theirs, theirs, send_sems, recv_sems, t, (x, y, c)))
        return started, landing

    return _Stage([], fs, [], len(fs), copies)


def _cast_place(w, k_me):
    rows, cols = w.shape
    tr = _tile(rows, 512, 16)

    def body(k_ref, w_ref, o_ref):
        o_ref[...] = w_ref[...].astype(BF16)

    return pl.pallas_call(
        body, name="cast_place", out_shape=_sds((N_CHIP, rows, cols), BF16),
        grid_spec=pltpu.PrefetchScalarGridSpec(
            num_scalar_prefetch=1, grid=(rows // tr,),
            in_specs=[pl.BlockSpec((tr, cols), lambda i, k_ref: (i, 0))],
            out_specs=pl.BlockSpec((None, tr, cols), lambda i, k_ref: (k_ref[0], i, 0))),
        compiler_params=_cparams(("arbitrary",)),
    )(_scalar(k_me), w)


_DIMS = {"nn": (((1,), (0,)), ((), ())), "nt": (((1,), (1,)), ((), ())), "tn": (((0,), (0,)), ((), ()))}


def _accum(ref, val, first):
    @pl.when(first)
    def _():
        ref[...] = val

    @pl.when(jnp.logical_not(first))
    def _():
        ref[...] += val


def _grid_edges(grid):
    ids = [pl.program_id(ax) for ax in range(len(grid))]
    first = functools.reduce(jnp.logical_and, [i == 0 for i in ids])
    last = functools.reduce(jnp.logical_and, [i == n - 1 for i, n in zip(ids, grid)])
    return first, last


def _host_call(name, grid, body, operands, in_specs, out_shape, out_specs, scratch, stages):
    s_ins, s_outs, aliases, s_scratch = _stage_operands(stages, len(operands), len(out_shape))
    n_in, n_out, n_scr = len(operands), len(out_shape), len(scratch)

    def full_body(*refs):
        in_refs = refs[:n_in]
        s_in_refs = refs[n_in:n_in + len(s_ins)]
        o0 = n_in + len(s_ins)
        out_refs = refs[o0:o0 + n_out]
        s_out_refs = refs[o0 + n_out:o0 + n_out + len(s_outs)]
        c0 = o0 + n_out + len(s_outs)
        scr_refs = refs[c0:c0 + n_scr]
        if stages:
            parts = _stage_refs(stages, s_in_refs, s_out_refs, refs[c0 + n_scr:])
            first, last = _grid_edges(grid)
            pl.when(first)(lambda: _stages_start(stages, parts))
        body(in_refs, out_refs, scr_refs)
        if stages:
            pl.when(last)(lambda: _stages_wait(stages, parts))

    res = pl.pallas_call(
        full_body, name=name, grid=grid, in_specs=list(in_specs) + [ANY] * len(s_ins),
        out_specs=list(out_specs) + [ANY] * len(s_outs), out_shape=list(out_shape) + s_outs,
        input_output_aliases=aliases, scratch_shapes=list(scratch) + s_scratch,
        compiler_params=_cparams(("arbitrary",) * len(grid)),
    )(*operands, *s_ins)
    return list(res[:n_out]), _stage_results(stages, res[n_out:])


def _matmul(name, grid, pairs, extras, outs, acc_shape, epilogue, stages=()):
    n_p = len(pairs)
    n_k = grid[-1]
    dims = [_DIMS[p[4]] for p in pairs]

    def body(in_refs, out, accs):
        ab, ex = in_refs[:2 * n_p], in_refs[2 * n_p:]
        ids = [pl.program_id(ax) for ax in range(len(grid))]
        k = ids[-1]

        @pl.when(k == 0)
        def _():
            for acc in accs:
                acc[...] = jnp.zeros_like(acc)

        for p in range(n_p):
            a = ab[2 * p][...].astype(BF16)
            b = ab[2 * p + 1][...].astype(BF16)
            accs[p][...] += lax.dot_general(a, b, dims[p], preferred_element_type=F32)

        @pl.when(k == n_k - 1)
        def _():
            epilogue([acc[...] for acc in accs], ex, out, ids)

    in_specs = []
    operands = []
    for a, a_spec, b, b_spec, _ in pairs:
        in_specs += [a_spec, b_spec]
        operands += [a, b]
    for e, e_spec in extras:
        in_specs.append(e_spec)
        operands.append(e)
    res, stage_res = _host_call(name, grid, body, operands, in_specs, [o[0] for o in outs], [o[1] for o in outs],
                                [pltpu.VMEM(acc_shape, F32) for _ in pairs], list(stages))
    return (res, stage_res) if stages else res


def _out(res, stages, single=False):
    outs = res[0] if stages else res
    outs = outs[0] if single else outs
    return (outs, res[1]) if stages else outs


def _sds(shape, dtype):
    return jax.ShapeDtypeStruct(shape, dtype)


def _row(n):
    return pl.BlockSpec((1, n), lambda *_: (0, 0))


def _layer_norm(r):
    mu = jnp.mean(r, axis=-1, keepdims=True)
    xc = r - mu
    var = jnp.mean(xc * xc, axis=-1, keepdims=True)
    rstd = lax.rsqrt(var + LN_EPS)
    return xc * rstd, rstd


def _layer_norm_bwd(dxhat, xhat, rstd):
    m1 = jnp.mean(dxhat, axis=-1, keepdims=True)
    m2 = jnp.mean(dxhat * xhat, axis=-1, keepdims=True)
    return rstd * (dxhat - m1 - xhat * m2)


def _colsum(v):
    return jnp.sum(v, axis=0, keepdims=True)


ROW_TILE = 256


def _rows_call(name, s, epi, ins, outs):
    tr = _tile(s, ROW_TILE, SUBLANES)

    def spec(shape, kind):
        n = shape[1]
        return pl.BlockSpec((tr, n), lambda i: (i, 0)) if kind == "tile" else pl.BlockSpec((1, n), lambda i: (0, 0))

    def body(in_refs, out_refs, scr):
        epi([in_refs[0][...]], in_refs[1:], out_refs, [pl.program_id(0)])

    return _host_call(name, (s // tr,), body, [a for a, _ in ins], [spec(a.shape, k) for a, k in ins],
                      [o for o, _ in outs], [spec(o.shape, k) for o, k in outs], [], [])[0]


def _plain_matmul(name, a, b, mode):
    m, k_dim = a.shape
    n = b.shape[1]
    tm, tn, tk = _tile(m, 1024), _tile(n, 1024), _tile(k_dim, 2048)

    def epi(accs, ex, out, ids):
        out[0][...] = accs[0]

    assert mode == "nn"
    return _matmul(
        name, (m // tm, n // tn, k_dim // tk),
        [(a, pl.BlockSpec((tm, tk), lambda i, j, k: (i, k)), b, pl.BlockSpec((tk, tn), lambda i, j, k: (k, j)), "nn")],
        [], [(_sds((m, n), F32), pl.BlockSpec((tm, tn), lambda i, j, k: (i, j)))], (tm, tn), epi)[0]


def _fwd_in(x_in, wg_in, stages=()):
    s, d = x_in.shape
    inc = wg_in.shape[2]
    tm, tn, tk = _tile(s, 1024), _tile(inc, 1280), _tile(d, 2048)
    nb = inc // tn

    def epi(accs, ex, out, ids):
        out[0][...] = accs[0].astype(BF16)

    return _out(_matmul(
        "fwd_in", (s // tm, N_CHIP * nb, d // tk),
        [(x_in, pl.BlockSpec((tm, tk), lambda i, j, k: (i, k)),
          wg_in, pl.BlockSpec((None, tk, tn), lambda i, j, k: (j // nb, k, j % nb)), "nn")],
        [], [(_sds((s, N_CHIP * inc), BF16), pl.BlockSpec((tm, tn), lambda i, j, k: (i, j)))],
        (tm, tn), epi, stages), stages, True)


def _fwd_merge(y_pool, y_lru, w_pu, w_lu, z, stages=()):
    s, d = y_pool.shape
    tm, tn, tk = _tile(s, 1024), _tile(d, 1024), _tile(d, 1024)
    ga0, gb0 = 3 * d // tn, 4 * d // tn

    def epi(accs, ex, out, ids):
        sa = _sigmoid(ex[0][...].astype(F32))
        sb = _sigmoid(ex[1][...].astype(F32))
        out[0][...] = (sa * accs[0] + sb * accs[1]).astype(BF16)
        out[1][...] = accs[0].astype(BF16)
        out[2][...] = accs[1].astype(BF16)

    a_spec = pl.BlockSpec((tm, tk), lambda i, j, k: (i, k))
    b_spec = pl.BlockSpec((tk, tn), lambda i, j, k: (k, j))
    o_spec = pl.BlockSpec((tm, tn), lambda i, j, k: (i, j))
    return _out(_matmul(
        "fwd_merge", (s // tm, d // tn, d // tk),
        [(y_pool, a_spec, w_pu, b_spec, "nn"), (y_lru, a_spec, w_lu, b_spec, "nn")],
        [(z, pl.BlockSpec((tm, tn), lambda i, j, k: (i, ga0 + j))), (z, pl.BlockSpec((tm, tn), lambda i, j, k: (i, gb0 + j)))],
        [(_sds((s, d), BF16), o_spec)] * 3, (tm, tn), epi, stages), stages)


def _fwd_out_ln1(m, w_out, x, b_out, g1, b1, stages=()):
    s, d = x.shape
    tm, tk = _tile(s, 512), _tile(d, 2048)

    def epi(accs, ex, out, ids):
        r = DN_ALPHA * ex[0][...] + accs[0] + ex[1][...]
        xhat, rstd = _layer_norm(r)
        out[0][...] = xhat
        out[1][...] = (xhat * ex[2][...] + ex[3][...]).astype(BF16)
        out[2][...] = rstd

    full = pl.BlockSpec((tm, d), lambda i, j, k: (i, 0))
    return _out(_matmul(
        "fwd_out_ln1", (s // tm, 1, d // tk),
        [(m, pl.BlockSpec((tm, tk), lambda i, j, k: (i, k)), w_out, pl.BlockSpec((tk, d), lambda i, j, k: (k, 0)), "nn")],
        [(x, full), (b_out, _row(d)), (g1, _row(d)), (b1, _row(d))],
        [(_sds((s, d), F32), full), (_sds((s, d), BF16), full), (_sds((s, 1), F32), pl.BlockSpec((tm, 1), lambda i, j, k: (i, 0)))],
        (tm, d), epi, stages), stages)


def _fwd_ff1(x1_bf, wg_ff1, b_ff1, stages=()):
    s, d = x1_bf.shape
    fc = wg_ff1.shape[2]
    tm, tn, tk = _tile(s, 1024), _tile(fc, 1024), _tile(d, 2048)
    nb = fc // tn

    def epi(accs, ex, out, ids):
        p = jnp.maximum(accs[0] + ex[0][...], 0.0)
        out[0][...] = (p * p).astype(BF16)

    return _out(_matmul(
        "fwd_ff1", (s // tm, N_CHIP * nb, d // tk),
        [(x1_bf, pl.BlockSpec((tm, tk), lambda i, j, k: (i, k)),
          wg_ff1, pl.BlockSpec((None, tk, tn), lambda i, j, k: (j // nb, k, j % nb)), "nn")],
        [(b_ff1, pl.BlockSpec((1, tn), lambda i, j, k: (0, j)))],
        [(_sds((s, N_CHIP * fc), BF16), pl.BlockSpec((tm, tn), lambda i, j, k: (i, j)))],
        (tm, tn), epi, stages), stages, True)


def _fwd_ff2_ln2_loss(hdn, w_ff2, xhat1, g1, b1, b_ff2, g2, b2, target):
    s, f = hdn.shape
    d = xhat1.shape[1]

    def epi(accs, ex, out, ids):
        first = ids[0] == 0
        x1 = ex[0][...] * ex[1][...] + ex[2][...]
        r = DN_ALPHA * x1 + accs[0] + ex[3][...]
        xhat, rstd = _layer_norm(r)
        g2v = ex[4][...]
        err = xhat * g2v + ex[5][...] - ex[6][...]
        part = 0.5 * jnp.sum(jnp.mean(err * err, axis=-1, keepdims=True), axis=0, keepdims=True)
        dy = err * (1.0 / d)
        dr2 = _layer_norm_bwd(dy * g2v, xhat, rstd)
        out[0][...] = dr2
        out[1][...] = dr2.astype(BF16)
        _accum(out[2], _colsum(dy * xhat), first)
        _accum(out[3], _colsum(dy), first)
        _accum(out[4], _colsum(dr2), first)
        _accum(out[5], jnp.broadcast_to(part, (1, 128)), first)

    ff = _plain_matmul("fwd_ff2", hdn, w_ff2, "nn")
    vec = lambda n: (_sds((1, n), F32), "vec")
    return _rows_call(
        "ln2_loss", s, epi,
        [(ff, "tile"), (xhat1, "tile"), (g1, "vec"), (b1, "vec"), (b_ff2, "vec"), (g2, "vec"), (b2, "vec"), (target, "tile")],
        [(_sds((s, d), F32), "tile"), (_sds((s, d), BF16), "tile"), vec(d), vec(d), vec(d), vec(128)])


def _bwd_ff2_in(dr2_bf, w_ff2, hdn, stages=()):
    s, d = dr2_bf.shape
    f = hdn.shape[1]
    tm, tn, tk = _tile(s, 1024), _tile(f, 1024), _tile(d, 2048)

    def epi(accs, ex, out, ids):
        dpre = accs[0] * (2.0 * jnp.sqrt(ex[0][...].astype(F32)))
        out[0][...] = dpre.astype(BF16)
        _accum(out[1], _colsum(dpre), ids[1] == 0)

    return _out(_matmul(
        "bwd_ff2_in", (f // tn, s // tm, d // tk),
        [(dr2_bf, pl.BlockSpec((tm, tk), lambda j, i, k: (i, k)), w_ff2, pl.BlockSpec((tn, tk), lambda j, i, k: (j, k)), "nt")],
        [(hdn, pl.BlockSpec((tm, tn), lambda j, i, k: (i, j)))],
        [(_sds((s, f), BF16), pl.BlockSpec((tm, tn), lambda j, i, k: (i, j))), (_sds((1, f), F32), pl.BlockSpec((1, tn), lambda j, i, k: (0, j)))],
        (tm, tn), epi, stages), stages)


def _bwd_ff1_in_ln1(dpre, wg_ff1, dr2, xhat1, rstd1, g1, stages=()):
    s, f = dpre.shape
    d = xhat1.shape[1]
    fc = wg_ff1.shape[2]
    tm, tn, tk = _tile(s, 1024), _tile(d, 1024), _tile(fc, 2048)
    nb = fc // tk

    def epi(accs, ex, out, ids):
        first = ids[0] == 0
        xhat = ex[1][...]
        dx1 = accs[0] + DN_ALPHA * ex[0][...]
        dr1 = _layer_norm_bwd(dx1 * ex[3][...], xhat, ex[2][...])
        out[0][...] = dr1
        out[1][...] = dr1.astype(BF16)
        _accum(out[2], _colsum(dx1 * xhat), first)
        _accum(out[3], _colsum(dx1), first)
        _accum(out[4], _colsum(dr1), first)

    def plain(accs, ex, out, ids):
        out[0][...] = accs[0]

    o_spec = pl.BlockSpec((tm, tn), lambda i, j, k: (i, j))
    mm = _out(_matmul(
        "bwd_ff1_in", (s // tm, d // tn, f // tk),
        [(dpre, pl.BlockSpec((tm, tk), lambda i, j, k: (i, k)),
          wg_ff1, pl.BlockSpec((None, tn, tk), lambda i, j, k: (k // nb, j, k % nb)), "nt")],
        [], [(_sds((s, d), F32), o_spec)], (tm, tn), plain, stages), stages, True)
    mm, stage_res = mm if stages else (mm, None)
    vec = (_sds((1, d), F32), "vec")
    rows = _rows_call(
        "ln1_bwd", s, epi, [(mm, "tile"), (dr2, "tile"), (xhat1, "tile"), (rstd1, "tile"), (g1, "vec")],
        [(_sds((s, d), F32), "tile"), (_sds((s, d), BF16), "tile"), vec, vec, vec])
    return (rows, stage_res) if stages else rows


def _bwd_out_in(dr1_bf, w_out, z, pa, pb, stages=()):
    s, d = dr1_bf.shape
    tm, tn, tk = _tile(s, 1024), _tile(d, 1024), _tile(d, 2048)
    ga0, gb0 = 3 * d // tn, 4 * d // tn

    def epi(accs, ex, out, ids):
        dm = accs[0]
        sa = _sigmoid(ex[0][...].astype(F32))
        sb = _sigmoid(ex[1][...].astype(F32))
        out[0][...] = (dm * sa).astype(BF16)
        out[1][...] = (dm * sb).astype(BF16)
        out[2][...] = (dm * ex[2][...].astype(F32) * sa * (1.0 - sa)).astype(BF16)
        out[3][...] = (dm * ex[3][...].astype(F32) * sb * (1.0 - sb)).astype(BF16)

    o_spec = pl.BlockSpec((tm, tn), lambda i, j, k: (i, j))
    return _out(_matmul(
        "bwd_out_in", (s // tm, d // tn, d // tk),
        [(dr1_bf, pl.BlockSpec((tm, tk), lambda i, j, k: (i, k)), w_out, pl.BlockSpec((tn, tk), lambda i, j, k: (j, k)), "nt")],
        [(z, pl.BlockSpec((tm, tn), lambda i, j, k: (i, ga0 + j))), (z, pl.BlockSpec((tm, tn), lambda i, j, k: (i, gb0 + j))),
         (pa, o_spec), (pb, o_spec)],
        [(_sds((s, d), BF16), o_spec)] * 4, (tm, tn), epi, stages), stages)


def _bwd_up_in(name, dp, w_up, stages=()):
    s, d = dp.shape
    n = w_up.shape[0]
    tm, tn, tk = _tile(s, 1024), _tile(n, 1024), _tile(d, 2048)

    def epi(accs, ex, out, ids):
        out[0][...] = accs[0].astype(BF16)

    return _out(_matmul(
        name, (s // tm, n // tn, d // tk),
        [(dp, pl.BlockSpec((tm, tk), lambda i, j, k: (i, k)), w_up, pl.BlockSpec((tn, tk), lambda i, j, k: (j, k)), "nt")],
        [], [(_sds((s, n), BF16), pl.BlockSpec((tm, tn), lambda i, j, k: (i, j)))], (tm, tn), epi, stages), stages, True)


def _bwd_in(dz, wg_in, dr1, stages=()):
    s, d = dr1.shape
    inc = wg_in.shape[2]
    tm, tn, tk = _tile(s, 1024), _tile(d, 1024), _tile(inc, 2560)
    nb = inc // tk

    def epi(accs, ex, out, ids):
        out[0][...] = accs[0] + DN_ALPHA * ex[0][...]

    o_spec = pl.BlockSpec((tm, tn), lambda i, j, k: (i, j))
    return _out(_matmul(
        "bwd_in", (s // tm, d // tn, N_CHIP * nb),
        [(dz, pl.BlockSpec((tm, tk), lambda i, j, k: (i, k)),
          wg_in, pl.BlockSpec((None, tn, tk), lambda i, j, k: (k // nb, j, k % nb)), "nt")],
        [(dr1, o_spec)], [(_sds((s, d), F32), o_spec)], (tm, tn), epi, stages), stages, True)


def _wgrad(name, a, b, col_sharded, stages=()):
    s, ka = a.shape
    n = b.shape[1]
    tm, tk = _tile(ka, 1024), _tile(s, 2048)
    tn = _tile(n // N_CHIP, 1280) if col_sharded else _tile(n, 1024)

    def epi(accs, ex, out, ids):
        out[0][...] = accs[0].astype(BF16)

    if col_sharded:
        nb = (n // N_CHIP) // tn
        o = (_sds((N_CHIP, ka, n // N_CHIP), BF16), pl.BlockSpec((None, tm, tn), lambda i, j, k: (j // nb, i, j % nb)))
    else:
        o = (_sds((ka, n), BF16), pl.BlockSpec((tm, tn), lambda i, j, k: (i, j)))
    res = _out(_matmul(
        name, (ka // tm, n // tn, s // tk),
        [(a, pl.BlockSpec((tk, tm), lambda i, j, k: (k, i)), b, pl.BlockSpec((tk, tn), lambda i, j, k: (k, j)), "tn")],
        [], [o], (tm, tn), epi, stages), stages, True)
    res, stage_res = res if stages else (res, None)
    res = res if col_sharded else res.reshape(N_CHIP, ka // N_CHIP, n)
    return (res, stage_res) if stages else res


def _chunk(s):
    return _tile(s, 512, SUBLANES)


def _zero_pads(ref, s):
    zeros = jnp.zeros((PAD, ref.shape[1]), F32)
    ref[pl.ds(0, PAD), :] = zeros
    ref[pl.ds(PAD + s, PAD), :] = zeros


def _window(ref, t0, t):
    return ref[pl.ds(t0, t + 2 * PAD), :]


def _shift(sup, off, t):
    return sup[PAD + off:PAD + off + t, :]


def _pool_count(t0, t, s, w):
    pos = t0 + lax.broadcasted_iota(jnp.int32, (t, 1), 0)
    return (jnp.minimum(pos + w // 2, s) - jnp.maximum(pos - w // 2, 0)).astype(F32)


def _pool_fwd(z, pool_w, pool_scale, stages=()):
    s = z.shape[0]
    n_g, pg = pool_w.shape[0], pool_w.shape[1]
    assert n_g == len(POOL_WINDOWS) and max(POOL_WINDOWS) // 2 <= PAD
    t = _chunk(s)

    def body(u_ref, w_ref, sc_ref, d_ref, y_ref, pad_ref):
        g = pl.program_id(0)
        _zero_pads(pad_ref, s)
        pad_ref[pl.ds(PAD, s), :] = u_ref[...].astype(F32)
        for gi, w in enumerate(POOL_WINDOWS):
            @pl.when(g == gi)
            def _():
                def step(ch, carry):
                    t0 = pl.multiple_of(ch * t, t)
                    sup = _window(pad_ref, t0, t)
                    acc = _shift(sup, -(w // 2), t)
                    for o in range(-(w // 2) + 1, w // 2):
                        acc = acc + _shift(sup, o, t)
                    dd = (acc * (1.0 / _pool_count(t0, t, s, w)) - _shift(sup, 0, t)).astype(BF16)
                    d_ref[pl.ds(t0, t), :] = dd
                    y = jnp.dot(dd, w_ref[...], preferred_element_type=F32) * sc_ref[...]
                    y_ref[pl.ds(t0, t), :] = y.astype(BF16)
                    return carry

                lax.fori_loop(0, s // t, step, 0)

    blk = pl.BlockSpec((s, pg), lambda g: (0, g))
    res = _host_call(
        "pool_fwd", (n_g,), lambda ins, outs, scr: body(*ins, *outs, *scr), [z, pool_w, pool_scale],
        [blk, pl.BlockSpec((None, pg, pg), lambda g: (g, 0, 0)), pl.BlockSpec((1, pg), lambda g: (0, g))],
        [_sds((s, n_g * pg), BF16)] * 2, [blk, blk], [pltpu.VMEM((s + 2 * PAD, pg), F32)], list(stages))
    return res if stages else res[0]


def _pool_bwd(dsv, dy, pool_w, pool_scale, stages=()):
    s = dsv.shape[0]
    n_g, pg = pool_w.shape[0], pool_w.shape[1]
    t = _chunk(s)

    def body(d_ref, dy_ref, w_ref, sc_ref, du_ref, dw_ref, dsc_ref, epad_ref, dwacc_ref):
        g = pl.program_id(0)
        _zero_pads(epad_ref, s)
        dwacc_ref[...] = jnp.zeros_like(dwacc_ref)
        for gi, w in enumerate(POOL_WINDOWS):
            @pl.when(g == gi)
            def _():
                def first(ch, dsc):
                    t0 = pl.multiple_of(ch * t, t)
                    dd = d_ref[pl.ds(t0, t), :]
                    dyc = dy_ref[pl.ds(t0, t), :].astype(F32)
                    wv = w_ref[...]
                    ypre = jnp.dot(dd, wv, preferred_element_type=F32)
                    dq = (dyc * sc_ref[...]).astype(BF16)
                    dwacc_ref[...] += lax.dot_general(dd, dq, _DIMS["tn"], preferred_element_type=F32)
                    ddv = lax.dot_general(dq, wv, _DIMS["nt"], preferred_element_type=F32)
                    epad_ref[pl.ds(pl.multiple_of(PAD + t0, SUBLANES), t), :] = ddv * (1.0 / _pool_count(t0, t, s, w))
                    return dsc + _colsum(dyc * ypre)

                dsc_ref[...] = lax.fori_loop(0, s // t, first, jnp.zeros((1, pg), F32))

                def second(ch, carry):
                    t0 = pl.multiple_of(ch * t, t)
                    sup = _window(epad_ref, t0, t)
                    acc = _shift(sup, -(w // 2) + 1, t)
                    for o in range(-(w // 2) + 2, w // 2 + 1):
                        acc = acc + _shift(sup, o, t)
                    du_ref[pl.ds(t0, t), :] = (acc - _shift(sup, 0, t) * _pool_count(t0, t, s, w)).astype(BF16)
                    return carry

                lax.fori_loop(0, s // t, second, 0)

        dw_ref[...] = dwacc_ref[...].astype(BF16)

    blk = pl.BlockSpec((s, pg), lambda g: (0, g))
    w_spec = pl.BlockSpec((None, pg, pg), lambda g: (g, 0, 0))
    sc_spec = pl.BlockSpec((1, pg), lambda g: (0, g))
    res = _host_call(
        "pool_bwd", (n_g,), lambda ins, outs, scr: body(*ins, *outs, *scr), [dsv, dy, pool_w, pool_scale],
        [blk, blk, w_spec, sc_spec], [_sds((s, n_g * pg), BF16), _sds((n_g, pg, pg), BF16), _sds((1, n_g * pg), F32)],
        [blk, w_spec, sc_spec], [pltpu.VMEM((s + 2 * PAD, pg), F32), pltpu.VMEM((pg, pg), F32)], list(stages))
    return res if stages else res[0]


def _sigmoid(x):
    return 0.5 * jnp.tanh(0.5 * x) + 0.5


def _softplus(x):
    e = jnp.exp(-jnp.abs(x))
    log1p_e = jnp.where(e < 1e-2, e * (1.0 - e * (0.5 - e * (1.0 / 3.0))), jnp.log(1.0 + e))
    return jnp.maximum(x, 0.0) + log1p_e


_GELU_C = math.sqrt(2.0 / math.pi)


def _gelu(x):
    th = jnp.tanh(_GELU_C * (x + 0.044715 * x * x * x))
    return 0.5 * x * (1.0 + th), th


def _gelu_grad(x, th):
    return 0.5 * (1.0 + th) + 0.5 * x * (1.0 - th * th) * _GELU_C * (1.0 + 3.0 * 0.044715 * x * x)


def _scan_chunk(a_ref, b_ref, o_ref, o_off, carry, t, reverse):
    n = a_ref.shape[1]
    row = lax.broadcasted_iota(jnp.int32, (SUBLANES, n), 0)
    n_groups = t // SUBLANES
    unroll = math.gcd(n_groups, SCAN_UNROLL)
    last = 0 if reverse else SUBLANES - 1

    def step(si, carry):
        for u in range(unroll):
            gi = si * unroll + u
            g = n_groups - 1 - gi if reverse else gi
            r0 = pl.multiple_of(g * SUBLANES, SUBLANES)
            a = a_ref[pl.ds(r0, SUBLANES), :]
            b = b_ref[pl.ds(r0, SUBLANES), :]
            for k in (1, 2, 4):
                keep = row < SUBLANES - k if reverse else row >= k
                sh = SUBLANES - k if reverse else k
                ar = jnp.where(keep, pltpu.roll(a, sh, 0), 1.0)
                br = jnp.where(keep, pltpu.roll(b, sh, 0), 0.0)
                b = a * br + b
                a = a * ar
            o_ref[pl.ds(pl.multiple_of(o_off + r0, SUBLANES), SUBLANES), :] = a * carry + b
            carry = (jnp.broadcast_to(a[last:last + 1, :], a.shape) * carry
                     + jnp.broadcast_to(b[last:last + 1, :], b.shape))
        return carry

    return lax.fori_loop(0, n_groups // unroll, step, carry)


def _lru_params(pk_ref):
    rows = pk_ref[...]
    get = lambda i: rows[i:i + 1, :]
    cw = [get(k) for k in range(4)]
    lam = (get(9), get(10))
    big_l = tuple(-LRU_C * _softplus(-v) for v in lam)
    return cw, get(4), (get(5), get(6)), (get(7), get(8)), lam, big_l


def _conv(sup, cw, cb, t):
    xc = cb + cw[0] * _shift(sup, -2, t)
    for k in range(1, 4):
        xc = xc + cw[k] * _shift(sup, k - 2, t)
    return xc


def _gates(xcb, w_ref, d, bk, ba, bx, big_l):
    pre = jnp.dot(xcb, w_ref[:, pl.ds(d * 2 * bk, 2 * bk)], preferred_element_type=F32)
    r = _sigmoid(pre[:, :bk] + ba[d])
    i = _sigmoid(pre[:, bk:] + bx[d])
    la = big_l[d] * r
    a = jnp.exp(la)
    var = jnp.tanh(-la) * (1.0 + a * a)
    rs = lax.rsqrt(jnp.maximum(var, 1e-30))
    return r, i, a, var * rs, rs


def _lru_specs(s, d, bk):
    u_spec = pl.BlockSpec((s, bk), lambda h: (0, d // bk + h))
    ug_spec = pl.BlockSpec((s, bk), lambda h: (0, 2 * d // bk + h))
    w_spec = pl.BlockSpec((None, bk, 4 * bk), lambda h: (h, 0, 0))
    pk_spec = pl.BlockSpec((None, 16, bk), lambda h: (h, 0, 0))
    blk = pl.BlockSpec((s, bk), lambda h: (0, h))
    return u_spec, ug_spec, w_spec, pk_spec, blk


def _lru_fwd(z, gatew, pk, stages=()):
    s = z.shape[0]
    n_h, bk = gatew.shape[0], gatew.shape[1]
    d = n_h * bk
    t = _chunk(s)
    n_ch = s // t

    def body(u_ref, ug_ref, w_ref, pk_ref, y_ref, upad, h0buf, abuf, bbuf, xcbuf, h1buf):
        _zero_pads(upad, s)
        upad[pl.ds(PAD, s), :] = u_ref[...].astype(F32)
        cw, cb, ba, bx, _, big_l = _lru_params(pk_ref)
        zero = jnp.zeros((SUBLANES, bk), F32)

        def fill(xc, dr):
            _, i, a, sq, _ = _gates(xc.astype(BF16), w_ref, dr, bk, ba, bx, big_l)
            abuf[...] = a
            bbuf[...] = sq * i * xc

        def up(ch, carry):
            t0 = pl.multiple_of(ch * t, t)
            xc = _conv(_window(upad, t0, t), cw, cb, t)
            xcbuf[pl.ds(t0, t), :] = xc
            fill(xc, 0)
            return _scan_chunk(abuf, bbuf, h0buf, t0, carry, t, False)

        lax.fori_loop(0, n_ch, up, zero)

        def down(ci, carry):
            t0 = pl.multiple_of((n_ch - 1 - ci) * t, t)
            fill(xcbuf[pl.ds(t0, t), :], 1)
            carry = _scan_chunk(abuf, bbuf, h1buf, 0, carry, t, True)
            gl, _ = _gelu(ug_ref[pl.ds(t0, t), :].astype(F32))
            y_ref[pl.ds(t0, t), :] = ((h0buf[pl.ds(t0, t), :] + h1buf[...]) * gl).astype(BF16)
            return carry

        lax.fori_loop(0, n_ch, down, zero)

    u_spec, ug_spec, w_spec, pk_spec, blk = _lru_specs(s, d, bk)
    res = _host_call(
        "lru_fwd", (n_h,), lambda ins, outs, scr: body(*ins, *outs, *scr), [z, z, gatew, pk],
        [u_spec, ug_spec, w_spec, pk_spec], [_sds((s, d), BF16)], [blk],
        [pltpu.VMEM((s + 2 * PAD, bk), F32), pltpu.VMEM((s, bk), F32), pltpu.VMEM((t, bk), F32), pltpu.VMEM((t, bk), F32),
         pltpu.VMEM((s, bk), F32), pltpu.VMEM((t, bk), F32)], list(stages))
    return (res[0][0], res[1]) if stages else res[0][0]


def _lru_grads(lam_, hnb, a, sq, rs, r, i, xc, xcb, w_ref, dwacc, d, big_l, acc):
    bk = xc.shape[1]
    dba, dbx, dl = acc
    q = lam_ * i * xc
    dla = lam_ * hnb * a - q * (a * a) * rs
    dpr = dla * big_l * r * (1.0 - r)
    dpi = q * sq * (1.0 - i)
    dprb, dpib = dpr.astype(BF16), dpi.astype(BF16)
    c0 = d * 2 * bk
    dxc = (lam_ * sq * i
           + lax.dot_general(dprb, w_ref[:, pl.ds(c0, bk)], _DIMS["nt"], preferred_element_type=F32)
           + lax.dot_general(dpib, w_ref[:, pl.ds(c0 + bk, bk)], _DIMS["nt"], preferred_element_type=F32))
    dwacc[:, pl.ds(c0, bk)] += lax.dot_general(xcb, dprb, _DIMS["tn"], preferred_element_type=F32)
    dwacc[:, pl.ds(c0 + bk, bk)] += lax.dot_general(xcb, dpib, _DIMS["tn"], preferred_element_type=F32)
    return dxc, (dba + _colsum(dpr), dbx + _colsum(dpi), dl + _colsum(dla * r))


def _lru_bwd(z, dy, gatew, pk, stages=()):
    s = z.shape[0]
    n_h, bk = gatew.shape[0], gatew.shape[1]
    d = n_h * bk
    t = _chunk(s)
    n_ch = s // t

    def body(u_ref, ug_ref, dy_ref, w_ref, pk_ref, du_ref, dug_ref, dw_ref, dpk_ref,
             upad, h0pad, h1pad, dxpad, abuf, bbuf, lbuf, dwacc, edge, xcbuf):
        for ref in (upad, h0pad, h1pad, dxpad):
            _zero_pads(ref, s)
        upad[pl.ds(PAD, s), :] = u_ref[...].astype(F32)
        dwacc[...] = jnp.zeros_like(dwacc)
        cw, cb, ba, bx, lam, big_l = _lru_params(pk_ref)
        zero = jnp.zeros((SUBLANES, bk), F32)
        zrow = jnp.zeros((1, bk), F32)
        rowi = lax.broadcasted_iota(jnp.int32, (t, bk), 0)

        def at(t0):
            return pl.ds(pl.multiple_of(PAD + t0, SUBLANES), t)

        def conv_in(t0):
            xc = xcbuf[pl.ds(t0, t), :]
            return xc, xc.astype(BF16)

        def dh_of(t0):
            ug = ug_ref[pl.ds(t0, t), :].astype(F32)
            gl, th = _gelu(ug)
            dyv = dy_ref[pl.ds(t0, t), :].astype(F32)
            return dyv * gl, dyv * _gelu_grad(ug, th)

        def sweep1(ch, carry):
            t0 = pl.multiple_of(ch * t, t)
            xc = _conv(_window(upad, t0, t), cw, cb, t)
            xcbuf[pl.ds(t0, t), :] = xc
            _, i, a, sq, _ = _gates(xc.astype(BF16), w_ref, 0, bk, ba, bx, big_l)
            abuf[...] = a
            bbuf[...] = sq * i * xc
            return _scan_chunk(abuf, bbuf, h0pad, PAD + t0, carry, t, False)

        lax.fori_loop(0, n_ch, sweep1, zero)

        edge[...] = zero

        def sweep2(ci, st):
            carry_h, carry_l, acc = st
            t0 = pl.multiple_of((n_ch - 1 - ci) * t, t)
            xc, xcb = conv_in(t0)
            _, i1, a1, sq1, _ = _gates(xcb, w_ref, 1, bk, ba, bx, big_l)
            abuf[...] = a1
            bbuf[...] = sq1 * i1 * xc
            carry_h = _scan_chunk(abuf, bbuf, h1pad, PAD + t0, carry_h, t, True)
            dh, dgl = dh_of(t0)
            dug_ref[pl.ds(t0, t), :] = (dgl * (h0pad[at(t0), :] + h1pad[at(t0), :])).astype(BF16)
            r0, i0, a0, sq0, rs0 = _gates(xcb, w_ref, 0, bk, ba, bx, big_l)
            abuf[...] = jnp.where(rowi == t - 1, edge[0:1, :], pltpu.roll(a0, t - 1, 0))
            bbuf[...] = dh
            carry_l = _scan_chunk(abuf, bbuf, lbuf, 0, carry_l, t, True)
            edge[...] = jnp.broadcast_to(a0[0:1, :], (SUBLANES, bk))
            hprev = _shift(_window(h0pad, t0, t), -1, t)
            dxc, acc = _lru_grads(lbuf[...], hprev, a0, sq0, rs0, r0, i0, xc, xcb, w_ref, dwacc, 0, big_l[0], acc)
            dxpad[at(t0), :] = dxc
            return carry_h, carry_l, acc

        _, _, acc0 = lax.fori_loop(0, n_ch, sweep2, (zero, zero, (zrow, zrow, zrow)))

        edge[...] = zero

        def sweep3(ch, st):
            carry_l, acc = st
            t0 = pl.multiple_of(ch * t, t)
            xc, xcb = conv_in(t0)
            r1, i1, a1, sq1, rs1 = _gates(xcb, w_ref, 1, bk, ba, bx, big_l)
            dh, _ = dh_of(t0)
            abuf[...] = jnp.where(rowi == 0, edge[0:1, :], pltpu.roll(a1, 1, 0))
            bbuf[...] = dh
            carry_l = _scan_chunk(abuf, bbuf, lbuf, 0, carry_l, t, False)
            edge[...] = jnp.broadcast_to(a1[t - 1:t, :], (SUBLANES, bk))
            hnext = _shift(_window(h1pad, t0, t), 1, t)
            dxc, acc = _lru_grads(lbuf[...], hnext, a1, sq1, rs1, r1, i1, xc, xcb, w_ref, dwacc, 1, big_l[1], acc)
            dxpad[at(t0), :] += dxc
            return carry_l, acc

        _, acc1 = lax.fori_loop(0, n_ch, sweep3, (zero, (zrow, zrow, zrow)))

        def sweep4(ch, st):
            t0 = pl.multiple_of(ch * t, t)
            sdx = _window(dxpad, t0, t)
            su = _window(upad, t0, t)
            dxc = _shift(sdx, 0, t)
            du = cw[0] * _shift(sdx, 2, t) + cw[1] * _shift(sdx, 1, t) + cw[2] * dxc + cw[3] * _shift(sdx, -1, t)
            du_ref[pl.ds(t0, t), :] = du.astype(BF16)
            return tuple(st[k] + _colsum(dxc * _shift(su, k - 2, t)) for k in range(4)) + (st[4] + _colsum(dxc),)

        conv_g = lax.fori_loop(0, n_ch, sweep4, (zrow,) * 5)

        dpk_ref[...] = jnp.zeros_like(dpk_ref)
        rows = list(conv_g) + [acc0[0], acc1[0], acc0[1], acc1[1],
                               acc0[2] * LRU_C * _sigmoid(-lam[0]), acc1[2] * LRU_C * _sigmoid(-lam[1])]
        for k, v in enumerate(rows):
            dpk_ref[pl.ds(k, 1), :] = v
        dw_ref[...] = dwacc[...].astype(BF16)

    u_spec, ug_spec, w_spec, pk_spec, blk = _lru_specs(s, d, bk)
    padded = pltpu.VMEM((s + 2 * PAD, bk), F32)
    chunk = pltpu.VMEM((t, bk), F32)
    res = _host_call(
        "lru_bwd", (n_h,), lambda ins, outs, scr: body(*ins, *outs, *scr), [z, z, dy, gatew, pk],
        [u_spec, ug_spec, blk, w_spec, pk_spec],
        [_sds((s, d), BF16), _sds((s, d), BF16), _sds((n_h, bk, 4 * bk), BF16), _sds((n_h, 16, bk), F32)],
        [blk, blk, w_spec, pk_spec],
        [padded, padded, padded, padded, chunk, chunk, chunk, pltpu.VMEM((bk, 4 * bk), F32),
         pltpu.VMEM((SUBLANES, bk), F32), pltpu.VMEM((s, bk), F32)], list(stages))
    return res if stages else res[0]


def _scalar(v):
    return jnp.reshape(v, (1,)).astype(jnp.int32)


def _add_sibling(g, r, c):
    _, rows, cols = g.shape
    rh = rows // 2
    tr = _tile(rh, 512, 16)
    nr = rh // tr

    def body(c_ref, g_ref, r_ref, o_ref):
        o_ref[...] = (g_ref[...].astype(F32) + r_ref[...].astype(F32)).astype(BF16)

    spec = pl.BlockSpec((None, tr, cols), lambda k, i, c_ref: (k, i, 0))
    return pl.pallas_call(
        body, name="add_sibling", out_shape=_sds((N_CHIP, rh, cols), BF16),
        grid_spec=pltpu.PrefetchScalarGridSpec(
            num_scalar_prefetch=1, grid=(N_CHIP, nr),
            in_specs=[pl.BlockSpec((None, tr, cols), lambda k, i, c_ref: (k, c_ref[0] * nr + i, 0)), spec], out_specs=spec),
        compiler_params=_cparams(("arbitrary", "arbitrary")),
    )(_scalar(c), g, r)


def _sum_chips(p, rcv, k_me, c):
    _, rh, cols = p.shape
    tr = _tile(rh, 512, 16)
    nr = rh // tr

    def body(kc_ref, p_ref, r_ref, o_ref):
        acc = p_ref[...].astype(F32)
        for j in range(3):
            acc = acc + r_ref[j].astype(F32)
        o_ref[...] = acc

    return pl.pallas_call(
        body, name="sum_chips", out_shape=_sds((2 * rh, cols), F32),
        grid_spec=pltpu.PrefetchScalarGridSpec(
            num_scalar_prefetch=1, grid=(nr,),
            in_specs=[pl.BlockSpec((None, tr, cols), lambda i, kc_ref: (kc_ref[0], i, 0)),
                      pl.BlockSpec((3, tr, cols), lambda i, kc_ref: (0, i, 0))],
            out_specs=pl.BlockSpec((tr, cols), lambda i, kc_ref: (kc_ref[1] * nr + i, 0))),
        compiler_params=_cparams(("arbitrary",)),
    )(jnp.stack([k_me, c]).astype(jnp.int32), p, rcv)


def _sum_devices(g):
    def body(g_ref, o_ref):
        acc = g_ref[0]
        for dev in range(1, N_DEV):
            acc = acc + g_ref[dev]
        o_ref[...] = acc

    return pl.pallas_call(body, name="sum_devices", out_shape=_sds(g.shape[1:], F32))(g)


def _adamw(w, g, m, v):
    rows, cols = w.shape
    tr = _tile(rows, 256, SUBLANES)

    def body(ins, outs, scr):
        w_ref, g_ref, m_ref, v_ref = ins
        go_ref, d_ref, nm_ref, nv_ref = outs
        gv = g_ref[...]
        go_ref[...] = gv
        nm = ADAM_B1 * m_ref[...] + (1.0 - ADAM_B1) * gv
        nv = ADAM_B2 * v_ref[...] + (1.0 - ADAM_B2) * (gv * gv)
        m_hat = nm / (1.0 - ADAM_B1 ** ADAM_STEP)
        v_hat = nv / (1.0 - ADAM_B2 ** ADAM_STEP)
        d_ref[...] = -ADAM_LR * (m_hat / (jnp.sqrt(v_hat) + ADAM_EPS) + ADAM_WD * w_ref[...])
        nm_ref[...] = nm
        nv_ref[...] = nv

    spec = pl.BlockSpec((tr, cols), lambda i: (i, 0))
    return _host_call("adamw", (rows // tr,), body, [w, g, m, v], [spec] * 4, [_sds((rows, cols), F32)] * 4, [spec] * 4, [], [])[0]


def _pack(vs, unit):
    flat = jnp.concatenate([v.reshape(-1).astype(F32) for v in vs])
    pad = (-flat.shape[0]) % unit
    if pad:
        flat = jnp.concatenate([flat, jnp.zeros((pad,), F32)])
    return flat.reshape(-1, 128)


def _unpack(p, like):
    flat = p.reshape(-1)
    out, off = [], 0
    for v in like:
        n = math.prod(v.shape)
        out.append(flat[off:off + n].reshape(v.shape))
        off += n
    return out


def kernel(x, w_in, pool_w, pool_scale, conv_w, conv_b, lru_wa, lru_ba, lru_wx, lru_bx, lru_lambda, w_pool_up, w_lru_up, w_out, b_out, ln1_g, ln1_b, w_ff1, b_ff1, w_ff2, b_ff2, ln2_g, ln2_b, loss_target, m_w_in, m_pool_w, m_pool_scale, m_conv_w, m_conv_b, m_lru_wa, m_lru_ba, m_lru_wx, m_lru_bx, m_lru_lambda, m_w_pool_up, m_w_lru_up, m_w_out, m_b_out, m_ln1_g, m_ln1_b, m_w_ff1, m_b_ff1, m_w_ff2, m_b_ff2, m_ln2_g, m_ln2_b, v_w_in, v_pool_w, v_pool_scale, v_conv_w, v_conv_b, v_lru_wa, v_lru_ba, v_lru_wx, v_lru_bx, v_lru_lambda, v_w_pool_up, v_w_lru_up, v_w_out, v_b_out, v_ln1_g, v_ln1_b, v_w_ff1, v_b_ff1, v_w_ff2, v_b_ff2, v_ln2_g, v_ln2_b):
    given = dict(locals())
    wt = {n: given[n] for n in WEIGHTS}
    mom = {n: given["m_" + n] for n in WEIGHTS}
    vel = {n: given["v_" + n] for n in WEIGHTS}

    ix, iy, ic = _mesh_pos()
    k_me = 2 * ix + iy
    s, d = x.shape[1], x.shape[2]
    ds = d // N_CHIP
    n_g, pgs, pg = pool_w.shape[1], pool_w.shape[2], pool_w.shape[3]
    n_h, bks, bk = lru_wa.shape[2], lru_wa.shape[3], lru_wa.shape[4]
    f = b_ff1.shape[1]
    x2 = x[0]
    vec = lambda a: a.reshape(1, -1)

    sharded_vecs = [conv_w[0], lru_ba[0], lru_bx[0], lru_lambda[0]]
    rows_sv = jnp.concatenate(sharded_vecs + [jnp.zeros((6, ds), F32)], axis=0)
    sv = _all_gather_small(rows_sv)
    sv = sv.reshape(N_CHIP, 2, 16, ds)[:, 0].transpose(1, 0, 2).reshape(16, d)
    conv_w_f, ba_f, bx_f, lam_f = sv[0:4], sv[4:6], sv[6:8], sv[8:10]
    pk = jnp.concatenate([conv_w_f, conv_b, ba_f, bx_f, lam_f, jnp.zeros((5, d), F32)], axis=0)
    pk = pk.reshape(16, n_h, bk).transpose(1, 0, 2)

    def gate_stack(wa, wx):
        return jnp.stack([wa[0], wx[0]], axis=1)

    mats = {
        "w_in": w_in[0], "w_pool_up": w_pool_up[0], "w_lru_up": w_lru_up[0], "w_out": w_out[0],
        "w_ff1": w_ff1[0], "w_ff2": w_ff2[0],
        "pool_w": pool_w[0].reshape(n_g * pgs, pg),
        "gate_w": gate_stack(lru_wa, lru_wx).reshape(4 * n_h * bks, bk),
    }
    names = list(mats)
    placed = {n: _cast_place(mats[n], k_me) for n in names}

    def add_sibling(gs, swapped):
        return [_add_sibling(g, r, ic) for g, r in zip(gs, swapped)]

    def sum_chips(ps, received):
        return [_sum_chips(p, r, k_me, ic) for p, r in zip(ps, received)]

    first = [placed[n] for n in ("w_in", "pool_w", "gate_w")]
    (bufs,) = _run_stages("gather_first", [_chain([_gather_direct(first), _gather_relay(first), _gather_d2d(first)])])
    wg_in = bufs[0]
    wf_pool = bufs[1].reshape(N_CHIP, n_g, pgs, pg).transpose(1, 0, 2, 3).reshape(n_g, pg, pg)
    wf_gate = bufs[2].reshape(N_CHIP, 2, 2, n_h, bks, bk).transpose(3, 0, 4, 1, 2, 5).reshape(n_h, bk, 4 * bk)

    z, (wb_mix, wb_ff1) = _fwd_in(x2, wg_in, stages=[
        _gather_direct([placed[n] for n in ("w_pool_up", "w_lru_up", "w_out")]), _gather_direct([placed["w_ff1"]])])
    (d_pool, y_pool), (wb_mix,) = _pool_fwd(z, wf_pool, pool_scale, stages=[_gather_relay(wb_mix)])
    y_lru, (wb_ff1, wb_mix, wb_ff2) = _lru_fwd(z, wf_gate, pk, stages=[
        _gather_relay(wb_ff1), _gather_d2d(wb_mix), _gather_direct([placed["w_ff2"]])])
    wf_pu, wf_lu, wf_out = (b.reshape(d, d) for b in wb_mix)
    (m_mix, p_a, p_b), (wb_ff1, wb_ff2) = _fwd_merge(y_pool, y_lru, wf_pu, wf_lu, z, stages=[
        _gather_d2d(wb_ff1), _gather_relay(wb_ff2)])
    wg_ff1 = wb_ff1[0]
    (xhat1, x1_bf, rstd1), (wb_ff2,) = _fwd_out_ln1(m_mix, wf_out, x2, b_out, ln1_g, ln1_b, stages=[_gather_d2d(wb_ff2)])
    hdn = _fwd_ff1(x1_bf, wg_ff1, b_ff1)
    wf_ff2 = wb_ff2[0].reshape(f, d)
    dr2, dr2_bf, g_ln2_g, g_ln2_b, g_b_ff2, loss_part = _fwd_ff2_ln2_loss(
        hdn, wf_ff2, xhat1, ln1_g, ln1_b, b_ff2, ln2_g, ln2_b, loss_target[0])

    dpre, g_b_ff1 = _bwd_ff2_in(dr2_bf, wf_ff2, hdn)
    g_ff = [_wgrad("wgrad_ff1", x1_bf, dpre, True), _wgrad("wgrad_ff2", hdn, dr2_bf, False)]
    (dr1, dr1_bf, g_ln1_g, g_ln1_b, g_b_out), (swapped,) = _bwd_ff1_in_ln1(
        dpre, wg_ff1, dr2, xhat1, rstd1, ln1_g, stages=[_swap_halves(g_ff)])
    sums_ff = add_sibling(g_ff, swapped)
    dp_a, dp_b, dg_a, dg_b = _bwd_out_in(dr1_bf, wf_out, z, p_a, p_b)
    dy_pool = _bwd_up_in("bwd_pool_up_in", dp_a, wf_pu)
    dy_lru = _bwd_up_in("bwd_lru_up_in", dp_b, wf_lu)
    g_mix = [_wgrad("wgrad_pool_up", y_pool, dp_a, False), _wgrad("wgrad_lru_up", y_lru, dp_b, False),
             _wgrad("wgrad_out", m_mix, dr1_bf, False)]
    (du_pool, g_pool_w, g_pool_scale), (swapped,) = _pool_bwd(
        d_pool, dy_pool, wf_pool, pool_scale, stages=[_swap_halves(g_mix)])
    sums_mix = add_sibling(g_mix, swapped)
    (du_lru, du_gate, g_gate_w, g_pk), (recv_ff, recv_mix) = _lru_bwd(
        z, dy_lru, wf_gate, pk, stages=[_scatter_chips(sums_ff), _scatter_chips(sums_mix)])
    halves = sum_chips(sums_ff + sums_mix, recv_ff + recv_mix)
    g_small = [g_pool_w.reshape(n_g, N_CHIP, pgs, pg).transpose(1, 0, 2, 3).reshape(N_CHIP, n_g * pgs, pg),
               g_gate_w.reshape(n_h, N_CHIP, bks, 2, 2, bk).transpose(1, 3, 4, 0, 2, 5).reshape(N_CHIP, 4 * n_h * bks, bk)]
    dz = jnp.concatenate([du_pool, du_lru, du_gate, dg_a, dg_b], axis=1)
    g_in, (joined, swapped) = _wgrad("wgrad_in", x2, dz, True, stages=[_join_halves(halves), _swap_halves(g_small)])
    g_mat = dict(zip(["w_ff1", "w_ff2", "w_pool_up", "w_lru_up", "w_out"], joined))
    sums_small = add_sibling(g_small, swapped)

    def stacked(tree):
        return gate_stack(tree["lru_wa"], tree["lru_wx"]).reshape(4 * n_h * bks, bk)

    res = {}

    def update(n):
        if n == "gate_w":
            outs = [o.reshape(2, 2, n_h, bks, bk) for o in _adamw(stacked(wt), g_mat[n], stacked(mom), stacked(vel))]
            res["lru_wa"] = [o[:, 0][None] for o in outs]
            res["lru_wx"] = [o[:, 1][None] for o in outs]
        else:
            shp2 = mats[n].shape
            outs = _adamw(wt[n].reshape(shp2), g_mat[n], mom[n].reshape(shp2), vel[n].reshape(shp2))
            res[n] = [o.reshape(wt[n].shape) for o in outs]

    (swapped,) = _run_stages("swap_in", [_swap_halves([g_in])])
    sums_in = add_sibling([g_in], swapped)
    grad_x, (recv_small, recv_in) = _bwd_in(dz, wg_in, dr1, stages=[_scatter_chips(sums_small), _scatter_chips(sums_in)])
    halves = sum_chips(sums_small + sums_in, recv_small + recv_in)
    (joined,) = _run_stages("join_last", [_join_halves(halves)])
    g_mat.update(zip(["pool_w", "gate_w", "w_in"], joined))
    for n in names:
        update(n)

    g_pk = g_pk.transpose(1, 0, 2).reshape(16, d)
    vec_full = {
        "pool_scale": g_pool_scale, "conv_w": g_pk[0:4], "conv_b": g_pk[4:5],
        "lru_ba": g_pk[5:7], "lru_bx": g_pk[7:9], "lru_lambda": g_pk[9:11],
        "b_out": g_b_out, "ln1_g": g_ln1_g, "ln1_b": g_ln1_b, "b_ff1": g_b_ff1, "b_ff2": g_b_ff2,
        "ln2_g": g_ln2_g, "ln2_b": g_ln2_b,
    }
    vnames = list(vec_full)
    vg = _sum_devices(_all_gather_small(_pack([vec_full[n] for n in vnames], 1024)))
    vg = dict(zip(vnames, _unpack(vg, [vec_full[n] for n in vnames])))
    for n in ("conv_w", "lru_ba", "lru_bx", "lru_lambda"):
        vg[n] = lax.dynamic_slice_in_dim(vg[n], k_me * ds, ds, axis=1)
    vg = {n: vg[n].reshape(wt[n].shape) for n in vnames}
    upd = _adamw(_pack([wt[n] for n in vnames], 1024), _pack([vg[n] for n in vnames], 1024),
                 _pack([mom[n] for n in vnames], 1024), _pack([vel[n] for n in vnames], 1024))
    upd = [_unpack(u, [wt[n] for n in vnames]) for u in upd]
    for i, n in enumerate(vnames):
        res[n] = [vg[n], upd[1][i], upd[2][i], upd[3][i]]

    loss = lax.psum(loss_part[0, 0], ("x", "y", "c"))
    return (loss, grad_x[None], *[res[n][0] for n in WEIGHTS], *[res[n][1] for n in WEIGHTS],
            *[res[n][2] for n in WEIGHTS], *[res[n][3] for n in WEIGHTS])
```

```python
import functools
import math

import jax
import jax.numpy as jnp
from jax import lax
from jax.experimental import pallas as pl
from jax.experimental.pallas import tpu as pltpu

F32 = jnp.float32
BF16 = jnp.bfloat16
MESH = pl.DeviceIdType.MESH
ANY = pl.BlockSpec(memory_space=pl.ANY)

N_CHIP = 4
N_DEV = 8
VMEM_LIMIT_BYTES = 56 * 1024 * 1024
SUBLANES = 8
PAD = 8
SCAN_UNROLL = 8

POOL_WINDOWS = (2, 4, 8, 16)
LRU_C = 8.0
DN_ALPHA = 2.0 ** 0.25
LN_EPS = 1e-5
ADAM_LR, ADAM_B1, ADAM_B2, ADAM_EPS, ADAM_WD, ADAM_STEP = 0.001, 0.9, 0.999, 1e-08, 0.01, 10

WEIGHTS = ("w_in", "pool_w", "pool_scale", "conv_w", "conv_b", "lru_wa", "lru_ba", "lru_wx", "lru_bx", "lru_lambda",
           "w_pool_up", "w_lru_up", "w_out", "b_out", "ln1_g", "ln1_b", "w_ff1", "b_ff1", "w_ff2", "b_ff2", "ln2_g", "ln2_b")


def _cparams(sem=None):
    return pltpu.CompilerParams(dimension_semantics=sem, vmem_limit_bytes=VMEM_LIMIT_BYTES)


def _tile(dim, pref, unit=128):
    if dim <= pref:
        return dim
    t = (pref // unit) * unit
    while t > unit and dim % t:
        t -= unit
    assert dim % t == 0, (dim, pref)
    return t


def _mesh_pos():
    x, y, c = lax.axis_index("x"), lax.axis_index("y"), lax.axis_index("c")
    return x, y, c


def _other_chips(x, y):
    return [(1 - x, y), (x, 1 - y), (1 - x, 1 - y)]


def _all_gather_small(v):
    m_per, n = v.shape

    def body(x_ref, out_ref, send_sems, recv_sems, local_sem):
        x, y, c = _mesh_pos()
        me, sibling = (x, y, c), (x, y, 1 - c)
        chips = _other_chips(x, y)

        def rows(px, py, pc):
            return out_ref.at[4 * px + 2 * py + pc]

        def copy(k, block, to, src=None):
            return pltpu.make_async_remote_copy(
                src_ref=rows(*block) if src is None else src, dst_ref=rows(*block),
                send_sem=send_sems.at[k], recv_sem=recv_sems.at[k], device_id=to, device_id_type=MESH)

        mine = pltpu.make_async_copy(x_ref, rows(*me), local_sem)
        mine.start()
        first = [copy(0, me, sibling, src=x_ref)]
        first += [copy(1 + j, me, (*chip, c), src=x_ref) for j, chip in enumerate(chips)]
        for cp in first:
            cp.start()
        passed = [copy(4 + j, (*chip, c), sibling) for j, chip in enumerate(chips)]
        for j, chip in enumerate(chips):
            copy(1 + j, (*chip, c), me).wait_recv()
            passed[j].start()
        copy(0, sibling, me).wait_recv()
        for j, chip in enumerate(chips):
            copy(4 + j, (*chip, 1 - c), me).wait_recv()
        for cp in first + passed:
            cp.wait_send()
        mine.wait()

    return pl.pallas_call(
        body, name="all_gather_small",
        out_shape=jax.ShapeDtypeStruct((N_DEV, m_per, n), v.dtype),
        in_specs=[pl.BlockSpec(memory_space=pltpu.VMEM)],
        out_specs=pl.BlockSpec(memory_space=pltpu.VMEM),
        scratch_shapes=[pltpu.SemaphoreType.DMA((7,)), pltpu.SemaphoreType.DMA((7,)), pltpu.SemaphoreType.DMA],
    )(v)


class _Stage:
    def __init__(self, srcs, bufs, news, n_sems, copies):
        self.srcs, self.bufs, self.news, self.n_sems, self.copies = list(srcs), list(bufs), list(news), n_sems, copies
        self.phases = [(copies, 0)]


class _SemsFrom:
    def __init__(self, ref, offset):
        self.ref, self.offset, self.at = ref, offset, self

    def __getitem__(self, s):
        return self.ref.at[self.offset + s]


def _chain(stages):
    chained = _Stage([], stages[0].bufs, [], sum(st.n_sems for st in stages), None)
    chained.phases, first = [], 0
    for st in stages:
        chained.phases.append((st.copies, first))
        first += st.n_sems
    return chained


def _remote(src, dst, send_sems, recv_sems, s, to):
    return pltpu.make_async_remote_copy(src_ref=src, dst_ref=dst, send_sem=send_sems.at[s], recv_sem=recv_sems.at[s],
                                        device_id=to, device_id_type=MESH)


def _stage_operands(stages, n_in, n_out):
    ins, outs, aliases, scratch = [], [], {}, []
    for st in stages:
        for i in range(len(st.bufs)):
            aliases[n_in + len(ins) + len(st.srcs) + i] = n_out + len(outs) + i
        ins += st.srcs + st.bufs
        outs += [jax.ShapeDtypeStruct(b.shape, b.dtype) for b in st.bufs] + st.news
        scratch += [pltpu.SemaphoreType.DMA((st.n_sems,)), pltpu.SemaphoreType.DMA((st.n_sems,))]
    return ins, outs, aliases, scratch


def _stage_refs(stages, in_refs, out_refs, sem_refs):
    parts, i, o = [], 0, 0
    for n, st in enumerate(stages):
        src = in_refs[i:i + len(st.srcs)]
        i += len(st.srcs) + len(st.bufs)
        buf = out_refs[o:o + len(st.bufs)]
        new = out_refs[o + len(st.bufs):o + len(st.bufs) + len(st.news)]
        o += len(st.bufs) + len(st.news)
        parts.append((src, buf, new, sem_refs[2 * n], sem_refs[2 * n + 1]))
    return parts


def _stage_results(stages, res):
    out, o = [], 0
    for st in stages:
        n = len(st.bufs) + len(st.news)
        out.append(list(res[o:o + n]))
        o += n
    return out


def _stages_start(stages, parts):
    for st, part in zip(stages, parts):
        for cp in st.copies(*part)[0]:
            cp.start()


def _stages_wait(stages, parts):
    for st, part in zip(stages, parts):
        started, landing = st.copies(*part)
        for cp in landing:
            cp.wait_recv()
        for cp in started:
            cp.wait_send()


def _run_stages(name, stages):
    ins, outs, aliases, scratch = _stage_operands(stages, 0, 0)

    def body(*refs):
        parts = _stage_refs(stages, refs[:len(ins)], refs[len(ins):len(ins) + len(outs)], refs[len(ins) + len(outs):])
        for st, (src, buf, new, send_sems, recv_sems) in zip(stages, parts):
            for copies, first in st.phases:
                started, landing = copies(src, buf, new, _SemsFrom(send_sems, first), _SemsFrom(recv_sems, first))
                for cp in started:
                    cp.start()
                for cp in landing:
                    cp.wait_recv()
                for cp in started:
                    cp.wait_send()

    res = pl.pallas_call(
        body, name=name, out_shape=outs, in_specs=[ANY] * len(ins), out_specs=[ANY] * len(outs),
        input_output_aliases=aliases, scratch_shapes=scratch)(*ins)
    return _stage_results(stages, res)


def _gather_direct(ts):
    def copies(src, buf, new, send_sems, recv_sems):
        x, y, c = _mesh_pos()
        started, landing = [], []
        for t in range(len(ts)):
            rh = ts[t].shape[1] // 2
            rows = pl.ds(c * rh, rh)
            mine = buf[t].at[2 * x + y, rows]
            for j, chip in enumerate(_other_chips(x, y)[:2]):
                theirs = buf[t].at[2 * chip[0] + chip[1], rows]
                started.append(_remote(mine, mine, send_sems, recv_sems, 2 * t + j, (*chip, c)))
                landing.append(_remote(theirs, theirs, send_sems, recv_sems, 2 * t + j, (x, y, c)))
        return started, landing

    return _Stage([], ts, [], 2 * len(ts), copies)


def _gather_relay(ts):
    def copies(src, buf, new, send_sems, recv_sems):
        x, y, c = _mesh_pos()
        (x_nb, y_nb, diag) = _other_chips(x, y)
        block = lambda chip: 2 * chip[0] + chip[1]
        started, landing = [], []
        for t in range(len(ts)):
            rq = ts[t].shape[1] // 4
            q0, q1 = pl.ds(2 * c * rq, rq), pl.ds((2 * c + 1) * rq, rq)
            from_y, from_x = buf[t].at[block(y_nb), q0], buf[t].at[block(x_nb), q1]
            started.append(_remote(from_y, from_y, send_sems, recv_sems, 2 * t, (*x_nb, c)))
            started.append(_remote(from_x, from_x, send_sems, recv_sems, 2 * t + 1, (*y_nb, c)))
            for j, q in enumerate((q0, q1)):
                lands = buf[t].at[block(diag), q]
                landing.append(_remote(lands, lands, send_sems, recv_sems, 2 * t + j, (x, y, c)))
        return started, landing

    return _Stage([], ts, [], 2 * len(ts), copies)


def _together(stages):
    def copies(src, buf, new, send_sems, recv_sems):
        started, landing, first = [], [], 0
        for st in stages:
            more = st.copies(src, buf, new, _SemsFrom(send_sems, first), _SemsFrom(recv_sems, first))
            started, landing, first = started + more[0], landing + more[1], first + st.n_sems
        return started, landing

    return _Stage([], stages[0].bufs, [], sum(st.n_sems for st in stages), copies)


def _gather_d2d(ts, which=(0, 1, 2)):
    def copies(src, buf, new, send_sems, recv_sems):
        x, y, c = _mesh_pos()
        started, landing = [], []
        for t in range(len(ts)):
            rh = ts[t].shape[1] // 2
            for j, chip in enumerate(_other_chips(x, y)):
                if j not in which:
                    continue
                got = buf[t].at[2 * chip[0] + chip[1], pl.ds(c * rh, rh)]
                other = buf[t].at[2 * chip[0] + chip[1], pl.ds((1 - c) * rh, rh)]
                started.append(_remote(got, got, send_sems, recv_sems, 3 * t + j, (x, y, 1 - c)))
                landing.append(_remote(other, other, send_sems, recv_sems, 3 * t + j, (x, y, c)))
        return started, landing

    return _Stage([], ts, [], 3 * len(ts), copies)


def _swap_halves(gs):
    def copies(src, buf, new, send_sems, recv_sems):
        x, y, c = _mesh_pos()
        started, landing = [], []
        for t in range(len(gs)):
            rh = gs[t].shape[1] // 2
            started.append(_remote(src[t].at[:, pl.ds((1 - c) * rh, rh)], new[t], send_sems, recv_sems, t, (x, y, 1 - c)))
            landing.append(_remote(new[t], new[t], send_sems, recv_sems, t, (x, y, c)))
        return started, landing

    news = [jax.ShapeDtypeStruct((g.shape[0], g.shape[1] // 2, g.shape[2]), g.dtype) for g in gs]
    return _Stage(gs, [], news, len(gs), copies)


def _scatter_chips(ps):
    def copies(src, buf, new, send_sems, recv_sems):
        x, y, c = _mesh_pos()
        started, landing = [], []
        for t in range(len(ps)):
            for j, chip in enumerate(_other_chips(x, y)):
                started.append(_remote(src[t].at[2 * chip[0] + chip[1]], new[t].at[j], send_sems, recv_sems, 3 * t + j, (*chip, c)))
                landing.append(_remote(new[t].at[j], new[t].at[j], send_sems, recv_sems, 3 * t + j, (x, y, c)))
        return started, landing

    return _Stage(ps, [], [jax.ShapeDtypeStruct((3,) + p.shape[1:], p.dtype) for p in ps], 3 * len(ps), copies)


def _join_halves(fs):
    def copies(src, buf, new, send_sems, recv_sems):
        x, y, c = _mesh_pos()
        started, landing = [], []
        for t in range(len(fs)):
            rh = fs[t].shape[0] // 2
            mine = buf[t].at[pl.ds(c * rh, rh)]
            theirs = buf[t].at[pl.ds((1 - c) * rh, rh)]
            started.append(_remote(mine, mine, send_sems, recv_sems, t, (x, y, 1 - c)))
            landing.append(_remote(theirs, theirs, send_sems, recv_sems, t, (x, y, c)))
        return started, landing

    return _Stage([], fs, [], len(fs), copies)


def _cast_place(w, k_me):
    rows, cols = w.shape
    tr = _tile(rows, 512, 16)

    def body(k_ref, w_ref, o_ref):
        o_ref[...] = w_ref[...].astype(BF16)

    return pl.pallas_call(
        body, name="cast_place", out_shape=_sds((N_CHIP, rows, cols), BF16),
        grid_spec=pltpu.PrefetchScalarGridSpec(
            num_scalar_prefetch=1, grid=(rows // tr,),
            in_specs=[pl.BlockSpec((tr, cols), lambda i, k_ref: (i, 0))],
            out_specs=pl.BlockSpec((None, tr, cols), lambda i, k_ref: (k_ref[0], i, 0))),
        compiler_params=_cparams(("arbitrary",)),
    )(_scalar(k_me), w)


_DIMS = {"nn": (((1,), (0,)), ((), ())), "nt": (((1,), (1,)), ((), ())), "tn": (((0,), (0,)), ((), ()))}


def _accum(ref, val, first):
    @pl.when(first)
    def _():
        ref[...] = val

    @pl.when(jnp.logical_not(first))
    def _():
        ref[...] += val


def _grid_edges(grid):
    ids = [pl.program_id(ax) for ax in range(len(grid))]
    first = functools.reduce(jnp.logical_and, [i == 0 for i in ids])
    last = functools.reduce(jnp.logical_and, [i == n - 1 for i, n in zip(ids, grid)])
    return first, last


def _host_call(name, grid, body, operands, in_specs, out_shape, out_specs, scratch, stages):
    s_ins, s_outs, aliases, s_scratch = _stage_operands(stages, len(operands), len(out_shape))
    n_in, n_out, n_scr = len(operands), len(out_shape), len(scratch)

    def full_body(*refs):
        in_refs = refs[:n_in]
        s_in_refs = refs[n_in:n_in + len(s_ins)]
        o0 = n_in + len(s_ins)
        out_refs = refs[o0:o0 + n_out]
        s_out_refs = refs[o0 + n_out:o0 + n_out + len(s_outs)]
        c0 = o0 + n_out + len(s_outs)
        scr_refs = refs[c0:c0 + n_scr]
        if stages:
            parts = _stage_refs(stages, s_in_refs, s_out_refs, refs[c0 + n_scr:])
            first, last = _grid_edges(grid)
            pl.when(first)(lambda: _stages_start(stages, parts))
        body(in_refs, out_refs, scr_refs)
        if stages:
            pl.when(last)(lambda: _stages_wait(stages, parts))

    res = pl.pallas_call(
        full_body, name=name, grid=grid, in_specs=list(in_specs) + [ANY] * len(s_ins),
        out_specs=list(out_specs) + [ANY] * len(s_outs), out_shape=list(out_shape) + s_outs,
        input_output_aliases=aliases, scratch_shapes=list(scratch) + s_scratch,
        compiler_params=_cparams(("arbitrary",) * len(grid)),
    )(*operands, *s_ins)
    return list(res[:n_out]), _stage_results(stages, res[n_out:])


def _matmul(name, grid, pairs, extras, outs, acc_shape, epilogue, stages=()):
    n_p = len(pairs)
    n_k = grid[-1]
    dims = [_DIMS[p[4]] for p in pairs]

    def body(in_refs, out, accs):
        ab, ex = in_refs[:2 * n_p], in_refs[2 * n_p:]
        ids = [pl.program_id(ax) for ax in range(len(grid))]
        k = ids[-1]

        def dot(p):
            return lax.dot_general(ab[2 * p][...].astype(BF16), ab[2 * p + 1][...].astype(BF16), dims[p],
                                   preferred_element_type=F32)

        if n_k == 1:
            epilogue([dot(p) for p in range(n_p)], ex, out, ids)
            return

        @pl.when(k == 0)
        def _():
            for acc in accs:
                acc[...] = jnp.zeros_like(acc)

        for p in range(n_p):
            accs[p][...] += dot(p)

        @pl.when(k == n_k - 1)
        def _():
            epilogue([acc[...] for acc in accs], ex, out, ids)

    in_specs = []
    operands = []
    for a, a_spec, b, b_spec, _ in pairs:
        in_specs += [a_spec, b_spec]
        operands += [a, b]
    for e, e_spec in extras:
        in_specs.append(e_spec)
        operands.append(e)
    res, stage_res = _host_call(name, grid, body, operands, in_specs, [o[0] for o in outs], [o[1] for o in outs],
                                [pltpu.VMEM(acc_shape, F32) for _ in pairs] if n_k > 1 else [], list(stages))
    return (res, stage_res) if stages else res


def _out(res, stages, single=False):
    outs = res[0] if stages else res
    outs = outs[0] if single else outs
    return (outs, res[1]) if stages else outs


def _sds(shape, dtype):
    return jax.ShapeDtypeStruct(shape, dtype)


def _row(n):
    return pl.BlockSpec((1, n), lambda *_: (0, 0))


def _layer_norm(r):
    mu = jnp.mean(r, axis=-1, keepdims=True)
    xc = r - mu
    var = jnp.mean(xc * xc, axis=-1, keepdims=True)
    rstd = lax.rsqrt(var + LN_EPS)
    return xc * rstd, rstd


def _layer_norm_bwd(dxhat, xhat, rstd):
    m1 = jnp.mean(dxhat, axis=-1, keepdims=True)
    m2 = jnp.mean(dxhat * xhat, axis=-1, keepdims=True)
    return rstd * (dxhat - m1 - xhat * m2)


def _colsum(v):
    return jnp.sum(v, axis=0, keepdims=True)


ROW_TILE = 256


def _rows_call(name, s, epi, ins, outs):
    tr = _tile(s, ROW_TILE, SUBLANES)

    def spec(shape, kind):
        n = shape[1]
        return pl.BlockSpec((tr, n), lambda i: (i, 0)) if kind == "tile" else pl.BlockSpec((1, n), lambda i: (0, 0))

    def body(in_refs, out_refs, scr):
        epi([in_refs[0][...]], in_refs[1:], out_refs, [pl.program_id(0)])

    return _host_call(name, (s // tr,), body, [a for a, _ in ins], [spec(a.shape, k) for a, k in ins],
                      [o for o, _ in outs], [spec(o.shape, k) for o, k in outs], [], [])[0]


def _plain_matmul(name, a, b, mode):
    m, k_dim = a.shape
    n = b.shape[1]
    tm, tn, tk = _tile(m, 1024), _tile(n, 1024), _tile(k_dim, 2048)

    def epi(accs, ex, out, ids):
        out[0][...] = accs[0]

    assert mode == "nn"
    return _matmul(
        name, (m // tm, n // tn, k_dim // tk),
        [(a, pl.BlockSpec((tm, tk), lambda i, j, k: (i, k)), b, pl.BlockSpec((tk, tn), lambda i, j, k: (k, j)), "nn")],
        [], [(_sds((m, n), F32), pl.BlockSpec((tm, tn), lambda i, j, k: (i, j)))], (tm, tn), epi)[0]


def _fwd_in(x_in, wg_in, stages=()):
    s, d = x_in.shape
    inc = wg_in.shape[2]
    tm, tn, tk = _tile(s, 1024), _tile(inc, 1280), _tile(d, 2048)
    nb = inc // tn

    def epi(accs, ex, out, ids):
        out[0][...] = accs[0].astype(BF16)

    return _out(_matmul(
        "fwd_in", (s // tm, N_CHIP * nb, d // tk),
        [(x_in, pl.BlockSpec((tm, tk), lambda i, j, k: (i, k)),
          wg_in, pl.BlockSpec((None, tk, tn), lambda i, j, k: (j // nb, k, j % nb)), "nn")],
        [], [(_sds((s, N_CHIP * inc), BF16), pl.BlockSpec((tm, tn), lambda i, j, k: (i, j)))],
        (tm, tn), epi, stages), stages, True)


def _fwd_merge(y_pool, y_lru, w_pu, w_lu, z, stages=()):
    s, d = y_pool.shape
    tm, tn, tk = _tile(s, 1024), _tile(d, 1024), _tile(d, 1024)
    ga0, gb0 = 3 * d // tn, 4 * d // tn

    def epi(accs, ex, out, ids):
        sa = _sigmoid(ex[0][...].astype(F32))
        sb = _sigmoid(ex[1][...].astype(F32))
        out[0][...] = (sa * accs[0] + sb * accs[1]).astype(BF16)
        out[1][...] = accs[0].astype(BF16)
        out[2][...] = accs[1].astype(BF16)

    a_spec = pl.BlockSpec((tm, tk), lambda i, j, k: (i, k))
    b_spec = pl.BlockSpec((tk, tn), lambda i, j, k: (k, j))
    o_spec = pl.BlockSpec((tm, tn), lambda i, j, k: (i, j))
    return _out(_matmul(
        "fwd_merge", (s // tm, d // tn, d // tk),
        [(y_pool, a_spec, w_pu, b_spec, "nn"), (y_lru, a_spec, w_lu, b_spec, "nn")],
        [(z, pl.BlockSpec((tm, tn), lambda i, j, k: (i, ga0 + j))), (z, pl.BlockSpec((tm, tn), lambda i, j, k: (i, gb0 + j)))],
        [(_sds((s, d), BF16), o_spec)] * 3, (tm, tn), epi, stages), stages)


def _fwd_out_ln1(m, w_out, x, b_out, g1, b1, stages=()):
    s, d = x.shape
    tm, tk = _tile(s, 512), _tile(d, 2048)

    def epi(accs, ex, out, ids):
        r = DN_ALPHA * ex[0][...] + accs[0] + ex[1][...]
        xhat, rstd = _layer_norm(r)
        out[0][...] = xhat
        out[1][...] = (xhat * ex[2][...] + ex[3][...]).astype(BF16)
        out[2][...] = rstd

    full = pl.BlockSpec((tm, d), lambda i, j, k: (i, 0))
    return _out(_matmul(
        "fwd_out_ln1", (s // tm, 1, d // tk),
        [(m, pl.BlockSpec((tm, tk), lambda i, j, k: (i, k)), w_out, pl.BlockSpec((tk, d), lambda i, j, k: (k, 0)), "nn")],
        [(x, full), (b_out, _row(d)), (g1, _row(d)), (b1, _row(d))],
        [(_sds((s, d), F32), full), (_sds((s, d), BF16), full), (_sds((s, 1), F32), pl.BlockSpec((tm, 1), lambda i, j, k: (i, 0)))],
        (tm, d), epi, stages), stages)


def _fwd_ff1(x1_bf, wg_ff1, b_ff1, stages=()):
    s, d = x1_bf.shape
    fc = wg_ff1.shape[2]
    tm, tn, tk = _tile(s, 1024), _tile(fc, 1024), _tile(d, 2048)
    nb = fc // tn

    def epi(accs, ex, out, ids):
        p = jnp.maximum(accs[0] + ex[0][...], 0.0)
        out[0][...] = (p * p).astype(BF16)

    return _out(_matmul(
        "fwd_ff1", (s // tm, N_CHIP * nb, d // tk),
        [(x1_bf, pl.BlockSpec((tm, tk), lambda i, j, k: (i, k)),
          wg_ff1, pl.BlockSpec((None, tk, tn), lambda i, j, k: (j // nb, k, j % nb)), "nn")],
        [(b_ff1, pl.BlockSpec((1, tn), lambda i, j, k: (0, j)))],
        [(_sds((s, N_CHIP * fc), BF16), pl.BlockSpec((tm, tn), lambda i, j, k: (i, j)))],
        (tm, tn), epi, stages), stages, True)


def _fwd_ff2_ln2_loss(hdn, w_ff2, xhat1, g1, b1, b_ff2, g2, b2, target):
    s, f = hdn.shape
    d = xhat1.shape[1]

    def epi(accs, ex, out, ids):
        first = ids[0] == 0
        x1 = ex[0][...] * ex[1][...] + ex[2][...]
        r = DN_ALPHA * x1 + accs[0] + ex[3][...]
        xhat, rstd = _layer_norm(r)
        g2v = ex[4][...]
        err = xhat * g2v + ex[5][...] - ex[6][...]
        part = 0.5 * jnp.sum(jnp.mean(err * err, axis=-1, keepdims=True), axis=0, keepdims=True)
        dy = err * (1.0 / d)
        dr2 = _layer_norm_bwd(dy * g2v, xhat, rstd)
        out[0][...] = dr2
        out[1][...] = dr2.astype(BF16)
        _accum(out[2], _colsum(dy * xhat), first)
        _accum(out[3], _colsum(dy), first)
        _accum(out[4], _colsum(dr2), first)
        _accum(out[5], jnp.broadcast_to(part, (1, 128)), first)

    ff = _plain_matmul("fwd_ff2", hdn, w_ff2, "nn")
    vec = lambda n: (_sds((1, n), F32), "vec")
    return _rows_call(
        "ln2_loss", s, epi,
        [(ff, "tile"), (xhat1, "tile"), (g1, "vec"), (b1, "vec"), (b_ff2, "vec"), (g2, "vec"), (b2, "vec"), (target, "tile")],
        [(_sds((s, d), F32), "tile"), (_sds((s, d), BF16), "tile"), vec(d), vec(d), vec(d), vec(128)])


def _bwd_ff2_in(dr2_bf, w_ff2, hdn, stages=()):
    s, d = dr2_bf.shape
    f = hdn.shape[1]
    tm, tn, tk = _tile(s, 1024), _tile(f, 1024), _tile(d, 2048)

    def epi(accs, ex, out, ids):
        dpre = accs[0] * (2.0 * jnp.sqrt(ex[0][...].astype(F32)))
        out[0][...] = dpre.astype(BF16)
        _accum(out[1], _colsum(dpre), ids[1] == 0)

    return _out(_matmul(
        "bwd_ff2_in", (f // tn, s // tm, d // tk),
        [(dr2_bf, pl.BlockSpec((tm, tk), lambda j, i, k: (i, k)), w_ff2, pl.BlockSpec((tn, tk), lambda j, i, k: (j, k)), "nt")],
        [(hdn, pl.BlockSpec((tm, tn), lambda j, i, k: (i, j)))],
        [(_sds((s, f), BF16), pl.BlockSpec((tm, tn), lambda j, i, k: (i, j))), (_sds((1, f), F32), pl.BlockSpec((1, tn), lambda j, i, k: (0, j)))],
        (tm, tn), epi, stages), stages)


def _bwd_ff1_in_ln1(dpre, wg_ff1, dr2, xhat1, rstd1, g1, stages=()):
    s, f = dpre.shape
    d = xhat1.shape[1]
    fc = wg_ff1.shape[2]
    tm, tn, tk = _tile(s, 1024), _tile(d, 1024), _tile(fc, 2048)
    nb = fc // tk

    def epi(accs, ex, out, ids):
        first = ids[0] == 0
        xhat = ex[1][...]
        dx1 = accs[0] + DN_ALPHA * ex[0][...]
        dr1 = _layer_norm_bwd(dx1 * ex[3][...], xhat, ex[2][...])
        out[0][...] = dr1
        out[1][...] = dr1.astype(BF16)
        _accum(out[2], _colsum(dx1 * xhat), first)
        _accum(out[3], _colsum(dx1), first)
        _accum(out[4], _colsum(dr1), first)

    def plain(accs, ex, out, ids):
        out[0][...] = accs[0]

    o_spec = pl.BlockSpec((tm, tn), lambda i, j, k: (i, j))
    mm = _out(_matmul(
        "bwd_ff1_in", (s // tm, d // tn, f // tk),
        [(dpre, pl.BlockSpec((tm, tk), lambda i, j, k: (i, k)),
          wg_ff1, pl.BlockSpec((None, tn, tk), lambda i, j, k: (k // nb, j, k % nb)), "nt")],
        [], [(_sds((s, d), F32), o_spec)], (tm, tn), plain, stages), stages, True)
    mm, stage_res = mm if stages else (mm, None)
    vec = (_sds((1, d), F32), "vec")
    rows = _rows_call(
        "ln1_bwd", s, epi, [(mm, "tile"), (dr2, "tile"), (xhat1, "tile"), (rstd1, "tile"), (g1, "vec")],
        [(_sds((s, d), F32), "tile"), (_sds((s, d), BF16), "tile"), vec, vec, vec])
    return (rows, stage_res) if stages else rows


def _bwd_out_in(dr1_bf, w_out, z, pa, pb, stages=()):
    s, d = dr1_bf.shape
    tm, tn, tk = _tile(s, 1024), _tile(d, 1024), _tile(d, 2048)
    ga0, gb0 = 3 * d // tn, 4 * d // tn

    def epi(accs, ex, out, ids):
        dm = accs[0]
        sa = _sigmoid(ex[0][...].astype(F32))
        sb = _sigmoid(ex[1][...].astype(F32))
        out[0][...] = (dm * sa).astype(BF16)
        out[1][...] = (dm * sb).astype(BF16)
        out[2][...] = (dm * ex[2][...].astype(F32) * sa * (1.0 - sa)).astype(BF16)
        out[3][...] = (dm * ex[3][...].astype(F32) * sb * (1.0 - sb)).astype(BF16)

    o_spec = pl.BlockSpec((tm, tn), lambda i, j, k: (i, j))
    return _out(_matmul(
        "bwd_out_in", (s // tm, d // tn, d // tk),
        [(dr1_bf, pl.BlockSpec((tm, tk), lambda i, j, k: (i, k)), w_out, pl.BlockSpec((tn, tk), lambda i, j, k: (j, k)), "nt")],
        [(z, pl.BlockSpec((tm, tn), lambda i, j, k: (i, ga0 + j))), (z, pl.BlockSpec((tm, tn), lambda i, j, k: (i, gb0 + j))),
         (pa, o_spec), (pb, o_spec)],
        [(_sds((s, d), BF16), o_spec)] * 4, (tm, tn), epi, stages), stages)


def _bwd_up_in(name, dp, w_up, stages=()):
    s, d = dp.shape
    n = w_up.shape[0]
    tm, tn, tk = _tile(s, 1024), _tile(n, 1024), _tile(d, 2048)

    def epi(accs, ex, out, ids):
        out[0][...] = accs[0].astype(BF16)

    return _out(_matmul(
        name, (s // tm, n // tn, d // tk),
        [(dp, pl.BlockSpec((tm, tk), lambda i, j, k: (i, k)), w_up, pl.BlockSpec((tn, tk), lambda i, j, k: (j, k)), "nt")],
        [], [(_sds((s, n), BF16), pl.BlockSpec((tm, tn), lambda i, j, k: (i, j)))], (tm, tn), epi, stages), stages, True)


def _bwd_in(dz, wg_in, dr1, stages=()):
    s, d = dr1.shape
    inc = wg_in.shape[2]
    tm, tn, tk = _tile(s, 1024), _tile(d, 1024), _tile(inc, 2560)
    nb = inc // tk

    def epi(accs, ex, out, ids):
        out[0][...] = accs[0] + DN_ALPHA * ex[0][...]

    o_spec = pl.BlockSpec((tm, tn), lambda i, j, k: (i, j))
    return _out(_matmul(
        "bwd_in", (s // tm, d // tn, N_CHIP * nb),
        [(dz, pl.BlockSpec((tm, tk), lambda i, j, k: (i, k)),
          wg_in, pl.BlockSpec((None, tn, tk), lambda i, j, k: (k // nb, j, k % nb)), "nt")],
        [(dr1, o_spec)], [(_sds((s, d), F32), o_spec)], (tm, tn), epi, stages), stages, True)


def _wgrad(name, a, b, col_sharded, stages=()):
    s, ka = a.shape
    n = b.shape[1]
    tm, tk = _tile(ka, 1024), _tile(s, 2048)
    tn = _tile(n // N_CHIP, 1280) if col_sharded else _tile(n, 1024)

    def epi(accs, ex, out, ids):
        out[0][...] = accs[0].astype(BF16)

    if col_sharded:
        nb = (n // N_CHIP) // tn
        o = (_sds((N_CHIP, ka, n // N_CHIP), BF16), pl.BlockSpec((None, tm, tn), lambda i, j, k: (j // nb, i, j % nb)))
    else:
        o = (_sds((ka, n), BF16), pl.BlockSpec((tm, tn), lambda i, j, k: (i, j)))
    res = _out(_matmul(
        name, (ka // tm, n // tn, s // tk),
        [(a, pl.BlockSpec((tk, tm), lambda i, j, k: (k, i)), b, pl.BlockSpec((tk, tn), lambda i, j, k: (k, j)), "tn")],
        [], [o], (tm, tn), epi, stages), stages, True)
    res, stage_res = res if stages else (res, None)
    res = res if col_sharded else res.reshape(N_CHIP, ka // N_CHIP, n)
    return (res, stage_res) if stages else res


def _chunk(s):
    return _tile(s, 512, SUBLANES)


def _zero_pads(ref, s):
    zeros = jnp.zeros((PAD, ref.shape[1]), F32)
    ref[pl.ds(0, PAD), :] = zeros
    ref[pl.ds(PAD + s, PAD), :] = zeros


def _window(ref, t0, t):
    return ref[pl.ds(t0, t + 2 * PAD), :]


def _shift(sup, off, t):
    return sup[PAD + off:PAD + off + t, :]


def _pool_count(t0, t, s, w):
    pos = t0 + lax.broadcasted_iota(jnp.int32, (t, 1), 0)
    return (jnp.minimum(pos + w // 2, s) - jnp.maximum(pos - w // 2, 0)).astype(F32)


def _pool_fwd(z, pool_w, pool_scale, stages=()):
    s = z.shape[0]
    n_g, pg = pool_w.shape[0], pool_w.shape[1]
    assert n_g == len(POOL_WINDOWS) and max(POOL_WINDOWS) // 2 <= PAD
    t = _chunk(s)

    def body(u_ref, w_ref, sc_ref, d_ref, y_ref, pad_ref):
        g = pl.program_id(0)
        _zero_pads(pad_ref, s)
        pad_ref[pl.ds(PAD, s), :] = u_ref[...].astype(F32)
        for gi, w in enumerate(POOL_WINDOWS):
            @pl.when(g == gi)
            def _():
                def step(ch, carry):
                    t0 = pl.multiple_of(ch * t, t)
                    sup = _window(pad_ref, t0, t)
                    acc = _shift(sup, -(w // 2), t)
                    for o in range(-(w // 2) + 1, w // 2):
                        acc = acc + _shift(sup, o, t)
                    dd = (acc * (1.0 / _pool_count(t0, t, s, w)) - _shift(sup, 0, t)).astype(BF16)
                    d_ref[pl.ds(t0, t), :] = dd
                    y = jnp.dot(dd, w_ref[...], preferred_element_type=F32) * sc_ref[...]
                    y_ref[pl.ds(t0, t), :] = y.astype(BF16)
                    return carry

                lax.fori_loop(0, s // t, step, 0)

    blk = pl.BlockSpec((s, pg), lambda g: (0, g))
    res = _host_call(
        "pool_fwd", (n_g,), lambda ins, outs, scr: body(*ins, *outs, *scr), [z, pool_w, pool_scale],
        [blk, pl.BlockSpec((None, pg, pg), lambda g: (g, 0, 0)), pl.BlockSpec((1, pg), lambda g: (0, g))],
        [_sds((s, n_g * pg), BF16)] * 2, [blk, blk], [pltpu.VMEM((s + 2 * PAD, pg), F32)], list(stages))
    return res if stages else res[0]


def _pool_bwd(dsv, dy, pool_w, pool_scale, stages=()):
    s = dsv.shape[0]
    n_g, pg = pool_w.shape[0], pool_w.shape[1]
    t = _chunk(s)

    def body(d_ref, dy_ref, w_ref, sc_ref, du_ref, dw_ref, dsc_ref, epad_ref, dwacc_ref):
        g = pl.program_id(0)
        _zero_pads(epad_ref, s)
        dwacc_ref[...] = jnp.zeros_like(dwacc_ref)
        for gi, w in enumerate(POOL_WINDOWS):
            @pl.when(g == gi)
            def _():
                def first(ch, dsc):
                    t0 = pl.multiple_of(ch * t, t)
                    dd = d_ref[pl.ds(t0, t), :]
                    dyc = dy_ref[pl.ds(t0, t), :].astype(F32)
                    wv = w_ref[...]
                    ypre = jnp.dot(dd, wv, preferred_element_type=F32)
                    dq = (dyc * sc_ref[...]).astype(BF16)
                    dwacc_ref[...] += lax.dot_general(dd, dq, _DIMS["tn"], preferred_element_type=F32)
                    ddv = lax.dot_general(dq, wv, _DIMS["nt"], preferred_element_type=F32)
                    epad_ref[pl.ds(pl.multiple_of(PAD + t0, SUBLANES), t), :] = ddv * (1.0 / _pool_count(t0, t, s, w))
                    return dsc + _colsum(dyc * ypre)

                dsc_ref[...] = lax.fori_loop(0, s // t, first, jnp.zeros((1, pg), F32))

                def second(ch, carry):
                    t0 = pl.multiple_of(ch * t, t)
                    sup = _window(epad_ref, t0, t)
                    acc = _shift(sup, -(w // 2) + 1, t)
                    for o in range(-(w // 2) + 2, w // 2 + 1):
                        acc = acc + _shift(sup, o, t)
                    du_ref[pl.ds(t0, t), :] = (acc - _shift(sup, 0, t) * _pool_count(t0, t, s, w)).astype(BF16)
                    return carry

                lax.fori_loop(0, s // t, second, 0)

        dw_ref[...] = dwacc_ref[...].astype(BF16)

    blk = pl.BlockSpec((s, pg), lambda g: (0, g))
    w_spec = pl.BlockSpec((None, pg, pg), lambda g: (g, 0, 0))
    sc_spec = pl.BlockSpec((1, pg), lambda g: (0, g))
    res = _host_call(
        "pool_bwd", (n_g,), lambda ins, outs, scr: body(*ins, *outs, *scr), [dsv, dy, pool_w, pool_scale],
        [blk, blk, w_spec, sc_spec], [_sds((s, n_g * pg), BF16), _sds((n_g, pg, pg), BF16), _sds((1, n_g * pg), F32)],
        [blk, w_spec, sc_spec], [pltpu.VMEM((s + 2 * PAD, pg), F32), pltpu.VMEM((pg, pg), F32)], list(stages))
    return res if stages else res[0]


def _sigmoid(x):
    return 0.5 * jnp.tanh(0.5 * x) + 0.5


def _softplus(x):
    e = jnp.exp(-jnp.abs(x))
    log1p_e = jnp.where(e < 1e-2, e * (1.0 - e * (0.5 - e * (1.0 / 3.0))), jnp.log(1.0 + e))
    return jnp.maximum(x, 0.0) + log1p_e


_GELU_C = math.sqrt(2.0 / math.pi)


def _gelu(x):
    th = jnp.tanh(_GELU_C * (x + 0.044715 * x * x * x))
    return 0.5 * x * (1.0 + th), th


def _gelu_grad(x, th):
    return 0.5 * (1.0 + th) + 0.5 * x * (1.0 - th * th) * _GELU_C * (1.0 + 3.0 * 0.044715 * x * x)


def _scan_chunk(a_ref, b_ref, o_ref, o_off, carry, t, reverse):
    n = a_ref.shape[1]
    row = lax.broadcasted_iota(jnp.int32, (SUBLANES, n), 0)
    n_groups = t // SUBLANES
    unroll = math.gcd(n_groups, SCAN_UNROLL)
    last = 0 if reverse else SUBLANES - 1

    def step(si, carry):
        for u in range(unroll):
            gi = si * unroll + u
            g = n_groups - 1 - gi if reverse else gi
            r0 = pl.multiple_of(g * SUBLANES, SUBLANES)
            a = a_ref[pl.ds(r0, SUBLANES), :]
            b = b_ref[pl.ds(r0, SUBLANES), :]
            for k in (1, 2, 4):
                keep = row < SUBLANES - k if reverse else row >= k
                sh = SUBLANES - k if reverse else k
                ar = jnp.where(keep, pltpu.roll(a, sh, 0), 1.0)
                br = jnp.where(keep, pltpu.roll(b, sh, 0), 0.0)
                b = a * br + b
                a = a * ar
            o_ref[pl.ds(pl.multiple_of(o_off + r0, SUBLANES), SUBLANES), :] = a * carry + b
            carry = (jnp.broadcast_to(a[last:last + 1, :], a.shape) * carry
                     + jnp.broadcast_to(b[last:last + 1, :], b.shape))
        return carry

    return lax.fori_loop(0, n_groups // unroll, step, carry)


def _lru_params(pk_ref):
    rows = pk_ref[...]
    get = lambda i: rows[i:i + 1, :]
    cw = [get(k) for k in range(4)]
    lam = (get(9), get(10))
    big_l = tuple(-LRU_C * _softplus(-v) for v in lam)
    return cw, get(4), (get(5), get(6)), (get(7), get(8)), lam, big_l


def _conv(sup, cw, cb, t):
    xc = cb + cw[0] * _shift(sup, -2, t)
    for k in range(1, 4):
        xc = xc + cw[k] * _shift(sup, k - 2, t)
    return xc


def _gates(xcb, w_ref, d, bk, ba, bx, big_l):
    pre = jnp.dot(xcb, w_ref[:, pl.ds(d * 2 * bk, 2 * bk)], preferred_element_type=F32)
    r = _sigmoid(pre[:, :bk] + ba[d])
    i = _sigmoid(pre[:, bk:] + bx[d])
    la = big_l[d] * r
    a = jnp.exp(la)
    var = jnp.tanh(-la) * (1.0 + a * a)
    rs = lax.rsqrt(jnp.maximum(var, 1e-30))
    return r, i, a, var * rs, rs


def _lru_specs(s, d, bk):
    u_spec = pl.BlockSpec((s, bk), lambda h: (0, d // bk + h))
    ug_spec = pl.BlockSpec((s, bk), lambda h: (0, 2 * d // bk + h))
    w_spec = pl.BlockSpec((None, bk, 4 * bk), lambda h: (h, 0, 0))
    pk_spec = pl.BlockSpec((None, 16, bk), lambda h: (h, 0, 0))
    blk = pl.BlockSpec((s, bk), lambda h: (0, h))
    return u_spec, ug_spec, w_spec, pk_spec, blk


def _lru_fwd(z, gatew, pk, stages=()):
    s = z.shape[0]
    n_h, bk = gatew.shape[0], gatew.shape[1]
    d = n_h * bk
    t = _chunk(s)
    n_ch = s // t

    def body(u_ref, ug_ref, w_ref, pk_ref, y_ref, upad, h0buf, abuf, bbuf, xcbuf, h1buf):
        _zero_pads(upad, s)
        upad[pl.ds(PAD, s), :] = u_ref[...].astype(F32)
        cw, cb, ba, bx, _, big_l = _lru_params(pk_ref)
        zero = jnp.zeros((SUBLANES, bk), F32)

        def fill(xc, dr):
            _, i, a, sq, _ = _gates(xc.astype(BF16), w_ref, dr, bk, ba, bx, big_l)
            abuf[...] = a
            bbuf[...] = sq * i * xc

        def up(ch, carry):
            t0 = pl.multiple_of(ch * t, t)
            xc = _conv(_window(upad, t0, t), cw, cb, t)
            xcbuf[pl.ds(t0, t), :] = xc
            fill(xc, 0)
            return _scan_chunk(abuf, bbuf, h0buf, t0, carry, t, False)

        lax.fori_loop(0, n_ch, up, zero)

        def down(ci, carry):
            t0 = pl.multiple_of((n_ch - 1 - ci) * t, t)
            fill(xcbuf[pl.ds(t0, t), :], 1)
            carry = _scan_chunk(abuf, bbuf, h1buf, 0, carry, t, True)
            gl, _ = _gelu(ug_ref[pl.ds(t0, t), :].astype(F32))
            y_ref[pl.ds(t0, t), :] = ((h0buf[pl.ds(t0, t), :] + h1buf[...]) * gl).astype(BF16)
            return carry

        lax.fori_loop(0, n_ch, down, zero)

    u_spec, ug_spec, w_spec, pk_spec, blk = _lru_specs(s, d, bk)
    res = _host_call(
        "lru_fwd", (n_h,), lambda ins, outs, scr: body(*ins, *outs, *scr), [z, z, gatew, pk],
        [u_spec, ug_spec, w_spec, pk_spec], [_sds((s, d), BF16)], [blk],
        [pltpu.VMEM((s + 2 * PAD, bk), F32), pltpu.VMEM((s, bk), F32), pltpu.VMEM((t, bk), F32), pltpu.VMEM((t, bk), F32),
         pltpu.VMEM((s, bk), F32), pltpu.VMEM((t, bk), F32)], list(stages))
    return (res[0][0], res[1]) if stages else res[0][0]


def _lru_grads(lam_, hnb, a, sq, rs, r, i, xc, xcb, w_ref, dwacc, d, big_l, acc):
    bk = xc.shape[1]
    dba, dbx, dl = acc
    q = lam_ * i * xc
    dla = lam_ * hnb * a - q * (a * a) * rs
    dpr = dla * big_l * r * (1.0 - r)
    dpi = q * sq * (1.0 - i)
    dprb, dpib = dpr.astype(BF16), dpi.astype(BF16)
    c0 = d * 2 * bk
    dxc = (lam_ * sq * i
           + lax.dot_general(dprb, w_ref[:, pl.ds(c0, bk)], _DIMS["nt"], preferred_element_type=F32)
           + lax.dot_general(dpib, w_ref[:, pl.ds(c0 + bk, bk)], _DIMS["nt"], preferred_element_type=F32))
    dwacc[:, pl.ds(c0, bk)] += lax.dot_general(xcb, dprb, _DIMS["tn"], preferred_element_type=F32)
    dwacc[:, pl.ds(c0 + bk, bk)] += lax.dot_general(xcb, dpib, _DIMS["tn"], preferred_element_type=F32)
    return dxc, (dba + _colsum(dpr), dbx + _colsum(dpi), dl + _colsum(dla * r))


def _lru_bwd(z, dy, gatew, pk, stages=()):
    s = z.shape[0]
    n_h, bk = gatew.shape[0], gatew.shape[1]
    d = n_h * bk
    t = _chunk(s)
    n_ch = s // t

    def body(u_ref, ug_ref, dy_ref, w_ref, pk_ref, du_ref, dug_ref, dw_ref, dpk_ref,
             upad, h0pad, h1pad, dxpad, abuf, bbuf, lbuf, dwacc, edge, xcbuf):
        for ref in (upad, h0pad, h1pad, dxpad):
            _zero_pads(ref, s)
        upad[pl.ds(PAD, s), :] = u_ref[...].astype(F32)
        dwacc[...] = jnp.zeros_like(dwacc)
        cw, cb, ba, bx, lam, big_l = _lru_params(pk_ref)
        zero = jnp.zeros((SUBLANES, bk), F32)
        zrow = jnp.zeros((1, bk), F32)
        rowi = lax.broadcasted_iota(jnp.int32, (t, bk), 0)

        def at(t0):
            return pl.ds(pl.multiple_of(PAD + t0, SUBLANES), t)

        def conv_in(t0):
            xc = xcbuf[pl.ds(t0, t), :]
            return xc, xc.astype(BF16)

        def dh_of(t0):
            ug = ug_ref[pl.ds(t0, t), :].astype(F32)
            gl, th = _gelu(ug)
            dyv = dy_ref[pl.ds(t0, t), :].astype(F32)
            return dyv * gl, dyv * _gelu_grad(ug, th)

        def sweep1(ch, carry):
            t0 = pl.multiple_of(ch * t, t)
            xc = _conv(_window(upad, t0, t), cw, cb, t)
            xcbuf[pl.ds(t0, t), :] = xc
            _, i, a, sq, _ = _gates(xc.astype(BF16), w_ref, 0, bk, ba, bx, big_l)
            abuf[...] = a
            bbuf[...] = sq * i * xc
            return _scan_chunk(abuf, bbuf, h0pad, PAD + t0, carry, t, False)

        lax.fori_loop(0, n_ch, sweep1, zero)

        edge[...] = zero

        def sweep2(ci, st):
            carry_h, carry_l, acc = st
            t0 = pl.multiple_of((n_ch - 1 - ci) * t, t)
            xc, xcb = conv_in(t0)
            _, i1, a1, sq1, _ = _gates(xcb, w_ref, 1, bk, ba, bx, big_l)
            abuf[...] = a1
            bbuf[...] = sq1 * i1 * xc
            carry_h = _scan_chunk(abuf, bbuf, h1pad, PAD + t0, carry_h, t, True)
            dh, dgl = dh_of(t0)
            dug_ref[pl.ds(t0, t), :] = (dgl * (h0pad[at(t0), :] + h1pad[at(t0), :])).astype(BF16)
            r0, i0, a0, sq0, rs0 = _gates(xcb, w_ref, 0, bk, ba, bx, big_l)
            abuf[...] = jnp.where(rowi == t - 1, edge[0:1, :], pltpu.roll(a0, t - 1, 0))
            bbuf[...] = dh
            carry_l = _scan_chunk(abuf, bbuf, lbuf, 0, carry_l, t, True)
            edge[...] = jnp.broadcast_to(a0[0:1, :], (SUBLANES, bk))
            hprev = _shift(_window(h0pad, t0, t), -1, t)
            dxc, acc = _lru_grads(lbuf[...], hprev, a0, sq0, rs0, r0, i0, xc, xcb, w_ref, dwacc, 0, big_l[0], acc)
            dxpad[at(t0), :] = dxc
            return carry_h, carry_l, acc

        _, _, acc0 = lax.fori_loop(0, n_ch, sweep2, (zero, zero, (zrow, zrow, zrow)))

        edge[...] = zero

        def sweep3(ch, st):
            carry_l, acc = st
            t0 = pl.multiple_of(ch * t, t)
            xc, xcb = conv_in(t0)
            r1, i1, a1, sq1, rs1 = _gates(xcb, w_ref, 1, bk, ba, bx, big_l)
            dh, _ = dh_of(t0)
            abuf[...] = jnp.where(rowi == 0, edge[0:1, :], pltpu.roll(a1, 1, 0))
            bbuf[...] = dh
            carry_l = _scan_chunk(abuf, bbuf, lbuf, 0, carry_l, t, False)
            edge[...] = jnp.broadcast_to(a1[t - 1:t, :], (SUBLANES, bk))
            hnext = _shift(_window(h1pad, t0, t), 1, t)
            dxc, acc = _lru_grads(lbuf[...], hnext, a1, sq1, rs1, r1, i1, xc, xcb, w_ref, dwacc, 1, big_l[1], acc)
            dxpad[at(t0), :] += dxc
            return carry_l, acc

        _, acc1 = lax.fori_loop(0, n_ch, sweep3, (zero, (zrow, zrow, zrow)))

        def sweep4(ch, st):
            t0 = pl.multiple_of(ch * t, t)
            sdx = _window(dxpad, t0, t)
            su = _window(upad, t0, t)
            dxc = _shift(sdx, 0, t)
            du = cw[0] * _shift(sdx, 2, t) + cw[1] * _shift(sdx, 1, t) + cw[2] * dxc + cw[3] * _shift(sdx, -1, t)
            du_ref[pl.ds(t0, t), :] = du.astype(BF16)
            return tuple(st[k] + _colsum(dxc * _shift(su, k - 2, t)) for k in range(4)) + (st[4] + _colsum(dxc),)

        conv_g = lax.fori_loop(0, n_ch, sweep4, (zrow,) * 5)

        dpk_ref[...] = jnp.zeros_like(dpk_ref)
        rows = list(conv_g) + [acc0[0], acc1[0], acc0[1], acc1[1],
                               acc0[2] * LRU_C * _sigmoid(-lam[0]), acc1[2] * LRU_C * _sigmoid(-lam[1])]
        for k, v in enumerate(rows):
            dpk_ref[pl.ds(k, 1), :] = v
        dw_ref[...] = dwacc[...].astype(BF16)

    u_spec, ug_spec, w_spec, pk_spec, blk = _lru_specs(s, d, bk)
    padded = pltpu.VMEM((s + 2 * PAD, bk), F32)
    chunk = pltpu.VMEM((t, bk), F32)
    res = _host_call(
        "lru_bwd", (n_h,), lambda ins, outs, scr: body(*ins, *outs, *scr), [z, z, dy, gatew, pk],
        [u_spec, ug_spec, blk, w_spec, pk_spec],
        [_sds((s, d), BF16), _sds((s, d), BF16), _sds((n_h, bk, 4 * bk), BF16), _sds((n_h, 16, bk), F32)],
        [blk, blk, w_spec, pk_spec],
        [padded, padded, padded, padded, chunk, chunk, chunk, pltpu.VMEM((bk, 4 * bk), F32),
         pltpu.VMEM((SUBLANES, bk), F32), pltpu.VMEM((s, bk), F32)], list(stages))
    return res if stages else res[0]


def _scalar(v):
    return jnp.reshape(v, (1,)).astype(jnp.int32)


def _add_sibling(g, r, c):
    _, rows, cols = g.shape
    rh = rows // 2
    tr = _tile(rh, 512, 16)
    nr = rh // tr

    def body(c_ref, g_ref, r_ref, o_ref):
        o_ref[...] = (g_ref[...].astype(F32) + r_ref[...].astype(F32)).astype(BF16)

    spec = pl.BlockSpec((None, tr, cols), lambda k, i, c_ref: (k, i, 0))
    return pl.pallas_call(
        body, name="add_sibling", out_shape=_sds((N_CHIP, rh, cols), BF16),
        grid_spec=pltpu.PrefetchScalarGridSpec(
            num_scalar_prefetch=1, grid=(N_CHIP, nr),
            in_specs=[pl.BlockSpec((None, tr, cols), lambda k, i, c_ref: (k, c_ref[0] * nr + i, 0)), spec], out_specs=spec),
        compiler_params=_cparams(("arbitrary", "arbitrary")),
    )(_scalar(c), g, r)


def _sum_chips(p, rcv, k_me, c):
    _, rh, cols = p.shape
    tr = _tile(rh, 512, 16)
    nr = rh // tr

    def body(kc_ref, p_ref, r_ref, o_ref):
        acc = p_ref[...].astype(F32)
        for j in range(3):
            acc = acc + r_ref[j].astype(F32)
        o_ref[...] = acc

    return pl.pallas_call(
        body, name="sum_chips", out_shape=_sds((2 * rh, cols), F32),
        grid_spec=pltpu.PrefetchScalarGridSpec(
            num_scalar_prefetch=1, grid=(nr,),
            in_specs=[pl.BlockSpec((None, tr, cols), lambda i, kc_ref: (kc_ref[0], i, 0)),
                      pl.BlockSpec((3, tr, cols), lambda i, kc_ref: (0, i, 0))],
            out_specs=pl.BlockSpec((tr, cols), lambda i, kc_ref: (kc_ref[1] * nr + i, 0))),
        compiler_params=_cparams(("arbitrary",)),
    )(jnp.stack([k_me, c]).astype(jnp.int32), p, rcv)


def _sum_devices(g):
    def body(g_ref, o_ref):
        acc = g_ref[0]
        for dev in range(1, N_DEV):
            acc = acc + g_ref[dev]
        o_ref[...] = acc

    return pl.pallas_call(body, name="sum_devices", out_shape=_sds(g.shape[1:], F32))(g)


def _adamw(w, g, m, v):
    rows, cols = w.shape
    tr = _tile(rows, 256, SUBLANES)

    def body(ins, outs, scr):
        w_ref, g_ref, m_ref, v_ref = ins
        go_ref, d_ref, nm_ref, nv_ref = outs
        gv = g_ref[...]
        go_ref[...] = gv
        nm = ADAM_B1 * m_ref[...] + (1.0 - ADAM_B1) * gv
        nv = ADAM_B2 * v_ref[...] + (1.0 - ADAM_B2) * (gv * gv)
        m_hat = nm / (1.0 - ADAM_B1 ** ADAM_STEP)
        v_hat = nv / (1.0 - ADAM_B2 ** ADAM_STEP)
        d_ref[...] = -ADAM_LR * (m_hat / (jnp.sqrt(v_hat) + ADAM_EPS) + ADAM_WD * w_ref[...])
        nm_ref[...] = nm
        nv_ref[...] = nv

    spec = pl.BlockSpec((tr, cols), lambda i: (i, 0))
    return _host_call("adamw", (rows // tr,), body, [w, g, m, v], [spec] * 4, [_sds((rows, cols), F32)] * 4, [spec] * 4, [], [])[0]


def _pack(vs, unit):
    flat = jnp.concatenate([v.reshape(-1).astype(F32) for v in vs])
    pad = (-flat.shape[0]) % unit
    if pad:
        flat = jnp.concatenate([flat, jnp.zeros((pad,), F32)])
    return flat.reshape(-1, 128)


def _unpack(p, like):
    flat = p.reshape(-1)
    out, off = [], 0
    for v in like:
        n = math.prod(v.shape)
        out.append(flat[off:off + n].reshape(v.shape))
        off += n
    return out


def kernel(x, w_in, pool_w, pool_scale, conv_w, conv_b, lru_wa, lru_ba, lru_wx, lru_bx, lru_lambda, w_pool_up, w_lru_up, w_out, b_out, ln1_g, ln1_b, w_ff1, b_ff1, w_ff2, b_ff2, ln2_g, ln2_b, loss_target, m_w_in, m_pool_w, m_pool_scale, m_conv_w, m_conv_b, m_lru_wa, m_lru_ba, m_lru_wx, m_lru_bx, m_lru_lambda, m_w_pool_up, m_w_lru_up, m_w_out, m_b_out, m_ln1_g, m_ln1_b, m_w_ff1, m_b_ff1, m_w_ff2, m_b_ff2, m_ln2_g, m_ln2_b, v_w_in, v_pool_w, v_pool_scale, v_conv_w, v_conv_b, v_lru_wa, v_lru_ba, v_lru_wx, v_lru_bx, v_lru_lambda, v_w_pool_up, v_w_lru_up, v_w_out, v_b_out, v_ln1_g, v_ln1_b, v_w_ff1, v_b_ff1, v_w_ff2, v_b_ff2, v_ln2_g, v_ln2_b):
    given = dict(locals())
    wt = {n: given[n] for n in WEIGHTS}
    mom = {n: given["m_" + n] for n in WEIGHTS}
    vel = {n: given["v_" + n] for n in WEIGHTS}

    ix, iy, ic = _mesh_pos()
    k_me = 2 * ix + iy
    s, d = x.shape[1], x.shape[2]
    ds = d // N_CHIP
    n_g, pgs, pg = pool_w.shape[1], pool_w.shape[2], pool_w.shape[3]
    n_h, bks, bk = lru_wa.shape[2], lru_wa.shape[3], lru_wa.shape[4]
    f = b_ff1.shape[1]
    x2 = x[0]
    vec = lambda a: a.reshape(1, -1)

    sharded_vecs = [conv_w[0], lru_ba[0], lru_bx[0], lru_lambda[0]]
    rows_sv = jnp.concatenate(sharded_vecs + [jnp.zeros((6, ds), F32)], axis=0)
    sv = _all_gather_small(rows_sv)
    sv = sv.reshape(N_CHIP, 2, 16, ds)[:, 0].transpose(1, 0, 2).reshape(16, d)
    conv_w_f, ba_f, bx_f, lam_f = sv[0:4], sv[4:6], sv[6:8], sv[8:10]
    pk = jnp.concatenate([conv_w_f, conv_b, ba_f, bx_f, lam_f, jnp.zeros((5, d), F32)], axis=0)
    pk = pk.reshape(16, n_h, bk).transpose(1, 0, 2)

    def gate_stack(wa, wx):
        return jnp.stack([wa[0], wx[0]], axis=1)

    mats = {
        "w_in": w_in[0], "w_pool_up": w_pool_up[0], "w_lru_up": w_lru_up[0], "w_out": w_out[0],
        "w_ff1": w_ff1[0], "w_ff2": w_ff2[0],
        "pool_w": pool_w[0].reshape(n_g * pgs, pg),
        "gate_w": gate_stack(lru_wa, lru_wx).reshape(4 * n_h * bks, bk),
    }
    names = list(mats)
    placed = {n: _cast_place(mats[n], k_me) for n in names}

    def add_sibling(gs, swapped):
        return [_add_sibling(g, r, ic) for g, r in zip(gs, swapped)]

    def sum_chips(ps, received):
        return [_sum_chips(p, r, k_me, ic) for p, r in zip(ps, received)]

    first = [placed[n] for n in ("w_in", "pool_w", "gate_w")]
    (bufs,) = _run_stages("gather_first", [_chain([
        _gather_direct(first), _together([_gather_relay(first), _gather_d2d(first, (0, 1))]), _gather_d2d(first, (2,))])])
    wg_in = bufs[0]
    wf_pool = bufs[1].reshape(N_CHIP, n_g, pgs, pg).transpose(1, 0, 2, 3).reshape(n_g, pg, pg)
    wf_gate = bufs[2].reshape(N_CHIP, 2, 2, n_h, bks, bk).transpose(3, 0, 4, 1, 2, 5).reshape(n_h, bk, 4 * bk)

    z, (wb_mix, wb_ff1) = _fwd_in(x2, wg_in, stages=[
        _gather_direct([placed[n] for n in ("w_pool_up", "w_lru_up", "w_out")]), _gather_direct([placed["w_ff1"]])])
    (d_pool, y_pool), (wb_mix,) = _pool_fwd(z, wf_pool, pool_scale, stages=[_gather_relay(wb_mix)])
    y_lru, (wb_ff1, wb_mix, wb_ff2) = _lru_fwd(z, wf_gate, pk, stages=[
        _gather_relay(wb_ff1), _gather_d2d(wb_mix), _gather_direct([placed["w_ff2"]])])
    wf_pu, wf_lu, wf_out = (b.reshape(d, d) for b in wb_mix)
    (m_mix, p_a, p_b), (wb_ff1, wb_ff2) = _fwd_merge(y_pool, y_lru, wf_pu, wf_lu, z, stages=[
        _gather_d2d(wb_ff1), _gather_relay(wb_ff2)])
    wg_ff1 = wb_ff1[0]
    (xhat1, x1_bf, rstd1), (wb_ff2,) = _fwd_out_ln1(m_mix, wf_out, x2, b_out, ln1_g, ln1_b, stages=[_gather_d2d(wb_ff2)])
    hdn = _fwd_ff1(x1_bf, wg_ff1, b_ff1)
    wf_ff2 = wb_ff2[0].reshape(f, d)
    dr2, dr2_bf, g_ln2_g, g_ln2_b, g_b_ff2, loss_part = _fwd_ff2_ln2_loss(
        hdn, wf_ff2, xhat1, ln1_g, ln1_b, b_ff2, ln2_g, ln2_b, loss_target[0])

    dpre, g_b_ff1 = _bwd_ff2_in(dr2_bf, wf_ff2, hdn)
    g_ff = [_wgrad("wgrad_ff1", x1_bf, dpre, True), _wgrad("wgrad_ff2", hdn, dr2_bf, False)]
    (dr1, dr1_bf, g_ln1_g, g_ln1_b, g_b_out), (swapped,) = _bwd_ff1_in_ln1(
        dpre, wg_ff1, dr2, xhat1, rstd1, ln1_g, stages=[_swap_halves(g_ff)])
    sums_ff = add_sibling(g_ff, swapped)
    dp_a, dp_b, dg_a, dg_b = _bwd_out_in(dr1_bf, wf_out, z, p_a, p_b)
    dy_pool = _bwd_up_in("bwd_pool_up_in", dp_a, wf_pu)
    dy_lru = _bwd_up_in("bwd_lru_up_in", dp_b, wf_lu)
    g_mix = [_wgrad("wgrad_pool_up", y_pool, dp_a, False), _wgrad("wgrad_lru_up", y_lru, dp_b, False),
             _wgrad("wgrad_out", m_mix, dr1_bf, False)]
    (du_pool, g_pool_w, g_pool_scale), (swapped,) = _pool_bwd(
        d_pool, dy_pool, wf_pool, pool_scale, stages=[_swap_halves(g_mix)])
    sums_mix = add_sibling(g_mix, swapped)
    (du_lru, du_gate, g_gate_w, g_pk), (recv_ff, recv_mix) = _lru_bwd(
        z, dy_lru, wf_gate, pk, stages=[_scatter_chips(sums_ff), _scatter_chips(sums_mix)])
    halves = sum_chips(sums_ff + sums_mix, recv_ff + recv_mix)
    g_small = [g_pool_w.reshape(n_g, N_CHIP, pgs, pg).transpose(1, 0, 2, 3).reshape(N_CHIP, n_g * pgs, pg),
               g_gate_w.reshape(n_h, N_CHIP, bks, 2, 2, bk).transpose(1, 3, 4, 0, 2, 5).reshape(N_CHIP, 4 * n_h * bks, bk)]
    dz = jnp.concatenate([du_pool, du_lru, du_gate, dg_a, dg_b], axis=1)
    g_in, (joined, swapped) = _wgrad("wgrad_in", x2, dz, True, stages=[_join_halves(halves), _swap_halves(g_small)])
    g_mat = dict(zip(["w_ff1", "w_ff2", "w_pool_up", "w_lru_up", "w_out"], joined))
    sums_small = add_sibling(g_small, swapped)

    def stacked(tree):
        return gate_stack(tree["lru_wa"], tree["lru_wx"]).reshape(4 * n_h * bks, bk)

    res = {}

    def update(n):
        if n == "gate_w":
            outs = [o.reshape(2, 2, n_h, bks, bk) for o in _adamw(stacked(wt), g_mat[n], stacked(mom), stacked(vel))]
            res["lru_wa"] = [o[:, 0][None] for o in outs]
            res["lru_wx"] = [o[:, 1][None] for o in outs]
        else:
            shp2 = mats[n].shape
            outs = _adamw(wt[n].reshape(shp2), g_mat[n], mom[n].reshape(shp2), vel[n].reshape(shp2))
            res[n] = [o.reshape(wt[n].shape) for o in outs]

    (swapped,) = _run_stages("swap_in", [_swap_halves([g_in])])
    sums_in = add_sibling([g_in], swapped)
    grad_x, (recv_small, recv_in) = _bwd_in(dz, wg_in, dr1, stages=[_scatter_chips(sums_small), _scatter_chips(sums_in)])
    halves = sum_chips(sums_small + sums_in, recv_small + recv_in)
    (joined,) = _run_stages("join_last", [_join_halves(halves)])
    g_mat.update(zip(["pool_w", "gate_w", "w_in"], joined))
    for n in names:
        update(n)

    g_pk = g_pk.transpose(1, 0, 2).reshape(16, d)
    vec_full = {
        "pool_scale": g_pool_scale, "conv_w": g_pk[0:4], "conv_b": g_pk[4:5],
        "lru_ba": g_pk[5:7], "lru_bx": g_pk[7:9], "lru_lambda": g_pk[9:11],
        "b_out": g_b_out, "ln1_g": g_ln1_g, "ln1_b": g_ln1_b, "b_ff1": g_b_ff1, "b_ff2": g_b_ff2,
        "ln2_g": g_ln2_g, "ln2_b": g_ln2_b,
    }
    vnames = list(vec_full)
    vg = _sum_devices(_all_gather_small(_pack([vec_full[n] for n in vnames], 1024)))
    vg = dict(zip(vnames, _unpack(vg, [vec_full[n] for n in vnames])))
    for n in ("conv_w", "lru_ba", "lru_bx", "lru_lambda"):
        vg[n] = lax.dynamic_slice_in_dim(vg[n], k_me * ds, ds, axis=1)
    vg = {n: vg[n].reshape(wt[n].shape) for n in vnames}
    upd = _adamw(_pack([wt[n] for n in vnames], 1024), _pack([vg[n] for n in vnames], 1024),
                 _pack([mom[n] for n in vnames], 1024), _pack([vel[n] for n in vnames], 1024))
    upd = [_unpack(u, [wt[n] for n in vnames]) for u in upd]
    for i, n in enumerate(vnames):
        res[n] = [vg[n], upd[1][i], upd[2][i], upd[3][i]]

    loss = lax.psum(loss_part[0, 0], ("x", "y", "c"))
    return (loss, grad_x[None], *[res[n][0] for n in WEIGHTS], *[res[n][1] for n in WEIGHTS],
            *[res[n][2] for n in WEIGHTS], *[res[n][3] for n in WEIGHTS])
```

```python
import functools
import math

import jax
import jax.numpy as jnp
from jax import lax
from jax.experimental import pallas as pl
from jax.experimental.pallas import tpu as pltpu

F32 = jnp.float32
BF16 = jnp.bfloat16
MESH = pl.DeviceIdType.MESH
ANY = pl.BlockSpec(memory_space=pl.ANY)

N_CHIP = 4
N_DEV = 8
VMEM_LIMIT_BYTES = 56 * 1024 * 1024
SUBLANES = 8
PAD = 8
SCAN_UNROLL = 8

POOL_WINDOWS = (2, 4, 8, 16)
LRU_C = 8.0
DN_ALPHA = 2.0 ** 0.25
LN_EPS = 1e-5
ADAM_LR, ADAM_B1, ADAM_B2, ADAM_EPS, ADAM_WD, ADAM_STEP = 0.001, 0.9, 0.999, 1e-08, 0.01, 10

WEIGHTS = ("w_in", "pool_w", "pool_scale", "conv_w", "conv_b", "lru_wa", "lru_ba", "lru_wx", "lru_bx", "lru_lambda",
           "w_pool_up", "w_lru_up", "w_out", "b_out", "ln1_g", "ln1_b", "w_ff1", "b_ff1", "w_ff2", "b_ff2", "ln2_g", "ln2_b")


def _cparams(sem=None):
    return pltpu.CompilerParams(dimension_semantics=sem, vmem_limit_bytes=VMEM_LIMIT_BYTES)


def _tile(dim, pref, unit=128):
    if dim <= pref:
        return dim
    t = (pref // unit) * unit
    while t > unit and dim % t:
        t -= unit
    assert dim % t == 0, (dim, pref)
    return t


def _mesh_pos():
    x, y, c = lax.axis_index("x"), lax.axis_index("y"), lax.axis_index("c")
    return x, y, c


def _other_chips(x, y):
    return [(1 - x, y), (x, 1 - y), (1 - x, 1 - y)]


def _all_gather_small(v):
    m_per, n = v.shape

    def body(x_ref, out_ref, send_sems, recv_sems, local_sem):
        x, y, c = _mesh_pos()
        me, sibling = (x, y, c), (x, y, 1 - c)
        chips = _other_chips(x, y)

        def rows(px, py, pc):
            return out_ref.at[4 * px + 2 * py + pc]

        def copy(k, block, to, src=None):
            return pltpu.make_async_remote_copy(
                src_ref=rows(*block) if src is None else src, dst_ref=rows(*block),
                send_sem=send_sems.at[k], recv_sem=recv_sems.at[k], device_id=to, device_id_type=MESH)

        mine = pltpu.make_async_copy(x_ref, rows(*me), local_sem)
        mine.start()
        first = [copy(0, me, sibling, src=x_ref)]
        first += [copy(1 + j, me, (*chip, c), src=x_ref) for j, chip in enumerate(chips)]
        for cp in first:
            cp.start()
        passed = [copy(4 + j, (*chip, c), sibling) for j, chip in enumerate(chips)]
        for j, chip in enumerate(chips):
            copy(1 + j, (*chip, c), me).wait_recv()
            passed[j].start()
        copy(0, sibling, me).wait_recv()
        for j, chip in enumerate(chips):
            copy(4 + j, (*chip, 1 - c), me).wait_recv()
        for cp in first + passed:
            cp.wait_send()
        mine.wait()

    return pl.pallas_call(
        body, name="all_gather_small",
        out_shape=jax.ShapeDtypeStruct((N_DEV, m_per, n), v.dtype),
        in_specs=[pl.BlockSpec(memory_space=pltpu.VMEM)],
        out_specs=pl.BlockSpec(memory_space=pltpu.VMEM),
        scratch_shapes=[pltpu.SemaphoreType.DMA((7,)), pltpu.SemaphoreType.DMA((7,)), pltpu.SemaphoreType.DMA],
    )(v)


class _Stage:
    def __init__(self, srcs, bufs, news, n_sems, copies):
        self.srcs, self.bufs, self.news, self.n_sems, self.copies = list(srcs), list(bufs), list(news), n_sems, copies
        self.phases = [(copies, 0)]


class _SemsFrom:
    def __init__(self, ref, offset):
        self.ref, self.offset, self.at = ref, offset, self

    def __getitem__(self, s):
        return self.ref.at[self.offset + s]


def _chain(stages):
    chained = _Stage([], stages[0].bufs, [], sum(st.n_sems for st in stages), None)
    chained.phases, first = [], 0
    for st in stages:
        chained.phases.append((st.copies, first))
        first += st.n_sems
    return chained


def _remote(src, dst, send_sems, recv_sems, s, to):
    return pltpu.make_async_remote_copy(src_ref=src, dst_ref=dst, send_sem=send_sems.at[s], recv_sem=recv_sems.at[s],
                                        device_id=to, device_id_type=MESH)


def _stage_operands(stages, n_in, n_out):
    ins, outs, aliases, scratch = [], [], {}, []
    for st in stages:
        for i in range(len(st.bufs)):
            aliases[n_in + len(ins) + len(st.srcs) + i] = n_out + len(outs) + i
        ins += st.srcs + st.bufs
        outs += [jax.ShapeDtypeStruct(b.shape, b.dtype) for b in st.bufs] + st.news
        scratch += [pltpu.SemaphoreType.DMA((st.n_sems,)), pltpu.SemaphoreType.DMA((st.n_sems,))]
    return ins, outs, aliases, scratch


def _stage_refs(stages, in_refs, out_refs, sem_refs):
    parts, i, o = [], 0, 0
    for n, st in enumerate(stages):
        src = in_refs[i:i + len(st.srcs)]
        i += len(st.srcs) + len(st.bufs)
        buf = out_refs[o:o + len(st.bufs)]
        new = out_refs[o + len(st.bufs):o + len(st.bufs) + len(st.news)]
        o += len(st.bufs) + len(st.news)
        parts.append((src, buf, new, sem_refs[2 * n], sem_refs[2 * n + 1]))
    return parts


def _stage_results(stages, res):
    out, o = [], 0
    for st in stages:
        n = len(st.bufs) + len(st.news)
        out.append(list(res[o:o + n]))
        o += n
    return out


def _stages_start(stages, parts):
    for st, part in zip(stages, parts):
        for cp in st.copies(*part)[0]:
            cp.start()


def _stages_wait(stages, parts):
    for st, part in zip(stages, parts):
        started, landing = st.copies(*part)
        for cp in landing:
            cp.wait_recv()
        for cp in started:
            cp.wait_send()


def _run_stages(name, stages):
    ins, outs, aliases, scratch = _stage_operands(stages, 0, 0)

    def body(*refs):
        parts = _stage_refs(stages, refs[:len(ins)], refs[len(ins):len(ins) + len(outs)], refs[len(ins) + len(outs):])
        for st, (src, buf, new, send_sems, recv_sems) in zip(stages, parts):
            for copies, first in st.phases:
                started, landing = copies(src, buf, new, _SemsFrom(send_sems, first), _SemsFrom(recv_sems, first))
                for cp in started:
                    cp.start()
                for cp in landing:
                    cp.wait_recv()
                for cp in started:
                    cp.wait_send()

    res = pl.pallas_call(
        body, name=name, out_shape=outs, in_specs=[ANY] * len(ins), out_specs=[ANY] * len(outs),
        input_output_aliases=aliases, scratch_shapes=scratch)(*ins)
    return _stage_results(stages, res)


def _gather_direct(ts):
    def copies(src, buf, new, send_sems, recv_sems):
        x, y, c = _mesh_pos()
        started, landing = [], []
        for t in range(len(ts)):
            rh = ts[t].shape[1] // 2
            rows = pl.ds(c * rh, rh)
            mine = buf[t].at[2 * x + y, rows]
            for j, chip in enumerate(_other_chips(x, y)[:2]):
                theirs = buf[t].at[2 * chip[0] + chip[1], rows]
                started.append(_remote(mine, mine, send_sems, recv_sems, 2 * t + j, (*chip, c)))
                landing.append(_remote(theirs, theirs, send_sems, recv_sems, 2 * t + j, (x, y, c)))
        return started, landing

    return _Stage([], ts, [], 2 * len(ts), copies)


def _gather_relay(ts):
    def copies(src, buf, new, send_sems, recv_sems):
        x, y, c = _mesh_pos()
        (x_nb, y_nb, diag) = _other_chips(x, y)
        block = lambda chip: 2 * chip[0] + chip[1]
        started, landing = [], []
        for t in range(len(ts)):
            rq = ts[t].shape[1] // 4
            q0, q1 = pl.ds(2 * c * rq, rq), pl.ds((2 * c + 1) * rq, rq)
            from_y, from_x = buf[t].at[block(y_nb), q0], buf[t].at[block(x_nb), q1]
            started.append(_remote(from_y, from_y, send_sems, recv_sems, 2 * t, (*x_nb, c)))
            started.append(_remote(from_x, from_x, send_sems, recv_sems, 2 * t + 1, (*y_nb, c)))
            for j, q in enumerate((q0, q1)):
                lands = buf[t].at[block(diag), q]
                landing.append(_remote(lands, lands, send_sems, recv_sems, 2 * t + j, (x, y, c)))
        return started, landing

    return _Stage([], ts, [], 2 * len(ts), copies)


def _together(stages):
    def copies(src, buf, new, send_sems, recv_sems):
        started, landing, first = [], [], 0
        for st in stages:
            more = st.copies(src, buf, new, _SemsFrom(send_sems, first), _SemsFrom(recv_sems, first))
            started, landing, first = started + more[0], landing + more[1], first + st.n_sems
        return started, landing

    return _Stage([], stages[0].bufs, [], sum(st.n_sems for st in stages), copies)


def _gather_d2d(ts, which=(0, 1, 2)):
    def copies(src, buf, new, send_sems, recv_sems):
        x, y, c = _mesh_pos()
        started, landing = [], []
        for t in range(len(ts)):
            rh = ts[t].shape[1] // 2
            for j, chip in enumerate(_other_chips(x, y)):
                if j not in which:
                    continue
                got = buf[t].at[2 * chip[0] + chip[1], pl.ds(c * rh, rh)]
                other = buf[t].at[2 * chip[0] + chip[1], pl.ds((1 - c) * rh, rh)]
                started.append(_remote(got, got, send_sems, recv_sems, 3 * t + j, (x, y, 1 - c)))
                landing.append(_remote(other, other, send_sems, recv_sems, 3 * t + j, (x, y, c)))
        return started, landing

    return _Stage([], ts, [], 3 * len(ts), copies)


def _swap_halves(gs):
    def copies(src, buf, new, send_sems, recv_sems):
        x, y, c = _mesh_pos()
        started, landing = [], []
        for t in range(len(gs)):
            rh = gs[t].shape[1] // 2
            started.append(_remote(src[t].at[:, pl.ds((1 - c) * rh, rh)], new[t], send_sems, recv_sems, t, (x, y, 1 - c)))
            landing.append(_remote(new[t], new[t], send_sems, recv_sems, t, (x, y, c)))
        return started, landing

    news = [jax.ShapeDtypeStruct((g.shape[0], g.shape[1] // 2, g.shape[2]), g.dtype) for g in gs]
    return _Stage(gs, [], news, len(gs), copies)


def _send_to_sibling(gs):
    def copies(src, buf, new, send_sems, recv_sems):
        x, y, c = _mesh_pos()
        started = [_remote(src[t], new[t], send_sems, recv_sems, t, (x, y, 1 - c)) for t in range(len(gs))]
        landing = [_remote(new[t], new[t], send_sems, recv_sems, t, (x, y, c)) for t in range(len(gs))]
        return started, landing

    return _Stage(gs, [], [jax.ShapeDtypeStruct(g.shape, g.dtype) for g in gs], len(gs), copies)


def _scatter_chips(ps):
    def copies(src, buf, new, send_sems, recv_sems):
        x, y, c = _mesh_pos()
        started, landing = [], []
        for t in range(len(ps)):
            for j, chip in enumerate(_other_chips(x, y)):
                started.append(_remote(src[t].at[2 * chip[0] + chip[1]], new[t].at[j], send_sems, recv_sems, 3 * t + j, (*chip, c)))
                landing.append(_remote(new[t].at[j], new[t].at[j], send_sems, recv_sems, 3 * t + j, (x, y, c)))
        return started, landing

    return _Stage(ps, [], [jax.ShapeDtypeStruct((3,) + p.shape[1:], p.dtype) for p in ps], 3 * len(ps), copies)


def _join_halves(fs):
    def copies(src, buf, new, send_sems, recv_sems):
        x, y, c = _mesh_pos()
        started, landing = [], []
        for t in range(len(fs)):
            rh = fs[t].shape[0] // 2
            mine = buf[t].at[pl.ds(c * rh, rh)]
            theirs = buf[t].at[pl.ds((1 - c) * rh, rh)]
            started.append(_remote(mine, mine, send_sems, recv_sems, t, (x, y, 1 - c)))
            landing.append(_remote(theirs, theirs, send_sems, recv_sems, t, (x, y, c)))
        return started, landing

    return _Stage([], fs, [], len(fs), copies)


def _cast_place(w, k_me):
    rows, cols = w.shape
    tr = _tile(rows, 512, 16)

    def body(k_ref, w_ref, o_ref):
        o_ref[...] = w_ref[...].astype(BF16)

    return pl.pallas_call(
        body, name="cast_place", out_shape=_sds((N_CHIP, rows, cols), BF16),
        grid_spec=pltpu.PrefetchScalarGridSpec(
            num_scalar_prefetch=1, grid=(rows // tr,),
            in_specs=[pl.BlockSpec((tr, cols), lambda i, k_ref: (i, 0))],
            out_specs=pl.BlockSpec((None, tr, cols), lambda i, k_ref: (k_ref[0], i, 0))),
        compiler_params=_cparams(("arbitrary",)),
    )(_scalar(k_me), w)


_DIMS = {"nn": (((1,), (0,)), ((), ())), "nt": (((1,), (1,)), ((), ())), "tn": (((0,), (0,)), ((), ()))}


def _accum(ref, val, first):
    @pl.when(first)
    def _():
        ref[...] = val

    @pl.when(jnp.logical_not(first))
    def _():
        ref[...] += val


def _grid_edges(grid):
    ids = [pl.program_id(ax) for ax in range(len(grid))]
    first = functools.reduce(jnp.logical_and, [i == 0 for i in ids])
    last = functools.reduce(jnp.logical_and, [i == n - 1 for i, n in zip(ids, grid)])
    return first, last


def _host_call(name, grid, body, operands, in_specs, out_shape, out_specs, scratch, stages, prefetch=None):
    s_ins, s_outs, aliases, s_scratch = _stage_operands(stages, len(operands), len(out_shape))
    n_in, n_out, n_scr = len(operands), len(out_shape), len(scratch)
    n_pre = 0 if prefetch is None else 1

    def full_body(*refs):
        refs = refs[n_pre:]
        in_refs = refs[:n_in]
        s_in_refs = refs[n_in:n_in + len(s_ins)]
        o0 = n_in + len(s_ins)
        out_refs = refs[o0:o0 + n_out]
        s_out_refs = refs[o0 + n_out:o0 + n_out + len(s_outs)]
        c0 = o0 + n_out + len(s_outs)
        scr_refs = refs[c0:c0 + n_scr]
        if stages:
            parts = _stage_refs(stages, s_in_refs, s_out_refs, refs[c0 + n_scr:])
            first, last = _grid_edges(grid)
            pl.when(first)(lambda: _stages_start(stages, parts))
        body(in_refs, out_refs, scr_refs)
        if stages:
            pl.when(last)(lambda: _stages_wait(stages, parts))

    all_in = list(in_specs) + [ANY] * len(s_ins)
    all_out = list(out_specs) + [ANY] * len(s_outs)
    all_scratch = list(scratch) + s_scratch
    params = _cparams(("arbitrary",) * len(grid))
    if prefetch is None:
        res = pl.pallas_call(
            full_body, name=name, grid=grid, in_specs=all_in, out_specs=all_out, out_shape=list(out_shape) + s_outs,
            input_output_aliases=aliases, scratch_shapes=all_scratch, compiler_params=params,
        )(*operands, *s_ins)
    else:
        res = pl.pallas_call(
            full_body, name=name, out_shape=list(out_shape) + s_outs,
            grid_spec=pltpu.PrefetchScalarGridSpec(num_scalar_prefetch=1, grid=grid, in_specs=all_in, out_specs=all_out,
                                                   scratch_shapes=all_scratch),
            input_output_aliases={i + 1: o for i, o in aliases.items()}, compiler_params=params,
        )(prefetch, *operands, *s_ins)
    return list(res[:n_out]), _stage_results(stages, res[n_out:])


def _matmul(name, grid, pairs, extras, outs, acc_shape, epilogue, stages=(), prefetch=None, lhs_to_epilogue=False):
    n_p = len(pairs)
    n_k = grid[-1]
    dims = [_DIMS[p[4]] for p in pairs]

    def body(in_refs, out, accs):
        ab, ex = in_refs[:2 * n_p], in_refs[2 * n_p:]
        if lhs_to_epilogue:
            ex = [ab[0]] + list(ex)
        ids = [pl.program_id(ax) for ax in range(len(grid))]
        k = ids[-1]

        def dot(p):
            return lax.dot_general(ab[2 * p][...].astype(BF16), ab[2 * p + 1][...].astype(BF16), dims[p],
                                   preferred_element_type=F32)

        if n_k == 1:
            epilogue([dot(p) for p in range(n_p)], ex, out, ids)
            return

        @pl.when(k == 0)
        def _():
            for acc in accs:
                acc[...] = jnp.zeros_like(acc)

        for p in range(n_p):
            accs[p][...] += dot(p)

        @pl.when(k == n_k - 1)
        def _():
            epilogue([acc[...] for acc in accs], ex, out, ids)

    in_specs = []
    operands = []
    for a, a_spec, b, b_spec, _ in pairs:
        in_specs += [a_spec, b_spec]
        operands += [a, b]
    for e, e_spec in extras:
        in_specs.append(e_spec)
        operands.append(e)
    res, stage_res = _host_call(name, grid, body, operands, in_specs, [o[0] for o in outs], [o[1] for o in outs],
                                [pltpu.VMEM(acc_shape, F32) for _ in pairs] if n_k > 1 else [], list(stages), prefetch)
    return (res, stage_res) if stages else res


def _out(res, stages, single=False):
    outs = res[0] if stages else res
    outs = outs[0] if single else outs
    return (outs, res[1]) if stages else outs


def _sds(shape, dtype):
    return jax.ShapeDtypeStruct(shape, dtype)


def _row(n):
    return pl.BlockSpec((1, n), lambda *_: (0, 0))


def _layer_norm(r):
    mu = jnp.mean(r, axis=-1, keepdims=True)
    xc = r - mu
    var = jnp.mean(xc * xc, axis=-1, keepdims=True)
    rstd = lax.rsqrt(var + LN_EPS)
    return xc * rstd, rstd


def _layer_norm_bwd(dxhat, xhat, rstd):
    m1 = jnp.mean(dxhat, axis=-1, keepdims=True)
    m2 = jnp.mean(dxhat * xhat, axis=-1, keepdims=True)
    return rstd * (dxhat - m1 - xhat * m2)


def _colsum(v):
    return jnp.sum(v, axis=0, keepdims=True)


ROW_TILE = 256


def _rows_call(name, s, epi, ins, outs):
    tr = _tile(s, ROW_TILE, SUBLANES)

    def spec(shape, kind):
        n = shape[1]
        return pl.BlockSpec((tr, n), lambda i: (i, 0)) if kind == "tile" else pl.BlockSpec((1, n), lambda i: (0, 0))

    def body(in_refs, out_refs, scr):
        epi([in_refs[0][...]], in_refs[1:], out_refs, [pl.program_id(0)])

    return _host_call(name, (s // tr,), body, [a for a, _ in ins], [spec(a.shape, k) for a, k in ins],
                      [o for o, _ in outs], [spec(o.shape, k) for o, k in outs], [], [])[0]


def _plain_matmul(name, a, b, mode):
    m, k_dim = a.shape
    n = b.shape[1]
    tm, tn, tk = _tile(m, 1024), _tile(n, 1024), _tile(k_dim, 2048)

    def epi(accs, ex, out, ids):
        out[0][...] = accs[0]

    assert mode == "nn"
    return _matmul(
        name, (m // tm, n // tn, k_dim // tk),
        [(a, pl.BlockSpec((tm, tk), lambda i, j, k: (i, k)), b, pl.BlockSpec((tk, tn), lambda i, j, k: (k, j)), "nn")],
        [], [(_sds((m, n), F32), pl.BlockSpec((tm, tn), lambda i, j, k: (i, j)))], (tm, tn), epi)[0]


def _fwd_in(x_in, wg_in, stages=()):
    s, d = x_in.shape
    inc = wg_in.shape[2]
    tm, tn, tk = _tile(s, 1024), _tile(inc, 1280), _tile(d, 2048)
    nb = inc // tn
    assert tk == d

    def epi(accs, ex, out, ids):
        out[0][...] = accs[0].astype(BF16)
        out[1][...] = ex[0][...].astype(BF16)

    x_spec = pl.BlockSpec((tm, tk), lambda i, j, k: (i, k))
    return _out(_matmul(
        "fwd_in", (s // tm, N_CHIP * nb, d // tk),
        [(x_in, x_spec, wg_in, pl.BlockSpec((None, tk, tn), lambda i, j, k: (j // nb, k, j % nb)), "nn")],
        [],
        [(_sds((s, N_CHIP * inc), BF16), pl.BlockSpec((tm, tn), lambda i, j, k: (i, j))), (_sds((s, d), BF16), x_spec)],
        (tm, tn), epi, stages, lhs_to_epilogue=True), stages)


def _fwd_merge(y_pool, y_lru, w_pu, w_lu, z, stages=()):
    s, d = y_pool.shape
    tm, tn, tk = _tile(s, 1024), _tile(d, 1024), _tile(d, 1024)
    ga0, gb0 = 3 * d // tn, 4 * d // tn

    def epi(accs, ex, out, ids):
        sa = _sigmoid(ex[0][...].astype(F32))
        sb = _sigmoid(ex[1][...].astype(F32))
        out[0][...] = (sa * accs[0] + sb * accs[1]).astype(BF16)
        out[1][...] = accs[0].astype(BF16)
        out[2][...] = accs[1].astype(BF16)

    a_spec = pl.BlockSpec((tm, tk), lambda i, j, k: (i, k))
    b_spec = pl.BlockSpec((tk, tn), lambda i, j, k: (k, j))
    o_spec = pl.BlockSpec((tm, tn), lambda i, j, k: (i, j))
    return _out(_matmul(
        "fwd_merge", (s // tm, d // tn, d // tk),
        [(y_pool, a_spec, w_pu, b_spec, "nn"), (y_lru, a_spec, w_lu, b_spec, "nn")],
        [(z, pl.BlockSpec((tm, tn), lambda i, j, k: (i, ga0 + j))), (z, pl.BlockSpec((tm, tn), lambda i, j, k: (i, gb0 + j)))],
        [(_sds((s, d), BF16), o_spec)] * 3, (tm, tn), epi, stages), stages)


def _fwd_out_ln1(m, w_out, x, b_out, g1, b1, stages=()):
    s, d = x.shape
    tm, tk = _tile(s, 512), _tile(d, 2048)

    def epi(accs, ex, out, ids):
        r = DN_ALPHA * ex[0][...] + accs[0] + ex[1][...]
        xhat, rstd = _layer_norm(r)
        out[0][...] = xhat
        out[1][...] = (xhat * ex[2][...] + ex[3][...]).astype(BF16)
        out[2][...] = rstd

    full = pl.BlockSpec((tm, d), lambda i, j, k: (i, 0))
    return _out(_matmul(
        "fwd_out_ln1", (s // tm, 1, d // tk),
        [(m, pl.BlockSpec((tm, tk), lambda i, j, k: (i, k)), w_out, pl.BlockSpec((tk, d), lambda i, j, k: (k, 0)), "nn")],
        [(x, full), (b_out, _row(d)), (g1, _row(d)), (b1, _row(d))],
        [(_sds((s, d), F32), full), (_sds((s, d), BF16), full), (_sds((s, 1), F32), pl.BlockSpec((tm, 1), lambda i, j, k: (i, 0)))],
        (tm, d), epi, stages), stages)


def _fwd_ff1(x1_bf, wg_ff1, b_ff1, stages=()):
    s, d = x1_bf.shape
    fc = wg_ff1.shape[2]
    tm, tn, tk = _tile(s, 1024), _tile(fc, 1024), _tile(d, 2048)
    nb = fc // tn

    def epi(accs, ex, out, ids):
        p = jnp.maximum(accs[0] + ex[0][...], 0.0)
        out[0][...] = (p * p).astype(BF16)
        out[1][...] = (2.0 * p).astype(BF16)

    o = (_sds((s, N_CHIP * fc), BF16), pl.BlockSpec((tm, tn), lambda i, j, k: (i, j)))
    return _out(_matmul(
        "fwd_ff1", (s // tm, N_CHIP * nb, d // tk),
        [(x1_bf, pl.BlockSpec((tm, tk), lambda i, j, k: (i, k)),
          wg_ff1, pl.BlockSpec((None, tk, tn), lambda i, j, k: (j // nb, k, j % nb)), "nn")],
        [(b_ff1, pl.BlockSpec((1, tn), lambda i, j, k: (0, j)))], [o, o], (tm, tn), epi, stages), stages)


def _fwd_ff2_ln2_loss(hdn, w_ff2, xhat1, g1, b1, b_ff2, g2, b2, target):
    s, f = hdn.shape
    d = xhat1.shape[1]

    def epi(accs, ex, out, ids):
        first = ids[0] == 0
        x1 = ex[0][...] * ex[1][...] + ex[2][...]
        r = DN_ALPHA * x1 + accs[0] + ex[3][...]
        xhat, rstd = _layer_norm(r)
        g2v = ex[4][...]
        err = xhat * g2v + ex[5][...] - ex[6][...]
        part = 0.5 * jnp.sum(jnp.mean(err * err, axis=-1, keepdims=True), axis=0, keepdims=True)
        dy = err * (1.0 / d)
        dr2 = _layer_norm_bwd(dy * g2v, xhat, rstd)
        out[0][...] = dr2
        out[1][...] = dr2.astype(BF16)
        _accum(out[2], _colsum(dy * xhat), first)
        _accum(out[3], _colsum(dy), first)
        _accum(out[4], _colsum(dr2), first)
        _accum(out[5], jnp.broadcast_to(part, (1, 128)), first)

    ff = _plain_matmul("fwd_ff2", hdn, w_ff2, "nn")
    vec = lambda n: (_sds((1, n), F32), "vec")
    return _rows_call(
        "ln2_loss", s, epi,
        [(ff, "tile"), (xhat1, "tile"), (g1, "vec"), (b1, "vec"), (b_ff2, "vec"), (g2, "vec"), (b2, "vec"), (target, "tile")],
        [(_sds((s, d), F32), "tile"), (_sds((s, d), BF16), "tile"), vec(d), vec(d), vec(d), vec(128)])


def _bwd_ff2_in(dr2_bf, w_ff2, hdn, stages=()):
    s, d = dr2_bf.shape
    f = hdn.shape[1]
    tm, tn, tk = _tile(s, 1024), _tile(f, 1024), _tile(d, 2048)

    def epi(accs, ex, out, ids):
        dpre = accs[0] * ex[0][...].astype(F32)
        out[0][...] = dpre.astype(BF16)
        _accum(out[1], _colsum(dpre), ids[1] == 0)

    return _out(_matmul(
        "bwd_ff2_in", (f // tn, s // tm, d // tk),
        [(dr2_bf, pl.BlockSpec((tm, tk), lambda j, i, k: (i, k)), w_ff2, pl.BlockSpec((tn, tk), lambda j, i, k: (j, k)), "nt")],
        [(hdn, pl.BlockSpec((tm, tn), lambda j, i, k: (i, j)))],
        [(_sds((s, f), BF16), pl.BlockSpec((tm, tn), lambda j, i, k: (i, j))), (_sds((1, f), F32), pl.BlockSpec((1, tn), lambda j, i, k: (0, j)))],
        (tm, tn), epi, stages), stages)


def _bwd_ff1_in_ln1(dpre, wg_ff1, dr2, xhat1, rstd1, g1, stages=()):
    s, f = dpre.shape
    d = xhat1.shape[1]
    fc = wg_ff1.shape[2]
    tm, tn, tk = _tile(s, 1024), _tile(d, 1024), _tile(fc, 2048)
    nb = fc // tk

    def epi(accs, ex, out, ids):
        first = ids[0] == 0
        xhat = ex[1][...]
        dx1 = accs[0] + DN_ALPHA * ex[0][...]
        dr1 = _layer_norm_bwd(dx1 * ex[3][...], xhat, ex[2][...])
        out[0][...] = dr1
        out[1][...] = dr1.astype(BF16)
        _accum(out[2], _colsum(dx1 * xhat), first)
        _accum(out[3], _colsum(dx1), first)
        _accum(out[4], _colsum(dr1), first)

    def plain(accs, ex, out, ids):
        out[0][...] = accs[0]

    o_spec = pl.BlockSpec((tm, tn), lambda i, j, k: (i, j))
    mm = _out(_matmul(
        "bwd_ff1_in", (s // tm, d // tn, f // tk),
        [(dpre, pl.BlockSpec((tm, tk), lambda i, j, k: (i, k)),
          wg_ff1, pl.BlockSpec((None, tn, tk), lambda i, j, k: (k // nb, j, k % nb)), "nt")],
        [], [(_sds((s, d), F32), o_spec)], (tm, tn), plain, stages), stages, True)
    mm, stage_res = mm if stages else (mm, None)
    vec = (_sds((1, d), F32), "vec")
    rows = _rows_call(
        "ln1_bwd", s, epi, [(mm, "tile"), (dr2, "tile"), (xhat1, "tile"), (rstd1, "tile"), (g1, "vec")],
        [(_sds((s, d), F32), "tile"), (_sds((s, d), BF16), "tile"), vec, vec, vec])
    return (rows, stage_res) if stages else rows


def _bwd_out_in(dr1_bf, w_out, z, pa, pb, stages=()):
    s, d = dr1_bf.shape
    tm, tn, tk = _tile(s, 1024), _tile(d, 1024), _tile(d, 2048)
    ga0, gb0 = 3 * d // tn, 4 * d // tn

    def epi(accs, ex, out, ids):
        dm = accs[0]
        sa = _sigmoid(ex[0][...].astype(F32))
        sb = _sigmoid(ex[1][...].astype(F32))
        out[0][...] = (dm * sa).astype(BF16)
        out[1][...] = (dm * sb).astype(BF16)
        out[2][...] = (dm * ex[2][...].astype(F32) * sa * (1.0 - sa)).astype(BF16)
        out[3][...] = (dm * ex[3][...].astype(F32) * sb * (1.0 - sb)).astype(BF16)

    o_spec = pl.BlockSpec((tm, tn), lambda i, j, k: (i, j))
    return _out(_matmul(
        "bwd_out_in", (s // tm, d // tn, d // tk),
        [(dr1_bf, pl.BlockSpec((tm, tk), lambda i, j, k: (i, k)), w_out, pl.BlockSpec((tn, tk), lambda i, j, k: (j, k)), "nt")],
        [(z, pl.BlockSpec((tm, tn), lambda i, j, k: (i, ga0 + j))), (z, pl.BlockSpec((tm, tn), lambda i, j, k: (i, gb0 + j))),
         (pa, o_spec), (pb, o_spec)],
        [(_sds((s, d), BF16), o_spec)] * 4, (tm, tn), epi, stages), stages)


def _bwd_up_in(name, dp, w_up, stages=()):
    s, d = dp.shape
    n = w_up.shape[0]
    tm, tn, tk = _tile(s, 1024), _tile(n, 1024), _tile(d, 2048)

    def epi(accs, ex, out, ids):
        out[0][...] = accs[0].astype(BF16)

    return _out(_matmul(
        name, (s // tm, n // tn, d // tk),
        [(dp, pl.BlockSpec((tm, tk), lambda i, j, k: (i, k)), w_up, pl.BlockSpec((tn, tk), lambda i, j, k: (j, k)), "nt")],
        [], [(_sds((s, n), BF16), pl.BlockSpec((tm, tn), lambda i, j, k: (i, j)))], (tm, tn), epi, stages), stages, True)


def _bwd_in(dz, wg_in, dr1, stages=()):
    s, d = dr1.shape
    inc = wg_in.shape[2]
    tm, tn, tk = _tile(s, 1024), _tile(d, 1024), _tile(inc, 2560)
    nb = inc // tk

    def epi(accs, ex, out, ids):
        out[0][...] = accs[0] + DN_ALPHA * ex[0][...]

    o_spec = pl.BlockSpec((tm, tn), lambda i, j, k: (i, j))
    return _out(_matmul(
        "bwd_in", (s // tm, d // tn, N_CHIP * nb),
        [(dz, pl.BlockSpec((tm, tk), lambda i, j, k: (i, k)),
          wg_in, pl.BlockSpec((None, tn, tk), lambda i, j, k: (k // nb, j, k % nb)), "nt")],
        [(dr1, o_spec)], [(_sds((s, d), F32), o_spec)], (tm, tn), epi, stages), stages, True)


def _wgrad(name, a, b, col_sharded, stages=()):
    s, ka = a.shape
    n = b.shape[1]
    tm, tk = _tile(ka, 1024), _tile(s, 2048)
    tn = _tile(n // N_CHIP, 1280) if col_sharded else _tile(n, 1024)

    def epi(accs, ex, out, ids):
        out[0][...] = accs[0].astype(BF16)

    if col_sharded:
        nb = (n // N_CHIP) // tn
        o = (_sds((N_CHIP, ka, n // N_CHIP), BF16), pl.BlockSpec((None, tm, tn), lambda i, j, k: (j // nb, i, j % nb)))
    else:
        o = (_sds((ka, n), BF16), pl.BlockSpec((tm, tn), lambda i, j, k: (i, j)))
    res = _out(_matmul(
        name, (ka // tm, n // tn, s // tk),
        [(a, pl.BlockSpec((tk, tm), lambda i, j, k: (k, i)), b, pl.BlockSpec((tk, tn), lambda i, j, k: (k, j)), "tn")],
        [], [o], (tm, tn), epi, stages), stages, True)
    res, stage_res = res if stages else (res, None)
    res = res if col_sharded else res.reshape(N_CHIP, ka // N_CHIP, n)
    return (res, stage_res) if stages else res


def _wgrad_rows_half(name, a, b, half, stages):
    s, ka = a.shape
    n = b.shape[1]
    kh = ka // 2
    tm, tk, tn = _tile(kh, 1024), _tile(s, 2048), _tile(n // N_CHIP, 1280)
    nb, ni = (n // N_CHIP) // tn, kh // tm

    def epi(accs, ex, out, ids):
        out[0][...] = accs[0].astype(BF16)

    return _out(_matmul(
        name, (ni, n // tn, s // tk),
        [(a, pl.BlockSpec((tk, tm), lambda i, j, k, sel: (k, sel[0] * ni + i)),
          b, pl.BlockSpec((tk, tn), lambda i, j, k, sel: (k, j)), "tn")],
        [], [(_sds((N_CHIP, kh, n // N_CHIP), BF16), pl.BlockSpec((None, tm, tn), lambda i, j, k, sel: (j // nb, i, j % nb)))],
        (tm, tn), epi, stages, _scalar(half)), stages, True)


def _chunk(s):
    return _tile(s, 512, SUBLANES)


def _zero_pads(ref, s):
    zeros = jnp.zeros((PAD, ref.shape[1]), F32)
    ref[pl.ds(0, PAD), :] = zeros
    ref[pl.ds(PAD + s, PAD), :] = zeros


def _window(ref, t0, t):
    return ref[pl.ds(t0, t + 2 * PAD), :]


def _shift(sup, off, t):
    return sup[PAD + off:PAD + off + t, :]


def _pool_count(t0, t, s, w):
    pos = t0 + lax.broadcasted_iota(jnp.int32, (t, 1), 0)
    return (jnp.minimum(pos + w // 2, s) - jnp.maximum(pos - w // 2, 0)).astype(F32)


def _pool_fwd(z, pool_w, pool_scale, stages=()):
    s = z.shape[0]
    n_g, pg = pool_w.shape[0], pool_w.shape[1]
    assert n_g == len(POOL_WINDOWS) and max(POOL_WINDOWS) // 2 <= PAD
    t = _chunk(s)

    def body(u_ref, w_ref, sc_ref, d_ref, y_ref, pad_ref):
        g = pl.program_id(0)
        _zero_pads(pad_ref, s)
        pad_ref[pl.ds(PAD, s), :] = u_ref[...].astype(F32)
        for gi, w in enumerate(POOL_WINDOWS):
            @pl.when(g == gi)
            def _():
                def step(ch, carry):
                    t0 = pl.multiple_of(ch * t, t)
                    sup = _window(pad_ref, t0, t)
                    acc = _shift(sup, -(w // 2), t)
                    for o in range(-(w // 2) + 1, w // 2):
                        acc = acc + _shift(sup, o, t)
                    dd = (acc * (1.0 / _pool_count(t0, t, s, w)) - _shift(sup, 0, t)).astype(BF16)
                    d_ref[pl.ds(t0, t), :] = dd
                    y = jnp.dot(dd, w_ref[...], preferred_element_type=F32) * sc_ref[...]
                    y_ref[pl.ds(t0, t), :] = y.astype(BF16)
                    return carry

                lax.fori_loop(0, s // t, step, 0)

    blk = pl.BlockSpec((s, pg), lambda g: (0, g))
    res = _host_call(
        "pool_fwd", (n_g,), lambda ins, outs, scr: body(*ins, *outs, *scr), [z, pool_w, pool_scale],
        [blk, pl.BlockSpec((None, pg, pg), lambda g: (g, 0, 0)), pl.BlockSpec((1, pg), lambda g: (0, g))],
        [_sds((s, n_g * pg), BF16)] * 2, [blk, blk], [pltpu.VMEM((s + 2 * PAD, pg), F32)], list(stages))
    return res if stages else res[0]


def _pool_bwd(dsv, dy, pool_w, pool_scale, stages=()):
    s = dsv.shape[0]
    n_g, pg = pool_w.shape[0], pool_w.shape[1]
    t = _chunk(s)

    def body(d_ref, dy_ref, w_ref, sc_ref, du_ref, dw_ref, dsc_ref, epad_ref, dwacc_ref):
        g = pl.program_id(0)
        _zero_pads(epad_ref, s)
        dwacc_ref[...] = jnp.zeros_like(dwacc_ref)
        for gi, w in enumerate(POOL_WINDOWS):
            @pl.when(g == gi)
            def _():
                def first(ch, dsc):
                    t0 = pl.multiple_of(ch * t, t)
                    dd = d_ref[pl.ds(t0, t), :]
                    dyc = dy_ref[pl.ds(t0, t), :].astype(F32)
                    wv = w_ref[...]
                    ypre = jnp.dot(dd, wv, preferred_element_type=F32)
                    dq = (dyc * sc_ref[...]).astype(BF16)
                    dwacc_ref[...] += lax.dot_general(dd, dq, _DIMS["tn"], preferred_element_type=F32)
                    ddv = lax.dot_general(dq, wv, _DIMS["nt"], preferred_element_type=F32)
                    epad_ref[pl.ds(pl.multiple_of(PAD + t0, SUBLANES), t), :] = ddv * (1.0 / _pool_count(t0, t, s, w))
                    return dsc + _colsum(dyc * ypre)

                dsc_ref[...] = lax.fori_loop(0, s // t, first, jnp.zeros((1, pg), F32))

                def second(ch, carry):
                    t0 = pl.multiple_of(ch * t, t)
                    sup = _window(epad_ref, t0, t)
                    acc = _shift(sup, -(w // 2) + 1, t)
                    for o in range(-(w // 2) + 2, w // 2 + 1):
                        acc = acc + _shift(sup, o, t)
                    du_ref[pl.ds(t0, t), :] = (acc - _shift(sup, 0, t) * _pool_count(t0, t, s, w)).astype(BF16)
                    return carry

                lax.fori_loop(0, s // t, second, 0)

        dw_ref[...] = dwacc_ref[...].astype(BF16)

    blk = pl.BlockSpec((s, pg), lambda g: (0, g))
    w_spec = pl.BlockSpec((None, pg, pg), lambda g: (g, 0, 0))
    sc_spec = pl.BlockSpec((1, pg), lambda g: (0, g))
    res = _host_call(
        "pool_bwd", (n_g,), lambda ins, outs, scr: body(*ins, *outs, *scr), [dsv, dy, pool_w, pool_scale],
        [blk, blk, w_spec, sc_spec], [_sds((s, n_g * pg), BF16), _sds((n_g, pg, pg), BF16), _sds((1, n_g * pg), F32)],
        [blk, w_spec, sc_spec], [pltpu.VMEM((s + 2 * PAD, pg), F32), pltpu.VMEM((pg, pg), F32)], list(stages))
    return res if stages else res[0]


def _sigmoid(x):
    return 0.5 * jnp.tanh(0.5 * x) + 0.5


def _softplus(x):
    e = jnp.exp(-jnp.abs(x))
    log1p_e = jnp.where(e < 1e-2, e * (1.0 - e * (0.5 - e * (1.0 / 3.0))), jnp.log(1.0 + e))
    return jnp.maximum(x, 0.0) + log1p_e


_GELU_C = math.sqrt(2.0 / math.pi)


def _gelu(x):
    th = jnp.tanh(_GELU_C * (x + 0.044715 * x * x * x))
    return 0.5 * x * (1.0 + th), th


def _gelu_grad(x, th):
    return 0.5 * (1.0 + th) + 0.5 * x * (1.0 - th * th) * _GELU_C * (1.0 + 3.0 * 0.044715 * x * x)


def _scan_chunk(a_ref, b_ref, o_ref, o_off, carry, t, reverse):
    n = a_ref.shape[1]
    row = lax.broadcasted_iota(jnp.int32, (SUBLANES, n), 0)
    n_groups = t // SUBLANES
    unroll = math.gcd(n_groups, SCAN_UNROLL)
    last = 0 if reverse else SUBLANES - 1

    def step(si, carry):
        for u in range(unroll):
            gi = si * unroll + u
            g = n_groups - 1 - gi if reverse else gi
            r0 = pl.multiple_of(g * SUBLANES, SUBLANES)
            a = a_ref[pl.ds(r0, SUBLANES), :]
            b = b_ref[pl.ds(r0, SUBLANES), :]
            for k in (1, 2, 4):
                keep = row < SUBLANES - k if reverse else row >= k
                sh = SUBLANES - k if reverse else k
                ar = jnp.where(keep, pltpu.roll(a, sh, 0), 1.0)
                br = jnp.where(keep, pltpu.roll(b, sh, 0), 0.0)
                b = a * br + b
                a = a * ar
            o_ref[pl.ds(pl.multiple_of(o_off + r0, SUBLANES), SUBLANES), :] = a * carry + b
            carry = (jnp.broadcast_to(a[last:last + 1, :], a.shape) * carry
                     + jnp.broadcast_to(b[last:last + 1, :], b.shape))
        return carry

    return lax.fori_loop(0, n_groups // unroll, step, carry)


def _lru_params(pk_ref):
    rows = pk_ref[...]
    get = lambda i: rows[i:i + 1, :]
    cw = [get(k) for k in range(4)]
    lam = (get(9), get(10))
    big_l = tuple(-LRU_C * _softplus(-v) for v in lam)
    return cw, get(4), (get(5), get(6)), (get(7), get(8)), lam, big_l


def _conv(sup, cw, cb, t):
    xc = cb + cw[0] * _shift(sup, -2, t)
    for k in range(1, 4):
        xc = xc + cw[k] * _shift(sup, k - 2, t)
    return xc


def _gates(xcb, w_ref, d, bk, ba, bx, big_l):
    pre = jnp.dot(xcb, w_ref[:, pl.ds(d * 2 * bk, 2 * bk)], preferred_element_type=F32)
    r = _sigmoid(pre[:, :bk] + ba[d])
    i = _sigmoid(pre[:, bk:] + bx[d])
    la = big_l[d] * r
    a = jnp.exp(la)
    var = jnp.tanh(-la) * (1.0 + a * a)
    rs = lax.rsqrt(jnp.maximum(var, 1e-30))
    return r, i, a, var * rs, rs


def _lru_specs(s, d, bk):
    u_spec = pl.BlockSpec((s, bk), lambda h: (0, d // bk + h))
    ug_spec = pl.BlockSpec((s, bk), lambda h: (0, 2 * d // bk + h))
    w_spec = pl.BlockSpec((None, bk, 4 * bk), lambda h: (h, 0, 0))
    pk_spec = pl.BlockSpec((None, 16, bk), lambda h: (h, 0, 0))
    blk = pl.BlockSpec((s, bk), lambda h: (0, h))
    return u_spec, ug_spec, w_spec, pk_spec, blk


def _lru_fwd(z, gatew, pk, stages=()):
    s = z.shape[0]
    n_h, bk = gatew.shape[0], gatew.shape[1]
    d = n_h * bk
    t = _chunk(s)
    n_ch = s // t

    def body(u_ref, ug_ref, w_ref, pk_ref, y_ref, upad, h0buf, abuf, bbuf, xcbuf, h1buf):
        _zero_pads(upad, s)
        upad[pl.ds(PAD, s), :] = u_ref[...].astype(F32)
        cw, cb, ba, bx, _, big_l = _lru_params(pk_ref)
        zero = jnp.zeros((SUBLANES, bk), F32)

        def fill(xc, dr):
            _, i, a, sq, _ = _gates(xc.astype(BF16), w_ref, dr, bk, ba, bx, big_l)
            abuf[...] = a
            bbuf[...] = sq * i * xc

        def up(ch, carry):
            t0 = pl.multiple_of(ch * t, t)
            xc = _conv(_window(upad, t0, t), cw, cb, t)
            xcbuf[pl.ds(t0, t), :] = xc
            fill(xc, 0)
            return _scan_chunk(abuf, bbuf, h0buf, t0, carry, t, False)

        lax.fori_loop(0, n_ch, up, zero)

        def down(ci, carry):
            t0 = pl.multiple_of((n_ch - 1 - ci) * t, t)
            fill(xcbuf[pl.ds(t0, t), :], 1)
            carry = _scan_chunk(abuf, bbuf, h1buf, 0, carry, t, True)
            gl, _ = _gelu(ug_ref[pl.ds(t0, t), :].astype(F32))
            y_ref[pl.ds(t0, t), :] = ((h0buf[pl.ds(t0, t), :] + h1buf[...]) * gl).astype(BF16)
            return carry

        lax.fori_loop(0, n_ch, down, zero)

    u_spec, ug_spec, w_spec, pk_spec, blk = _lru_specs(s, d, bk)
    res = _host_call(
        "lru_fwd", (n_h,), lambda ins, outs, scr: body(*ins, *outs, *scr), [z, z, gatew, pk],
        [u_spec, ug_spec, w_spec, pk_spec], [_sds((s, d), BF16)], [blk],
        [pltpu.VMEM((s + 2 * PAD, bk), F32), pltpu.VMEM((s, bk), F32), pltpu.VMEM((t, bk), F32), pltpu.VMEM((t, bk), F32),
         pltpu.VMEM((s, bk), F32), pltpu.VMEM((t, bk), F32)], list(stages))
    return (res[0][0], res[1]) if stages else res[0][0]


def _lru_grads(lam_, hnb, a, sq, rs, r, i, xc, xcb, w_ref, dwacc, d, big_l, acc):
    bk = xc.shape[1]
    dba, dbx, dl = acc
    q = lam_ * i * xc
    dla = lam_ * hnb * a - q * (a * a) * rs
    dpr = dla * big_l * r * (1.0 - r)
    dpi = q * sq * (1.0 - i)
    dprb, dpib = dpr.astype(BF16), dpi.astype(BF16)
    c0 = d * 2 * bk
    dxc = (lam_ * sq * i
           + lax.dot_general(dprb, w_ref[:, pl.ds(c0, bk)], _DIMS["nt"], preferred_element_type=F32)
           + lax.dot_general(dpib, w_ref[:, pl.ds(c0 + bk, bk)], _DIMS["nt"], preferred_element_type=F32))
    dwacc[:, pl.ds(c0, bk)] += lax.dot_general(xcb, dprb, _DIMS["tn"], preferred_element_type=F32)
    dwacc[:, pl.ds(c0 + bk, bk)] += lax.dot_general(xcb, dpib, _DIMS["tn"], preferred_element_type=F32)
    return dxc, (dba + _colsum(dpr), dbx + _colsum(dpi), dl + _colsum(dla * r))


def _lru_bwd(z, dy, gatew, pk, stages=()):
    s = z.shape[0]
    n_h, bk = gatew.shape[0], gatew.shape[1]
    d = n_h * bk
    t = _chunk(s)
    n_ch = s // t

    def body(u_ref, ug_ref, dy_ref, w_ref, pk_ref, du_ref, dug_ref, dw_ref, dpk_ref,
             upad, h0pad, h1pad, dxpad, abuf, bbuf, lbuf, dwacc, edge, xcbuf):
        for ref in (upad, h0pad, h1pad, dxpad):
            _zero_pads(ref, s)
        upad[pl.ds(PAD, s), :] = u_ref[...].astype(F32)
        dwacc[...] = jnp.zeros_like(dwacc)
        cw, cb, ba, bx, lam, big_l = _lru_params(pk_ref)
        zero = jnp.zeros((SUBLANES, bk), F32)
        zrow = jnp.zeros((1, bk), F32)
        rowi = lax.broadcasted_iota(jnp.int32, (t, bk), 0)

        def at(t0):
            return pl.ds(pl.multiple_of(PAD + t0, SUBLANES), t)

        def conv_in(t0):
            xc = xcbuf[pl.ds(t0, t), :]
            return xc, xc.astype(BF16)

        def dh_of(t0):
            ug = ug_ref[pl.ds(t0, t), :].astype(F32)
            gl, th = _gelu(ug)
            dyv = dy_ref[pl.ds(t0, t), :].astype(F32)
            return dyv * gl, dyv * _gelu_grad(ug, th)

        def sweep1(ch, carry):
            t0 = pl.multiple_of(ch * t, t)
            xc = _conv(_window(upad, t0, t), cw, cb, t)
            xcbuf[pl.ds(t0, t), :] = xc
            _, i, a, sq, _ = _gates(xc.astype(BF16), w_ref, 0, bk, ba, bx, big_l)
            abuf[...] = a
            bbuf[...] = sq * i * xc
            return _scan_chunk(abuf, bbuf, h0pad, PAD + t0, carry, t, False)

        lax.fori_loop(0, n_ch, sweep1, zero)

        edge[...] = zero

        def sweep2(ci, st):
            carry_h, carry_l, acc = st
            t0 = pl.multiple_of((n_ch - 1 - ci) * t, t)
            xc, xcb = conv_in(t0)
            _, i1, a1, sq1, _ = _gates(xcb, w_ref, 1, bk, ba, bx, big_l)
            abuf[...] = a1
            bbuf[...] = sq1 * i1 * xc
            carry_h = _scan_chunk(abuf, bbuf, h1pad, PAD + t0, carry_h, t, True)
            dh, dgl = dh_of(t0)
            dug_ref[pl.ds(t0, t), :] = (dgl * (h0pad[at(t0), :] + h1pad[at(t0), :])).astype(BF16)
            r0, i0, a0, sq0, rs0 = _gates(xcb, w_ref, 0, bk, ba, bx, big_l)
            abuf[...] = jnp.where(rowi == t - 1, edge[0:1, :], pltpu.roll(a0, t - 1, 0))
            bbuf[...] = dh
            carry_l = _scan_chunk(abuf, bbuf, lbuf, 0, carry_l, t, True)
            edge[...] = jnp.broadcast_to(a0[0:1, :], (SUBLANES, bk))
            hprev = _shift(_window(h0pad, t0, t), -1, t)
            dxc, acc = _lru_grads(lbuf[...], hprev, a0, sq0, rs0, r0, i0, xc, xcb, w_ref, dwacc, 0, big_l[0], acc)
            dxpad[at(t0), :] = dxc
            return carry_h, carry_l, acc

        _, _, acc0 = lax.fori_loop(0, n_ch, sweep2, (zero, zero, (zrow, zrow, zrow)))

        edge[...] = zero

        def sweep3(ch, st):
            carry_l, acc = st
            t0 = pl.multiple_of(ch * t, t)
            xc, xcb = conv_in(t0)
            r1, i1, a1, sq1, rs1 = _gates(xcb, w_ref, 1, bk, ba, bx, big_l)
            dh, _ = dh_of(t0)
            abuf[...] = jnp.where(rowi == 0, edge[0:1, :], pltpu.roll(a1, 1, 0))
            bbuf[...] = dh
            carry_l = _scan_chunk(abuf, bbuf, lbuf, 0, carry_l, t, False)
            edge[...] = jnp.broadcast_to(a1[t - 1:t, :], (SUBLANES, bk))
            hnext = _shift(_window(h1pad, t0, t), 1, t)
            dxc, acc = _lru_grads(lbuf[...], hnext, a1, sq1, rs1, r1, i1, xc, xcb, w_ref, dwacc, 1, big_l[1], acc)
            dxpad[at(t0), :] += dxc
            return carry_l, acc

        _, acc1 = lax.fori_loop(0, n_ch, sweep3, (zero, (zrow, zrow, zrow)))

        def sweep4(ch, st):
            t0 = pl.multiple_of(ch * t, t)
            sdx = _window(dxpad, t0, t)
            su = _window(upad, t0, t)
            dxc = _shift(sdx, 0, t)
            du = cw[0] * _shift(sdx, 2, t) + cw[1] * _shift(sdx, 1, t) + cw[2] * dxc + cw[3] * _shift(sdx, -1, t)
            du_ref[pl.ds(t0, t), :] = du.astype(BF16)
            return tuple(st[k] + _colsum(dxc * _shift(su, k - 2, t)) for k in range(4)) + (st[4] + _colsum(dxc),)

        conv_g = lax.fori_loop(0, n_ch, sweep4, (zrow,) * 5)

        dpk_ref[...] = jnp.zeros_like(dpk_ref)
        rows = list(conv_g) + [acc0[0], acc1[0], acc0[1], acc1[1],
                               acc0[2] * LRU_C * _sigmoid(-lam[0]), acc1[2] * LRU_C * _sigmoid(-lam[1])]
        for k, v in enumerate(rows):
            dpk_ref[pl.ds(k, 1), :] = v
        dw_ref[...] = dwacc[...].astype(BF16)

    u_spec, ug_spec, w_spec, pk_spec, blk = _lru_specs(s, d, bk)
    padded = pltpu.VMEM((s + 2 * PAD, bk), F32)
    chunk = pltpu.VMEM((t, bk), F32)
    res = _host_call(
        "lru_bwd", (n_h,), lambda ins, outs, scr: body(*ins, *outs, *scr), [z, z, dy, gatew, pk],
        [u_spec, ug_spec, blk, w_spec, pk_spec],
        [_sds((s, d), BF16), _sds((s, d), BF16), _sds((n_h, bk, 4 * bk), BF16), _sds((n_h, 16, bk), F32)],
        [blk, blk, w_spec, pk_spec],
        [padded, padded, padded, padded, chunk, chunk, chunk, pltpu.VMEM((bk, 4 * bk), F32),
         pltpu.VMEM((SUBLANES, bk), F32), pltpu.VMEM((s, bk), F32)], list(stages))
    return res if stages else res[0]


def _scalar(v):
    return jnp.reshape(v, (1,)).astype(jnp.int32)


def _add_sibling(g, r, c):
    _, rows, cols = g.shape
    rh = rows // 2
    tr = _tile(rh, 512, 16)
    nr = rh // tr

    def body(c_ref, g_ref, r_ref, o_ref):
        o_ref[...] = (g_ref[...].astype(F32) + r_ref[...].astype(F32)).astype(BF16)

    spec = pl.BlockSpec((None, tr, cols), lambda k, i, c_ref: (k, i, 0))
    return pl.pallas_call(
        body, name="add_sibling", out_shape=_sds((N_CHIP, rh, cols), BF16),
        grid_spec=pltpu.PrefetchScalarGridSpec(
            num_scalar_prefetch=1, grid=(N_CHIP, nr),
            in_specs=[pl.BlockSpec((None, tr, cols), lambda k, i, c_ref: (k, c_ref[0] * nr + i, 0)), spec], out_specs=spec),
        compiler_params=_cparams(("arbitrary", "arbitrary")),
    )(_scalar(c), g, r)


def _add_pair(g, r):
    _, rh, cols = g.shape
    tr = _tile(rh, 512, 16)

    def body(ins, outs, scr):
        outs[0][...] = (ins[0][...].astype(F32) + ins[1][...].astype(F32)).astype(BF16)

    spec = pl.BlockSpec((None, tr, cols), lambda k, i: (k, i, 0))
    return _host_call("add_pair", (N_CHIP, rh // tr), body, [g, r], [spec, spec], [_sds(g.shape, BF16)], [spec], [], [])[0][0]


def _sum_chips(p, rcv, k_me, c):
    _, rh, cols = p.shape
    tr = _tile(rh, 512, 16)
    nr = rh // tr

    def body(kc_ref, p_ref, r_ref, o_ref):
        acc = p_ref[...].astype(F32)
        for j in range(3):
            acc = acc + r_ref[j].astype(F32)
        o_ref[...] = acc

    return pl.pallas_call(
        body, name="sum_chips", out_shape=_sds((2 * rh, cols), F32),
        grid_spec=pltpu.PrefetchScalarGridSpec(
            num_scalar_prefetch=1, grid=(nr,),
            in_specs=[pl.BlockSpec((None, tr, cols), lambda i, kc_ref: (kc_ref[0], i, 0)),
                      pl.BlockSpec((3, tr, cols), lambda i, kc_ref: (0, i, 0))],
            out_specs=pl.BlockSpec((tr, cols), lambda i, kc_ref: (kc_ref[1] * nr + i, 0))),
        compiler_params=_cparams(("arbitrary",)),
    )(jnp.stack([k_me, c]).astype(jnp.int32), p, rcv)


def _sum_devices(g):
    def body(g_ref, o_ref):
        acc = g_ref[0]
        for dev in range(1, N_DEV):
            acc = acc + g_ref[dev]
        o_ref[...] = acc

    return pl.pallas_call(body, name="sum_devices", out_shape=_sds(g.shape[1:], F32))(g)


def _adamw(w, g, m, v):
    rows, cols = w.shape
    tr = _tile(rows, 256, SUBLANES)

    def body(ins, outs, scr):
        w_ref, g_ref, m_ref, v_ref = ins
        go_ref, d_ref, nm_ref, nv_ref = outs
        gv = g_ref[...]
        go_ref[...] = gv
        nm = ADAM_B1 * m_ref[...] + (1.0 - ADAM_B1) * gv
        nv = ADAM_B2 * v_ref[...] + (1.0 - ADAM_B2) * (gv * gv)
        m_hat = nm / (1.0 - ADAM_B1 ** ADAM_STEP)
        v_hat = nv / (1.0 - ADAM_B2 ** ADAM_STEP)
        d_ref[...] = -ADAM_LR * (m_hat / (jnp.sqrt(v_hat) + ADAM_EPS) + ADAM_WD * w_ref[...])
        nm_ref[...] = nm
        nv_ref[...] = nv

    spec = pl.BlockSpec((tr, cols), lambda i: (i, 0))
    return _host_call("adamw", (rows // tr,), body, [w, g, m, v], [spec] * 4, [_sds((rows, cols), F32)] * 4, [spec] * 4, [], [])[0]


def _pack(vs, unit):
    flat = jnp.concatenate([v.reshape(-1).astype(F32) for v in vs])
    pad = (-flat.shape[0]) % unit
    if pad:
        flat = jnp.concatenate([flat, jnp.zeros((pad,), F32)])
    return flat.reshape(-1, 128)


def _unpack(p, like):
    flat = p.reshape(-1)
    out, off = [], 0
    for v in like:
        n = math.prod(v.shape)
        out.append(flat[off:off + n].reshape(v.shape))
        off += n
    return out


def kernel(x, w_in, pool_w, pool_scale, conv_w, conv_b, lru_wa, lru_ba, lru_wx, lru_bx, lru_lambda, w_pool_up, w_lru_up, w_out, b_out, ln1_g, ln1_b, w_ff1, b_ff1, w_ff2, b_ff2, ln2_g, ln2_b, loss_target, m_w_in, m_pool_w, m_pool_scale, m_conv_w, m_conv_b, m_lru_wa, m_lru_ba, m_lru_wx, m_lru_bx, m_lru_lambda, m_w_pool_up, m_w_lru_up, m_w_out, m_b_out, m_ln1_g, m_ln1_b, m_w_ff1, m_b_ff1, m_w_ff2, m_b_ff2, m_ln2_g, m_ln2_b, v_w_in, v_pool_w, v_pool_scale, v_conv_w, v_conv_b, v_lru_wa, v_lru_ba, v_lru_wx, v_lru_bx, v_lru_lambda, v_w_pool_up, v_w_lru_up, v_w_out, v_b_out, v_ln1_g, v_ln1_b, v_w_ff1, v_b_ff1, v_w_ff2, v_b_ff2, v_ln2_g, v_ln2_b):
    given = dict(locals())
    wt = {n: given[n] for n in WEIGHTS}
    mom = {n: given["m_" + n] for n in WEIGHTS}
    vel = {n: given["v_" + n] for n in WEIGHTS}

    ix, iy, ic = _mesh_pos()
    k_me = 2 * ix + iy
    s, d = x.shape[1], x.shape[2]
    ds = d // N_CHIP
    n_g, pgs, pg = pool_w.shape[1], pool_w.shape[2], pool_w.shape[3]
    n_h, bks, bk = lru_wa.shape[2], lru_wa.shape[3], lru_wa.shape[4]
    f = b_ff1.shape[1]
    x2 = x[0]
    vec = lambda a: a.reshape(1, -1)

    sharded_vecs = [conv_w[0], lru_ba[0], lru_bx[0], lru_lambda[0]]
    rows_sv = jnp.concatenate(sharded_vecs + [jnp.zeros((6, ds), F32)], axis=0)
    sv = _all_gather_small(rows_sv)
    sv = sv.reshape(N_CHIP, 2, 16, ds)[:, 0].transpose(1, 0, 2).reshape(16, d)
    conv_w_f, ba_f, bx_f, lam_f = sv[0:4], sv[4:6], sv[6:8], sv[8:10]
    pk = jnp.concatenate([conv_w_f, conv_b, ba_f, bx_f, lam_f, jnp.zeros((5, d), F32)], axis=0)
    pk = pk.reshape(16, n_h, bk).transpose(1, 0, 2)

    def gate_stack(wa, wx):
        return jnp.stack([wa[0], wx[0]], axis=1)

    mats = {
        "w_in": w_in[0], "w_pool_up": w_pool_up[0], "w_lru_up": w_lru_up[0], "w_out": w_out[0],
        "w_ff1": w_ff1[0], "w_ff2": w_ff2[0],
        "pool_w": pool_w[0].reshape(n_g * pgs, pg),
        "gate_w": gate_stack(lru_wa, lru_wx).reshape(4 * n_h * bks, bk),
    }
    names = list(mats)
    placed = {n: _cast_place(mats[n], k_me) for n in names}

    def add_sibling(gs, swapped):
        return [_add_sibling(g, r, ic) for g, r in zip(gs, swapped)]

    def sum_chips(ps, received):
        return [_sum_chips(p, r, k_me, ic) for p, r in zip(ps, received)]

    first = [placed[n] for n in ("w_in", "pool_w", "gate_w")]
    (bufs,) = _run_stages("gather_first", [_chain([
        _gather_direct(first), _together([_gather_relay(first), _gather_d2d(first, (0, 1))]), _gather_d2d(first, (2,))])])
    wg_in = bufs[0]
    wf_pool = bufs[1].reshape(N_CHIP, n_g, pgs, pg).transpose(1, 0, 2, 3).reshape(n_g, pg, pg)
    wf_gate = bufs[2].reshape(N_CHIP, 2, 2, n_h, bks, bk).transpose(3, 0, 4, 1, 2, 5).reshape(n_h, bk, 4 * bk)

    (z, x_bf), (wb_mix, wb_ff1) = _fwd_in(x2, wg_in, stages=[
        _gather_direct([placed[n] for n in ("w_pool_up", "w_lru_up", "w_out")]), _gather_direct([placed["w_ff1"]])])
    (d_pool, y_pool), (wb_mix,) = _pool_fwd(z, wf_pool, pool_scale, stages=[_gather_relay(wb_mix)])
    y_lru, (wb_ff1, wb_mix, wb_ff2) = _lru_fwd(z, wf_gate, pk, stages=[
        _gather_relay(wb_ff1), _gather_d2d(wb_mix), _gather_direct([placed["w_ff2"]])])
    wf_pu, wf_lu, wf_out = (b.reshape(d, d) for b in wb_mix)
    (m_mix, p_a, p_b), (wb_ff1, wb_ff2) = _fwd_merge(y_pool, y_lru, wf_pu, wf_lu, z, stages=[
        _gather_d2d(wb_ff1), _gather_relay(wb_ff2)])
    wg_ff1 = wb_ff1[0]
    (xhat1, x1_bf, rstd1), (wb_ff2,) = _fwd_out_ln1(m_mix, wf_out, x2, b_out, ln1_g, ln1_b, stages=[_gather_d2d(wb_ff2)])
    hdn, d_hdn = _fwd_ff1(x1_bf, wg_ff1, b_ff1)
    wf_ff2 = wb_ff2[0].reshape(f, d)
    dr2, dr2_bf, g_ln2_g, g_ln2_b, g_b_ff2, loss_part = _fwd_ff2_ln2_loss(
        hdn, wf_ff2, xhat1, ln1_g, ln1_b, b_ff2, ln2_g, ln2_b, loss_target[0])

    dpre, g_b_ff1 = _bwd_ff2_in(dr2_bf, wf_ff2, d_hdn)
    g_ff = [_wgrad("wgrad_ff1", x1_bf, dpre, True), _wgrad("wgrad_ff2", hdn, dr2_bf, False)]
    (dr1, dr1_bf, g_ln1_g, g_ln1_b, g_b_out), (swapped,) = _bwd_ff1_in_ln1(
        dpre, wg_ff1, dr2, xhat1, rstd1, ln1_g, stages=[_swap_halves(g_ff)])
    sums_ff = add_sibling(g_ff, swapped)
    dp_a, dp_b, dg_a, dg_b = _bwd_out_in(dr1_bf, wf_out, z, p_a, p_b)
    dy_pool = _bwd_up_in("bwd_pool_up_in", dp_a, wf_pu)
    dy_lru = _bwd_up_in("bwd_lru_up_in", dp_b, wf_lu)
    g_mix = [_wgrad("wgrad_pool_up", y_pool, dp_a, False), _wgrad("wgrad_lru_up", y_lru, dp_b, False),
             _wgrad("wgrad_out", m_mix, dr1_bf, False)]
    (du_pool, g_pool_w, g_pool_scale), (swapped,) = _pool_bwd(
        d_pool, dy_pool, wf_pool, pool_scale, stages=[_swap_halves(g_mix)])
    sums_mix = add_sibling(g_mix, swapped)
    (du_lru, du_gate, g_gate_w, g_pk), (recv_ff, recv_mix) = _lru_bwd(
        z, dy_lru, wf_gate, pk, stages=[_scatter_chips(sums_ff), _scatter_chips(sums_mix)])
    halves = sum_chips(sums_ff + sums_mix, recv_ff + recv_mix)
    g_small = [g_pool_w.reshape(n_g, N_CHIP, pgs, pg).transpose(1, 0, 2, 3).reshape(N_CHIP, n_g * pgs, pg),
               g_gate_w.reshape(n_h, N_CHIP, bks, 2, 2, bk).transpose(1, 3, 4, 0, 2, 5).reshape(N_CHIP, 4 * n_h * bks, bk)]
    dz = jnp.concatenate([du_pool, du_lru, du_gate, dg_a, dg_b], axis=1)
    g_other, (joined, swapped) = _wgrad_rows_half(
        "wgrad_in_other", x_bf, dz, 1 - ic, [_join_halves(halves), _swap_halves(g_small)])
    g_mat = dict(zip(["w_ff1", "w_ff2", "w_pool_up", "w_lru_up", "w_out"], joined))
    sums_small = add_sibling(g_small, swapped)
    g_mine, (from_sibling,) = _wgrad_rows_half("wgrad_in_mine", x_bf, dz, ic, [_send_to_sibling([g_other])])
    sums_in = [_add_pair(g_mine, from_sibling[0])]

    def stacked(tree):
        return gate_stack(tree["lru_wa"], tree["lru_wx"]).reshape(4 * n_h * bks, bk)

    res = {}

    def update(n):
        if n == "gate_w":
            outs = [o.reshape(2, 2, n_h, bks, bk) for o in _adamw(stacked(wt), g_mat[n], stacked(mom), stacked(vel))]
            res["lru_wa"] = [o[:, 0][None] for o in outs]
            res["lru_wx"] = [o[:, 1][None] for o in outs]
        else:
            shp2 = mats[n].shape
            outs = _adamw(wt[n].reshape(shp2), g_mat[n], mom[n].reshape(shp2), vel[n].reshape(shp2))
            res[n] = [o.reshape(wt[n].shape) for o in outs]

    grad_x, (recv_small, recv_in) = _bwd_in(dz, wg_in, dr1, stages=[_scatter_chips(sums_small), _scatter_chips(sums_in)])
    halves = sum_chips(sums_small + sums_in, recv_small + recv_in)
    (joined,) = _run_stages("join_last", [_join_halves(halves)])
    g_mat.update(zip(["pool_w", "gate_w", "w_in"], joined))
    for n in names:
        update(n)

    g_pk = g_pk.transpose(1, 0, 2).reshape(16, d)
    vec_full = {
        "pool_scale": g_pool_scale, "conv_w": g_pk[0:4], "conv_b": g_pk[4:5],
        "lru_ba": g_pk[5:7], "lru_bx": g_pk[7:9], "lru_lambda": g_pk[9:11],
        "b_out": g_b_out, "ln1_g": g_ln1_g, "ln1_b": g_ln1_b, "b_ff1": g_b_ff1, "b_ff2": g_b_ff2,
        "ln2_g": g_ln2_g, "ln2_b": g_ln2_b,
    }
    vnames = list(vec_full)
    vg = _sum_devices(_all_gather_small(_pack([vec_full[n] for n in vnames], 1024)))
    vg = dict(zip(vnames, _unpack(vg, [vec_full[n] for n in vnames])))
    for n in ("conv_w", "lru_ba", "lru_bx", "lru_lambda"):
        vg[n] = lax.dynamic_slice_in_dim(vg[n], k_me * ds, ds, axis=1)
    vg = {n: vg[n].reshape(wt[n].shape) for n in vnames}
    upd = _adamw(_pack([wt[n] for n in vnames], 1024), _pack([vg[n] for n in vnames], 1024),
                 _pack([mom[n] for n in vnames], 1024), _pack([vel[n] for n in vnames], 1024))
    upd = [_unpack(u, [wt[n] for n in vnames]) for u in upd]
    for i, n in enumerate(vnames):
        res[n] = [vg[n], upd[1][i], upd[2][i], upd[3][i]]

    loss = lax.psum(loss_part[0, 0], ("x", "y", "c"))
    return (loss, grad_x[None], *[res[n][0] for n in WEIGHTS], *[res[n][1] for n in WEIGHTS],
            *[res[n][2] for n in WEIGHTS], *[res[n][3] for n in WEIGHTS])
```

```python
import functools
import math

import jax
import jax.numpy as jnp
from jax import lax
from jax.experimental import pallas as pl
from jax.experimental.pallas import tpu as pltpu

F32 = jnp.float32
BF16 = jnp.bfloat16
MESH = pl.DeviceIdType.MESH
ANY = pl.BlockSpec(memory_space=pl.ANY)

N_CHIP = 4
N_DEV = 8
VMEM_LIMIT_BYTES = 56 * 1024 * 1024
SUBLANES = 8
PAD = 8
SCAN_UNROLL = 8

POOL_WINDOWS = (2, 4, 8, 16)
LRU_C = 8.0
DN_ALPHA = 2.0 ** 0.25
LN_EPS = 1e-5
ADAM_LR, ADAM_B1, ADAM_B2, ADAM_EPS, ADAM_WD, ADAM_STEP = 0.001, 0.9, 0.999, 1e-08, 0.01, 10

WEIGHTS = ("w_in", "pool_w", "pool_scale", "conv_w", "conv_b", "lru_wa", "lru_ba", "lru_wx", "lru_bx", "lru_lambda",
           "w_pool_up", "w_lru_up", "w_out", "b_out", "ln1_g", "ln1_b", "w_ff1", "b_ff1", "w_ff2", "b_ff2", "ln2_g", "ln2_b")


def _cparams(sem=None):
    return pltpu.CompilerParams(dimension_semantics=sem, vmem_limit_bytes=VMEM_LIMIT_BYTES)


def _tile(dim, pref, unit=128):
    if dim <= pref:
        return dim
    t = (pref // unit) * unit
    while t > unit and dim % t:
        t -= unit
    assert dim % t == 0, (dim, pref)
    return t


def _mesh_pos():
    x, y, c = lax.axis_index("x"), lax.axis_index("y"), lax.axis_index("c")
    return x, y, c


def _other_chips(x, y):
    return [(1 - x, y), (x, 1 - y), (1 - x, 1 - y)]


def _all_gather_small(v):
    m_per, n = v.shape

    def body(x_ref, out_ref, send_sems, recv_sems, local_sem):
        x, y, c = _mesh_pos()
        me, sibling = (x, y, c), (x, y, 1 - c)
        chips = _other_chips(x, y)

        def rows(px, py, pc):
            return out_ref.at[4 * px + 2 * py + pc]

        def copy(k, block, to, src=None):
            return pltpu.make_async_remote_copy(
                src_ref=rows(*block) if src is None else src, dst_ref=rows(*block),
                send_sem=send_sems.at[k], recv_sem=recv_sems.at[k], device_id=to, device_id_type=MESH)

        mine = pltpu.make_async_copy(x_ref, rows(*me), local_sem)
        mine.start()
        first = [copy(0, me, sibling, src=x_ref)]
        first += [copy(1 + j, me, (*chip, c), src=x_ref) for j, chip in enumerate(chips)]
        for cp in first:
            cp.start()
        passed = [copy(4 + j, (*chip, c), sibling) for j, chip in enumerate(chips)]
        for j, chip in enumerate(chips):
            copy(1 + j, (*chip, c), me).wait_recv()
            passed[j].start()
        copy(0, sibling, me).wait_recv()
        for j, chip in enumerate(chips):
            copy(4 + j, (*chip, 1 - c), me).wait_recv()
        for cp in first + passed:
            cp.wait_send()
        mine.wait()

    return pl.pallas_call(
        body, name="all_gather_small",
        out_shape=jax.ShapeDtypeStruct((N_DEV, m_per, n), v.dtype),
        in_specs=[pl.BlockSpec(memory_space=pltpu.VMEM)],
        out_specs=pl.BlockSpec(memory_space=pltpu.VMEM),
        scratch_shapes=[pltpu.SemaphoreType.DMA((7,)), pltpu.SemaphoreType.DMA((7,)), pltpu.SemaphoreType.DMA],
    )(v)


class _Stage:
    def __init__(self, srcs, bufs, news, n_sems, copies):
        self.srcs, self.bufs, self.news, self.n_sems, self.copies = list(srcs), list(bufs), list(news), n_sems, copies
        self.phases = [(copies, 0)]


class _SemsFrom:
    def __init__(self, ref, offset):
        self.ref, self.offset, self.at = ref, offset, self

    def __getitem__(self, s):
        return self.ref.at[self.offset + s]


def _chain(stages):
    chained = _Stage([], stages[0].bufs, [], sum(st.n_sems for st in stages), None)
    chained.phases, first = [], 0
    for st in stages:
        chained.phases.append((st.copies, first))
        first += st.n_sems
    return chained


def _remote(src, dst, send_sems, recv_sems, s, to):
    return pltpu.make_async_remote_copy(src_ref=src, dst_ref=dst, send_sem=send_sems.at[s], recv_sem=recv_sems.at[s],
                                        device_id=to, device_id_type=MESH)


def _stage_operands(stages, n_in, n_out):
    ins, outs, aliases, scratch = [], [], {}, []
    for st in stages:
        for i in range(len(st.bufs)):
            aliases[n_in + len(ins) + len(st.srcs) + i] = n_out + len(outs) + i
        ins += st.srcs + st.bufs
        outs += [jax.ShapeDtypeStruct(b.shape, b.dtype) for b in st.bufs] + st.news
        scratch += [pltpu.SemaphoreType.DMA((st.n_sems,)), pltpu.SemaphoreType.DMA((st.n_sems,))]
    return ins, outs, aliases, scratch


def _stage_refs(stages, in_refs, out_refs, sem_refs):
    parts, i, o = [], 0, 0
    for n, st in enumerate(stages):
        src = in_refs[i:i + len(st.srcs)]
        i += len(st.srcs) + len(st.bufs)
        buf = out_refs[o:o + len(st.bufs)]
        new = out_refs[o + len(st.bufs):o + len(st.bufs) + len(st.news)]
        o += len(st.bufs) + len(st.news)
        parts.append((src, buf, new, sem_refs[2 * n], sem_refs[2 * n + 1]))
    return parts


def _stage_results(stages, res):
    out, o = [], 0
    for st in stages:
        n = len(st.bufs) + len(st.news)
        out.append(list(res[o:o + n]))
        o += n
    return out


def _stages_start(stages, parts):
    for st, part in zip(stages, parts):
        for cp in st.copies(*part)[0]:
            cp.start()


def _stages_wait(stages, parts):
    for st, part in zip(stages, parts):
        started, landing = st.copies(*part)
        for cp in landing:
            cp.wait_recv()
        for cp in started:
            cp.wait_send()


def _run_stages(name, stages):
    ins, outs, aliases, scratch = _stage_operands(stages, 0, 0)

    def body(*refs):
        parts = _stage_refs(stages, refs[:len(ins)], refs[len(ins):len(ins) + len(outs)], refs[len(ins) + len(outs):])
        for st, (src, buf, new, send_sems, recv_sems) in zip(stages, parts):
            for copies, first in st.phases:
                started, landing = copies(src, buf, new, _SemsFrom(send_sems, first), _SemsFrom(recv_sems, first))
                for cp in started:
                    cp.start()
                for cp in landing:
                    cp.wait_recv()
                for cp in started:
                    cp.wait_send()

    res = pl.pallas_call(
        body, name=name, out_shape=outs, in_specs=[ANY] * len(ins), out_specs=[ANY] * len(outs),
        input_output_aliases=aliases, scratch_shapes=scratch)(*ins)
    return _stage_results(stages, res)


def _gather_direct(ts):
    def copies(src, buf, new, send_sems, recv_sems):
        x, y, c = _mesh_pos()
        started, landing = [], []
        for t in range(len(ts)):
            rh = ts[t].shape[1] // 2
            rows = pl.ds(c * rh, rh)
            mine = buf[t].at[2 * x + y, rows]
            for j, chip in enumerate(_other_chips(x, y)[:2]):
                theirs = buf[t].at[2 * chip[0] + chip[1], rows]
                started.append(_remote(mine, mine, send_sems, recv_sems, 2 * t + j, (*chip, c)))
                landing.append(_remote(theirs, theirs, send_sems, recv_sems, 2 * t + j, (x, y, c)))
        return started, landing

    return _Stage([], ts, [], 2 * len(ts), copies)


def _gather_relay(ts):
    def copies(src, buf, new, send_sems, recv_sems):
        x, y, c = _mesh_pos()
        (x_nb, y_nb, diag) = _other_chips(x, y)
        block = lambda chip: 2 * chip[0] + chip[1]
        started, landing = [], []
        for t in range(len(ts)):
            rq = ts[t].shape[1] // 4
            q0, q1 = pl.ds(2 * c * rq, rq), pl.ds((2 * c + 1) * rq, rq)
            from_y, from_x = buf[t].at[block(y_nb), q0], buf[t].at[block(x_nb), q1]
            started.append(_remote(from_y, from_y, send_sems, recv_sems, 2 * t, (*x_nb, c)))
            started.append(_remote(from_x, from_x, send_sems, recv_sems, 2 * t + 1, (*y_nb, c)))
            for j, q in enumerate((q0, q1)):
                lands = buf[t].at[block(diag), q]
                landing.append(_remote(lands, lands, send_sems, recv_sems, 2 * t + j, (x, y, c)))
        return started, landing

    return _Stage([], ts, [], 2 * len(ts), copies)


def _together(stages):
    def copies(src, buf, new, send_sems, recv_sems):
        started, landing, first = [], [], 0
        for st in stages:
            more = st.copies(src, buf, new, _SemsFrom(send_sems, first), _SemsFrom(recv_sems, first))
            started, landing, first = started + more[0], landing + more[1], first + st.n_sems
        return started, landing

    return _Stage([], stages[0].bufs, [], sum(st.n_sems for st in stages), copies)


def _gather_d2d(ts, which=(0, 1, 2)):
    def copies(src, buf, new, send_sems, recv_sems):
        x, y, c = _mesh_pos()
        started, landing = [], []
        for t in range(len(ts)):
            rh = ts[t].shape[1] // 2
            for j, chip in enumerate(_other_chips(x, y)):
                if j not in which:
                    continue
                got = buf[t].at[2 * chip[0] + chip[1], pl.ds(c * rh, rh)]
                other = buf[t].at[2 * chip[0] + chip[1], pl.ds((1 - c) * rh, rh)]
                started.append(_remote(got, got, send_sems, recv_sems, 3 * t + j, (x, y, 1 - c)))
                landing.append(_remote(other, other, send_sems, recv_sems, 3 * t + j, (x, y, c)))
        return started, landing

    return _Stage([], ts, [], 3 * len(ts), copies)


def _swap_halves(gs):
    def copies(src, buf, new, send_sems, recv_sems):
        x, y, c = _mesh_pos()
        started, landing = [], []
        for t in range(len(gs)):
            rh = gs[t].shape[1] // 2
            started.append(_remote(src[t].at[:, pl.ds((1 - c) * rh, rh)], new[t], send_sems, recv_sems, t, (x, y, 1 - c)))
            landing.append(_remote(new[t], new[t], send_sems, recv_sems, t, (x, y, c)))
        return started, landing

    news = [jax.ShapeDtypeStruct((g.shape[0], g.shape[1] // 2, g.shape[2]), g.dtype) for g in gs]
    return _Stage(gs, [], news, len(gs), copies)


def _send_to_sibling(gs):
    def copies(src, buf, new, send_sems, recv_sems):
        x, y, c = _mesh_pos()
        started = [_remote(src[t], new[t], send_sems, recv_sems, t, (x, y, 1 - c)) for t in range(len(gs))]
        landing = [_remote(new[t], new[t], send_sems, recv_sems, t, (x, y, c)) for t in range(len(gs))]
        return started, landing

    return _Stage(gs, [], [jax.ShapeDtypeStruct(g.shape, g.dtype) for g in gs], len(gs), copies)


def _scatter_chips(ps):
    def copies(src, buf, new, send_sems, recv_sems):
        x, y, c = _mesh_pos()
        started, landing = [], []
        for t in range(len(ps)):
            for j, chip in enumerate(_other_chips(x, y)):
                started.append(_remote(src[t].at[2 * chip[0] + chip[1]], new[t].at[j], send_sems, recv_sems, 3 * t + j, (*chip, c)))
                landing.append(_remote(new[t].at[j], new[t].at[j], send_sems, recv_sems, 3 * t + j, (x, y, c)))
        return started, landing

    return _Stage(ps, [], [jax.ShapeDtypeStruct((3,) + p.shape[1:], p.dtype) for p in ps], 3 * len(ps), copies)


def _join_halves(fs):
    def copies(src, buf, new, send_sems, recv_sems):
        x, y, c = _mesh_pos()
        started, landing = [], []
        for t in range(len(fs)):
            rh = fs[t].shape[0] // 2
            mine = buf[t].at[pl.ds(c * rh, rh)]
            theirs = buf[t].at[pl.ds((1 - c) * rh, rh)]
            started.append(_remote(mine, mine, send_sems, recv_sems, t, (x, y, 1 - c)))
            landing.append(_remote(theirs, theirs, send_sems, recv_sems, t, (x, y, c)))
        return started, landing

    return _Stage([], fs, [], len(fs), copies)


def _cast_place(w, k_me):
    rows, cols = w.shape
    tr = _tile(rows, 512, 16)

    def body(k_ref, w_ref, o_ref):
        o_ref[...] = w_ref[...].astype(BF16)

    return pl.pallas_call(
        body, name="cast_place", out_shape=_sds((N_CHIP, rows, cols), BF16),
        grid_spec=pltpu.PrefetchScalarGridSpec(
            num_scalar_prefetch=1, grid=(rows // tr,),
            in_specs=[pl.BlockSpec((tr, cols), lambda i, k_ref: (i, 0))],
            out_specs=pl.BlockSpec((None, tr, cols), lambda i, k_ref: (k_ref[0], i, 0))),
        compiler_params=_cparams(("arbitrary",)),
    )(_scalar(k_me), w)


_DIMS = {"nn": (((1,), (0,)), ((), ())), "nt": (((1,), (1,)), ((), ())), "tn": (((0,), (0,)), ((), ()))}


def _accum(ref, val, first):
    @pl.when(first)
    def _():
        ref[...] = val

    @pl.when(jnp.logical_not(first))
    def _():
        ref[...] += val


def _grid_edges(grid):
    ids = [pl.program_id(ax) for ax in range(len(grid))]
    first = functools.reduce(jnp.logical_and, [i == 0 for i in ids])
    last = functools.reduce(jnp.logical_and, [i == n - 1 for i, n in zip(ids, grid)])
    return first, last


def _host_call(name, grid, body, operands, in_specs, out_shape, out_specs, scratch, stages, prefetch=None):
    s_ins, s_outs, aliases, s_scratch = _stage_operands(stages, len(operands), len(out_shape))
    n_in, n_out, n_scr = len(operands), len(out_shape), len(scratch)
    n_pre = 0 if prefetch is None else 1

    def full_body(*refs):
        refs = refs[n_pre:]
        in_refs = refs[:n_in]
        s_in_refs = refs[n_in:n_in + len(s_ins)]
        o0 = n_in + len(s_ins)
        out_refs = refs[o0:o0 + n_out]
        s_out_refs = refs[o0 + n_out:o0 + n_out + len(s_outs)]
        c0 = o0 + n_out + len(s_outs)
        scr_refs = refs[c0:c0 + n_scr]
        if stages:
            parts = _stage_refs(stages, s_in_refs, s_out_refs, refs[c0 + n_scr:])
            first, last = _grid_edges(grid)
            pl.when(first)(lambda: _stages_start(stages, parts))
        body(in_refs, out_refs, scr_refs)
        if stages:
            pl.when(last)(lambda: _stages_wait(stages, parts))

    all_in = list(in_specs) + [ANY] * len(s_ins)
    all_out = list(out_specs) + [ANY] * len(s_outs)
    all_scratch = list(scratch) + s_scratch
    params = _cparams(("arbitrary",) * len(grid))
    if prefetch is None:
        res = pl.pallas_call(
            full_body, name=name, grid=grid, in_specs=all_in, out_specs=all_out, out_shape=list(out_shape) + s_outs,
            input_output_aliases=aliases, scratch_shapes=all_scratch, compiler_params=params,
        )(*operands, *s_ins)
    else:
        res = pl.pallas_call(
            full_body, name=name, out_shape=list(out_shape) + s_outs,
            grid_spec=pltpu.PrefetchScalarGridSpec(num_scalar_prefetch=1, grid=grid, in_specs=all_in, out_specs=all_out,
                                                   scratch_shapes=all_scratch),
            input_output_aliases={i + 1: o for i, o in aliases.items()}, compiler_params=params,
        )(prefetch, *operands, *s_ins)
    return list(res[:n_out]), _stage_results(stages, res[n_out:])


def _matmul(name, grid, pairs, extras, outs, acc_shape, epilogue, stages=(), prefetch=None, lhs_to_epilogue=False):
    n_p = len(pairs)
    n_k = grid[-1]
    dims = [_DIMS[p[4]] for p in pairs]

    def body(in_refs, out, accs):
        ab, ex = in_refs[:2 * n_p], in_refs[2 * n_p:]
        if lhs_to_epilogue:
            ex = [ab[0]] + list(ex)
        ids = [pl.program_id(ax) for ax in range(len(grid))]
        k = ids[-1]

        def dot(p):
            return lax.dot_general(ab[2 * p][...].astype(BF16), ab[2 * p + 1][...].astype(BF16), dims[p],
                                   preferred_element_type=F32)

        if n_k == 1:
            epilogue([dot(p) for p in range(n_p)], ex, out, ids)
            return

        @pl.when(k == 0)
        def _():
            for acc in accs:
                acc[...] = jnp.zeros_like(acc)

        for p in range(n_p):
            accs[p][...] += dot(p)

        @pl.when(k == n_k - 1)
        def _():
            epilogue([acc[...] for acc in accs], ex, out, ids)

    in_specs = []
    operands = []
    for a, a_spec, b, b_spec, _ in pairs:
        in_specs += [a_spec, b_spec]
        operands += [a, b]
    for e, e_spec in extras:
        in_specs.append(e_spec)
        operands.append(e)
    res, stage_res = _host_call(name, grid, body, operands, in_specs, [o[0] for o in outs], [o[1] for o in outs],
                                [pltpu.VMEM(acc_shape, F32) for _ in pairs] if n_k > 1 else [], list(stages), prefetch)
    return (res, stage_res) if stages else res


def _out(res, stages, single=False):
    outs = res[0] if stages else res
    outs = outs[0] if single else outs
    return (outs, res[1]) if stages else outs


def _sds(shape, dtype):
    return jax.ShapeDtypeStruct(shape, dtype)


def _row(n):
    return pl.BlockSpec((1, n), lambda *_: (0, 0))


def _layer_norm(r):
    mu = jnp.mean(r, axis=-1, keepdims=True)
    xc = r - mu
    var = jnp.mean(xc * xc, axis=-1, keepdims=True)
    rstd = lax.rsqrt(var + LN_EPS)
    return xc * rstd, rstd


def _layer_norm_bwd(dxhat, xhat, rstd):
    m1 = jnp.mean(dxhat, axis=-1, keepdims=True)
    m2 = jnp.mean(dxhat * xhat, axis=-1, keepdims=True)
    return rstd * (dxhat - m1 - xhat * m2)


def _colsum(v):
    return jnp.sum(v, axis=0, keepdims=True)


ROW_TILE = 256


def _rows_call(name, s, epi, ins, outs):
    tr = _tile(s, ROW_TILE, SUBLANES)

    def spec(shape, kind):
        n = shape[1]
        return pl.BlockSpec((tr, n), lambda i: (i, 0)) if kind == "tile" else pl.BlockSpec((1, n), lambda i: (0, 0))

    def body(in_refs, out_refs, scr):
        epi([in_refs[0][...]], in_refs[1:], out_refs, [pl.program_id(0)])

    return _host_call(name, (s // tr,), body, [a for a, _ in ins], [spec(a.shape, k) for a, k in ins],
                      [o for o, _ in outs], [spec(o.shape, k) for o, k in outs], [], [])[0]


def _plain_matmul(name, a, b, mode):
    m, k_dim = a.shape
    n = b.shape[1]
    tm, tn, tk = _tile(m, 1024), _tile(n, 1024), _tile(k_dim, 2048)

    def epi(accs, ex, out, ids):
        out[0][...] = accs[0]

    assert mode == "nn"
    return _matmul(
        name, (m // tm, n // tn, k_dim // tk),
        [(a, pl.BlockSpec((tm, tk), lambda i, j, k: (i, k)), b, pl.BlockSpec((tk, tn), lambda i, j, k: (k, j)), "nn")],
        [], [(_sds((m, n), F32), pl.BlockSpec((tm, tn), lambda i, j, k: (i, j)))], (tm, tn), epi)[0]


def _fwd_in(x_in, wg_in, stages=()):
    s, d = x_in.shape
    inc = wg_in.shape[2]
    tm, tn, tk = _tile(s, 1024), _tile(inc, 1280), _tile(d, 2048)
    nb = inc // tn
    assert tk == d

    def epi(accs, ex, out, ids):
        out[0][...] = accs[0].astype(BF16)

        @pl.when(ids[1] == 0)
        def _():
            out[1][...] = ex[0][...].astype(BF16)

    x_spec = pl.BlockSpec((tm, tk), lambda i, j, k: (i, k))
    return _out(_matmul(
        "fwd_in", (s // tm, N_CHIP * nb, d // tk),
        [(x_in, x_spec, wg_in, pl.BlockSpec((None, tk, tn), lambda i, j, k: (j // nb, k, j % nb)), "nn")],
        [],
        [(_sds((s, N_CHIP * inc), BF16), pl.BlockSpec((tm, tn), lambda i, j, k: (i, j))), (_sds((s, d), BF16), x_spec)],
        (tm, tn), epi, stages, lhs_to_epilogue=True), stages)


def _fwd_merge(y_pool, y_lru, w_pu, w_lu, z, stages=()):
    s, d = y_pool.shape
    tm, tn, tk = _tile(s, 1024), _tile(d, 1024), _tile(d, 1024)
    ga0, gb0 = 3 * d // tn, 4 * d // tn

    def epi(accs, ex, out, ids):
        sa = _sigmoid(ex[0][...].astype(F32))
        sb = _sigmoid(ex[1][...].astype(F32))
        out[0][...] = (sa * accs[0] + sb * accs[1]).astype(BF16)
        out[1][...] = accs[0].astype(BF16)
        out[2][...] = accs[1].astype(BF16)

    a_spec = pl.BlockSpec((tm, tk), lambda i, j, k: (i, k))
    b_spec = pl.BlockSpec((tk, tn), lambda i, j, k: (k, j))
    o_spec = pl.BlockSpec((tm, tn), lambda i, j, k: (i, j))
    return _out(_matmul(
        "fwd_merge", (s // tm, d // tn, d // tk),
        [(y_pool, a_spec, w_pu, b_spec, "nn"), (y_lru, a_spec, w_lu, b_spec, "nn")],
        [(z, pl.BlockSpec((tm, tn), lambda i, j, k: (i, ga0 + j))), (z, pl.BlockSpec((tm, tn), lambda i, j, k: (i, gb0 + j)))],
        [(_sds((s, d), BF16), o_spec)] * 3, (tm, tn), epi, stages), stages)


def _fwd_out_ln1(m, w_out, x, b_out, g1, b1, stages=()):
    s, d = x.shape
    tm, tk = _tile(s, 512), _tile(d, 2048)

    def epi(accs, ex, out, ids):
        r = DN_ALPHA * ex[0][...] + accs[0] + ex[1][...]
        xhat, rstd = _layer_norm(r)
        out[0][...] = xhat
        out[1][...] = (xhat * ex[2][...] + ex[3][...]).astype(BF16)
        out[2][...] = rstd

    full = pl.BlockSpec((tm, d), lambda i, j, k: (i, 0))
    return _out(_matmul(
        "fwd_out_ln1", (s // tm, 1, d // tk),
        [(m, pl.BlockSpec((tm, tk), lambda i, j, k: (i, k)), w_out, pl.BlockSpec((tk, d), lambda i, j, k: (k, 0)), "nn")],
        [(x, full), (b_out, _row(d)), (g1, _row(d)), (b1, _row(d))],
        [(_sds((s, d), F32), full), (_sds((s, d), BF16), full), (_sds((s, 1), F32), pl.BlockSpec((tm, 1), lambda i, j, k: (i, 0)))],
        (tm, d), epi, stages), stages)


def _fwd_ff1(x1_bf, wg_ff1, b_ff1, stages=()):
    s, d = x1_bf.shape
    fc = wg_ff1.shape[2]
    tm, tn, tk = _tile(s, 1024), _tile(fc, 1024), _tile(d, 2048)
    nb = fc // tn

    def epi(accs, ex, out, ids):
        p = jnp.maximum(accs[0] + ex[0][...], 0.0)
        out[0][...] = (p * p).astype(BF16)
        out[1][...] = (2.0 * p).astype(BF16)

    o = (_sds((s, N_CHIP * fc), BF16), pl.BlockSpec((tm, tn), lambda i, j, k: (i, j)))
    return _out(_matmul(
        "fwd_ff1", (s // tm, N_CHIP * nb, d // tk),
        [(x1_bf, pl.BlockSpec((tm, tk), lambda i, j, k: (i, k)),
          wg_ff1, pl.BlockSpec((None, tk, tn), lambda i, j, k: (j // nb, k, j % nb)), "nn")],
        [(b_ff1, pl.BlockSpec((1, tn), lambda i, j, k: (0, j)))], [o, o], (tm, tn), epi, stages), stages)


def _fwd_ff2_ln2_loss(hdn, w_ff2, xhat1, g1, b1, b_ff2, g2, b2, target):
    s, f = hdn.shape
    d = xhat1.shape[1]

    def epi(accs, ex, out, ids):
        first = ids[0] == 0
        x1 = ex[0][...] * ex[1][...] + ex[2][...]
        r = DN_ALPHA * x1 + accs[0] + ex[3][...]
        xhat, rstd = _layer_norm(r)
        g2v = ex[4][...]
        err = xhat * g2v + ex[5][...] - ex[6][...]
        part = 0.5 * jnp.sum(jnp.mean(err * err, axis=-1, keepdims=True), axis=0, keepdims=True)
        dy = err * (1.0 / d)
        dr2 = _layer_norm_bwd(dy * g2v, xhat, rstd)
        out[0][...] = dr2
        out[1][...] = dr2.astype(BF16)
        _accum(out[2], _colsum(dy * xhat), first)
        _accum(out[3], _colsum(dy), first)
        _accum(out[4], _colsum(dr2), first)
        _accum(out[5], jnp.broadcast_to(part, (1, 128)), first)

    ff = _plain_matmul("fwd_ff2", hdn, w_ff2, "nn")
    vec = lambda n: (_sds((1, n), F32), "vec")
    return _rows_call(
        "ln2_loss", s, epi,
        [(ff, "tile"), (xhat1, "tile"), (g1, "vec"), (b1, "vec"), (b_ff2, "vec"), (g2, "vec"), (b2, "vec"), (target, "tile")],
        [(_sds((s, d), F32), "tile"), (_sds((s, d), BF16), "tile"), vec(d), vec(d), vec(d), vec(128)])


def _bwd_ff2_in(dr2_bf, w_ff2, hdn, stages=()):
    s, d = dr2_bf.shape
    f = hdn.shape[1]
    tm, tn, tk = _tile(s, 1024), _tile(f, 1024), _tile(d, 2048)

    def epi(accs, ex, out, ids):
        dpre = accs[0] * ex[0][...].astype(F32)
        out[0][...] = dpre.astype(BF16)
        _accum(out[1], _colsum(dpre), ids[1] == 0)

    return _out(_matmul(
        "bwd_ff2_in", (f // tn, s // tm, d // tk),
        [(dr2_bf, pl.BlockSpec((tm, tk), lambda j, i, k: (i, k)), w_ff2, pl.BlockSpec((tn, tk), lambda j, i, k: (j, k)), "nt")],
        [(hdn, pl.BlockSpec((tm, tn), lambda j, i, k: (i, j)))],
        [(_sds((s, f), BF16), pl.BlockSpec((tm, tn), lambda j, i, k: (i, j))), (_sds((1, f), F32), pl.BlockSpec((1, tn), lambda j, i, k: (0, j)))],
        (tm, tn), epi, stages), stages)


def _bwd_ff1_in_ln1(dpre, wg_ff1, dr2, xhat1, rstd1, g1, stages=()):
    s, f = dpre.shape
    d = xhat1.shape[1]
    fc = wg_ff1.shape[2]
    tm, tn, tk = _tile(s, 1024), _tile(d, 1024), _tile(fc, 2048)
    nb = fc // tk

    def epi(accs, ex, out, ids):
        first = ids[0] == 0
        xhat = ex[1][...]
        dx1 = accs[0] + DN_ALPHA * ex[0][...]
        dr1 = _layer_norm_bwd(dx1 * ex[3][...], xhat, ex[2][...])
        out[0][...] = dr1
        out[1][...] = dr1.astype(BF16)
        _accum(out[2], _colsum(dx1 * xhat), first)
        _accum(out[3], _colsum(dx1), first)
        _accum(out[4], _colsum(dr1), first)

    def plain(accs, ex, out, ids):
        out[0][...] = accs[0]

    o_spec = pl.BlockSpec((tm, tn), lambda i, j, k: (i, j))
    mm = _out(_matmul(
        "bwd_ff1_in", (s // tm, d // tn, f // tk),
        [(dpre, pl.BlockSpec((tm, tk), lambda i, j, k: (i, k)),
          wg_ff1, pl.BlockSpec((None, tn, tk), lambda i, j, k: (k // nb, j, k % nb)), "nt")],
        [], [(_sds((s, d), F32), o_spec)], (tm, tn), plain, stages), stages, True)
    mm, stage_res = mm if stages else (mm, None)
    vec = (_sds((1, d), F32), "vec")
    rows = _rows_call(
        "ln1_bwd", s, epi, [(mm, "tile"), (dr2, "tile"), (xhat1, "tile"), (rstd1, "tile"), (g1, "vec")],
        [(_sds((s, d), F32), "tile"), (_sds((s, d), BF16), "tile"), vec, vec, vec])
    return (rows, stage_res) if stages else rows


def _bwd_out_in(dr1_bf, w_out, z, pa, pb, stages=()):
    s, d = dr1_bf.shape
    tm, tn, tk = _tile(s, 1024), _tile(d, 1024), _tile(d, 2048)
    ga0, gb0 = 3 * d // tn, 4 * d // tn

    def epi(accs, ex, out, ids):
        dm = accs[0]
        sa = _sigmoid(ex[0][...].astype(F32))
        sb = _sigmoid(ex[1][...].astype(F32))
        out[0][...] = (dm * sa).astype(BF16)
        out[1][...] = (dm * sb).astype(BF16)
        out[2][...] = (dm * ex[2][...].astype(F32) * sa * (1.0 - sa)).astype(BF16)
        out[3][...] = (dm * ex[3][...].astype(F32) * sb * (1.0 - sb)).astype(BF16)

    o_spec = pl.BlockSpec((tm, tn), lambda i, j, k: (i, j))
    return _out(_matmul(
        "bwd_out_in", (s // tm, d // tn, d // tk),
        [(dr1_bf, pl.BlockSpec((tm, tk), lambda i, j, k: (i, k)), w_out, pl.BlockSpec((tn, tk), lambda i, j, k: (j, k)), "nt")],
        [(z, pl.BlockSpec((tm, tn), lambda i, j, k: (i, ga0 + j))), (z, pl.BlockSpec((tm, tn), lambda i, j, k: (i, gb0 + j))),
         (pa, o_spec), (pb, o_spec)],
        [(_sds((s, d), BF16), o_spec)] * 4, (tm, tn), epi, stages), stages)


def _bwd_up_in(name, dp, w_up, stages=()):
    s, d = dp.shape
    n = w_up.shape[0]
    tm, tn, tk = _tile(s, 1024), _tile(n, 1024), _tile(d, 2048)

    def epi(accs, ex, out, ids):
        out[0][...] = accs[0].astype(BF16)

    return _out(_matmul(
        name, (s // tm, n // tn, d // tk),
        [(dp, pl.BlockSpec((tm, tk), lambda i, j, k: (i, k)), w_up, pl.BlockSpec((tn, tk), lambda i, j, k: (j, k)), "nt")],
        [], [(_sds((s, n), BF16), pl.BlockSpec((tm, tn), lambda i, j, k: (i, j)))], (tm, tn), epi, stages), stages, True)


def _bwd_in(dz, wg_in, dr1, stages=()):
    s, d = dr1.shape
    inc = wg_in.shape[2]
    tm, tn, tk = _tile(s, 1024), _tile(d, 1024), _tile(inc, 2560)
    nb = inc // tk

    def epi(accs, ex, out, ids):
        out[0][...] = accs[0] + DN_ALPHA * ex[0][...]

    o_spec = pl.BlockSpec((tm, tn), lambda i, j, k: (i, j))
    return _out(_matmul(
        "bwd_in", (s // tm, d // tn, N_CHIP * nb),
        [(dz, pl.BlockSpec((tm, tk), lambda i, j, k: (i, k)),
          wg_in, pl.BlockSpec((None, tn, tk), lambda i, j, k: (k // nb, j, k % nb)), "nt")],
        [(dr1, o_spec)], [(_sds((s, d), F32), o_spec)], (tm, tn), epi, stages), stages, True)


def _wgrad(name, a, b, col_sharded, stages=()):
    s, ka = a.shape
    n = b.shape[1]
    tm, tk = _tile(ka, 1024), _tile(s, 2048)
    tn = _tile(n // N_CHIP, 1280) if col_sharded else _tile(n, 1024)

    def epi(accs, ex, out, ids):
        out[0][...] = accs[0].astype(BF16)

    if col_sharded:
        nb = (n // N_CHIP) // tn
        o = (_sds((N_CHIP, ka, n // N_CHIP), BF16), pl.BlockSpec((None, tm, tn), lambda i, j, k: (j // nb, i, j % nb)))
    else:
        o = (_sds((ka, n), BF16), pl.BlockSpec((tm, tn), lambda i, j, k: (i, j)))
    res = _out(_matmul(
        name, (ka // tm, n // tn, s // tk),
        [(a, pl.BlockSpec((tk, tm), lambda i, j, k: (k, i)), b, pl.BlockSpec((tk, tn), lambda i, j, k: (k, j)), "tn")],
        [], [o], (tm, tn), epi, stages), stages, True)
    res, stage_res = res if stages else (res, None)
    res = res if col_sharded else res.reshape(N_CHIP, ka // N_CHIP, n)
    return (res, stage_res) if stages else res


def _wgrad_rows_half(name, a, b, half, stages):
    s, ka = a.shape
    n = b.shape[1]
    kh = ka // 2
    tm, tk, tn = _tile(kh, 1024), _tile(s, 2048), _tile(n // N_CHIP, 1280)
    nb, ni = (n // N_CHIP) // tn, kh // tm

    def epi(accs, ex, out, ids):
        out[0][...] = accs[0].astype(BF16)

    return _out(_matmul(
        name, (ni, n // tn, s // tk),
        [(a, pl.BlockSpec((tk, tm), lambda i, j, k, sel: (k, sel[0] * ni + i)),
          b, pl.BlockSpec((tk, tn), lambda i, j, k, sel: (k, j)), "tn")],
        [], [(_sds((N_CHIP, kh, n // N_CHIP), BF16), pl.BlockSpec((None, tm, tn), lambda i, j, k, sel: (j // nb, i, j % nb)))],
        (tm, tn), epi, stages, _scalar(half)), stages, True)


def _chunk(s):
    return _tile(s, 512, SUBLANES)


def _zero_pads(ref, s):
    zeros = jnp.zeros((PAD, ref.shape[1]), F32)
    ref[pl.ds(0, PAD), :] = zeros
    ref[pl.ds(PAD + s, PAD), :] = zeros


def _window(ref, t0, t):
    return ref[pl.ds(t0, t + 2 * PAD), :]


def _shift(sup, off, t):
    return sup[PAD + off:PAD + off + t, :]


def _pool_count(t0, t, s, w):
    pos = t0 + lax.broadcasted_iota(jnp.int32, (t, 1), 0)
    return (jnp.minimum(pos + w // 2, s) - jnp.maximum(pos - w // 2, 0)).astype(F32)


def _pool_fwd(z, pool_w, pool_scale, stages=()):
    s = z.shape[0]
    n_g, pg = pool_w.shape[0], pool_w.shape[1]
    assert n_g == len(POOL_WINDOWS) and max(POOL_WINDOWS) // 2 <= PAD
    t = _chunk(s)

    def body(u_ref, w_ref, sc_ref, d_ref, y_ref, pad_ref):
        g = pl.program_id(0)
        _zero_pads(pad_ref, s)
        pad_ref[pl.ds(PAD, s), :] = u_ref[...].astype(F32)
        for gi, w in enumerate(POOL_WINDOWS):
            @pl.when(g == gi)
            def _():
                def step(ch, carry):
                    t0 = pl.multiple_of(ch * t, t)
                    sup = _window(pad_ref, t0, t)
                    acc = _shift(sup, -(w // 2), t)
                    for o in range(-(w // 2) + 1, w // 2):
                        acc = acc + _shift(sup, o, t)
                    dd = (acc * (1.0 / _pool_count(t0, t, s, w)) - _shift(sup, 0, t)).astype(BF16)
                    d_ref[pl.ds(t0, t), :] = dd
                    y = jnp.dot(dd, w_ref[...], preferred_element_type=F32) * sc_ref[...]
                    y_ref[pl.ds(t0, t), :] = y.astype(BF16)
                    return carry

                lax.fori_loop(0, s // t, step, 0)

    blk = pl.BlockSpec((s, pg), lambda g: (0, g))
    res = _host_call(
        "pool_fwd", (n_g,), lambda ins, outs, scr: body(*ins, *outs, *scr), [z, pool_w, pool_scale],
        [blk, pl.BlockSpec((None, pg, pg), lambda g: (g, 0, 0)), pl.BlockSpec((1, pg), lambda g: (0, g))],
        [_sds((s, n_g * pg), BF16)] * 2, [blk, blk], [pltpu.VMEM((s + 2 * PAD, pg), F32)], list(stages))
    return res if stages else res[0]


def _pool_bwd(dsv, dy, pool_w, pool_scale, stages=()):
    s = dsv.shape[0]
    n_g, pg = pool_w.shape[0], pool_w.shape[1]
    t = _chunk(s)

    def body(d_ref, dy_ref, w_ref, sc_ref, du_ref, dw_ref, dsc_ref, epad_ref, dwacc_ref):
        g = pl.program_id(0)
        _zero_pads(epad_ref, s)
        dwacc_ref[...] = jnp.zeros_like(dwacc_ref)
        for gi, w in enumerate(POOL_WINDOWS):
            @pl.when(g == gi)
            def _():
                def first(ch, dsc):
                    t0 = pl.multiple_of(ch * t, t)
                    dd = d_ref[pl.ds(t0, t), :]
                    dyc = dy_ref[pl.ds(t0, t), :].astype(F32)
                    wv = w_ref[...]
                    ypre = jnp.dot(dd, wv, preferred_element_type=F32)
                    dq = (dyc * sc_ref[...]).astype(BF16)
                    dwacc_ref[...] += lax.dot_general(dd, dq, _DIMS["tn"], preferred_element_type=F32)
                    ddv = lax.dot_general(dq, wv, _DIMS["nt"], preferred_element_type=F32)
                    epad_ref[pl.ds(pl.multiple_of(PAD + t0, SUBLANES), t), :] = ddv * (1.0 / _pool_count(t0, t, s, w))
                    return dsc + _colsum(dyc * ypre)

                dsc_ref[...] = lax.fori_loop(0, s // t, first, jnp.zeros((1, pg), F32))

                def second(ch, carry):
                    t0 = pl.multiple_of(ch * t, t)
                    sup = _window(epad_ref, t0, t)
                    acc = _shift(sup, -(w // 2) + 1, t)
                    for o in range(-(w // 2) + 2, w // 2 + 1):
                        acc = acc + _shift(sup, o, t)
                    du_ref[pl.ds(t0, t), :] = (acc - _shift(sup, 0, t) * _pool_count(t0, t, s, w)).astype(BF16)
                    return carry

                lax.fori_loop(0, s // t, second, 0)

        dw_ref[...] = dwacc_ref[...].astype(BF16)

    blk = pl.BlockSpec((s, pg), lambda g: (0, g))
    w_spec = pl.BlockSpec((None, pg, pg), lambda g: (g, 0, 0))
    sc_spec = pl.BlockSpec((1, pg), lambda g: (0, g))
    res = _host_call(
        "pool_bwd", (n_g,), lambda ins, outs, scr: body(*ins, *outs, *scr), [dsv, dy, pool_w, pool_scale],
        [blk, blk, w_spec, sc_spec], [_sds((s, n_g * pg), BF16), _sds((n_g, pg, pg), BF16), _sds((1, n_g * pg), F32)],
        [blk, w_spec, sc_spec], [pltpu.VMEM((s + 2 * PAD, pg), F32), pltpu.VMEM((pg, pg), F32)], list(stages))
    return res if stages else res[0]


def _sigmoid(x):
    return 0.5 * jnp.tanh(0.5 * x) + 0.5


def _softplus(x):
    e = jnp.exp(-jnp.abs(x))
    log1p_e = jnp.where(e < 1e-2, e * (1.0 - e * (0.5 - e * (1.0 / 3.0))), jnp.log(1.0 + e))
    return jnp.maximum(x, 0.0) + log1p_e


_GELU_C = math.sqrt(2.0 / math.pi)


def _gelu(x):
    th = jnp.tanh(_GELU_C * (x + 0.044715 * x * x * x))
    return 0.5 * x * (1.0 + th), th


def _gelu_grad(x, th):
    return 0.5 * (1.0 + th) + 0.5 * x * (1.0 - th * th) * _GELU_C * (1.0 + 3.0 * 0.044715 * x * x)


def _scan_chunk(a_ref, b_ref, o_ref, o_off, carry, t, reverse):
    n = a_ref.shape[1]
    row = lax.broadcasted_iota(jnp.int32, (SUBLANES, n), 0)
    n_groups = t // SUBLANES
    unroll = math.gcd(n_groups, SCAN_UNROLL)
    last = 0 if reverse else SUBLANES - 1

    def step(si, carry):
        for u in range(unroll):
            gi = si * unroll + u
            g = n_groups - 1 - gi if reverse else gi
            r0 = pl.multiple_of(g * SUBLANES, SUBLANES)
            a = a_ref[pl.ds(r0, SUBLANES), :]
            b = b_ref[pl.ds(r0, SUBLANES), :]
            for k in (1, 2, 4):
                keep = row < SUBLANES - k if reverse else row >= k
                sh = SUBLANES - k if reverse else k
                ar = jnp.where(keep, pltpu.roll(a, sh, 0), 1.0)
                br = jnp.where(keep, pltpu.roll(b, sh, 0), 0.0)
                b = a * br + b
                a = a * ar
            o_ref[pl.ds(pl.multiple_of(o_off + r0, SUBLANES), SUBLANES), :] = a * carry + b
            carry = (jnp.broadcast_to(a[last:last + 1, :], a.shape) * carry
                     + jnp.broadcast_to(b[last:last + 1, :], b.shape))
        return carry

    return lax.fori_loop(0, n_groups // unroll, step, carry)


def _lru_params(pk_ref):
    rows = pk_ref[...]
    get = lambda i: rows[i:i + 1, :]
    cw = [get(k) for k in range(4)]
    lam = (get(9), get(10))
    big_l = tuple(-LRU_C * _softplus(-v) for v in lam)
    return cw, get(4), (get(5), get(6)), (get(7), get(8)), lam, big_l


def _conv(sup, cw, cb, t):
    xc = cb + cw[0] * _shift(sup, -2, t)
    for k in range(1, 4):
        xc = xc + cw[k] * _shift(sup, k - 2, t)
    return xc


def _gates(xcb, w_ref, d, bk, ba, bx, big_l):
    pre = jnp.dot(xcb, w_ref[:, pl.ds(d * 2 * bk, 2 * bk)], preferred_element_type=F32)
    r = _sigmoid(pre[:, :bk] + ba[d])
    i = _sigmoid(pre[:, bk:] + bx[d])
    la = big_l[d] * r
    a = jnp.exp(la)
    var = jnp.tanh(-la) * (1.0 + a * a)
    rs = lax.rsqrt(jnp.maximum(var, 1e-30))
    return r, i, a, var * rs, rs


def _lru_specs(s, d, bk):
    u_spec = pl.BlockSpec((s, bk), lambda h: (0, d // bk + h))
    ug_spec = pl.BlockSpec((s, bk), lambda h: (0, 2 * d // bk + h))
    w_spec = pl.BlockSpec((None, bk, 4 * bk), lambda h: (h, 0, 0))
    pk_spec = pl.BlockSpec((None, 16, bk), lambda h: (h, 0, 0))
    blk = pl.BlockSpec((s, bk), lambda h: (0, h))
    return u_spec, ug_spec, w_spec, pk_spec, blk


def _lru_fwd(z, gatew, pk, stages=()):
    s = z.shape[0]
    n_h, bk = gatew.shape[0], gatew.shape[1]
    d = n_h * bk
    t = _chunk(s)
    n_ch = s // t

    def body(u_ref, ug_ref, w_ref, pk_ref, y_ref, upad, h0buf, abuf, bbuf, xcbuf, h1buf):
        _zero_pads(upad, s)
        upad[pl.ds(PAD, s), :] = u_ref[...].astype(F32)
        cw, cb, ba, bx, _, big_l = _lru_params(pk_ref)
        zero = jnp.zeros((SUBLANES, bk), F32)

        def fill(xc, dr):
            _, i, a, sq, _ = _gates(xc.astype(BF16), w_ref, dr, bk, ba, bx, big_l)
            abuf[...] = a
            bbuf[...] = sq * i * xc

        def up(ch, carry):
            t0 = pl.multiple_of(ch * t, t)
            xc = _conv(_window(upad, t0, t), cw, cb, t)
            xcbuf[pl.ds(t0, t), :] = xc
            fill(xc, 0)
            return _scan_chunk(abuf, bbuf, h0buf, t0, carry, t, False)

        lax.fori_loop(0, n_ch, up, zero)

        def down(ci, carry):
            t0 = pl.multiple_of((n_ch - 1 - ci) * t, t)
            fill(xcbuf[pl.ds(t0, t), :], 1)
            carry = _scan_chunk(abuf, bbuf, h1buf, 0, carry, t, True)
            gl, _ = _gelu(ug_ref[pl.ds(t0, t), :].astype(F32))
            y_ref[pl.ds(t0, t), :] = ((h0buf[pl.ds(t0, t), :] + h1buf[...]) * gl).astype(BF16)
            return carry

        lax.fori_loop(0, n_ch, down, zero)

    u_spec, ug_spec, w_spec, pk_spec, blk = _lru_specs(s, d, bk)
    res = _host_call(
        "lru_fwd", (n_h,), lambda ins, outs, scr: body(*ins, *outs, *scr), [z, z, gatew, pk],
        [u_spec, ug_spec, w_spec, pk_spec], [_sds((s, d), BF16)], [blk],
        [pltpu.VMEM((s + 2 * PAD, bk), F32), pltpu.VMEM((s, bk), F32), pltpu.VMEM((t, bk), F32), pltpu.VMEM((t, bk), F32),
         pltpu.VMEM((s, bk), F32), pltpu.VMEM((t, bk), F32)], list(stages))
    return (res[0][0], res[1]) if stages else res[0][0]


def _lru_grads(lam_, hnb, a, sq, rs, r, i, xc, xcb, w_ref, dwacc, d, big_l, acc):
    bk = xc.shape[1]
    dba, dbx, dl = acc
    q = lam_ * i * xc
    dla = lam_ * hnb * a - q * (a * a) * rs
    dpr = dla * big_l * r * (1.0 - r)
    dpi = q * sq * (1.0 - i)
    dprb, dpib = dpr.astype(BF16), dpi.astype(BF16)
    c0 = d * 2 * bk
    dxc = (lam_ * sq * i
           + lax.dot_general(dprb, w_ref[:, pl.ds(c0, bk)], _DIMS["nt"], preferred_element_type=F32)
           + lax.dot_general(dpib, w_ref[:, pl.ds(c0 + bk, bk)], _DIMS["nt"], preferred_element_type=F32))
    dwacc[:, pl.ds(c0, bk)] += lax.dot_general(xcb, dprb, _DIMS["tn"], preferred_element_type=F32)
    dwacc[:, pl.ds(c0 + bk, bk)] += lax.dot_general(xcb, dpib, _DIMS["tn"], preferred_element_type=F32)
    return dxc, (dba + _colsum(dpr), dbx + _colsum(dpi), dl + _colsum(dla * r))


def _lru_bwd(z, dy, gatew, pk, stages=()):
    s = z.shape[0]
    n_h, bk = gatew.shape[0], gatew.shape[1]
    d = n_h * bk
    t = _chunk(s)
    n_ch = s // t

    def body(u_ref, ug_ref, dy_ref, w_ref, pk_ref, du_ref, dug_ref, dw_ref, dpk_ref,
             upad, h0pad, h1pad, dxpad, abuf, bbuf, lbuf, dwacc, edge, xcbuf):
        for ref in (upad, h0pad, h1pad, dxpad):
            _zero_pads(ref, s)
        upad[pl.ds(PAD, s), :] = u_ref[...].astype(F32)
        dwacc[...] = jnp.zeros_like(dwacc)
        cw, cb, ba, bx, lam, big_l = _lru_params(pk_ref)
        zero = jnp.zeros((SUBLANES, bk), F32)
        zrow = jnp.zeros((1, bk), F32)
        rowi = lax.broadcasted_iota(jnp.int32, (t, bk), 0)

        def at(t0):
            return pl.ds(pl.multiple_of(PAD + t0, SUBLANES), t)

        def conv_in(t0):
            xc = xcbuf[pl.ds(t0, t), :]
            return xc, xc.astype(BF16)

        def dh_of(t0):
            ug = ug_ref[pl.ds(t0, t), :].astype(F32)
            gl, th = _gelu(ug)
            dyv = dy_ref[pl.ds(t0, t), :].astype(F32)
            return dyv * gl, dyv * _gelu_grad(ug, th)

        def sweep1(ch, carry):
            t0 = pl.multiple_of(ch * t, t)
            xc = _conv(_window(upad, t0, t), cw, cb, t)
            xcbuf[pl.ds(t0, t), :] = xc
            _, i, a, sq, _ = _gates(xc.astype(BF16), w_ref, 0, bk, ba, bx, big_l)
            abuf[...] = a
            bbuf[...] = sq * i * xc
            return _scan_chunk(abuf, bbuf, h0pad, PAD + t0, carry, t, False)

        lax.fori_loop(0, n_ch, sweep1, zero)

        edge[...] = zero

        def sweep2(ci, st):
            carry_h, carry_l, acc = st
            t0 = pl.multiple_of((n_ch - 1 - ci) * t, t)
            xc, xcb = conv_in(t0)
            _, i1, a1, sq1, _ = _gates(xcb, w_ref, 1, bk, ba, bx, big_l)
            abuf[...] = a1
            bbuf[...] = sq1 * i1 * xc
            carry_h = _scan_chunk(abuf, bbuf, h1pad, PAD + t0, carry_h, t, True)
            dh, dgl = dh_of(t0)
            dug_ref[pl.ds(t0, t), :] = (dgl * (h0pad[at(t0), :] + h1pad[at(t0), :])).astype(BF16)
            r0, i0, a0, sq0, rs0 = _gates(xcb, w_ref, 0, bk, ba, bx, big_l)
            abuf[...] = jnp.where(rowi == t - 1, edge[0:1, :], pltpu.roll(a0, t - 1, 0))
            bbuf[...] = dh
            carry_l = _scan_chunk(abuf, bbuf, lbuf, 0, carry_l, t, True)
            edge[...] = jnp.broadcast_to(a0[0:1, :], (SUBLANES, bk))
            hprev = _shift(_window(h0pad, t0, t), -1, t)
            dxc, acc = _lru_grads(lbuf[...], hprev, a0, sq0, rs0, r0, i0, xc, xcb, w_ref, dwacc, 0, big_l[0], acc)
            dxpad[at(t0), :] = dxc
            return carry_h, carry_l, acc

        _, _, acc0 = lax.fori_loop(0, n_ch, sweep2, (zero, zero, (zrow, zrow, zrow)))

        edge[...] = zero

        def sweep3(ch, st):
            carry_l, acc = st
            t0 = pl.multiple_of(ch * t, t)
            xc, xcb = conv_in(t0)
            r1, i1, a1, sq1, rs1 = _gates(xcb, w_ref, 1, bk, ba, bx, big_l)
            dh, _ = dh_of(t0)
            abuf[...] = jnp.where(rowi == 0, edge[0:1, :], pltpu.roll(a1, 1, 0))
            bbuf[...] = dh
            carry_l = _scan_chunk(abuf, bbuf, lbuf, 0, carry_l, t, False)
            edge[...] = jnp.broadcast_to(a1[t - 1:t, :], (SUBLANES, bk))
            hnext = _shift(_window(h1pad, t0, t), 1, t)
            dxc, acc = _lru_grads(lbuf[...], hnext, a1, sq1, rs1, r1, i1, xc, xcb, w_ref, dwacc, 1, big_l[1], acc)
            dxpad[at(t0), :] += dxc
            return carry_l, acc

        _, acc1 = lax.fori_loop(0, n_ch, sweep3, (zero, (zrow, zrow, zrow)))

        def sweep4(ch, st):
            t0 = pl.multiple_of(ch * t, t)
            sdx = _window(dxpad, t0, t)
            su = _window(upad, t0, t)
            dxc = _shift(sdx, 0, t)
            du = cw[0] * _shift(sdx, 2, t) + cw[1] * _shift(sdx, 1, t) + cw[2] * dxc + cw[3] * _shift(sdx, -1, t)
            du_ref[pl.ds(t0, t), :] = du.astype(BF16)
            return tuple(st[k] + _colsum(dxc * _shift(su, k - 2, t)) for k in range(4)) + (st[4] + _colsum(dxc),)

        conv_g = lax.fori_loop(0, n_ch, sweep4, (zrow,) * 5)

        dpk_ref[...] = jnp.zeros_like(dpk_ref)
        rows = list(conv_g) + [acc0[0], acc1[0], acc0[1], acc1[1],
                               acc0[2] * LRU_C * _sigmoid(-lam[0]), acc1[2] * LRU_C * _sigmoid(-lam[1])]
        for k, v in enumerate(rows):
            dpk_ref[pl.ds(k, 1), :] = v
        dw_ref[...] = dwacc[...].astype(BF16)

    u_spec, ug_spec, w_spec, pk_spec, blk = _lru_specs(s, d, bk)
    padded = pltpu.VMEM((s + 2 * PAD, bk), F32)
    chunk = pltpu.VMEM((t, bk), F32)
    res = _host_call(
        "lru_bwd", (n_h,), lambda ins, outs, scr: body(*ins, *outs, *scr), [z, z, dy, gatew, pk],
        [u_spec, ug_spec, blk, w_spec, pk_spec],
        [_sds((s, d), BF16), _sds((s, d), BF16), _sds((n_h, bk, 4 * bk), BF16), _sds((n_h, 16, bk), F32)],
        [blk, blk, w_spec, pk_spec],
        [padded, padded, padded, padded, chunk, chunk, chunk, pltpu.VMEM((bk, 4 * bk), F32),
         pltpu.VMEM((SUBLANES, bk), F32), pltpu.VMEM((s, bk), F32)], list(stages))
    return res if stages else res[0]


def _scalar(v):
    return jnp.reshape(v, (1,)).astype(jnp.int32)


def _add_sibling(g, r, c):
    _, rows, cols = g.shape
    rh = rows // 2
    tr = _tile(rh, 512, 16)
    nr = rh // tr

    def body(c_ref, g_ref, r_ref, o_ref):
        o_ref[...] = (g_ref[...].astype(F32) + r_ref[...].astype(F32)).astype(BF16)

    spec = pl.BlockSpec((None, tr, cols), lambda k, i, c_ref: (k, i, 0))
    return pl.pallas_call(
        body, name="add_sibling", out_shape=_sds((N_CHIP, rh, cols), BF16),
        grid_spec=pltpu.PrefetchScalarGridSpec(
            num_scalar_prefetch=1, grid=(N_CHIP, nr),
            in_specs=[pl.BlockSpec((None, tr, cols), lambda k, i, c_ref: (k, c_ref[0] * nr + i, 0)), spec], out_specs=spec),
        compiler_params=_cparams(("arbitrary", "arbitrary")),
    )(_scalar(c), g, r)


def _add_pair(g, r):
    _, rh, cols = g.shape
    tr = _tile(rh, 512, 16)

    def body(ins, outs, scr):
        outs[0][...] = (ins[0][...].astype(F32) + ins[1][...].astype(F32)).astype(BF16)

    spec = pl.BlockSpec((None, tr, cols), lambda k, i: (k, i, 0))
    return _host_call("add_pair", (N_CHIP, rh // tr), body, [g, r], [spec, spec], [_sds(g.shape, BF16)], [spec], [], [])[0][0]


def _sum_chips(p, rcv, k_me, c):
    _, rh, cols = p.shape
    tr = _tile(rh, 512, 16)
    nr = rh // tr

    def body(kc_ref, p_ref, r_ref, o_ref):
        acc = p_ref[...].astype(F32)
        for j in range(3):
            acc = acc + r_ref[j].astype(F32)
        o_ref[...] = acc

    return pl.pallas_call(
        body, name="sum_chips", out_shape=_sds((2 * rh, cols), F32),
        grid_spec=pltpu.PrefetchScalarGridSpec(
            num_scalar_prefetch=1, grid=(nr,),
            in_specs=[pl.BlockSpec((None, tr, cols), lambda i, kc_ref: (kc_ref[0], i, 0)),
                      pl.BlockSpec((3, tr, cols), lambda i, kc_ref: (0, i, 0))],
            out_specs=pl.BlockSpec((tr, cols), lambda i, kc_ref: (kc_ref[1] * nr + i, 0))),
        compiler_params=_cparams(("arbitrary",)),
    )(jnp.stack([k_me, c]).astype(jnp.int32), p, rcv)


def _sum_devices(g):
    def body(g_ref, o_ref):
        acc = g_ref[0]
        for dev in range(1, N_DEV):
            acc = acc + g_ref[dev]
        o_ref[...] = acc

    return pl.pallas_call(body, name="sum_devices", out_shape=_sds(g.shape[1:], F32))(g)


def _adamw(w, g, m, v):
    rows, cols = w.shape
    tr = _tile(rows, 256, SUBLANES)

    def body(ins, outs, scr):
        w_ref, g_ref, m_ref, v_ref = ins
        go_ref, d_ref, nm_ref, nv_ref = outs
        gv = g_ref[...]
        go_ref[...] = gv
        nm = ADAM_B1 * m_ref[...] + (1.0 - ADAM_B1) * gv
        nv = ADAM_B2 * v_ref[...] + (1.0 - ADAM_B2) * (gv * gv)
        m_hat = nm / (1.0 - ADAM_B1 ** ADAM_STEP)
        v_hat = nv / (1.0 - ADAM_B2 ** ADAM_STEP)
        d_ref[...] = -ADAM_LR * (m_hat / (jnp.sqrt(v_hat) + ADAM_EPS) + ADAM_WD * w_ref[...])
        nm_ref[...] = nm
        nv_ref[...] = nv

    spec = pl.BlockSpec((tr, cols), lambda i: (i, 0))
    return _host_call("adamw", (rows // tr,), body, [w, g, m, v], [spec] * 4, [_sds((rows, cols), F32)] * 4, [spec] * 4, [], [])[0]


def _pack(vs, unit):
    flat = jnp.concatenate([v.reshape(-1).astype(F32) for v in vs])
    pad = (-flat.shape[0]) % unit
    if pad:
        flat = jnp.concatenate([flat, jnp.zeros((pad,), F32)])
    return flat.reshape(-1, 128)


def _unpack(p, like):
    flat = p.reshape(-1)
    out, off = [], 0
    for v in like:
        n = math.prod(v.shape)
        out.append(flat[off:off + n].reshape(v.shape))
        off += n
    return out


def kernel(x, w_in, pool_w, pool_scale, conv_w, conv_b, lru_wa, lru_ba, lru_wx, lru_bx, lru_lambda, w_pool_up, w_lru_up, w_out, b_out, ln1_g, ln1_b, w_ff1, b_ff1, w_ff2, b_ff2, ln2_g, ln2_b, loss_target, m_w_in, m_pool_w, m_pool_scale, m_conv_w, m_conv_b, m_lru_wa, m_lru_ba, m_lru_wx, m_lru_bx, m_lru_lambda, m_w_pool_up, m_w_lru_up, m_w_out, m_b_out, m_ln1_g, m_ln1_b, m_w_ff1, m_b_ff1, m_w_ff2, m_b_ff2, m_ln2_g, m_ln2_b, v_w_in, v_pool_w, v_pool_scale, v_conv_w, v_conv_b, v_lru_wa, v_lru_ba, v_lru_wx, v_lru_bx, v_lru_lambda, v_w_pool_up, v_w_lru_up, v_w_out, v_b_out, v_ln1_g, v_ln1_b, v_w_ff1, v_b_ff1, v_w_ff2, v_b_ff2, v_ln2_g, v_ln2_b):
    given = dict(locals())
    wt = {n: given[n] for n in WEIGHTS}
    mom = {n: given["m_" + n] for n in WEIGHTS}
    vel = {n: given["v_" + n] for n in WEIGHTS}

    ix, iy, ic = _mesh_pos()
    k_me = 2 * ix + iy
    s, d = x.shape[1], x.shape[2]
    ds = d // N_CHIP
    n_g, pgs, pg = pool_w.shape[1], pool_w.shape[2], pool_w.shape[3]
    n_h, bks, bk = lru_wa.shape[2], lru_wa.shape[3], lru_wa.shape[4]
    f = b_ff1.shape[1]
    x2 = x[0]
    vec = lambda a: a.reshape(1, -1)

    sharded_vecs = [conv_w[0], lru_ba[0], lru_bx[0], lru_lambda[0]]
    rows_sv = jnp.concatenate(sharded_vecs + [jnp.zeros((6, ds), F32)], axis=0)
    sv = _all_gather_small(rows_sv)
    sv = sv.reshape(N_CHIP, 2, 16, ds)[:, 0].transpose(1, 0, 2).reshape(16, d)
    conv_w_f, ba_f, bx_f, lam_f = sv[0:4], sv[4:6], sv[6:8], sv[8:10]
    pk = jnp.concatenate([conv_w_f, conv_b, ba_f, bx_f, lam_f, jnp.zeros((5, d), F32)], axis=0)
    pk = pk.reshape(16, n_h, bk).transpose(1, 0, 2)

    def gate_stack(wa, wx):
        return jnp.stack([wa[0], wx[0]], axis=1)

    mats = {
        "w_in": w_in[0], "w_pool_up": w_pool_up[0], "w_lru_up": w_lru_up[0], "w_out": w_out[0],
        "w_ff1": w_ff1[0], "w_ff2": w_ff2[0],
        "pool_w": pool_w[0].reshape(n_g * pgs, pg),
        "gate_w": gate_stack(lru_wa, lru_wx).reshape(4 * n_h * bks, bk),
    }
    names = list(mats)
    placed = {n: _cast_place(mats[n], k_me) for n in names}

    def add_sibling(gs, swapped):
        return [_add_sibling(g, r, ic) for g, r in zip(gs, swapped)]

    def sum_chips(ps, received):
        return [_sum_chips(p, r, k_me, ic) for p, r in zip(ps, received)]

    first = [placed[n] for n in ("w_in", "pool_w", "gate_w")]
    (bufs,) = _run_stages("gather_first", [_chain([
        _gather_direct(first), _together([_gather_relay(first), _gather_d2d(first, (0, 1))]), _gather_d2d(first, (2,))])])
    wg_in = bufs[0]
    wf_pool = bufs[1].reshape(N_CHIP, n_g, pgs, pg).transpose(1, 0, 2, 3).reshape(n_g, pg, pg)
    wf_gate = bufs[2].reshape(N_CHIP, 2, 2, n_h, bks, bk).transpose(3, 0, 4, 1, 2, 5).reshape(n_h, bk, 4 * bk)

    (z, x_bf), (wb_mix, wb_ff1) = _fwd_in(x2, wg_in, stages=[
        _gather_direct([placed[n] for n in ("w_pool_up", "w_lru_up", "w_out")]), _gather_direct([placed["w_ff1"]])])
    (d_pool, y_pool), (wb_mix,) = _pool_fwd(z, wf_pool, pool_scale, stages=[_gather_relay(wb_mix)])
    y_lru, (wb_ff1, wb_mix, wb_ff2) = _lru_fwd(z, wf_gate, pk, stages=[
        _gather_relay(wb_ff1), _gather_d2d(wb_mix), _gather_direct([placed["w_ff2"]])])
    wf_pu, wf_lu, wf_out = (b.reshape(d, d) for b in wb_mix)
    (m_mix, p_a, p_b), (wb_ff1, wb_ff2) = _fwd_merge(y_pool, y_lru, wf_pu, wf_lu, z, stages=[
        _gather_d2d(wb_ff1), _gather_relay(wb_ff2)])
    wg_ff1 = wb_ff1[0]
    (xhat1, x1_bf, rstd1), (wb_ff2,) = _fwd_out_ln1(m_mix, wf_out, x2, b_out, ln1_g, ln1_b, stages=[_gather_d2d(wb_ff2)])
    hdn, d_hdn = _fwd_ff1(x1_bf, wg_ff1, b_ff1)
    wf_ff2 = wb_ff2[0].reshape(f, d)
    dr2, dr2_bf, g_ln2_g, g_ln2_b, g_b_ff2, loss_part = _fwd_ff2_ln2_loss(
        hdn, wf_ff2, xhat1, ln1_g, ln1_b, b_ff2, ln2_g, ln2_b, loss_target[0])

    dpre, g_b_ff1 = _bwd_ff2_in(dr2_bf, wf_ff2, d_hdn)
    g_ff = [_wgrad("wgrad_ff1", x1_bf, dpre, True), _wgrad("wgrad_ff2", hdn, dr2_bf, False)]
    (dr1, dr1_bf, g_ln1_g, g_ln1_b, g_b_out), (swapped,) = _bwd_ff1_in_ln1(
        dpre, wg_ff1, dr2, xhat1, rstd1, ln1_g, stages=[_swap_halves(g_ff)])
    sums_ff = add_sibling(g_ff, swapped)
    dp_a, dp_b, dg_a, dg_b = _bwd_out_in(dr1_bf, wf_out, z, p_a, p_b)
    dy_pool = _bwd_up_in("bwd_pool_up_in", dp_a, wf_pu)
    dy_lru = _bwd_up_in("bwd_lru_up_in", dp_b, wf_lu)
    g_mix = [_wgrad("wgrad_pool_up", y_pool, dp_a, False), _wgrad("wgrad_lru_up", y_lru, dp_b, False),
             _wgrad("wgrad_out", m_mix, dr1_bf, False)]
    (du_pool, g_pool_w, g_pool_scale), (swapped,) = _pool_bwd(
        d_pool, dy_pool, wf_pool, pool_scale, stages=[_swap_halves(g_mix)])
    sums_mix = add_sibling(g_mix, swapped)
    (du_lru, du_gate, g_gate_w, g_pk), (recv_ff, recv_mix) = _lru_bwd(
        z, dy_lru, wf_gate, pk, stages=[_scatter_chips(sums_ff), _scatter_chips(sums_mix)])
    halves = sum_chips(sums_ff + sums_mix, recv_ff + recv_mix)
    g_small = [g_pool_w.reshape(n_g, N_CHIP, pgs, pg).transpose(1, 0, 2, 3).reshape(N_CHIP, n_g * pgs, pg),
               g_gate_w.reshape(n_h, N_CHIP, bks, 2, 2, bk).transpose(1, 3, 4, 0, 2, 5).reshape(N_CHIP, 4 * n_h * bks, bk)]
    dz = jnp.concatenate([du_pool, du_lru, du_gate, dg_a, dg_b], axis=1)
    g_other, (joined, swapped) = _wgrad_rows_half(
        "wgrad_in_other", x_bf, dz, 1 - ic, [_join_halves(halves), _swap_halves(g_small)])
    g_mat = dict(zip(["w_ff1", "w_ff2", "w_pool_up", "w_lru_up", "w_out"], joined))
    sums_small = add_sibling(g_small, swapped)
    g_mine, (from_sibling, recv_small) = _wgrad_rows_half(
        "wgrad_in_mine", x_bf, dz, ic, [_send_to_sibling([g_other]), _scatter_chips(sums_small)])
    sums_in = [_add_pair(g_mine, from_sibling[0])]

    def stacked(tree):
        return gate_stack(tree["lru_wa"], tree["lru_wx"]).reshape(4 * n_h * bks, bk)

    res = {}

    def update(n):
        if n == "gate_w":
            outs = [o.reshape(2, 2, n_h, bks, bk) for o in _adamw(stacked(wt), g_mat[n], stacked(mom), stacked(vel))]
            res["lru_wa"] = [o[:, 0][None] for o in outs]
            res["lru_wx"] = [o[:, 1][None] for o in outs]
        else:
            shp2 = mats[n].shape
            outs = _adamw(wt[n].reshape(shp2), g_mat[n], mom[n].reshape(shp2), vel[n].reshape(shp2))
            res[n] = [o.reshape(wt[n].shape) for o in outs]

    grad_x, (recv_in,) = _bwd_in(dz, wg_in, dr1, stages=[_scatter_chips(sums_in)])
    halves = sum_chips(sums_small + sums_in, recv_small + recv_in)
    (joined,) = _run_stages("join_last", [_join_halves(halves)])
    g_mat.update(zip(["pool_w", "gate_w", "w_in"], joined))
    for n in names:
        update(n)

    g_pk = g_pk.transpose(1, 0, 2).reshape(16, d)
    vec_full = {
        "pool_scale": g_pool_scale, "conv_w": g_pk[0:4], "conv_b": g_pk[4:5],
        "lru_ba": g_pk[5:7], "lru_bx": g_pk[7:9], "lru_lambda": g_pk[9:11],
        "b_out": g_b_out, "ln1_g": g_ln1_g, "ln1_b": g_ln1_b, "b_ff1": g_b_ff1, "b_ff2": g_b_ff2,
        "ln2_g": g_ln2_g, "ln2_b": g_ln2_b,
    }
    vnames = list(vec_full)
    vg = _sum_devices(_all_gather_small(_pack([vec_full[n] for n in vnames], 1024)))
    vg = dict(zip(vnames, _unpack(vg, [vec_full[n] for n in vnames])))
    for n in ("conv_w", "lru_ba", "lru_bx", "lru_lambda"):
        vg[n] = lax.dynamic_slice_in_dim(vg[n], k_me * ds, ds, axis=1)
    vg = {n: vg[n].reshape(wt[n].shape) for n in vnames}
    upd = _adamw(_pack([wt[n] for n in vnames], 1024), _pack([vg[n] for n in vnames], 1024),
                 _pack([mom[n] for n in vnames], 1024), _pack([vel[n] for n in vnames], 1024))
    upd = [_unpack(u, [wt[n] for n in vnames]) for u in upd]
    for i, n in enumerate(vnames):
        res[n] = [vg[n], upd[1][i], upd[2][i], upd[3][i]]

    loss = lax.psum(loss_part[0, 0], ("x", "y", "c"))
    return (loss, grad_x[None], *[res[n][0] for n in WEIGHTS], *[res[n][1] for n in WEIGHTS],
            *[res[n][2] for n in WEIGHTS], *[res[n][3] for n in WEIGHTS])
```

```python
import functools
import math

import jax
import jax.numpy as jnp
from jax import lax
from jax.experimental import pallas as pl
from jax.experimental.pallas import tpu as pltpu

F32 = jnp.float32
BF16 = jnp.bfloat16
MESH = pl.DeviceIdType.MESH
ANY = pl.BlockSpec(memory_space=pl.ANY)

N_CHIP = 4
N_DEV = 8
VMEM_LIMIT_BYTES = 56 * 1024 * 1024
SUBLANES = 8
PAD = 8
SCAN_UNROLL = 8

POOL_WINDOWS = (2, 4, 8, 16)
LRU_C = 8.0
DN_ALPHA = 2.0 ** 0.25
LN_EPS = 1e-5
ADAM_LR, ADAM_B1, ADAM_B2, ADAM_EPS, ADAM_WD, ADAM_STEP = 0.001, 0.9, 0.999, 1e-08, 0.01, 10

WEIGHTS = ("w_in", "pool_w", "pool_scale", "conv_w", "conv_b", "lru_wa", "lru_ba", "lru_wx", "lru_bx", "lru_lambda",
           "w_pool_up", "w_lru_up", "w_out", "b_out", "ln1_g", "ln1_b", "w_ff1", "b_ff1", "w_ff2", "b_ff2", "ln2_g", "ln2_b")


def _cparams(sem=None):
    return pltpu.CompilerParams(dimension_semantics=sem, vmem_limit_bytes=VMEM_LIMIT_BYTES)


def _tile(dim, pref, unit=128):
    if dim <= pref:
        return dim
    t = (pref // unit) * unit
    while t > unit and dim % t:
        t -= unit
    assert dim % t == 0, (dim, pref)
    return t


def _mesh_pos():
    x, y, c = lax.axis_index("x"), lax.axis_index("y"), lax.axis_index("c")
    return x, y, c


def _other_chips(x, y):
    return [(1 - x, y), (x, 1 - y), (1 - x, 1 - y)]


def _all_gather_small(v):
    m_per, n = v.shape

    def body(x_ref, out_ref, send_sems, recv_sems, local_sem):
        x, y, c = _mesh_pos()
        me, sibling = (x, y, c), (x, y, 1 - c)
        chips = _other_chips(x, y)

        def rows(px, py, pc):
            return out_ref.at[4 * px + 2 * py + pc]

        def copy(k, block, to, src=None):
            return pltpu.make_async_remote_copy(
                src_ref=rows(*block) if src is None else src, dst_ref=rows(*block),
                send_sem=send_sems.at[k], recv_sem=recv_sems.at[k], device_id=to, device_id_type=MESH)

        mine = pltpu.make_async_copy(x_ref, rows(*me), local_sem)
        mine.start()
        first = [copy(0, me, sibling, src=x_ref)]
        first += [copy(1 + j, me, (*chip, c), src=x_ref) for j, chip in enumerate(chips)]
        for cp in first:
            cp.start()
        passed = [copy(4 + j, (*chip, c), sibling) for j, chip in enumerate(chips)]
        for j, chip in enumerate(chips):
            copy(1 + j, (*chip, c), me).wait_recv()
            passed[j].start()
        copy(0, sibling, me).wait_recv()
        for j, chip in enumerate(chips):
            copy(4 + j, (*chip, 1 - c), me).wait_recv()
        for cp in first + passed:
            cp.wait_send()
        mine.wait()

    return pl.pallas_call(
        body, name="all_gather_small",
        out_shape=jax.ShapeDtypeStruct((N_DEV, m_per, n), v.dtype),
        in_specs=[pl.BlockSpec(memory_space=pltpu.VMEM)],
        out_specs=pl.BlockSpec(memory_space=pltpu.VMEM),
        scratch_shapes=[pltpu.SemaphoreType.DMA((7,)), pltpu.SemaphoreType.DMA((7,)), pltpu.SemaphoreType.DMA],
    )(v)


class _Stage:
    def __init__(self, srcs, bufs, news, n_sems, copies):
        self.srcs, self.bufs, self.news, self.n_sems, self.copies = list(srcs), list(bufs), list(news), n_sems, copies
        self.phases = [(copies, 0)]


class _SemsFrom:
    def __init__(self, ref, offset):
        self.ref, self.offset, self.at = ref, offset, self

    def __getitem__(self, s):
        return self.ref.at[self.offset + s]


def _chain(stages):
    chained = _Stage([], stages[0].bufs, [], sum(st.n_sems for st in stages), None)
    chained.phases, first = [], 0
    for st in stages:
        chained.phases.append((st.copies, first))
        first += st.n_sems
    return chained


def _remote(src, dst, send_sems, recv_sems, s, to):
    return pltpu.make_async_remote_copy(src_ref=src, dst_ref=dst, send_sem=send_sems.at[s], recv_sem=recv_sems.at[s],
                                        device_id=to, device_id_type=MESH)


def _stage_operands(stages, n_in, n_out):
    ins, outs, aliases, scratch = [], [], {}, []
    for st in stages:
        for i in range(len(st.bufs)):
            aliases[n_in + len(ins) + len(st.srcs) + i] = n_out + len(outs) + i
        ins += st.srcs + st.bufs
        outs += [jax.ShapeDtypeStruct(b.shape, b.dtype) for b in st.bufs] + st.news
        scratch += [pltpu.SemaphoreType.DMA((st.n_sems,)), pltpu.SemaphoreType.DMA((st.n_sems,))]
    return ins, outs, aliases, scratch


def _stage_refs(stages, in_refs, out_refs, sem_refs):
    parts, i, o = [], 0, 0
    for n, st in enumerate(stages):
        src = in_refs[i:i + len(st.srcs)]
        i += len(st.srcs) + len(st.bufs)
        buf = out_refs[o:o + len(st.bufs)]
        new = out_refs[o + len(st.bufs):o + len(st.bufs) + len(st.news)]
        o += len(st.bufs) + len(st.news)
        parts.append((src, buf, new, sem_refs[2 * n], sem_refs[2 * n + 1]))
    return parts


def _stage_results(stages, res):
    out, o = [], 0
    for st in stages:
        n = len(st.bufs) + len(st.news)
        out.append(list(res[o:o + n]))
        o += n
    return out


def _phase_copies(st, part, p):
    src, buf, new, send_sems, recv_sems = part
    copies, first = st.phases[p]
    return copies(src, buf, new, _SemsFrom(send_sems, first), _SemsFrom(recv_sems, first))


def _phase_wait(st, part, p):
    started, landing = _phase_copies(st, part, p)
    for cp in landing:
        cp.wait_recv()
    for cp in started:
        cp.wait_send()


def _phase_begin(st, part, p):
    if p:
        _phase_wait(st, part, p - 1)
    for cp in _phase_copies(st, part, p)[0]:
        cp.start()


def _run_stages(name, stages):
    ins, outs, aliases, scratch = _stage_operands(stages, 0, 0)

    def body(*refs):
        parts = _stage_refs(stages, refs[:len(ins)], refs[len(ins):len(ins) + len(outs)], refs[len(ins) + len(outs):])
        for st, part in zip(stages, parts):
            for p in range(len(st.phases)):
                _phase_begin(st, part, p)
            _phase_wait(st, part, len(st.phases) - 1)

    res = pl.pallas_call(
        body, name=name, out_shape=outs, in_specs=[ANY] * len(ins), out_specs=[ANY] * len(outs),
        input_output_aliases=aliases, scratch_shapes=scratch)(*ins)
    return _stage_results(stages, res)


def _gather_direct(ts):
    def copies(src, buf, new, send_sems, recv_sems):
        x, y, c = _mesh_pos()
        started, landing = [], []
        for t in range(len(ts)):
            rh = ts[t].shape[1] // 2
            rows = pl.ds(c * rh, rh)
            mine = buf[t].at[2 * x + y, rows]
            for j, chip in enumerate(_other_chips(x, y)[:2]):
                theirs = buf[t].at[2 * chip[0] + chip[1], rows]
                started.append(_remote(mine, mine, send_sems, recv_sems, 2 * t + j, (*chip, c)))
                landing.append(_remote(theirs, theirs, send_sems, recv_sems, 2 * t + j, (x, y, c)))
        return started, landing

    return _Stage([], ts, [], 2 * len(ts), copies)


def _gather_relay(ts):
    def copies(src, buf, new, send_sems, recv_sems):
        x, y, c = _mesh_pos()
        (x_nb, y_nb, diag) = _other_chips(x, y)
        block = lambda chip: 2 * chip[0] + chip[1]
        started, landing = [], []
        for t in range(len(ts)):
            rq = ts[t].shape[1] // 4
            q0, q1 = pl.ds(2 * c * rq, rq), pl.ds((2 * c + 1) * rq, rq)
            from_y, from_x = buf[t].at[block(y_nb), q0], buf[t].at[block(x_nb), q1]
            started.append(_remote(from_y, from_y, send_sems, recv_sems, 2 * t, (*x_nb, c)))
            started.append(_remote(from_x, from_x, send_sems, recv_sems, 2 * t + 1, (*y_nb, c)))
            for j, q in enumerate((q0, q1)):
                lands = buf[t].at[block(diag), q]
                landing.append(_remote(lands, lands, send_sems, recv_sems, 2 * t + j, (x, y, c)))
        return started, landing

    return _Stage([], ts, [], 2 * len(ts), copies)


def _together(stages):
    def copies(src, buf, new, send_sems, recv_sems):
        started, landing, first = [], [], 0
        for st in stages:
            more = st.copies(src, buf, new, _SemsFrom(send_sems, first), _SemsFrom(recv_sems, first))
            started, landing, first = started + more[0], landing + more[1], first + st.n_sems
        return started, landing

    return _Stage([], stages[0].bufs, [], sum(st.n_sems for st in stages), copies)


def _gather_d2d(ts, which=(0, 1, 2)):
    def copies(src, buf, new, send_sems, recv_sems):
        x, y, c = _mesh_pos()
        started, landing = [], []
        for t in range(len(ts)):
            rh = ts[t].shape[1] // 2
            for j, chip in enumerate(_other_chips(x, y)):
                if j not in which:
                    continue
                got = buf[t].at[2 * chip[0] + chip[1], pl.ds(c * rh, rh)]
                other = buf[t].at[2 * chip[0] + chip[1], pl.ds((1 - c) * rh, rh)]
                started.append(_remote(got, got, send_sems, recv_sems, 3 * t + j, (x, y, 1 - c)))
                landing.append(_remote(other, other, send_sems, recv_sems, 3 * t + j, (x, y, c)))
        return started, landing

    return _Stage([], ts, [], 3 * len(ts), copies)


def _swap_halves(gs):
    def copies(src, buf, new, send_sems, recv_sems):
        x, y, c = _mesh_pos()
        started, landing = [], []
        for t in range(len(gs)):
            rh = gs[t].shape[1] // 2
            started.append(_remote(src[t].at[:, pl.ds((1 - c) * rh, rh)], new[t], send_sems, recv_sems, t, (x, y, 1 - c)))
            landing.append(_remote(new[t], new[t], send_sems, recv_sems, t, (x, y, c)))
        return started, landing

    news = [jax.ShapeDtypeStruct((g.shape[0], g.shape[1] // 2, g.shape[2]), g.dtype) for g in gs]
    return _Stage(gs, [], news, len(gs), copies)


def _send_to_sibling(gs):
    def copies(src, buf, new, send_sems, recv_sems):
        x, y, c = _mesh_pos()
        started = [_remote(src[t], new[t], send_sems, recv_sems, t, (x, y, 1 - c)) for t in range(len(gs))]
        landing = [_remote(new[t], new[t], send_sems, recv_sems, t, (x, y, c)) for t in range(len(gs))]
        return started, landing

    return _Stage(gs, [], [jax.ShapeDtypeStruct(g.shape, g.dtype) for g in gs], len(gs), copies)


def _scatter_chips(ps):
    def copies(src, buf, new, send_sems, recv_sems):
        x, y, c = _mesh_pos()
        started, landing = [], []
        for t in range(len(ps)):
            for j, chip in enumerate(_other_chips(x, y)):
                started.append(_remote(src[t].at[2 * chip[0] + chip[1]], new[t].at[j], send_sems, recv_sems, 3 * t + j, (*chip, c)))
                landing.append(_remote(new[t].at[j], new[t].at[j], send_sems, recv_sems, 3 * t + j, (x, y, c)))
        return started, landing

    return _Stage(ps, [], [jax.ShapeDtypeStruct((3,) + p.shape[1:], p.dtype) for p in ps], 3 * len(ps), copies)


def _join_halves(fs):
    def copies(src, buf, new, send_sems, recv_sems):
        x, y, c = _mesh_pos()
        started, landing = [], []
        for t in range(len(fs)):
            rh = fs[t].shape[0] // 2
            mine = buf[t].at[pl.ds(c * rh, rh)]
            theirs = buf[t].at[pl.ds((1 - c) * rh, rh)]
            started.append(_remote(mine, mine, send_sems, recv_sems, t, (x, y, 1 - c)))
            landing.append(_remote(theirs, theirs, send_sems, recv_sems, t, (x, y, c)))
        return started, landing

    return _Stage([], fs, [], len(fs), copies)


def _cast_place(w, k_me):
    rows, cols = w.shape
    tr = _tile(rows, 512, 16)

    def body(k_ref, w_ref, o_ref):
        o_ref[...] = w_ref[...].astype(BF16)

    return pl.pallas_call(
        body, name="cast_place", out_shape=_sds((N_CHIP, rows, cols), BF16),
        grid_spec=pltpu.PrefetchScalarGridSpec(
            num_scalar_prefetch=1, grid=(rows // tr,),
            in_specs=[pl.BlockSpec((tr, cols), lambda i, k_ref: (i, 0))],
            out_specs=pl.BlockSpec((None, tr, cols), lambda i, k_ref: (k_ref[0], i, 0))),
        compiler_params=_cparams(("arbitrary",)),
    )(_scalar(k_me), w)


_DIMS = {"nn": (((1,), (0,)), ((), ())), "nt": (((1,), (1,)), ((), ())), "tn": (((0,), (0,)), ((), ()))}


def _accum(ref, val, first):
    @pl.when(first)
    def _():
        ref[...] = val

    @pl.when(jnp.logical_not(first))
    def _():
        ref[...] += val


def _grid_step(grid):
    step = pl.program_id(0)
    for ax in range(1, len(grid)):
        step = step * grid[ax] + pl.program_id(ax)
    return step


def _host_call(name, grid, body, operands, in_specs, out_shape, out_specs, scratch, stages, prefetch=None):
    s_ins, s_outs, aliases, s_scratch = _stage_operands(stages, len(operands), len(out_shape))
    n_in, n_out, n_scr = len(operands), len(out_shape), len(scratch)
    n_pre = 0 if prefetch is None else 1

    def full_body(*refs):
        refs = refs[n_pre:]
        in_refs = refs[:n_in]
        s_in_refs = refs[n_in:n_in + len(s_ins)]
        o0 = n_in + len(s_ins)
        out_refs = refs[o0:o0 + n_out]
        s_out_refs = refs[o0 + n_out:o0 + n_out + len(s_outs)]
        c0 = o0 + n_out + len(s_outs)
        scr_refs = refs[c0:c0 + n_scr]
        if stages:
            parts = _stage_refs(stages, s_in_refs, s_out_refs, refs[c0 + n_scr:])
            step, n_steps = _grid_step(grid), math.prod(grid)
            for st, part in zip(stages, parts):
                for p in range(len(st.phases)):
                    pl.when(step == (p * n_steps) // len(st.phases))(functools.partial(_phase_begin, st, part, p))
        body(in_refs, out_refs, scr_refs)
        if stages:
            for st, part in zip(stages, parts):
                pl.when(step == n_steps - 1)(functools.partial(_phase_wait, st, part, len(st.phases) - 1))

    all_in = list(in_specs) + [ANY] * len(s_ins)
    all_out = list(out_specs) + [ANY] * len(s_outs)
    all_scratch = list(scratch) + s_scratch
    params = _cparams(("arbitrary",) * len(grid))
    if prefetch is None:
        res = pl.pallas_call(
            full_body, name=name, grid=grid, in_specs=all_in, out_specs=all_out, out_shape=list(out_shape) + s_outs,
            input_output_aliases=aliases, scratch_shapes=all_scratch, compiler_params=params,
        )(*operands, *s_ins)
    else:
        res = pl.pallas_call(
            full_body, name=name, out_shape=list(out_shape) + s_outs,
            grid_spec=pltpu.PrefetchScalarGridSpec(num_scalar_prefetch=1, grid=grid, in_specs=all_in, out_specs=all_out,
                                                   scratch_shapes=all_scratch),
            input_output_aliases={i + 1: o for i, o in aliases.items()}, compiler_params=params,
        )(prefetch, *operands, *s_ins)
    return list(res[:n_out]), _stage_results(stages, res[n_out:])


def _matmul(name, grid, pairs, extras, outs, acc_shape, epilogue, stages=(), prefetch=None, lhs_to_epilogue=False):
    n_p = len(pairs)
    n_k = grid[-1]
    dims = [_DIMS[p[4]] for p in pairs]

    def body(in_refs, out, accs):
        ab, ex = in_refs[:2 * n_p], in_refs[2 * n_p:]
        if lhs_to_epilogue:
            ex = [ab[0]] + list(ex)
        ids = [pl.program_id(ax) for ax in range(len(grid))]
        k = ids[-1]

        def dot(p):
            return lax.dot_general(ab[2 * p][...].astype(BF16), ab[2 * p + 1][...].astype(BF16), dims[p],
                                   preferred_element_type=F32)

        if n_k == 1:
            epilogue([dot(p) for p in range(n_p)], ex, out, ids)
            return

        @pl.when(k == 0)
        def _():
            for acc in accs:
                acc[...] = jnp.zeros_like(acc)

        for p in range(n_p):
            accs[p][...] += dot(p)

        @pl.when(k == n_k - 1)
        def _():
            epilogue([acc[...] for acc in accs], ex, out, ids)

    in_specs = []
    operands = []
    for a, a_spec, b, b_spec, _ in pairs:
        in_specs += [a_spec, b_spec]
        operands += [a, b]
    for e, e_spec in extras:
        in_specs.append(e_spec)
        operands.append(e)
    res, stage_res = _host_call(name, grid, body, operands, in_specs, [o[0] for o in outs], [o[1] for o in outs],
                                [pltpu.VMEM(acc_shape, F32) for _ in pairs] if n_k > 1 else [], list(stages), prefetch)
    return (res, stage_res) if stages else res


def _out(res, stages, single=False):
    outs = res[0] if stages else res
    outs = outs[0] if single else outs
    return (outs, res[1]) if stages else outs


def _sds(shape, dtype):
    return jax.ShapeDtypeStruct(shape, dtype)


def _row(n):
    return pl.BlockSpec((1, n), lambda *_: (0, 0))


def _layer_norm(r):
    mu = jnp.mean(r, axis=-1, keepdims=True)
    xc = r - mu
    var = jnp.mean(xc * xc, axis=-1, keepdims=True)
    rstd = lax.rsqrt(var + LN_EPS)
    return xc * rstd, rstd


def _layer_norm_bwd(dxhat, xhat, rstd):
    m1 = jnp.mean(dxhat, axis=-1, keepdims=True)
    m2 = jnp.mean(dxhat * xhat, axis=-1, keepdims=True)
    return rstd * (dxhat - m1 - xhat * m2)


def _colsum(v):
    return jnp.sum(v, axis=0, keepdims=True)


ROW_TILE = 256


def _rows_call(name, s, epi, ins, outs):
    tr = _tile(s, ROW_TILE, SUBLANES)

    def spec(shape, kind):
        n = shape[1]
        return pl.BlockSpec((tr, n), lambda i: (i, 0)) if kind == "tile" else pl.BlockSpec((1, n), lambda i: (0, 0))

    def body(in_refs, out_refs, scr):
        epi([in_refs[0][...]], in_refs[1:], out_refs, [pl.program_id(0)])

    return _host_call(name, (s // tr,), body, [a for a, _ in ins], [spec(a.shape, k) for a, k in ins],
                      [o for o, _ in outs], [spec(o.shape, k) for o, k in outs], [], [])[0]


def _plain_matmul(name, a, b, mode):
    m, k_dim = a.shape
    n = b.shape[1]
    tm, tn, tk = _tile(m, 1024), _tile(n, 1024), _tile(k_dim, 2048)

    def epi(accs, ex, out, ids):
        out[0][...] = accs[0]

    assert mode == "nn"
    return _matmul(
        name, (m // tm, n // tn, k_dim // tk),
        [(a, pl.BlockSpec((tm, tk), lambda i, j, k: (i, k)), b, pl.BlockSpec((tk, tn), lambda i, j, k: (k, j)), "nn")],
        [], [(_sds((m, n), F32), pl.BlockSpec((tm, tn), lambda i, j, k: (i, j)))], (tm, tn), epi)[0]


def _fwd_in(x_in, wg_in, stages=()):
    s, d = x_in.shape
    inc = wg_in.shape[2]
    tm, tn, tk = _tile(s, 1024), _tile(inc, 1280), _tile(d, 2048)
    nb = inc // tn
    assert tk == d

    def epi(accs, ex, out, ids):
        out[0][...] = accs[0].astype(BF16)

        @pl.when(ids[1] == 0)
        def _():
            out[1][...] = ex[0][...].astype(BF16)

    x_spec = pl.BlockSpec((tm, tk), lambda i, j, k: (i, k))
    return _out(_matmul(
        "fwd_in", (s // tm, N_CHIP * nb, d // tk),
        [(x_in, x_spec, wg_in, pl.BlockSpec((None, tk, tn), lambda i, j, k: (j // nb, k, j % nb)), "nn")],
        [],
        [(_sds((s, N_CHIP * inc), BF16), pl.BlockSpec((tm, tn), lambda i, j, k: (i, j))), (_sds((s, d), BF16), x_spec)],
        (tm, tn), epi, stages, lhs_to_epilogue=True), stages)


def _fwd_merge(y_pool, y_lru, w_pu, w_lu, z, stages=()):
    s, d = y_pool.shape
    tm, tn, tk = _tile(s, 1024), _tile(d, 1024), _tile(d, 1024)
    ga0, gb0 = 3 * d // tn, 4 * d // tn

    def epi(accs, ex, out, ids):
        sa = _sigmoid(ex[0][...].astype(F32))
        sb = _sigmoid(ex[1][...].astype(F32))
        out[0][...] = (sa * accs[0] + sb * accs[1]).astype(BF16)
        out[1][...] = accs[0].astype(BF16)
        out[2][...] = accs[1].astype(BF16)

    a_spec = pl.BlockSpec((tm, tk), lambda i, j, k: (i, k))
    b_spec = pl.BlockSpec((tk, tn), lambda i, j, k: (k, j))
    o_spec = pl.BlockSpec((tm, tn), lambda i, j, k: (i, j))
    return _out(_matmul(
        "fwd_merge", (s // tm, d // tn, d // tk),
        [(y_pool, a_spec, w_pu, b_spec, "nn"), (y_lru, a_spec, w_lu, b_spec, "nn")],
        [(z, pl.BlockSpec((tm, tn), lambda i, j, k: (i, ga0 + j))), (z, pl.BlockSpec((tm, tn), lambda i, j, k: (i, gb0 + j)))],
        [(_sds((s, d), BF16), o_spec)] * 3, (tm, tn), epi, stages), stages)


def _fwd_out_ln1(m, w_out, x, b_out, g1, b1, stages=()):
    s, d = x.shape
    tm, tk = _tile(s, 512), _tile(d, 2048)

    def epi(accs, ex, out, ids):
        r = DN_ALPHA * ex[0][...] + accs[0] + ex[1][...]
        xhat, rstd = _layer_norm(r)
        out[0][...] = xhat
        out[1][...] = (xhat * ex[2][...] + ex[3][...]).astype(BF16)
        out[2][...] = rstd

    full = pl.BlockSpec((tm, d), lambda i, j, k: (i, 0))
    return _out(_matmul(
        "fwd_out_ln1", (s // tm, 1, d // tk),
        [(m, pl.BlockSpec((tm, tk), lambda i, j, k: (i, k)), w_out, pl.BlockSpec((tk, d), lambda i, j, k: (k, 0)), "nn")],
        [(x, full), (b_out, _row(d)), (g1, _row(d)), (b1, _row(d))],
        [(_sds((s, d), F32), full), (_sds((s, d), BF16), full), (_sds((s, 1), F32), pl.BlockSpec((tm, 1), lambda i, j, k: (i, 0)))],
        (tm, d), epi, stages), stages)


def _fwd_ff1(x1_bf, wg_ff1, b_ff1, stages=()):
    s, d = x1_bf.shape
    fc = wg_ff1.shape[2]
    tm, tn, tk = _tile(s, 1024), _tile(fc, 1024), _tile(d, 2048)
    nb = fc // tn

    def epi(accs, ex, out, ids):
        p = jnp.maximum(accs[0] + ex[0][...], 0.0)
        out[0][...] = (p * p).astype(BF16)
        out[1][...] = (2.0 * p).astype(BF16)

    o = (_sds((s, N_CHIP * fc), BF16), pl.BlockSpec((tm, tn), lambda i, j, k: (i, j)))
    return _out(_matmul(
        "fwd_ff1", (s // tm, N_CHIP * nb, d // tk),
        [(x1_bf, pl.BlockSpec((tm, tk), lambda i, j, k: (i, k)),
          wg_ff1, pl.BlockSpec((None, tk, tn), lambda i, j, k: (j // nb, k, j % nb)), "nn")],
        [(b_ff1, pl.BlockSpec((1, tn), lambda i, j, k: (0, j)))], [o, o], (tm, tn), epi, stages), stages)


def _fwd_ff2_ln2_loss(hdn, w_ff2, xhat1, g1, b1, b_ff2, g2, b2, target):
    s, f = hdn.shape
    d = xhat1.shape[1]

    def epi(accs, ex, out, ids):
        first = ids[0] == 0
        x1 = ex[0][...] * ex[1][...] + ex[2][...]
        r = DN_ALPHA * x1 + accs[0] + ex[3][...]
        xhat, rstd = _layer_norm(r)
        g2v = ex[4][...]
        err = xhat * g2v + ex[5][...] - ex[6][...]
        part = 0.5 * jnp.sum(jnp.mean(err * err, axis=-1, keepdims=True), axis=0, keepdims=True)
        dy = err * (1.0 / d)
        dr2 = _layer_norm_bwd(dy * g2v, xhat, rstd)
        out[0][...] = dr2
        out[1][...] = dr2.astype(BF16)
        _accum(out[2], _colsum(dy * xhat), first)
        _accum(out[3], _colsum(dy), first)
        _accum(out[4], _colsum(dr2), first)
        _accum(out[5], jnp.broadcast_to(part, (1, 128)), first)

    ff = _plain_matmul("fwd_ff2", hdn, w_ff2, "nn")
    vec = lambda n: (_sds((1, n), F32), "vec")
    return _rows_call(
        "ln2_loss", s, epi,
        [(ff, "tile"), (xhat1, "tile"), (g1, "vec"), (b1, "vec"), (b_ff2, "vec"), (g2, "vec"), (b2, "vec"), (target, "tile")],
        [(_sds((s, d), F32), "tile"), (_sds((s, d), BF16), "tile"), vec(d), vec(d), vec(d), vec(128)])


def _bwd_ff2_in(dr2_bf, w_ff2, hdn, stages=()):
    s, d = dr2_bf.shape
    f = hdn.shape[1]
    tm, tn, tk = _tile(s, 1024), _tile(f, 1024), _tile(d, 2048)

    def epi(accs, ex, out, ids):
        dpre = accs[0] * ex[0][...].astype(F32)
        out[0][...] = dpre.astype(BF16)
        _accum(out[1], _colsum(dpre), ids[1] == 0)

    return _out(_matmul(
        "bwd_ff2_in", (f // tn, s // tm, d // tk),
        [(dr2_bf, pl.BlockSpec((tm, tk), lambda j, i, k: (i, k)), w_ff2, pl.BlockSpec((tn, tk), lambda j, i, k: (j, k)), "nt")],
        [(hdn, pl.BlockSpec((tm, tn), lambda j, i, k: (i, j)))],
        [(_sds((s, f), BF16), pl.BlockSpec((tm, tn), lambda j, i, k: (i, j))), (_sds((1, f), F32), pl.BlockSpec((1, tn), lambda j, i, k: (0, j)))],
        (tm, tn), epi, stages), stages)


def _bwd_ff1_in_ln1(dpre, wg_ff1, dr2, xhat1, rstd1, g1, stages=()):
    s, f = dpre.shape
    d = xhat1.shape[1]
    fc = wg_ff1.shape[2]
    tm, tn, tk = _tile(s, 1024), _tile(d, 1024), _tile(fc, 2048)
    nb = fc // tk

    def epi(accs, ex, out, ids):
        first = ids[0] == 0
        xhat = ex[1][...]
        dx1 = accs[0] + DN_ALPHA * ex[0][...]
        dr1 = _layer_norm_bwd(dx1 * ex[3][...], xhat, ex[2][...])
        out[0][...] = dr1
        out[1][...] = dr1.astype(BF16)
        _accum(out[2], _colsum(dx1 * xhat), first)
        _accum(out[3], _colsum(dx1), first)
        _accum(out[4], _colsum(dr1), first)

    def plain(accs, ex, out, ids):
        out[0][...] = accs[0]

    o_spec = pl.BlockSpec((tm, tn), lambda i, j, k: (i, j))
    mm = _out(_matmul(
        "bwd_ff1_in", (s // tm, d // tn, f // tk),
        [(dpre, pl.BlockSpec((tm, tk), lambda i, j, k: (i, k)),
          wg_ff1, pl.BlockSpec((None, tn, tk), lambda i, j, k: (k // nb, j, k % nb)), "nt")],
        [], [(_sds((s, d), F32), o_spec)], (tm, tn), plain, stages), stages, True)
    mm, stage_res = mm if stages else (mm, None)
    vec = (_sds((1, d), F32), "vec")
    rows = _rows_call(
        "ln1_bwd", s, epi, [(mm, "tile"), (dr2, "tile"), (xhat1, "tile"), (rstd1, "tile"), (g1, "vec")],
        [(_sds((s, d), F32), "tile"), (_sds((s, d), BF16), "tile"), vec, vec, vec])
    return (rows, stage_res) if stages else rows


def _bwd_out_in(dr1_bf, w_out, z, pa, pb, stages=()):
    s, d = dr1_bf.shape
    tm, tn, tk = _tile(s, 1024), _tile(d, 1024), _tile(d, 2048)
    ga0, gb0 = 3 * d // tn, 4 * d // tn

    def epi(accs, ex, out, ids):
        dm = accs[0]
        sa = _sigmoid(ex[0][...].astype(F32))
        sb = _sigmoid(ex[1][...].astype(F32))
        out[0][...] = (dm * sa).astype(BF16)
        out[1][...] = (dm * sb).astype(BF16)
        out[2][...] = (dm * ex[2][...].astype(F32) * sa * (1.0 - sa)).astype(BF16)
        out[3][...] = (dm * ex[3][...].astype(F32) * sb * (1.0 - sb)).astype(BF16)

    o_spec = pl.BlockSpec((tm, tn), lambda i, j, k: (i, j))
    return _out(_matmul(
        "bwd_out_in", (s // tm, d // tn, d // tk),
        [(dr1_bf, pl.BlockSpec((tm, tk), lambda i, j, k: (i, k)), w_out, pl.BlockSpec((tn, tk), lambda i, j, k: (j, k)), "nt")],
        [(z, pl.BlockSpec((tm, tn), lambda i, j, k: (i, ga0 + j))), (z, pl.BlockSpec((tm, tn), lambda i, j, k: (i, gb0 + j))),
         (pa, o_spec), (pb, o_spec)],
        [(_sds((s, d), BF16), o_spec)] * 4, (tm, tn), epi, stages), stages)


def _bwd_up_in(name, dp, w_up, stages=()):
    s, d = dp.shape
    n = w_up.shape[0]
    tm, tn, tk = _tile(s, 1024), _tile(n, 1024), _tile(d, 2048)

    def epi(accs, ex, out, ids):
        out[0][...] = accs[0].astype(BF16)

    return _out(_matmul(
        name, (s // tm, n // tn, d // tk),
        [(dp, pl.BlockSpec((tm, tk), lambda i, j, k: (i, k)), w_up, pl.BlockSpec((tn, tk), lambda i, j, k: (j, k)), "nt")],
        [], [(_sds((s, n), BF16), pl.BlockSpec((tm, tn), lambda i, j, k: (i, j)))], (tm, tn), epi, stages), stages, True)


def _bwd_in(dz, wg_in, dr1, stages=()):
    s, d = dr1.shape
    inc = wg_in.shape[2]
    tm, tn, tk = _tile(s, 1024), _tile(d, 1024), _tile(inc, 2560)
    nb = inc // tk

    def epi(accs, ex, out, ids):
        out[0][...] = accs[0] + DN_ALPHA * ex[0][...]

    o_spec = pl.BlockSpec((tm, tn), lambda i, j, k: (i, j))
    return _out(_matmul(
        "bwd_in", (s // tm, d // tn, N_CHIP * nb),
        [(dz, pl.BlockSpec((tm, tk), lambda i, j, k: (i, k)),
          wg_in, pl.BlockSpec((None, tn, tk), lambda i, j, k: (k // nb, j, k % nb)), "nt")],
        [(dr1, o_spec)], [(_sds((s, d), F32), o_spec)], (tm, tn), epi, stages), stages, True)


def _wgrad(name, a, b, col_sharded, stages=()):
    s, ka = a.shape
    n = b.shape[1]
    tm, tk = _tile(ka, 1024), _tile(s, 2048)
    tn = _tile(n // N_CHIP, 1280) if col_sharded else _tile(n, 1024)

    def epi(accs, ex, out, ids):
        out[0][...] = accs[0].astype(BF16)

    if col_sharded:
        nb = (n // N_CHIP) // tn
        o = (_sds((N_CHIP, ka, n // N_CHIP), BF16), pl.BlockSpec((None, tm, tn), lambda i, j, k: (j // nb, i, j % nb)))
    else:
        o = (_sds((ka, n), BF16), pl.BlockSpec((tm, tn), lambda i, j, k: (i, j)))
    res = _out(_matmul(
        name, (ka // tm, n // tn, s // tk),
        [(a, pl.BlockSpec((tk, tm), lambda i, j, k: (k, i)), b, pl.BlockSpec((tk, tn), lambda i, j, k: (k, j)), "tn")],
        [], [o], (tm, tn), epi, stages), stages, True)
    res, stage_res = res if stages else (res, None)
    res = res if col_sharded else res.reshape(N_CHIP, ka // N_CHIP, n)
    return (res, stage_res) if stages else res


def _wgrad_rows_half(name, a, b, half, stages):
    s, ka = a.shape
    n = b.shape[1]
    kh = ka // 2
    tm, tk, tn = _tile(kh, 1024), _tile(s, 2048), _tile(n // N_CHIP, 1280)
    nb, ni = (n // N_CHIP) // tn, kh // tm

    def epi(accs, ex, out, ids):
        out[0][...] = accs[0].astype(BF16)

    return _out(_matmul(
        name, (ni, n // tn, s // tk),
        [(a, pl.BlockSpec((tk, tm), lambda i, j, k, sel: (k, sel[0] * ni + i)),
          b, pl.BlockSpec((tk, tn), lambda i, j, k, sel: (k, j)), "tn")],
        [], [(_sds((N_CHIP, kh, n // N_CHIP), BF16), pl.BlockSpec((None, tm, tn), lambda i, j, k, sel: (j // nb, i, j % nb)))],
        (tm, tn), epi, stages, _scalar(half)), stages, True)


def _chunk(s):
    return _tile(s, 512, SUBLANES)


def _zero_pads(ref, s):
    zeros = jnp.zeros((PAD, ref.shape[1]), F32)
    ref[pl.ds(0, PAD), :] = zeros
    ref[pl.ds(PAD + s, PAD), :] = zeros


def _window(ref, t0, t):
    return ref[pl.ds(t0, t + 2 * PAD), :]


def _shift(sup, off, t):
    return sup[PAD + off:PAD + off + t, :]


def _pool_count(t0, t, s, w):
    pos = t0 + lax.broadcasted_iota(jnp.int32, (t, 1), 0)
    return (jnp.minimum(pos + w // 2, s) - jnp.maximum(pos - w // 2, 0)).astype(F32)


def _pool_fwd(z, pool_w, pool_scale, stages=()):
    s = z.shape[0]
    n_g, pg = pool_w.shape[0], pool_w.shape[1]
    assert n_g == len(POOL_WINDOWS) and max(POOL_WINDOWS) // 2 <= PAD
    t = _chunk(s)

    def body(u_ref, w_ref, sc_ref, d_ref, y_ref, pad_ref):
        g = pl.program_id(0)
        _zero_pads(pad_ref, s)
        pad_ref[pl.ds(PAD, s), :] = u_ref[...].astype(F32)
        for gi, w in enumerate(POOL_WINDOWS):
            @pl.when(g == gi)
            def _():
                def step(ch, carry):
                    t0 = pl.multiple_of(ch * t, t)
                    sup = _window(pad_ref, t0, t)
                    acc = _shift(sup, -(w // 2), t)
                    for o in range(-(w // 2) + 1, w // 2):
                        acc = acc + _shift(sup, o, t)
                    dd = (acc * (1.0 / _pool_count(t0, t, s, w)) - _shift(sup, 0, t)).astype(BF16)
                    d_ref[pl.ds(t0, t), :] = dd
                    y = jnp.dot(dd, w_ref[...], preferred_element_type=F32) * sc_ref[...]
                    y_ref[pl.ds(t0, t), :] = y.astype(BF16)
                    return carry

                lax.fori_loop(0, s // t, step, 0)

    blk = pl.BlockSpec((s, pg), lambda g: (0, g))
    res = _host_call(
        "pool_fwd", (n_g,), lambda ins, outs, scr: body(*ins, *outs, *scr), [z, pool_w, pool_scale],
        [blk, pl.BlockSpec((None, pg, pg), lambda g: (g, 0, 0)), pl.BlockSpec((1, pg), lambda g: (0, g))],
        [_sds((s, n_g * pg), BF16)] * 2, [blk, blk], [pltpu.VMEM((s + 2 * PAD, pg), F32)], list(stages))
    return res if stages else res[0]


def _pool_bwd(dsv, dy, pool_w, pool_scale, stages=()):
    s = dsv.shape[0]
    n_g, pg = pool_w.shape[0], pool_w.shape[1]
    t = _chunk(s)

    def body(d_ref, dy_ref, w_ref, sc_ref, du_ref, dw_ref, dsc_ref, epad_ref, dwacc_ref):
        g = pl.program_id(0)
        _zero_pads(epad_ref, s)
        dwacc_ref[...] = jnp.zeros_like(dwacc_ref)
        for gi, w in enumerate(POOL_WINDOWS):
            @pl.when(g == gi)
            def _():
                def first(ch, dsc):
                    t0 = pl.multiple_of(ch * t, t)
                    dd = d_ref[pl.ds(t0, t), :]
                    dyc = dy_ref[pl.ds(t0, t), :].astype(F32)
                    wv = w_ref[...]
                    ypre = jnp.dot(dd, wv, preferred_element_type=F32)
                    dq = (dyc * sc_ref[...]).astype(BF16)
                    dwacc_ref[...] += lax.dot_general(dd, dq, _DIMS["tn"], preferred_element_type=F32)
                    ddv = lax.dot_general(dq, wv, _DIMS["nt"], preferred_element_type=F32)
                    epad_ref[pl.ds(pl.multiple_of(PAD + t0, SUBLANES), t), :] = ddv * (1.0 / _pool_count(t0, t, s, w))
                    return dsc + _colsum(dyc * ypre)

                dsc_ref[...] = lax.fori_loop(0, s // t, first, jnp.zeros((1, pg), F32))

                def second(ch, carry):
                    t0 = pl.multiple_of(ch * t, t)
                    sup = _window(epad_ref, t0, t)
                    acc = _shift(sup, -(w // 2) + 1, t)
                    for o in range(-(w // 2) + 2, w // 2 + 1):
                        acc = acc + _shift(sup, o, t)
                    du_ref[pl.ds(t0, t), :] = (acc - _shift(sup, 0, t) * _pool_count(t0, t, s, w)).astype(BF16)
                    return carry

                lax.fori_loop(0, s // t, second, 0)

        dw_ref[...] = dwacc_ref[...].astype(BF16)

    blk = pl.BlockSpec((s, pg), lambda g: (0, g))
    w_spec = pl.BlockSpec((None, pg, pg), lambda g: (g, 0, 0))
    sc_spec = pl.BlockSpec((1, pg), lambda g: (0, g))
    res = _host_call(
        "pool_bwd", (n_g,), lambda ins, outs, scr: body(*ins, *outs, *scr), [dsv, dy, pool_w, pool_scale],
        [blk, blk, w_spec, sc_spec], [_sds((s, n_g * pg), BF16), _sds((n_g, pg, pg), BF16), _sds((1, n_g * pg), F32)],
        [blk, w_spec, sc_spec], [pltpu.VMEM((s + 2 * PAD, pg), F32), pltpu.VMEM((pg, pg), F32)], list(stages))
    return res if stages else res[0]


def _sigmoid(x):
    return 0.5 * jnp.tanh(0.5 * x) + 0.5


def _softplus(x):
    e = jnp.exp(-jnp.abs(x))
    log1p_e = jnp.where(e < 1e-2, e * (1.0 - e * (0.5 - e * (1.0 / 3.0))), jnp.log(1.0 + e))
    return jnp.maximum(x, 0.0) + log1p_e


_GELU_C = math.sqrt(2.0 / math.pi)


def _gelu(x):
    th = jnp.tanh(_GELU_C * (x + 0.044715 * x * x * x))
    return 0.5 * x * (1.0 + th), th


def _gelu_grad(x, th):
    return 0.5 * (1.0 + th) + 0.5 * x * (1.0 - th * th) * _GELU_C * (1.0 + 3.0 * 0.044715 * x * x)


def _scan_chunk(a_ref, b_ref, o_ref, o_off, carry, t, reverse):
    n = a_ref.shape[1]
    row = lax.broadcasted_iota(jnp.int32, (SUBLANES, n), 0)
    n_groups = t // SUBLANES
    unroll = math.gcd(n_groups, SCAN_UNROLL)
    last = 0 if reverse else SUBLANES - 1

    def step(si, carry):
        for u in range(unroll):
            gi = si * unroll + u
            g = n_groups - 1 - gi if reverse else gi
            r0 = pl.multiple_of(g * SUBLANES, SUBLANES)
            a = a_ref[pl.ds(r0, SUBLANES), :]
            b = b_ref[pl.ds(r0, SUBLANES), :]
            for k in (1, 2, 4):
                keep = row < SUBLANES - k if reverse else row >= k
                sh = SUBLANES - k if reverse else k
                ar = jnp.where(keep, pltpu.roll(a, sh, 0), 1.0)
                br = jnp.where(keep, pltpu.roll(b, sh, 0), 0.0)
                b = a * br + b
                a = a * ar
            o_ref[pl.ds(pl.multiple_of(o_off + r0, SUBLANES), SUBLANES), :] = a * carry + b
            carry = (jnp.broadcast_to(a[last:last + 1, :], a.shape) * carry
                     + jnp.broadcast_to(b[last:last + 1, :], b.shape))
        return carry

    return lax.fori_loop(0, n_groups // unroll, step, carry)


def _lru_params(pk_ref):
    rows = pk_ref[...]
    get = lambda i: rows[i:i + 1, :]
    cw = [get(k) for k in range(4)]
    lam = (get(9), get(10))
    big_l = tuple(-LRU_C * _softplus(-v) for v in lam)
    return cw, get(4), (get(5), get(6)), (get(7), get(8)), lam, big_l


def _conv(sup, cw, cb, t):
    xc = cb + cw[0] * _shift(sup, -2, t)
    for k in range(1, 4):
        xc = xc + cw[k] * _shift(sup, k - 2, t)
    return xc


def _gates(xcb, w_ref, d, bk, ba, bx, big_l):
    pre = jnp.dot(xcb, w_ref[:, pl.ds(d * 2 * bk, 2 * bk)], preferred_element_type=F32)
    r = _sigmoid(pre[:, :bk] + ba[d])
    i = _sigmoid(pre[:, bk:] + bx[d])
    la = big_l[d] * r
    a = jnp.exp(la)
    var = jnp.tanh(-la) * (1.0 + a * a)
    rs = lax.rsqrt(jnp.maximum(var, 1e-30))
    return r, i, a, var * rs, rs


def _lru_specs(s, d, bk):
    u_spec = pl.BlockSpec((s, bk), lambda h: (0, d // bk + h))
    ug_spec = pl.BlockSpec((s, bk), lambda h: (0, 2 * d // bk + h))
    w_spec = pl.BlockSpec((None, bk, 4 * bk), lambda h: (h, 0, 0))
    pk_spec = pl.BlockSpec((None, 16, bk), lambda h: (h, 0, 0))
    blk = pl.BlockSpec((s, bk), lambda h: (0, h))
    return u_spec, ug_spec, w_spec, pk_spec, blk


def _lru_fwd(z, gatew, pk, stages=()):
    s = z.shape[0]
    n_h, bk = gatew.shape[0], gatew.shape[1]
    d = n_h * bk
    t = _chunk(s)
    n_ch = s // t

    def body(u_ref, ug_ref, w_ref, pk_ref, y_ref, upad, h0buf, abuf, bbuf, xcbuf, h1buf):
        _zero_pads(upad, s)
        upad[pl.ds(PAD, s), :] = u_ref[...].astype(F32)
        cw, cb, ba, bx, _, big_l = _lru_params(pk_ref)
        zero = jnp.zeros((SUBLANES, bk), F32)

        def fill(xc, dr):
            _, i, a, sq, _ = _gates(xc.astype(BF16), w_ref, dr, bk, ba, bx, big_l)
            abuf[...] = a
            bbuf[...] = sq * i * xc

        def up(ch, carry):
            t0 = pl.multiple_of(ch * t, t)
            xc = _conv(_window(upad, t0, t), cw, cb, t)
            xcbuf[pl.ds(t0, t), :] = xc
            fill(xc, 0)
            return _scan_chunk(abuf, bbuf, h0buf, t0, carry, t, False)

        lax.fori_loop(0, n_ch, up, zero)

        def down(ci, carry):
            t0 = pl.multiple_of((n_ch - 1 - ci) * t, t)
            fill(xcbuf[pl.ds(t0, t), :], 1)
            carry = _scan_chunk(abuf, bbuf, h1buf, 0, carry, t, True)
            gl, _ = _gelu(ug_ref[pl.ds(t0, t), :].astype(F32))
            y_ref[pl.ds(t0, t), :] = ((h0buf[pl.ds(t0, t), :] + h1buf[...]) * gl).astype(BF16)
            return carry

        lax.fori_loop(0, n_ch, down, zero)

    u_spec, ug_spec, w_spec, pk_spec, blk = _lru_specs(s, d, bk)
    res = _host_call(
        "lru_fwd", (n_h,), lambda ins, outs, scr: body(*ins, *outs, *scr), [z, z, gatew, pk],
        [u_spec, ug_spec, w_spec, pk_spec], [_sds((s, d), BF16)], [blk],
        [pltpu.VMEM((s + 2 * PAD, bk), F32), pltpu.VMEM((s, bk), F32), pltpu.VMEM((t, bk), F32), pltpu.VMEM((t, bk), F32),
         pltpu.VMEM((s, bk), F32), pltpu.VMEM((t, bk), F32)], list(stages))
    return (res[0][0], res[1]) if stages else res[0][0]


def _lru_grads(lam_, hnb, a, sq, rs, r, i, xc, xcb, w_ref, dwacc, d, big_l, acc):
    bk = xc.shape[1]
    dba, dbx, dl = acc
    q = lam_ * i * xc
    dla = lam_ * hnb * a - q * (a * a) * rs
    dpr = dla * big_l * r * (1.0 - r)
    dpi = q * sq * (1.0 - i)
    dprb, dpib = dpr.astype(BF16), dpi.astype(BF16)
    c0 = d * 2 * bk
    dxc = (lam_ * sq * i
           + lax.dot_general(dprb, w_ref[:, pl.ds(c0, bk)], _DIMS["nt"], preferred_element_type=F32)
           + lax.dot_general(dpib, w_ref[:, pl.ds(c0 + bk, bk)], _DIMS["nt"], preferred_element_type=F32))
    dwacc[:, pl.ds(c0, bk)] += lax.dot_general(xcb, dprb, _DIMS["tn"], preferred_element_type=F32)
    dwacc[:, pl.ds(c0 + bk, bk)] += lax.dot_general(xcb, dpib, _DIMS["tn"], preferred_element_type=F32)
    return dxc, (dba + _colsum(dpr), dbx + _colsum(dpi), dl + _colsum(dla * r))


def _lru_bwd(z, dy, gatew, pk, stages=()):
    s = z.shape[0]
    n_h, bk = gatew.shape[0], gatew.shape[1]
    d = n_h * bk
    t = _chunk(s)
    n_ch = s // t

    def body(u_ref, ug_ref, dy_ref, w_ref, pk_ref, du_ref, dug_ref, dw_ref, dpk_ref,
             upad, h0pad, h1pad, dxpad, abuf, bbuf, lbuf, dwacc, edge, xcbuf):
        for ref in (upad, h0pad, h1pad, dxpad):
            _zero_pads(ref, s)
        upad[pl.ds(PAD, s), :] = u_ref[...].astype(F32)
        dwacc[...] = jnp.zeros_like(dwacc)
        cw, cb, ba, bx, lam, big_l = _lru_params(pk_ref)
        zero = jnp.zeros((SUBLANES, bk), F32)
        zrow = jnp.zeros((1, bk), F32)
        rowi = lax.broadcasted_iota(jnp.int32, (t, bk), 0)

        def at(t0):
            return pl.ds(pl.multiple_of(PAD + t0, SUBLANES), t)

        def conv_in(t0):
            xc = xcbuf[pl.ds(t0, t), :]
            return xc, xc.astype(BF16)

        def dh_of(t0):
            ug = ug_ref[pl.ds(t0, t), :].astype(F32)
            gl, th = _gelu(ug)
            dyv = dy_ref[pl.ds(t0, t), :].astype(F32)
            return dyv * gl, dyv * _gelu_grad(ug, th)

        def sweep1(ch, carry):
            t0 = pl.multiple_of(ch * t, t)
            xc = _conv(_window(upad, t0, t), cw, cb, t)
            xcbuf[pl.ds(t0, t), :] = xc
            _, i, a, sq, _ = _gates(xc.astype(BF16), w_ref, 0, bk, ba, bx, big_l)
            abuf[...] = a
            bbuf[...] = sq * i * xc
            return _scan_chunk(abuf, bbuf, h0pad, PAD + t0, carry, t, False)

        lax.fori_loop(0, n_ch, sweep1, zero)

        edge[...] = zero

        def sweep2(ci, st):
            carry_h, carry_l, acc = st
            t0 = pl.multiple_of((n_ch - 1 - ci) * t, t)
            xc, xcb = conv_in(t0)
            _, i1, a1, sq1, _ = _gates(xcb, w_ref, 1, bk, ba, bx, big_l)
            abuf[...] = a1
            bbuf[...] = sq1 * i1 * xc
            carry_h = _scan_chunk(abuf, bbuf, h1pad, PAD + t0, carry_h, t, True)
            dh, dgl = dh_of(t0)
            dug_ref[pl.ds(t0, t), :] = (dgl * (h0pad[at(t0), :] + h1pad[at(t0), :])).astype(BF16)
            r0, i0, a0, sq0, rs0 = _gates(xcb, w_ref, 0, bk, ba, bx, big_l)
            abuf[...] = jnp.where(rowi == t - 1, edge[0:1, :], pltpu.roll(a0, t - 1, 0))
            bbuf[...] = dh
            carry_l = _scan_chunk(abuf, bbuf, lbuf, 0, carry_l, t, True)
            edge[...] = jnp.broadcast_to(a0[0:1, :], (SUBLANES, bk))
            hprev = _shift(_window(h0pad, t0, t), -1, t)
            dxc, acc = _lru_grads(lbuf[...], hprev, a0, sq0, rs0, r0, i0, xc, xcb, w_ref, dwacc, 0, big_l[0], acc)
            dxpad[at(t0), :] = dxc
            return carry_h, carry_l, acc

        _, _, acc0 = lax.fori_loop(0, n_ch, sweep2, (zero, zero, (zrow, zrow, zrow)))

        edge[...] = zero

        def sweep3(ch, st):
            carry_l, acc = st
            t0 = pl.multiple_of(ch * t, t)
            xc, xcb = conv_in(t0)
            r1, i1, a1, sq1, rs1 = _gates(xcb, w_ref, 1, bk, ba, bx, big_l)
            dh, _ = dh_of(t0)
            abuf[...] = jnp.where(rowi == 0, edge[0:1, :], pltpu.roll(a1, 1, 0))
            bbuf[...] = dh
            carry_l = _scan_chunk(abuf, bbuf, lbuf, 0, carry_l, t, False)
            edge[...] = jnp.broadcast_to(a1[t - 1:t, :], (SUBLANES, bk))
            hnext = _shift(_window(h1pad, t0, t), 1, t)
            dxc, acc = _lru_grads(lbuf[...], hnext, a1, sq1, rs1, r1, i1, xc, xcb, w_ref, dwacc, 1, big_l[1], acc)
            dxpad[at(t0), :] += dxc
            return carry_l, acc

        _, acc1 = lax.fori_loop(0, n_ch, sweep3, (zero, (zrow, zrow, zrow)))

        def sweep4(ch, st):
            t0 = pl.multiple_of(ch * t, t)
            sdx = _window(dxpad, t0, t)
            su = _window(upad, t0, t)
            dxc = _shift(sdx, 0, t)
            du = cw[0] * _shift(sdx, 2, t) + cw[1] * _shift(sdx, 1, t) + cw[2] * dxc + cw[3] * _shift(sdx, -1, t)
            du_ref[pl.ds(t0, t), :] = du.astype(BF16)
            return tuple(st[k] + _colsum(dxc * _shift(su, k - 2, t)) for k in range(4)) + (st[4] + _colsum(dxc),)

        conv_g = lax.fori_loop(0, n_ch, sweep4, (zrow,) * 5)

        dpk_ref[...] = jnp.zeros_like(dpk_ref)
        rows = list(conv_g) + [acc0[0], acc1[0], acc0[1], acc1[1],
                               acc0[2] * LRU_C * _sigmoid(-lam[0]), acc1[2] * LRU_C * _sigmoid(-lam[1])]
        for k, v in enumerate(rows):
            dpk_ref[pl.ds(k, 1), :] = v
        dw_ref[...] = dwacc[...].astype(BF16)

    u_spec, ug_spec, w_spec, pk_spec, blk = _lru_specs(s, d, bk)
    padded = pltpu.VMEM((s + 2 * PAD, bk), F32)
    chunk = pltpu.VMEM((t, bk), F32)
    res = _host_call(
        "lru_bwd", (n_h,), lambda ins, outs, scr: body(*ins, *outs, *scr), [z, z, dy, gatew, pk],
        [u_spec, ug_spec, blk, w_spec, pk_spec],
        [_sds((s, d), BF16), _sds((s, d), BF16), _sds((n_h, bk, 4 * bk), BF16), _sds((n_h, 16, bk), F32)],
        [blk, blk, w_spec, pk_spec],
        [padded, padded, padded, padded, chunk, chunk, chunk, pltpu.VMEM((bk, 4 * bk), F32),
         pltpu.VMEM((SUBLANES, bk), F32), pltpu.VMEM((s, bk), F32)], list(stages))
    return res if stages else res[0]


def _scalar(v):
    return jnp.reshape(v, (1,)).astype(jnp.int32)


def _add_sibling(g, r, c):
    _, rows, cols = g.shape
    rh = rows // 2
    tr = _tile(rh, 512, 16)
    nr = rh // tr

    def body(c_ref, g_ref, r_ref, o_ref):
        o_ref[...] = (g_ref[...].astype(F32) + r_ref[...].astype(F32)).astype(BF16)

    spec = pl.BlockSpec((None, tr, cols), lambda k, i, c_ref: (k, i, 0))
    return pl.pallas_call(
        body, name="add_sibling", out_shape=_sds((N_CHIP, rh, cols), BF16),
        grid_spec=pltpu.PrefetchScalarGridSpec(
            num_scalar_prefetch=1, grid=(N_CHIP, nr),
            in_specs=[pl.BlockSpec((None, tr, cols), lambda k, i, c_ref: (k, c_ref[0] * nr + i, 0)), spec], out_specs=spec),
        compiler_params=_cparams(("arbitrary", "arbitrary")),
    )(_scalar(c), g, r)


def _add_pair(g, r):
    _, rh, cols = g.shape
    tr = _tile(rh, 512, 16)

    def body(ins, outs, scr):
        outs[0][...] = (ins[0][...].astype(F32) + ins[1][...].astype(F32)).astype(BF16)

    spec = pl.BlockSpec((None, tr, cols), lambda k, i: (k, i, 0))
    return _host_call("add_pair", (N_CHIP, rh // tr), body, [g, r], [spec, spec], [_sds(g.shape, BF16)], [spec], [], [])[0][0]


def _sum_chips(p, rcv, k_me, c):
    _, rh, cols = p.shape
    tr = _tile(rh, 512, 16)
    nr = rh // tr

    def body(kc_ref, p_ref, r_ref, o_ref):
        acc = p_ref[...].astype(F32)
        for j in range(3):
            acc = acc + r_ref[j].astype(F32)
        o_ref[...] = acc

    return pl.pallas_call(
        body, name="sum_chips", out_shape=_sds((2 * rh, cols), F32),
        grid_spec=pltpu.PrefetchScalarGridSpec(
            num_scalar_prefetch=1, grid=(nr,),
            in_specs=[pl.BlockSpec((None, tr, cols), lambda i, kc_ref: (kc_ref[0], i, 0)),
                      pl.BlockSpec((3, tr, cols), lambda i, kc_ref: (0, i, 0))],
            out_specs=pl.BlockSpec((tr, cols), lambda i, kc_ref: (kc_ref[1] * nr + i, 0))),
        compiler_params=_cparams(("arbitrary",)),
    )(jnp.stack([k_me, c]).astype(jnp.int32), p, rcv)


def _sum_devices(g):
    def body(g_ref, o_ref):
        acc = g_ref[0]
        for dev in range(1, N_DEV):
            acc = acc + g_ref[dev]
        o_ref[...] = acc

    return pl.pallas_call(body, name="sum_devices", out_shape=_sds(g.shape[1:], F32))(g)


def _adamw(w, g, m, v):
    rows, cols = w.shape
    tr = _tile(rows, 256, SUBLANES)

    def body(ins, outs, scr):
        w_ref, g_ref, m_ref, v_ref = ins
        go_ref, d_ref, nm_ref, nv_ref = outs
        gv = g_ref[...]
        go_ref[...] = gv
        nm = ADAM_B1 * m_ref[...] + (1.0 - ADAM_B1) * gv
        nv = ADAM_B2 * v_ref[...] + (1.0 - ADAM_B2) * (gv * gv)
        m_hat = nm / (1.0 - ADAM_B1 ** ADAM_STEP)
        v_hat = nv / (1.0 - ADAM_B2 ** ADAM_STEP)
        d_ref[...] = -ADAM_LR * (m_hat / (jnp.sqrt(v_hat) + ADAM_EPS) + ADAM_WD * w_ref[...])
        nm_ref[...] = nm
        nv_ref[...] = nv

    spec = pl.BlockSpec((tr, cols), lambda i: (i, 0))
    return _host_call("adamw", (rows // tr,), body, [w, g, m, v], [spec] * 4, [_sds((rows, cols), F32)] * 4, [spec] * 4, [], [])[0]


def _pack(vs, unit):
    flat = jnp.concatenate([v.reshape(-1).astype(F32) for v in vs])
    pad = (-flat.shape[0]) % unit
    if pad:
        flat = jnp.concatenate([flat, jnp.zeros((pad,), F32)])
    return flat.reshape(-1, 128)


def _unpack(p, like):
    flat = p.reshape(-1)
    out, off = [], 0
    for v in like:
        n = math.prod(v.shape)
        out.append(flat[off:off + n].reshape(v.shape))
        off += n
    return out


def kernel(x, w_in, pool_w, pool_scale, conv_w, conv_b, lru_wa, lru_ba, lru_wx, lru_bx, lru_lambda, w_pool_up, w_lru_up, w_out, b_out, ln1_g, ln1_b, w_ff1, b_ff1, w_ff2, b_ff2, ln2_g, ln2_b, loss_target, m_w_in, m_pool_w, m_pool_scale, m_conv_w, m_conv_b, m_lru_wa, m_lru_ba, m_lru_wx, m_lru_bx, m_lru_lambda, m_w_pool_up, m_w_lru_up, m_w_out, m_b_out, m_ln1_g, m_ln1_b, m_w_ff1, m_b_ff1, m_w_ff2, m_b_ff2, m_ln2_g, m_ln2_b, v_w_in, v_pool_w, v_pool_scale, v_conv_w, v_conv_b, v_lru_wa, v_lru_ba, v_lru_wx, v_lru_bx, v_lru_lambda, v_w_pool_up, v_w_lru_up, v_w_out, v_b_out, v_ln1_g, v_ln1_b, v_w_ff1, v_b_ff1, v_w_ff2, v_b_ff2, v_ln2_g, v_ln2_b):
    given = dict(locals())
    wt = {n: given[n] for n in WEIGHTS}
    mom = {n: given["m_" + n] for n in WEIGHTS}
    vel = {n: given["v_" + n] for n in WEIGHTS}

    ix, iy, ic = _mesh_pos()
    k_me = 2 * ix + iy
    s, d = x.shape[1], x.shape[2]
    ds = d // N_CHIP
    n_g, pgs, pg = pool_w.shape[1], pool_w.shape[2], pool_w.shape[3]
    n_h, bks, bk = lru_wa.shape[2], lru_wa.shape[3], lru_wa.shape[4]
    f = b_ff1.shape[1]
    x2 = x[0]
    vec = lambda a: a.reshape(1, -1)

    sharded_vecs = [conv_w[0], lru_ba[0], lru_bx[0], lru_lambda[0]]
    rows_sv = jnp.concatenate(sharded_vecs + [jnp.zeros((6, ds), F32)], axis=0)
    sv = _all_gather_small(rows_sv)
    sv = sv.reshape(N_CHIP, 2, 16, ds)[:, 0].transpose(1, 0, 2).reshape(16, d)
    conv_w_f, ba_f, bx_f, lam_f = sv[0:4], sv[4:6], sv[6:8], sv[8:10]
    pk = jnp.concatenate([conv_w_f, conv_b, ba_f, bx_f, lam_f, jnp.zeros((5, d), F32)], axis=0)
    pk = pk.reshape(16, n_h, bk).transpose(1, 0, 2)

    def gate_stack(wa, wx):
        return jnp.stack([wa[0], wx[0]], axis=1)

    mats = {
        "w_in": w_in[0], "w_pool_up": w_pool_up[0], "w_lru_up": w_lru_up[0], "w_out": w_out[0],
        "w_ff1": w_ff1[0], "w_ff2": w_ff2[0],
        "pool_w": pool_w[0].reshape(n_g * pgs, pg),
        "gate_w": gate_stack(lru_wa, lru_wx).reshape(4 * n_h * bks, bk),
    }
    names = list(mats)
    placed = {n: _cast_place(mats[n], k_me) for n in names}

    def add_sibling(gs, swapped):
        return [_add_sibling(g, r, ic) for g, r in zip(gs, swapped)]

    def sum_chips(ps, received):
        return [_sum_chips(p, r, k_me, ic) for p, r in zip(ps, received)]

    def whole_gather(ts):
        return _chain([_gather_direct(ts), _together([_gather_relay(ts), _gather_d2d(ts, (0, 1))]), _gather_d2d(ts, (2,))])

    ((wg_in,),) = _run_stages("gather_first", [whole_gather([placed["w_in"]])])

    (z, x_bf), (wb_mix, wb_ff1, wb_small) = _fwd_in(x2, wg_in, stages=[
        _gather_direct([placed[n] for n in ("w_pool_up", "w_lru_up", "w_out")]), _gather_direct([placed["w_ff1"]]),
        whole_gather([placed["pool_w"], placed["gate_w"]])])
    wf_pool = wb_small[0].reshape(N_CHIP, n_g, pgs, pg).transpose(1, 0, 2, 3).reshape(n_g, pg, pg)
    wf_gate = wb_small[1].reshape(N_CHIP, 2, 2, n_h, bks, bk).transpose(3, 0, 4, 1, 2, 5).reshape(n_h, bk, 4 * bk)
    (d_pool, y_pool), (wb_mix,) = _pool_fwd(z, wf_pool, pool_scale, stages=[_gather_relay(wb_mix)])
    y_lru, (wb_ff1, wb_mix, wb_ff2) = _lru_fwd(z, wf_gate, pk, stages=[
        _gather_relay(wb_ff1), _gather_d2d(wb_mix), _gather_direct([placed["w_ff2"]])])
    wf_pu, wf_lu, wf_out = (b.reshape(d, d) for b in wb_mix)
    (m_mix, p_a, p_b), (wb_ff1, wb_ff2) = _fwd_merge(y_pool, y_lru, wf_pu, wf_lu, z, stages=[
        _gather_d2d(wb_ff1), _gather_relay(wb_ff2)])
    wg_ff1 = wb_ff1[0]
    (xhat1, x1_bf, rstd1), (wb_ff2,) = _fwd_out_ln1(m_mix, wf_out, x2, b_out, ln1_g, ln1_b, stages=[_gather_d2d(wb_ff2)])
    hdn, d_hdn = _fwd_ff1(x1_bf, wg_ff1, b_ff1)
    wf_ff2 = wb_ff2[0].reshape(f, d)
    dr2, dr2_bf, g_ln2_g, g_ln2_b, g_b_ff2, loss_part = _fwd_ff2_ln2_loss(
        hdn, wf_ff2, xhat1, ln1_g, ln1_b, b_ff2, ln2_g, ln2_b, loss_target[0])

    dpre, g_b_ff1 = _bwd_ff2_in(dr2_bf, wf_ff2, d_hdn)
    g_ff = [_wgrad("wgrad_ff1", x1_bf, dpre, True), _wgrad("wgrad_ff2", hdn, dr2_bf, False)]
    (dr1, dr1_bf, g_ln1_g, g_ln1_b, g_b_out), (swapped,) = _bwd_ff1_in_ln1(
        dpre, wg_ff1, dr2, xhat1, rstd1, ln1_g, stages=[_swap_halves(g_ff)])
    sums_ff = add_sibling(g_ff, swapped)
    dp_a, dp_b, dg_a, dg_b = _bwd_out_in(dr1_bf, wf_out, z, p_a, p_b)
    dy_pool = _bwd_up_in("bwd_pool_up_in", dp_a, wf_pu)
    dy_lru = _bwd_up_in("bwd_lru_up_in", dp_b, wf_lu)
    g_mix = [_wgrad("wgrad_pool_up", y_pool, dp_a, False), _wgrad("wgrad_lru_up", y_lru, dp_b, False),
             _wgrad("wgrad_out", m_mix, dr1_bf, False)]
    (du_pool, g_pool_w, g_pool_scale), (swapped,) = _pool_bwd(
        d_pool, dy_pool, wf_pool, pool_scale, stages=[_swap_halves(g_mix)])
    sums_mix = add_sibling(g_mix, swapped)
    (du_lru, du_gate, g_gate_w, g_pk), (recv_ff, recv_mix) = _lru_bwd(
        z, dy_lru, wf_gate, pk, stages=[_scatter_chips(sums_ff), _scatter_chips(sums_mix)])
    halves = sum_chips(sums_ff + sums_mix, recv_ff + recv_mix)
    g_small = [g_pool_w.reshape(n_g, N_CHIP, pgs, pg).transpose(1, 0, 2, 3).reshape(N_CHIP, n_g * pgs, pg),
               g_gate_w.reshape(n_h, N_CHIP, bks, 2, 2, bk).transpose(1, 3, 4, 0, 2, 5).reshape(N_CHIP, 4 * n_h * bks, bk)]
    dz = jnp.concatenate([du_pool, du_lru, du_gate, dg_a, dg_b], axis=1)
    g_other, (joined, swapped) = _wgrad_rows_half(
        "wgrad_in_other", x_bf, dz, 1 - ic, [_join_halves(halves), _swap_halves(g_small)])
    g_mat = dict(zip(["w_ff1", "w_ff2", "w_pool_up", "w_lru_up", "w_out"], joined))
    sums_small = add_sibling(g_small, swapped)
    g_mine, (from_sibling, recv_small) = _wgrad_rows_half(
        "wgrad_in_mine", x_bf, dz, ic, [_send_to_sibling([g_other]), _scatter_chips(sums_small)])
    sums_in = [_add_pair(g_mine, from_sibling[0])]

    def stacked(tree):
        return gate_stack(tree["lru_wa"], tree["lru_wx"]).reshape(4 * n_h * bks, bk)

    res = {}

    def update(n):
        if n == "gate_w":
            outs = [o.reshape(2, 2, n_h, bks, bk) for o in _adamw(stacked(wt), g_mat[n], stacked(mom), stacked(vel))]
            res["lru_wa"] = [o[:, 0][None] for o in outs]
            res["lru_wx"] = [o[:, 1][None] for o in outs]
        else:
            shp2 = mats[n].shape
            outs = _adamw(wt[n].reshape(shp2), g_mat[n], mom[n].reshape(shp2), vel[n].reshape(shp2))
            res[n] = [o.reshape(wt[n].shape) for o in outs]

    grad_x, (recv_in,) = _bwd_in(dz, wg_in, dr1, stages=[_scatter_chips(sums_in)])
    halves = sum_chips(sums_small + sums_in, recv_small + recv_in)
    (joined,) = _run_stages("join_last", [_join_halves(halves)])
    g_mat.update(zip(["pool_w", "gate_w", "w_in"], joined))
    for n in names:
        update(n)

    g_pk = g_pk.transpose(1, 0, 2).reshape(16, d)
    vec_full = {
        "pool_scale": g_pool_scale, "conv_w": g_pk[0:4], "conv_b": g_pk[4:5],
        "lru_ba": g_pk[5:7], "lru_bx": g_pk[7:9], "lru_lambda": g_pk[9:11],
        "b_out": g_b_out, "ln1_g": g_ln1_g, "ln1_b": g_ln1_b, "b_ff1": g_b_ff1, "b_ff2": g_b_ff2,
        "ln2_g": g_ln2_g, "ln2_b": g_ln2_b,
    }
    vnames = list(vec_full)
    vg = _sum_devices(_all_gather_small(_pack([vec_full[n] for n in vnames], 1024)))
    vg = dict(zip(vnames, _unpack(vg, [vec_full[n] for n in vnames])))
    for n in ("conv_w", "lru_ba", "lru_bx", "lru_lambda"):
        vg[n] = lax.dynamic_slice_in_dim(vg[n], k_me * ds, ds, axis=1)
    vg = {n: vg[n].reshape(wt[n].shape) for n in vnames}
    upd = _adamw(_pack([wt[n] for n in vnames], 1024), _pack([vg[n] for n in vnames], 1024),
                 _pack([mom[n] for n in vnames], 1024), _pack([vel[n] for n in vnames], 1024))
    upd = [_unpack(u, [wt[n] for n in vnames]) for u in upd]
    for i, n in enumerate(vnames):
        res[n] = [vg[n], upd[1][i], upd[2][i], upd[3][i]]

    loss = lax.psum(loss_part[0, 0], ("x", "y", "c"))
    return (loss, grad_x[None], *[res[n][0] for n in WEIGHTS], *[res[n][1] for n in WEIGHTS],
            *[res[n][2] for n in WEIGHTS], *[res[n][3] for n in WEIGHTS])
```

```python
import functools
import math

import jax
import jax.numpy as jnp
from jax import lax
from jax.experimental import pallas as pl
from jax.experimental.pallas import tpu as pltpu

F32 = jnp.float32
BF16 = jnp.bfloat16
MESH = pl.DeviceIdType.MESH
ANY = pl.BlockSpec(memory_space=pl.ANY)

N_CHIP = 4
N_DEV = 8
VMEM_LIMIT_BYTES = 56 * 1024 * 1024
SUBLANES = 8
PAD = 8
SCAN_UNROLL = 8

POOL_WINDOWS = (2, 4, 8, 16)
LRU_C = 8.0
DN_ALPHA = 2.0 ** 0.25
LN_EPS = 1e-5
ADAM_LR, ADAM_B1, ADAM_B2, ADAM_EPS, ADAM_WD, ADAM_STEP = 0.001, 0.9, 0.999, 1e-08, 0.01, 10

WEIGHTS = ("w_in", "pool_w", "pool_scale", "conv_w", "conv_b", "lru_wa", "lru_ba", "lru_wx", "lru_bx", "lru_lambda",
           "w_pool_up", "w_lru_up", "w_out", "b_out", "ln1_g", "ln1_b", "w_ff1", "b_ff1", "w_ff2", "b_ff2", "ln2_g", "ln2_b")


def _cparams(sem=None):
    return pltpu.CompilerParams(dimension_semantics=sem, vmem_limit_bytes=VMEM_LIMIT_BYTES)


def _tile(dim, pref, unit=128):
    if dim <= pref:
        return dim
    t = (pref // unit) * unit
    while t > unit and dim % t:
        t -= unit
    assert dim % t == 0, (dim, pref)
    return t


def _mesh_pos():
    x, y, c = lax.axis_index("x"), lax.axis_index("y"), lax.axis_index("c")
    return x, y, c


def _other_chips(x, y):
    return [(1 - x, y), (x, 1 - y), (1 - x, 1 - y)]


def _all_gather_small(v):
    m_per, n = v.shape

    def body(x_ref, out_ref, send_sems, recv_sems, local_sem):
        x, y, c = _mesh_pos()
        me, sibling = (x, y, c), (x, y, 1 - c)
        chips = _other_chips(x, y)

        def rows(px, py, pc):
            return out_ref.at[4 * px + 2 * py + pc]

        def copy(k, block, to, src=None):
            return pltpu.make_async_remote_copy(
                src_ref=rows(*block) if src is None else src, dst_ref=rows(*block),
                send_sem=send_sems.at[k], recv_sem=recv_sems.at[k], device_id=to, device_id_type=MESH)

        mine = pltpu.make_async_copy(x_ref, rows(*me), local_sem)
        mine.start()
        first = [copy(0, me, sibling, src=x_ref)]
        first += [copy(1 + j, me, (*chip, c), src=x_ref) for j, chip in enumerate(chips)]
        for cp in first:
            cp.start()
        passed = [copy(4 + j, (*chip, c), sibling) for j, chip in enumerate(chips)]
        for j, chip in enumerate(chips):
            copy(1 + j, (*chip, c), me).wait_recv()
            passed[j].start()
        copy(0, sibling, me).wait_recv()
        for j, chip in enumerate(chips):
            copy(4 + j, (*chip, 1 - c), me).wait_recv()
        for cp in first + passed:
            cp.wait_send()
        mine.wait()

    return pl.pallas_call(
        body, name="all_gather_small",
        out_shape=jax.ShapeDtypeStruct((N_DEV, m_per, n), v.dtype),
        in_specs=[pl.BlockSpec(memory_space=pltpu.VMEM)],
        out_specs=pl.BlockSpec(memory_space=pltpu.VMEM),
        scratch_shapes=[pltpu.SemaphoreType.DMA((7,)), pltpu.SemaphoreType.DMA((7,)), pltpu.SemaphoreType.DMA],
    )(v)


class _Stage:
    def __init__(self, srcs, bufs, news, n_sems, copies):
        self.srcs, self.bufs, self.news, self.n_sems, self.copies = list(srcs), list(bufs), list(news), n_sems, copies
        self.phases = [(copies, 0)]


class _SemsFrom:
    def __init__(self, ref, offset):
        self.ref, self.offset, self.at = ref, offset, self

    def __getitem__(self, s):
        return self.ref.at[self.offset + s]


def _chain(stages):
    chained = _Stage([], stages[0].bufs, [], sum(st.n_sems for st in stages), None)
    chained.phases, first = [], 0
    for st in stages:
        chained.phases.append((st.copies, first))
        first += st.n_sems
    return chained


def _remote(src, dst, send_sems, recv_sems, s, to):
    return pltpu.make_async_remote_copy(src_ref=src, dst_ref=dst, send_sem=send_sems.at[s], recv_sem=recv_sems.at[s],
                                        device_id=to, device_id_type=MESH)


def _stage_operands(stages, n_in, n_out):
    ins, outs, aliases, scratch = [], [], {}, []
    for st in stages:
        for i in range(len(st.bufs)):
            aliases[n_in + len(ins) + len(st.srcs) + i] = n_out + len(outs) + i
        ins += st.srcs + st.bufs
        outs += [jax.ShapeDtypeStruct(b.shape, b.dtype) for b in st.bufs] + st.news
        scratch += [pltpu.SemaphoreType.DMA((st.n_sems,)), pltpu.SemaphoreType.DMA((st.n_sems,))]
    return ins, outs, aliases, scratch


def _stage_refs(stages, in_refs, out_refs, sem_refs):
    parts, i, o = [], 0, 0
    for n, st in enumerate(stages):
        src = in_refs[i:i + len(st.srcs)]
        i += len(st.srcs) + len(st.bufs)
        buf = out_refs[o:o + len(st.bufs)]
        new = out_refs[o + len(st.bufs):o + len(st.bufs) + len(st.news)]
        o += len(st.bufs) + len(st.news)
        parts.append((src, buf, new, sem_refs[2 * n], sem_refs[2 * n + 1]))
    return parts


def _stage_results(stages, res):
    out, o = [], 0
    for st in stages:
        n = len(st.bufs) + len(st.news)
        out.append(list(res[o:o + n]))
        o += n
    return out


def _phase_copies(st, part, p):
    src, buf, new, send_sems, recv_sems = part
    copies, first = st.phases[p]
    return copies(src, buf, new, _SemsFrom(send_sems, first), _SemsFrom(recv_sems, first))


def _phase_wait(st, part, p):
    started, landing = _phase_copies(st, part, p)
    for cp in landing:
        cp.wait_recv()
    for cp in started:
        cp.wait_send()


def _phase_begin(st, part, p):
    if p:
        _phase_wait(st, part, p - 1)
    for cp in _phase_copies(st, part, p)[0]:
        cp.start()


LAST_PHASE_AT = 0.9


def _run_stages(name, stages):
    ins, outs, aliases, scratch = _stage_operands(stages, 0, 0)

    def body(*refs):
        parts = _stage_refs(stages, refs[:len(ins)], refs[len(ins):len(ins) + len(outs)], refs[len(ins) + len(outs):])
        for st, part in zip(stages, parts):
            for p in range(len(st.phases)):
                _phase_begin(st, part, p)
            _phase_wait(st, part, len(st.phases) - 1)

    res = pl.pallas_call(
        body, name=name, out_shape=outs, in_specs=[ANY] * len(ins), out_specs=[ANY] * len(outs),
        input_output_aliases=aliases, scratch_shapes=scratch)(*ins)
    return _stage_results(stages, res)


def _gather_direct(ts):
    def copies(src, buf, new, send_sems, recv_sems):
        x, y, c = _mesh_pos()
        started, landing = [], []
        for t in range(len(ts)):
            rh = ts[t].shape[1] // 2
            rows = pl.ds(c * rh, rh)
            mine = buf[t].at[2 * x + y, rows]
            for j, chip in enumerate(_other_chips(x, y)[:2]):
                theirs = buf[t].at[2 * chip[0] + chip[1], rows]
                started.append(_remote(mine, mine, send_sems, recv_sems, 2 * t + j, (*chip, c)))
                landing.append(_remote(theirs, theirs, send_sems, recv_sems, 2 * t + j, (x, y, c)))
        return started, landing

    return _Stage([], ts, [], 2 * len(ts), copies)


def _gather_relay(ts):
    def copies(src, buf, new, send_sems, recv_sems):
        x, y, c = _mesh_pos()
        (x_nb, y_nb, diag) = _other_chips(x, y)
        block = lambda chip: 2 * chip[0] + chip[1]
        started, landing = [], []
        for t in range(len(ts)):
            rq = ts[t].shape[1] // 4
            q0, q1 = pl.ds(2 * c * rq, rq), pl.ds((2 * c + 1) * rq, rq)
            from_y, from_x = buf[t].at[block(y_nb), q0], buf[t].at[block(x_nb), q1]
            started.append(_remote(from_y, from_y, send_sems, recv_sems, 2 * t, (*x_nb, c)))
            started.append(_remote(from_x, from_x, send_sems, recv_sems, 2 * t + 1, (*y_nb, c)))
            for j, q in enumerate((q0, q1)):
                lands = buf[t].at[block(diag), q]
                landing.append(_remote(lands, lands, send_sems, recv_sems, 2 * t + j, (x, y, c)))
        return started, landing

    return _Stage([], ts, [], 2 * len(ts), copies)


def _together(stages):
    def copies(src, buf, new, send_sems, recv_sems):
        started, landing, first = [], [], 0
        for st in stages:
            more = st.copies(src, buf, new, _SemsFrom(send_sems, first), _SemsFrom(recv_sems, first))
            started, landing, first = started + more[0], landing + more[1], first + st.n_sems
        return started, landing

    return _Stage([], stages[0].bufs, [], sum(st.n_sems for st in stages), copies)


def _gather_d2d(ts, which=(0, 1, 2)):
    def copies(src, buf, new, send_sems, recv_sems):
        x, y, c = _mesh_pos()
        started, landing = [], []
        for t in range(len(ts)):
            rh = ts[t].shape[1] // 2
            for j, chip in enumerate(_other_chips(x, y)):
                if j not in which:
                    continue
                got = buf[t].at[2 * chip[0] + chip[1], pl.ds(c * rh, rh)]
                other = buf[t].at[2 * chip[0] + chip[1], pl.ds((1 - c) * rh, rh)]
                started.append(_remote(got, got, send_sems, recv_sems, 3 * t + j, (x, y, 1 - c)))
                landing.append(_remote(other, other, send_sems, recv_sems, 3 * t + j, (x, y, c)))
        return started, landing

    return _Stage([], ts, [], 3 * len(ts), copies)


def _swap_halves(gs):
    def copies(src, buf, new, send_sems, recv_sems):
        x, y, c = _mesh_pos()
        started, landing = [], []
        for t in range(len(gs)):
            rh = gs[t].shape[1] // 2
            started.append(_remote(src[t].at[:, pl.ds((1 - c) * rh, rh)], new[t], send_sems, recv_sems, t, (x, y, 1 - c)))
            landing.append(_remote(new[t], new[t], send_sems, recv_sems, t, (x, y, c)))
        return started, landing

    news = [jax.ShapeDtypeStruct((g.shape[0], g.shape[1] // 2, g.shape[2]), g.dtype) for g in gs]
    return _Stage(gs, [], news, len(gs), copies)


def _send_to_sibling(gs):
    def copies(src, buf, new, send_sems, recv_sems):
        x, y, c = _mesh_pos()
        started = [_remote(src[t], new[t], send_sems, recv_sems, t, (x, y, 1 - c)) for t in range(len(gs))]
        landing = [_remote(new[t], new[t], send_sems, recv_sems, t, (x, y, c)) for t in range(len(gs))]
        return started, landing

    return _Stage(gs, [], [jax.ShapeDtypeStruct(g.shape, g.dtype) for g in gs], len(gs), copies)


def _scatter_chips(ps):
    def copies(src, buf, new, send_sems, recv_sems):
        x, y, c = _mesh_pos()
        started, landing = [], []
        for t in range(len(ps)):
            for j, chip in enumerate(_other_chips(x, y)):
                started.append(_remote(src[t].at[2 * chip[0] + chip[1]], new[t].at[j], send_sems, recv_sems, 3 * t + j, (*chip, c)))
                landing.append(_remote(new[t].at[j], new[t].at[j], send_sems, recv_sems, 3 * t + j, (x, y, c)))
        return started, landing

    return _Stage(ps, [], [jax.ShapeDtypeStruct((3,) + p.shape[1:], p.dtype) for p in ps], 3 * len(ps), copies)


def _join_halves(fs):
    def copies(src, buf, new, send_sems, recv_sems):
        x, y, c = _mesh_pos()
        started, landing = [], []
        for t in range(len(fs)):
            rh = fs[t].shape[0] // 2
            mine = buf[t].at[pl.ds(c * rh, rh)]
            theirs = buf[t].at[pl.ds((1 - c) * rh, rh)]
            started.append(_remote(mine, mine, send_sems, recv_sems, t, (x, y, 1 - c)))
            landing.append(_remote(theirs, theirs, send_sems, recv_sems, t, (x, y, c)))
        return started, landing

    return _Stage([], fs, [], len(fs), copies)


def _cast_place(w, k_me):
    rows, cols = w.shape
    tr = _tile(rows, 512, 16)

    def body(k_ref, w_ref, o_ref):
        o_ref[...] = w_ref[...].astype(BF16)

    return pl.pallas_call(
        body, name="cast_place", out_shape=_sds((N_CHIP, rows, cols), BF16),
        grid_spec=pltpu.PrefetchScalarGridSpec(
            num_scalar_prefetch=1, grid=(rows // tr,),
            in_specs=[pl.BlockSpec((tr, cols), lambda i, k_ref: (i, 0))],
            out_specs=pl.BlockSpec((None, tr, cols), lambda i, k_ref: (k_ref[0], i, 0))),
        compiler_params=_cparams(("arbitrary",)),
    )(_scalar(k_me), w)


_DIMS = {"nn": (((1,), (0,)), ((), ())), "nt": (((1,), (1,)), ((), ())), "tn": (((0,), (0,)), ((), ()))}


def _accum(ref, val, first):
    @pl.when(first)
    def _():
        ref[...] = val

    @pl.when(jnp.logical_not(first))
    def _():
        ref[...] += val


def _grid_step(grid):
    step = pl.program_id(0)
    for ax in range(1, len(grid)):
        step = step * grid[ax] + pl.program_id(ax)
    return step


def _host_call(name, grid, body, operands, in_specs, out_shape, out_specs, scratch, stages, prefetch=None):
    s_ins, s_outs, aliases, s_scratch = _stage_operands(stages, len(operands), len(out_shape))
    n_in, n_out, n_scr = len(operands), len(out_shape), len(scratch)
    n_pre = 0 if prefetch is None else 1

    def full_body(*refs):
        refs = refs[n_pre:]
        in_refs = refs[:n_in]
        s_in_refs = refs[n_in:n_in + len(s_ins)]
        o0 = n_in + len(s_ins)
        out_refs = refs[o0:o0 + n_out]
        s_out_refs = refs[o0 + n_out:o0 + n_out + len(s_outs)]
        c0 = o0 + n_out + len(s_outs)
        scr_refs = refs[c0:c0 + n_scr]
        if stages:
            parts = _stage_refs(stages, s_in_refs, s_out_refs, refs[c0 + n_scr:])
            step, n_steps = _grid_step(grid), math.prod(grid)
            for st, part in zip(stages, parts):
                n_ph = len(st.phases)
                for p in range(n_ph):
                    at = int(LAST_PHASE_AT * (n_steps - 1) * p / max(n_ph - 1, 1))
                    pl.when(step == at)(functools.partial(_phase_begin, st, part, p))
        body(in_refs, out_refs, scr_refs)
        if stages:
            for st, part in zip(stages, parts):
                pl.when(step == n_steps - 1)(functools.partial(_phase_wait, st, part, len(st.phases) - 1))

    all_in = list(in_specs) + [ANY] * len(s_ins)
    all_out = list(out_specs) + [ANY] * len(s_outs)
    all_scratch = list(scratch) + s_scratch
    params = _cparams(("arbitrary",) * len(grid))
    if prefetch is None:
        res = pl.pallas_call(
            full_body, name=name, grid=grid, in_specs=all_in, out_specs=all_out, out_shape=list(out_shape) + s_outs,
            input_output_aliases=aliases, scratch_shapes=all_scratch, compiler_params=params,
        )(*operands, *s_ins)
    else:
        res = pl.pallas_call(
            full_body, name=name, out_shape=list(out_shape) + s_outs,
            grid_spec=pltpu.PrefetchScalarGridSpec(num_scalar_prefetch=1, grid=grid, in_specs=all_in, out_specs=all_out,
                                                   scratch_shapes=all_scratch),
            input_output_aliases={i + 1: o for i, o in aliases.items()}, compiler_params=params,
        )(prefetch, *operands, *s_ins)
    return list(res[:n_out]), _stage_results(stages, res[n_out:])


def _matmul(name, grid, pairs, extras, outs, acc_shape, epilogue, stages=(), prefetch=None, lhs_to_epilogue=False):
    n_p = len(pairs)
    n_k = grid[-1]
    dims = [_DIMS[p[4]] for p in pairs]

    def body(in_refs, out, accs):
        ab, ex = in_refs[:2 * n_p], in_refs[2 * n_p:]
        if lhs_to_epilogue:
            ex = [ab[0]] + list(ex)
        ids = [pl.program_id(ax) for ax in range(len(grid))]
        k = ids[-1]

        def dot(p):
            return lax.dot_general(ab[2 * p][...].astype(BF16), ab[2 * p + 1][...].astype(BF16), dims[p],
                                   preferred_element_type=F32)

        if n_k == 1:
            epilogue([dot(p) for p in range(n_p)], ex, out, ids)
            return

        @pl.when(k == 0)
        def _():
            for acc in accs:
                acc[...] = jnp.zeros_like(acc)

        for p in range(n_p):
            accs[p][...] += dot(p)

        @pl.when(k == n_k - 1)
        def _():
            epilogue([acc[...] for acc in accs], ex, out, ids)

    in_specs = []
    operands = []
    for a, a_spec, b, b_spec, _ in pairs:
        in_specs += [a_spec, b_spec]
        operands += [a, b]
    for e, e_spec in extras:
        in_specs.append(e_spec)
        operands.append(e)
    res, stage_res = _host_call(name, grid, body, operands, in_specs, [o[0] for o in outs], [o[1] for o in outs],
                                [pltpu.VMEM(acc_shape, F32) for _ in pairs] if n_k > 1 else [], list(stages), prefetch)
    return (res, stage_res) if stages else res


def _out(res, stages, single=False):
    outs = res[0] if stages else res
    outs = outs[0] if single else outs
    return (outs, res[1]) if stages else outs


def _sds(shape, dtype):
    return jax.ShapeDtypeStruct(shape, dtype)


def _row(n):
    return pl.BlockSpec((1, n), lambda *_: (0, 0))


def _layer_norm(r):
    mu = jnp.mean(r, axis=-1, keepdims=True)
    xc = r - mu
    var = jnp.mean(xc * xc, axis=-1, keepdims=True)
    rstd = lax.rsqrt(var + LN_EPS)
    return xc * rstd, rstd


def _layer_norm_bwd(dxhat, xhat, rstd):
    m1 = jnp.mean(dxhat, axis=-1, keepdims=True)
    m2 = jnp.mean(dxhat * xhat, axis=-1, keepdims=True)
    return rstd * (dxhat - m1 - xhat * m2)


def _colsum(v):
    return jnp.sum(v, axis=0, keepdims=True)


ROW_TILE = 256


def _rows_call(name, s, epi, ins, outs):
    tr = _tile(s, ROW_TILE, SUBLANES)

    def spec(shape, kind):
        n = shape[1]
        return pl.BlockSpec((tr, n), lambda i: (i, 0)) if kind == "tile" else pl.BlockSpec((1, n), lambda i: (0, 0))

    def body(in_refs, out_refs, scr):
        epi([in_refs[0][...]], in_refs[1:], out_refs, [pl.program_id(0)])

    return _host_call(name, (s // tr,), body, [a for a, _ in ins], [spec(a.shape, k) for a, k in ins],
                      [o for o, _ in outs], [spec(o.shape, k) for o, k in outs], [], [])[0]


def _plain_matmul(name, a, b, mode):
    m, k_dim = a.shape
    n = b.shape[1]
    tm, tn, tk = _tile(m, 1024), _tile(n, 1024), _tile(k_dim, 2048)

    def epi(accs, ex, out, ids):
        out[0][...] = accs[0]

    assert mode == "nn"
    return _matmul(
        name, (m // tm, n // tn, k_dim // tk),
        [(a, pl.BlockSpec((tm, tk), lambda i, j, k: (i, k)), b, pl.BlockSpec((tk, tn), lambda i, j, k: (k, j)), "nn")],
        [], [(_sds((m, n), F32), pl.BlockSpec((tm, tn), lambda i, j, k: (i, j)))], (tm, tn), epi)[0]


def _fwd_in(x_in, wg_in, stages=()):
    s, d = x_in.shape
    inc = wg_in.shape[2]
    tm, tn, tk = _tile(s, 1024), _tile(inc, 1280), _tile(d, 2048)
    nb = inc // tn
    assert tk == d

    def epi(accs, ex, out, ids):
        out[0][...] = accs[0].astype(BF16)

        @pl.when(ids[1] == 0)
        def _():
            out[1][...] = ex[0][...].astype(BF16)

    x_spec = pl.BlockSpec((tm, tk), lambda i, j, k: (i, k))
    return _out(_matmul(
        "fwd_in", (s // tm, N_CHIP * nb, d // tk),
        [(x_in, x_spec, wg_in, pl.BlockSpec((None, tk, tn), lambda i, j, k: (j // nb, k, j % nb)), "nn")],
        [],
        [(_sds((s, N_CHIP * inc), BF16), pl.BlockSpec((tm, tn), lambda i, j, k: (i, j))), (_sds((s, d), BF16), x_spec)],
        (tm, tn), epi, stages, lhs_to_epilogue=True), stages)


def _fwd_merge(y_pool, y_lru, w_pu, w_lu, z, stages=()):
    s, d = y_pool.shape
    tm, tn, tk = _tile(s, 1024), _tile(d, 1024), _tile(d, 1024)
    ga0, gb0 = 3 * d // tn, 4 * d // tn

    def epi(accs, ex, out, ids):
        sa = _sigmoid(ex[0][...].astype(F32))
        sb = _sigmoid(ex[1][...].astype(F32))
        out[0][...] = (sa * accs[0] + sb * accs[1]).astype(BF16)
        out[1][...] = accs[0].astype(BF16)
        out[2][...] = accs[1].astype(BF16)

    a_spec = pl.BlockSpec((tm, tk), lambda i, j, k: (i, k))
    b_spec = pl.BlockSpec((tk, tn), lambda i, j, k: (k, j))
    o_spec = pl.BlockSpec((tm, tn), lambda i, j, k: (i, j))
    return _out(_matmul(
        "fwd_merge", (s // tm, d // tn, d // tk),
        [(y_pool, a_spec, w_pu, b_spec, "nn"), (y_lru, a_spec, w_lu, b_spec, "nn")],
        [(z, pl.BlockSpec((tm, tn), lambda i, j, k: (i, ga0 + j))), (z, pl.BlockSpec((tm, tn), lambda i, j, k: (i, gb0 + j)))],
        [(_sds((s, d), BF16), o_spec)] * 3, (tm, tn), epi, stages), stages)


def _fwd_out_ln1(m, w_out, x, b_out, g1, b1, stages=()):
    s, d = x.shape
    tm, tk = _tile(s, 512), _tile(d, 2048)

    def epi(accs, ex, out, ids):
        r = DN_ALPHA * ex[0][...] + accs[0] + ex[1][...]
        xhat, rstd = _layer_norm(r)
        out[0][...] = xhat
        out[1][...] = (xhat * ex[2][...] + ex[3][...]).astype(BF16)
        out[2][...] = rstd

    full = pl.BlockSpec((tm, d), lambda i, j, k: (i, 0))
    return _out(_matmul(
        "fwd_out_ln1", (s // tm, 1, d // tk),
        [(m, pl.BlockSpec((tm, tk), lambda i, j, k: (i, k)), w_out, pl.BlockSpec((tk, d), lambda i, j, k: (k, 0)), "nn")],
        [(x, full), (b_out, _row(d)), (g1, _row(d)), (b1, _row(d))],
        [(_sds((s, d), F32), full), (_sds((s, d), BF16), full), (_sds((s, 1), F32), pl.BlockSpec((tm, 1), lambda i, j, k: (i, 0)))],
        (tm, d), epi, stages), stages)


def _fwd_ff1(x1_bf, wg_ff1, b_ff1, stages=()):
    s, d = x1_bf.shape
    fc = wg_ff1.shape[2]
    tm, tn, tk = _tile(s, 1024), _tile(fc, 1024), _tile(d, 2048)
    nb = fc // tn

    def epi(accs, ex, out, ids):
        p = jnp.maximum(accs[0] + ex[0][...], 0.0)
        out[0][...] = (p * p).astype(BF16)
        out[1][...] = (2.0 * p).astype(BF16)

    o = (_sds((s, N_CHIP * fc), BF16), pl.BlockSpec((tm, tn), lambda i, j, k: (i, j)))
    return _out(_matmul(
        "fwd_ff1", (s // tm, N_CHIP * nb, d // tk),
        [(x1_bf, pl.BlockSpec((tm, tk), lambda i, j, k: (i, k)),
          wg_ff1, pl.BlockSpec((None, tk, tn), lambda i, j, k: (j // nb, k, j % nb)), "nn")],
        [(b_ff1, pl.BlockSpec((1, tn), lambda i, j, k: (0, j)))], [o, o], (tm, tn), epi, stages), stages)


def _fwd_ff2_ln2_loss(hdn, w_ff2, xhat1, g1, b1, b_ff2, g2, b2, target):
    s, f = hdn.shape
    d = xhat1.shape[1]

    def epi(accs, ex, out, ids):
        first = ids[0] == 0
        x1 = ex[0][...] * ex[1][...] + ex[2][...]
        r = DN_ALPHA * x1 + accs[0] + ex[3][...]
        xhat, rstd = _layer_norm(r)
        g2v = ex[4][...]
        err = xhat * g2v + ex[5][...] - ex[6][...]
        part = 0.5 * jnp.sum(jnp.mean(err * err, axis=-1, keepdims=True), axis=0, keepdims=True)
        dy = err * (1.0 / d)
        dr2 = _layer_norm_bwd(dy * g2v, xhat, rstd)
        out[0][...] = dr2
        out[1][...] = dr2.astype(BF16)
        _accum(out[2], _colsum(dy * xhat), first)
        _accum(out[3], _colsum(dy), first)
        _accum(out[4], _colsum(dr2), first)
        _accum(out[5], jnp.broadcast_to(part, (1, 128)), first)

    ff = _plain_matmul("fwd_ff2", hdn, w_ff2, "nn")
    vec = lambda n: (_sds((1, n), F32), "vec")
    return _rows_call(
        "ln2_loss", s, epi,
        [(ff, "tile"), (xhat1, "tile"), (g1, "vec"), (b1, "vec"), (b_ff2, "vec"), (g2, "vec"), (b2, "vec"), (target, "tile")],
        [(_sds((s, d), F32), "tile"), (_sds((s, d), BF16), "tile"), vec(d), vec(d), vec(d), vec(128)])


def _bwd_ff2_in(dr2_bf, w_ff2, hdn, stages=()):
    s, d = dr2_bf.shape
    f = hdn.shape[1]
    tm, tn, tk = _tile(s, 1024), _tile(f, 1024), _tile(d, 2048)

    def epi(accs, ex, out, ids):
        dpre = accs[0] * ex[0][...].astype(F32)
        out[0][...] = dpre.astype(BF16)
        _accum(out[1], _colsum(dpre), ids[1] == 0)

    return _out(_matmul(
        "bwd_ff2_in", (f // tn, s // tm, d // tk),
        [(dr2_bf, pl.BlockSpec((tm, tk), lambda j, i, k: (i, k)), w_ff2, pl.BlockSpec((tn, tk), lambda j, i, k: (j, k)), "nt")],
        [(hdn, pl.BlockSpec((tm, tn), lambda j, i, k: (i, j)))],
        [(_sds((s, f), BF16), pl.BlockSpec((tm, tn), lambda j, i, k: (i, j))), (_sds((1, f), F32), pl.BlockSpec((1, tn), lambda j, i, k: (0, j)))],
        (tm, tn), epi, stages), stages)


def _bwd_ff1_in_ln1(dpre, wg_ff1, dr2, xhat1, rstd1, g1, stages=()):
    s, f = dpre.shape
    d = xhat1.shape[1]
    fc = wg_ff1.shape[2]
    tm, tn, tk = _tile(s, 1024), _tile(d, 1024), _tile(fc, 2048)
    nb = fc // tk

    def epi(accs, ex, out, ids):
        first = ids[0] == 0
        xhat = ex[1][...]
        dx1 = accs[0] + DN_ALPHA * ex[0][...]
        dr1 = _layer_norm_bwd(dx1 * ex[3][...], xhat, ex[2][...])
        out[0][...] = dr1
        out[1][...] = dr1.astype(BF16)
        _accum(out[2], _colsum(dx1 * xhat), first)
        _accum(out[3], _colsum(dx1), first)
        _accum(out[4], _colsum(dr1), first)

    def plain(accs, ex, out, ids):
        out[0][...] = accs[0]

    o_spec = pl.BlockSpec((tm, tn), lambda i, j, k: (i, j))
    mm = _out(_matmul(
        "bwd_ff1_in", (s // tm, d // tn, f // tk),
        [(dpre, pl.BlockSpec((tm, tk), lambda i, j, k: (i, k)),
          wg_ff1, pl.BlockSpec((None, tn, tk), lambda i, j, k: (k // nb, j, k % nb)), "nt")],
        [], [(_sds((s, d), F32), o_spec)], (tm, tn), plain, stages), stages, True)
    mm, stage_res = mm if stages else (mm, None)
    vec = (_sds((1, d), F32), "vec")
    rows = _rows_call(
        "ln1_bwd", s, epi, [(mm, "tile"), (dr2, "tile"), (xhat1, "tile"), (rstd1, "tile"), (g1, "vec")],
        [(_sds((s, d), F32), "tile"), (_sds((s, d), BF16), "tile"), vec, vec, vec])
    return (rows, stage_res) if stages else rows


def _bwd_out_in(dr1_bf, w_out, z, pa, pb, stages=()):
    s, d = dr1_bf.shape
    tm, tn, tk = _tile(s, 1024), _tile(d, 1024), _tile(d, 2048)
    ga0, gb0 = 3 * d // tn, 4 * d // tn

    def epi(accs, ex, out, ids):
        dm = accs[0]
        sa = _sigmoid(ex[0][...].astype(F32))
        sb = _sigmoid(ex[1][...].astype(F32))
        out[0][...] = (dm * sa).astype(BF16)
        out[1][...] = (dm * sb).astype(BF16)
        out[2][...] = (dm * ex[2][...].astype(F32) * sa * (1.0 - sa)).astype(BF16)
        out[3][...] = (dm * ex[3][...].astype(F32) * sb * (1.0 - sb)).astype(BF16)

    o_spec = pl.BlockSpec((tm, tn), lambda i, j, k: (i, j))
    return _out(_matmul(
        "bwd_out_in", (s // tm, d // tn, d // tk),
        [(dr1_bf, pl.BlockSpec((tm, tk), lambda i, j, k: (i, k)), w_out, pl.BlockSpec((tn, tk), lambda i, j, k: (j, k)), "nt")],
        [(z, pl.BlockSpec((tm, tn), lambda i, j, k: (i, ga0 + j))), (z, pl.BlockSpec((tm, tn), lambda i, j, k: (i, gb0 + j))),
         (pa, o_spec), (pb, o_spec)],
        [(_sds((s, d), BF16), o_spec)] * 4, (tm, tn), epi, stages), stages)


def _bwd_up_in(name, dp, w_up, stages=()):
    s, d = dp.shape
    n = w_up.shape[0]
    tm, tn, tk = _tile(s, 1024), _tile(n, 1024), _tile(d, 2048)

    def epi(accs, ex, out, ids):
        out[0][...] = accs[0].astype(BF16)

    return _out(_matmul(
        name, (s // tm, n // tn, d // tk),
        [(dp, pl.BlockSpec((tm, tk), lambda i, j, k: (i, k)), w_up, pl.BlockSpec((tn, tk), lambda i, j, k: (j, k)), "nt")],
        [], [(_sds((s, n), BF16), pl.BlockSpec((tm, tn), lambda i, j, k: (i, j)))], (tm, tn), epi, stages), stages, True)


def _bwd_in(dz, wg_in, dr1, stages=()):
    s, d = dr1.shape
    inc = wg_in.shape[2]
    tm, tn, tk = _tile(s, 1024), _tile(d, 1024), _tile(inc, 2560)
    nb = inc // tk

    def epi(accs, ex, out, ids):
        out[0][...] = accs[0] + DN_ALPHA * ex[0][...]

    o_spec = pl.BlockSpec((tm, tn), lambda i, j, k: (i, j))
    return _out(_matmul(
        "bwd_in", (s // tm, d // tn, N_CHIP * nb),
        [(dz, pl.BlockSpec((tm, tk), lambda i, j, k: (i, k)),
          wg_in, pl.BlockSpec((None, tn, tk), lambda i, j, k: (k // nb, j, k % nb)), "nt")],
        [(dr1, o_spec)], [(_sds((s, d), F32), o_spec)], (tm, tn), epi, stages), stages, True)


def _wgrad(name, a, b, col_sharded, stages=()):
    s, ka = a.shape
    n = b.shape[1]
    tm, tk = _tile(ka, 1024), _tile(s, 2048)
    tn = _tile(n // N_CHIP, 1280) if col_sharded else _tile(n, 1024)

    def epi(accs, ex, out, ids):
        out[0][...] = accs[0].astype(BF16)

    if col_sharded:
        nb = (n // N_CHIP) // tn
        o = (_sds((N_CHIP, ka, n // N_CHIP), BF16), pl.BlockSpec((None, tm, tn), lambda i, j, k: (j // nb, i, j % nb)))
    else:
        o = (_sds((ka, n), BF16), pl.BlockSpec((tm, tn), lambda i, j, k: (i, j)))
    res = _out(_matmul(
        name, (ka // tm, n // tn, s // tk),
        [(a, pl.BlockSpec((tk, tm), lambda i, j, k: (k, i)), b, pl.BlockSpec((tk, tn), lambda i, j, k: (k, j)), "tn")],
        [], [o], (tm, tn), epi, stages), stages, True)
    res, stage_res = res if stages else (res, None)
    res = res if col_sharded else res.reshape(N_CHIP, ka // N_CHIP, n)
    return (res, stage_res) if stages else res


def _wgrad_rows_half(name, a, b, half, stages):
    s, ka = a.shape
    n = b.shape[1]
    kh = ka // 2
    tm, tk, tn = _tile(kh, 1024), _tile(s, 2048), _tile(n // N_CHIP, 1280)
    nb, ni = (n // N_CHIP) // tn, kh // tm

    def epi(accs, ex, out, ids):
        out[0][...] = accs[0].astype(BF16)

    return _out(_matmul(
        name, (ni, n // tn, s // tk),
        [(a, pl.BlockSpec((tk, tm), lambda i, j, k, sel: (k, sel[0] * ni + i)),
          b, pl.BlockSpec((tk, tn), lambda i, j, k, sel: (k, j)), "tn")],
        [], [(_sds((N_CHIP, kh, n // N_CHIP), BF16), pl.BlockSpec((None, tm, tn), lambda i, j, k, sel: (j // nb, i, j % nb)))],
        (tm, tn), epi, stages, _scalar(half)), stages, True)


def _chunk(s):
    return _tile(s, 512, SUBLANES)


def _zero_pads(ref, s):
    zeros = jnp.zeros((PAD, ref.shape[1]), F32)
    ref[pl.ds(0, PAD), :] = zeros
    ref[pl.ds(PAD + s, PAD), :] = zeros


def _window(ref, t0, t):
    return ref[pl.ds(t0, t + 2 * PAD), :]


def _shift(sup, off, t):
    return sup[PAD + off:PAD + off + t, :]


def _pool_count(t0, t, s, w):
    pos = t0 + lax.broadcasted_iota(jnp.int32, (t, 1), 0)
    return (jnp.minimum(pos + w // 2, s) - jnp.maximum(pos - w // 2, 0)).astype(F32)


def _pool_fwd(z, pool_w, pool_scale, stages=()):
    s = z.shape[0]
    n_g, pg = pool_w.shape[0], pool_w.shape[1]
    assert n_g == len(POOL_WINDOWS) and max(POOL_WINDOWS) // 2 <= PAD
    t = _chunk(s)

    def body(u_ref, w_ref, sc_ref, d_ref, y_ref, pad_ref):
        g = pl.program_id(0)
        _zero_pads(pad_ref, s)
        pad_ref[pl.ds(PAD, s), :] = u_ref[...].astype(F32)
        for gi, w in enumerate(POOL_WINDOWS):
            @pl.when(g == gi)
            def _():
                def step(ch, carry):
                    t0 = pl.multiple_of(ch * t, t)
                    sup = _window(pad_ref, t0, t)
                    acc = _shift(sup, -(w // 2), t)
                    for o in range(-(w // 2) + 1, w // 2):
                        acc = acc + _shift(sup, o, t)
                    dd = (acc * (1.0 / _pool_count(t0, t, s, w)) - _shift(sup, 0, t)).astype(BF16)
                    d_ref[pl.ds(t0, t), :] = dd
                    y = jnp.dot(dd, w_ref[...], preferred_element_type=F32) * sc_ref[...]
                    y_ref[pl.ds(t0, t), :] = y.astype(BF16)
                    return carry

                lax.fori_loop(0, s // t, step, 0)

    blk = pl.BlockSpec((s, pg), lambda g: (0, g))
    res = _host_call(
        "pool_fwd", (n_g,), lambda ins, outs, scr: body(*ins, *outs, *scr), [z, pool_w, pool_scale],
        [blk, pl.BlockSpec((None, pg, pg), lambda g: (g, 0, 0)), pl.BlockSpec((1, pg), lambda g: (0, g))],
        [_sds((s, n_g * pg), BF16)] * 2, [blk, blk], [pltpu.VMEM((s + 2 * PAD, pg), F32)], list(stages))
    return res if stages else res[0]


def _pool_bwd(dsv, dy, pool_w, pool_scale, stages=()):
    s = dsv.shape[0]
    n_g, pg = pool_w.shape[0], pool_w.shape[1]
    t = _chunk(s)

    def body(d_ref, dy_ref, w_ref, sc_ref, du_ref, dw_ref, dsc_ref, epad_ref, dwacc_ref):
        g = pl.program_id(0)
        _zero_pads(epad_ref, s)
        dwacc_ref[...] = jnp.zeros_like(dwacc_ref)
        for gi, w in enumerate(POOL_WINDOWS):
            @pl.when(g == gi)
            def _():
                def first(ch, dsc):
                    t0 = pl.multiple_of(ch * t, t)
                    dd = d_ref[pl.ds(t0, t), :]
                    dyc = dy_ref[pl.ds(t0, t), :].astype(F32)
                    wv = w_ref[...]
                    ypre = jnp.dot(dd, wv, preferred_element_type=F32)
                    dq = (dyc * sc_ref[...]).astype(BF16)
                    dwacc_ref[...] += lax.dot_general(dd, dq, _DIMS["tn"], preferred_element_type=F32)
                    ddv = lax.dot_general(dq, wv, _DIMS["nt"], preferred_element_type=F32)
                    epad_ref[pl.ds(pl.multiple_of(PAD + t0, SUBLANES), t), :] = ddv * (1.0 / _pool_count(t0, t, s, w))
                    return dsc + _colsum(dyc * ypre)

                dsc_ref[...] = lax.fori_loop(0, s // t, first, jnp.zeros((1, pg), F32))

                def second(ch, carry):
                    t0 = pl.multiple_of(ch * t, t)
                    sup = _window(epad_ref, t0, t)
                    acc = _shift(sup, -(w // 2) + 1, t)
                    for o in range(-(w // 2) + 2, w // 2 + 1):
                        acc = acc + _shift(sup, o, t)
                    du_ref[pl.ds(t0, t), :] = (acc - _shift(sup, 0, t) * _pool_count(t0, t, s, w)).astype(BF16)
                    return carry

                lax.fori_loop(0, s // t, second, 0)

        dw_ref[...] = dwacc_ref[...].astype(BF16)

    blk = pl.BlockSpec((s, pg), lambda g: (0, g))
    w_spec = pl.BlockSpec((None, pg, pg), lambda g: (g, 0, 0))
    sc_spec = pl.BlockSpec((1, pg), lambda g: (0, g))
    res = _host_call(
        "pool_bwd", (n_g,), lambda ins, outs, scr: body(*ins, *outs, *scr), [dsv, dy, pool_w, pool_scale],
        [blk, blk, w_spec, sc_spec], [_sds((s, n_g * pg), BF16), _sds((n_g, pg, pg), BF16), _sds((1, n_g * pg), F32)],
        [blk, w_spec, sc_spec], [pltpu.VMEM((s + 2 * PAD, pg), F32), pltpu.VMEM((pg, pg), F32)], list(stages))
    return res if stages else res[0]


def _sigmoid(x):
    return 0.5 * jnp.tanh(0.5 * x) + 0.5


def _softplus(x):
    e = jnp.exp(-jnp.abs(x))
    log1p_e = jnp.where(e < 1e-2, e * (1.0 - e * (0.5 - e * (1.0 / 3.0))), jnp.log(1.0 + e))
    return jnp.maximum(x, 0.0) + log1p_e


_GELU_C = math.sqrt(2.0 / math.pi)


def _gelu(x):
    th = jnp.tanh(_GELU_C * (x + 0.044715 * x * x * x))
    return 0.5 * x * (1.0 + th), th


def _gelu_grad(x, th):
    return 0.5 * (1.0 + th) + 0.5 * x * (1.0 - th * th) * _GELU_C * (1.0 + 3.0 * 0.044715 * x * x)


def _scan_chunk(a_ref, b_ref, o_ref, o_off, carry, t, reverse):
    n = a_ref.shape[1]
    row = lax.broadcasted_iota(jnp.int32, (SUBLANES, n), 0)
    n_groups = t // SUBLANES
    unroll = math.gcd(n_groups, SCAN_UNROLL)
    last = 0 if reverse else SUBLANES - 1

    def step(si, carry):
        for u in range(unroll):
            gi = si * unroll + u
            g = n_groups - 1 - gi if reverse else gi
            r0 = pl.multiple_of(g * SUBLANES, SUBLANES)
            a = a_ref[pl.ds(r0, SUBLANES), :]
            b = b_ref[pl.ds(r0, SUBLANES), :]
            for k in (1, 2, 4):
                keep = row < SUBLANES - k if reverse else row >= k
                sh = SUBLANES - k if reverse else k
                ar = jnp.where(keep, pltpu.roll(a, sh, 0), 1.0)
                br = jnp.where(keep, pltpu.roll(b, sh, 0), 0.0)
                b = a * br + b
                a = a * ar
            o_ref[pl.ds(pl.multiple_of(o_off + r0, SUBLANES), SUBLANES), :] = a * carry + b
            carry = (jnp.broadcast_to(a[last:last + 1, :], a.shape) * carry
                     + jnp.broadcast_to(b[last:last + 1, :], b.shape))
        return carry

    return lax.fori_loop(0, n_groups // unroll, step, carry)


def _lru_params(pk_ref):
    rows = pk_ref[...]
    get = lambda i: rows[i:i + 1, :]
    cw = [get(k) for k in range(4)]
    lam = (get(9), get(10))
    big_l = tuple(-LRU_C * _softplus(-v) for v in lam)
    return cw, get(4), (get(5), get(6)), (get(7), get(8)), lam, big_l


def _conv(sup, cw, cb, t):
    xc = cb + cw[0] * _shift(sup, -2, t)
    for k in range(1, 4):
        xc = xc + cw[k] * _shift(sup, k - 2, t)
    return xc


def _gates(xcb, w_ref, d, bk, ba, bx, big_l):
    pre = jnp.dot(xcb, w_ref[:, pl.ds(d * 2 * bk, 2 * bk)], preferred_element_type=F32)
    r = _sigmoid(pre[:, :bk] + ba[d])
    i = _sigmoid(pre[:, bk:] + bx[d])
    la = big_l[d] * r
    a = jnp.exp(la)
    var = jnp.tanh(-la) * (1.0 + a * a)
    rs = lax.rsqrt(jnp.maximum(var, 1e-30))
    return r, i, a, var * rs, rs


def _lru_specs(s, d, bk):
    u_spec = pl.BlockSpec((s, bk), lambda h: (0, d // bk + h))
    ug_spec = pl.BlockSpec((s, bk), lambda h: (0, 2 * d // bk + h))
    w_spec = pl.BlockSpec((None, bk, 4 * bk), lambda h: (h, 0, 0))
    pk_spec = pl.BlockSpec((None, 16, bk), lambda h: (h, 0, 0))
    blk = pl.BlockSpec((s, bk), lambda h: (0, h))
    return u_spec, ug_spec, w_spec, pk_spec, blk


def _lru_fwd(z, gatew, pk, stages=()):
    s = z.shape[0]
    n_h, bk = gatew.shape[0], gatew.shape[1]
    d = n_h * bk
    t = _chunk(s)
    n_ch = s // t

    def body(u_ref, ug_ref, w_ref, pk_ref, y_ref, upad, h0buf, abuf, bbuf, xcbuf, h1buf):
        _zero_pads(upad, s)
        upad[pl.ds(PAD, s), :] = u_ref[...].astype(F32)
        cw, cb, ba, bx, _, big_l = _lru_params(pk_ref)
        zero = jnp.zeros((SUBLANES, bk), F32)

        def fill(xc, dr):
            _, i, a, sq, _ = _gates(xc.astype(BF16), w_ref, dr, bk, ba, bx, big_l)
            abuf[...] = a
            bbuf[...] = sq * i * xc

        def up(ch, carry):
            t0 = pl.multiple_of(ch * t, t)
            xc = _conv(_window(upad, t0, t), cw, cb, t)
            xcbuf[pl.ds(t0, t), :] = xc
            fill(xc, 0)
            return _scan_chunk(abuf, bbuf, h0buf, t0, carry, t, False)

        lax.fori_loop(0, n_ch, up, zero)

        def down(ci, carry):
            t0 = pl.multiple_of((n_ch - 1 - ci) * t, t)
            fill(xcbuf[pl.ds(t0, t), :], 1)
            carry = _scan_chunk(abuf, bbuf, h1buf, 0, carry, t, True)
            gl, _ = _gelu(ug_ref[pl.ds(t0, t), :].astype(F32))
            y_ref[pl.ds(t0, t), :] = ((h0buf[pl.ds(t0, t), :] + h1buf[...]) * gl).astype(BF16)
            return carry

        lax.fori_loop(0, n_ch, down, zero)

    u_spec, ug_spec, w_spec, pk_spec, blk = _lru_specs(s, d, bk)
    res = _host_call(
        "lru_fwd", (n_h,), lambda ins, outs, scr: body(*ins, *outs, *scr), [z, z, gatew, pk],
        [u_spec, ug_spec, w_spec, pk_spec], [_sds((s, d), BF16)], [blk],
        [pltpu.VMEM((s + 2 * PAD, bk), F32), pltpu.VMEM((s, bk), F32), pltpu.VMEM((t, bk), F32), pltpu.VMEM((t, bk), F32),
         pltpu.VMEM((s, bk), F32), pltpu.VMEM((t, bk), F32)], list(stages))
    return (res[0][0], res[1]) if stages else res[0][0]


def _lru_grads(lam_, hnb, a, sq, rs, r, i, xc, xcb, w_ref, dwacc, d, big_l, acc):
    bk = xc.shape[1]
    dba, dbx, dl = acc
    q = lam_ * i * xc
    dla = lam_ * hnb * a - q * (a * a) * rs
    dpr = dla * big_l * r * (1.0 - r)
    dpi = q * sq * (1.0 - i)
    dprb, dpib = dpr.astype(BF16), dpi.astype(BF16)
    c0 = d * 2 * bk
    dxc = (lam_ * sq * i
           + lax.dot_general(dprb, w_ref[:, pl.ds(c0, bk)], _DIMS["nt"], preferred_element_type=F32)
           + lax.dot_general(dpib, w_ref[:, pl.ds(c0 + bk, bk)], _DIMS["nt"], preferred_element_type=F32))
    dwacc[:, pl.ds(c0, bk)] += lax.dot_general(xcb, dprb, _DIMS["tn"], preferred_element_type=F32)
    dwacc[:, pl.ds(c0 + bk, bk)] += lax.dot_general(xcb, dpib, _DIMS["tn"], preferred_element_type=F32)
    return dxc, (dba + _colsum(dpr), dbx + _colsum(dpi), dl + _colsum(dla * r))


def _lru_bwd(z, dy, gatew, pk, stages=()):
    s = z.shape[0]
    n_h, bk = gatew.shape[0], gatew.shape[1]
    d = n_h * bk
    t = _chunk(s)
    n_ch = s // t

    def body(u_ref, ug_ref, dy_ref, w_ref, pk_ref, du_ref, dug_ref, dw_ref, dpk_ref,
             upad, h0pad, h1pad, dxpad, abuf, bbuf, lbuf, dwacc, edge, xcbuf):
        for ref in (upad, h0pad, h1pad, dxpad):
            _zero_pads(ref, s)
        upad[pl.ds(PAD, s), :] = u_ref[...].astype(F32)
        dwacc[...] = jnp.zeros_like(dwacc)
        cw, cb, ba, bx, lam, big_l = _lru_params(pk_ref)
        zero = jnp.zeros((SUBLANES, bk), F32)
        zrow = jnp.zeros((1, bk), F32)
        rowi = lax.broadcasted_iota(jnp.int32, (t, bk), 0)

        def at(t0):
            return pl.ds(pl.multiple_of(PAD + t0, SUBLANES), t)

        def conv_in(t0):
            xc = xcbuf[pl.ds(t0, t), :]
            return xc, xc.astype(BF16)

        def dh_of(t0):
            ug = ug_ref[pl.ds(t0, t), :].astype(F32)
            gl, th = _gelu(ug)
            dyv = dy_ref[pl.ds(t0, t), :].astype(F32)
            return dyv * gl, dyv * _gelu_grad(ug, th)

        def sweep1(ch, carry):
            t0 = pl.multiple_of(ch * t, t)
            xc = _conv(_window(upad, t0, t), cw, cb, t)
            xcbuf[pl.ds(t0, t), :] = xc
            _, i, a, sq, _ = _gates(xc.astype(BF16), w_ref, 0, bk, ba, bx, big_l)
            abuf[...] = a
            bbuf[...] = sq * i * xc
            return _scan_chunk(abuf, bbuf, h0pad, PAD + t0, carry, t, False)

        lax.fori_loop(0, n_ch, sweep1, zero)

        edge[...] = zero

        def sweep2(ci, st):
            carry_h, carry_l, acc = st
            t0 = pl.multiple_of((n_ch - 1 - ci) * t, t)
            xc, xcb = conv_in(t0)
            _, i1, a1, sq1, _ = _gates(xcb, w_ref, 1, bk, ba, bx, big_l)
            abuf[...] = a1
            bbuf[...] = sq1 * i1 * xc
            carry_h = _scan_chunk(abuf, bbuf, h1pad, PAD + t0, carry_h, t, True)
            dh, dgl = dh_of(t0)
            dug_ref[pl.ds(t0, t), :] = (dgl * (h0pad[at(t0), :] + h1pad[at(t0), :])).astype(BF16)
            r0, i0, a0, sq0, rs0 = _gates(xcb, w_ref, 0, bk, ba, bx, big_l)
            abuf[...] = jnp.where(rowi == t - 1, edge[0:1, :], pltpu.roll(a0, t - 1, 0))
            bbuf[...] = dh
            carry_l = _scan_chunk(abuf, bbuf, lbuf, 0, carry_l, t, True)
            edge[...] = jnp.broadcast_to(a0[0:1, :], (SUBLANES, bk))
            hprev = _shift(_window(h0pad, t0, t), -1, t)
            dxc, acc = _lru_grads(lbuf[...], hprev, a0, sq0, rs0, r0, i0, xc, xcb, w_ref, dwacc, 0, big_l[0], acc)
            dxpad[at(t0), :] = dxc
            return carry_h, carry_l, acc

        _, _, acc0 = lax.fori_loop(0, n_ch, sweep2, (zero, zero, (zrow, zrow, zrow)))

        edge[...] = zero

        def sweep3(ch, st):
            carry_l, acc = st
            t0 = pl.multiple_of(ch * t, t)
            xc, xcb = conv_in(t0)
            r1, i1, a1, sq1, rs1 = _gates(xcb, w_ref, 1, bk, ba, bx, big_l)
            dh, _ = dh_of(t0)
            abuf[...] = jnp.where(rowi == 0, edge[0:1, :], pltpu.roll(a1, 1, 0))
            bbuf[...] = dh
            carry_l = _scan_chunk(abuf, bbuf, lbuf, 0, carry_l, t, False)
            edge[...] = jnp.broadcast_to(a1[t - 1:t, :], (SUBLANES, bk))
            hnext = _shift(_window(h1pad, t0, t), 1, t)
            dxc, acc = _lru_grads(lbuf[...], hnext, a1, sq1, rs1, r1, i1, xc, xcb, w_ref, dwacc, 1, big_l[1], acc)
            dxpad[at(t0), :] += dxc
            return carry_l, acc

        _, acc1 = lax.fori_loop(0, n_ch, sweep3, (zero, (zrow, zrow, zrow)))

        def sweep4(ch, st):
            t0 = pl.multiple_of(ch * t, t)
            sdx = _window(dxpad, t0, t)
            su = _window(upad, t0, t)
            dxc = _shift(sdx, 0, t)
            du = cw[0] * _shift(sdx, 2, t) + cw[1] * _shift(sdx, 1, t) + cw[2] * dxc + cw[3] * _shift(sdx, -1, t)
            du_ref[pl.ds(t0, t), :] = du.astype(BF16)
            return tuple(st[k] + _colsum(dxc * _shift(su, k - 2, t)) for k in range(4)) + (st[4] + _colsum(dxc),)

        conv_g = lax.fori_loop(0, n_ch, sweep4, (zrow,) * 5)

        dpk_ref[...] = jnp.zeros_like(dpk_ref)
        rows = list(conv_g) + [acc0[0], acc1[0], acc0[1], acc1[1],
                               acc0[2] * LRU_C * _sigmoid(-lam[0]), acc1[2] * LRU_C * _sigmoid(-lam[1])]
        for k, v in enumerate(rows):
            dpk_ref[pl.ds(k, 1), :] = v
        dw_ref[...] = dwacc[...].astype(BF16)

    u_spec, ug_spec, w_spec, pk_spec, blk = _lru_specs(s, d, bk)
    padded = pltpu.VMEM((s + 2 * PAD, bk), F32)
    chunk = pltpu.VMEM((t, bk), F32)
    res = _host_call(
        "lru_bwd", (n_h,), lambda ins, outs, scr: body(*ins, *outs, *scr), [z, z, dy, gatew, pk],
        [u_spec, ug_spec, blk, w_spec, pk_spec],
        [_sds((s, d), BF16), _sds((s, d), BF16), _sds((n_h, bk, 4 * bk), BF16), _sds((n_h, 16, bk), F32)],
        [blk, blk, w_spec, pk_spec],
        [padded, padded, padded, padded, chunk, chunk, chunk, pltpu.VMEM((bk, 4 * bk), F32),
         pltpu.VMEM((SUBLANES, bk), F32), pltpu.VMEM((s, bk), F32)], list(stages))
    return res if stages else res[0]


def _scalar(v):
    return jnp.reshape(v, (1,)).astype(jnp.int32)


def _add_sibling(g, r, c):
    _, rows, cols = g.shape
    rh = rows // 2
    tr = _tile(rh, 512, 16)
    nr = rh // tr

    def body(c_ref, g_ref, r_ref, o_ref):
        o_ref[...] = (g_ref[...].astype(F32) + r_ref[...].astype(F32)).astype(BF16)

    spec = pl.BlockSpec((None, tr, cols), lambda k, i, c_ref: (k, i, 0))
    return pl.pallas_call(
        body, name="add_sibling", out_shape=_sds((N_CHIP, rh, cols), BF16),
        grid_spec=pltpu.PrefetchScalarGridSpec(
            num_scalar_prefetch=1, grid=(N_CHIP, nr),
            in_specs=[pl.BlockSpec((None, tr, cols), lambda k, i, c_ref: (k, c_ref[0] * nr + i, 0)), spec], out_specs=spec),
        compiler_params=_cparams(("arbitrary", "arbitrary")),
    )(_scalar(c), g, r)


def _add_pair(g, r):
    _, rh, cols = g.shape
    tr = _tile(rh, 512, 16)

    def body(ins, outs, scr):
        outs[0][...] = (ins[0][...].astype(F32) + ins[1][...].astype(F32)).astype(BF16)

    spec = pl.BlockSpec((None, tr, cols), lambda k, i: (k, i, 0))
    return _host_call("add_pair", (N_CHIP, rh // tr), body, [g, r], [spec, spec], [_sds(g.shape, BF16)], [spec], [], [])[0][0]


def _sum_chips(p, rcv, k_me, c):
    _, rh, cols = p.shape
    tr = _tile(rh, 512, 16)
    nr = rh // tr

    def body(kc_ref, p_ref, r_ref, o_ref):
        acc = p_ref[...].astype(F32)
        for j in range(3):
            acc = acc + r_ref[j].astype(F32)
        o_ref[...] = acc

    return pl.pallas_call(
        body, name="sum_chips", out_shape=_sds((2 * rh, cols), F32),
        grid_spec=pltpu.PrefetchScalarGridSpec(
            num_scalar_prefetch=1, grid=(nr,),
            in_specs=[pl.BlockSpec((None, tr, cols), lambda i, kc_ref: (kc_ref[0], i, 0)),
                      pl.BlockSpec((3, tr, cols), lambda i, kc_ref: (0, i, 0))],
            out_specs=pl.BlockSpec((tr, cols), lambda i, kc_ref: (kc_ref[1] * nr + i, 0))),
        compiler_params=_cparams(("arbitrary",)),
    )(jnp.stack([k_me, c]).astype(jnp.int32), p, rcv)


def _sum_devices(g):
    def body(g_ref, o_ref):
        acc = g_ref[0]
        for dev in range(1, N_DEV):
            acc = acc + g_ref[dev]
        o_ref[...] = acc

    return pl.pallas_call(body, name="sum_devices", out_shape=_sds(g.shape[1:], F32))(g)


def _adamw(w, g, m, v):
    rows, cols = w.shape
    tr = _tile(rows, 256, SUBLANES)

    def body(ins, outs, scr):
        w_ref, g_ref, m_ref, v_ref = ins
        go_ref, d_ref, nm_ref, nv_ref = outs
        gv = g_ref[...]
        go_ref[...] = gv
        nm = ADAM_B1 * m_ref[...] + (1.0 - ADAM_B1) * gv
        nv = ADAM_B2 * v_ref[...] + (1.0 - ADAM_B2) * (gv * gv)
        m_hat = nm / (1.0 - ADAM_B1 ** ADAM_STEP)
        v_hat = nv / (1.0 - ADAM_B2 ** ADAM_STEP)
        d_ref[...] = -ADAM_LR * (m_hat / (jnp.sqrt(v_hat) + ADAM_EPS) + ADAM_WD * w_ref[...])
        nm_ref[...] = nm
        nv_ref[...] = nv

    spec = pl.BlockSpec((tr, cols), lambda i: (i, 0))
    return _host_call("adamw", (rows // tr,), body, [w, g, m, v], [spec] * 4, [_sds((rows, cols), F32)] * 4, [spec] * 4, [], [])[0]


def _pack(vs, unit):
    flat = jnp.concatenate([v.reshape(-1).astype(F32) for v in vs])
    pad = (-flat.shape[0]) % unit
    if pad:
        flat = jnp.concatenate([flat, jnp.zeros((pad,), F32)])
    return flat.reshape(-1, 128)


def _unpack(p, like):
    flat = p.reshape(-1)
    out, off = [], 0
    for v in like:
        n = math.prod(v.shape)
        out.append(flat[off:off + n].reshape(v.shape))
        off += n
    return out


def kernel(x, w_in, pool_w, pool_scale, conv_w, conv_b, lru_wa, lru_ba, lru_wx, lru_bx, lru_lambda, w_pool_up, w_lru_up, w_out, b_out, ln1_g, ln1_b, w_ff1, b_ff1, w_ff2, b_ff2, ln2_g, ln2_b, loss_target, m_w_in, m_pool_w, m_pool_scale, m_conv_w, m_conv_b, m_lru_wa, m_lru_ba, m_lru_wx, m_lru_bx, m_lru_lambda, m_w_pool_up, m_w_lru_up, m_w_out, m_b_out, m_ln1_g, m_ln1_b, m_w_ff1, m_b_ff1, m_w_ff2, m_b_ff2, m_ln2_g, m_ln2_b, v_w_in, v_pool_w, v_pool_scale, v_conv_w, v_conv_b, v_lru_wa, v_lru_ba, v_lru_wx, v_lru_bx, v_lru_lambda, v_w_pool_up, v_w_lru_up, v_w_out, v_b_out, v_ln1_g, v_ln1_b, v_w_ff1, v_b_ff1, v_w_ff2, v_b_ff2, v_ln2_g, v_ln2_b):
    given = dict(locals())
    wt = {n: given[n] for n in WEIGHTS}
    mom = {n: given["m_" + n] for n in WEIGHTS}
    vel = {n: given["v_" + n] for n in WEIGHTS}

    ix, iy, ic = _mesh_pos()
    k_me = 2 * ix + iy
    s, d = x.shape[1], x.shape[2]
    ds = d // N_CHIP
    n_g, pgs, pg = pool_w.shape[1], pool_w.shape[2], pool_w.shape[3]
    n_h, bks, bk = lru_wa.shape[2], lru_wa.shape[3], lru_wa.shape[4]
    f = b_ff1.shape[1]
    x2 = x[0]
    vec = lambda a: a.reshape(1, -1)

    sharded_vecs = [conv_w[0], lru_ba[0], lru_bx[0], lru_lambda[0]]
    rows_sv = jnp.concatenate(sharded_vecs + [jnp.zeros((6, ds), F32)], axis=0)
    sv = _all_gather_small(rows_sv)
    sv = sv.reshape(N_CHIP, 2, 16, ds)[:, 0].transpose(1, 0, 2).reshape(16, d)
    conv_w_f, ba_f, bx_f, lam_f = sv[0:4], sv[4:6], sv[6:8], sv[8:10]
    pk = jnp.concatenate([conv_w_f, conv_b, ba_f, bx_f, lam_f, jnp.zeros((5, d), F32)], axis=0)
    pk = pk.reshape(16, n_h, bk).transpose(1, 0, 2)

    def gate_stack(wa, wx):
        return jnp.stack([wa[0], wx[0]], axis=1)

    mats = {
        "w_in": w_in[0], "w_pool_up": w_pool_up[0], "w_lru_up": w_lru_up[0], "w_out": w_out[0],
        "w_ff1": w_ff1[0], "w_ff2": w_ff2[0],
        "pool_w": pool_w[0].reshape(n_g * pgs, pg),
        "gate_w": gate_stack(lru_wa, lru_wx).reshape(4 * n_h * bks, bk),
    }
    names = list(mats)
    placed = {n: _cast_place(mats[n], k_me) for n in names}

    def add_sibling(gs, swapped):
        return [_add_sibling(g, r, ic) for g, r in zip(gs, swapped)]

    def sum_chips(ps, received):
        return [_sum_chips(p, r, k_me, ic) for p, r in zip(ps, received)]

    def whole_gather(ts):
        return _chain([_gather_direct(ts), _together([_gather_relay(ts), _gather_d2d(ts, (0, 1))]), _gather_d2d(ts, (2,))])

    ((wg_in,),) = _run_stages("gather_first", [whole_gather([placed["w_in"]])])

    (z, x_bf), (wb_small, wb_mix, wb_ff1) = _fwd_in(x2, wg_in, stages=[
        whole_gather([placed["pool_w"], placed["gate_w"]]),
        _gather_direct([placed[n] for n in ("w_pool_up", "w_lru_up", "w_out")]), _gather_direct([placed["w_ff1"]])])
    wf_pool = wb_small[0].reshape(N_CHIP, n_g, pgs, pg).transpose(1, 0, 2, 3).reshape(n_g, pg, pg)
    wf_gate = wb_small[1].reshape(N_CHIP, 2, 2, n_h, bks, bk).transpose(3, 0, 4, 1, 2, 5).reshape(n_h, bk, 4 * bk)
    (d_pool, y_pool), (wb_mix,) = _pool_fwd(z, wf_pool, pool_scale, stages=[_gather_relay(wb_mix)])
    y_lru, (wb_ff1, wb_mix, wb_ff2) = _lru_fwd(z, wf_gate, pk, stages=[
        _gather_relay(wb_ff1), _gather_d2d(wb_mix), _gather_direct([placed["w_ff2"]])])
    wf_pu, wf_lu, wf_out = (b.reshape(d, d) for b in wb_mix)
    (m_mix, p_a, p_b), (wb_ff1, wb_ff2) = _fwd_merge(y_pool, y_lru, wf_pu, wf_lu, z, stages=[
        _gather_d2d(wb_ff1), _gather_relay(wb_ff2)])
    wg_ff1 = wb_ff1[0]
    (xhat1, x1_bf, rstd1), (wb_ff2,) = _fwd_out_ln1(m_mix, wf_out, x2, b_out, ln1_g, ln1_b, stages=[_gather_d2d(wb_ff2)])
    hdn, d_hdn = _fwd_ff1(x1_bf, wg_ff1, b_ff1)
    wf_ff2 = wb_ff2[0].reshape(f, d)
    dr2, dr2_bf, g_ln2_g, g_ln2_b, g_b_ff2, loss_part = _fwd_ff2_ln2_loss(
        hdn, wf_ff2, xhat1, ln1_g, ln1_b, b_ff2, ln2_g, ln2_b, loss_target[0])

    dpre, g_b_ff1 = _bwd_ff2_in(dr2_bf, wf_ff2, d_hdn)
    g_ff = [_wgrad("wgrad_ff1", x1_bf, dpre, True), _wgrad("wgrad_ff2", hdn, dr2_bf, False)]
    (dr1, dr1_bf, g_ln1_g, g_ln1_b, g_b_out), (swapped,) = _bwd_ff1_in_ln1(
        dpre, wg_ff1, dr2, xhat1, rstd1, ln1_g, stages=[_swap_halves(g_ff)])
    sums_ff = add_sibling(g_ff, swapped)
    dp_a, dp_b, dg_a, dg_b = _bwd_out_in(dr1_bf, wf_out, z, p_a, p_b)
    dy_pool = _bwd_up_in("bwd_pool_up_in", dp_a, wf_pu)
    dy_lru = _bwd_up_in("bwd_lru_up_in", dp_b, wf_lu)
    g_mix = [_wgrad("wgrad_pool_up", y_pool, dp_a, False), _wgrad("wgrad_lru_up", y_lru, dp_b, False),
             _wgrad("wgrad_out", m_mix, dr1_bf, False)]
    (du_pool, g_pool_w, g_pool_scale), (swapped,) = _pool_bwd(
        d_pool, dy_pool, wf_pool, pool_scale, stages=[_swap_halves(g_mix)])
    sums_mix = add_sibling(g_mix, swapped)
    (du_lru, du_gate, g_gate_w, g_pk), (recv_ff, recv_mix) = _lru_bwd(
        z, dy_lru, wf_gate, pk, stages=[_scatter_chips(sums_ff), _scatter_chips(sums_mix)])
    halves = sum_chips(sums_ff + sums_mix, recv_ff + recv_mix)
    g_small = [g_pool_w.reshape(n_g, N_CHIP, pgs, pg).transpose(1, 0, 2, 3).reshape(N_CHIP, n_g * pgs, pg),
               g_gate_w.reshape(n_h, N_CHIP, bks, 2, 2, bk).transpose(1, 3, 4, 0, 2, 5).reshape(N_CHIP, 4 * n_h * bks, bk)]
    dz = jnp.concatenate([du_pool, du_lru, du_gate, dg_a, dg_b], axis=1)
    g_other, (joined, swapped) = _wgrad_rows_half(
        "wgrad_in_other", x_bf, dz, 1 - ic, [_join_halves(halves), _swap_halves(g_small)])
    g_mat = dict(zip(["w_ff1", "w_ff2", "w_pool_up", "w_lru_up", "w_out"], joined))
    sums_small = add_sibling(g_small, swapped)
    g_mine, (from_sibling, recv_small) = _wgrad_rows_half(
        "wgrad_in_mine", x_bf, dz, ic, [_send_to_sibling([g_other]), _scatter_chips(sums_small)])
    sums_in = [_add_pair(g_mine, from_sibling[0])]

    def stacked(tree):
        return gate_stack(tree["lru_wa"], tree["lru_wx"]).reshape(4 * n_h * bks, bk)

    res = {}

    def update(n):
        if n == "gate_w":
            outs = [o.reshape(2, 2, n_h, bks, bk) for o in _adamw(stacked(wt), g_mat[n], stacked(mom), stacked(vel))]
            res["lru_wa"] = [o[:, 0][None] for o in outs]
            res["lru_wx"] = [o[:, 1][None] for o in outs]
        else:
            shp2 = mats[n].shape
            outs = _adamw(wt[n].reshape(shp2), g_mat[n], mom[n].reshape(shp2), vel[n].reshape(shp2))
            res[n] = [o.reshape(wt[n].shape) for o in outs]

    grad_x, (recv_in,) = _bwd_in(dz, wg_in, dr1, stages=[_scatter_chips(sums_in)])
    halves = sum_chips(sums_small + sums_in, recv_small + recv_in)
    (joined,) = _run_stages("join_last", [_join_halves(halves)])
    g_mat.update(zip(["pool_w", "gate_w", "w_in"], joined))
    for n in names:
        update(n)

    g_pk = g_pk.transpose(1, 0, 2).reshape(16, d)
    vec_full = {
        "pool_scale": g_pool_scale, "conv_w": g_pk[0:4], "conv_b": g_pk[4:5],
        "lru_ba": g_pk[5:7], "lru_bx": g_pk[7:9], "lru_lambda": g_pk[9:11],
        "b_out": g_b_out, "ln1_g": g_ln1_g, "ln1_b": g_ln1_b, "b_ff1": g_b_ff1, "b_ff2": g_b_ff2,
        "ln2_g": g_ln2_g, "ln2_b": g_ln2_b,
    }
    vnames = list(vec_full)
    vg = _sum_devices(_all_gather_small(_pack([vec_full[n] for n in vnames], 1024)))
    vg = dict(zip(vnames, _unpack(vg, [vec_full[n] for n in vnames])))
    for n in ("conv_w", "lru_ba", "lru_bx", "lru_lambda"):
        vg[n] = lax.dynamic_slice_in_dim(vg[n], k_me * ds, ds, axis=1)
    vg = {n: vg[n].reshape(wt[n].shape) for n in vnames}
    upd = _adamw(_pack([wt[n] for n in vnames], 1024), _pack([vg[n] for n in vnames], 1024),
                 _pack([mom[n] for n in vnames], 1024), _pack([vel[n] for n in vnames], 1024))
    upd = [_unpack(u, [wt[n] for n in vnames]) for u in upd]
    for i, n in enumerate(vnames):
        res[n] = [vg[n], upd[1][i], upd[2][i], upd[3][i]]

    loss = lax.psum(loss_part[0, 0], ("x", "y", "c"))
    return (loss, grad_x[None], *[res[n][0] for n in WEIGHTS], *[res[n][1] for n in WEIGHTS],
            *[res[n][2] for n in WEIGHTS], *[res[n][3] for n in WEIGHTS])
```
